```python
import math
import jax
import jax.numpy as jnp
from jax import lax

D_MODEL = 1024
BATCH = 8
SEQ = 16384
DEPTH = 4

CHUNK = 64
NORM_EPS = 1e-6
CONV_K = 4
D_FF = 2816
N_MOD = 9

GDN_HEADS = 4
GDN_DK = 128
GDN_DV = 128
GDN_WIDTH = GDN_HEADS * GDN_DV
GDN_CONV_CH = 2 * GDN_HEADS * GDN_DK + GDN_WIDTH

MLA_HEADS = 4
MLA_Q_RANK = 384
MLA_KV_RANK = 256
MLA_NOPE = 128
MLA_ROPE = 64
MLA_DV = 128
MLA_WIDTH = MLA_HEADS * MLA_DV
ROPE_THETA = 10000.0
Q_BLOCK = 128

MIX_WIDTH = GDN_WIDTH + MLA_WIDTH
EVEN_SPLITS = (GDN_CONV_CH, GDN_WIDTH, GDN_HEADS, GDN_HEADS, MLA_Q_RANK, MLA_KV_RANK, MLA_ROPE)
EVEN_IN = GDN_CONV_CH + GDN_WIDTH + 2 * GDN_HEADS + MLA_Q_RANK + MLA_KV_RANK + MLA_ROPE

SSD_D_INNER = 2 * D_MODEL
SSD_HEADDIM = 64
SSD_HEADS = SSD_D_INNER // SSD_HEADDIM
SSD_GROUPS = 4
SSD_STATE = 128
SSD_CONV_CH = SSD_D_INNER + 2 * SSD_GROUPS * SSD_STATE
ODD_SPLITS = (SSD_D_INNER, SSD_CONV_CH, SSD_HEADS)
ODD_IN = SSD_D_INNER + SSD_CONV_CH + SSD_HEADS

N_EVEN = (DEPTH + 1) // 2
N_ODD = DEPTH // 2

kernel_name = "hybrid_gdn_mla_mamba2_macaron_adaln"


def split_cols(t, sizes):
    out, start = [], 0
    for n in sizes:
        out.append(t[..., start:start + n])
        start += n
    return out


def rmsnorm(x, g):
    xf = x.astype(jnp.float32)
    y = xf * lax.rsqrt(jnp.mean(xf * xf, axis=-1, keepdims=True) + NORM_EPS)
    return (y * g.astype(jnp.float32)).astype(x.dtype)


def l2norm(x):
    xf = x.astype(jnp.float32)
    return xf * lax.rsqrt(jnp.sum(xf * xf, axis=-1, keepdims=True) + NORM_EPS)


def adaln_norm(x, g, shift, scale):
    return rmsnorm(x, g) * (1.0 + scale[:, None, :]) + shift[:, None, :]


def swiglu(h, w1, w3, w2):
    return (jax.nn.silu(h @ w1) * (h @ w3)) @ w2


def causal_conv(x, w, b=None):
    k_w = w.shape[0]
    s = x.shape[1]
    xp = jnp.pad(x, ((0, 0), (k_w - 1, 0), (0, 0)))
    y = sum(w[j] * xp[:, j:j + s] for j in range(k_w))
    return y if b is None else y + b


def rope_tables(positions, dim):
    half = dim // 2
    inv_freq = ROPE_THETA ** (-jnp.arange(half, dtype=jnp.float32) / half)
    ang = positions.astype(jnp.float32)[..., None] * inv_freq
    return jnp.cos(ang), jnp.sin(ang)


def apply_rope(x, cos, sin):
    half = x.shape[-1] // 2
    x1, x2 = x[..., :half], x[..., half:]
    return jnp.concatenate([x1 * cos - x2 * sin, x2 * cos + x1 * sin], axis=-1).astype(x.dtype)


def gated_delta_rule(q, k, v, beta_logit, a, a_log, dt_bias):
    b_, s, h, _ = q.shape
    nc = s // CHUNK
    f32 = jnp.float32
    q = l2norm(q) * (GDN_DK ** -0.5)
    k = l2norm(k)
    v = v.astype(f32)
    beta = jax.nn.sigmoid(beta_logit.astype(f32))
    g = -jnp.exp(a_log.astype(f32)) * jax.nn.softplus(a.astype(f32) + dt_bias.astype(f32))

    def to_chunks(t):
        return jnp.swapaxes(t.reshape(b_, nc, CHUNK, h, *t.shape[3:]), 2, 3)

    qc, kc, vc, bc = to_chunks(q), to_chunks(k), to_chunks(v), to_chunks(beta)
    gc = jnp.cumsum(to_chunks(g), axis=-1)
    lower = jnp.tril(jnp.ones((CHUNK, CHUNK), dtype=bool))
    strict = jnp.tril(jnp.ones((CHUNK, CHUNK), dtype=bool), -1)
    decay = jnp.exp(jnp.where(lower, gc[..., :, None] - gc[..., None, :], -jnp.inf))
    kb = kc * bc[..., None]
    a_strict = jnp.where(strict, jnp.einsum("bnhid,bnhjd->bnhij", kb, kc) * decay, 0.0)
    t_mat = a_strict + jnp.eye(CHUNK, dtype=f32)
    rhs = jnp.concatenate([vc * bc[..., None], kb * jnp.exp(gc)[..., None]], axis=-1)
    sol = lax.linalg.triangular_solve(t_mat, rhs, left_side=True, lower=True, unit_diagonal=True)
    u, w = sol[..., :GDN_DV], sol[..., GDN_DV:]
    attn = jnp.einsum("bnhid,bnhjd->bnhij", qc, kc) * decay
    g_last = gc[..., -1]
    k_end = kc * jnp.exp(g_last[..., None] - gc)[..., None]
    q_start = qc * jnp.exp(gc)[..., None]

    def step(state, inp):
        u_i, w_i, q_i, k_i, a_i, gl_i = inp
        v_new = u_i - jnp.einsum("bhck,bhkv->bhcv", w_i, state)
        o_i = jnp.einsum("bhck,bhkv->bhcv", q_i, state) + jnp.einsum("bhij,bhjv->bhiv", a_i, v_new)
        state = state * jnp.exp(gl_i)[..., None, None] + jnp.einsum("bhck,bhcv->bhkv", k_i, v_new)
        return state, o_i

    s0 = jnp.zeros((b_, h, GDN_DK, GDN_DV), f32)
    xs = (jnp.moveaxis(u, 1, 0), jnp.moveaxis(w, 1, 0), jnp.moveaxis(q_start, 1, 0),
          jnp.moveaxis(k_end, 1, 0), jnp.moveaxis(attn, 1, 0), jnp.moveaxis(g_last, 1, 0))
    _, o = lax.scan(step, s0, xs)
    return jnp.transpose(o, (1, 0, 3, 2, 4)).reshape(b_, s, h, GDN_DV)


def latent_attention(c_q, c_kv, k_rope, positions, q_norm_g, w_uq, kv_norm_g, w_ukv):
    b_, s, _ = c_q.shape
    q = jnp.einsum("bsr,rhd->bshd", rmsnorm(c_q, q_norm_g), w_uq)
    q_nope, q_pe = q[..., :MLA_NOPE], q[..., MLA_NOPE:]
    kv = jnp.einsum("bsr,rhd->bshd", rmsnorm(c_kv, kv_norm_g), w_ukv)
    k_nope, v = kv[..., :MLA_NOPE], kv[..., MLA_NOPE:]
    cos, sin = rope_tables(positions, MLA_ROPE)
    q_pe = apply_rope(q_pe, cos[:, :, None, :], sin[:, :, None, :])
    k_pe = apply_rope(k_rope, cos, sin)
    scale = (MLA_NOPE + MLA_ROPE) ** -0.5
    key_chunk = jnp.arange(s) // CHUNK

    def query_block(i):
        start = i * Q_BLOCK
        qn = lax.dynamic_slice_in_dim(q_nope, start, Q_BLOCK, axis=1)
        qp = lax.dynamic_slice_in_dim(q_pe, start, Q_BLOCK, axis=1)
        sc = jnp.einsum("bqhd,bkhd->bhqk", qn, k_nope) + jnp.einsum("bqhd,bkd->bhqk", qp, k_pe)
        sc = sc.astype(jnp.float32) * scale
        q_chunk = (start + jnp.arange(Q_BLOCK)) // CHUNK
        mask = key_chunk[None, :] <= q_chunk[:, None]
        p = jax.nn.softmax(jnp.where(mask, sc, -jnp.inf), axis=-1).astype(v.dtype)
        return jnp.einsum("bhqk,bkhd->bqhd", p, v)

    out = lax.map(query_block, jnp.arange(s // Q_BLOCK))
    return jnp.swapaxes(out, 0, 1).reshape(b_, s, MLA_WIDTH)


def ssd_chunked(x, dt_raw, bm, cm, a_log, dt_bias, d_skip):
    b_, s, _ = x.shape
    nc = s // CHUNK
    r = SSD_HEADS // SSD_GROUPS
    f32 = jnp.float32
    xh = x.astype(f32).reshape(b_, s, SSD_HEADS, SSD_HEADDIM)
    dt = jax.nn.softplus(dt_raw.astype(f32) + dt_bias.astype(f32))
    da = dt * (-jnp.exp(a_log.astype(f32)))
    xc = (xh * dt[..., None]).reshape(b_, nc, CHUNK, SSD_GROUPS, r, SSD_HEADDIM)
    bc = bm.astype(f32).reshape(b_, nc, CHUNK, SSD_GROUPS, SSD_STATE)
    cc = cm.astype(f32).reshape(b_, nc, CHUNK, SSD_GROUPS, SSD_STATE)
    acs = jnp.cumsum(da.reshape(b_, nc, CHUNK, SSD_GROUPS, r), axis=2)
    acs = jnp.transpose(acs, (0, 1, 3, 4, 2))
    lower = jnp.tril(jnp.ones((CHUNK, CHUNK), dtype=bool))
    lmat = jnp.exp(jnp.where(lower, acs[..., :, None] - acs[..., None, :], -jnp.inf))
    cb = jnp.einsum("bclgn,bcsgn->bcgls", cc, bc)
    y_diag = jnp.einsum("bcgls,bcgrls,bcsgrp->bclgrp", cb, lmat, xc)
    d_start = jnp.exp(acs)
    d_end = jnp.exp(acs[..., -1:] - acs)
    c_decay = jnp.exp(acs[..., -1])

    def step(hs, inp):
        c_i, b_i, x_i, ds_i, de_i, cd_i = inp
        y_off = jnp.einsum("blgn,bgrpn,bgrl->blgrp", c_i, hs, ds_i)
        hs = hs * cd_i[..., None, None] + jnp.einsum("blgn,bgrl,blgrp->bgrpn", b_i, de_i, x_i)
        return hs, y_off

    h0 = jnp.zeros((b_, SSD_GROUPS, r, SSD_HEADDIM, SSD_STATE), f32)
    xs = (jnp.moveaxis(cc, 1, 0), jnp.moveaxis(bc, 1, 0), jnp.moveaxis(xc, 1, 0),
          jnp.moveaxis(d_start, 1, 0), jnp.moveaxis(d_end, 1, 0), jnp.moveaxis(c_decay, 1, 0))
    _, y_off = lax.scan(step, h0, xs)
    y = (y_diag + jnp.moveaxis(y_off, 0, 1)).reshape(b_, s, SSD_HEADS, SSD_HEADDIM)
    y = y + d_skip.astype(f32)[:, None] * xh
    return y.reshape(b_, s, SSD_D_INNER)


def even_mixer(h, positions, w_in, conv_w, a_log, dt_bias, gdn_g,
               q_norm_g, w_uq, kv_norm_g, w_ukv, w_out):
    b_, s, _ = h.shape
    qkv, z, beta_logit, a, c_q, c_kv, k_rope = split_cols(h @ w_in, EVEN_SPLITS)
    qkv = jax.nn.silu(causal_conv(qkv, conv_w))
    q, k, v = split_cols(qkv, (GDN_HEADS * GDN_DK, GDN_HEADS * GDN_DK, GDN_WIDTH))
    o_a = gated_delta_rule(q.reshape(b_, s, GDN_HEADS, GDN_DK), k.reshape(b_, s, GDN_HEADS, GDN_DK),
                           v.reshape(b_, s, GDN_HEADS, GDN_DV), beta_logit, a, a_log, dt_bias)
    o_a = rmsnorm(o_a, gdn_g) * jax.nn.silu(z.astype(jnp.float32).reshape(b_, s, GDN_HEADS, GDN_DV))
    o_b = latent_attention(c_q, c_kv, k_rope, positions, q_norm_g, w_uq, kv_norm_g, w_ukv)
    o = jnp.concatenate([o_a.reshape(b_, s, GDN_WIDTH).astype(h.dtype), o_b.astype(h.dtype)], axis=-1)
    return o @ w_out


def odd_mixer(h, w_in, conv_w, conv_b, a_log, dt_bias, d_skip, norm_g, w_out):
    b_, s, _ = h.shape
    z, xbc, dt_raw = split_cols(h @ w_in, ODD_SPLITS)
    xbc = jax.nn.silu(causal_conv(xbc, conv_w, conv_b))
    xs, bm, cm = split_cols(xbc, (SSD_D_INNER, SSD_GROUPS * SSD_STATE, SSD_GROUPS * SSD_STATE))
    y = ssd_chunked(xs, dt_raw, bm, cm, a_log, dt_bias, d_skip)
    y = y * jax.nn.silu(z.astype(jnp.float32))
    yg = y.reshape(b_, s, SSD_GROUPS, SSD_D_INNER // SSD_GROUPS)
    yg = yg * lax.rsqrt(jnp.mean(yg * yg, axis=-1, keepdims=True) + NORM_EPS)
    y = (yg.reshape(b_, s, SSD_D_INNER) * norm_g.astype(jnp.float32)).astype(h.dtype)
    return y @ w_out


def _fwd_setup_inputs(seed: int = 0) -> dict:
    key = jax.random.key(seed)
    ks = jax.random.split(key, 27)
    f32 = jnp.float32

    def nrm(k, shape, scale):
        return jax.random.normal(k, shape, f32) * scale

    def gain(k, shape):
        return 1.0 + 0.05 * jax.random.normal(k, shape, f32)

    def a_log_init(k, shape):
        return jnp.log(jax.random.uniform(k, shape, f32, 1.0, 16.0))

    def dt_bias_init(k, shape):
        dt = jnp.exp(jax.random.uniform(k, shape, f32, math.log(1e-3), math.log(1e-1)))
        return dt + jnp.log(-jnp.expm1(-dt))

    positions = jnp.broadcast_to(jnp.arange(SEQ, dtype=jnp.int32), (BATCH, SEQ))
    return {
        "x": nrm(ks[0], (BATCH, SEQ, D_MODEL), 1.0),
        "c": nrm(ks[1], (BATCH, D_MODEL), 1.0),
        "positions": positions,
        "ada_w": nrm(ks[2], (DEPTH, D_MODEL, N_MOD * D_MODEL), 0.5 * D_MODEL ** -0.5),
        "ada_b": nrm(ks[3], (DEPTH, N_MOD * D_MODEL), 0.02),
        "norm_g": gain(ks[4], (DEPTH, 3, D_MODEL)),
        "ffn_w1": nrm(ks[5], (DEPTH, 2, D_MODEL, D_FF), D_MODEL ** -0.5),
        "ffn_w3": nrm(ks[6], (DEPTH, 2, D_MODEL, D_FF), D_MODEL ** -0.5),
        "ffn_w2": nrm(ks[7], (DEPTH, 2, D_FF, D_MODEL), D_FF ** -0.5),
        "ev_w_in": nrm(ks[8], (N_EVEN, D_MODEL, EVEN_IN), D_MODEL ** -0.5),
        "gdn_conv_w": nrm(ks[9], (N_EVEN, CONV_K, GDN_CONV_CH), CONV_K ** -0.5),
        "gdn_A_log": a_log_init(ks[10], (N_EVEN, GDN_HEADS)),
        "gdn_dt_bias": dt_bias_init(ks[11], (N_EVEN, GDN_HEADS)),
        "gdn_norm_g": gain(ks[12], (N_EVEN, GDN_DV)),
        "mla_q_norm_g": gain(ks[13], (N_EVEN, MLA_Q_RANK)),
        "mla_w_uq": nrm(ks[14], (N_EVEN, MLA_Q_RANK, MLA_HEADS, MLA_NOPE + MLA_ROPE), MLA_Q_RANK ** -0.5),
        "mla_kv_norm_g": gain(ks[15], (N_EVEN, MLA_KV_RANK)),
        "mla_w_ukv": nrm(ks[16], (N_EVEN, MLA_KV_RANK, MLA_HEADS, MLA_NOPE + MLA_DV), MLA_KV_RANK ** -0.5),
        "ev_w_out": nrm(ks[17], (N_EVEN, MIX_WIDTH, D_MODEL), MIX_WIDTH ** -0.5),
        "ssd_w_in": nrm(ks[18], (N_ODD, D_MODEL, ODD_IN), D_MODEL ** -0.5),
        "ssd_conv_w": nrm(ks[19], (N_ODD, CONV_K, SSD_CONV_CH), CONV_K ** -0.5),
        "ssd_conv_b": nrm(ks[20], (N_ODD, SSD_CONV_CH), 0.02),
        "ssd_A_log": a_log_init(ks[21], (N_ODD, SSD_HEADS)),
        "ssd_dt_bias": dt_bias_init(ks[22], (N_ODD, SSD_HEADS)),
        "ssd_D": gain(ks[23], (N_ODD, SSD_HEADS)),
        "ssd_norm_g": gain(ks[24], (N_ODD, SSD_D_INNER)),
        "ssd_w_out": nrm(ks[25], (N_ODD, SSD_D_INNER, D_MODEL), SSD_D_INNER ** -0.5),
        "final_g": gain(ks[26], (D_MODEL,)),
    }


def _fwd_reference(x, c, positions, ada_w, ada_b, norm_g, ffn_w1, ffn_w3, ffn_w2,
              ev_w_in, gdn_conv_w, gdn_A_log, gdn_dt_bias, gdn_norm_g,
              mla_q_norm_g, mla_w_uq, mla_kv_norm_g, mla_w_ukv, ev_w_out,
              ssd_w_in, ssd_conv_w, ssd_conv_b, ssd_A_log, ssd_dt_bias, ssd_D, ssd_norm_g, ssd_w_out,
              final_g):
    b_ = x.shape[0]
    c_act = jax.nn.silu(c)
    for l in range(DEPTH):
        mod = (c_act @ ada_w[l] + ada_b[l]).reshape(b_, 3, 3, D_MODEL)
        shift, scale, gate = mod[:, :, 0], mod[:, :, 1], mod[:, :, 2]
        h = adaln_norm(x, norm_g[l, 0], shift[:, 0], scale[:, 0])
        x = x + 0.5 * gate[:, 0, None] * swiglu(h, ffn_w1[l, 0], ffn_w3[l, 0], ffn_w2[l, 0])
        h = adaln_norm(x, norm_g[l, 1], shift[:, 1], scale[:, 1])
        if l % 2 == 0:
            e = l // 2
            y = even_mixer(h, positions, ev_w_in[e], gdn_conv_w[e], gdn_A_log[e], gdn_dt_bias[e],
                           gdn_norm_g[e], mla_q_norm_g[e], mla_w_uq[e], mla_kv_norm_g[e],
                           mla_w_ukv[e], ev_w_out[e])
        else:
            o = l // 2
            y = odd_mixer(h, ssd_w_in[o], ssd_conv_w[o], ssd_conv_b[o], ssd_A_log[o],
                          ssd_dt_bias[o], ssd_D[o], ssd_norm_g[o], ssd_w_out[o])
        x = x + gate[:, 1, None] * y
        h = adaln_norm(x, norm_g[l, 2], shift[:, 2], scale[:, 2])
        x = x + 0.5 * gate[:, 2, None] * swiglu(h, ffn_w1[l, 1], ffn_w3[l, 1], ffn_w2[l, 1])
    return rmsnorm(x, final_g)


import jax as _jax
import jax.numpy as _jnp

TWIN_FORMAT = 'train_step'
FWD_PARAMS = ['x', 'c', 'positions', 'ada_w', 'ada_b', 'norm_g', 'ffn_w1', 'ffn_w3', 'ffn_w2', 'ev_w_in', 'gdn_conv_w', 'gdn_A_log', 'gdn_dt_bias', 'gdn_norm_g', 'mla_q_norm_g', 'mla_w_uq', 'mla_kv_norm_g', 'mla_w_ukv', 'ev_w_out', 'ssd_w_in', 'ssd_conv_w', 'ssd_conv_b', 'ssd_A_log', 'ssd_dt_bias', 'ssd_D', 'ssd_norm_g', 'ssd_w_out', 'final_g']
TWIN_WEIGHTS = ['ada_w', 'ada_b', 'norm_g', 'ffn_w1', 'ffn_w3', 'ffn_w2', 'ev_w_in', 'gdn_conv_w', 'gdn_A_log', 'gdn_dt_bias', 'gdn_norm_g', 'mla_q_norm_g', 'mla_w_uq', 'mla_kv_norm_g', 'mla_w_ukv', 'ev_w_out', 'ssd_w_in', 'ssd_conv_w', 'ssd_conv_b', 'ssd_A_log', 'ssd_dt_bias', 'ssd_D', 'ssd_norm_g', 'ssd_w_out', 'final_g']
TWIN_DIFF_INPUT = 'x'
TWIN_INPUTS = ['x', 'c', 'positions', 'ada_w', 'ada_b', 'norm_g', 'ffn_w1', 'ffn_w3', 'ffn_w2', 'ev_w_in', 'gdn_conv_w', 'gdn_A_log', 'gdn_dt_bias', 'gdn_norm_g', 'mla_q_norm_g', 'mla_w_uq', 'mla_kv_norm_g', 'mla_w_ukv', 'ev_w_out', 'ssd_w_in', 'ssd_conv_w', 'ssd_conv_b', 'ssd_A_log', 'ssd_dt_bias', 'ssd_D', 'ssd_norm_g', 'ssd_w_out', 'final_g', 'loss_target', 'm_ada_w', 'm_ada_b', 'm_norm_g', 'm_ffn_w1', 'm_ffn_w3', 'm_ffn_w2', 'm_ev_w_in', 'm_gdn_conv_w', 'm_gdn_A_log', 'm_gdn_dt_bias', 'm_gdn_norm_g', 'm_mla_q_norm_g', 'm_mla_w_uq', 'm_mla_kv_norm_g', 'm_mla_w_ukv', 'm_ev_w_out', 'm_ssd_w_in', 'm_ssd_conv_w', 'm_ssd_conv_b', 'm_ssd_A_log', 'm_ssd_dt_bias', 'm_ssd_D', 'm_ssd_norm_g', 'm_ssd_w_out', 'm_final_g', 'v_ada_w', 'v_ada_b', 'v_norm_g', 'v_ffn_w1', 'v_ffn_w3', 'v_ffn_w2', 'v_ev_w_in', 'v_gdn_conv_w', 'v_gdn_A_log', 'v_gdn_dt_bias', 'v_gdn_norm_g', 'v_mla_q_norm_g', 'v_mla_w_uq', 'v_mla_kv_norm_g', 'v_mla_w_ukv', 'v_ev_w_out', 'v_ssd_w_in', 'v_ssd_conv_w', 'v_ssd_conv_b', 'v_ssd_A_log', 'v_ssd_dt_bias', 'v_ssd_D', 'v_ssd_norm_g', 'v_ssd_w_out', 'v_final_g']
TWIN_OUTPUTS = ['loss', 'grad_x', 'grad_ada_w', 'grad_ada_b', 'grad_norm_g', 'grad_ffn_w1', 'grad_ffn_w3', 'grad_ffn_w2', 'grad_ev_w_in', 'grad_gdn_conv_w', 'grad_gdn_A_log', 'grad_gdn_dt_bias', 'grad_gdn_norm_g', 'grad_mla_q_norm_g', 'grad_mla_w_uq', 'grad_mla_kv_norm_g', 'grad_mla_w_ukv', 'grad_ev_w_out', 'grad_ssd_w_in', 'grad_ssd_conv_w', 'grad_ssd_conv_b', 'grad_ssd_A_log', 'grad_ssd_dt_bias', 'grad_ssd_D', 'grad_ssd_norm_g', 'grad_ssd_w_out', 'grad_final_g', 'delta_ada_w', 'delta_ada_b', 'delta_norm_g', 'delta_ffn_w1', 'delta_ffn_w3', 'delta_ffn_w2', 'delta_ev_w_in', 'delta_gdn_conv_w', 'delta_gdn_A_log', 'delta_gdn_dt_bias', 'delta_gdn_norm_g', 'delta_mla_q_norm_g', 'delta_mla_w_uq', 'delta_mla_kv_norm_g', 'delta_mla_w_ukv', 'delta_ev_w_out', 'delta_ssd_w_in', 'delta_ssd_conv_w', 'delta_ssd_conv_b', 'delta_ssd_A_log', 'delta_ssd_dt_bias', 'delta_ssd_D', 'delta_ssd_norm_g', 'delta_ssd_w_out', 'delta_final_g', 'new_m_ada_w', 'new_m_ada_b', 'new_m_norm_g', 'new_m_ffn_w1', 'new_m_ffn_w3', 'new_m_ffn_w2', 'new_m_ev_w_in', 'new_m_gdn_conv_w', 'new_m_gdn_A_log', 'new_m_gdn_dt_bias', 'new_m_gdn_norm_g', 'new_m_mla_q_norm_g', 'new_m_mla_w_uq', 'new_m_mla_kv_norm_g', 'new_m_mla_w_ukv', 'new_m_ev_w_out', 'new_m_ssd_w_in', 'new_m_ssd_conv_w', 'new_m_ssd_conv_b', 'new_m_ssd_A_log', 'new_m_ssd_dt_bias', 'new_m_ssd_D', 'new_m_ssd_norm_g', 'new_m_ssd_w_out', 'new_m_final_g', 'new_v_ada_w', 'new_v_ada_b', 'new_v_norm_g', 'new_v_ffn_w1', 'new_v_ffn_w3', 'new_v_ffn_w2', 'new_v_ev_w_in', 'new_v_gdn_conv_w', 'new_v_gdn_A_log', 'new_v_gdn_dt_bias', 'new_v_gdn_norm_g', 'new_v_mla_q_norm_g', 'new_v_mla_w_uq', 'new_v_mla_kv_norm_g', 'new_v_mla_w_ukv', 'new_v_ev_w_out', 'new_v_ssd_w_in', 'new_v_ssd_conv_w', 'new_v_ssd_conv_b', 'new_v_ssd_A_log', 'new_v_ssd_dt_bias', 'new_v_ssd_D', 'new_v_ssd_norm_g', 'new_v_ssd_w_out', 'new_v_final_g']
TWIN_LEAF_KINDS = {'loss': 'loss', 'grad_x': 'grad_x', 'grad_ada_w': 'grad_w', 'grad_ada_b': 'grad_w', 'grad_norm_g': 'grad_w', 'grad_ffn_w1': 'grad_w', 'grad_ffn_w3': 'grad_w', 'grad_ffn_w2': 'grad_w', 'grad_ev_w_in': 'grad_w', 'grad_gdn_conv_w': 'grad_w', 'grad_gdn_A_log': 'grad_w', 'grad_gdn_dt_bias': 'grad_w', 'grad_gdn_norm_g': 'grad_w', 'grad_mla_q_norm_g': 'grad_w', 'grad_mla_w_uq': 'grad_w', 'grad_mla_kv_norm_g': 'grad_w', 'grad_mla_w_ukv': 'grad_w', 'grad_ev_w_out': 'grad_w', 'grad_ssd_w_in': 'grad_w', 'grad_ssd_conv_w': 'grad_w', 'grad_ssd_conv_b': 'grad_w', 'grad_ssd_A_log': 'grad_w', 'grad_ssd_dt_bias': 'grad_w', 'grad_ssd_D': 'grad_w', 'grad_ssd_norm_g': 'grad_w', 'grad_ssd_w_out': 'grad_w', 'grad_final_g': 'grad_w', 'delta_ada_w': 'delta_w', 'delta_ada_b': 'delta_w', 'delta_norm_g': 'delta_w', 'delta_ffn_w1': 'delta_w', 'delta_ffn_w3': 'delta_w', 'delta_ffn_w2': 'delta_w', 'delta_ev_w_in': 'delta_w', 'delta_gdn_conv_w': 'delta_w', 'delta_gdn_A_log': 'delta_w', 'delta_gdn_dt_bias': 'delta_w', 'delta_gdn_norm_g': 'delta_w', 'delta_mla_q_norm_g': 'delta_w', 'delta_mla_w_uq': 'delta_w', 'delta_mla_kv_norm_g': 'delta_w', 'delta_mla_w_ukv': 'delta_w', 'delta_ev_w_out': 'delta_w', 'delta_ssd_w_in': 'delta_w', 'delta_ssd_conv_w': 'delta_w', 'delta_ssd_conv_b': 'delta_w', 'delta_ssd_A_log': 'delta_w', 'delta_ssd_dt_bias': 'delta_w', 'delta_ssd_D': 'delta_w', 'delta_ssd_norm_g': 'delta_w', 'delta_ssd_w_out': 'delta_w', 'delta_final_g': 'delta_w', 'new_m_ada_w': 'new_m', 'new_m_ada_b': 'new_m', 'new_m_norm_g': 'new_m', 'new_m_ffn_w1': 'new_m', 'new_m_ffn_w3': 'new_m', 'new_m_ffn_w2': 'new_m', 'new_m_ev_w_in': 'new_m', 'new_m_gdn_conv_w': 'new_m', 'new_m_gdn_A_log': 'new_m', 'new_m_gdn_dt_bias': 'new_m', 'new_m_gdn_norm_g': 'new_m', 'new_m_mla_q_norm_g': 'new_m', 'new_m_mla_w_uq': 'new_m', 'new_m_mla_kv_norm_g': 'new_m', 'new_m_mla_w_ukv': 'new_m', 'new_m_ev_w_out': 'new_m', 'new_m_ssd_w_in': 'new_m', 'new_m_ssd_conv_w': 'new_m', 'new_m_ssd_conv_b': 'new_m', 'new_m_ssd_A_log': 'new_m', 'new_m_ssd_dt_bias': 'new_m', 'new_m_ssd_D': 'new_m', 'new_m_ssd_norm_g': 'new_m', 'new_m_ssd_w_out': 'new_m', 'new_m_final_g': 'new_m', 'new_v_ada_w': 'new_v', 'new_v_ada_b': 'new_v', 'new_v_norm_g': 'new_v', 'new_v_ffn_w1': 'new_v', 'new_v_ffn_w3': 'new_v', 'new_v_ffn_w2': 'new_v', 'new_v_ev_w_in': 'new_v', 'new_v_gdn_conv_w': 'new_v', 'new_v_gdn_A_log': 'new_v', 'new_v_gdn_dt_bias': 'new_v', 'new_v_gdn_norm_g': 'new_v', 'new_v_mla_q_norm_g': 'new_v', 'new_v_mla_w_uq': 'new_v', 'new_v_mla_kv_norm_g': 'new_v', 'new_v_mla_w_ukv': 'new_v', 'new_v_ev_w_out': 'new_v', 'new_v_ssd_w_in': 'new_v', 'new_v_ssd_conv_w': 'new_v', 'new_v_ssd_conv_b': 'new_v', 'new_v_ssd_A_log': 'new_v', 'new_v_ssd_dt_bias': 'new_v', 'new_v_ssd_D': 'new_v', 'new_v_ssd_norm_g': 'new_v', 'new_v_ssd_w_out': 'new_v', 'new_v_final_g': 'new_v'}


def _forward(args):
    return _fwd_reference(*[args[k] for k in FWD_PARAMS])


def _output_shape():
    def fwd():
        inp = _fwd_setup_inputs(0)
        return _fwd_reference(*[inp[k] for k in FWD_PARAMS])
    out = _jax.eval_shape(fwd)
    return out.shape, out.dtype

N_MICROBATCH = 1
ADAM_LR = 0.001
ADAM_B1 = 0.9
ADAM_B2 = 0.999
ADAM_EPS = 1e-08
ADAM_WD = 0.01
ADAM_STEP = 10
PER_EXAMPLE_BATCH_AXIS = {'x': 0, 'c': 0, 'positions': 0, 'loss_target': 0}
SHARED_INPUTS = []
_WEIGHT_DTYPES = {'ada_w': _jnp.float32, 'ada_b': _jnp.float32, 'norm_g': _jnp.float32, 'ffn_w1': _jnp.float32, 'ffn_w3': _jnp.float32, 'ffn_w2': _jnp.float32, 'ev_w_in': _jnp.float32, 'gdn_conv_w': _jnp.float32, 'gdn_A_log': _jnp.float32, 'gdn_dt_bias': _jnp.float32, 'gdn_norm_g': _jnp.float32, 'mla_q_norm_g': _jnp.float32, 'mla_w_uq': _jnp.float32, 'mla_kv_norm_g': _jnp.float32, 'mla_w_ukv': _jnp.float32, 'ev_w_out': _jnp.float32, 'ssd_w_in': _jnp.float32, 'ssd_conv_w': _jnp.float32, 'ssd_conv_b': _jnp.float32, 'ssd_A_log': _jnp.float32, 'ssd_dt_bias': _jnp.float32, 'ssd_D': _jnp.float32, 'ssd_norm_g': _jnp.float32, 'ssd_w_out': _jnp.float32, 'final_g': _jnp.float32}
MOMENT_SCALE = {'ada_w': 1.080612e-01, 'ada_b': 2.070229e-01, 'norm_g': 8.181692e-02, 'ffn_w1': 2.425780e-02, 'ffn_w3': 2.360796e-02, 'ffn_w2': 3.913303e-02, 'ev_w_in': 5.561097e-02, 'gdn_conv_w': 5.397059e-02, 'gdn_A_log': 3.188211e-01, 'gdn_dt_bias': 3.131658e-01, 'gdn_norm_g': 1.704545e-01, 'mla_q_norm_g': 1.868212e-02, 'mla_w_uq': 1.291231e-02, 'mla_kv_norm_g': 6.062572e-02, 'mla_w_ukv': 2.816805e-02, 'ev_w_out': 5.950033e-02, 'ssd_w_in': 6.720539e-02, 'ssd_conv_w': 6.396777e-02, 'ssd_conv_b': 8.561297e-02, 'ssd_A_log': 2.752925e-01, 'ssd_dt_bias': 1.585358e-01, 'ssd_D': 3.766246e-01, 'ssd_norm_g': 8.178764e-02, 'ssd_w_out': 1.064660e-01, 'final_g': 1.282388e+02}


def _to_microbatches(a, axis):
    t = _jnp.moveaxis(a, axis, 0)
    t = t.reshape((N_MICROBATCH, t.shape[0] // N_MICROBATCH) + t.shape[1:])
    return _jnp.moveaxis(t, 1, axis + 1)


def setup_inputs(seed: int = 0) -> dict:
    inp = _fwd_setup_inputs(seed)
    key = _jax.random.fold_in(_jax.random.key(seed), 7919)
    shape, _ = _output_shape()
    out = dict(inp)
    out["loss_target"] = _jax.random.normal(_jax.random.fold_in(key, 0), shape, _jnp.float32)
    for i, name in enumerate(TWIN_WEIGHTS):
        w = inp[name].astype(_jnp.float32)
        if MOMENT_SCALE is None:
            s = _jnp.sqrt(_jnp.mean(_jnp.square(w)) + 1e-30)
        else:
            s = MOMENT_SCALE[name]
        km, kv = _jax.random.split(_jax.random.fold_in(key, i + 1))
        out[name] = w
        out["m_" + name] = s * _jax.random.normal(km, w.shape, _jnp.float32)
        out["v_" + name] = (s * s) * _jax.random.uniform(kv, w.shape, _jnp.float32, 0.5, 1.5)
    if N_MICROBATCH > 1:
        for name, axis in PER_EXAMPLE_BATCH_AXIS.items():
            out[name] = _to_microbatches(out[name], axis)
    return {'x': out['x'], 'c': out['c'], 'positions': out['positions'], 'ada_w': out['ada_w'], 'ada_b': out['ada_b'], 'norm_g': out['norm_g'], 'ffn_w1': out['ffn_w1'], 'ffn_w3': out['ffn_w3'], 'ffn_w2': out['ffn_w2'], 'ev_w_in': out['ev_w_in'], 'gdn_conv_w': out['gdn_conv_w'], 'gdn_A_log': out['gdn_A_log'], 'gdn_dt_bias': out['gdn_dt_bias'], 'gdn_norm_g': out['gdn_norm_g'], 'mla_q_norm_g': out['mla_q_norm_g'], 'mla_w_uq': out['mla_w_uq'], 'mla_kv_norm_g': out['mla_kv_norm_g'], 'mla_w_ukv': out['mla_w_ukv'], 'ev_w_out': out['ev_w_out'], 'ssd_w_in': out['ssd_w_in'], 'ssd_conv_w': out['ssd_conv_w'], 'ssd_conv_b': out['ssd_conv_b'], 'ssd_A_log': out['ssd_A_log'], 'ssd_dt_bias': out['ssd_dt_bias'], 'ssd_D': out['ssd_D'], 'ssd_norm_g': out['ssd_norm_g'], 'ssd_w_out': out['ssd_w_out'], 'final_g': out['final_g'], 'loss_target': out['loss_target'], 'm_ada_w': out['m_ada_w'], 'm_ada_b': out['m_ada_b'], 'm_norm_g': out['m_norm_g'], 'm_ffn_w1': out['m_ffn_w1'], 'm_ffn_w3': out['m_ffn_w3'], 'm_ffn_w2': out['m_ffn_w2'], 'm_ev_w_in': out['m_ev_w_in'], 'm_gdn_conv_w': out['m_gdn_conv_w'], 'm_gdn_A_log': out['m_gdn_A_log'], 'm_gdn_dt_bias': out['m_gdn_dt_bias'], 'm_gdn_norm_g': out['m_gdn_norm_g'], 'm_mla_q_norm_g': out['m_mla_q_norm_g'], 'm_mla_w_uq': out['m_mla_w_uq'], 'm_mla_kv_norm_g': out['m_mla_kv_norm_g'], 'm_mla_w_ukv': out['m_mla_w_ukv'], 'm_ev_w_out': out['m_ev_w_out'], 'm_ssd_w_in': out['m_ssd_w_in'], 'm_ssd_conv_w': out['m_ssd_conv_w'], 'm_ssd_conv_b': out['m_ssd_conv_b'], 'm_ssd_A_log': out['m_ssd_A_log'], 'm_ssd_dt_bias': out['m_ssd_dt_bias'], 'm_ssd_D': out['m_ssd_D'], 'm_ssd_norm_g': out['m_ssd_norm_g'], 'm_ssd_w_out': out['m_ssd_w_out'], 'm_final_g': out['m_final_g'], 'v_ada_w': out['v_ada_w'], 'v_ada_b': out['v_ada_b'], 'v_norm_g': out['v_norm_g'], 'v_ffn_w1': out['v_ffn_w1'], 'v_ffn_w3': out['v_ffn_w3'], 'v_ffn_w2': out['v_ffn_w2'], 'v_ev_w_in': out['v_ev_w_in'], 'v_gdn_conv_w': out['v_gdn_conv_w'], 'v_gdn_A_log': out['v_gdn_A_log'], 'v_gdn_dt_bias': out['v_gdn_dt_bias'], 'v_gdn_norm_g': out['v_gdn_norm_g'], 'v_mla_q_norm_g': out['v_mla_q_norm_g'], 'v_mla_w_uq': out['v_mla_w_uq'], 'v_mla_kv_norm_g': out['v_mla_kv_norm_g'], 'v_mla_w_ukv': out['v_mla_w_ukv'], 'v_ev_w_out': out['v_ev_w_out'], 'v_ssd_w_in': out['v_ssd_w_in'], 'v_ssd_conv_w': out['v_ssd_conv_w'], 'v_ssd_conv_b': out['v_ssd_conv_b'], 'v_ssd_A_log': out['v_ssd_A_log'], 'v_ssd_dt_bias': out['v_ssd_dt_bias'], 'v_ssd_D': out['v_ssd_D'], 'v_ssd_norm_g': out['v_ssd_norm_g'], 'v_ssd_w_out': out['v_ssd_w_out'], 'v_final_g': out['v_final_g']}


def _loss(weights, diff, rest, loss_target):
    with _jax.named_scope("forward"):
        args = {**rest, TWIN_DIFF_INPUT: diff, **{k: w.astype(_WEIGHT_DTYPES[k]) for k, w in weights.items()}}
        y = _forward(args)
    with _jax.named_scope("loss_head"):
        err = _jnp.square(y.astype(_jnp.float32) - loss_target)
        return 0.5 * _jnp.sum(_jnp.mean(err, axis=-1)) if err.ndim else 0.5 * err


def _adamw(w, g, m, v):
    m = ADAM_B1 * m + (1.0 - ADAM_B1) * g
    v = ADAM_B2 * v + (1.0 - ADAM_B2) * _jnp.square(g)
    m_hat = m / (1.0 - ADAM_B1 ** ADAM_STEP)
    v_hat = v / (1.0 - ADAM_B2 ** ADAM_STEP)
    delta = -ADAM_LR * (m_hat / (_jnp.sqrt(v_hat) + ADAM_EPS) + ADAM_WD * w)
    return delta, m, v


def reference(x, c, positions, ada_w, ada_b, norm_g, ffn_w1, ffn_w3, ffn_w2, ev_w_in, gdn_conv_w, gdn_A_log, gdn_dt_bias, gdn_norm_g, mla_q_norm_g, mla_w_uq, mla_kv_norm_g, mla_w_ukv, ev_w_out, ssd_w_in, ssd_conv_w, ssd_conv_b, ssd_A_log, ssd_dt_bias, ssd_D, ssd_norm_g, ssd_w_out, final_g, loss_target, m_ada_w, m_ada_b, m_norm_g, m_ffn_w1, m_ffn_w3, m_ffn_w2, m_ev_w_in, m_gdn_conv_w, m_gdn_A_log, m_gdn_dt_bias, m_gdn_norm_g, m_mla_q_norm_g, m_mla_w_uq, m_mla_kv_norm_g, m_mla_w_ukv, m_ev_w_out, m_ssd_w_in, m_ssd_conv_w, m_ssd_conv_b, m_ssd_A_log, m_ssd_dt_bias, m_ssd_D, m_ssd_norm_g, m_ssd_w_out, m_final_g, v_ada_w, v_ada_b, v_norm_g, v_ffn_w1, v_ffn_w3, v_ffn_w2, v_ev_w_in, v_gdn_conv_w, v_gdn_A_log, v_gdn_dt_bias, v_gdn_norm_g, v_mla_q_norm_g, v_mla_w_uq, v_mla_kv_norm_g, v_mla_w_ukv, v_ev_w_out, v_ssd_w_in, v_ssd_conv_w, v_ssd_conv_b, v_ssd_A_log, v_ssd_dt_bias, v_ssd_D, v_ssd_norm_g, v_ssd_w_out, v_final_g):
    given = dict(x=x, c=c, positions=positions, ada_w=ada_w, ada_b=ada_b, norm_g=norm_g, ffn_w1=ffn_w1, ffn_w3=ffn_w3, ffn_w2=ffn_w2, ev_w_in=ev_w_in, gdn_conv_w=gdn_conv_w, gdn_A_log=gdn_A_log, gdn_dt_bias=gdn_dt_bias, gdn_norm_g=gdn_norm_g, mla_q_norm_g=mla_q_norm_g, mla_w_uq=mla_w_uq, mla_kv_norm_g=mla_kv_norm_g, mla_w_ukv=mla_w_ukv, ev_w_out=ev_w_out, ssd_w_in=ssd_w_in, ssd_conv_w=ssd_conv_w, ssd_conv_b=ssd_conv_b, ssd_A_log=ssd_A_log, ssd_dt_bias=ssd_dt_bias, ssd_D=ssd_D, ssd_norm_g=ssd_norm_g, ssd_w_out=ssd_w_out, final_g=final_g, loss_target=loss_target, m_ada_w=m_ada_w, m_ada_b=m_ada_b, m_norm_g=m_norm_g, m_ffn_w1=m_ffn_w1, m_ffn_w3=m_ffn_w3, m_ffn_w2=m_ffn_w2, m_ev_w_in=m_ev_w_in, m_gdn_conv_w=m_gdn_conv_w, m_gdn_A_log=m_gdn_A_log, m_gdn_dt_bias=m_gdn_dt_bias, m_gdn_norm_g=m_gdn_norm_g, m_mla_q_norm_g=m_mla_q_norm_g, m_mla_w_uq=m_mla_w_uq, m_mla_kv_norm_g=m_mla_kv_norm_g, m_mla_w_ukv=m_mla_w_ukv, m_ev_w_out=m_ev_w_out, m_ssd_w_in=m_ssd_w_in, m_ssd_conv_w=m_ssd_conv_w, m_ssd_conv_b=m_ssd_conv_b, m_ssd_A_log=m_ssd_A_log, m_ssd_dt_bias=m_ssd_dt_bias, m_ssd_D=m_ssd_D, m_ssd_norm_g=m_ssd_norm_g, m_ssd_w_out=m_ssd_w_out, m_final_g=m_final_g, v_ada_w=v_ada_w, v_ada_b=v_ada_b, v_norm_g=v_norm_g, v_ffn_w1=v_ffn_w1, v_ffn_w3=v_ffn_w3, v_ffn_w2=v_ffn_w2, v_ev_w_in=v_ev_w_in, v_gdn_conv_w=v_gdn_conv_w, v_gdn_A_log=v_gdn_A_log, v_gdn_dt_bias=v_gdn_dt_bias, v_gdn_norm_g=v_gdn_norm_g, v_mla_q_norm_g=v_mla_q_norm_g, v_mla_w_uq=v_mla_w_uq, v_mla_kv_norm_g=v_mla_kv_norm_g, v_mla_w_ukv=v_mla_w_ukv, v_ev_w_out=v_ev_w_out, v_ssd_w_in=v_ssd_w_in, v_ssd_conv_w=v_ssd_conv_w, v_ssd_conv_b=v_ssd_conv_b, v_ssd_A_log=v_ssd_A_log, v_ssd_dt_bias=v_ssd_dt_bias, v_ssd_D=v_ssd_D, v_ssd_norm_g=v_ssd_norm_g, v_ssd_w_out=v_ssd_w_out, v_final_g=v_final_g)
    weights = {n: given[n] for n in TWIN_WEIGHTS}
    shared = {n: given[n] for n in SHARED_INPUTS}
    per_example = {n: given[n] for n in ['x', 'c', 'positions']}
    grad_fn = _jax.value_and_grad(_loss, argnums=(0, 1))

    def one_microbatch(ex, loss_target):
        ex = dict(ex)
        diff = ex.pop(TWIN_DIFF_INPUT)
        return grad_fn(weights, diff, {**shared, **ex}, loss_target)

    if N_MICROBATCH == 1:
        loss, (grad_w, grad_x) = one_microbatch(per_example, given["loss_target"])
    else:
        def body(carry, xs):
            loss_sum, grad_sum = carry
            l_k, (gw_k, gx_k) = one_microbatch(xs[0], xs[1])
            with _jax.named_scope("update"):
                return (loss_sum + l_k, _jax.tree.map(_jnp.add, grad_sum, gw_k)), gx_k

        init = (_jnp.zeros((), _jnp.float32), _jax.tree.map(_jnp.zeros_like, weights))
        (loss, grad_w), grad_x = _jax.lax.scan(body, init, (per_example, given["loss_target"]))
    with _jax.named_scope("update"):
        delta_w, new_m, new_v = {}, {}, {}
        for n in TWIN_WEIGHTS:
            delta_w[n], new_m[n], new_v[n] = _adamw(weights[n], grad_w[n], given["m_" + n], given["v_" + n])
    return (loss, grad_x, *[grad_w[n] for n in TWIN_WEIGHTS], *[delta_w[n] for n in TWIN_WEIGHTS],
            *[new_m[n] for n in TWIN_WEIGHTS], *[new_v[n] for n in TWIN_WEIGHTS])
```

```python
import functools
import math

import jax
import jax.numpy as jnp
from jax import lax
from jax.experimental import pallas as pl
from jax.experimental.pallas import tpu as pltpu

F32 = jnp.float32
BF16 = jnp.bfloat16
HI = lax.Precision.HIGHEST
VMEM_LIMIT = 56 * 1024 * 1024
NORM_EPS = 1e-6


def _cp(sem=None):
    if sem is None:
        return pltpu.CompilerParams(vmem_limit_bytes=VMEM_LIMIT)
    return pltpu.CompilerParams(dimension_semantics=sem, vmem_limit_bytes=VMEM_LIMIT)


def _pick(dim, prefs):
    for p in prefs:
        if dim % p == 0:
            return p
    return dim


def mm(pairs, *, ta=False, tb=False, out_dtype=F32, name, epi=None, epi_rows=(), epi_pars=(), epi_out_dtypes=None,
       tm=None, tn=None, tk=None):
    if not isinstance(pairs, (list, tuple)) or not isinstance(pairs[0], (list, tuple)):
        pairs = [pairs]
    npair = len(pairs)
    a0, b0 = pairs[0]
    M = a0.shape[1] if ta else a0.shape[0]
    K = a0.shape[0] if ta else a0.shape[1]
    N = b0.shape[0] if tb else b0.shape[1]
    for a, b in pairs:
        assert (a.shape == ((K, M) if ta else (M, K))), (a.shape, M, K)
        assert (b.shape == ((N, K) if tb else (K, N))), (b.shape, K, N)
    tm = tm or _pick(M, (1024, 512, 256, 128))
    tn = tn or _pick(N, (512, 384, 256, 128))
    tk = tk or (K if K <= 1024 else _pick(K, (1024, 1408, 512, 256, 128)))
    nk = K // tk
    assert M % tm == 0 and N % tn == 0 and K % tk == 0, (M, N, K, tm, tn, tk)
    n_rows, n_pars = len(epi_rows), len(epi_pars)
    if epi is None:
        out_dtypes = (out_dtype,)
    else:
        out_dtypes = tuple(epi_out_dtypes)
    n_out = len(out_dtypes)
    dn = (((0 if ta else 1,), (1 if tb else 0,)), ((), ()))

    def body(*refs):
        ab = refs[:2 * npair]
        rows = refs[2 * npair:2 * npair + n_rows]
        pars = refs[2 * npair + n_rows:2 * npair + n_rows + n_pars]
        outs = refs[2 * npair + n_rows + n_pars:2 * npair + n_rows + n_pars + n_out]
        acc_ref = refs[-1]
        k = pl.program_id(2)

        @pl.when(k == 0)
        def _():
            acc_ref[...] = jnp.zeros_like(acc_ref)

        acc = acc_ref[...]
        for p in range(npair):
            a = ab[2 * p][...].astype(BF16)
            b = ab[2 * p + 1][...].astype(BF16)
            acc = acc + lax.dot_general(a, b, dn, preferred_element_type=F32)
        acc_ref[...] = acc

        @pl.when(k == nk - 1)
        def _():
            r = acc_ref[...]
            if epi is None:
                outs[0][...] = r.astype(outs[0].dtype)
            else:
                res = epi(r, *[x[...] for x in rows], *[x[...] for x in pars])
                for o, v in zip(outs, res):
                    o[...] = v.astype(o.dtype)

    a_spec = pl.BlockSpec((tk, tm), lambda i, j, k: (k, i)) if ta else pl.BlockSpec((tm, tk), lambda i, j, k: (i, k))
    b_spec = pl.BlockSpec((tn, tk), lambda i, j, k: (j, k)) if tb else pl.BlockSpec((tk, tn), lambda i, j, k: (k, j))
    in_specs = []
    args = []
    for a, b in pairs:
        in_specs += [a_spec, b_spec]
        args += [a, b]
    for r in epi_rows:
        in_specs.append(pl.BlockSpec((tm, tn), lambda i, j, k: (i, j)))
        args.append(r)
    for p_ in epi_pars:
        in_specs.append(pl.BlockSpec((1, tn), lambda i, j, k: (0, j)))
        args.append(p_)
    out_specs = [pl.BlockSpec((tm, tn), lambda i, j, k: (i, j)) for _ in range(n_out)]
    out_shape = [jax.ShapeDtypeStruct((M, N), d) for d in out_dtypes]
    res = pl.pallas_call(
        body, grid=(M // tm, N // tn, nk), in_specs=in_specs, out_specs=out_specs, out_shape=out_shape,
        scratch_shapes=[pltpu.VMEM((tm, tn), F32)], compiler_params=_cp(("parallel", "parallel", "arbitrary")), name=name,
    )(*args)
    return res[0] if epi is None else tuple(res)


def rowmap(fn, rows, pars, out_dtypes, *, name, tile=256, n_reduce=0):
    views = []
    for r in rows:
        if isinstance(r, tuple):
            views.append(r)
        else:
            views.append((r, r.shape[1], 0))
    S = views[0][0].shape[0]
    assert S % tile == 0
    nt = S // tile
    row_structs = [jax.ShapeDtypeStruct((tile, w), a.dtype) for a, w, _ in views]
    par_structs = [jax.ShapeDtypeStruct(p.shape, p.dtype) for p in pars]
    out_structs = jax.eval_shape(fn, *row_structs, *par_structs)
    n_out = len(out_structs)
    n_row_out = n_out - n_reduce
    nr, npar = len(views), len(pars)

    def body(*refs):
        ins = [x[...] for x in refs[:nr + npar]]
        outs = refs[nr + npar:]
        res = fn(*ins)
        for o, v in zip(outs[:n_row_out], res[:n_row_out]):
            o[...] = v.astype(o.dtype)
        if n_reduce:
            i = pl.program_id(0)

            @pl.when(i == 0)
            def _():
                for o, v in zip(outs[n_row_out:], res[n_row_out:]):
                    o[...] = v.astype(o.dtype)

            @pl.when(i > 0)
            def _():
                for o, v in zip(outs[n_row_out:], res[n_row_out:]):
                    o[...] += v.astype(o.dtype)

    in_specs = [pl.BlockSpec((tile, w), functools.partial(lambda i, c: (i, c), c=c)) for _, w, c in views]
    in_specs += [pl.BlockSpec(p.shape, lambda i: (0, 0)) for p in pars]
    out_specs = [pl.BlockSpec((tile, s.shape[1]), lambda i: (i, 0)) for s in out_structs[:n_row_out]]
    out_specs += [pl.BlockSpec(s.shape, lambda i: (0, 0)) for s in out_structs[n_row_out:]]
    out_shape = [jax.ShapeDtypeStruct((S, s.shape[1]), d) for s, d in zip(out_structs[:n_row_out], out_dtypes[:n_row_out])]
    out_shape += [jax.ShapeDtypeStruct(s.shape, F32) for s in out_structs[n_row_out:]]
    res = pl.pallas_call(
        body, grid=(nt,), in_specs=in_specs, out_specs=out_specs, out_shape=out_shape,
        compiler_params=_cp(("arbitrary",) if n_reduce else ("parallel",)), name=name,
    )(*[v[0] for v in views], *pars)
    return tuple(res)


CH = 64


def _softplus(x):
    return jnp.where(x > 20.0, x, jnp.log(1.0 + jnp.exp(jnp.minimum(x, 20.0))))


def _dot(a, b, dn=(((1,), (0,)), ((), ())), hi=False):
    if hi:
        return lax.dot_general(a.astype(F32), b.astype(F32), dn, precision=HI, preferred_element_type=F32)
    return lax.dot_general(a.astype(BF16), b.astype(BF16), dn, preferred_element_type=F32)


_NT = (((1,), (1,)), ((), ()))
_TN = (((0,), (0,)), ((), ()))


def _chunk_consts():
    r = lax.broadcasted_iota(jnp.int32, (CH, 2 * CH), 0)
    c0 = lax.broadcasted_iota(jnp.int32, (CH, 2 * CH), 1)
    c = jnp.where(c0 >= CH, c0 - CH, c0)
    r1 = lax.broadcasted_iota(jnp.int32, (CH, CH), 0)
    c1 = lax.broadcasted_iota(jnp.int32, (CH, CH), 1)
    return dict(
        lower2=r >= c, strict2=r > c, U2=(r <= c).astype(F32), eye2=(r == c).astype(F32),
        L=(r1 >= c1).astype(F32), ones=jnp.ones((CH, CH), F32), Z=jnp.zeros((CH, 2 * CH), F32))


def _gdn_head(q, k, v, bx, ax, S, alog, dtb, cst):
    lower2, strict2, U2, eye2, L, ones, Z = (cst[n] for n in ("lower2", "strict2", "U2", "eye2", "L", "ones", "Z"))
    qn = q * lax.rsqrt(jnp.sum(q * q, axis=-1, keepdims=True) + NORM_EPS) * (128.0 ** -0.5)
    kn = k * lax.rsqrt(jnp.sum(k * k, axis=-1, keepdims=True) + NORM_EPS)
    beta = jax.nn.sigmoid(bx)
    g = -jnp.exp(alog) * _softplus(ax + dtb)
    gc = _dot(L, g, hi=True)
    n2 = _dot(ones, g * U2, hi=True)
    d2 = gc - n2
    decay2 = jnp.where(lower2, jnp.exp(jnp.where(lower2, d2, 0.0)), 0.0)
    kb = kn * beta
    kn2 = jnp.concatenate([kn, kn], axis=0)
    a2 = jnp.where(strict2, _dot(kb, kn2, _NT) * decay2, 0.0)

    def prod(x2, y):
        return _dot(x2, jnp.concatenate([y, Z], axis=0), hi=True)

    b = -a2
    t2 = eye2 + b
    for _ in range(5):
        b = prod(b, b)
        t2 = t2 + prod(t2, b)
    glast = jnp.sum(g, axis=0, keepdims=True)
    u = prod(t2, v * beta)
    w = prod(t2, kb * jnp.exp(gc))
    attn2 = jnp.where(lower2, _dot(qn, kn2, _NT) * decay2, 0.0)
    k_end = kn * jnp.exp(glast - gc)
    q_start = qn * jnp.exp(gc)
    v_new = u - _dot(w, S)
    o = _dot(q_start, S) + _dot(attn2, jnp.concatenate([v_new, Z], axis=0))
    s_new = S * jnp.exp(glast) + _dot(k_end, v_new, _TN)
    return o, s_new


def gdn_fwd(qkv, proj, bcol, acol, alog_e, dtb_e, *, name):
    S_ = qkv.shape[0]
    nc = S_ // CH

    def body(q_ref, k_ref, v_ref, b_ref, a_ref, al_ref, dt_ref, o_ref, hist_ref, s_ref):
        i = pl.program_id(0)

        @pl.when(i == 0)
        def _():
            s_ref[...] = jnp.zeros_like(s_ref)

        cst = _chunk_consts()
        hist_ref[0] = s_ref[...]
        for h in range(4):
            ls = slice(128 * h, 128 * (h + 1))
            o, s_new = _gdn_head(q_ref[:, ls], k_ref[:, ls], v_ref[:, ls], b_ref[:, ls], a_ref[:, ls], s_ref[ls, :],
                                 al_ref[:, ls], dt_ref[:, ls], cst)
            o_ref[:, ls] = o
            s_ref[ls, :] = s_new

    blk = lambda cb: pl.BlockSpec((CH, 512), functools.partial(lambda i, cb: (i, cb), cb=cb))
    par = pl.BlockSpec((1, 512), lambda i: (0, 0))
    return pl.pallas_call(
        body, grid=(nc,), in_specs=[blk(0), blk(1), blk(2), blk(bcol), blk(acol), par, par],
        out_specs=[pl.BlockSpec((CH, 512), lambda i: (i, 0)), pl.BlockSpec((1, 512, 128), lambda i: (i, 0, 0))],
        out_shape=[jax.ShapeDtypeStruct((S_, 512), F32), jax.ShapeDtypeStruct((nc, 512, 128), F32)],
        scratch_shapes=[pltpu.VMEM((512, 128), F32)], compiler_params=_cp(("arbitrary",)), name=name,
    )(qkv, qkv, qkv, proj, proj, alog_e, dtb_e)


def gdn_bwd(qkv, proj, bcol, acol, alog_e, dtb_e, hist, do, *, name):
    S_ = qkv.shape[0]
    nc = S_ // CH

    def body(q_ref, k_ref, v_ref, b_ref, a_ref, al_ref, dt_ref, hist_ref, do_ref, dqkv_ref, db_ref, da_ref, dal_ref, ddt_ref, ds_ref):
        i = pl.program_id(0)

        @pl.when(i == 0)
        def _():
            ds_ref[...] = jnp.zeros_like(ds_ref)
            dal_ref[...] = jnp.zeros_like(dal_ref)
            ddt_ref[...] = jnp.zeros_like(ddt_ref)

        cst = _chunk_consts()
        for h in range(4):
            ls = slice(128 * h, 128 * (h + 1))
            fn = functools.partial(_gdn_head, cst=cst)
            _, vjp = jax.vjp(fn, q_ref[:, ls], k_ref[:, ls], v_ref[:, ls], b_ref[:, ls], a_ref[:, ls], hist_ref[0, ls, :],
                             al_ref[:, ls], dt_ref[:, ls])
            dq, dk, dv, db, da, ds_in, dal, ddt = vjp((do_ref[:, ls], ds_ref[ls, :]))
            dqkv_ref[:, 128 * h:128 * (h + 1)] = dq
            dqkv_ref[:, 512 + 128 * h:512 + 128 * (h + 1)] = dk
            dqkv_ref[:, 1024 + 128 * h:1024 + 128 * (h + 1)] = dv
            db_ref[:, ls] = db
            da_ref[:, ls] = da
            ds_ref[ls, :] = ds_in
            dal_ref[:, ls] += dal
            ddt_ref[:, ls] += ddt

    rblk = lambda cb: pl.BlockSpec((CH, 512), functools.partial(lambda i, cb: (nc - 1 - i, cb), cb=cb))
    par = pl.BlockSpec((1, 512), lambda i: (0, 0))
    return pl.pallas_call(
        body, grid=(nc,),
        in_specs=[rblk(0), rblk(1), rblk(2), rblk(bcol), rblk(acol), par, par,
                  pl.BlockSpec((1, 512, 128), lambda i: (nc - 1 - i, 0, 0)), rblk(0)],
        out_specs=[pl.BlockSpec((CH, 1536), lambda i: (nc - 1 - i, 0)), rblk(0), rblk(0), par, par],
        out_shape=[jax.ShapeDtypeStruct((S_, 1536), F32), jax.ShapeDtypeStruct((S_, 512), F32), jax.ShapeDtypeStruct((S_, 512), F32),
                   jax.ShapeDtypeStruct((1, 512), F32), jax.ShapeDtypeStruct((1, 512), F32)],
        scratch_shapes=[pltpu.VMEM((512, 128), F32)], compiler_params=_cp(("arbitrary",)), name=name,
    )(qkv, qkv, qkv, proj, proj, alog_e, dtb_e, hist, do)


def _ssd_group(xs, dtxs, bm, cm, hss, nas, dtbs, dsks, cst):
    lower2, U2, L, ones = cst["lower2"], cst["U2"], cst["L"], cst["ones"]
    lane = lax.broadcasted_iota(jnp.int32, (1, 2 * CH), 1)
    mask_l = (lane < CH).astype(F32)
    mask_r = 1.0 - mask_l
    ones_w = jnp.ones((CH, 2 * CH), F32)
    cb2 = _dot(cm, jnp.concatenate([bm, bm], axis=0), _NT)
    ys, hs_new = [], []
    for x, dtx, hs, na, dtb, dsk in zip(xs, dtxs, hss, nas, dtbs, dsks):
        dt = _softplus(dtx + dtb)
        da = dt * na
        m = _dot(L, da, hi=True)
        n2 = _dot(ones, da * U2, hi=True)
        lm2 = jnp.where(lower2, jnp.exp(jnp.where(lower2, m - n2, 0.0)), 0.0)
        xdt = x * dt
        x2 = jnp.concatenate([xdt * mask_l, xdt * mask_r], axis=0)
        y_diag = _dot(cb2 * lm2, x2)
        alast = jnp.sum(da, axis=0, keepdims=True)
        y_off = _dot(cm, hs, _NT) * jnp.exp(m)
        cd = jnp.exp(_dot(da, ones_w, _TN, hi=True))
        hs_new.append(hs * cd + _dot(xdt * jnp.exp(alast - m), bm, _TN))
        ys.append(y_diag + y_off + dsk * x)
    return tuple(ys), tuple(hs_new)


def _ssd_specs(nc, rev):
    ci = (lambda i: nc - 1 - i) if rev else (lambda i: i)
    xg = pl.BlockSpec((CH, 512), lambda i, g: (ci(i), g))
    dtg = pl.BlockSpec((CH, 512), lambda i, g: (ci(i), 10 + g))
    bg = pl.BlockSpec((CH, 128), lambda i, g: (ci(i), 16 + g))
    cg = pl.BlockSpec((CH, 128), lambda i, g: (ci(i), 20 + g))
    par = pl.BlockSpec((4, 1, 512), lambda i, g: (0, 0, 0))
    hist = pl.BlockSpec((1, 512, 128), lambda i, g: (ci(i), g, 0))
    return xg, dtg, bg, cg, par, hist


def ssd_fwd(xbc, proj, na_e, dtb_e, dsk_e, *, name):
    S_ = xbc.shape[0]
    nc = S_ // CH
    xg, dtg, bg, cg, par, hist = _ssd_specs(nc, False)

    def body(x_ref, dt_ref, b_ref, c_ref, na_ref, dtb_ref, dsk_ref, y_ref, hist_ref, s_ref):
        i, g = pl.program_id(0), pl.program_id(1)

        @pl.when(i == 0)
        def _():
            s_ref[g] = jnp.zeros((512, 128), F32)

        cst = _chunk_consts()
        hist_ref[0] = s_ref[g]
        sl = [slice(128 * p, 128 * (p + 1)) for p in range(4)]
        na, dtb, dsk = na_ref[g], dtb_ref[g], dsk_ref[g]
        ys, hs_new = _ssd_group([x_ref[:, s] for s in sl], [dt_ref[:, s] for s in sl], b_ref[...], c_ref[...],
                                [s_ref[g, s, :] for s in sl], [na[:, s] for s in sl], [dtb[:, s] for s in sl],
                                [dsk[:, s] for s in sl], cst)
        for p, s in enumerate(sl):
            y_ref[:, s] = ys[p]
            s_ref[g, s, :] = hs_new[p]

    return pl.pallas_call(
        body, grid=(nc, 4), in_specs=[xg, dtg, bg, cg, par, par, par],
        out_specs=[pl.BlockSpec((CH, 512), lambda i, g: (i, g)), hist],
        out_shape=[jax.ShapeDtypeStruct((S_, 2048), F32), jax.ShapeDtypeStruct((nc, 2048, 128), F32)],
        scratch_shapes=[pltpu.VMEM((4, 512, 128), F32)], compiler_params=_cp(("arbitrary", "arbitrary")), name=name,
    )(xbc, proj, xbc, xbc, na_e, dtb_e, dsk_e)


def ssd_bwd(xbc, proj, na_e, dtb_e, dsk_e, hist, dy, *, name):
    S_ = xbc.shape[0]
    nc = S_ // CH
    xg, dtg, bg, cg, par, hist_spec = _ssd_specs(nc, True)
    og = pl.BlockSpec((CH, 512), lambda i, g: (nc - 1 - i, g))
    o128 = pl.BlockSpec((CH, 128), lambda i, g: (nc - 1 - i, g))

    def body(x_ref, dt_ref, b_ref, c_ref, na_ref, dtb_ref, dsk_ref, hist_ref, dy_ref,
             dx_ref, ddt_ref, db_ref, dc_ref, dna_ref, ddtb_ref, ddsk_ref, ds_ref):
        i, g = pl.program_id(0), pl.program_id(1)

        @pl.when(i == 0)
        def _():
            ds_ref[g] = jnp.zeros((512, 128), F32)
            dna_ref[g] = jnp.zeros((1, 512), F32)
            ddtb_ref[g] = jnp.zeros((1, 512), F32)
            ddsk_ref[g] = jnp.zeros((1, 512), F32)

        cst = _chunk_consts()
        sl = [slice(128 * p, 128 * (p + 1)) for p in range(4)]
        na, dtb, dsk = na_ref[g], dtb_ref[g], dsk_ref[g]
        fn = functools.partial(_ssd_group, cst=cst)
        _, vjp = jax.vjp(fn, [x_ref[:, s] for s in sl], [dt_ref[:, s] for s in sl], b_ref[...], c_ref[...],
                         [hist_ref[0, s, :] for s in sl], [na[:, s] for s in sl], [dtb[:, s] for s in sl], [dsk[:, s] for s in sl])
        dxs, ddts, db, dc, dhs, dnas, ddtbs, ddsks = vjp((tuple(dy_ref[:, s] for s in sl), tuple(ds_ref[g, s, :] for s in sl)))
        db_ref[...] = db
        dc_ref[...] = dc
        for p, s in enumerate(sl):
            dx_ref[:, s] = dxs[p]
            ddt_ref[:, s] = ddts[p]
            ds_ref[g, s, :] = dhs[p]
            dna_ref[g, :, s] += dnas[p]
            ddtb_ref[g, :, s] += ddtbs[p]
            ddsk_ref[g, :, s] += ddsks[p]

    return pl.pallas_call(
        body, grid=(nc, 4), in_specs=[xg, dtg, bg, cg, par, par, par, hist_spec, og],
        out_specs=[og, og, o128, o128, par, par, par],
        out_shape=[jax.ShapeDtypeStruct((S_, 2048), F32), jax.ShapeDtypeStruct((S_, 2048), F32),
                   jax.ShapeDtypeStruct((S_, 512), F32), jax.ShapeDtypeStruct((S_, 512), F32)] +
                  [jax.ShapeDtypeStruct((4, 1, 512), F32)] * 3,
        scratch_shapes=[pltpu.VMEM((4, 512, 128), F32)], compiler_params=_cp(("arbitrary", "arbitrary")), name=name,
    )(xbc, proj, xbc, xbc, na_e, dtb_e, dsk_e, hist, dy)


ATT_T = 1024
ATT_SCALE = 192.0 ** -0.5
NEG = -1e30


def _att_scores(qn, qp, kn, kp, i, j):
    s = (_dot(qn, kn, _NT) + _dot(qp, kp, _NT)) * ATT_SCALE
    qc = (lax.broadcasted_iota(jnp.int32, s.shape, 0) + i * ATT_T) // CH
    kc = (lax.broadcasted_iota(jnp.int32, s.shape, 1) + j * ATT_T) // CH
    return s, kc <= qc


def att_fwd(q, kv, kp, *, name):
    S_ = q.shape[0]
    T = min(ATT_T, S_)
    n = S_ // T

    def body(qn_ref, qp_ref, kn_ref, kp_ref, v_ref, o_ref, lse_ref, m_ref, l_ref, acc_ref):
        i, j = pl.program_id(1), pl.program_id(2)

        @pl.when(j == 0)
        def _():
            m_ref[...] = jnp.full_like(m_ref, NEG)
            l_ref[...] = jnp.zeros_like(l_ref)
            acc_ref[...] = jnp.zeros_like(acc_ref)

        @pl.when(j <= i)
        def _():
            s, mask = _att_scores(qn_ref[...], qp_ref[...], kn_ref[...], kp_ref[...], i, j)
            s = jnp.where(mask, s, NEG)
            m_prev = m_ref[...]
            m_cur = jnp.maximum(m_prev, jnp.max(s, axis=-1, keepdims=True))
            p = jnp.exp(s - m_cur[:, :1])
            alpha = jnp.exp(m_prev - m_cur)
            l_ref[...] = alpha * l_ref[...] + jnp.sum(p, axis=-1, keepdims=True)
            acc_ref[...] = acc_ref[...] * alpha + _dot(p, v_ref[...])
            m_ref[...] = m_cur

        @pl.when(j == i)
        def _():
            o_ref[...] = acc_ref[...] / l_ref[...]
            lse_ref[...] = m_ref[...] + jnp.log(l_ref[...])

    qs = lambda off: pl.BlockSpec((T, 128), functools.partial(lambda h, i, j, off: (i, off + h), off=off))
    ks = lambda off: pl.BlockSpec((T, 128), functools.partial(lambda h, i, j, off: (jnp.minimum(j, i), off + h), off=off))
    kps = pl.BlockSpec((T, 128), lambda h, i, j: (jnp.minimum(j, i), 0))
    os_ = pl.BlockSpec((T, 128), lambda h, i, j: (i, h))
    return pl.pallas_call(
        body, grid=(4, n, n), in_specs=[qs(0), qs(4), ks(0), kps, ks(4)], out_specs=[os_, os_],
        out_shape=[jax.ShapeDtypeStruct((S_, 512), F32), jax.ShapeDtypeStruct((S_, 512), F32)],
        scratch_shapes=[pltpu.VMEM((T, 128), F32)] * 3, compiler_params=_cp(("parallel", "parallel", "arbitrary")), name=name,
    )(q, q, kv, kp, kv)


def att_bwd_dq(q, kv, kp, o, lse, do, *, name):
    S_ = q.shape[0]
    T = min(ATT_T, S_)
    n = S_ // T

    def body(qn_ref, qp_ref, kn_ref, kp_ref, v_ref, o_ref, lse_ref, do_ref, dqn_ref, dqp_ref, d_ref):
        i, j = pl.program_id(1), pl.program_id(2)

        @pl.when(j == 0)
        def _():
            d = jnp.sum(do_ref[...] * o_ref[...], axis=-1, keepdims=True)
            d_ref[...] = jnp.broadcast_to(d, d_ref.shape)
            dqn_ref[...] = jnp.zeros_like(dqn_ref)
            dqp_ref[...] = jnp.zeros_like(dqp_ref)

        @pl.when(j <= i)
        def _():
            s, mask = _att_scores(qn_ref[...], qp_ref[...], kn_ref[...], kp_ref[...], i, j)
            p = jnp.where(mask, jnp.exp(s - lse_ref[:, :1]), 0.0)
            dp = _dot(do_ref[...], v_ref[...], _NT)
            ds = p * (dp - d_ref[:, :1]) * ATT_SCALE
            dqn_ref[...] += _dot(ds, kn_ref[...])
            dqp_ref[...] += _dot(ds, kp_ref[...])

    qs = lambda off: pl.BlockSpec((T, 128), functools.partial(lambda h, i, j, off: (i, off + h), off=off))
    ks = lambda off: pl.BlockSpec((T, 128), functools.partial(lambda h, i, j, off: (jnp.minimum(j, i), off + h), off=off))
    kps = pl.BlockSpec((T, 128), lambda h, i, j: (jnp.minimum(j, i), 0))
    os_ = pl.BlockSpec((T, 128), lambda h, i, j: (i, h))
    dqn, dqp, d = pl.pallas_call(
        body, grid=(4, n, n), in_specs=[qs(0), qs(4), ks(0), kps, ks(4), os_, os_, os_], out_specs=[os_, os_, os_],
        out_shape=[jax.ShapeDtypeStruct((S_, 512), F32)] * 3,
        compiler_params=_cp(("parallel", "parallel", "arbitrary")), name=name,
    )(q, q, kv, kp, kv, o, lse, do)
    return dqn, dqp, d


def att_bwd_dkv(q, kv, kp, lse, d, do, *, name):
    S_ = q.shape[0]
    T = min(ATT_T, S_)
    n = S_ // T

    def body(qn_ref, qp_ref, kn_ref, kp_ref, v_ref, lse_ref, d_ref, do_ref, dkn_ref, dv_ref, dkp_ref):
        j, h, i = pl.program_id(0), pl.program_id(1), pl.program_id(2)

        @pl.when(i == 0)
        def _():
            dkn_ref[...] = jnp.zeros_like(dkn_ref)
            dv_ref[...] = jnp.zeros_like(dv_ref)

        @pl.when((i == 0) & (h == 0))
        def _():
            dkp_ref[...] = jnp.zeros_like(dkp_ref)

        @pl.when(i >= j)
        def _():
            s, mask = _att_scores(qn_ref[...], qp_ref[...], kn_ref[...], kp_ref[...], i, j)
            p = jnp.where(mask, jnp.exp(s - lse_ref[:, :1]), 0.0)
            dv_ref[...] += _dot(p, do_ref[...], _TN)
            dp = _dot(do_ref[...], v_ref[...], _NT)
            ds = p * (dp - d_ref[:, :1]) * ATT_SCALE
            dkn_ref[...] += _dot(ds, qn_ref[...], _TN)
            dkp_ref[...] += _dot(ds, qp_ref[...], _TN)

    qs = lambda off: pl.BlockSpec((T, 128), functools.partial(lambda j, h, i, off: (jnp.maximum(i, j), off + h), off=off))
    ks = lambda off: pl.BlockSpec((T, 128), functools.partial(lambda j, h, i, off: (j, off + h), off=off))
    kps = pl.BlockSpec((T, 128), lambda j, h, i: (j, 0))
    return pl.pallas_call(
        body, grid=(n, 4, n), in_specs=[qs(0), qs(4), ks(0), kps, ks(4), qs(0), qs(0), qs(0)], out_specs=[ks(0), ks(0), kps],
        out_shape=[jax.ShapeDtypeStruct((S_, 512), F32), jax.ShapeDtypeStruct((S_, 512), F32), jax.ShapeDtypeStruct((S_, 128), F32)],
        compiler_params=_cp(("arbitrary", "arbitrary", "arbitrary")), name=name,
    )(q, q, kv, kp, kv, lse, d, do)


CONV_T = 256


def _shift_down(x, halo, s):
    sh = pltpu.roll(x, s, axis=0)
    hr = pltpu.roll(halo, s, axis=0)
    r8 = lax.broadcasted_iota(jnp.int32, hr.shape, 0)
    top = jnp.where(r8 < s, hr, sh[:8])
    return jnp.concatenate([top, sh[8:]], axis=0)


def _shift_up(x, halo, s):
    n = x.shape[0]
    sh = pltpu.roll(x, n - s, axis=0)
    hr = pltpu.roll(halo, 8 - s, axis=0)
    r8 = lax.broadcasted_iota(jnp.int32, hr.shape, 0)
    bot = jnp.where(r8 >= 8 - s, hr, sh[n - 8:])
    return jnp.concatenate([sh[:n - 8], bot], axis=0)


def _conv_pre(x, halo, w, b):
    y = x * w[3:4] + b
    for j in range(3):
        y = y + _shift_down(x, halo, 3 - j) * w[j:j + 1]
    return y


def conv_fwd(src, cb0, ncb, w, b, *, name):
    S_ = src.shape[0]
    T = min(CONV_T, S_)
    nt = S_ // T

    def body(x_ref, h_ref, w_ref, b_ref, o_ref):
        i = pl.program_id(1)
        halo = jnp.where(i > 0, h_ref[...], 0.0)
        y = _conv_pre(x_ref[...], halo, w_ref[...], b_ref[...])
        o_ref[...] = y * jax.nn.sigmoid(y)

    return pl.pallas_call(
        body, grid=(ncb, nt),
        in_specs=[pl.BlockSpec((T, 512), lambda c, i: (i, cb0 + c)),
                  pl.BlockSpec((8, 512), lambda c, i: (jnp.maximum(i * (T // 8) - 1, 0), cb0 + c)),
                  pl.BlockSpec((4, 512), lambda c, i: (0, c)), pl.BlockSpec((1, 512), lambda c, i: (0, c))],
        out_specs=pl.BlockSpec((T, 512), lambda c, i: (i, c)),
        out_shape=jax.ShapeDtypeStruct((S_, 512 * ncb), F32), compiler_params=_cp(("parallel", "parallel")), name=name,
    )(src, src, w, b)


def conv_bwd_pre(src, cb0, ncb, w, b, dy, *, name):
    S_ = src.shape[0]
    T = min(CONV_T, S_)
    nt = S_ // T

    def body(x_ref, h_ref, w_ref, b_ref, dy_ref, dp_ref, dw_ref, db_ref):
        i = pl.program_id(1)
        halo = jnp.where(i > 0, h_ref[...], 0.0)
        x = x_ref[...]
        y = _conv_pre(x, halo, w_ref[...], b_ref[...])
        sg = jax.nn.sigmoid(y)
        dpre = dy_ref[...] * (sg * (1.0 + y * (1.0 - sg)))
        dp_ref[...] = dpre
        rows = [jnp.sum(dpre * _shift_down(x, halo, 3 - j), axis=0, keepdims=True) for j in range(3)]
        rows.append(jnp.sum(dpre * x, axis=0, keepdims=True))
        dw = jnp.concatenate(rows, axis=0)
        db = jnp.sum(dpre, axis=0, keepdims=True)

        @pl.when(i == 0)
        def _():
            dw_ref[...] = dw
            db_ref[...] = db

        @pl.when(i > 0)
        def _():
            dw_ref[...] += dw
            db_ref[...] += db

    return pl.pallas_call(
        body, grid=(ncb, nt),
        in_specs=[pl.BlockSpec((T, 512), lambda c, i: (i, cb0 + c)),
                  pl.BlockSpec((8, 512), lambda c, i: (jnp.maximum(i * (T // 8) - 1, 0), cb0 + c)),
                  pl.BlockSpec((4, 512), lambda c, i: (0, c)), pl.BlockSpec((1, 512), lambda c, i: (0, c)),
                  pl.BlockSpec((T, 512), lambda c, i: (i, c))],
        out_specs=[pl.BlockSpec((T, 512), lambda c, i: (i, c)), pl.BlockSpec((4, 512), lambda c, i: (0, c)),
                   pl.BlockSpec((1, 512), lambda c, i: (0, c))],
        out_shape=[jax.ShapeDtypeStruct((S_, 512 * ncb), F32), jax.ShapeDtypeStruct((4, 512 * ncb), F32),
                   jax.ShapeDtypeStruct((1, 512 * ncb), F32)],
        compiler_params=_cp(("parallel", "arbitrary")), name=name,
    )(src, src, w, b, dy)


def conv_bwd_x(dpre, w, *, name, out_dtype=F32):
    S_, C = dpre.shape
    T = min(CONV_T, S_)
    nt = S_ // T
    ncb = C // 512

    def body(d_ref, h_ref, w_ref, o_ref):
        i = pl.program_id(1)
        halo = jnp.where(i < nt - 1, h_ref[...], 0.0)
        d = d_ref[...]
        w_ = w_ref[...]
        y = d * w_[3:4]
        for j in range(3):
            y = y + _shift_up(d, halo, 3 - j) * w_[j:j + 1]
        o_ref[...] = y.astype(o_ref.dtype)

    return pl.pallas_call(
        body, grid=(ncb, nt),
        in_specs=[pl.BlockSpec((T, 512), lambda c, i: (i, c)),
                  pl.BlockSpec((8, 512), lambda c, i: (jnp.minimum((i + 1) * (T // 8), S_ // 8 - 1), c)),
                  pl.BlockSpec((4, 512), lambda c, i: (0, c))],
        out_specs=pl.BlockSpec((T, 512), lambda c, i: (i, c)),
        out_shape=jax.ShapeDtypeStruct((S_, C), out_dtype), compiler_params=_cp(("parallel", "parallel")), name=name,
    )(dpre, dpre, w)


def ffn_mid_fwd(h, w1, w3, *, name):
    S_, D = h.shape
    F = w1.shape[1]
    tm, tn = _pick(S_, (1024, 512, 256)), 256

    def body(h_ref, w1_ref, w3_ref, a_ref):
        hb = h_ref[...]
        u = _dot(hb, w1_ref[...])
        v = _dot(hb, w3_ref[...])
        a_ref[...] = (u * jax.nn.sigmoid(u) * v).astype(a_ref.dtype)

    return pl.pallas_call(
        body, grid=(S_ // tm, F // tn),
        in_specs=[pl.BlockSpec((tm, D), lambda i, j: (i, 0)), pl.BlockSpec((D, tn), lambda i, j: (0, j)),
                  pl.BlockSpec((D, tn), lambda i, j: (0, j))],
        out_specs=pl.BlockSpec((tm, tn), lambda i, j: (i, j)), out_shape=jax.ShapeDtypeStruct((S_, F), BF16),
        compiler_params=_cp(("parallel", "parallel")), name=name,
    )(h, w1, w3)


def ffn_mid_bwd(h, dy, w1, w3, w2, *, name):
    S_, D = h.shape
    F = w1.shape[1]
    tm, tn = _pick(S_, (1024, 512, 256)), 256

    def body(h_ref, dy_ref, w1_ref, w3_ref, w2_ref, du_ref, dv_ref, a_ref):
        hb = h_ref[...]
        u = _dot(hb, w1_ref[...])
        v = _dot(hb, w3_ref[...])
        da = _dot(dy_ref[...], w2_ref[...], _NT)
        sg = jax.nn.sigmoid(u)
        si = u * sg
        a_ref[...] = (si * v).astype(a_ref.dtype)
        dv_ref[...] = (da * si).astype(dv_ref.dtype)
        du_ref[...] = (da * v * (sg * (1.0 + u * (1.0 - sg)))).astype(du_ref.dtype)

    o = pl.BlockSpec((tm, tn), lambda i, j: (i, j))
    return pl.pallas_call(
        body, grid=(S_ // tm, F // tn),
        in_specs=[pl.BlockSpec((tm, D), lambda i, j: (i, 0)), pl.BlockSpec((tm, D), lambda i, j: (i, 0)),
                  pl.BlockSpec((D, tn), lambda i, j: (0, j)), pl.BlockSpec((D, tn), lambda i, j: (0, j)),
                  pl.BlockSpec((tn, D), lambda i, j: (j, 0))],
        out_specs=[o, o, o], out_shape=[jax.ShapeDtypeStruct((S_, F), BF16)] * 3,
        compiler_params=_cp(("parallel", "parallel")), name=name,
    )(h, dy, w1, w3, w2)


MESH = pl.DeviceIdType.MESH
ANY = pl.BlockSpec(memory_space=pl.ANY)


def allgather8(x_shard, *, name):
    m_per, n = x_shard.shape

    def body(x_ref, out_ref, send_sems, recv_sems, local_sem):
        x, y, c = lax.axis_index("x"), lax.axis_index("y"), lax.axis_index("c")
        me, sibling = (x, y, c), (x, y, 1 - c)
        chips = [(1 - x, y), (x, 1 - y), (1 - x, 1 - y)]

        def rows(px, py, pc):
            return out_ref.at[pl.ds((4 * px + 2 * py + pc) * m_per, m_per), :]

        def copy(k, block, to, src=None):
            return pltpu.make_async_remote_copy(
                src_ref=rows(*block) if src is None else src, dst_ref=rows(*block),
                send_sem=send_sems.at[k], recv_sem=recv_sems.at[k], device_id=to, device_id_type=MESH)

        mine = pltpu.make_async_copy(x_ref, rows(*me), local_sem)
        mine.start()
        first = [copy(0, me, sibling, src=x_ref)]
        first += [copy(1 + j, me, (*chip, c), src=x_ref) for j, chip in enumerate(chips)]
        for cp in first:
            cp.start()
        passed = [copy(4 + j, (*chip, c), sibling) for j, chip in enumerate(chips)]
        for j, chip in enumerate(chips):
            copy(1 + j, (*chip, c), me).wait_recv()
            passed[j].start()
        copy(0, sibling, me).wait_recv()
        for j, chip in enumerate(chips):
            copy(4 + j, (*chip, 1 - c), me).wait_recv()
        for cp in first + passed:
            cp.wait_send()
        mine.wait()

    return pl.pallas_call(
        body, out_shape=jax.ShapeDtypeStruct((8 * m_per, n), x_shard.dtype),
        in_specs=[pl.BlockSpec(memory_space=pltpu.VMEM)], out_specs=pl.BlockSpec(memory_space=pltpu.VMEM),
        scratch_shapes=[pltpu.SemaphoreType.DMA((7,)), pltpu.SemaphoreType.DMA((7,)), pltpu.SemaphoreType.DMA],
        name=name,
    )(x_shard)


def _chip_peers():
    x, y, c = lax.axis_index("x"), lax.axis_index("y"), lax.axis_index("c")
    return x, y, c, [(1 - x, y), (x, 1 - y), (1 - x, 1 - y)]


def allgather_chips(x_shard, *, name):
    r, cdim = x_shard.shape

    def body(x_ref, out_ref, send_sems, recv_sems, local_sem):
        x, y, c, chips = _chip_peers()
        me = 2 * x + y
        mine = pltpu.make_async_copy(x_ref, out_ref.at[me], local_sem)
        mine.start()
        sends = []
        for k, (px, py) in enumerate(chips):
            cp = pltpu.make_async_remote_copy(src_ref=x_ref, dst_ref=out_ref.at[me], send_sem=send_sems.at[k],
                                              recv_sem=recv_sems.at[k], device_id=(px, py, c), device_id_type=MESH)
            cp.start()
            sends.append(cp)
        for k, (px, py) in enumerate(chips):
            pltpu.make_async_remote_copy(src_ref=x_ref, dst_ref=out_ref.at[2 * px + py], send_sem=send_sems.at[k],
                                         recv_sem=recv_sems.at[k], device_id=(px, py, c), device_id_type=MESH).wait_recv()
        for cp in sends:
            cp.wait_send()
        mine.wait()

    return pl.pallas_call(
        body, out_shape=jax.ShapeDtypeStruct((4, r, cdim), x_shard.dtype), in_specs=[ANY], out_specs=ANY,
        scratch_shapes=[pltpu.SemaphoreType.DMA((3,)), pltpu.SemaphoreType.DMA((3,)), pltpu.SemaphoreType.DMA],
        name=name,
    )(x_shard)


def exchange_chips(g, *, name):
    _, r, cdim = g.shape

    def body(g_ref, out_ref, send_sems, recv_sems, local_sem):
        x, y, c, chips = _chip_peers()
        me = 2 * x + y
        mine = pltpu.make_async_copy(g_ref.at[me], out_ref.at[me], local_sem)
        mine.start()
        sends = []
        for k, (px, py) in enumerate(chips):
            cp = pltpu.make_async_remote_copy(src_ref=g_ref.at[2 * px + py], dst_ref=out_ref.at[me], send_sem=send_sems.at[k],
                                              recv_sem=recv_sems.at[k], device_id=(px, py, c), device_id_type=MESH)
            cp.start()
            sends.append(cp)
        for k, (px, py) in enumerate(chips):
            pltpu.make_async_remote_copy(src_ref=g_ref.at[me], dst_ref=out_ref.at[2 * px + py], send_sem=send_sems.at[k],
                                         recv_sem=recv_sems.at[k], device_id=(px, py, c), device_id_type=MESH).wait_recv()
        for cp in sends:
            cp.wait_send()
        mine.wait()

    return pl.pallas_call(
        body, out_shape=jax.ShapeDtypeStruct(g.shape, g.dtype), in_specs=[ANY], out_specs=ANY,
        scratch_shapes=[pltpu.SemaphoreType.DMA((3,)), pltpu.SemaphoreType.DMA((3,)), pltpu.SemaphoreType.DMA],
        name=name,
    )(g)


def swap_sibling(p, *, name):
    def body(p_ref, out_ref, send_sem, recv_sem):
        x, y, c = lax.axis_index("x"), lax.axis_index("y"), lax.axis_index("c")
        cp = pltpu.make_async_remote_copy(src_ref=p_ref, dst_ref=out_ref, send_sem=send_sem, recv_sem=recv_sem,
                                          device_id=(x, y, 1 - c), device_id_type=MESH)
        cp.start()
        cp.wait()

    return pl.pallas_call(
        body, out_shape=jax.ShapeDtypeStruct(p.shape, p.dtype), in_specs=[ANY], out_specs=ANY,
        scratch_shapes=[pltpu.SemaphoreType.DMA, pltpu.SemaphoreType.DMA], name=name,
    )(p)


def sum_slots(r, *, name):
    n, rows, cdim = r.shape
    t = _pick(rows, (256, 128, 64, 32, 16, 8))

    def body(r_ref, o_ref):
        acc = r_ref[0]
        for s in range(1, n):
            acc = acc + r_ref[s]
        o_ref[...] = acc

    return pl.pallas_call(
        body, grid=(rows // t,), in_specs=[pl.BlockSpec((n, t, cdim), lambda i: (0, i, 0))],
        out_specs=pl.BlockSpec((t, cdim), lambda i: (i, 0)), out_shape=jax.ShapeDtypeStruct((rows, cdim), r.dtype),
        compiler_params=_cp(("parallel",)), name=name,
    )(r)


def _rms(x, g):
    return x * lax.rsqrt(jnp.mean(x * x, axis=-1, keepdims=True) + NORM_EPS) * g


def _adaln(x, g, shift, scale):
    return _rms(x, g) * (1.0 + scale) + shift


def _silu(x):
    return x * jax.nn.sigmoid(x)


def _gdn_gate(o, z, g):
    return jnp.concatenate([_rms(o[:, 128 * h:128 * (h + 1)], g) * _silu(z[:, 128 * h:128 * (h + 1)]) for h in range(4)], axis=1)


def _ssd_gate(y, z0, z1, z2, z3, g):
    outs = []
    for k, z in enumerate((z0, z1, z2, z3)):
        t = y[:, 512 * k:512 * (k + 1)] * _silu(z)
        outs.append(t * lax.rsqrt(jnp.mean(t * t, axis=-1, keepdims=True) + NORM_EPS))
    return jnp.concatenate(outs, axis=1) * g


def _rope(x, cos, sin, rot):
    return x * cos + _dot(x, rot, hi=True) * sin


def _rope_q(q, cos, sin, rot):
    return jnp.concatenate([q[:, :512]] + [_rope(q[:, 512 + 128 * h:640 + 128 * h], cos, sin, rot) for h in range(4)], axis=1)


def _rope_t(d, cos, sin, rot):
    return d * cos + _dot(d * sin, rot, _NT, hi=True)


def _vjp_rows(fn, n_rows, n_pars, out_dtypes, rows, cts, pars, *, name, tile=256, extra=None):
    nct = len(cts)

    def bwd(*a):
        r, c, e, p = a[:n_rows], a[n_rows:n_rows + nct], a[n_rows + nct:len(a) - n_pars], a[len(a) - n_pars:]
        out, vjp = jax.vjp(fn, *[t.astype(F32) for t in r], *p)
        ct = tuple(t.astype(F32) for t in c)
        grads = vjp(ct[0] if not isinstance(out, tuple) else ct)
        drows = list(grads[:n_rows])
        if e:
            drows[0] = drows[0] + e[0]
        return (*drows, *grads[n_rows:])

    return rowmap(bwd, list(rows) + list(cts) + ([extra] if extra is not None else []), list(pars), out_dtypes,
                  name=name, tile=tile, n_reduce=n_pars)


ADAM_LR, ADAM_B1, ADAM_B2, ADAM_EPS, ADAM_WD, ADAM_STEP = 0.001, 0.9, 0.999, 1e-08, 0.01, 10


def _adam_math(w, g, m, v):
    m = ADAM_B1 * m + (1.0 - ADAM_B1) * g
    v = ADAM_B2 * v + (1.0 - ADAM_B2) * (g * g)
    m_hat = m / (1.0 - ADAM_B1 ** ADAM_STEP)
    v_hat = v / (1.0 - ADAM_B2 ** ADAM_STEP)
    delta = -ADAM_LR * (m_hat / (jnp.sqrt(v_hat) + ADAM_EPS) + ADAM_WD * w)
    return delta, m, v


def adamw(w, gs, m, v, *, name):
    shape = w.shape
    last = shape[-1]
    to2 = lambda a: a.reshape(-1, last)
    rows = w.size // last
    tile = _pick(rows, (256, 128, 64, 32, 16, 8))
    ng = len(gs)

    def fn(w_, *rest):
        g = rest[0]
        for t in rest[1:ng]:
            g = g + t
        m_, v_ = rest[ng], rest[ng + 1]
        return (g, *_adam_math(w_, g, m_, v_))

    outs = rowmap(fn, [to2(w)] + [to2(g) for g in gs] + [to2(m), to2(v)], [], (F32,) * 4, name=name, tile=tile)
    return tuple(o.reshape(shape) for o in outs)


PACK_W = 1024
BIG = (
    ("ffn_w1", (4, 2, 1024, 704), 3), ("ffn_w3", (4, 2, 1024, 704), 3), ("ffn_w2", (4, 2, 704, 1024), 2),
    ("ev_w_in", (2, 1024, 690), 2), ("mla_w_uq", (2, 96, 4, 192), 1), ("mla_w_ukv", (2, 64, 4, 256), 1),
    ("ev_w_out", (2, 256, 1024), 1), ("ssd_w_in", (2, 1024, 1288), 2), ("ssd_w_out", (2, 512, 1024), 1))


def _seg_rows(shape):
    n = math.prod(shape)
    return -(-n // (16 * PACK_W)) * 16


def _pack(shards, dtype):
    parts = []
    for (_, shape, _), a in zip(BIG, shards):
        flat = a.reshape(-1).astype(dtype)
        pad = _seg_rows(shape) * PACK_W - flat.shape[0]
        parts.append(jnp.pad(flat, (0, pad)) if pad else flat)
    return jnp.concatenate(parts).reshape(-1, PACK_W)


def _unpack(buf):
    out, r0 = [], 0
    for _, shape, _ in BIG:
        n = math.prod(shape)
        out.append(buf[r0:r0 + _seg_rows(shape)].reshape(-1)[:n].reshape(shape))
        r0 += _seg_rows(shape)
    return out


SMALL_SHARDED = (
    ("norm_g", (4, 3, 256), 2), ("gdn_conv_w", (2, 4, 384), 2), ("ssd_conv_w", (2, 4, 768), 2),
    ("ssd_conv_b", (2, 768), 1), ("ssd_norm_g", (2, 512), 1))


def _flat_pack(arrs, width, row_mult):
    flat = jnp.concatenate([a.reshape(-1).astype(F32) for a in arrs])
    n = flat.shape[0]
    tot = -(-n // (width * row_mult)) * width * row_mult
    return jnp.pad(flat, (0, tot - n)).reshape(-1, width)


def _flat_unpack(buf, shapes):
    flat = buf.reshape(-1)
    out, o = [], 0
    for s in shapes:
        n = math.prod(s)
        out.append(flat[o:o + n].reshape(s))
        o += n
    return out


def _rep(v, n):
    return jnp.repeat(v, n, axis=-1)


def kernel(x, c, positions, ada_w, ada_b, norm_g, ffn_w1, ffn_w3, ffn_w2, ev_w_in, gdn_conv_w, gdn_A_log, gdn_dt_bias, gdn_norm_g, mla_q_norm_g, mla_w_uq, mla_kv_norm_g, mla_w_ukv, ev_w_out, ssd_w_in, ssd_conv_w, ssd_conv_b, ssd_A_log, ssd_dt_bias, ssd_D, ssd_norm_g, ssd_w_out, final_g, loss_target, m_ada_w, m_ada_b, m_norm_g, m_ffn_w1, m_ffn_w3, m_ffn_w2, m_ev_w_in, m_gdn_conv_w, m_gdn_A_log, m_gdn_dt_bias, m_gdn_norm_g, m_mla_q_norm_g, m_mla_w_uq, m_mla_kv_norm_g, m_mla_w_ukv, m_ev_w_out, m_ssd_w_in, m_ssd_conv_w, m_ssd_conv_b, m_ssd_A_log, m_ssd_dt_bias, m_ssd_D, m_ssd_norm_g, m_ssd_w_out, m_final_g, v_ada_w, v_ada_b, v_norm_g, v_ffn_w1, v_ffn_w3, v_ffn_w2, v_ev_w_in, v_gdn_conv_w, v_gdn_A_log, v_gdn_dt_bias, v_gdn_norm_g, v_mla_q_norm_g, v_mla_w_uq, v_mla_kv_norm_g, v_mla_w_ukv, v_ev_w_out, v_ssd_w_in, v_ssd_conv_w, v_ssd_conv_b, v_ssd_A_log, v_ssd_dt_bias, v_ssd_D, v_ssd_norm_g, v_ssd_w_out, v_final_g):
    P = dict(ada_w=ada_w, ada_b=ada_b, norm_g=norm_g, ffn_w1=ffn_w1, ffn_w3=ffn_w3, ffn_w2=ffn_w2, ev_w_in=ev_w_in, gdn_conv_w=gdn_conv_w, gdn_A_log=gdn_A_log, gdn_dt_bias=gdn_dt_bias, gdn_norm_g=gdn_norm_g, mla_q_norm_g=mla_q_norm_g, mla_w_uq=mla_w_uq, mla_kv_norm_g=mla_kv_norm_g, mla_w_ukv=mla_w_ukv, ev_w_out=ev_w_out, ssd_w_in=ssd_w_in, ssd_conv_w=ssd_conv_w, ssd_conv_b=ssd_conv_b, ssd_A_log=ssd_A_log, ssd_dt_bias=ssd_dt_bias, ssd_D=ssd_D, ssd_norm_g=ssd_norm_g, ssd_w_out=ssd_w_out, final_g=final_g)
    M1 = dict(ada_w=m_ada_w, ada_b=m_ada_b, norm_g=m_norm_g, ffn_w1=m_ffn_w1, ffn_w3=m_ffn_w3, ffn_w2=m_ffn_w2, ev_w_in=m_ev_w_in, gdn_conv_w=m_gdn_conv_w, gdn_A_log=m_gdn_A_log, gdn_dt_bias=m_gdn_dt_bias, gdn_norm_g=m_gdn_norm_g, mla_q_norm_g=m_mla_q_norm_g, mla_w_uq=m_mla_w_uq, mla_kv_norm_g=m_mla_kv_norm_g, mla_w_ukv=m_mla_w_ukv, ev_w_out=m_ev_w_out, ssd_w_in=m_ssd_w_in, ssd_conv_w=m_ssd_conv_w, ssd_conv_b=m_ssd_conv_b, ssd_A_log=m_ssd_A_log, ssd_dt_bias=m_ssd_dt_bias, ssd_D=m_ssd_D, ssd_norm_g=m_ssd_norm_g, ssd_w_out=m_ssd_w_out, final_g=m_final_g)
    M2 = dict(ada_w=v_ada_w, ada_b=v_ada_b, norm_g=v_norm_g, ffn_w1=v_ffn_w1, ffn_w3=v_ffn_w3, ffn_w2=v_ffn_w2, ev_w_in=v_ev_w_in, gdn_conv_w=v_gdn_conv_w, gdn_A_log=v_gdn_A_log, gdn_dt_bias=v_gdn_dt_bias, gdn_norm_g=v_gdn_norm_g, mla_q_norm_g=v_mla_q_norm_g, mla_w_uq=v_mla_w_uq, mla_kv_norm_g=v_mla_kv_norm_g, mla_w_ukv=v_mla_w_ukv, ev_w_out=v_ev_w_out, ssd_w_in=v_ssd_w_in, ssd_conv_w=v_ssd_conv_w, ssd_conv_b=v_ssd_conv_b, ssd_A_log=v_ssd_A_log, ssd_dt_bias=v_ssd_dt_bias, ssd_D=v_ssd_D, ssd_norm_g=v_ssd_norm_g, ssd_w_out=v_ssd_w_out, final_g=v_final_g)
    names = list(P)
    xi, yi, ci = lax.axis_index("x"), lax.axis_index("y"), lax.axis_index("c")
    chip = 2 * xi + yi
    bidx = 4 * xi + 2 * yi + ci
    xa = x[0]
    S_, D = xa.shape
    tgt = loss_target[0]
    depth = ffn_w1.shape[0]

    wg = allgather_chips(_pack([P[n] for n, _, _ in BIG], BF16), name="gather_weights")
    per_chip = [_unpack(wg[s]) for s in range(4)]
    W = {n: jnp.concatenate([per_chip[s][k] for s in range(4)], axis=ax) for k, (n, _, ax) in enumerate(BIG)}
    sg = allgather_chips(_flat_pack([P[n] for n, _, _ in SMALL_SHARDED], 1024, 16), name="gather_small")
    per_chip_s = [_flat_unpack(sg[s], [sh for _, sh, _ in SMALL_SHARDED]) for s in range(4)]
    Wf = {n: jnp.concatenate([per_chip_s[s][k] for s in range(4)], axis=ax) for k, (n, _, ax) in enumerate(SMALL_SHARDED)}

    c_all = allgather8(jnp.pad(c, ((0, 7), (0, 0))), name="gather_c").reshape(8, 8, D)[:, 0]
    c_act, = rowmap(lambda t: (_silu(t),), [jnp.pad(c_all, ((0, 8), (0, 0)))], [], (F32,), name="c_act", tile=16)
    ncol = ada_w.shape[2]
    ada_b_loc = lax.dynamic_slice(ada_b, (0, chip * ncol), (depth, ncol))
    mod_loc = [mm((c_act, ada_w[l]), name=f"mod_{l}", epi=lambda acc, b: (acc + b,), epi_pars=(ada_b_loc[l][None],),
                  epi_out_dtypes=(F32,), tm=16, tn=256)[0][:8] for l in range(depth)]
    mod_g = allgather8(jnp.stack(mod_loc).reshape(-1, 1024), name="gather_mod").reshape(8, depth, 8, ncol)
    mod_b = lax.dynamic_index_in_dim(mod_g[0::2], bidx, axis=2, keepdims=False)
    mod = jnp.transpose(mod_b, (1, 0, 2)).reshape(depth, 3, 3, D)

    def ev_ext(w):
        z = lambda n: jnp.zeros((w.shape[0], n), w.dtype)
        return jnp.concatenate([w[:, :2048], _rep(w[:, 2048:2052], 128), _rep(w[:, 2052:2056], 128), w[:, 2056:2440], z(128),
                                w[:, 2440:2696], w[:, 2696:2760], z(192)], axis=1)

    def ev_ext_t(dw):
        return jnp.concatenate([dw[:, :2048], dw[:, 2048:2560].reshape(-1, 4, 128).sum(-1), dw[:, 2560:3072].reshape(-1, 4, 128).sum(-1),
                                dw[:, 3072:3456], dw[:, 3584:3840], dw[:, 3840:3904]], axis=1)

    def od_ext(w):
        return jnp.concatenate([w[:, 2048:5120], w[:, :2048], _rep(w[:, 5120:5152], 64)], axis=1)

    def od_ext_t(dw):
        return jnp.concatenate([dw[:, 3072:5120], dw[:, :3072], dw[:, 5120:].reshape(-1, 32, 64).sum(-1)], axis=1)

    def wq_ext(w):
        return jnp.concatenate([w[:, :, :128].reshape(384, 512), jnp.pad(w[:, :, 128:], ((0, 0), (0, 0), (0, 64))).reshape(384, 512)], axis=1)

    def wq_ext_t(dw):
        return jnp.concatenate([dw[:, :512].reshape(384, 4, 128), dw[:, 512:].reshape(384, 4, 128)[:, :, :64]], axis=2)

    def wkv_ext(w):
        return jnp.concatenate([w[:, :, :128].reshape(256, 512), w[:, :, 128:].reshape(256, 512)], axis=1)

    def wkv_ext_t(dw):
        return jnp.concatenate([dw[:, :512].reshape(256, 4, 128), dw[:, 512:].reshape(256, 4, 128)], axis=2)

    half = 32
    inv_freq = 10000.0 ** (-jnp.arange(half, dtype=F32) / half)
    ang = positions[0].astype(F32)[:, None] * inv_freq
    zpad = jnp.zeros((S_, 64), F32)
    cos_t = jnp.concatenate([jnp.cos(ang), jnp.cos(ang), zpad], axis=1)
    sin_t = jnp.concatenate([jnp.sin(ang), jnp.sin(ang), zpad], axis=1)
    ii = jnp.arange(128)
    rot = (jnp.where((ii[:, None] < 32) & (ii[None, :] == ii[:, None] + 32), 1.0, 0.0)
           - jnp.where((ii[:, None] >= 32) & (ii[:, None] < 64) & (ii[None, :] == ii[:, None] - 32), 1.0, 0.0)).astype(F32)

    grads = {}
    dmod = [[[None] * 3 for _ in range(3)] for _ in range(depth)]
    dnorm_g = [[None] * 3 for _ in range(depth)]

    def acc(name, idx, val):
        grads.setdefault(name, {})[idx] = val

    def ffn_sub(xin, l, k, j):
        g, (shift, scale, gate) = Wf["norm_g"][l, k][None], [mod[l, k, t][None] for t in range(3)]
        w1, w3, w2 = W["ffn_w1"][l, j], W["ffn_w3"][l, j], W["ffn_w2"][l, j]
        tag = f"l{l}f{j}"
        h, = rowmap(lambda *a: (_adaln(*a),), [xin], [g, shift, scale], (BF16,), name=f"adaln_{tag}")
        a = ffn_mid_fwd(h, w1, w3, name=f"ffn_mid_{tag}")
        xn, y = mm((a, w2), name=f"ffn_out_{tag}", epi=lambda acc_, xr, gt: (xr + 0.5 * gt * acc_, acc_), epi_rows=(xin,),
                   epi_pars=(gate,), epi_out_dtypes=(F32, F32))

        def bwd(dxn):
            dy, dgate = rowmap(lambda d, y_, gt: ((0.5 * gt) * d, jnp.sum(0.5 * y_ * d, axis=0, keepdims=True)), [dxn, y], [gate],
                               (BF16, F32), name=f"dres_{tag}", n_reduce=1)
            du, dv, a_ = ffn_mid_bwd(h, dy, w1, w3, w2, name=f"ffn_midb_{tag}")
            dh = mm([(du, w1), (dv, w3)], tb=True, name=f"ffn_dh_{tag}")
            acc("ffn_w1", (l, j), mm((h, du), ta=True, name=f"ffn_dw1_{tag}"))
            acc("ffn_w3", (l, j), mm((h, dv), ta=True, name=f"ffn_dw3_{tag}"))
            acc("ffn_w2", (l, j), mm((a_, dy), ta=True, name=f"ffn_dw2_{tag}"))
            dx, dg, dsh, dsc = _vjp_rows(_adaln, 1, 3, (F32,), [xin], [dh], [g, shift, scale], name=f"adalnb_{tag}", extra=dxn)
            dnorm_g[l][k] = dg[0]
            dmod[l][k] = [dsh[0], dsc[0], dgate[0]]
            return dx

        return xn, bwd

    def mixer_tail(xin, l, tag, dh, dxn, g, shift, scale, dgate):
        dx, dg, dsh, dsc = _vjp_rows(_adaln, 1, 3, (F32,), [xin], [dh], [g, shift, scale], name=f"adalnb_{tag}", extra=dxn)
        dnorm_g[l][1] = dg[0]
        dmod[l][1] = [dsh[0], dsc[0], dgate[0]]
        return dx

    def even_sub(xin, l):
        e = l // 2
        tag = f"l{l}m"
        g, (shift, scale, gate) = Wf["norm_g"][l, 1][None], [mod[l, 1, t][None] for t in range(3)]
        wext, wq, wkv, wout = ev_ext(W["ev_w_in"][e]), wq_ext(W["mla_w_uq"][e]), wkv_ext(W["mla_w_ukv"][e]), W["ev_w_out"][e]
        conv_w, zb = Wf["gdn_conv_w"][e], jnp.zeros((1, 1536), F32)
        alog_e, dtb_e = _rep(gdn_A_log[e], 128)[None], _rep(gdn_dt_bias[e], 128)[None]
        gg, qg, kvg = gdn_norm_g[e][None], mla_q_norm_g[e][None], mla_kv_norm_g[e][None]
        h, = rowmap(lambda *a: (_adaln(*a),), [xin], [g, shift, scale], (BF16,), name=f"adaln_{tag}")
        proj = mm((h, wext), name=f"ev_in_{tag}")
        qkvc = conv_fwd(proj, 0, 3, conv_w, zb, name=f"gdn_conv_{tag}")
        o_g, hist = gdn_fwd(qkvc, proj, 4, 5, alog_e, dtb_e, name=f"gdn_{tag}")
        o_a, = rowmap(lambda o, z, g_: (_gdn_gate(o, z, g_),), [o_g, (proj, 512, 3)], [gg], (BF16,), name=f"gdn_gate_{tag}")
        cqn, = rowmap(lambda t, g_: (_rms(t, g_),), [(proj, 384, 8)], [qg], (BF16,), name=f"q_norm_{tag}")
        ckvn, = rowmap(lambda t, g_: (_rms(t, g_),), [(proj, 256, 14)], [kvg], (BF16,), name=f"kv_norm_{tag}")
        q0 = mm((cqn, wq), name=f"q_up_{tag}")
        kv = mm((ckvn, wkv), name=f"kv_up_{tag}")
        q, = rowmap(lambda t, cs, sn, r: (_rope_q(t, cs, sn, r),), [q0, cos_t, sin_t], [rot], (F32,), name=f"rope_q_{tag}")
        kp, = rowmap(lambda t, cs, sn, r: (_rope(t, cs, sn, r),), [(proj, 128, 30), cos_t, sin_t], [rot], (F32,), name=f"rope_k_{tag}")
        o_b, lse = att_fwd(q, kv, kp, name=f"att_{tag}")
        xn, y = mm([(o_a, wout[:512]), (o_b, wout[512:])], name=f"ev_out_{tag}", epi=lambda acc_, xr, gt: (xr + gt * acc_, acc_),
                   epi_rows=(xin,), epi_pars=(gate,), epi_out_dtypes=(F32, F32))

        def bwd(dxn):
            dy, dgate = rowmap(lambda d, y_, gt: (gt * d, jnp.sum(y_ * d, axis=0, keepdims=True)), [dxn, y], [gate],
                               (BF16, F32), name=f"dres_{tag}", n_reduce=1)
            do_a = mm((dy, wout[:512]), tb=True, name=f"ev_doa_{tag}")
            do_b = mm((dy, wout[512:]), tb=True, name=f"ev_dob_{tag}")
            acc("ev_w_out", e, jnp.concatenate([mm((o_a, dy), ta=True, name=f"ev_dwoa_{tag}"), mm((o_b, dy), ta=True, name=f"ev_dwob_{tag}")], axis=0))
            dqn, dqp, dsum = att_bwd_dq(q, kv, kp, o_b, lse, do_b, name=f"att_dq_{tag}")
            dkn, dv, dkp = att_bwd_dkv(q, kv, kp, lse, dsum, do_b, name=f"att_dkv_{tag}")
            dqpe, = rowmap(lambda d, cs, sn, r: (jnp.concatenate([_rope_t(d[:, 128 * hh:128 * (hh + 1)], cs, sn, r) for hh in range(4)], axis=1),),
                           [dqp, cos_t, sin_t], [rot], (F32,), name=f"rope_qb_{tag}")
            dkr, = rowmap(lambda d, cs, sn, r: (_rope_t(d, cs, sn, r),), [dkp, cos_t, sin_t], [rot], (BF16,), name=f"rope_kb_{tag}")
            dcqn = mm([(dqn, wq[:, :512]), (dqpe, wq[:, 512:])], tb=True, name=f"q_upb_{tag}")
            acc("mla_w_uq", e, wq_ext_t(jnp.concatenate([mm((cqn, dqn), ta=True, name=f"q_dwn_{tag}"), mm((cqn, dqpe), ta=True, name=f"q_dwp_{tag}")], axis=1)))
            dckvn = mm([(dkn, wkv[:, :512]), (dv, wkv[:, 512:])], tb=True, name=f"kv_upb_{tag}")
            acc("mla_w_ukv", e, wkv_ext_t(jnp.concatenate([mm((ckvn, dkn), ta=True, name=f"kv_dwk_{tag}"), mm((ckvn, dv), ta=True, name=f"kv_dwv_{tag}")], axis=1)))
            dcq, dqg = _vjp_rows(_rms, 1, 1, (BF16,), [(proj, 384, 8)], [dcqn], [qg], name=f"q_normb_{tag}")
            dckv, dkvg = _vjp_rows(_rms, 1, 1, (BF16,), [(proj, 256, 14)], [dckvn], [kvg], name=f"kv_normb_{tag}")
            acc("mla_q_norm_g", e, dqg[0])
            acc("mla_kv_norm_g", e, dkvg[0])
            do_g, dz, dgg = _vjp_rows(_gdn_gate, 2, 1, (F32, BF16), [o_g, (proj, 512, 3)], [do_a], [gg], name=f"gdn_gateb_{tag}")
            acc("gdn_norm_g", e, dgg[0])
            dqkvc, dbe, dae, dal, ddt = gdn_bwd(qkvc, proj, 4, 5, alog_e, dtb_e, hist, do_g, name=f"gdnb_{tag}")
            acc("gdn_A_log", e, dal.reshape(4, 128).sum(-1))
            acc("gdn_dt_bias", e, ddt.reshape(4, 128).sum(-1))
            dpre, dcw, _ = conv_bwd_pre(proj, 0, 3, conv_w, zb, dqkvc, name=f"gdn_convb_{tag}")
            acc("gdn_conv_w", e, dcw)
            dqkv = conv_bwd_x(dpre, conv_w, name=f"gdn_convx_{tag}", out_dtype=BF16)
            zc = lambda n: jnp.zeros((S_, n), BF16)
            dproj = jnp.concatenate([dqkv, dz, dbe.astype(BF16), dae.astype(BF16), dcq, zc(128), dckv, dkr, zc(128)], axis=1)
            dh = mm((dproj, wext), tb=True, name=f"ev_inb_{tag}")
            acc("ev_w_in", e, ev_ext_t(mm((h, dproj), ta=True, name=f"ev_dwin_{tag}")))
            return mixer_tail(xin, l, tag, dh, dxn, g, shift, scale, dgate)

        return xn, bwd

    def odd_sub(xin, l):
        o = l // 2
        tag = f"l{l}m"
        g, (shift, scale, gate) = Wf["norm_g"][l, 1][None], [mod[l, 1, t][None] for t in range(3)]
        wext, wout = od_ext(W["ssd_w_in"][o]), W["ssd_w_out"][o]
        conv_w, conv_b, ng = Wf["ssd_conv_w"][o], Wf["ssd_conv_b"][o][None], Wf["ssd_norm_g"][o][None]
        ex = lambda v: _rep(v, 64).reshape(4, 1, 512)
        na_e, dtb_e, dsk_e = ex(-jnp.exp(ssd_A_log[o])), ex(ssd_dt_bias[o]), ex(ssd_D[o])
        h, = rowmap(lambda *a: (_adaln(*a),), [xin], [g, shift, scale], (BF16,), name=f"adaln_{tag}")
        proj = mm((h, wext), name=f"ssd_in_{tag}")
        zv = [(proj, 512, 6 + t) for t in range(4)]
        xbc = conv_fwd(proj, 0, 6, conv_w, conv_b, name=f"ssd_conv_{tag}")
        ys, hist = ssd_fwd(xbc, proj, na_e, dtb_e, dsk_e, name=f"ssd_{tag}")
        yn, = rowmap(lambda *a: (_ssd_gate(*a),), [ys] + zv, [ng], (BF16,), name=f"ssd_gate_{tag}", tile=128)
        xn, y = mm((yn, wout), name=f"ssd_out_{tag}", epi=lambda acc_, xr, gt: (xr + gt * acc_, acc_), epi_rows=(xin,),
                   epi_pars=(gate,), epi_out_dtypes=(F32, F32))

        def bwd(dxn):
            dy, dgate = rowmap(lambda d, y_, gt: (gt * d, jnp.sum(y_ * d, axis=0, keepdims=True)), [dxn, y], [gate],
                               (BF16, F32), name=f"dres_{tag}", n_reduce=1)
            dyn = mm((dy, wout), tb=True, name=f"ssd_dyn_{tag}", out_dtype=BF16)
            acc("ssd_w_out", o, mm((yn, dy), ta=True, name=f"ssd_dwout_{tag}"))
            dys, dz0, dz1, dz2, dz3, dng = _vjp_rows(_ssd_gate, 5, 1, (F32, BF16, BF16, BF16, BF16), [ys] + zv, [dyn], [ng],
                                                     name=f"ssd_gateb_{tag}", tile=128)
            acc("ssd_norm_g", o, dng[0])
            dxs, ddtx, db_, dc_, dna, ddtb, ddsk = ssd_bwd(xbc, proj, na_e, dtb_e, dsk_e, hist, dys, name=f"ssdb_{tag}")
            acc("ssd_A_log", o, dna.reshape(32, 64).sum(-1) * (-jnp.exp(ssd_A_log[o])))
            acc("ssd_dt_bias", o, ddtb.reshape(32, 64).sum(-1))
            acc("ssd_D", o, ddsk.reshape(32, 64).sum(-1))
            dxbc = jnp.concatenate([dxs, db_, dc_], axis=1)
            dpre, dcw, dcb = conv_bwd_pre(proj, 0, 6, conv_w, conv_b, dxbc, name=f"ssd_convb_{tag}")
            acc("ssd_conv_w", o, dcw)
            acc("ssd_conv_b", o, dcb[0])
            dxp = conv_bwd_x(dpre, conv_w, name=f"ssd_convx_{tag}", out_dtype=BF16)
            dproj = jnp.concatenate([dxp, dz0, dz1, dz2, dz3, ddtx.astype(BF16)], axis=1)
            dh = mm((dproj, wext), tb=True, name=f"ssd_inb_{tag}")
            acc("ssd_w_in", o, od_ext_t(mm((h, dproj), ta=True, name=f"ssd_dwin_{tag}")))
            return mixer_tail(xin, l, tag, dh, dxn, g, shift, scale, dgate)

        return xn, bwd

    tape = []
    xc = xa
    for l in range(depth):
        xc, b0 = ffn_sub(xc, l, 0, 0)
        xc, b1 = (even_sub if l % 2 == 0 else odd_sub)(xc, l)
        xc, b2 = ffn_sub(xc, l, 2, 1)
        tape += [b0, b1, b2]

    def head(xr, tg, g_):
        def f(xv, gv):
            err = _rms(xv, gv) - tg
            return 0.5 * jnp.sum(jnp.mean(err * err, axis=-1, keepdims=True), axis=0, keepdims=True)
        lo, vjp = jax.vjp(f, xr, g_)
        dxv, dgv = vjp(jnp.ones_like(lo))
        return dxv, jnp.broadcast_to(lo, (1, 128)), dgv

    dx, loss_p, dfg = rowmap(head, [xc, tgt], [final_g[None]], (F32,), name="loss_head", n_reduce=2)
    loss = lax.psum(loss_p[0, 0], ("x", "y", "c"))

    for b in reversed(tape):
        dx = b(dx)
    grad_x = dx[None]

    full = {n: jnp.stack([grads[n][k] for k in sorted(grads[n])]) for n in ("ev_w_in", "mla_w_uq", "mla_w_ukv", "ev_w_out", "ssd_w_in", "ssd_w_out")}
    for n in ("ffn_w1", "ffn_w3", "ffn_w2"):
        full[n] = jnp.stack([jnp.stack([grads[n][(l, j)] for j in range(2)]) for l in range(depth)])
    chunks = []
    for s in range(4):
        chunks.append(_pack([lax.slice_in_dim(full[n], s * sh[ax], (s + 1) * sh[ax], axis=ax) for n, sh, ax in BIG], F32))
    recv = exchange_chips(jnp.stack(chunks), name="exchange_grads")
    part = sum_slots(recv, name="sum_chips")
    sib = swap_sibling(part, name="swap_sibling")
    g_mine, g_sib = _unpack(part), _unpack(sib)

    dmod_flat = jnp.stack([jnp.stack([jnp.stack(dmod[l][k]) for k in range(3)]) for l in range(depth)]).reshape(depth, 9 * D)
    small_names = ["norm_g", "gdn_conv_w", "gdn_A_log", "gdn_dt_bias", "gdn_norm_g", "mla_q_norm_g", "mla_kv_norm_g",
                   "ssd_conv_w", "ssd_conv_b", "ssd_A_log", "ssd_dt_bias", "ssd_D", "ssd_norm_g", "final_g"]
    small_full = {n: jnp.stack([grads[n][k] for k in sorted(grads[n])]) for n in small_names if n in grads}
    small_full["norm_g"] = jnp.stack([jnp.stack(dnorm_g[l]) for l in range(depth)])
    small_full["final_g"] = dfg[0]
    small_list = [dmod_flat] + [small_full[n] for n in small_names]
    small_shapes = [a.shape for a in small_list]
    sp = _flat_pack(small_list, 128, 8)
    sgath = allgather8(sp, name="gather_small_grads").reshape(8, sp.shape[0], 128)
    ssum = sum_slots(sgath, name="sum_small")
    tot = dict(zip(["ada_b"] + small_names, _flat_unpack(ssum, small_shapes)))
    dmod_all = sgath.reshape(8, -1)[:, :depth * 9 * D].reshape(8, depth, 9 * D)
    dmod_loc = lax.dynamic_slice(dmod_all, (0, 0, chip * ncol), (8, depth, ncol))
    g_ada_w = jnp.stack([mm((c_act, jnp.pad(dmod_loc[:, l], ((0, 8), (0, 0)))), ta=True, name=f"ada_dw_{l}", tk=16, tn=256)
                         for l in range(depth)])

    def own(n, a):
        for m_, sh, ax in SMALL_SHARDED:
            if m_ == n:
                return lax.dynamic_slice_in_dim(a, chip * sh[ax], sh[ax], axis=ax)
        return a

    res = {}
    for k, (n, _, _) in enumerate(BIG):
        res[n] = adamw(P[n], [g_mine[k], g_sib[k]], M1[n], M2[n], name=f"adamw_{n}")
    res["ada_w"] = adamw(ada_w, [g_ada_w], m_ada_w, v_ada_w, name="adamw_ada_w")
    sm = ["ada_b"] + small_names
    shapes = [P[n].shape for n in sm]
    pk = lambda d: _flat_pack([d[n] for n in sm], 128, 8)
    outs = adamw(pk(P), [pk({n: own(n, tot[n]).reshape(P[n].shape) for n in sm})], pk(M1), pk(M2), name="adamw_small")
    un = [_flat_unpack(o, shapes) for o in outs]
    for i, n in enumerate(sm):
        res[n] = tuple(un[t][i] for t in range(4))
    return (loss, grad_x, *[res[n][0] for n in names], *[res[n][1] for n in names], *[res[n][2] for n in names], *[res[n][3] for n in names])
```

```python
import functools
import math

import jax
import jax.numpy as jnp
from jax import lax
from jax.experimental import pallas as pl
from jax.experimental.pallas import tpu as pltpu

F32 = jnp.float32
BF16 = jnp.bfloat16
HI = lax.Precision.HIGHEST
HI3 = lax.Precision.HIGH
VMEM_LIMIT = 56 * 1024 * 1024
NORM_EPS = 1e-6
MM_VMEM_BUDGET = 40 * 1024 * 1024


def _cp(sem=None):
    if sem is None:
        return pltpu.CompilerParams(vmem_limit_bytes=VMEM_LIMIT)
    return pltpu.CompilerParams(dimension_semantics=sem, vmem_limit_bytes=VMEM_LIMIT)


def _pick(dim, prefs):
    for p in prefs:
        if dim % p == 0:
            return p
    return dim


def mm(pairs, *, ta=False, tb=False, out_dtype=F32, name, epi=None, epi_rows=(), epi_pars=(), epi_out_dtypes=None,
       tm=None, tn=None, tk=None):
    if not isinstance(pairs, (list, tuple)) or not isinstance(pairs[0], (list, tuple)):
        pairs = [pairs]
    npair = len(pairs)
    a0, b0 = pairs[0]
    M = a0.shape[1] if ta else a0.shape[0]
    K = a0.shape[0] if ta else a0.shape[1]
    N = b0.shape[0] if tb else b0.shape[1]
    for a, b in pairs:
        assert (a.shape == ((K, M) if ta else (M, K))), (a.shape, M, K)
        assert (b.shape == ((N, K) if tb else (K, N))), (b.shape, K, N)
    tm = tm or _pick(M, (1024, 1408, 512, 384, 256, 128))
    tk = tk or (K if K <= 1024 else _pick(K, (1024, 1408, 512, 256, 128)))
    if tn is None:
        n_epi_out = 1 if epi is None else len(epi_out_dtypes)
        for tn in (1024, 1408, 512, 384, 256, 128, N):
            if N % tn:
                continue
            need = sum(2 * tk * (tm * a.dtype.itemsize + tn * b.dtype.itemsize) for a, b in pairs)
            need += tm * tn * 4 * (1 + 2 * n_epi_out + 2 * len(epi_rows))
            if need <= MM_VMEM_BUDGET:
                break
    nk = K // tk
    assert M % tm == 0 and N % tn == 0 and K % tk == 0, (M, N, K, tm, tn, tk)
    n_rows, n_pars = len(epi_rows), len(epi_pars)
    if epi is None:
        out_dtypes = (out_dtype,)
    else:
        out_dtypes = tuple(epi_out_dtypes)
    n_out = len(out_dtypes)
    dn = (((0 if ta else 1,), (1 if tb else 0,)), ((), ()))

    def body(*refs):
        ab = refs[:2 * npair]
        rows = refs[2 * npair:2 * npair + n_rows]
        pars = refs[2 * npair + n_rows:2 * npair + n_rows + n_pars]
        outs = refs[2 * npair + n_rows + n_pars:2 * npair + n_rows + n_pars + n_out]
        acc_ref = refs[-1]
        k = pl.program_id(2)

        @pl.when(k == 0)
        def _():
            acc_ref[...] = jnp.zeros_like(acc_ref)

        acc = acc_ref[...]
        for p in range(npair):
            a = ab[2 * p][...].astype(BF16)
            b = ab[2 * p + 1][...].astype(BF16)
            acc = acc + lax.dot_general(a, b, dn, preferred_element_type=F32)
        acc_ref[...] = acc

        @pl.when(k == nk - 1)
        def _():
            r = acc_ref[...]
            if epi is None:
                outs[0][...] = r.astype(outs[0].dtype)
            else:
                res = epi(r, *[x[...] for x in rows], *[x[...] for x in pars])
                for o, v in zip(outs, res):
                    o[...] = v.astype(o.dtype)

    a_spec = pl.BlockSpec((tk, tm), lambda i, j, k: (k, i)) if ta else pl.BlockSpec((tm, tk), lambda i, j, k: (i, k))
    b_spec = pl.BlockSpec((tn, tk), lambda i, j, k: (j, k)) if tb else pl.BlockSpec((tk, tn), lambda i, j, k: (k, j))
    in_specs = []
    args = []
    for a, b in pairs:
        in_specs += [a_spec, b_spec]
        args += [a, b]
    for r in epi_rows:
        in_specs.append(pl.BlockSpec((tm, tn), lambda i, j, k: (i, j)))
        args.append(r)
    for p_ in epi_pars:
        in_specs.append(pl.BlockSpec((1, tn), lambda i, j, k: (0, j)))
        args.append(p_)
    out_specs = [pl.BlockSpec((tm, tn), lambda i, j, k: (i, j)) for _ in range(n_out)]
    out_shape = [jax.ShapeDtypeStruct((M, N), d) for d in out_dtypes]
    res = pl.pallas_call(
        body, grid=(M // tm, N // tn, nk), in_specs=in_specs, out_specs=out_specs, out_shape=out_shape,
        scratch_shapes=[pltpu.VMEM((tm, tn), F32)], compiler_params=_cp(("parallel", "parallel", "arbitrary")), name=name,
    )(*args)
    return res[0] if epi is None else tuple(res)


def rowmap(fn, rows, pars, out_dtypes, *, name, tile=256, n_reduce=0):
    views = []
    for r in rows:
        if isinstance(r, tuple):
            views.append(r)
        else:
            views.append((r, r.shape[1], 0))
    S = views[0][0].shape[0]
    assert S % tile == 0
    nt = S // tile
    row_structs = [jax.ShapeDtypeStruct((tile, w), a.dtype) for a, w, _ in views]
    par_structs = [jax.ShapeDtypeStruct(p.shape, p.dtype) for p in pars]
    out_structs = jax.eval_shape(fn, *row_structs, *par_structs)
    n_out = len(out_structs)
    n_row_out = n_out - n_reduce
    nr, npar = len(views), len(pars)

    def body(*refs):
        ins = [x[...] for x in refs[:nr + npar]]
        outs = refs[nr + npar:]
        res = fn(*ins)
        for o, v in zip(outs[:n_row_out], res[:n_row_out]):
            o[...] = v.astype(o.dtype)
        if n_reduce:
            i = pl.program_id(0)

            @pl.when(i == 0)
            def _():
                for o, v in zip(outs[n_row_out:], res[n_row_out:]):
                    o[...] = v.astype(o.dtype)

            @pl.when(i > 0)
            def _():
                for o, v in zip(outs[n_row_out:], res[n_row_out:]):
                    o[...] += v.astype(o.dtype)

    in_specs = [pl.BlockSpec((tile, w), functools.partial(lambda i, c: (i, c), c=c)) for _, w, c in views]
    in_specs += [pl.BlockSpec(p.shape, lambda i: (0, 0)) for p in pars]
    out_specs = [pl.BlockSpec((tile, s.shape[1]), lambda i: (i, 0)) for s in out_structs[:n_row_out]]
    out_specs += [pl.BlockSpec(s.shape, lambda i: (0, 0)) for s in out_structs[n_row_out:]]
    out_shape = [jax.ShapeDtypeStruct((S, s.shape[1]), d) for s, d in zip(out_structs[:n_row_out], out_dtypes[:n_row_out])]
    out_shape += [jax.ShapeDtypeStruct(s.shape, F32) for s in out_structs[n_row_out:]]
    res = pl.pallas_call(
        body, grid=(nt,), in_specs=in_specs, out_specs=out_specs, out_shape=out_shape,
        compiler_params=_cp(("arbitrary",) if n_reduce else ("parallel",)), name=name,
    )(*[v[0] for v in views], *pars)
    return tuple(res)


CH = 64


def _softplus(x):
    return jnp.where(x > 20.0, x, jnp.log(1.0 + jnp.exp(jnp.minimum(x, 20.0))))


def _dot(a, b, dn=(((1,), (0,)), ((), ())), hi=False):
    if hi:
        return lax.dot_general(a.astype(F32), b.astype(F32), dn, precision=HI if hi is True else hi, preferred_element_type=F32)
    return lax.dot_general(a.astype(BF16), b.astype(BF16), dn, preferred_element_type=F32)


_NT = (((1,), (1,)), ((), ()))
_TN = (((0,), (0,)), ((), ()))


def _chunk_consts():
    r = lax.broadcasted_iota(jnp.int32, (CH, 2 * CH), 0)
    c0 = lax.broadcasted_iota(jnp.int32, (CH, 2 * CH), 1)
    c = jnp.where(c0 >= CH, c0 - CH, c0)
    r1 = lax.broadcasted_iota(jnp.int32, (CH, CH), 0)
    c1 = lax.broadcasted_iota(jnp.int32, (CH, CH), 1)
    return dict(
        lower2=r >= c, strict2=r > c, U2=(r <= c).astype(F32), eye2=(r == c).astype(F32),
        L=(r1 >= c1).astype(F32), ones=jnp.ones((CH, CH), F32), Z=jnp.zeros((CH, 2 * CH), F32))


@jax.custom_vjp
def _tri_inv2(a2, eye2, Z):
    def prod(x2, y):
        return _dot(x2, jnp.concatenate([y, Z], axis=0), hi=HI3)

    b = -a2
    t2 = eye2 + b
    for _ in range(5):
        b = prod(b, b)
        t2 = t2 + prod(t2, b)
    return t2


def _tri_inv2_fwd(a2, eye2, Z):
    t2 = _tri_inv2(a2, eye2, Z)
    return t2, (t2, eye2, Z)


def _tri_inv2_bwd(res, dt2):
    t2, eye2, Z = res
    x2 = _dot(t2, dt2, _TN, hi=HI3)[:CH]
    da2 = -_dot(x2, jnp.concatenate([t2, Z], axis=0), _NT, hi=HI3)
    return da2, jnp.zeros_like(eye2), jnp.zeros_like(Z)


_tri_inv2.defvjp(_tri_inv2_fwd, _tri_inv2_bwd)


def _gdn_head(q, k, v, bx, ax, S, alog, dtb, cst):
    lower2, strict2, U2, eye2, L, ones, Z = (cst[n] for n in ("lower2", "strict2", "U2", "eye2", "L", "ones", "Z"))
    qn = q * lax.rsqrt(jnp.sum(q * q, axis=-1, keepdims=True) + NORM_EPS) * (128.0 ** -0.5)
    kn = k * lax.rsqrt(jnp.sum(k * k, axis=-1, keepdims=True) + NORM_EPS)
    beta = jax.nn.sigmoid(bx)
    g = -jnp.exp(alog) * _softplus(ax + dtb)
    gc = _dot(L, g, hi=True)
    n2 = _dot(ones, g * U2, hi=True)
    d2 = gc - n2
    decay2 = jnp.where(lower2, jnp.exp(jnp.where(lower2, d2, 0.0)), 0.0)
    kb = kn * beta
    kn2 = jnp.concatenate([kn, kn], axis=0)
    a2 = jnp.where(strict2, _dot(kb, kn2, _NT) * decay2, 0.0)

    def prod(x2, y):
        return _dot(x2, jnp.concatenate([y, Z], axis=0), hi=HI3)

    t2 = _tri_inv2(a2, eye2, Z)
    glast = jnp.sum(g, axis=0, keepdims=True)
    u = prod(t2, v * beta)
    w = prod(t2, kb * jnp.exp(gc))
    attn2 = jnp.where(lower2, _dot(qn, kn2, _NT) * decay2, 0.0)
    k_end = kn * jnp.exp(glast - gc)
    q_start = qn * jnp.exp(gc)
    v_new = u - _dot(w, S)
    o = _dot(q_start, S) + _dot(attn2, jnp.concatenate([v_new, Z], axis=0))
    s_new = S * jnp.exp(glast) + _dot(k_end, v_new, _TN)
    return o, s_new


def gdn_fwd(qkv, proj, bcol, acol, alog_e, dtb_e, *, name):
    S_ = qkv.shape[0]
    nc = S_ // CH

    def body(q_ref, k_ref, v_ref, b_ref, a_ref, al_ref, dt_ref, o_ref, hist_ref, s_ref):
        i = pl.program_id(0)

        @pl.when(i == 0)
        def _():
            s_ref[...] = jnp.zeros_like(s_ref)

        cst = _chunk_consts()
        hist_ref[0] = s_ref[...]
        heads = [slice(128 * h, 128 * (h + 1)) for h in range(4)]
        res = [_gdn_head(q_ref[:, ls], k_ref[:, ls], v_ref[:, ls], b_ref[:, ls], a_ref[:, ls], s_ref[ls, :],
                         al_ref[:, ls], dt_ref[:, ls], cst) for ls in heads]
        for ls, (o, s_new) in zip(heads, res):
            o_ref[:, ls] = o
            s_ref[ls, :] = s_new

    blk = lambda cb: pl.BlockSpec((CH, 512), functools.partial(lambda i, cb: (i, cb), cb=cb))
    par = pl.BlockSpec((1, 512), lambda i: (0, 0))
    return pl.pallas_call(
        body, grid=(nc,), in_specs=[blk(0), blk(1), blk(2), blk(bcol), blk(acol), par, par],
        out_specs=[pl.BlockSpec((CH, 512), lambda i: (i, 0)), pl.BlockSpec((1, 512, 128), lambda i: (i, 0, 0))],
        out_shape=[jax.ShapeDtypeStruct((S_, 512), F32), jax.ShapeDtypeStruct((nc, 512, 128), F32)],
        scratch_shapes=[pltpu.VMEM((512, 128), F32)], compiler_params=_cp(("arbitrary",)), name=name,
    )(qkv, qkv, qkv, proj, proj, alog_e, dtb_e)


def gdn_bwd(qkv, proj, bcol, acol, alog_e, dtb_e, hist, do, *, name):
    S_ = qkv.shape[0]
    nc = S_ // CH

    def body(q_ref, k_ref, v_ref, b_ref, a_ref, al_ref, dt_ref, hist_ref, do_ref, dqkv_ref, db_ref, da_ref, dal_ref, ddt_ref, ds_ref):
        i = pl.program_id(0)

        @pl.when(i == 0)
        def _():
            ds_ref[...] = jnp.zeros_like(ds_ref)
            dal_ref[...] = jnp.zeros_like(dal_ref)
            ddt_ref[...] = jnp.zeros_like(ddt_ref)

        cst = _chunk_consts()
        fn = functools.partial(_gdn_head, cst=cst)
        res = []
        for h in range(4):
            ls = slice(128 * h, 128 * (h + 1))
            _, vjp = jax.vjp(fn, q_ref[:, ls], k_ref[:, ls], v_ref[:, ls], b_ref[:, ls], a_ref[:, ls], hist_ref[0, ls, :],
                             al_ref[:, ls], dt_ref[:, ls])
            res.append(vjp((do_ref[:, ls], ds_ref[ls, :])))
        for h in range(4):
            ls = slice(128 * h, 128 * (h + 1))
            dq, dk, dv, db, da, ds_in, dal, ddt = res[h]
            dqkv_ref[:, 128 * h:128 * (h + 1)] = dq
            dqkv_ref[:, 512 + 128 * h:512 + 128 * (h + 1)] = dk
            dqkv_ref[:, 1024 + 128 * h:1024 + 128 * (h + 1)] = dv
            db_ref[:, ls] = db
            da_ref[:, ls] = da
            ds_ref[ls, :] = ds_in
            dal_ref[:, ls] += dal
            ddt_ref[:, ls] += ddt

    rblk = lambda cb: pl.BlockSpec((CH, 512), functools.partial(lambda i, cb: (nc - 1 - i, cb), cb=cb))
    par = pl.BlockSpec((1, 512), lambda i: (0, 0))
    return pl.pallas_call(
        body, grid=(nc,),
        in_specs=[rblk(0), rblk(1), rblk(2), rblk(bcol), rblk(acol), par, par,
                  pl.BlockSpec((1, 512, 128), lambda i: (nc - 1 - i, 0, 0)), rblk(0)],
        out_specs=[pl.BlockSpec((CH, 1536), lambda i: (nc - 1 - i, 0)), rblk(0), rblk(0), par, par],
        out_shape=[jax.ShapeDtypeStruct((S_, 1536), F32), jax.ShapeDtypeStruct((S_, 512), F32), jax.ShapeDtypeStruct((S_, 512), F32),
                   jax.ShapeDtypeStruct((1, 512), F32), jax.ShapeDtypeStruct((1, 512), F32)],
        scratch_shapes=[pltpu.VMEM((512, 128), F32)], compiler_params=_cp(("arbitrary",)), name=name,
    )(qkv, qkv, qkv, proj, proj, alog_e, dtb_e, hist, do)


def _ssd_group(xs, dtxs, bm, cm, hss, nas, dtbs, dsks, cst):
    lower2, U2, L, ones = cst["lower2"], cst["U2"], cst["L"], cst["ones"]
    lane = lax.broadcasted_iota(jnp.int32, (1, 2 * CH), 1)
    mask_l = (lane < CH).astype(F32)
    mask_r = 1.0 - mask_l
    ones_w = jnp.ones((CH, 2 * CH), F32)
    cb2 = _dot(cm, jnp.concatenate([bm, bm], axis=0), _NT)
    ys, hs_new = [], []
    for x, dtx, hs, na, dtb, dsk in zip(xs, dtxs, hss, nas, dtbs, dsks):
        dt = _softplus(dtx + dtb)
        da = dt * na
        m = _dot(L, da, hi=True)
        n2 = _dot(ones, da * U2, hi=True)
        lm2 = jnp.where(lower2, jnp.exp(jnp.where(lower2, m - n2, 0.0)), 0.0)
        xdt = x * dt
        x2 = jnp.concatenate([xdt * mask_l, xdt * mask_r], axis=0)
        y_diag = _dot(cb2 * lm2, x2)
        alast = jnp.sum(da, axis=0, keepdims=True)
        y_off = _dot(cm, hs, _NT) * jnp.exp(m)
        cd = jnp.exp(_dot(da, ones_w, _TN, hi=True))
        hs_new.append(hs * cd + _dot(xdt * jnp.exp(alast - m), bm, _TN))
        ys.append(y_diag + y_off + dsk * x)
    return tuple(ys), tuple(hs_new)


def _ssd_specs(nc, rev):
    ci = (lambda i: nc - 1 - i) if rev else (lambda i: i)
    xg = pl.BlockSpec((CH, 512), lambda i, g: (ci(i), g))
    dtg = pl.BlockSpec((CH, 512), lambda i, g: (ci(i), 10 + g))
    bg = pl.BlockSpec((CH, 128), lambda i, g: (ci(i), 16 + g))
    cg = pl.BlockSpec((CH, 128), lambda i, g: (ci(i), 20 + g))
    par = pl.BlockSpec((4, 1, 512), lambda i, g: (0, 0, 0))
    hist = pl.BlockSpec((1, 512, 128), lambda i, g: (ci(i), g, 0))
    return xg, dtg, bg, cg, par, hist


def ssd_fwd(xbc, proj, na_e, dtb_e, dsk_e, *, name):
    S_ = xbc.shape[0]
    nc = S_ // CH
    xg, dtg, bg, cg, par, hist = _ssd_specs(nc, False)

    def body(x_ref, dt_ref, b_ref, c_ref, na_ref, dtb_ref, dsk_ref, y_ref, hist_ref, s_ref):
        i, g = pl.program_id(0), pl.program_id(1)

        @pl.when(i == 0)
        def _():
            s_ref[g] = jnp.zeros((512, 128), F32)

        cst = _chunk_consts()
        hist_ref[0] = s_ref[g]
        sl = [slice(128 * p, 128 * (p + 1)) for p in range(4)]
        na, dtb, dsk = na_ref[g], dtb_ref[g], dsk_ref[g]
        ys, hs_new = _ssd_group([x_ref[:, s] for s in sl], [dt_ref[:, s] for s in sl], b_ref[...], c_ref[...],
                                [s_ref[g, s, :] for s in sl], [na[:, s] for s in sl], [dtb[:, s] for s in sl],
                                [dsk[:, s] for s in sl], cst)
        for p, s in enumerate(sl):
            y_ref[:, s] = ys[p]
            s_ref[g, s, :] = hs_new[p]

    return pl.pallas_call(
        body, grid=(nc, 4), in_specs=[xg, dtg, bg, cg, par, par, par],
        out_specs=[pl.BlockSpec((CH, 512), lambda i, g: (i, g)), hist],
        out_shape=[jax.ShapeDtypeStruct((S_, 2048), F32), jax.ShapeDtypeStruct((nc, 2048, 128), F32)],
        scratch_shapes=[pltpu.VMEM((4, 512, 128), F32)], compiler_params=_cp(("arbitrary", "arbitrary")), name=name,
    )(xbc, proj, xbc, xbc, na_e, dtb_e, dsk_e)


def ssd_bwd(xbc, proj, na_e, dtb_e, dsk_e, hist, dy, *, name):
    S_ = xbc.shape[0]
    nc = S_ // CH
    xg, dtg, bg, cg, par, hist_spec = _ssd_specs(nc, True)
    og = pl.BlockSpec((CH, 512), lambda i, g: (nc - 1 - i, g))
    o128 = pl.BlockSpec((CH, 128), lambda i, g: (nc - 1 - i, g))

    def body(x_ref, dt_ref, b_ref, c_ref, na_ref, dtb_ref, dsk_ref, hist_ref, dy_ref,
             dx_ref, ddt_ref, db_ref, dc_ref, dna_ref, ddtb_ref, ddsk_ref, ds_ref):
        i, g = pl.program_id(0), pl.program_id(1)

        @pl.when(i == 0)
        def _():
            ds_ref[g] = jnp.zeros((512, 128), F32)
            dna_ref[g] = jnp.zeros((1, 512), F32)
            ddtb_ref[g] = jnp.zeros((1, 512), F32)
            ddsk_ref[g] = jnp.zeros((1, 512), F32)

        cst = _chunk_consts()
        sl = [slice(128 * p, 128 * (p + 1)) for p in range(4)]
        na, dtb, dsk = na_ref[g], dtb_ref[g], dsk_ref[g]
        fn = functools.partial(_ssd_group, cst=cst)
        _, vjp = jax.vjp(fn, [x_ref[:, s] for s in sl], [dt_ref[:, s] for s in sl], b_ref[...], c_ref[...],
                         [hist_ref[0, s, :] for s in sl], [na[:, s] for s in sl], [dtb[:, s] for s in sl], [dsk[:, s] for s in sl])
        dxs, ddts, db, dc, dhs, dnas, ddtbs, ddsks = vjp((tuple(dy_ref[:, s] for s in sl), tuple(ds_ref[g, s, :] for s in sl)))
        db_ref[...] = db
        dc_ref[...] = dc
        for p, s in enumerate(sl):
            dx_ref[:, s] = dxs[p]
            ddt_ref[:, s] = ddts[p]
            ds_ref[g, s, :] = dhs[p]
            dna_ref[g, :, s] += dnas[p]
            ddtb_ref[g, :, s] += ddtbs[p]
            ddsk_ref[g, :, s] += ddsks[p]

    return pl.pallas_call(
        body, grid=(nc, 4), in_specs=[xg, dtg, bg, cg, par, par, par, hist_spec, og],
        out_specs=[og, og, o128, o128, par, par, par],
        out_shape=[jax.ShapeDtypeStruct((S_, 2048), F32), jax.ShapeDtypeStruct((S_, 2048), F32),
                   jax.ShapeDtypeStruct((S_, 512), F32), jax.ShapeDtypeStruct((S_, 512), F32)] +
                  [jax.ShapeDtypeStruct((4, 1, 512), F32)] * 3,
        scratch_shapes=[pltpu.VMEM((4, 512, 128), F32)], compiler_params=_cp(("arbitrary", "arbitrary")), name=name,
    )(xbc, proj, xbc, xbc, na_e, dtb_e, dsk_e, hist, dy)


ATT_T = 1024
ATT_SCALE = 192.0 ** -0.5
NEG = -1e30


def _att_scores(qn, qp, kn, kp, diag):
    s = (_dot(qn, kn, _NT) + _dot(qp, kp, _NT)) * ATT_SCALE
    if not diag:
        return s, None
    qc = lax.broadcasted_iota(jnp.int32, s.shape, 0) // CH
    kc = lax.broadcasted_iota(jnp.int32, s.shape, 1) // CH
    return s, kc <= qc


def att_fwd(q, kv, kp, *, name):
    S_ = q.shape[0]
    T = min(ATT_T, S_)
    n = S_ // T

    def body(qn_ref, qp_ref, kn_ref, kp_ref, v_ref, o_ref, lse_ref, m_ref, l_ref, acc_ref):
        i, j = pl.program_id(1), pl.program_id(2)

        @pl.when(j == 0)
        def _():
            m_ref[...] = jnp.full_like(m_ref, NEG)
            l_ref[...] = jnp.zeros_like(l_ref)
            acc_ref[...] = jnp.zeros_like(acc_ref)

        def step(diag):
            s, mask = _att_scores(qn_ref[...], qp_ref[...], kn_ref[...], kp_ref[...], diag)
            if diag:
                s = jnp.where(mask, s, NEG)
            m_prev = m_ref[...]
            m_cur = jnp.maximum(m_prev, jnp.max(s, axis=-1, keepdims=True))
            p = jnp.exp(s - m_cur[:, :1])
            alpha = jnp.exp(m_prev - m_cur)
            l_ref[...] = alpha * l_ref[...] + jnp.sum(p, axis=-1, keepdims=True)
            acc_ref[...] = acc_ref[...] * alpha + _dot(p, v_ref[...])
            m_ref[...] = m_cur

        @pl.when(j < i)
        def _():
            step(False)

        @pl.when(j == i)
        def _():
            step(True)
            o_ref[...] = acc_ref[...] / l_ref[...]
            lse_ref[...] = m_ref[...] + jnp.log(l_ref[...])

    qs = lambda off: pl.BlockSpec((T, 128), functools.partial(lambda h, i, j, off: (i, off + h), off=off))
    ks = lambda off: pl.BlockSpec((T, 128), functools.partial(lambda h, i, j, off: (jnp.minimum(j, i), off + h), off=off))
    kps = pl.BlockSpec((T, 128), lambda h, i, j: (jnp.minimum(j, i), 0))
    os_ = pl.BlockSpec((T, 128), lambda h, i, j: (i, h))
    return pl.pallas_call(
        body, grid=(4, n, n), in_specs=[qs(0), qs(4), ks(0), kps, ks(4)], out_specs=[os_, os_],
        out_shape=[jax.ShapeDtypeStruct((S_, 512), F32), jax.ShapeDtypeStruct((S_, 512), F32)],
        scratch_shapes=[pltpu.VMEM((T, 128), F32)] * 3, compiler_params=_cp(("parallel", "parallel", "arbitrary")), name=name,
    )(q, q, kv, kp, kv)


def att_bwd_dq(q, kv, kp, o, lse, do, *, name):
    S_ = q.shape[0]
    T = min(ATT_T, S_)
    n = S_ // T

    def body(qn_ref, qp_ref, kn_ref, kp_ref, v_ref, o_ref, lse_ref, do_ref, dqn_ref, dqp_ref, d_ref):
        i, j = pl.program_id(1), pl.program_id(2)

        @pl.when(j == 0)
        def _():
            d = jnp.sum(do_ref[...] * o_ref[...], axis=-1, keepdims=True)
            d_ref[...] = jnp.broadcast_to(d, d_ref.shape)
            dqn_ref[...] = jnp.zeros_like(dqn_ref)
            dqp_ref[...] = jnp.zeros_like(dqp_ref)

        def step(diag):
            s, mask = _att_scores(qn_ref[...], qp_ref[...], kn_ref[...], kp_ref[...], diag)
            p = jnp.exp(s - lse_ref[:, :1])
            if diag:
                p = jnp.where(mask, p, 0.0)
            dp = _dot(do_ref[...], v_ref[...], _NT)
            ds = p * (dp - d_ref[:, :1]) * ATT_SCALE
            dqn_ref[...] += _dot(ds, kn_ref[...])
            dqp_ref[...] += _dot(ds, kp_ref[...])

        @pl.when(j < i)
        def _():
            step(False)

        @pl.when(j == i)
        def _():
            step(True)

    qs = lambda off: pl.BlockSpec((T, 128), functools.partial(lambda h, i, j, off: (i, off + h), off=off))
    ks = lambda off: pl.BlockSpec((T, 128), functools.partial(lambda h, i, j, off: (jnp.minimum(j, i), off + h), off=off))
    kps = pl.BlockSpec((T, 128), lambda h, i, j: (jnp.minimum(j, i), 0))
    os_ = pl.BlockSpec((T, 128), lambda h, i, j: (i, h))
    dqn, dqp, d = pl.pallas_call(
        body, grid=(4, n, n), in_specs=[qs(0), qs(4), ks(0), kps, ks(4), os_, os_, os_], out_specs=[os_, os_, os_],
        out_shape=[jax.ShapeDtypeStruct((S_, 512), F32)] * 3,
        compiler_params=_cp(("parallel", "parallel", "arbitrary")), name=name,
    )(q, q, kv, kp, kv, o, lse, do)
    return dqn, dqp, d


def att_bwd_dkv(q, kv, kp, lse, d, do, *, name):
    S_ = q.shape[0]
    T = min(ATT_T, S_)
    n = S_ // T

    def body(qn_ref, qp_ref, kn_ref, kp_ref, v_ref, lse_ref, d_ref, do_ref, dkn_ref, dv_ref, dkp_ref):
        j, h, i = pl.program_id(0), pl.program_id(1), pl.program_id(2)

        @pl.when(i == 0)
        def _():
            dkn_ref[...] = jnp.zeros_like(dkn_ref)
            dv_ref[...] = jnp.zeros_like(dv_ref)

        @pl.when((i == 0) & (h == 0))
        def _():
            dkp_ref[...] = jnp.zeros_like(dkp_ref)

        def step(diag):
            s, mask = _att_scores(qn_ref[...], qp_ref[...], kn_ref[...], kp_ref[...], diag)
            p = jnp.exp(s - lse_ref[:, :1])
            if diag:
                p = jnp.where(mask, p, 0.0)
            dv_ref[...] += _dot(p, do_ref[...], _TN)
            dp = _dot(do_ref[...], v_ref[...], _NT)
            ds = p * (dp - d_ref[:, :1]) * ATT_SCALE
            dkn_ref[...] += _dot(ds, qn_ref[...], _TN)
            dkp_ref[...] += _dot(ds, qp_ref[...], _TN)

        @pl.when(i > j)
        def _():
            step(False)

        @pl.when(i == j)
        def _():
            step(True)

    qs = lambda off: pl.BlockSpec((T, 128), functools.partial(lambda j, h, i, off: (jnp.maximum(i, j), off + h), off=off))
    ks = lambda off: pl.BlockSpec((T, 128), functools.partial(lambda j, h, i, off: (j, off + h), off=off))
    kps = pl.BlockSpec((T, 128), lambda j, h, i: (j, 0))
    return pl.pallas_call(
        body, grid=(n, 4, n), in_specs=[qs(0), qs(4), ks(0), kps, ks(4), qs(0), qs(0), qs(0)], out_specs=[ks(0), ks(0), kps],
        out_shape=[jax.ShapeDtypeStruct((S_, 512), F32), jax.ShapeDtypeStruct((S_, 512), F32), jax.ShapeDtypeStruct((S_, 128), F32)],
        compiler_params=_cp(("arbitrary", "arbitrary", "arbitrary")), name=name,
    )(q, q, kv, kp, kv, lse, d, do)


CONV_T = 256


def _shift_down(x, halo, s):
    sh = pltpu.roll(x, s, axis=0)
    hr = pltpu.roll(halo, s, axis=0)
    r8 = lax.broadcasted_iota(jnp.int32, hr.shape, 0)
    top = jnp.where(r8 < s, hr, sh[:8])
    return jnp.concatenate([top, sh[8:]], axis=0)


def _shift_up(x, halo, s):
    n = x.shape[0]
    sh = pltpu.roll(x, n - s, axis=0)
    hr = pltpu.roll(halo, 8 - s, axis=0)
    r8 = lax.broadcasted_iota(jnp.int32, hr.shape, 0)
    bot = jnp.where(r8 >= 8 - s, hr, sh[n - 8:])
    return jnp.concatenate([sh[:n - 8], bot], axis=0)


def _conv_pre(x, halo, w, b):
    y = x * w[3:4] + b
    for j in range(3):
        y = y + _shift_down(x, halo, 3 - j) * w[j:j + 1]
    return y


def conv_fwd(src, cb0, ncb, w, b, *, name):
    S_ = src.shape[0]
    T = min(CONV_T, S_)
    nt = S_ // T

    def body(x_ref, h_ref, w_ref, b_ref, o_ref):
        i = pl.program_id(1)
        halo = jnp.where(i > 0, h_ref[...], 0.0)
        y = _conv_pre(x_ref[...], halo, w_ref[...], b_ref[...])
        o_ref[...] = y * jax.nn.sigmoid(y)

    return pl.pallas_call(
        body, grid=(ncb, nt),
        in_specs=[pl.BlockSpec((T, 512), lambda c, i: (i, cb0 + c)),
                  pl.BlockSpec((8, 512), lambda c, i: (jnp.maximum(i * (T // 8) - 1, 0), cb0 + c)),
                  pl.BlockSpec((4, 512), lambda c, i: (0, c)), pl.BlockSpec((1, 512), lambda c, i: (0, c))],
        out_specs=pl.BlockSpec((T, 512), lambda c, i: (i, c)),
        out_shape=jax.ShapeDtypeStruct((S_, 512 * ncb), F32), compiler_params=_cp(("parallel", "parallel")), name=name,
    )(src, src, w, b)


def conv_bwd_pre(src, cb0, ncb, w, b, dy, *, name):
    S_ = src.shape[0]
    T = min(CONV_T, S_)
    nt = S_ // T

    def body(x_ref, h_ref, w_ref, b_ref, dy_ref, dp_ref, dw_ref, db_ref):
        i = pl.program_id(1)
        halo = jnp.where(i > 0, h_ref[...], 0.0)
        x = x_ref[...]
        y = _conv_pre(x, halo, w_ref[...], b_ref[...])
        sg = jax.nn.sigmoid(y)
        dpre = dy_ref[...] * (sg * (1.0 + y * (1.0 - sg)))
        dp_ref[...] = dpre
        rows = [jnp.sum(dpre * _shift_down(x, halo, 3 - j), axis=0, keepdims=True) for j in range(3)]
        rows.append(jnp.sum(dpre * x, axis=0, keepdims=True))
        dw = jnp.concatenate(rows, axis=0)
        db = jnp.sum(dpre, axis=0, keepdims=True)

        @pl.when(i == 0)
        def _():
            dw_ref[...] = dw
            db_ref[...] = db

        @pl.when(i > 0)
        def _():
            dw_ref[...] += dw
            db_ref[...] += db

    return pl.pallas_call(
        body, grid=(ncb, nt),
        in_specs=[pl.BlockSpec((T, 512), lambda c, i: (i, cb0 + c)),
                  pl.BlockSpec((8, 512), lambda c, i: (jnp.maximum(i * (T // 8) - 1, 0), cb0 + c)),
                  pl.BlockSpec((4, 512), lambda c, i: (0, c)), pl.BlockSpec((1, 512), lambda c, i: (0, c)),
                  pl.BlockSpec((T, 512), lambda c, i: (i, c))],
        out_specs=[pl.BlockSpec((T, 512), lambda c, i: (i, c)), pl.BlockSpec((4, 512), lambda c, i: (0, c)),
                   pl.BlockSpec((1, 512), lambda c, i: (0, c))],
        out_shape=[jax.ShapeDtypeStruct((S_, 512 * ncb), F32), jax.ShapeDtypeStruct((4, 512 * ncb), F32),
                   jax.ShapeDtypeStruct((1, 512 * ncb), F32)],
        compiler_params=_cp(("parallel", "arbitrary")), name=name,
    )(src, src, w, b, dy)


def conv_bwd_x(dpre, w, *, name, out_dtype=F32):
    S_, C = dpre.shape
    T = min(CONV_T, S_)
    nt = S_ // T
    ncb = C // 512

    def body(d_ref, h_ref, w_ref, o_ref):
        i = pl.program_id(1)
        halo = jnp.where(i < nt - 1, h_ref[...], 0.0)
        d = d_ref[...]
        w_ = w_ref[...]
        y = d * w_[3:4]
        for j in range(3):
            y = y + _shift_up(d, halo, 3 - j) * w_[j:j + 1]
        o_ref[...] = y.astype(o_ref.dtype)

    return pl.pallas_call(
        body, grid=(ncb, nt),
        in_specs=[pl.BlockSpec((T, 512), lambda c, i: (i, c)),
                  pl.BlockSpec((8, 512), lambda c, i: (jnp.minimum((i + 1) * (T // 8), S_ // 8 - 1), c)),
                  pl.BlockSpec((4, 512), lambda c, i: (0, c))],
        out_specs=pl.BlockSpec((T, 512), lambda c, i: (i, c)),
        out_shape=jax.ShapeDtypeStruct((S_, C), out_dtype), compiler_params=_cp(("parallel", "parallel")), name=name,
    )(dpre, dpre, w)


def ffn_mid_fwd(h, w1, w3, *, name):
    S_, D = h.shape
    F = w1.shape[1]
    tm, tn = _pick(S_, (1024, 512, 256)), 256

    def body(h_ref, w1_ref, w3_ref, a_ref):
        hb = h_ref[...]
        u = _dot(hb, w1_ref[...])
        v = _dot(hb, w3_ref[...])
        a_ref[...] = (u * jax.nn.sigmoid(u) * v).astype(a_ref.dtype)

    return pl.pallas_call(
        body, grid=(S_ // tm, F // tn),
        in_specs=[pl.BlockSpec((tm, D), lambda i, j: (i, 0)), pl.BlockSpec((D, tn), lambda i, j: (0, j)),
                  pl.BlockSpec((D, tn), lambda i, j: (0, j))],
        out_specs=pl.BlockSpec((tm, tn), lambda i, j: (i, j)), out_shape=jax.ShapeDtypeStruct((S_, F), BF16),
        compiler_params=_cp(("parallel", "parallel")), name=name,
    )(h, w1, w3)


def ffn_mid_bwd(h, dy, w1, w3, w2, *, name):
    S_, D = h.shape
    F = w1.shape[1]
    tm, tn = _pick(S_, (1024, 512, 256)), 256

    def body(h_ref, dy_ref, w1_ref, w3_ref, w2_ref, du_ref, dv_ref, a_ref):
        hb = h_ref[...]
        u = _dot(hb, w1_ref[...])
        v = _dot(hb, w3_ref[...])
        da = _dot(dy_ref[...], w2_ref[...], _NT)
        sg = jax.nn.sigmoid(u)
        si = u * sg
        a_ref[...] = (si * v).astype(a_ref.dtype)
        dv_ref[...] = (da * si).astype(dv_ref.dtype)
        du_ref[...] = (da * v * (sg * (1.0 + u * (1.0 - sg)))).astype(du_ref.dtype)

    o = pl.BlockSpec((tm, tn), lambda i, j: (i, j))
    return pl.pallas_call(
        body, grid=(S_ // tm, F // tn),
        in_specs=[pl.BlockSpec((tm, D), lambda i, j: (i, 0)), pl.BlockSpec((tm, D), lambda i, j: (i, 0)),
                  pl.BlockSpec((D, tn), lambda i, j: (0, j)), pl.BlockSpec((D, tn), lambda i, j: (0, j)),
                  pl.BlockSpec((tn, D), lambda i, j: (j, 0))],
        out_specs=[o, o, o], out_shape=[jax.ShapeDtypeStruct((S_, F), BF16)] * 3,
        compiler_params=_cp(("parallel", "parallel")), name=name,
    )(h, dy, w1, w3, w2)


MESH = pl.DeviceIdType.MESH
ANY = pl.BlockSpec(memory_space=pl.ANY)


def allgather8(x_shard, *, name):
    m_per, n = x_shard.shape

    def body(x_ref, out_ref, send_sems, recv_sems, local_sem):
        x, y, c = lax.axis_index("x"), lax.axis_index("y"), lax.axis_index("c")
        me, sibling = (x, y, c), (x, y, 1 - c)
        chips = [(1 - x, y), (x, 1 - y), (1 - x, 1 - y)]

        def rows(px, py, pc):
            return out_ref.at[pl.ds((4 * px + 2 * py + pc) * m_per, m_per), :]

        def copy(k, block, to, src=None):
            return pltpu.make_async_remote_copy(
                src_ref=rows(*block) if src is None else src, dst_ref=rows(*block),
                send_sem=send_sems.at[k], recv_sem=recv_sems.at[k], device_id=to, device_id_type=MESH)

        mine = pltpu.make_async_copy(x_ref, rows(*me), local_sem)
        mine.start()
        first = [copy(0, me, sibling, src=x_ref)]
        first += [copy(1 + j, me, (*chip, c), src=x_ref) for j, chip in enumerate(chips)]
        for cp in first:
            cp.start()
        passed = [copy(4 + j, (*chip, c), sibling) for j, chip in enumerate(chips)]
        for j, chip in enumerate(chips):
            copy(1 + j, (*chip, c), me).wait_recv()
            passed[j].start()
        copy(0, sibling, me).wait_recv()
        for j, chip in enumerate(chips):
            copy(4 + j, (*chip, 1 - c), me).wait_recv()
        for cp in first + passed:
            cp.wait_send()
        mine.wait()

    return pl.pallas_call(
        body, out_shape=jax.ShapeDtypeStruct((8 * m_per, n), x_shard.dtype),
        in_specs=[pl.BlockSpec(memory_space=pltpu.VMEM)], out_specs=pl.BlockSpec(memory_space=pltpu.VMEM),
        scratch_shapes=[pltpu.SemaphoreType.DMA((7,)), pltpu.SemaphoreType.DMA((7,)), pltpu.SemaphoreType.DMA],
        name=name,
    )(x_shard)


def _chip_peers():
    x, y, c = lax.axis_index("x"), lax.axis_index("y"), lax.axis_index("c")
    return x, y, c, [(1 - x, y), (x, 1 - y), (1 - x, 1 - y)]


def allgather_chips(x_shard, *, name):
    r, cdim = x_shard.shape

    def body(x_ref, out_ref, send_sems, recv_sems, local_sem):
        x, y, c, chips = _chip_peers()
        me = 2 * x + y
        mine = pltpu.make_async_copy(x_ref, out_ref.at[me], local_sem)
        mine.start()
        sends = []
        for k, (px, py) in enumerate(chips):
            cp = pltpu.make_async_remote_copy(src_ref=x_ref, dst_ref=out_ref.at[me], send_sem=send_sems.at[k],
                                              recv_sem=recv_sems.at[k], device_id=(px, py, c), device_id_type=MESH)
            cp.start()
            sends.append(cp)
        for k, (px, py) in enumerate(chips):
            pltpu.make_async_remote_copy(src_ref=x_ref, dst_ref=out_ref.at[2 * px + py], send_sem=send_sems.at[k],
                                         recv_sem=recv_sems.at[k], device_id=(px, py, c), device_id_type=MESH).wait_recv()
        for cp in sends:
            cp.wait_send()
        mine.wait()

    return pl.pallas_call(
        body, out_shape=jax.ShapeDtypeStruct((4, r, cdim), x_shard.dtype), in_specs=[ANY], out_specs=ANY,
        scratch_shapes=[pltpu.SemaphoreType.DMA((3,)), pltpu.SemaphoreType.DMA((3,)), pltpu.SemaphoreType.DMA],
        name=name,
    )(x_shard)


def exchange_chips(g, *, name):
    _, r, cdim = g.shape

    def body(g_ref, out_ref, send_sems, recv_sems, local_sem):
        x, y, c, chips = _chip_peers()
        me = 2 * x + y
        mine = pltpu.make_async_copy(g_ref.at[me], out_ref.at[me], local_sem)
        mine.start()
        sends = []
        for k, (px, py) in enumerate(chips):
            cp = pltpu.make_async_remote_copy(src_ref=g_ref.at[2 * px + py], dst_ref=out_ref.at[me], send_sem=send_sems.at[k],
                                              recv_sem=recv_sems.at[k], device_id=(px, py, c), device_id_type=MESH)
            cp.start()
            sends.append(cp)
        for k, (px, py) in enumerate(chips):
            pltpu.make_async_remote_copy(src_ref=g_ref.at[me], dst_ref=out_ref.at[2 * px + py], send_sem=send_sems.at[k],
                                         recv_sem=recv_sems.at[k], device_id=(px, py, c), device_id_type=MESH).wait_recv()
        for cp in sends:
            cp.wait_send()
        mine.wait()

    return pl.pallas_call(
        body, out_shape=jax.ShapeDtypeStruct(g.shape, g.dtype), in_specs=[ANY], out_specs=ANY,
        scratch_shapes=[pltpu.SemaphoreType.DMA((3,)), pltpu.SemaphoreType.DMA((3,)), pltpu.SemaphoreType.DMA],
        name=name,
    )(g)


def swap_sibling(p, *, name):
    def body(p_ref, out_ref, send_sem, recv_sem):
        x, y, c = lax.axis_index("x"), lax.axis_index("y"), lax.axis_index("c")
        cp = pltpu.make_async_remote_copy(src_ref=p_ref, dst_ref=out_ref, send_sem=send_sem, recv_sem=recv_sem,
                                          device_id=(x, y, 1 - c), device_id_type=MESH)
        cp.start()
        cp.wait()

    return pl.pallas_call(
        body, out_shape=jax.ShapeDtypeStruct(p.shape, p.dtype), in_specs=[ANY], out_specs=ANY,
        scratch_shapes=[pltpu.SemaphoreType.DMA, pltpu.SemaphoreType.DMA], name=name,
    )(p)


def sum_slots(r, *, name):
    n, rows, cdim = r.shape
    t = _pick(rows, (256, 128, 64, 32, 16, 8))

    def body(r_ref, o_ref):
        acc = r_ref[0].astype(F32)
        for s in range(1, n):
            acc = acc + r_ref[s].astype(F32)
        o_ref[...] = acc

    return pl.pallas_call(
        body, grid=(rows // t,), in_specs=[pl.BlockSpec((n, t, cdim), lambda i: (0, i, 0))],
        out_specs=pl.BlockSpec((t, cdim), lambda i: (i, 0)), out_shape=jax.ShapeDtypeStruct((rows, cdim), F32),
        compiler_params=_cp(("parallel",)), name=name,
    )(r)


def _rms(x, g):
    return x * lax.rsqrt(jnp.mean(x * x, axis=-1, keepdims=True) + NORM_EPS) * g


def _adaln(x, g, shift, scale):
    return _rms(x, g) * (1.0 + scale) + shift


def _silu(x):
    return x * jax.nn.sigmoid(x)


def _gdn_gate(o, z, g):
    return jnp.concatenate([_rms(o[:, 128 * h:128 * (h + 1)], g) * _silu(z[:, 128 * h:128 * (h + 1)]) for h in range(4)], axis=1)


def _ssd_gate(y, z0, z1, z2, z3, g):
    outs = []
    for k, z in enumerate((z0, z1, z2, z3)):
        t = y[:, 512 * k:512 * (k + 1)] * _silu(z)
        outs.append(t * lax.rsqrt(jnp.mean(t * t, axis=-1, keepdims=True) + NORM_EPS))
    return jnp.concatenate(outs, axis=1) * g


def _rope(x, cos, sin, rot):
    return x * cos + _dot(x, rot, hi=True) * sin


def _rope_q(q, cos, sin, rot):
    return jnp.concatenate([q[:, :512]] + [_rope(q[:, 512 + 128 * h:640 + 128 * h], cos, sin, rot) for h in range(4)], axis=1)


def _rope_t(d, cos, sin, rot):
    return d * cos + _dot(d * sin, rot, _NT, hi=True)


def _vjp_rows(fn, n_rows, n_pars, out_dtypes, rows, cts, pars, *, name, tile=256, extra=None):
    nct = len(cts)

    def bwd(*a):
        r, c, e, p = a[:n_rows], a[n_rows:n_rows + nct], a[n_rows + nct:len(a) - n_pars], a[len(a) - n_pars:]
        out, vjp = jax.vjp(fn, *[t.astype(F32) for t in r], *p)
        ct = tuple(t.astype(F32) for t in c)
        grads = vjp(ct[0] if not isinstance(out, tuple) else ct)
        drows = list(grads[:n_rows])
        if e:
            drows[0] = drows[0] + e[0]
        return (*drows, *grads[n_rows:])

    return rowmap(bwd, list(rows) + list(cts) + ([extra] if extra is not None else []), list(pars), out_dtypes,
                  name=name, tile=tile, n_reduce=n_pars)


ADAM_LR, ADAM_B1, ADAM_B2, ADAM_EPS, ADAM_WD, ADAM_STEP = 0.001, 0.9, 0.999, 1e-08, 0.01, 10


def _adam_math(w, g, m, v):
    m = ADAM_B1 * m + (1.0 - ADAM_B1) * g
    v = ADAM_B2 * v + (1.0 - ADAM_B2) * (g * g)
    m_hat = m / (1.0 - ADAM_B1 ** ADAM_STEP)
    v_hat = v / (1.0 - ADAM_B2 ** ADAM_STEP)
    delta = -ADAM_LR * (m_hat / (jnp.sqrt(v_hat) + ADAM_EPS) + ADAM_WD * w)
    return delta, m, v


def adamw(w, gs, m, v, *, name):
    shape = w.shape
    last = shape[-1]
    to2 = lambda a: a.reshape(-1, last)
    rows = w.size // last
    tile = _pick(rows, (256, 128, 64, 32, 16, 8))
    ng = len(gs)

    def fn(w_, *rest):
        g = rest[0]
        for t in rest[1:ng]:
            g = g + t
        m_, v_ = rest[ng], rest[ng + 1]
        return (g, *_adam_math(w_, g, m_, v_))

    outs = rowmap(fn, [to2(w)] + [to2(g) for g in gs] + [to2(m), to2(v)], [], (F32,) * 4, name=name, tile=tile)
    return tuple(o.reshape(shape) for o in outs)


PACK_W = 1024
BIG = (
    ("ffn_w1", (4, 2, 1024, 704), 3), ("ffn_w3", (4, 2, 1024, 704), 3), ("ffn_w2", (4, 2, 704, 1024), 2),
    ("ev_w_in", (2, 1024, 690), 2), ("mla_w_uq", (2, 96, 4, 192), 1), ("mla_w_ukv", (2, 64, 4, 256), 1),
    ("ev_w_out", (2, 256, 1024), 1), ("ssd_w_in", (2, 1024, 1288), 2), ("ssd_w_out", (2, 512, 1024), 1))


def _seg_rows(shape):
    n = math.prod(shape)
    return -(-n // (16 * PACK_W)) * 16


def _pack(shards, dtype):
    parts = []
    for (_, shape, _), a in zip(BIG, shards):
        flat = a.reshape(-1).astype(dtype)
        pad = _seg_rows(shape) * PACK_W - flat.shape[0]
        parts.append(jnp.pad(flat, (0, pad)) if pad else flat)
    return jnp.concatenate(parts).reshape(-1, PACK_W)


def _unpack(buf):
    out, r0 = [], 0
    for _, shape, _ in BIG:
        n = math.prod(shape)
        out.append(buf[r0:r0 + _seg_rows(shape)].reshape(-1)[:n].reshape(shape))
        r0 += _seg_rows(shape)
    return out


SMALL_SHARDED = (
    ("norm_g", (4, 3, 256), 2), ("gdn_conv_w", (2, 4, 384), 2), ("ssd_conv_w", (2, 4, 768), 2),
    ("ssd_conv_b", (2, 768), 1), ("ssd_norm_g", (2, 512), 1))


def _flat_pack(arrs, width, row_mult):
    flat = jnp.concatenate([a.reshape(-1).astype(F32) for a in arrs])
    n = flat.shape[0]
    tot = -(-n // (width * row_mult)) * width * row_mult
    return jnp.pad(flat, (0, tot - n)).reshape(-1, width)


def _flat_unpack(buf, shapes):
    flat = buf.reshape(-1)
    out, o = [], 0
    for s in shapes:
        n = math.prod(s)
        out.append(flat[o:o + n].reshape(s))
        o += n
    return out


def _rep(v, n):
    return jnp.repeat(v, n, axis=-1)


def kernel(x, c, positions, ada_w, ada_b, norm_g, ffn_w1, ffn_w3, ffn_w2, ev_w_in, gdn_conv_w, gdn_A_log, gdn_dt_bias, gdn_norm_g, mla_q_norm_g, mla_w_uq, mla_kv_norm_g, mla_w_ukv, ev_w_out, ssd_w_in, ssd_conv_w, ssd_conv_b, ssd_A_log, ssd_dt_bias, ssd_D, ssd_norm_g, ssd_w_out, final_g, loss_target, m_ada_w, m_ada_b, m_norm_g, m_ffn_w1, m_ffn_w3, m_ffn_w2, m_ev_w_in, m_gdn_conv_w, m_gdn_A_log, m_gdn_dt_bias, m_gdn_norm_g, m_mla_q_norm_g, m_mla_w_uq, m_mla_kv_norm_g, m_mla_w_ukv, m_ev_w_out, m_ssd_w_in, m_ssd_conv_w, m_ssd_conv_b, m_ssd_A_log, m_ssd_dt_bias, m_ssd_D, m_ssd_norm_g, m_ssd_w_out, m_final_g, v_ada_w, v_ada_b, v_norm_g, v_ffn_w1, v_ffn_w3, v_ffn_w2, v_ev_w_in, v_gdn_conv_w, v_gdn_A_log, v_gdn_dt_bias, v_gdn_norm_g, v_mla_q_norm_g, v_mla_w_uq, v_mla_kv_norm_g, v_mla_w_ukv, v_ev_w_out, v_ssd_w_in, v_ssd_conv_w, v_ssd_conv_b, v_ssd_A_log, v_ssd_dt_bias, v_ssd_D, v_ssd_norm_g, v_ssd_w_out, v_final_g):
    P = dict(ada_w=ada_w, ada_b=ada_b, norm_g=norm_g, ffn_w1=ffn_w1, ffn_w3=ffn_w3, ffn_w2=ffn_w2, ev_w_in=ev_w_in, gdn_conv_w=gdn_conv_w, gdn_A_log=gdn_A_log, gdn_dt_bias=gdn_dt_bias, gdn_norm_g=gdn_norm_g, mla_q_norm_g=mla_q_norm_g, mla_w_uq=mla_w_uq, mla_kv_norm_g=mla_kv_norm_g, mla_w_ukv=mla_w_ukv, ev_w_out=ev_w_out, ssd_w_in=ssd_w_in, ssd_conv_w=ssd_conv_w, ssd_conv_b=ssd_conv_b, ssd_A_log=ssd_A_log, ssd_dt_bias=ssd_dt_bias, ssd_D=ssd_D, ssd_norm_g=ssd_norm_g, ssd_w_out=ssd_w_out, final_g=final_g)
    M1 = dict(ada_w=m_ada_w, ada_b=m_ada_b, norm_g=m_norm_g, ffn_w1=m_ffn_w1, ffn_w3=m_ffn_w3, ffn_w2=m_ffn_w2, ev_w_in=m_ev_w_in, gdn_conv_w=m_gdn_conv_w, gdn_A_log=m_gdn_A_log, gdn_dt_bias=m_gdn_dt_bias, gdn_norm_g=m_gdn_norm_g, mla_q_norm_g=m_mla_q_norm_g, mla_w_uq=m_mla_w_uq, mla_kv_norm_g=m_mla_kv_norm_g, mla_w_ukv=m_mla_w_ukv, ev_w_out=m_ev_w_out, ssd_w_in=m_ssd_w_in, ssd_conv_w=m_ssd_conv_w, ssd_conv_b=m_ssd_conv_b, ssd_A_log=m_ssd_A_log, ssd_dt_bias=m_ssd_dt_bias, ssd_D=m_ssd_D, ssd_norm_g=m_ssd_norm_g, ssd_w_out=m_ssd_w_out, final_g=m_final_g)
    M2 = dict(ada_w=v_ada_w, ada_b=v_ada_b, norm_g=v_norm_g, ffn_w1=v_ffn_w1, ffn_w3=v_ffn_w3, ffn_w2=v_ffn_w2, ev_w_in=v_ev_w_in, gdn_conv_w=v_gdn_conv_w, gdn_A_log=v_gdn_A_log, gdn_dt_bias=v_gdn_dt_bias, gdn_norm_g=v_gdn_norm_g, mla_q_norm_g=v_mla_q_norm_g, mla_w_uq=v_mla_w_uq, mla_kv_norm_g=v_mla_kv_norm_g, mla_w_ukv=v_mla_w_ukv, ev_w_out=v_ev_w_out, ssd_w_in=v_ssd_w_in, ssd_conv_w=v_ssd_conv_w, ssd_conv_b=v_ssd_conv_b, ssd_A_log=v_ssd_A_log, ssd_dt_bias=v_ssd_dt_bias, ssd_D=v_ssd_D, ssd_norm_g=v_ssd_norm_g, ssd_w_out=v_ssd_w_out, final_g=v_final_g)
    names = list(P)
    xi, yi, ci = lax.axis_index("x"), lax.axis_index("y"), lax.axis_index("c")
    chip = 2 * xi + yi
    bidx = 4 * xi + 2 * yi + ci
    xa = x[0]
    S_, D = xa.shape
    tgt = loss_target[0]
    depth = ffn_w1.shape[0]

    wg = allgather_chips(_pack([P[n] for n, _, _ in BIG], BF16), name="gather_weights")
    per_chip = [_unpack(wg[s]) for s in range(4)]
    W = {n: jnp.concatenate([per_chip[s][k] for s in range(4)], axis=ax) for k, (n, _, ax) in enumerate(BIG)}
    sg = allgather_chips(_flat_pack([P[n] for n, _, _ in SMALL_SHARDED], 1024, 16), name="gather_small")
    per_chip_s = [_flat_unpack(sg[s], [sh for _, sh, _ in SMALL_SHARDED]) for s in range(4)]
    Wf = {n: jnp.concatenate([per_chip_s[s][k] for s in range(4)], axis=ax) for k, (n, _, ax) in enumerate(SMALL_SHARDED)}

    c_all = allgather8(jnp.pad(c, ((0, 7), (0, 0))), name="gather_c").reshape(8, 8, D)[:, 0]
    c_act, = rowmap(lambda t: (_silu(t),), [jnp.pad(c_all, ((0, 8), (0, 0)))], [], (F32,), name="c_act", tile=16)
    ncol = ada_w.shape[2]
    ada_b_loc = lax.dynamic_slice(ada_b, (0, chip * ncol), (depth, ncol))
    mod_loc = [mm((c_act, ada_w[l]), name=f"mod_{l}", epi=lambda acc, b: (acc + b,), epi_pars=(ada_b_loc[l][None],),
                  epi_out_dtypes=(F32,), tm=16, tn=256)[0][:8] for l in range(depth)]
    mod_g = allgather8(jnp.stack(mod_loc).reshape(-1, 1024), name="gather_mod").reshape(8, depth, 8, ncol)
    mod_b = lax.dynamic_index_in_dim(mod_g[0::2], bidx, axis=2, keepdims=False)
    mod = jnp.transpose(mod_b, (1, 0, 2)).reshape(depth, 3, 3, D)

    def ev_ext(w):
        z = lambda n: jnp.zeros((w.shape[0], n), w.dtype)
        return jnp.concatenate([w[:, :2048], _rep(w[:, 2048:2052], 128), _rep(w[:, 2052:2056], 128), w[:, 2056:2440], z(128),
                                w[:, 2440:2696], w[:, 2696:2760], z(192)], axis=1)

    def ev_ext_t(dw):
        return jnp.concatenate([dw[:, :2048], dw[:, 2048:2560].reshape(-1, 4, 128).sum(-1), dw[:, 2560:3072].reshape(-1, 4, 128).sum(-1),
                                dw[:, 3072:3456], dw[:, 3584:3840], dw[:, 3840:3904]], axis=1)

    def od_ext(w):
        return jnp.concatenate([w[:, 2048:5120], w[:, :2048], _rep(w[:, 5120:5152], 64)], axis=1)

    def od_ext_t(dw):
        return jnp.concatenate([dw[:, 3072:5120], dw[:, :3072], dw[:, 5120:].reshape(-1, 32, 64).sum(-1)], axis=1)

    def wq_ext(w):
        return jnp.concatenate([w[:, :, :128].reshape(384, 512), jnp.pad(w[:, :, 128:], ((0, 0), (0, 0), (0, 64))).reshape(384, 512)], axis=1)

    def wq_ext_t(dw):
        return jnp.concatenate([dw[:, :512].reshape(384, 4, 128), dw[:, 512:].reshape(384, 4, 128)[:, :, :64]], axis=2)

    def wkv_ext(w):
        return jnp.concatenate([w[:, :, :128].reshape(256, 512), w[:, :, 128:].reshape(256, 512)], axis=1)

    def wkv_ext_t(dw):
        return jnp.concatenate([dw[:, :512].reshape(256, 4, 128), dw[:, 512:].reshape(256, 4, 128)], axis=2)

    half = 32
    inv_freq = 10000.0 ** (-jnp.arange(half, dtype=F32) / half)
    ang = positions[0].astype(F32)[:, None] * inv_freq
    zpad = jnp.zeros((S_, 64), F32)
    cos_t = jnp.concatenate([jnp.cos(ang), jnp.cos(ang), zpad], axis=1)
    sin_t = jnp.concatenate([jnp.sin(ang), jnp.sin(ang), zpad], axis=1)
    ii = jnp.arange(128)
    rot = (jnp.where((ii[:, None] < 32) & (ii[None, :] == ii[:, None] + 32), 1.0, 0.0)
           - jnp.where((ii[:, None] >= 32) & (ii[:, None] < 64) & (ii[None, :] == ii[:, None] - 32), 1.0, 0.0)).astype(F32)

    grads = {}
    dmod = [[[None] * 3 for _ in range(3)] for _ in range(depth)]
    dnorm_g = [[None] * 3 for _ in range(depth)]

    def acc(name, idx, val):
        grads.setdefault(name, {})[idx] = val

    def ffn_sub(xin, l, k, j):
        g, (shift, scale, gate) = Wf["norm_g"][l, k][None], [mod[l, k, t][None] for t in range(3)]
        w1, w3, w2 = W["ffn_w1"][l, j], W["ffn_w3"][l, j], W["ffn_w2"][l, j]
        tag = f"l{l}f{j}"
        h, = rowmap(lambda *a: (_adaln(*a),), [xin], [g, shift, scale], (BF16,), name=f"adaln_{tag}")
        a = ffn_mid_fwd(h, w1, w3, name=f"ffn_mid_{tag}")
        xn, y = mm((a, w2), name=f"ffn_out_{tag}", epi=lambda acc_, xr, gt: (xr + 0.5 * gt * acc_, acc_), epi_rows=(xin,),
                   epi_pars=(gate,), epi_out_dtypes=(F32, F32))

        def bwd(dxn):
            dy, dgate = rowmap(lambda d, y_, gt: ((0.5 * gt) * d, jnp.sum(0.5 * y_ * d, axis=0, keepdims=True)), [dxn, y], [gate],
                               (BF16, F32), name=f"dres_{tag}", n_reduce=1)
            du, dv, a_ = ffn_mid_bwd(h, dy, w1, w3, w2, name=f"ffn_midb_{tag}")
            dh = mm([(du, w1), (dv, w3)], tb=True, name=f"ffn_dh_{tag}")
            acc("ffn_w1", (l, j), mm((h, du), ta=True, name=f"ffn_dw1_{tag}"))
            acc("ffn_w3", (l, j), mm((h, dv), ta=True, name=f"ffn_dw3_{tag}"))
            acc("ffn_w2", (l, j), mm((a_, dy), ta=True, name=f"ffn_dw2_{tag}"))
            dx, dg, dsh, dsc = _vjp_rows(_adaln, 1, 3, (F32,), [xin], [dh], [g, shift, scale], name=f"adalnb_{tag}", extra=dxn)
            dnorm_g[l][k] = dg[0]
            dmod[l][k] = [dsh[0], dsc[0], dgate[0]]
            return dx

        return xn, bwd

    def mixer_tail(xin, l, tag, dh, dxn, g, shift, scale, dgate):
        dx, dg, dsh, dsc = _vjp_rows(_adaln, 1, 3, (F32,), [xin], [dh], [g, shift, scale], name=f"adalnb_{tag}", extra=dxn)
        dnorm_g[l][1] = dg[0]
        dmod[l][1] = [dsh[0], dsc[0], dgate[0]]
        return dx

    def even_sub(xin, l):
        e = l // 2
        tag = f"l{l}m"
        g, (shift, scale, gate) = Wf["norm_g"][l, 1][None], [mod[l, 1, t][None] for t in range(3)]
        wext, wq, wkv, wout = ev_ext(W["ev_w_in"][e]), wq_ext(W["mla_w_uq"][e]), wkv_ext(W["mla_w_ukv"][e]), W["ev_w_out"][e]
        conv_w, zb = Wf["gdn_conv_w"][e], jnp.zeros((1, 1536), F32)
        alog_e, dtb_e = _rep(gdn_A_log[e], 128)[None], _rep(gdn_dt_bias[e], 128)[None]
        gg, qg, kvg = gdn_norm_g[e][None], mla_q_norm_g[e][None], mla_kv_norm_g[e][None]
        h, = rowmap(lambda *a: (_adaln(*a),), [xin], [g, shift, scale], (BF16,), name=f"adaln_{tag}")
        proj = mm((h, wext), name=f"ev_in_{tag}")
        qkvc = conv_fwd(proj, 0, 3, conv_w, zb, name=f"gdn_conv_{tag}")
        o_g, hist = gdn_fwd(qkvc, proj, 4, 5, alog_e, dtb_e, name=f"gdn_{tag}")
        o_a, = rowmap(lambda o, z, g_: (_gdn_gate(o, z, g_),), [o_g, (proj, 512, 3)], [gg], (BF16,), name=f"gdn_gate_{tag}")
        cqn, = rowmap(lambda t, g_: (_rms(t, g_),), [(proj, 384, 8)], [qg], (BF16,), name=f"q_norm_{tag}")
        ckvn, = rowmap(lambda t, g_: (_rms(t, g_),), [(proj, 256, 14)], [kvg], (BF16,), name=f"kv_norm_{tag}")
        q0 = mm((cqn, wq), name=f"q_up_{tag}")
        kv = mm((ckvn, wkv), name=f"kv_up_{tag}", out_dtype=BF16)
        q, = rowmap(lambda t, cs, sn, r: (_rope_q(t, cs, sn, r),), [q0, cos_t, sin_t], [rot], (BF16,), name=f"rope_q_{tag}")
        kp, = rowmap(lambda t, cs, sn, r: (_rope(t, cs, sn, r),), [(proj, 128, 30), cos_t, sin_t], [rot], (BF16,), name=f"rope_k_{tag}")
        o_b, lse = att_fwd(q, kv, kp, name=f"att_{tag}")
        xn, y = mm([(o_a, wout[:512]), (o_b, wout[512:])], name=f"ev_out_{tag}", epi=lambda acc_, xr, gt: (xr + gt * acc_, acc_),
                   epi_rows=(xin,), epi_pars=(gate,), epi_out_dtypes=(F32, F32))

        def bwd(dxn):
            dy, dgate = rowmap(lambda d, y_, gt: (gt * d, jnp.sum(y_ * d, axis=0, keepdims=True)), [dxn, y], [gate],
                               (BF16, F32), name=f"dres_{tag}", n_reduce=1)
            do_a = mm((dy, wout[:512]), tb=True, name=f"ev_doa_{tag}")
            do_b = mm((dy, wout[512:]), tb=True, name=f"ev_dob_{tag}")
            acc("ev_w_out", e, jnp.concatenate([mm((o_a, dy), ta=True, name=f"ev_dwoa_{tag}"), mm((o_b, dy), ta=True, name=f"ev_dwob_{tag}")], axis=0))
            dqn, dqp, dsum = att_bwd_dq(q, kv, kp, o_b, lse, do_b, name=f"att_dq_{tag}")
            dkn, dv, dkp = att_bwd_dkv(q, kv, kp, lse, dsum, do_b, name=f"att_dkv_{tag}")
            dqpe, = rowmap(lambda d, cs, sn, r: (jnp.concatenate([_rope_t(d[:, 128 * hh:128 * (hh + 1)], cs, sn, r) for hh in range(4)], axis=1),),
                           [dqp, cos_t, sin_t], [rot], (F32,), name=f"rope_qb_{tag}")
            dkr, = rowmap(lambda d, cs, sn, r: (_rope_t(d, cs, sn, r),), [dkp, cos_t, sin_t], [rot], (BF16,), name=f"rope_kb_{tag}")
            dcqn = mm([(dqn, wq[:, :512]), (dqpe, wq[:, 512:])], tb=True, name=f"q_upb_{tag}")
            acc("mla_w_uq", e, wq_ext_t(jnp.concatenate([mm((cqn, dqn), ta=True, name=f"q_dwn_{tag}"), mm((cqn, dqpe), ta=True, name=f"q_dwp_{tag}")], axis=1)))
            dckvn = mm([(dkn, wkv[:, :512]), (dv, wkv[:, 512:])], tb=True, name=f"kv_upb_{tag}")
            acc("mla_w_ukv", e, wkv_ext_t(jnp.concatenate([mm((ckvn, dkn), ta=True, name=f"kv_dwk_{tag}"), mm((ckvn, dv), ta=True, name=f"kv_dwv_{tag}")], axis=1)))
            dcq, dqg = _vjp_rows(_rms, 1, 1, (BF16,), [(proj, 384, 8)], [dcqn], [qg], name=f"q_normb_{tag}")
            dckv, dkvg = _vjp_rows(_rms, 1, 1, (BF16,), [(proj, 256, 14)], [dckvn], [kvg], name=f"kv_normb_{tag}")
            acc("mla_q_norm_g", e, dqg[0])
            acc("mla_kv_norm_g", e, dkvg[0])
            do_g, dz, dgg = _vjp_rows(_gdn_gate, 2, 1, (F32, BF16), [o_g, (proj, 512, 3)], [do_a], [gg], name=f"gdn_gateb_{tag}")
            acc("gdn_norm_g", e, dgg[0])
            dqkvc, dbe, dae, dal, ddt = gdn_bwd(qkvc, proj, 4, 5, alog_e, dtb_e, hist, do_g, name=f"gdnb_{tag}")
            acc("gdn_A_log", e, dal.reshape(4, 128).sum(-1))
            acc("gdn_dt_bias", e, ddt.reshape(4, 128).sum(-1))
            dpre, dcw, _ = conv_bwd_pre(proj, 0, 3, conv_w, zb, dqkvc, name=f"gdn_convb_{tag}")
            acc("gdn_conv_w", e, dcw)
            dqkv = conv_bwd_x(dpre, conv_w, name=f"gdn_convx_{tag}", out_dtype=BF16)
            zc = lambda n: jnp.zeros((S_, n), BF16)
            dproj = jnp.concatenate([dqkv, dz, dbe.astype(BF16), dae.astype(BF16), dcq, zc(128), dckv, dkr, zc(128)], axis=1)
            dh = mm((dproj, wext), tb=True, name=f"ev_inb_{tag}")
            acc("ev_w_in", e, ev_ext_t(mm((h, dproj), ta=True, name=f"ev_dwin_{tag}")))
            return mixer_tail(xin, l, tag, dh, dxn, g, shift, scale, dgate)

        return xn, bwd

    def odd_sub(xin, l):
        o = l // 2
        tag = f"l{l}m"
        g, (shift, scale, gate) = Wf["norm_g"][l, 1][None], [mod[l, 1, t][None] for t in range(3)]
        wext, wout = od_ext(W["ssd_w_in"][o]), W["ssd_w_out"][o]
        conv_w, conv_b, ng = Wf["ssd_conv_w"][o], Wf["ssd_conv_b"][o][None], Wf["ssd_norm_g"][o][None]
        ex = lambda v: _rep(v, 64).reshape(4, 1, 512)
        na_e, dtb_e, dsk_e = ex(-jnp.exp(ssd_A_log[o])), ex(ssd_dt_bias[o]), ex(ssd_D[o])
        h, = rowmap(lambda *a: (_adaln(*a),), [xin], [g, shift, scale], (BF16,), name=f"adaln_{tag}")
        proj = mm((h, wext), name=f"ssd_in_{tag}")
        zv = [(proj, 512, 6 + t) for t in range(4)]
        xbc = conv_fwd(proj, 0, 6, conv_w, conv_b, name=f"ssd_conv_{tag}")
        ys, hist = ssd_fwd(xbc, proj, na_e, dtb_e, dsk_e, name=f"ssd_{tag}")
        yn, = rowmap(lambda *a: (_ssd_gate(*a),), [ys] + zv, [ng], (BF16,), name=f"ssd_gate_{tag}", tile=128)
        xn, y = mm((yn, wout), name=f"ssd_out_{tag}", epi=lambda acc_, xr, gt: (xr + gt * acc_, acc_), epi_rows=(xin,),
                   epi_pars=(gate,), epi_out_dtypes=(F32, F32))

        def bwd(dxn):
            dy, dgate = rowmap(lambda d, y_, gt: (gt * d, jnp.sum(y_ * d, axis=0, keepdims=True)), [dxn, y], [gate],
                               (BF16, F32), name=f"dres_{tag}", n_reduce=1)
            dyn = mm((dy, wout), tb=True, name=f"ssd_dyn_{tag}", out_dtype=BF16)
            acc("ssd_w_out", o, mm((yn, dy), ta=True, name=f"ssd_dwout_{tag}"))
            dys, dz0, dz1, dz2, dz3, dng = _vjp_rows(_ssd_gate, 5, 1, (F32, BF16, BF16, BF16, BF16), [ys] + zv, [dyn], [ng],
                                                     name=f"ssd_gateb_{tag}", tile=128)
            acc("ssd_norm_g", o, dng[0])
            dxs, ddtx, db_, dc_, dna, ddtb, ddsk = ssd_bwd(xbc, proj, na_e, dtb_e, dsk_e, hist, dys, name=f"ssdb_{tag}")
            acc("ssd_A_log", o, dna.reshape(32, 64).sum(-1) * (-jnp.exp(ssd_A_log[o])))
            acc("ssd_dt_bias", o, ddtb.reshape(32, 64).sum(-1))
            acc("ssd_D", o, ddsk.reshape(32, 64).sum(-1))
            dxbc = jnp.concatenate([dxs, db_, dc_], axis=1)
            dpre, dcw, dcb = conv_bwd_pre(proj, 0, 6, conv_w, conv_b, dxbc, name=f"ssd_convb_{tag}")
            acc("ssd_conv_w", o, dcw)
            acc("ssd_conv_b", o, dcb[0])
            dxp = conv_bwd_x(dpre, conv_w, name=f"ssd_convx_{tag}", out_dtype=BF16)
            dproj = jnp.concatenate([dxp, dz0, dz1, dz2, dz3, ddtx.astype(BF16)], axis=1)
            dh = mm((dproj, wext), tb=True, name=f"ssd_inb_{tag}")
            acc("ssd_w_in", o, od_ext_t(mm((h, dproj), ta=True, name=f"ssd_dwin_{tag}")))
            return mixer_tail(xin, l, tag, dh, dxn, g, shift, scale, dgate)

        return xn, bwd

    tape = []
    xc = xa
    for l in range(depth):
        xc, b0 = ffn_sub(xc, l, 0, 0)
        xc, b1 = (even_sub if l % 2 == 0 else odd_sub)(xc, l)
        xc, b2 = ffn_sub(xc, l, 2, 1)
        tape += [b0, b1, b2]

    def head(xr, tg, g_):
        def f(xv, gv):
            err = _rms(xv, gv) - tg
            return 0.5 * jnp.sum(jnp.mean(err * err, axis=-1, keepdims=True), axis=0, keepdims=True)
        lo, vjp = jax.vjp(f, xr, g_)
        dxv, dgv = vjp(jnp.ones_like(lo))
        return dxv, jnp.broadcast_to(lo, (1, 128)), dgv

    dx, loss_p, dfg = rowmap(head, [xc, tgt], [final_g[None]], (F32,), name="loss_head", n_reduce=2)
    loss = lax.psum(loss_p[0, 0], ("x", "y", "c"))

    for b in reversed(tape):
        dx = b(dx)
    grad_x = dx[None]

    full = {n: jnp.stack([grads[n][k] for k in sorted(grads[n])]) for n in ("ev_w_in", "mla_w_uq", "mla_w_ukv", "ev_w_out", "ssd_w_in", "ssd_w_out")}
    for n in ("ffn_w1", "ffn_w3", "ffn_w2"):
        full[n] = jnp.stack([jnp.stack([grads[n][(l, j)] for j in range(2)]) for l in range(depth)])
    chunks = []
    for s in range(4):
        chunks.append(_pack([lax.slice_in_dim(full[n], s * sh[ax], (s + 1) * sh[ax], axis=ax) for n, sh, ax in BIG], BF16))
    recv = exchange_chips(jnp.stack(chunks), name="exchange_grads")
    part = sum_slots(recv, name="sum_chips")
    sib = swap_sibling(part, name="swap_sibling")
    g_mine, g_sib = _unpack(part), _unpack(sib)

    dmod_flat = jnp.stack([jnp.stack([jnp.stack(dmod[l][k]) for k in range(3)]) for l in range(depth)]).reshape(depth, 9 * D)
    small_names = ["norm_g", "gdn_conv_w", "gdn_A_log", "gdn_dt_bias", "gdn_norm_g", "mla_q_norm_g", "mla_kv_norm_g",
                   "ssd_conv_w", "ssd_conv_b", "ssd_A_log", "ssd_dt_bias", "ssd_D", "ssd_norm_g", "final_g"]
    small_full = {n: jnp.stack([grads[n][k] for k in sorted(grads[n])]) for n in small_names if n in grads}
    small_full["norm_g"] = jnp.stack([jnp.stack(dnorm_g[l]) for l in range(depth)])
    small_full["final_g"] = dfg[0]
    small_list = [dmod_flat] + [small_full[n] for n in small_names]
    small_shapes = [a.shape for a in small_list]
    sp = _flat_pack(small_list, 128, 8)
    sgath = allgather8(sp, name="gather_small_grads").reshape(8, sp.shape[0], 128)
    ssum = sum_slots(sgath, name="sum_small")
    tot = dict(zip(["ada_b"] + small_names, _flat_unpack(ssum, small_shapes)))
    dmod_all = sgath.reshape(8, -1)[:, :depth * 9 * D].reshape(8, depth, 9 * D)
    dmod_loc = lax.dynamic_slice(dmod_all, (0, 0, chip * ncol), (8, depth, ncol))
    g_ada_w = jnp.stack([mm((c_act, jnp.pad(dmod_loc[:, l], ((0, 8), (0, 0)))), ta=True, name=f"ada_dw_{l}", tk=16, tn=256)
                         for l in range(depth)])

    def own(n, a):
        for m_, sh, ax in SMALL_SHARDED:
            if m_ == n:
                return lax.dynamic_slice_in_dim(a, chip * sh[ax], sh[ax], axis=ax)
        return a

    res = {}
    for k, (n, _, _) in enumerate(BIG):
        res[n] = adamw(P[n], [g_mine[k], g_sib[k]], M1[n], M2[n], name=f"adamw_{n}")
    res["ada_w"] = adamw(ada_w, [g_ada_w], m_ada_w, v_ada_w, name="adamw_ada_w")
    sm = ["ada_b"] + small_names
    shapes = [P[n].shape for n in sm]
    pk = lambda d: _flat_pack([d[n] for n in sm], 128, 8)
    outs = adamw(pk(P), [pk({n: own(n, tot[n]).reshape(P[n].shape) for n in sm})], pk(M1), pk(M2), name="adamw_small")
    un = [_flat_unpack(o, shapes) for o in outs]
    for i, n in enumerate(sm):
        res[n] = tuple(un[t][i] for t in range(4))
    return (loss, grad_x, *[res[n][0] for n in names], *[res[n][1] for n in names], *[res[n][2] for n in names], *[res[n][3] for n in names])
```

```python
import functools
import math

import jax
import jax.numpy as jnp
from jax import lax
from jax.experimental import pallas as pl
from jax.experimental.pallas import tpu as pltpu

F32 = jnp.float32
BF16 = jnp.bfloat16
HI = lax.Precision.HIGHEST
HI3 = lax.Precision.HIGH
VMEM_LIMIT = 56 * 1024 * 1024
NORM_EPS = 1e-6
MM_VMEM_BUDGET = 40 * 1024 * 1024


def _cp(sem=None):
    if sem is None:
        return pltpu.CompilerParams(vmem_limit_bytes=VMEM_LIMIT)
    return pltpu.CompilerParams(dimension_semantics=sem, vmem_limit_bytes=VMEM_LIMIT)


def _pick(dim, prefs):
    for p in prefs:
        if dim % p == 0:
            return p
    return dim


def mm(pairs, *, ta=False, tb=False, out_dtype=F32, name, epi=None, epi_rows=(), epi_pars=(), epi_out_dtypes=None,
       tm=None, tn=None, tk=None):
    if not isinstance(pairs, (list, tuple)) or not isinstance(pairs[0], (list, tuple)):
        pairs = [pairs]
    npair = len(pairs)
    a0, b0 = pairs[0]
    M = a0.shape[1] if ta else a0.shape[0]
    K = a0.shape[0] if ta else a0.shape[1]
    N = b0.shape[0] if tb else b0.shape[1]
    for a, b in pairs:
        assert (a.shape == ((K, M) if ta else (M, K))), (a.shape, M, K)
        assert (b.shape == ((N, K) if tb else (K, N))), (b.shape, K, N)
    tm = tm or _pick(M, (1024, 1408, 512, 384, 256, 128))
    tk = tk or (K if K <= 1024 else _pick(K, (1024, 1408, 512, 256, 128)))
    if tn is None:
        n_epi_out = 1 if epi is None else len(epi_out_dtypes)
        for tn in (1024, 1408, 512, 384, 256, 128, N):
            if N % tn:
                continue
            need = sum(2 * tk * (tm * a.dtype.itemsize + tn * b.dtype.itemsize) for a, b in pairs)
            need += tm * tn * 4 * (1 + 2 * n_epi_out + 2 * len(epi_rows))
            if need <= MM_VMEM_BUDGET:
                break
    nk = K // tk
    assert M % tm == 0 and N % tn == 0 and K % tk == 0, (M, N, K, tm, tn, tk)
    n_rows, n_pars = len(epi_rows), len(epi_pars)
    if epi is None:
        out_dtypes = (out_dtype,)
    else:
        out_dtypes = tuple(epi_out_dtypes)
    n_out = len(out_dtypes)
    dn = (((0 if ta else 1,), (1 if tb else 0,)), ((), ()))

    def body(*refs):
        ab = refs[:2 * npair]
        rows = refs[2 * npair:2 * npair + n_rows]
        pars = refs[2 * npair + n_rows:2 * npair + n_rows + n_pars]
        outs = refs[2 * npair + n_rows + n_pars:2 * npair + n_rows + n_pars + n_out]
        acc_ref = refs[-1]
        k = pl.program_id(2)

        @pl.when(k == 0)
        def _():
            acc_ref[...] = jnp.zeros_like(acc_ref)

        acc = acc_ref[...]
        for p in range(npair):
            a = ab[2 * p][...].astype(BF16)
            b = ab[2 * p + 1][...].astype(BF16)
            acc = acc + lax.dot_general(a, b, dn, preferred_element_type=F32)
        acc_ref[...] = acc

        @pl.when(k == nk - 1)
        def _():
            r = acc_ref[...]
            if epi is None:
                outs[0][...] = r.astype(outs[0].dtype)
            else:
                res = epi(r, *[x[...] for x in rows], *[x[...] for x in pars])
                for o, v in zip(outs, res):
                    o[...] = v.astype(o.dtype)

    a_spec = pl.BlockSpec((tk, tm), lambda i, j, k: (k, i)) if ta else pl.BlockSpec((tm, tk), lambda i, j, k: (i, k))
    b_spec = pl.BlockSpec((tn, tk), lambda i, j, k: (j, k)) if tb else pl.BlockSpec((tk, tn), lambda i, j, k: (k, j))
    in_specs = []
    args = []
    for a, b in pairs:
        in_specs += [a_spec, b_spec]
        args += [a, b]
    for r in epi_rows:
        in_specs.append(pl.BlockSpec((tm, tn), lambda i, j, k: (i, j)))
        args.append(r)
    for p_ in epi_pars:
        in_specs.append(pl.BlockSpec((1, tn), lambda i, j, k: (0, j)))
        args.append(p_)
    out_specs = [pl.BlockSpec((tm, tn), lambda i, j, k: (i, j)) for _ in range(n_out)]
    out_shape = [jax.ShapeDtypeStruct((M, N), d) for d in out_dtypes]
    res = pl.pallas_call(
        body, grid=(M // tm, N // tn, nk), in_specs=in_specs, out_specs=out_specs, out_shape=out_shape,
        scratch_shapes=[pltpu.VMEM((tm, tn), F32)], compiler_params=_cp(("parallel", "parallel", "arbitrary")), name=name,
    )(*args)
    return res[0] if epi is None else tuple(res)


def rowmap(fn, rows, pars, out_dtypes, *, name, tile=256, n_reduce=0):
    views = []
    for r in rows:
        if isinstance(r, tuple):
            views.append(r)
        else:
            views.append((r, r.shape[1], 0))
    S = views[0][0].shape[0]
    assert S % tile == 0
    nt = S // tile
    row_structs = [jax.ShapeDtypeStruct((tile, w), a.dtype) for a, w, _ in views]
    par_structs = [jax.ShapeDtypeStruct(p.shape, p.dtype) for p in pars]
    out_structs = jax.eval_shape(fn, *row_structs, *par_structs)
    n_out = len(out_structs)
    n_row_out = n_out - n_reduce
    nr, npar = len(views), len(pars)

    def body(*refs):
        ins = [x[...] for x in refs[:nr + npar]]
        outs = refs[nr + npar:]
        res = fn(*ins)
        for o, v in zip(outs[:n_row_out], res[:n_row_out]):
            o[...] = v.astype(o.dtype)
        if n_reduce:
            i = pl.program_id(0)

            @pl.when(i == 0)
            def _():
                for o, v in zip(outs[n_row_out:], res[n_row_out:]):
                    o[...] = v.astype(o.dtype)

            @pl.when(i > 0)
            def _():
                for o, v in zip(outs[n_row_out:], res[n_row_out:]):
                    o[...] += v.astype(o.dtype)

    in_specs = [pl.BlockSpec((tile, w), functools.partial(lambda i, c: (i, c), c=c)) for _, w, c in views]
    in_specs += [pl.BlockSpec(p.shape, lambda i: (0, 0)) for p in pars]
    out_specs = [pl.BlockSpec((tile, s.shape[1]), lambda i: (i, 0)) for s in out_structs[:n_row_out]]
    out_specs += [pl.BlockSpec(s.shape, lambda i: (0, 0)) for s in out_structs[n_row_out:]]
    out_shape = [jax.ShapeDtypeStruct((S, s.shape[1]), d) for s, d in zip(out_structs[:n_row_out], out_dtypes[:n_row_out])]
    out_shape += [jax.ShapeDtypeStruct(s.shape, F32) for s in out_structs[n_row_out:]]
    res = pl.pallas_call(
        body, grid=(nt,), in_specs=in_specs, out_specs=out_specs, out_shape=out_shape,
        compiler_params=_cp(("arbitrary",) if n_reduce else ("parallel",)), name=name,
    )(*[v[0] for v in views], *pars)
    return tuple(res)


CH = 64


def _softplus(x):
    return jnp.where(x > 20.0, x, jnp.log(1.0 + jnp.exp(jnp.minimum(x, 20.0))))


def _dot(a, b, dn=(((1,), (0,)), ((), ())), hi=False):
    if hi:
        return lax.dot_general(a.astype(F32), b.astype(F32), dn, precision=HI if hi is True else hi, preferred_element_type=F32)
    return lax.dot_general(a.astype(BF16), b.astype(BF16), dn, preferred_element_type=F32)


_NT = (((1,), (1,)), ((), ()))
_TN = (((0,), (0,)), ((), ()))


def _chunk_consts():
    r = lax.broadcasted_iota(jnp.int32, (CH, 2 * CH), 0)
    c0 = lax.broadcasted_iota(jnp.int32, (CH, 2 * CH), 1)
    c = jnp.where(c0 >= CH, c0 - CH, c0)
    r1 = lax.broadcasted_iota(jnp.int32, (CH, CH), 0)
    c1 = lax.broadcasted_iota(jnp.int32, (CH, CH), 1)
    return dict(
        lower2=r >= c, strict2=r > c, U2=(r <= c).astype(F32), eye2=(r == c).astype(F32),
        L=(r1 >= c1).astype(F32), ones=jnp.ones((CH, CH), F32), Z=jnp.zeros((CH, 2 * CH), F32))


@jax.custom_vjp
def _tri_inv2(a2, eye2, Z):
    def prod(x2, y):
        return _dot(x2, jnp.concatenate([y, Z], axis=0), hi=HI3)

    b = -a2
    t2 = eye2 + b
    for _ in range(5):
        b = prod(b, b)
        t2 = t2 + prod(t2, b)
    return t2


def _tri_inv2_fwd(a2, eye2, Z):
    t2 = _tri_inv2(a2, eye2, Z)
    return t2, (t2, eye2, Z)


def _tri_inv2_bwd(res, dt2):
    t2, eye2, Z = res
    x2 = _dot(t2, dt2, _TN, hi=HI3)[:CH]
    da2 = -_dot(x2, jnp.concatenate([t2, Z], axis=0), _NT, hi=HI3)
    return da2, jnp.zeros_like(eye2), jnp.zeros_like(Z)


_tri_inv2.defvjp(_tri_inv2_fwd, _tri_inv2_bwd)


def _gdn_head(q, k, v, bx, ax, S, alog, dtb, cst):
    lower2, strict2, U2, eye2, L, ones, Z = (cst[n] for n in ("lower2", "strict2", "U2", "eye2", "L", "ones", "Z"))
    qn = q * lax.rsqrt(jnp.sum(q * q, axis=-1, keepdims=True) + NORM_EPS) * (128.0 ** -0.5)
    kn = k * lax.rsqrt(jnp.sum(k * k, axis=-1, keepdims=True) + NORM_EPS)
    beta = jax.nn.sigmoid(bx)
    g = -jnp.exp(alog) * _softplus(ax + dtb)
    gc = _dot(L, g, hi=HI3)
    n2 = _dot(ones, g * U2, hi=HI3)
    d2 = gc - n2
    decay2 = jnp.where(lower2, jnp.exp(jnp.where(lower2, d2, 0.0)), 0.0)
    kb = kn * beta
    kn2 = jnp.concatenate([kn, kn], axis=0)
    a2 = jnp.where(strict2, _dot(kb, kn2, _NT) * decay2, 0.0)

    def prod(x2, y):
        return _dot(x2, jnp.concatenate([y, Z], axis=0), hi=HI3)

    t2 = _tri_inv2(a2, eye2, Z)
    glast = jnp.sum(g, axis=0, keepdims=True)
    u = prod(t2, v * beta)
    w = prod(t2, kb * jnp.exp(gc))
    attn2 = jnp.where(lower2, _dot(qn, kn2, _NT) * decay2, 0.0)
    k_end = kn * jnp.exp(glast - gc)
    q_start = qn * jnp.exp(gc)
    v_new = u - _dot(w, S)
    o = _dot(q_start, S) + _dot(attn2, jnp.concatenate([v_new, Z], axis=0))
    s_new = S * jnp.exp(glast) + _dot(k_end, v_new, _TN)
    return o, s_new


def gdn_fwd(qkv, proj, bcol, acol, alog_e, dtb_e, *, name):
    S_ = qkv.shape[0]
    nc = S_ // CH

    def body(q_ref, k_ref, v_ref, b_ref, a_ref, al_ref, dt_ref, o_ref, hist_ref, s_ref):
        i = pl.program_id(0)

        @pl.when(i == 0)
        def _():
            s_ref[...] = jnp.zeros_like(s_ref)

        cst = _chunk_consts()
        hist_ref[0] = s_ref[...]
        heads = [slice(128 * h, 128 * (h + 1)) for h in range(4)]
        res = [_gdn_head(q_ref[:, ls], k_ref[:, ls], v_ref[:, ls], b_ref[:, ls], a_ref[:, ls], s_ref[ls, :],
                         al_ref[:, ls], dt_ref[:, ls], cst) for ls in heads]
        for ls, (o, s_new) in zip(heads, res):
            o_ref[:, ls] = o
            s_ref[ls, :] = s_new

    blk = lambda cb: pl.BlockSpec((CH, 512), functools.partial(lambda i, cb: (i, cb), cb=cb))
    par = pl.BlockSpec((1, 512), lambda i: (0, 0))
    return pl.pallas_call(
        body, grid=(nc,), in_specs=[blk(0), blk(1), blk(2), blk(bcol), blk(acol), par, par],
        out_specs=[pl.BlockSpec((CH, 512), lambda i: (i, 0)), pl.BlockSpec((1, 512, 128), lambda i: (i, 0, 0))],
        out_shape=[jax.ShapeDtypeStruct((S_, 512), F32), jax.ShapeDtypeStruct((nc, 512, 128), F32)],
        scratch_shapes=[pltpu.VMEM((512, 128), F32)], compiler_params=_cp(("arbitrary",)), name=name,
    )(qkv, qkv, qkv, proj, proj, alog_e, dtb_e)


def gdn_bwd(qkv, proj, bcol, acol, alog_e, dtb_e, hist, do, *, name):
    S_ = qkv.shape[0]
    nc = S_ // CH

    def body(q_ref, k_ref, v_ref, b_ref, a_ref, al_ref, dt_ref, hist_ref, do_ref, dqkv_ref, db_ref, da_ref, dal_ref, ddt_ref, ds_ref):
        i = pl.program_id(0)

        @pl.when(i == 0)
        def _():
            ds_ref[...] = jnp.zeros_like(ds_ref)
            dal_ref[...] = jnp.zeros_like(dal_ref)
            ddt_ref[...] = jnp.zeros_like(ddt_ref)

        cst = _chunk_consts()
        fn = functools.partial(_gdn_head, cst=cst)
        res = []
        for h in range(4):
            ls = slice(128 * h, 128 * (h + 1))
            _, vjp = jax.vjp(fn, q_ref[:, ls], k_ref[:, ls], v_ref[:, ls], b_ref[:, ls], a_ref[:, ls], hist_ref[0, ls, :],
                             al_ref[:, ls], dt_ref[:, ls])
            res.append(vjp((do_ref[:, ls], ds_ref[ls, :])))
        for h in range(4):
            ls = slice(128 * h, 128 * (h + 1))
            dq, dk, dv, db, da, ds_in, dal, ddt = res[h]
            dqkv_ref[:, 128 * h:128 * (h + 1)] = dq
            dqkv_ref[:, 512 + 128 * h:512 + 128 * (h + 1)] = dk
            dqkv_ref[:, 1024 + 128 * h:1024 + 128 * (h + 1)] = dv
            db_ref[:, ls] = db
            da_ref[:, ls] = da
            ds_ref[ls, :] = ds_in
            dal_ref[:, ls] += dal
            ddt_ref[:, ls] += ddt

    rblk = lambda cb: pl.BlockSpec((CH, 512), functools.partial(lambda i, cb: (nc - 1 - i, cb), cb=cb))
    par = pl.BlockSpec((1, 512), lambda i: (0, 0))
    return pl.pallas_call(
        body, grid=(nc,),
        in_specs=[rblk(0), rblk(1), rblk(2), rblk(bcol), rblk(acol), par, par,
                  pl.BlockSpec((1, 512, 128), lambda i: (nc - 1 - i, 0, 0)), rblk(0)],
        out_specs=[pl.BlockSpec((CH, 1536), lambda i: (nc - 1 - i, 0)), rblk(0), rblk(0), par, par],
        out_shape=[jax.ShapeDtypeStruct((S_, 1536), F32), jax.ShapeDtypeStruct((S_, 512), F32), jax.ShapeDtypeStruct((S_, 512), F32),
                   jax.ShapeDtypeStruct((1, 512), F32), jax.ShapeDtypeStruct((1, 512), F32)],
        scratch_shapes=[pltpu.VMEM((512, 128), F32)], compiler_params=_cp(("arbitrary",)), name=name,
    )(qkv, qkv, qkv, proj, proj, alog_e, dtb_e, hist, do)


def _ssd_group(xs, dtxs, bm, cm, hss, nas, dtbs, dsks, cst):
    lower2, U2, L, ones = cst["lower2"], cst["U2"], cst["L"], cst["ones"]
    lane = lax.broadcasted_iota(jnp.int32, (1, 2 * CH), 1)
    mask_l = (lane < CH).astype(F32)
    mask_r = 1.0 - mask_l
    ones_w = jnp.ones((CH, 2 * CH), F32)
    cb2 = _dot(cm, jnp.concatenate([bm, bm], axis=0), _NT)
    ys, hs_new = [], []
    for x, dtx, hs, na, dtb, dsk in zip(xs, dtxs, hss, nas, dtbs, dsks):
        dt = _softplus(dtx + dtb)
        da = dt * na
        m = _dot(L, da, hi=HI3)
        n2 = _dot(ones, da * U2, hi=HI3)
        lm2 = jnp.where(lower2, jnp.exp(jnp.where(lower2, m - n2, 0.0)), 0.0)
        xdt = x * dt
        x2 = jnp.concatenate([xdt * mask_l, xdt * mask_r], axis=0)
        y_diag = _dot(cb2 * lm2, x2)
        alast = jnp.sum(da, axis=0, keepdims=True)
        y_off = _dot(cm, hs, _NT) * jnp.exp(m)
        cd = jnp.exp(_dot(da, ones_w, _TN, hi=HI3))
        hs_new.append(hs * cd + _dot(xdt * jnp.exp(alast - m), bm, _TN))
        ys.append(y_diag + y_off + dsk * x)
    return tuple(ys), tuple(hs_new)


def _ssd_specs(nc, rev):
    ci = (lambda i: nc - 1 - i) if rev else (lambda i: i)
    xg = pl.BlockSpec((CH, 512), lambda i, g: (ci(i), g))
    dtg = pl.BlockSpec((CH, 512), lambda i, g: (ci(i), 10 + g))
    bg = pl.BlockSpec((CH, 128), lambda i, g: (ci(i), 16 + g))
    cg = pl.BlockSpec((CH, 128), lambda i, g: (ci(i), 20 + g))
    par = pl.BlockSpec((4, 1, 512), lambda i, g: (0, 0, 0))
    hist = pl.BlockSpec((1, 512, 128), lambda i, g: (ci(i), g, 0))
    return xg, dtg, bg, cg, par, hist


def ssd_fwd(xbc, proj, na_e, dtb_e, dsk_e, *, name):
    S_ = xbc.shape[0]
    nc = S_ // CH
    xg, dtg, bg, cg, par, hist = _ssd_specs(nc, False)

    def body(x_ref, dt_ref, b_ref, c_ref, na_ref, dtb_ref, dsk_ref, y_ref, hist_ref, s_ref):
        i, g = pl.program_id(0), pl.program_id(1)

        @pl.when(i == 0)
        def _():
            s_ref[g] = jnp.zeros((512, 128), F32)

        cst = _chunk_consts()
        hist_ref[0] = s_ref[g]
        sl = [slice(128 * p, 128 * (p + 1)) for p in range(4)]
        na, dtb, dsk = na_ref[g], dtb_ref[g], dsk_ref[g]
        ys, hs_new = _ssd_group([x_ref[:, s] for s in sl], [dt_ref[:, s] for s in sl], b_ref[...], c_ref[...],
                                [s_ref[g, s, :] for s in sl], [na[:, s] for s in sl], [dtb[:, s] for s in sl],
                                [dsk[:, s] for s in sl], cst)
        for p, s in enumerate(sl):
            y_ref[:, s] = ys[p]
            s_ref[g, s, :] = hs_new[p]

    return pl.pallas_call(
        body, grid=(nc, 4), in_specs=[xg, dtg, bg, cg, par, par, par],
        out_specs=[pl.BlockSpec((CH, 512), lambda i, g: (i, g)), hist],
        out_shape=[jax.ShapeDtypeStruct((S_, 2048), F32), jax.ShapeDtypeStruct((nc, 2048, 128), F32)],
        scratch_shapes=[pltpu.VMEM((4, 512, 128), F32)], compiler_params=_cp(("arbitrary", "arbitrary")), name=name,
    )(xbc, proj, xbc, xbc, na_e, dtb_e, dsk_e)


def ssd_bwd(xbc, proj, na_e, dtb_e, dsk_e, hist, dy, *, name):
    S_ = xbc.shape[0]
    nc = S_ // CH
    xg, dtg, bg, cg, par, hist_spec = _ssd_specs(nc, True)
    og = pl.BlockSpec((CH, 512), lambda i, g: (nc - 1 - i, g))
    o128 = pl.BlockSpec((CH, 128), lambda i, g: (nc - 1 - i, g))

    def body(x_ref, dt_ref, b_ref, c_ref, na_ref, dtb_ref, dsk_ref, hist_ref, dy_ref,
             dx_ref, ddt_ref, db_ref, dc_ref, dna_ref, ddtb_ref, ddsk_ref, ds_ref):
        i, g = pl.program_id(0), pl.program_id(1)

        @pl.when(i == 0)
        def _():
            ds_ref[g] = jnp.zeros((512, 128), F32)
            dna_ref[g] = jnp.zeros((1, 512), F32)
            ddtb_ref[g] = jnp.zeros((1, 512), F32)
            ddsk_ref[g] = jnp.zeros((1, 512), F32)

        cst = _chunk_consts()
        sl = [slice(128 * p, 128 * (p + 1)) for p in range(4)]
        na, dtb, dsk = na_ref[g], dtb_ref[g], dsk_ref[g]
        fn = functools.partial(_ssd_group, cst=cst)
        _, vjp = jax.vjp(fn, [x_ref[:, s] for s in sl], [dt_ref[:, s] for s in sl], b_ref[...], c_ref[...],
                         [hist_ref[0, s, :] for s in sl], [na[:, s] for s in sl], [dtb[:, s] for s in sl], [dsk[:, s] for s in sl])
        dxs, ddts, db, dc, dhs, dnas, ddtbs, ddsks = vjp((tuple(dy_ref[:, s] for s in sl), tuple(ds_ref[g, s, :] for s in sl)))
        db_ref[...] = db
        dc_ref[...] = dc
        for p, s in enumerate(sl):
            dx_ref[:, s] = dxs[p]
            ddt_ref[:, s] = ddts[p]
            ds_ref[g, s, :] = dhs[p]
            dna_ref[g, :, s] += dnas[p]
            ddtb_ref[g, :, s] += ddtbs[p]
            ddsk_ref[g, :, s] += ddsks[p]

    return pl.pallas_call(
        body, grid=(nc, 4), in_specs=[xg, dtg, bg, cg, par, par, par, hist_spec, og],
        out_specs=[og, og, o128, o128, par, par, par],
        out_shape=[jax.ShapeDtypeStruct((S_, 2048), F32), jax.ShapeDtypeStruct((S_, 2048), F32),
                   jax.ShapeDtypeStruct((S_, 512), F32), jax.ShapeDtypeStruct((S_, 512), F32)] +
                  [jax.ShapeDtypeStruct((4, 1, 512), F32)] * 3,
        scratch_shapes=[pltpu.VMEM((4, 512, 128), F32)], compiler_params=_cp(("arbitrary", "arbitrary")), name=name,
    )(xbc, proj, xbc, xbc, na_e, dtb_e, dsk_e, hist, dy)


ATT_T = 1024
ATT_SCALE = 192.0 ** -0.5
NEG = -1e30


def _chunk_mask(shape):
    return lax.broadcasted_iota(jnp.int32, shape, 1) // CH <= lax.broadcasted_iota(jnp.int32, shape, 0) // CH


def att_fwd(q, kv, kp, *, name):
    S_ = q.shape[0]
    T = min(ATT_T, S_)
    n = S_ // T

    def body(q_ref, kn_ref, kp_ref, v_ref, o_ref, lse_ref, m_ref, l_ref, acc_ref):
        i, j = pl.program_id(1), pl.program_id(2)

        @pl.when(j == 0)
        def _():
            m_ref[...] = jnp.full_like(m_ref, NEG)
            l_ref[...] = jnp.zeros_like(l_ref)
            acc_ref[...] = jnp.zeros_like(acc_ref)

        def step(diag):
            k2 = jnp.concatenate([kn_ref[...], kp_ref[...]], axis=1)
            s = _dot(q_ref[...], k2, _NT)
            if diag:
                s = jnp.where(_chunk_mask(s.shape), s, NEG)
            m_prev = m_ref[...]
            m_cur = jnp.maximum(m_prev, jnp.max(s, axis=-1, keepdims=True))
            p = jnp.exp(s - m_cur[:, :1])
            alpha = jnp.exp(m_prev - m_cur)
            l_ref[...] = alpha * l_ref[...] + jnp.sum(p, axis=-1, keepdims=True)
            acc_ref[...] = acc_ref[...] * alpha + _dot(p, v_ref[...])
            m_ref[...] = m_cur

        @pl.when(j < i)
        def _():
            step(False)

        @pl.when(j == i)
        def _():
            step(True)
            o_ref[...] = acc_ref[...] / l_ref[...]
            lse_ref[...] = m_ref[...] + jnp.log(l_ref[...])

    return pl.pallas_call(
        body, grid=(4, n, n),
        in_specs=[pl.BlockSpec((T, 256), lambda h, i, j: (i, h)), pl.BlockSpec((T, 128), lambda h, i, j: (jnp.minimum(j, i), h)),
                  pl.BlockSpec((T, 128), lambda h, i, j: (jnp.minimum(j, i), 0)),
                  pl.BlockSpec((T, 128), lambda h, i, j: (jnp.minimum(j, i), 4 + h))],
        out_specs=[pl.BlockSpec((T, 128), lambda h, i, j: (i, h))] * 2,
        out_shape=[jax.ShapeDtypeStruct((S_, 512), F32), jax.ShapeDtypeStruct((S_, 512), F32)],
        scratch_shapes=[pltpu.VMEM((T, 128), F32)] * 3,
        compiler_params=_cp(("parallel", "parallel", "arbitrary")), name=name,
    )(q, kv, kp, kv)


def att_bwd(q, kv, kp, lse, dsum, do, *, name):
    S_ = q.shape[0]
    T = min(ATT_T, S_)
    n = S_ // T

    def body(q_ref, kn_ref, kp_ref, v_ref, lse_ref, d_ref, do_ref, dq_hbm, dk_ref, dv_ref, dq_acc, sem):
        h, j, i = pl.program_id(0), pl.program_id(1), pl.program_id(2)

        @pl.when((j == 0) & (i == 0))
        def _():
            dq_acc[...] = jnp.zeros_like(dq_acc)

        @pl.when(i == 0)
        def _():
            dk_ref[...] = jnp.zeros_like(dk_ref)
            dv_ref[...] = jnp.zeros_like(dv_ref)

        def step(diag):
            k2 = jnp.concatenate([kn_ref[...], kp_ref[...]], axis=1)
            qb = q_ref[...]
            dob = do_ref[...].astype(BF16)
            s = _dot(qb, k2, _NT)
            p = jnp.exp(s - lse_ref[:, :1])
            if diag:
                p = jnp.where(_chunk_mask(s.shape), p, 0.0)
            dv_ref[...] += _dot(p, dob, _TN)
            ds = (p * (_dot(dob, v_ref[...], _NT) - d_ref[:, :1])).astype(BF16)
            dk_ref[...] += _dot(ds, qb, _TN)
            rows = pl.ds(pl.multiple_of(i * T, T), T)
            dq_acc[rows, :] += _dot(ds, k2)

        @pl.when(i > j)
        def _():
            step(False)

        @pl.when(i == j)
        def _():
            step(True)

        @pl.when((j == n - 1) & (i == n - 1))
        def _():
            cp = pltpu.make_async_copy(dq_acc, dq_hbm.at[h], sem)
            cp.start()
            cp.wait()

    qmap = lambda h, j, i: (jnp.maximum(i, j), h)
    return pl.pallas_call(
        body, grid=(4, n, n),
        in_specs=[pl.BlockSpec((T, 256), qmap), pl.BlockSpec((T, 128), lambda h, j, i: (j, h)),
                  pl.BlockSpec((T, 128), lambda h, j, i: (j, 0)), pl.BlockSpec((T, 128), lambda h, j, i: (j, 4 + h)),
                  pl.BlockSpec((T, 128), qmap), pl.BlockSpec((T, 128), qmap), pl.BlockSpec((T, 128), qmap)],
        out_specs=[pl.BlockSpec(memory_space=pl.ANY), pl.BlockSpec((T, 256), lambda h, j, i: (j, h)),
                   pl.BlockSpec((T, 128), lambda h, j, i: (j, h))],
        out_shape=[jax.ShapeDtypeStruct((4, S_, 256), F32), jax.ShapeDtypeStruct((S_, 1024), F32), jax.ShapeDtypeStruct((S_, 512), F32)],
        scratch_shapes=[pltpu.VMEM((S_, 256), F32), pltpu.SemaphoreType.DMA],
        compiler_params=_cp(("arbitrary", "arbitrary", "arbitrary")), name=name,
    )(q, kv, kp, kv, lse, dsum, do)


CONV_T = 256


def _shift_down(x, halo, s):
    sh = pltpu.roll(x, s, axis=0)
    hr = pltpu.roll(halo, s, axis=0)
    r8 = lax.broadcasted_iota(jnp.int32, hr.shape, 0)
    top = jnp.where(r8 < s, hr, sh[:8])
    return jnp.concatenate([top, sh[8:]], axis=0)


def _shift_up(x, halo, s):
    n = x.shape[0]
    sh = pltpu.roll(x, n - s, axis=0)
    hr = pltpu.roll(halo, 8 - s, axis=0)
    r8 = lax.broadcasted_iota(jnp.int32, hr.shape, 0)
    bot = jnp.where(r8 >= 8 - s, hr, sh[n - 8:])
    return jnp.concatenate([sh[:n - 8], bot], axis=0)


def _conv_pre(x, halo, w, b):
    y = x * w[3:4] + b
    for j in range(3):
        y = y + _shift_down(x, halo, 3 - j) * w[j:j + 1]
    return y


def conv_fwd(src, cb0, ncb, w, b, *, name):
    S_ = src.shape[0]
    T = min(CONV_T, S_)
    nt = S_ // T

    def body(x_ref, h_ref, w_ref, b_ref, o_ref):
        i = pl.program_id(1)
        halo = jnp.where(i > 0, h_ref[...], 0.0)
        y = _conv_pre(x_ref[...], halo, w_ref[...], b_ref[...])
        o_ref[...] = y * jax.nn.sigmoid(y)

    return pl.pallas_call(
        body, grid=(ncb, nt),
        in_specs=[pl.BlockSpec((T, 512), lambda c, i: (i, cb0 + c)),
                  pl.BlockSpec((8, 512), lambda c, i: (jnp.maximum(i * (T // 8) - 1, 0), cb0 + c)),
                  pl.BlockSpec((4, 512), lambda c, i: (0, c)), pl.BlockSpec((1, 512), lambda c, i: (0, c))],
        out_specs=pl.BlockSpec((T, 512), lambda c, i: (i, c)),
        out_shape=jax.ShapeDtypeStruct((S_, 512 * ncb), F32), compiler_params=_cp(("parallel", "parallel")), name=name,
    )(src, src, w, b)


def conv_bwd_pre(src, cb0, ncb, w, b, dy, *, name):
    S_ = src.shape[0]
    T = min(CONV_T, S_)
    nt = S_ // T

    def body(x_ref, h_ref, w_ref, b_ref, dy_ref, dp_ref, dw_ref, db_ref):
        i = pl.program_id(1)
        halo = jnp.where(i > 0, h_ref[...], 0.0)
        x = x_ref[...]
        y = _conv_pre(x, halo, w_ref[...], b_ref[...])
        sg = jax.nn.sigmoid(y)
        dpre = dy_ref[...] * (sg * (1.0 + y * (1.0 - sg)))
        dp_ref[...] = dpre
        rows = [jnp.sum(dpre * _shift_down(x, halo, 3 - j), axis=0, keepdims=True) for j in range(3)]
        rows.append(jnp.sum(dpre * x, axis=0, keepdims=True))
        dw = jnp.concatenate(rows, axis=0)
        db = jnp.sum(dpre, axis=0, keepdims=True)

        @pl.when(i == 0)
        def _():
            dw_ref[...] = dw
            db_ref[...] = db

        @pl.when(i > 0)
        def _():
            dw_ref[...] += dw
            db_ref[...] += db

    return pl.pallas_call(
        body, grid=(ncb, nt),
        in_specs=[pl.BlockSpec((T, 512), lambda c, i: (i, cb0 + c)),
                  pl.BlockSpec((8, 512), lambda c, i: (jnp.maximum(i * (T // 8) - 1, 0), cb0 + c)),
                  pl.BlockSpec((4, 512), lambda c, i: (0, c)), pl.BlockSpec((1, 512), lambda c, i: (0, c)),
                  pl.BlockSpec((T, 512), lambda c, i: (i, c))],
        out_specs=[pl.BlockSpec((T, 512), lambda c, i: (i, c)), pl.BlockSpec((4, 512), lambda c, i: (0, c)),
                   pl.BlockSpec((1, 512), lambda c, i: (0, c))],
        out_shape=[jax.ShapeDtypeStruct((S_, 512 * ncb), F32), jax.ShapeDtypeStruct((4, 512 * ncb), F32),
                   jax.ShapeDtypeStruct((1, 512 * ncb), F32)],
        compiler_params=_cp(("parallel", "arbitrary")), name=name,
    )(src, src, w, b, dy)


def conv_bwd_x(dpre, w, *, name, out_dtype=F32):
    S_, C = dpre.shape
    T = min(CONV_T, S_)
    nt = S_ // T
    ncb = C // 512

    def body(d_ref, h_ref, w_ref, o_ref):
        i = pl.program_id(1)
        halo = jnp.where(i < nt - 1, h_ref[...], 0.0)
        d = d_ref[...]
        w_ = w_ref[...]
        y = d * w_[3:4]
        for j in range(3):
            y = y + _shift_up(d, halo, 3 - j) * w_[j:j + 1]
        o_ref[...] = y.astype(o_ref.dtype)

    return pl.pallas_call(
        body, grid=(ncb, nt),
        in_specs=[pl.BlockSpec((T, 512), lambda c, i: (i, c)),
                  pl.BlockSpec((8, 512), lambda c, i: (jnp.minimum((i + 1) * (T // 8), S_ // 8 - 1), c)),
                  pl.BlockSpec((4, 512), lambda c, i: (0, c))],
        out_specs=pl.BlockSpec((T, 512), lambda c, i: (i, c)),
        out_shape=jax.ShapeDtypeStruct((S_, C), out_dtype), compiler_params=_cp(("parallel", "parallel")), name=name,
    )(dpre, dpre, w)


def ffn_mid_fwd(h, w1, w3, *, name):
    S_, D = h.shape
    F = w1.shape[1]
    tm, tn = _pick(S_, (1024, 512, 256)), 256

    def body(h_ref, w1_ref, w3_ref, a_ref):
        hb = h_ref[...]
        u = _dot(hb, w1_ref[...])
        v = _dot(hb, w3_ref[...])
        a_ref[...] = (u * jax.nn.sigmoid(u) * v).astype(a_ref.dtype)

    return pl.pallas_call(
        body, grid=(S_ // tm, F // tn),
        in_specs=[pl.BlockSpec((tm, D), lambda i, j: (i, 0)), pl.BlockSpec((D, tn), lambda i, j: (0, j)),
                  pl.BlockSpec((D, tn), lambda i, j: (0, j))],
        out_specs=pl.BlockSpec((tm, tn), lambda i, j: (i, j)), out_shape=jax.ShapeDtypeStruct((S_, F), BF16),
        compiler_params=_cp(("parallel", "parallel")), name=name,
    )(h, w1, w3)


def ffn_mid_bwd(h, dy, w1, w3, w2, *, name):
    S_, D = h.shape
    F = w1.shape[1]
    tm, tn = _pick(S_, (1024, 512, 256)), 256

    def body(h_ref, dy_ref, w1_ref, w3_ref, w2_ref, du_ref, dv_ref, a_ref):
        hb = h_ref[...]
        u = _dot(hb, w1_ref[...])
        v = _dot(hb, w3_ref[...])
        da = _dot(dy_ref[...], w2_ref[...], _NT)
        sg = jax.nn.sigmoid(u)
        si = u * sg
        a_ref[...] = (si * v).astype(a_ref.dtype)
        dv_ref[...] = (da * si).astype(dv_ref.dtype)
        du_ref[...] = (da * v * (sg * (1.0 + u * (1.0 - sg)))).astype(du_ref.dtype)

    o = pl.BlockSpec((tm, tn), lambda i, j: (i, j))
    return pl.pallas_call(
        body, grid=(S_ // tm, F // tn),
        in_specs=[pl.BlockSpec((tm, D), lambda i, j: (i, 0)), pl.BlockSpec((tm, D), lambda i, j: (i, 0)),
                  pl.BlockSpec((D, tn), lambda i, j: (0, j)), pl.BlockSpec((D, tn), lambda i, j: (0, j)),
                  pl.BlockSpec((tn, D), lambda i, j: (j, 0))],
        out_specs=[o, o, o], out_shape=[jax.ShapeDtypeStruct((S_, F), BF16)] * 3,
        compiler_params=_cp(("parallel", "parallel")), name=name,
    )(h, dy, w1, w3, w2)


MESH = pl.DeviceIdType.MESH
ANY = pl.BlockSpec(memory_space=pl.ANY)


def allgather8(x_shard, *, name):
    m_per, n = x_shard.shape

    def body(x_ref, out_ref, send_sems, recv_sems, local_sem):
        x, y, c = lax.axis_index("x"), lax.axis_index("y"), lax.axis_index("c")
        me, sibling = (x, y, c), (x, y, 1 - c)
        chips = [(1 - x, y), (x, 1 - y), (1 - x, 1 - y)]

        def rows(px, py, pc):
            return out_ref.at[pl.ds((4 * px + 2 * py + pc) * m_per, m_per), :]

        def copy(k, block, to, src=None):
            return pltpu.make_async_remote_copy(
                src_ref=rows(*block) if src is None else src, dst_ref=rows(*block),
                send_sem=send_sems.at[k], recv_sem=recv_sems.at[k], device_id=to, device_id_type=MESH)

        mine = pltpu.make_async_copy(x_ref, rows(*me), local_sem)
        mine.start()
        first = [copy(0, me, sibling, src=x_ref)]
        first += [copy(1 + j, me, (*chip, c), src=x_ref) for j, chip in enumerate(chips)]
        for cp in first:
            cp.start()
        passed = [copy(4 + j, (*chip, c), sibling) for j, chip in enumerate(chips)]
        for j, chip in enumerate(chips):
            copy(1 + j, (*chip, c), me).wait_recv()
            passed[j].start()
        copy(0, sibling, me).wait_recv()
        for j, chip in enumerate(chips):
            copy(4 + j, (*chip, 1 - c), me).wait_recv()
        for cp in first + passed:
            cp.wait_send()
        mine.wait()

    return pl.pallas_call(
        body, out_shape=jax.ShapeDtypeStruct((8 * m_per, n), x_shard.dtype),
        in_specs=[pl.BlockSpec(memory_space=pltpu.VMEM)], out_specs=pl.BlockSpec(memory_space=pltpu.VMEM),
        scratch_shapes=[pltpu.SemaphoreType.DMA((7,)), pltpu.SemaphoreType.DMA((7,)), pltpu.SemaphoreType.DMA],
        name=name,
    )(x_shard)


def _chip_peers():
    x, y, c = lax.axis_index("x"), lax.axis_index("y"), lax.axis_index("c")
    return x, y, c, [(1 - x, y), (x, 1 - y), (1 - x, 1 - y)]


def allgather_chips(x_shard, *, name):
    r, cdim = x_shard.shape

    def body(x_ref, out_ref, send_sems, recv_sems, local_sem):
        x, y, c, chips = _chip_peers()
        me = 2 * x + y
        mine = pltpu.make_async_copy(x_ref, out_ref.at[me], local_sem)
        mine.start()
        sends = []
        for k, (px, py) in enumerate(chips):
            cp = pltpu.make_async_remote_copy(src_ref=x_ref, dst_ref=out_ref.at[me], send_sem=send_sems.at[k],
                                              recv_sem=recv_sems.at[k], device_id=(px, py, c), device_id_type=MESH)
            cp.start()
            sends.append(cp)
        for k, (px, py) in enumerate(chips):
            pltpu.make_async_remote_copy(src_ref=x_ref, dst_ref=out_ref.at[2 * px + py], send_sem=send_sems.at[k],
                                         recv_sem=recv_sems.at[k], device_id=(px, py, c), device_id_type=MESH).wait_recv()
        for cp in sends:
            cp.wait_send()
        mine.wait()

    return pl.pallas_call(
        body, out_shape=jax.ShapeDtypeStruct((4, r, cdim), x_shard.dtype), in_specs=[ANY], out_specs=ANY,
        scratch_shapes=[pltpu.SemaphoreType.DMA((3,)), pltpu.SemaphoreType.DMA((3,)), pltpu.SemaphoreType.DMA],
        name=name,
    )(x_shard)


def exchange_chips(g, *, name):
    _, r, cdim = g.shape

    def body(g_ref, out_ref, send_sems, recv_sems, local_sem):
        x, y, c, chips = _chip_peers()
        me = 2 * x + y
        mine = pltpu.make_async_copy(g_ref.at[me], out_ref.at[me], local_sem)
        mine.start()
        sends = []
        for k, (px, py) in enumerate(chips):
            cp = pltpu.make_async_remote_copy(src_ref=g_ref.at[2 * px + py], dst_ref=out_ref.at[me], send_sem=send_sems.at[k],
                                              recv_sem=recv_sems.at[k], device_id=(px, py, c), device_id_type=MESH)
            cp.start()
            sends.append(cp)
        for k, (px, py) in enumerate(chips):
            pltpu.make_async_remote_copy(src_ref=g_ref.at[me], dst_ref=out_ref.at[2 * px + py], send_sem=send_sems.at[k],
                                         recv_sem=recv_sems.at[k], device_id=(px, py, c), device_id_type=MESH).wait_recv()
        for cp in sends:
            cp.wait_send()
        mine.wait()

    return pl.pallas_call(
        body, out_shape=jax.ShapeDtypeStruct(g.shape, g.dtype), in_specs=[ANY], out_specs=ANY,
        scratch_shapes=[pltpu.SemaphoreType.DMA((3,)), pltpu.SemaphoreType.DMA((3,)), pltpu.SemaphoreType.DMA],
        name=name,
    )(g)


def swap_sibling(p, *, name):
    def body(p_ref, out_ref, send_sem, recv_sem):
        x, y, c = lax.axis_index("x"), lax.axis_index("y"), lax.axis_index("c")
        cp = pltpu.make_async_remote_copy(src_ref=p_ref, dst_ref=out_ref, send_sem=send_sem, recv_sem=recv_sem,
                                          device_id=(x, y, 1 - c), device_id_type=MESH)
        cp.start()
        cp.wait()

    return pl.pallas_call(
        body, out_shape=jax.ShapeDtypeStruct(p.shape, p.dtype), in_specs=[ANY], out_specs=ANY,
        scratch_shapes=[pltpu.SemaphoreType.DMA, pltpu.SemaphoreType.DMA], name=name,
    )(p)


def sum_slots(r, *, name):
    n, rows, cdim = r.shape
    t = _pick(rows, (256, 128, 64, 32, 16, 8))

    def body(r_ref, o_ref):
        acc = r_ref[0].astype(F32)
        for s in range(1, n):
            acc = acc + r_ref[s].astype(F32)
        o_ref[...] = acc

    return pl.pallas_call(
        body, grid=(rows // t,), in_specs=[pl.BlockSpec((n, t, cdim), lambda i: (0, i, 0))],
        out_specs=pl.BlockSpec((t, cdim), lambda i: (i, 0)), out_shape=jax.ShapeDtypeStruct((rows, cdim), F32),
        compiler_params=_cp(("parallel",)), name=name,
    )(r)


def _rms(x, g):
    return x * lax.rsqrt(jnp.mean(x * x, axis=-1, keepdims=True) + NORM_EPS) * g


def _adaln(x, g, shift, scale):
    return _rms(x, g) * (1.0 + scale) + shift


def _silu(x):
    return x * jax.nn.sigmoid(x)


def _gdn_gate(o, z, g):
    return jnp.concatenate([_rms(o[:, 128 * h:128 * (h + 1)], g) * _silu(z[:, 128 * h:128 * (h + 1)]) for h in range(4)], axis=1)


def _ssd_gate(y, z0, z1, z2, z3, g):
    outs = []
    for k, z in enumerate((z0, z1, z2, z3)):
        t = y[:, 512 * k:512 * (k + 1)] * _silu(z)
        outs.append(t * lax.rsqrt(jnp.mean(t * t, axis=-1, keepdims=True) + NORM_EPS))
    return jnp.concatenate(outs, axis=1) * g


def _rope(x, cos, sin, rot):
    return x * cos + _dot(x, rot, hi=True) * sin


def _rope_q(q, cos, sin, rot):
    parts = []
    for h in range(4):
        parts += [q[:, 256 * h:256 * h + 128], _rope(q[:, 256 * h + 128:256 * (h + 1)], cos, sin, rot)]
    return jnp.concatenate(parts, axis=1) * ATT_SCALE


def _rope_t(d, cos, sin, rot):
    return d * cos + _dot(d * sin, rot, _NT, hi=True)


def _vjp_rows(fn, n_rows, n_pars, out_dtypes, rows, cts, pars, *, name, tile=256, extra=None):
    nct = len(cts)

    def bwd(*a):
        r, c, e, p = a[:n_rows], a[n_rows:n_rows + nct], a[n_rows + nct:len(a) - n_pars], a[len(a) - n_pars:]
        out, vjp = jax.vjp(fn, *[t.astype(F32) for t in r], *p)
        ct = tuple(t.astype(F32) for t in c)
        grads = vjp(ct[0] if not isinstance(out, tuple) else ct)
        drows = list(grads[:n_rows])
        if e:
            drows[0] = drows[0] + e[0]
        return (*drows, *grads[n_rows:])

    return rowmap(bwd, list(rows) + list(cts) + ([extra] if extra is not None else []), list(pars), out_dtypes,
                  name=name, tile=tile, n_reduce=n_pars)


ADAM_LR, ADAM_B1, ADAM_B2, ADAM_EPS, ADAM_WD, ADAM_STEP = 0.001, 0.9, 0.999, 1e-08, 0.01, 10


def _adam_math(w, g, m, v):
    m = ADAM_B1 * m + (1.0 - ADAM_B1) * g
    v = ADAM_B2 * v + (1.0 - ADAM_B2) * (g * g)
    m_hat = m / (1.0 - ADAM_B1 ** ADAM_STEP)
    v_hat = v / (1.0 - ADAM_B2 ** ADAM_STEP)
    delta = -ADAM_LR * (m_hat / (jnp.sqrt(v_hat) + ADAM_EPS) + ADAM_WD * w)
    return delta, m, v


def adamw(w, gs, m, v, *, name):
    shape = w.shape
    last = shape[-1]
    to2 = lambda a: a.reshape(-1, last)
    rows = w.size // last
    tile = _pick(rows, (256, 128, 64, 32, 16, 8))
    ng = len(gs)

    def fn(w_, *rest):
        g = rest[0]
        for t in rest[1:ng]:
            g = g + t
        m_, v_ = rest[ng], rest[ng + 1]
        return (g, *_adam_math(w_, g, m_, v_))

    outs = rowmap(fn, [to2(w)] + [to2(g) for g in gs] + [to2(m), to2(v)], [], (F32,) * 4, name=name, tile=tile)
    return tuple(o.reshape(shape) for o in outs)


PACK_W = 1024
BIG = (
    ("ffn_w1", (4, 2, 1024, 704), 3), ("ffn_w3", (4, 2, 1024, 704), 3), ("ffn_w2", (4, 2, 704, 1024), 2),
    ("ev_w_in", (2, 1024, 690), 2), ("mla_w_uq", (2, 96, 4, 192), 1), ("mla_w_ukv", (2, 64, 4, 256), 1),
    ("ev_w_out", (2, 256, 1024), 1), ("ssd_w_in", (2, 1024, 1288), 2), ("ssd_w_out", (2, 512, 1024), 1))


def _seg_rows(shape):
    n = math.prod(shape)
    return -(-n // (16 * PACK_W)) * 16


def _pack(shards, dtype):
    parts = []
    for (_, shape, _), a in zip(BIG, shards):
        flat = a.reshape(-1).astype(dtype)
        pad = _seg_rows(shape) * PACK_W - flat.shape[0]
        parts.append(jnp.pad(flat, (0, pad)) if pad else flat)
    return jnp.concatenate(parts).reshape(-1, PACK_W)


def _unpack(buf):
    out, r0 = [], 0
    for _, shape, _ in BIG:
        n = math.prod(shape)
        out.append(buf[r0:r0 + _seg_rows(shape)].reshape(-1)[:n].reshape(shape))
        r0 += _seg_rows(shape)
    return out


SMALL_SHARDED = (
    ("norm_g", (4, 3, 256), 2), ("gdn_conv_w", (2, 4, 384), 2), ("ssd_conv_w", (2, 4, 768), 2),
    ("ssd_conv_b", (2, 768), 1), ("ssd_norm_g", (2, 512), 1))


def _flat_pack(arrs, width, row_mult):
    flat = jnp.concatenate([a.reshape(-1).astype(F32) for a in arrs])
    n = flat.shape[0]
    tot = -(-n // (width * row_mult)) * width * row_mult
    return jnp.pad(flat, (0, tot - n)).reshape(-1, width)


def _flat_unpack(buf, shapes):
    flat = buf.reshape(-1)
    out, o = [], 0
    for s in shapes:
        n = math.prod(s)
        out.append(flat[o:o + n].reshape(s))
        o += n
    return out


def _rep(v, n):
    return jnp.repeat(v, n, axis=-1)


def kernel(x, c, positions, ada_w, ada_b, norm_g, ffn_w1, ffn_w3, ffn_w2, ev_w_in, gdn_conv_w, gdn_A_log, gdn_dt_bias, gdn_norm_g, mla_q_norm_g, mla_w_uq, mla_kv_norm_g, mla_w_ukv, ev_w_out, ssd_w_in, ssd_conv_w, ssd_conv_b, ssd_A_log, ssd_dt_bias, ssd_D, ssd_norm_g, ssd_w_out, final_g, loss_target, m_ada_w, m_ada_b, m_norm_g, m_ffn_w1, m_ffn_w3, m_ffn_w2, m_ev_w_in, m_gdn_conv_w, m_gdn_A_log, m_gdn_dt_bias, m_gdn_norm_g, m_mla_q_norm_g, m_mla_w_uq, m_mla_kv_norm_g, m_mla_w_ukv, m_ev_w_out, m_ssd_w_in, m_ssd_conv_w, m_ssd_conv_b, m_ssd_A_log, m_ssd_dt_bias, m_ssd_D, m_ssd_norm_g, m_ssd_w_out, m_final_g, v_ada_w, v_ada_b, v_norm_g, v_ffn_w1, v_ffn_w3, v_ffn_w2, v_ev_w_in, v_gdn_conv_w, v_gdn_A_log, v_gdn_dt_bias, v_gdn_norm_g, v_mla_q_norm_g, v_mla_w_uq, v_mla_kv_norm_g, v_mla_w_ukv, v_ev_w_out, v_ssd_w_in, v_ssd_conv_w, v_ssd_conv_b, v_ssd_A_log, v_ssd_dt_bias, v_ssd_D, v_ssd_norm_g, v_ssd_w_out, v_final_g):
    P = dict(ada_w=ada_w, ada_b=ada_b, norm_g=norm_g, ffn_w1=ffn_w1, ffn_w3=ffn_w3, ffn_w2=ffn_w2, ev_w_in=ev_w_in, gdn_conv_w=gdn_conv_w, gdn_A_log=gdn_A_log, gdn_dt_bias=gdn_dt_bias, gdn_norm_g=gdn_norm_g, mla_q_norm_g=mla_q_norm_g, mla_w_uq=mla_w_uq, mla_kv_norm_g=mla_kv_norm_g, mla_w_ukv=mla_w_ukv, ev_w_out=ev_w_out, ssd_w_in=ssd_w_in, ssd_conv_w=ssd_conv_w, ssd_conv_b=ssd_conv_b, ssd_A_log=ssd_A_log, ssd_dt_bias=ssd_dt_bias, ssd_D=ssd_D, ssd_norm_g=ssd_norm_g, ssd_w_out=ssd_w_out, final_g=final_g)
    M1 = dict(ada_w=m_ada_w, ada_b=m_ada_b, norm_g=m_norm_g, ffn_w1=m_ffn_w1, ffn_w3=m_ffn_w3, ffn_w2=m_ffn_w2, ev_w_in=m_ev_w_in, gdn_conv_w=m_gdn_conv_w, gdn_A_log=m_gdn_A_log, gdn_dt_bias=m_gdn_dt_bias, gdn_norm_g=m_gdn_norm_g, mla_q_norm_g=m_mla_q_norm_g, mla_w_uq=m_mla_w_uq, mla_kv_norm_g=m_mla_kv_norm_g, mla_w_ukv=m_mla_w_ukv, ev_w_out=m_ev_w_out, ssd_w_in=m_ssd_w_in, ssd_conv_w=m_ssd_conv_w, ssd_conv_b=m_ssd_conv_b, ssd_A_log=m_ssd_A_log, ssd_dt_bias=m_ssd_dt_bias, ssd_D=m_ssd_D, ssd_norm_g=m_ssd_norm_g, ssd_w_out=m_ssd_w_out, final_g=m_final_g)
    M2 = dict(ada_w=v_ada_w, ada_b=v_ada_b, norm_g=v_norm_g, ffn_w1=v_ffn_w1, ffn_w3=v_ffn_w3, ffn_w2=v_ffn_w2, ev_w_in=v_ev_w_in, gdn_conv_w=v_gdn_conv_w, gdn_A_log=v_gdn_A_log, gdn_dt_bias=v_gdn_dt_bias, gdn_norm_g=v_gdn_norm_g, mla_q_norm_g=v_mla_q_norm_g, mla_w_uq=v_mla_w_uq, mla_kv_norm_g=v_mla_kv_norm_g, mla_w_ukv=v_mla_w_ukv, ev_w_out=v_ev_w_out, ssd_w_in=v_ssd_w_in, ssd_conv_w=v_ssd_conv_w, ssd_conv_b=v_ssd_conv_b, ssd_A_log=v_ssd_A_log, ssd_dt_bias=v_ssd_dt_bias, ssd_D=v_ssd_D, ssd_norm_g=v_ssd_norm_g, ssd_w_out=v_ssd_w_out, final_g=v_final_g)
    names = list(P)
    xi, yi, ci = lax.axis_index("x"), lax.axis_index("y"), lax.axis_index("c")
    chip = 2 * xi + yi
    bidx = 4 * xi + 2 * yi + ci
    xa = x[0]
    S_, D = xa.shape
    tgt = loss_target[0]
    depth = ffn_w1.shape[0]

    wg = allgather_chips(_pack([P[n] for n, _, _ in BIG], BF16), name="gather_weights")
    per_chip = [_unpack(wg[s]) for s in range(4)]
    W = {n: jnp.concatenate([per_chip[s][k] for s in range(4)], axis=ax) for k, (n, _, ax) in enumerate(BIG)}
    sg = allgather_chips(_flat_pack([P[n] for n, _, _ in SMALL_SHARDED], 1024, 16), name="gather_small")
    per_chip_s = [_flat_unpack(sg[s], [sh for _, sh, _ in SMALL_SHARDED]) for s in range(4)]
    Wf = {n: jnp.concatenate([per_chip_s[s][k] for s in range(4)], axis=ax) for k, (n, _, ax) in enumerate(SMALL_SHARDED)}

    c_all = allgather8(jnp.pad(c, ((0, 7), (0, 0))), name="gather_c").reshape(8, 8, D)[:, 0]
    c_act, = rowmap(lambda t: (_silu(t),), [jnp.pad(c_all, ((0, 8), (0, 0)))], [], (F32,), name="c_act", tile=16)
    ncol = ada_w.shape[2]
    ada_b_loc = lax.dynamic_slice(ada_b, (0, chip * ncol), (depth, ncol))
    mod_loc = [mm((c_act, ada_w[l]), name=f"mod_{l}", epi=lambda acc, b: (acc + b,), epi_pars=(ada_b_loc[l][None],),
                  epi_out_dtypes=(F32,), tm=16, tn=256)[0][:8] for l in range(depth)]
    mod_g = allgather8(jnp.stack(mod_loc).reshape(-1, 1024), name="gather_mod").reshape(8, depth, 8, ncol)
    mod_b = lax.dynamic_index_in_dim(mod_g[0::2], bidx, axis=2, keepdims=False)
    mod = jnp.transpose(mod_b, (1, 0, 2)).reshape(depth, 3, 3, D)

    def ev_ext(w):
        z = lambda n: jnp.zeros((w.shape[0], n), w.dtype)
        return jnp.concatenate([w[:, :2048], _rep(w[:, 2048:2052], 128), _rep(w[:, 2052:2056], 128), w[:, 2056:2440], z(128),
                                w[:, 2440:2696], w[:, 2696:2760], z(192)], axis=1)

    def ev_ext_t(dw):
        return jnp.concatenate([dw[:, :2048], dw[:, 2048:2560].reshape(-1, 4, 128).sum(-1), dw[:, 2560:3072].reshape(-1, 4, 128).sum(-1),
                                dw[:, 3072:3456], dw[:, 3584:3840], dw[:, 3840:3904]], axis=1)

    def od_ext(w):
        return jnp.concatenate([w[:, 2048:5120], w[:, :2048], _rep(w[:, 5120:5152], 64)], axis=1)

    def od_ext_t(dw):
        return jnp.concatenate([dw[:, 3072:5120], dw[:, :3072], dw[:, 5120:].reshape(-1, 32, 64).sum(-1)], axis=1)

    def wq_ext(w):
        return jnp.pad(w, ((0, 0), (0, 0), (0, 64))).reshape(384, 1024)

    def wq_ext_t(dw):
        return dw.reshape(384, 4, 256)[:, :, :192]

    def wkv_ext(w):
        return jnp.concatenate([w[:, :, :128].reshape(256, 512), w[:, :, 128:].reshape(256, 512)], axis=1)

    def wkv_ext_t(dw):
        return jnp.concatenate([dw[:, :512].reshape(256, 4, 128), dw[:, 512:].reshape(256, 4, 128)], axis=2)

    half = 32
    inv_freq = 10000.0 ** (-jnp.arange(half, dtype=F32) / half)
    ang = positions[0].astype(F32)[:, None] * inv_freq
    zpad = jnp.zeros((S_, 64), F32)
    cos_t = jnp.concatenate([jnp.cos(ang), jnp.cos(ang), zpad], axis=1)
    sin_t = jnp.concatenate([jnp.sin(ang), jnp.sin(ang), zpad], axis=1)
    ii = jnp.arange(128)
    rot = (jnp.where((ii[:, None] < 32) & (ii[None, :] == ii[:, None] + 32), 1.0, 0.0)
           - jnp.where((ii[:, None] >= 32) & (ii[:, None] < 64) & (ii[None, :] == ii[:, None] - 32), 1.0, 0.0)).astype(F32)

    grads = {}
    dmod = [[[None] * 3 for _ in range(3)] for _ in range(depth)]
    dnorm_g = [[None] * 3 for _ in range(depth)]

    def acc(name, idx, val):
        grads.setdefault(name, {})[idx] = val

    def ffn_sub(xin, l, k, j):
        g, (shift, scale, gate) = Wf["norm_g"][l, k][None], [mod[l, k, t][None] for t in range(3)]
        w1, w3, w2 = W["ffn_w1"][l, j], W["ffn_w3"][l, j], W["ffn_w2"][l, j]
        tag = f"l{l}f{j}"
        h, = rowmap(lambda *a: (_adaln(*a),), [xin], [g, shift, scale], (BF16,), name=f"adaln_{tag}")
        a = ffn_mid_fwd(h, w1, w3, name=f"ffn_mid_{tag}")
        xn, y = mm((a, w2), name=f"ffn_out_{tag}", epi=lambda acc_, xr, gt: (xr + 0.5 * gt * acc_, acc_), epi_rows=(xin,),
                   epi_pars=(gate,), epi_out_dtypes=(F32, F32))

        def bwd(dxn):
            dy, dgate = rowmap(lambda d, y_, gt: ((0.5 * gt) * d, jnp.sum(0.5 * y_ * d, axis=0, keepdims=True)), [dxn, y], [gate],
                               (BF16, F32), name=f"dres_{tag}", n_reduce=1)
            du, dv, a_ = ffn_mid_bwd(h, dy, w1, w3, w2, name=f"ffn_midb_{tag}")
            dh = mm([(du, w1), (dv, w3)], tb=True, name=f"ffn_dh_{tag}")
            acc("ffn_w1", (l, j), mm((h, du), ta=True, name=f"ffn_dw1_{tag}"))
            acc("ffn_w3", (l, j), mm((h, dv), ta=True, name=f"ffn_dw3_{tag}"))
            acc("ffn_w2", (l, j), mm((a_, dy), ta=True, name=f"ffn_dw2_{tag}"))
            dx, dg, dsh, dsc = _vjp_rows(_adaln, 1, 3, (F32,), [xin], [dh], [g, shift, scale], name=f"adalnb_{tag}", extra=dxn)
            dnorm_g[l][k] = dg[0]
            dmod[l][k] = [dsh[0], dsc[0], dgate[0]]
            return dx

        return xn, bwd

    def mixer_tail(xin, l, tag, dh, dxn, g, shift, scale, dgate):
        dx, dg, dsh, dsc = _vjp_rows(_adaln, 1, 3, (F32,), [xin], [dh], [g, shift, scale], name=f"adalnb_{tag}", extra=dxn)
        dnorm_g[l][1] = dg[0]
        dmod[l][1] = [dsh[0], dsc[0], dgate[0]]
        return dx

    def even_sub(xin, l):
        e = l // 2
        tag = f"l{l}m"
        g, (shift, scale, gate) = Wf["norm_g"][l, 1][None], [mod[l, 1, t][None] for t in range(3)]
        wext, wq, wkv, wout = ev_ext(W["ev_w_in"][e]), wq_ext(W["mla_w_uq"][e]), wkv_ext(W["mla_w_ukv"][e]), W["ev_w_out"][e]
        conv_w, zb = Wf["gdn_conv_w"][e], jnp.zeros((1, 1536), F32)
        alog_e, dtb_e = _rep(gdn_A_log[e], 128)[None], _rep(gdn_dt_bias[e], 128)[None]
        gg, qg, kvg = gdn_norm_g[e][None], mla_q_norm_g[e][None], mla_kv_norm_g[e][None]
        h, = rowmap(lambda *a: (_adaln(*a),), [xin], [g, shift, scale], (BF16,), name=f"adaln_{tag}")
        proj = mm((h, wext), name=f"ev_in_{tag}")
        qkvc = conv_fwd(proj, 0, 3, conv_w, zb, name=f"gdn_conv_{tag}")
        o_g, hist = gdn_fwd(qkvc, proj, 4, 5, alog_e, dtb_e, name=f"gdn_{tag}")
        o_a, = rowmap(lambda o, z, g_: (_gdn_gate(o, z, g_),), [o_g, (proj, 512, 3)], [gg], (BF16,), name=f"gdn_gate_{tag}")
        cqn, = rowmap(lambda t, g_: (_rms(t, g_),), [(proj, 384, 8)], [qg], (BF16,), name=f"q_norm_{tag}")
        ckvn, = rowmap(lambda t, g_: (_rms(t, g_),), [(proj, 256, 14)], [kvg], (BF16,), name=f"kv_norm_{tag}")
        q0 = mm((cqn, wq), name=f"q_up_{tag}")
        kv = mm((ckvn, wkv), name=f"kv_up_{tag}", out_dtype=BF16)
        q, = rowmap(lambda t, cs, sn, r: (_rope_q(t, cs, sn, r),), [q0, cos_t, sin_t], [rot], (BF16,), name=f"rope_q_{tag}")
        kp, = rowmap(lambda t, cs, sn, r: (_rope(t, cs, sn, r),), [(proj, 128, 30), cos_t, sin_t], [rot], (BF16,), name=f"rope_k_{tag}")
        o_b, lse = att_fwd(q, kv, kp, name=f"att_{tag}")
        xn, y = mm([(o_a, wout[:512]), (o_b, wout[512:])], name=f"ev_out_{tag}", epi=lambda acc_, xr, gt: (xr + gt * acc_, acc_),
                   epi_rows=(xin,), epi_pars=(gate,), epi_out_dtypes=(F32, F32))

        def bwd(dxn):
            dy, dgate = rowmap(lambda d, y_, gt: (gt * d, jnp.sum(y_ * d, axis=0, keepdims=True)), [dxn, y], [gate],
                               (BF16, F32), name=f"dres_{tag}", n_reduce=1)
            do_a = mm((dy, wout[:512]), tb=True, name=f"ev_doa_{tag}")
            do_b = mm((dy, wout[512:]), tb=True, name=f"ev_dob_{tag}")
            acc("ev_w_out", e, jnp.concatenate([mm((o_a, dy), ta=True, name=f"ev_dwoa_{tag}"), mm((o_b, dy), ta=True, name=f"ev_dwob_{tag}")], axis=0))
            dsum, = rowmap(lambda d, o_: (jnp.concatenate([jnp.broadcast_to(jnp.sum(d[:, 128 * hh:128 * (hh + 1)] * o_[:, 128 * hh:128 * (hh + 1)],
                                                                                        axis=-1, keepdims=True), (d.shape[0], 128))
                                                  for hh in range(4)], axis=1),), [do_b, o_b], [], (F32,), name=f"att_dsum_{tag}")
            dq4, dk2, dv = att_bwd(q, kv, kp, lse, dsum, do_b, name=f"att_bwd_{tag}")

            def rope_qb(d0, d1, d2, d3, cs, sn, r):
                parts = []
                for d in (d0, d1, d2, d3):
                    parts += [d[:, :128], _rope_t(d[:, 128:], cs, sn, r)]
                return (jnp.concatenate(parts, axis=1) * ATT_SCALE,)

            dq0, = rowmap(rope_qb, [dq4[0], dq4[1], dq4[2], dq4[3], cos_t, sin_t], [rot], (BF16,), name=f"rope_qb_{tag}")

            def rope_kb(d, cs, sn, r):
                dkp = d[:, 128:256] + d[:, 384:512] + d[:, 640:768] + d[:, 896:1024]
                return jnp.concatenate([d[:, 256 * hh:256 * hh + 128] for hh in range(4)], axis=1), _rope_t(dkp, cs, sn, r)

            dkn, dkr = rowmap(rope_kb, [dk2, cos_t, sin_t], [rot], (BF16, BF16), name=f"rope_kb_{tag}")
            dcqn = mm((dq0, wq), tb=True, name=f"q_upb_{tag}")
            acc("mla_w_uq", e, wq_ext_t(mm((cqn, dq0), ta=True, name=f"q_dw_{tag}")))
            dckvn = mm([(dkn, wkv[:, :512]), (dv, wkv[:, 512:])], tb=True, name=f"kv_upb_{tag}")
            acc("mla_w_ukv", e, wkv_ext_t(jnp.concatenate([mm((ckvn, dkn), ta=True, name=f"kv_dwk_{tag}"), mm((ckvn, dv), ta=True, name=f"kv_dwv_{tag}")], axis=1)))
            dcq, dqg = _vjp_rows(_rms, 1, 1, (BF16,), [(proj, 384, 8)], [dcqn], [qg], name=f"q_normb_{tag}")
            dckv, dkvg = _vjp_rows(_rms, 1, 1, (BF16,), [(proj, 256, 14)], [dckvn], [kvg], name=f"kv_normb_{tag}")
            acc("mla_q_norm_g", e, dqg[0])
            acc("mla_kv_norm_g", e, dkvg[0])
            do_g, dz, dgg = _vjp_rows(_gdn_gate, 2, 1, (F32, BF16), [o_g, (proj, 512, 3)], [do_a], [gg], name=f"gdn_gateb_{tag}")
            acc("gdn_norm_g", e, dgg[0])
            dqkvc, dbe, dae, dal, ddt = gdn_bwd(qkvc, proj, 4, 5, alog_e, dtb_e, hist, do_g, name=f"gdnb_{tag}")
            acc("gdn_A_log", e, dal.reshape(4, 128).sum(-1))
            acc("gdn_dt_bias", e, ddt.reshape(4, 128).sum(-1))
            dpre, dcw, _ = conv_bwd_pre(proj, 0, 3, conv_w, zb, dqkvc, name=f"gdn_convb_{tag}")
            acc("gdn_conv_w", e, dcw)
            dqkv = conv_bwd_x(dpre, conv_w, name=f"gdn_convx_{tag}", out_dtype=BF16)
            zc = lambda n: jnp.zeros((S_, n), BF16)
            dproj = jnp.concatenate([dqkv, dz, dbe.astype(BF16), dae.astype(BF16), dcq, zc(128), dckv, dkr, zc(128)], axis=1)
            dh = mm((dproj, wext), tb=True, name=f"ev_inb_{tag}")
            acc("ev_w_in", e, ev_ext_t(mm((h, dproj), ta=True, name=f"ev_dwin_{tag}")))
            return mixer_tail(xin, l, tag, dh, dxn, g, shift, scale, dgate)

        return xn, bwd

    def odd_sub(xin, l):
        o = l // 2
        tag = f"l{l}m"
        g, (shift, scale, gate) = Wf["norm_g"][l, 1][None], [mod[l, 1, t][None] for t in range(3)]
        wext, wout = od_ext(W["ssd_w_in"][o]), W["ssd_w_out"][o]
        conv_w, conv_b, ng = Wf["ssd_conv_w"][o], Wf["ssd_conv_b"][o][None], Wf["ssd_norm_g"][o][None]
        ex = lambda v: _rep(v, 64).reshape(4, 1, 512)
        na_e, dtb_e, dsk_e = ex(-jnp.exp(ssd_A_log[o])), ex(ssd_dt_bias[o]), ex(ssd_D[o])
        h, = rowmap(lambda *a: (_adaln(*a),), [xin], [g, shift, scale], (BF16,), name=f"adaln_{tag}")
        proj = mm((h, wext), name=f"ssd_in_{tag}")
        zv = [(proj, 512, 6 + t) for t in range(4)]
        xbc = conv_fwd(proj, 0, 6, conv_w, conv_b, name=f"ssd_conv_{tag}")
        ys, hist = ssd_fwd(xbc, proj, na_e, dtb_e, dsk_e, name=f"ssd_{tag}")
        yn, = rowmap(lambda *a: (_ssd_gate(*a),), [ys] + zv, [ng], (BF16,), name=f"ssd_gate_{tag}", tile=128)
        xn, y = mm((yn, wout), name=f"ssd_out_{tag}", epi=lambda acc_, xr, gt: (xr + gt * acc_, acc_), epi_rows=(xin,),
                   epi_pars=(gate,), epi_out_dtypes=(F32, F32))

        def bwd(dxn):
            dy, dgate = rowmap(lambda d, y_, gt: (gt * d, jnp.sum(y_ * d, axis=0, keepdims=True)), [dxn, y], [gate],
                               (BF16, F32), name=f"dres_{tag}", n_reduce=1)
            dyn = mm((dy, wout), tb=True, name=f"ssd_dyn_{tag}", out_dtype=BF16)
            acc("ssd_w_out", o, mm((yn, dy), ta=True, name=f"ssd_dwout_{tag}"))
            dys, dz0, dz1, dz2, dz3, dng = _vjp_rows(_ssd_gate, 5, 1, (F32, BF16, BF16, BF16, BF16), [ys] + zv, [dyn], [ng],
                                                     name=f"ssd_gateb_{tag}", tile=128)
            acc("ssd_norm_g", o, dng[0])
            dxs, ddtx, db_, dc_, dna, ddtb, ddsk = ssd_bwd(xbc, proj, na_e, dtb_e, dsk_e, hist, dys, name=f"ssdb_{tag}")
            acc("ssd_A_log", o, dna.reshape(32, 64).sum(-1) * (-jnp.exp(ssd_A_log[o])))
            acc("ssd_dt_bias", o, ddtb.reshape(32, 64).sum(-1))
            acc("ssd_D", o, ddsk.reshape(32, 64).sum(-1))
            dxbc = jnp.concatenate([dxs, db_, dc_], axis=1)
            dpre, dcw, dcb = conv_bwd_pre(proj, 0, 6, conv_w, conv_b, dxbc, name=f"ssd_convb_{tag}")
            acc("ssd_conv_w", o, dcw)
            acc("ssd_conv_b", o, dcb[0])
            dxp = conv_bwd_x(dpre, conv_w, name=f"ssd_convx_{tag}", out_dtype=BF16)
            dproj = jnp.concatenate([dxp, dz0, dz1, dz2, dz3, ddtx.astype(BF16)], axis=1)
            dh = mm((dproj, wext), tb=True, name=f"ssd_inb_{tag}")
            acc("ssd_w_in", o, od_ext_t(mm((h, dproj), ta=True, name=f"ssd_dwin_{tag}")))
            return mixer_tail(xin, l, tag, dh, dxn, g, shift, scale, dgate)

        return xn, bwd

    tape = []
    xc = xa
    for l in range(depth):
        xc, b0 = ffn_sub(xc, l, 0, 0)
        xc, b1 = (even_sub if l % 2 == 0 else odd_sub)(xc, l)
        xc, b2 = ffn_sub(xc, l, 2, 1)
        tape += [b0, b1, b2]

    def head(xr, tg, g_):
        def f(xv, gv):
            err = _rms(xv, gv) - tg
            return 0.5 * jnp.sum(jnp.mean(err * err, axis=-1, keepdims=True), axis=0, keepdims=True)
        lo, vjp = jax.vjp(f, xr, g_)
        dxv, dgv = vjp(jnp.ones_like(lo))
        return dxv, jnp.broadcast_to(lo, (1, 128)), dgv

    dx, loss_p, dfg = rowmap(head, [xc, tgt], [final_g[None]], (F32,), name="loss_head", n_reduce=2)
    loss = lax.psum(loss_p[0, 0], ("x", "y", "c"))

    for b in reversed(tape):
        dx = b(dx)
    grad_x = dx[None]

    full = {n: jnp.stack([grads[n][k] for k in sorted(grads[n])]) for n in ("ev_w_in", "mla_w_uq", "mla_w_ukv", "ev_w_out", "ssd_w_in", "ssd_w_out")}
    for n in ("ffn_w1", "ffn_w3", "ffn_w2"):
        full[n] = jnp.stack([jnp.stack([grads[n][(l, j)] for j in range(2)]) for l in range(depth)])
    chunks = []
    for s in range(4):
        chunks.append(_pack([lax.slice_in_dim(full[n], s * sh[ax], (s + 1) * sh[ax], axis=ax) for n, sh, ax in BIG], BF16))
    recv = exchange_chips(jnp.stack(chunks), name="exchange_grads")
    part = sum_slots(recv, name="sum_chips")
    sib = swap_sibling(part, name="swap_sibling")
    g_mine, g_sib = _unpack(part), _unpack(sib)

    dmod_flat = jnp.stack([jnp.stack([jnp.stack(dmod[l][k]) for k in range(3)]) for l in range(depth)]).reshape(depth, 9 * D)
    small_names = ["norm_g", "gdn_conv_w", "gdn_A_log", "gdn_dt_bias", "gdn_norm_g", "mla_q_norm_g", "mla_kv_norm_g",
                   "ssd_conv_w", "ssd_conv_b", "ssd_A_log", "ssd_dt_bias", "ssd_D", "ssd_norm_g", "final_g"]
    small_full = {n: jnp.stack([grads[n][k] for k in sorted(grads[n])]) for n in small_names if n in grads}
    small_full["norm_g"] = jnp.stack([jnp.stack(dnorm_g[l]) for l in range(depth)])
    small_full["final_g"] = dfg[0]
    small_list = [dmod_flat] + [small_full[n] for n in small_names]
    small_shapes = [a.shape for a in small_list]
    sp = _flat_pack(small_list, 128, 8)
    sgath = allgather8(sp, name="gather_small_grads").reshape(8, sp.shape[0], 128)
    ssum = sum_slots(sgath, name="sum_small")
    tot = dict(zip(["ada_b"] + small_names, _flat_unpack(ssum, small_shapes)))
    dmod_all = sgath.reshape(8, -1)[:, :depth * 9 * D].reshape(8, depth, 9 * D)
    dmod_loc = lax.dynamic_slice(dmod_all, (0, 0, chip * ncol), (8, depth, ncol))
    g_ada_w = jnp.stack([mm((c_act, jnp.pad(dmod_loc[:, l], ((0, 8), (0, 0)))), ta=True, name=f"ada_dw_{l}", tk=16, tn=256)
                         for l in range(depth)])

    def own(n, a):
        for m_, sh, ax in SMALL_SHARDED:
            if m_ == n:
                return lax.dynamic_slice_in_dim(a, chip * sh[ax], sh[ax], axis=ax)
        return a

    res = {}
    for k, (n, _, _) in enumerate(BIG):
        res[n] = adamw(P[n], [g_mine[k], g_sib[k]], M1[n], M2[n], name=f"adamw_{n}")
    res["ada_w"] = adamw(ada_w, [g_ada_w], m_ada_w, v_ada_w, name="adamw_ada_w")
    sm = ["ada_b"] + small_names
    shapes = [P[n].shape for n in sm]
    pk = lambda d: _flat_pack([d[n] for n in sm], 128, 8)
    outs = adamw(pk(P), [pk({n: own(n, tot[n]).reshape(P[n].shape) for n in sm})], pk(M1), pk(M2), name="adamw_small")
    un = [_flat_unpack(o, shapes) for o in outs]
    for i, n in enumerate(sm):
        res[n] = tuple(un[t][i] for t in range(4))
    return (loss, grad_x, *[res[n][0] for n in names], *[res[n][1] for n in names], *[res[n][2] for n in names], *[res[n][3] for n in names])
```

```python
import functools
import math

import jax
import jax.numpy as jnp
from jax import lax
from jax.experimental import pallas as pl
from jax.experimental.pallas import tpu as pltpu

F32 = jnp.float32
BF16 = jnp.bfloat16
HI = lax.Precision.HIGHEST
HI3 = lax.Precision.HIGH
VMEM_LIMIT = 56 * 1024 * 1024
NORM_EPS = 1e-6
MM_VMEM_BUDGET = 40 * 1024 * 1024


def _cp(sem=None):
    if sem is None:
        return pltpu.CompilerParams(vmem_limit_bytes=VMEM_LIMIT)
    return pltpu.CompilerParams(dimension_semantics=sem, vmem_limit_bytes=VMEM_LIMIT)


def _pick(dim, prefs):
    for p in prefs:
        if dim % p == 0:
            return p
    return dim


def mm(pairs, *, ta=False, tb=False, out_dtype=F32, name, epi=None, epi_rows=(), epi_pars=(), epi_out_dtypes=None,
       tm=None, tn=None, tk=None):
    if not isinstance(pairs, (list, tuple)) or not isinstance(pairs[0], (list, tuple)):
        pairs = [pairs]
    npair = len(pairs)
    a0, b0 = pairs[0]
    M = a0.shape[1] if ta else a0.shape[0]
    K = a0.shape[0] if ta else a0.shape[1]
    N = b0.shape[0] if tb else b0.shape[1]
    for a, b in pairs:
        assert (a.shape == ((K, M) if ta else (M, K))), (a.shape, M, K)
        assert (b.shape == ((N, K) if tb else (K, N))), (b.shape, K, N)
    tm = tm or _pick(M, (1024, 1408, 512, 384, 256, 128))
    tk = tk or (K if K <= 1024 else _pick(K, (1024, 1408, 512, 256, 128)))
    if tn is None:
        n_epi_out = 1 if epi is None else len(epi_out_dtypes)
        for tn in (1024, 1408, 512, 384, 256, 128, N):
            if N % tn:
                continue
            need = sum(2 * tk * (tm * a.dtype.itemsize + tn * b.dtype.itemsize) for a, b in pairs)
            need += tm * tn * 4 * (1 + 2 * n_epi_out + 2 * len(epi_rows))
            if need <= MM_VMEM_BUDGET:
                break
    nk = K // tk
    assert M % tm == 0 and N % tn == 0 and K % tk == 0, (M, N, K, tm, tn, tk)
    n_rows, n_pars = len(epi_rows), len(epi_pars)
    if epi is None:
        out_dtypes = (out_dtype,)
    else:
        out_dtypes = tuple(epi_out_dtypes)
    n_out = len(out_dtypes)
    dn = (((0 if ta else 1,), (1 if tb else 0,)), ((), ()))

    def body(*refs):
        ab = refs[:2 * npair]
        rows = refs[2 * npair:2 * npair + n_rows]
        pars = refs[2 * npair + n_rows:2 * npair + n_rows + n_pars]
        outs = refs[2 * npair + n_rows + n_pars:2 * npair + n_rows + n_pars + n_out]
        acc_ref = refs[-1]
        k = pl.program_id(2)

        @pl.when(k == 0)
        def _():
            acc_ref[...] = jnp.zeros_like(acc_ref)

        acc = acc_ref[...]
        for p in range(npair):
            a = ab[2 * p][...].astype(BF16)
            b = ab[2 * p + 1][...].astype(BF16)
            acc = acc + lax.dot_general(a, b, dn, preferred_element_type=F32)
        acc_ref[...] = acc

        @pl.when(k == nk - 1)
        def _():
            r = acc_ref[...]
            if epi is None:
                outs[0][...] = r.astype(outs[0].dtype)
            else:
                res = epi(r, *[x[...] for x in rows], *[x[...] for x in pars])
                for o, v in zip(outs, res):
                    o[...] = v.astype(o.dtype)

    a_spec = pl.BlockSpec((tk, tm), lambda i, j, k: (k, i)) if ta else pl.BlockSpec((tm, tk), lambda i, j, k: (i, k))
    b_spec = pl.BlockSpec((tn, tk), lambda i, j, k: (j, k)) if tb else pl.BlockSpec((tk, tn), lambda i, j, k: (k, j))
    in_specs = []
    args = []
    for a, b in pairs:
        in_specs += [a_spec, b_spec]
        args += [a, b]
    for r in epi_rows:
        in_specs.append(pl.BlockSpec((tm, tn), lambda i, j, k: (i, j)))
        args.append(r)
    for p_ in epi_pars:
        in_specs.append(pl.BlockSpec((1, tn), lambda i, j, k: (0, j)))
        args.append(p_)
    out_specs = [pl.BlockSpec((tm, tn), lambda i, j, k: (i, j)) for _ in range(n_out)]
    out_shape = [jax.ShapeDtypeStruct((M, N), d) for d in out_dtypes]
    res = pl.pallas_call(
        body, grid=(M // tm, N // tn, nk), in_specs=in_specs, out_specs=out_specs, out_shape=out_shape,
        scratch_shapes=[pltpu.VMEM((tm, tn), F32)], compiler_params=_cp(("parallel", "parallel", "arbitrary")), name=name,
    )(*args)
    return res[0] if epi is None else tuple(res)


def rowmap(fn, rows, pars, out_dtypes, *, name, tile=256, n_reduce=0):
    views = []
    for r in rows:
        if isinstance(r, tuple):
            views.append(r)
        else:
            views.append((r, r.shape[1], 0))
    S = views[0][0].shape[0]
    assert S % tile == 0
    nt = S // tile
    row_structs = [jax.ShapeDtypeStruct((tile, w), a.dtype) for a, w, _ in views]
    par_structs = [jax.ShapeDtypeStruct(p.shape, p.dtype) for p in pars]
    out_structs = jax.eval_shape(fn, *row_structs, *par_structs)
    n_out = len(out_structs)
    n_row_out = n_out - n_reduce
    nr, npar = len(views), len(pars)

    def body(*refs):
        ins = [x[...] for x in refs[:nr + npar]]
        outs = refs[nr + npar:]
        res = fn(*ins)
        for o, v in zip(outs[:n_row_out], res[:n_row_out]):
            o[...] = v.astype(o.dtype)
        if n_reduce:
            i = pl.program_id(0)

            @pl.when(i == 0)
            def _():
                for o, v in zip(outs[n_row_out:], res[n_row_out:]):
                    o[...] = v.astype(o.dtype)

            @pl.when(i > 0)
            def _():
                for o, v in zip(outs[n_row_out:], res[n_row_out:]):
                    o[...] += v.astype(o.dtype)

    in_specs = [pl.BlockSpec((tile, w), functools.partial(lambda i, c: (i, c), c=c)) for _, w, c in views]
    in_specs += [pl.BlockSpec(p.shape, lambda i: (0, 0)) for p in pars]
    out_specs = [pl.BlockSpec((tile, s.shape[1]), lambda i: (i, 0)) for s in out_structs[:n_row_out]]
    out_specs += [pl.BlockSpec(s.shape, lambda i: (0, 0)) for s in out_structs[n_row_out:]]
    out_shape = [jax.ShapeDtypeStruct((S, s.shape[1]), d) for s, d in zip(out_structs[:n_row_out], out_dtypes[:n_row_out])]
    out_shape += [jax.ShapeDtypeStruct(s.shape, F32) for s in out_structs[n_row_out:]]
    res = pl.pallas_call(
        body, grid=(nt,), in_specs=in_specs, out_specs=out_specs, out_shape=out_shape,
        compiler_params=_cp(("arbitrary",) if n_reduce else ("parallel",)), name=name,
    )(*[v[0] for v in views], *pars)
    return tuple(res)


CH = 64


def _softplus(x):
    return jnp.where(x > 20.0, x, jnp.log(1.0 + jnp.exp(jnp.minimum(x, 20.0))))


def _dot(a, b, dn=(((1,), (0,)), ((), ())), hi=False):
    if hi:
        return lax.dot_general(a.astype(F32), b.astype(F32), dn, precision=HI if hi is True else hi, preferred_element_type=F32)
    return lax.dot_general(a.astype(BF16), b.astype(BF16), dn, preferred_element_type=F32)


_NT = (((1,), (1,)), ((), ()))
_TN = (((0,), (0,)), ((), ()))


def _chunk_consts():
    r = lax.broadcasted_iota(jnp.int32, (CH, 2 * CH), 0)
    c0 = lax.broadcasted_iota(jnp.int32, (CH, 2 * CH), 1)
    c = jnp.where(c0 >= CH, c0 - CH, c0)
    r1 = lax.broadcasted_iota(jnp.int32, (CH, CH), 0)
    c1 = lax.broadcasted_iota(jnp.int32, (CH, CH), 1)
    return dict(
        lower2=r >= c, strict2=r > c, U2=(r <= c).astype(F32), eye2=(r == c).astype(F32),
        L=(r1 >= c1).astype(F32), ones=jnp.ones((CH, CH), F32), Z=jnp.zeros((CH, 2 * CH), F32))


@jax.custom_vjp
def _tri_inv2(a2s, eye2, Z):
    def prod(x2, y):
        return _dot(x2, jnp.concatenate([y, Z], axis=0), hi=HI3)

    bs = [-a2 for a2 in a2s]
    ts = [eye2 + b for b in bs]
    for _ in range(5):
        bs = [prod(b, b) for b in bs]
        ts = [t + prod(t, b) for t, b in zip(ts, bs)]
    return tuple(ts)


def _tri_inv2_fwd(a2s, eye2, Z):
    ts = _tri_inv2(a2s, eye2, Z)
    return ts, (ts, eye2, Z)


def _tri_inv2_bwd(res, dts):
    ts, eye2, Z = res
    xs = [_dot(t2, dt2, _TN, hi=HI3)[:CH] for t2, dt2 in zip(ts, dts)]
    das = tuple(-_dot(x2, jnp.concatenate([t2, Z], axis=0), _NT, hi=HI3) for x2, t2 in zip(xs, ts))
    return das, jnp.zeros_like(eye2), jnp.zeros_like(Z)


_tri_inv2.defvjp(_tri_inv2_fwd, _tri_inv2_bwd)


def _gdn_heads(qs, ks, vs, bxs, axs, Ss, alogs, dtbs, cst):
    lower2, strict2, U2, eye2, L, ones, Z = (cst[n] for n in ("lower2", "strict2", "U2", "eye2", "L", "ones", "Z"))
    H = range(len(qs))

    def prod(x2, y):
        return _dot(x2, jnp.concatenate([y, Z], axis=0), hi=HI3)

    qn = [qs[h] * lax.rsqrt(jnp.sum(qs[h] * qs[h], axis=-1, keepdims=True) + NORM_EPS) * (128.0 ** -0.5) for h in H]
    kn = [ks[h] * lax.rsqrt(jnp.sum(ks[h] * ks[h], axis=-1, keepdims=True) + NORM_EPS) for h in H]
    beta = [jax.nn.sigmoid(bxs[h]) for h in H]
    g = [-jnp.exp(alogs[h]) * _softplus(axs[h] + dtbs[h]) for h in H]
    gc = [_dot(L, g[h], hi=HI3) for h in H]
    n2 = [_dot(ones, g[h] * U2, hi=HI3) for h in H]
    decay2 = [jnp.where(lower2, jnp.exp(jnp.where(lower2, gc[h] - n2[h], 0.0)), 0.0) for h in H]
    kb = [kn[h] * beta[h] for h in H]
    kn2 = [jnp.concatenate([kn[h], kn[h]], axis=0) for h in H]
    a2 = tuple(jnp.where(strict2, _dot(kb[h], kn2[h], _NT) * decay2[h], 0.0) for h in H)
    t2 = _tri_inv2(a2, eye2, Z)
    glast = [jnp.sum(g[h], axis=0, keepdims=True) for h in H]
    u = [prod(t2[h], vs[h] * beta[h]) for h in H]
    w = [prod(t2[h], kb[h] * jnp.exp(gc[h])) for h in H]
    attn2 = [jnp.where(lower2, _dot(qn[h], kn2[h], _NT) * decay2[h], 0.0) for h in H]
    k_end = [kn[h] * jnp.exp(glast[h] - gc[h]) for h in H]
    q_start = [qn[h] * jnp.exp(gc[h]) for h in H]
    v_new = [u[h] - _dot(w[h], Ss[h]) for h in H]
    o = [_dot(q_start[h], Ss[h]) + _dot(attn2[h], jnp.concatenate([v_new[h], Z], axis=0)) for h in H]
    s_new = [Ss[h] * jnp.exp(glast[h]) + _dot(k_end[h], v_new[h], _TN) for h in H]
    return tuple(o), tuple(s_new)


def gdn_fwd(qkv, proj, bcol, acol, alog_e, dtb_e, *, name):
    S_ = qkv.shape[0]
    nc = S_ // CH

    def body(q_ref, k_ref, v_ref, b_ref, a_ref, al_ref, dt_ref, o_ref, hist_ref, s_ref):
        i = pl.program_id(0)

        @pl.when(i == 0)
        def _():
            s_ref[...] = jnp.zeros_like(s_ref)

        cst = _chunk_consts()
        hist_ref[0] = s_ref[...]
        heads = [slice(128 * h, 128 * (h + 1)) for h in range(4)]
        rd = lambda ref: tuple(ref[:, ls] for ls in heads)
        os_, s_news = _gdn_heads(rd(q_ref), rd(k_ref), rd(v_ref), rd(b_ref), rd(a_ref), tuple(s_ref[ls, :] for ls in heads),
                                 rd(al_ref), rd(dt_ref), cst)
        for ls, o, s_new in zip(heads, os_, s_news):
            o_ref[:, ls] = o
            s_ref[ls, :] = s_new

    blk = lambda cb: pl.BlockSpec((CH, 512), functools.partial(lambda i, cb: (i, cb), cb=cb))
    par = pl.BlockSpec((1, 512), lambda i: (0, 0))
    return pl.pallas_call(
        body, grid=(nc,), in_specs=[blk(0), blk(1), blk(2), blk(bcol), blk(acol), par, par],
        out_specs=[pl.BlockSpec((CH, 512), lambda i: (i, 0)), pl.BlockSpec((1, 512, 128), lambda i: (i, 0, 0))],
        out_shape=[jax.ShapeDtypeStruct((S_, 512), F32), jax.ShapeDtypeStruct((nc, 512, 128), F32)],
        scratch_shapes=[pltpu.VMEM((512, 128), F32)], compiler_params=_cp(("arbitrary",)), name=name,
    )(qkv, qkv, qkv, proj, proj, alog_e, dtb_e)


def gdn_bwd(qkv, proj, bcol, acol, alog_e, dtb_e, hist, do, *, name):
    S_ = qkv.shape[0]
    nc = S_ // CH

    def body(q_ref, k_ref, v_ref, b_ref, a_ref, al_ref, dt_ref, hist_ref, do_ref, dqkv_ref, db_ref, da_ref, dal_ref, ddt_ref, ds_ref):
        i = pl.program_id(0)

        @pl.when(i == 0)
        def _():
            ds_ref[...] = jnp.zeros_like(ds_ref)
            dal_ref[...] = jnp.zeros_like(dal_ref)
            ddt_ref[...] = jnp.zeros_like(ddt_ref)

        cst = _chunk_consts()
        heads = [slice(128 * h, 128 * (h + 1)) for h in range(4)]
        rd = lambda ref: tuple(ref[:, ls] for ls in heads)
        fn = functools.partial(_gdn_heads, cst=cst)
        _, vjp = jax.vjp(fn, rd(q_ref), rd(k_ref), rd(v_ref), rd(b_ref), rd(a_ref), tuple(hist_ref[0, ls, :] for ls in heads),
                         rd(al_ref), rd(dt_ref))
        grads = vjp((rd(do_ref), tuple(ds_ref[ls, :] for ls in heads)))
        for h in range(4):
            ls = heads[h]
            dq, dk, dv, db, da, ds_in, dal, ddt = (t[h] for t in grads)
            dqkv_ref[:, 128 * h:128 * (h + 1)] = dq
            dqkv_ref[:, 512 + 128 * h:512 + 128 * (h + 1)] = dk
            dqkv_ref[:, 1024 + 128 * h:1024 + 128 * (h + 1)] = dv
            db_ref[:, ls] = db
            da_ref[:, ls] = da
            ds_ref[ls, :] = ds_in
            dal_ref[:, ls] += dal
            ddt_ref[:, ls] += ddt

    rblk = lambda cb: pl.BlockSpec((CH, 512), functools.partial(lambda i, cb: (nc - 1 - i, cb), cb=cb))
    par = pl.BlockSpec((1, 512), lambda i: (0, 0))
    return pl.pallas_call(
        body, grid=(nc,),
        in_specs=[rblk(0), rblk(1), rblk(2), rblk(bcol), rblk(acol), par, par,
                  pl.BlockSpec((1, 512, 128), lambda i: (nc - 1 - i, 0, 0)), rblk(0)],
        out_specs=[pl.BlockSpec((CH, 1536), lambda i: (nc - 1 - i, 0)), rblk(0), rblk(0), par, par],
        out_shape=[jax.ShapeDtypeStruct((S_, 1536), F32), jax.ShapeDtypeStruct((S_, 512), F32), jax.ShapeDtypeStruct((S_, 512), F32),
                   jax.ShapeDtypeStruct((1, 512), F32), jax.ShapeDtypeStruct((1, 512), F32)],
        scratch_shapes=[pltpu.VMEM((512, 128), F32)], compiler_params=_cp(("arbitrary",)), name=name,
    )(qkv, qkv, qkv, proj, proj, alog_e, dtb_e, hist, do)


def _ssd_group(xs, dtxs, bm, cm, hss, nas, dtbs, dsks, cst):
    lower2, U2, L, ones = cst["lower2"], cst["U2"], cst["L"], cst["ones"]
    lane = lax.broadcasted_iota(jnp.int32, (1, 2 * CH), 1)
    mask_l = (lane < CH).astype(F32)
    mask_r = 1.0 - mask_l
    ones_w = jnp.ones((CH, 2 * CH), F32)
    cb2 = _dot(cm, jnp.concatenate([bm, bm], axis=0), _NT)
    P_ = range(len(xs))
    dt = [_softplus(dtxs[p] + dtbs[p]) for p in P_]
    da = [dt[p] * nas[p] for p in P_]
    m = [_dot(L, da[p], hi=HI3) for p in P_]
    n2 = [_dot(ones, da[p] * U2, hi=HI3) for p in P_]
    lm2 = [jnp.where(lower2, jnp.exp(jnp.where(lower2, m[p] - n2[p], 0.0)), 0.0) for p in P_]
    xdt = [xs[p] * dt[p] for p in P_]
    x2 = [jnp.concatenate([xdt[p] * mask_l, xdt[p] * mask_r], axis=0) for p in P_]
    y_diag = [_dot(cb2 * lm2[p], x2[p]) for p in P_]
    alast = [jnp.sum(da[p], axis=0, keepdims=True) for p in P_]
    y_off = [_dot(cm, hss[p], _NT) * jnp.exp(m[p]) for p in P_]
    cd = [jnp.exp(_dot(da[p], ones_w, _TN, hi=HI3)) for p in P_]
    hs_new = [hss[p] * cd[p] + _dot(xdt[p] * jnp.exp(alast[p] - m[p]), bm, _TN) for p in P_]
    ys = [y_diag[p] + y_off[p] + dsks[p] * xs[p] for p in P_]
    return tuple(ys), tuple(hs_new)


def _ssd_specs(nc, rev):
    ci = (lambda i: nc - 1 - i) if rev else (lambda i: i)
    xg = pl.BlockSpec((CH, 512), lambda i, g: (ci(i), g))
    dtg = pl.BlockSpec((CH, 512), lambda i, g: (ci(i), 10 + g))
    bg = pl.BlockSpec((CH, 128), lambda i, g: (ci(i), 16 + g))
    cg = pl.BlockSpec((CH, 128), lambda i, g: (ci(i), 20 + g))
    par = pl.BlockSpec((4, 1, 512), lambda i, g: (0, 0, 0))
    hist = pl.BlockSpec((1, 512, 128), lambda i, g: (ci(i), g, 0))
    return xg, dtg, bg, cg, par, hist


def ssd_fwd(xbc, proj, na_e, dtb_e, dsk_e, *, name):
    S_ = xbc.shape[0]
    nc = S_ // CH
    xg, dtg, bg, cg, par, hist = _ssd_specs(nc, False)

    def body(x_ref, dt_ref, b_ref, c_ref, na_ref, dtb_ref, dsk_ref, y_ref, hist_ref, s_ref):
        i, g = pl.program_id(0), pl.program_id(1)

        @pl.when(i == 0)
        def _():
            s_ref[g] = jnp.zeros((512, 128), F32)

        cst = _chunk_consts()
        hist_ref[0] = s_ref[g]
        sl = [slice(128 * p, 128 * (p + 1)) for p in range(4)]
        na, dtb, dsk = na_ref[g], dtb_ref[g], dsk_ref[g]
        ys, hs_new = _ssd_group([x_ref[:, s] for s in sl], [dt_ref[:, s] for s in sl], b_ref[...], c_ref[...],
                                [s_ref[g, s, :] for s in sl], [na[:, s] for s in sl], [dtb[:, s] for s in sl],
                                [dsk[:, s] for s in sl], cst)
        for p, s in enumerate(sl):
            y_ref[:, s] = ys[p]
            s_ref[g, s, :] = hs_new[p]

    return pl.pallas_call(
        body, grid=(nc, 4), in_specs=[xg, dtg, bg, cg, par, par, par],
        out_specs=[pl.BlockSpec((CH, 512), lambda i, g: (i, g)), hist],
        out_shape=[jax.ShapeDtypeStruct((S_, 2048), F32), jax.ShapeDtypeStruct((nc, 2048, 128), F32)],
        scratch_shapes=[pltpu.VMEM((4, 512, 128), F32)], compiler_params=_cp(("arbitrary", "arbitrary")), name=name,
    )(xbc, proj, xbc, xbc, na_e, dtb_e, dsk_e)


def ssd_bwd(xbc, proj, na_e, dtb_e, dsk_e, hist, dy, *, name):
    S_ = xbc.shape[0]
    nc = S_ // CH
    xg, dtg, bg, cg, par, hist_spec = _ssd_specs(nc, True)
    og = pl.BlockSpec((CH, 512), lambda i, g: (nc - 1 - i, g))
    o128 = pl.BlockSpec((CH, 128), lambda i, g: (nc - 1 - i, g))

    def body(x_ref, dt_ref, b_ref, c_ref, na_ref, dtb_ref, dsk_ref, hist_ref, dy_ref,
             dx_ref, ddt_ref, db_ref, dc_ref, dna_ref, ddtb_ref, ddsk_ref, ds_ref):
        i, g = pl.program_id(0), pl.program_id(1)

        @pl.when(i == 0)
        def _():
            ds_ref[g] = jnp.zeros((512, 128), F32)
            dna_ref[g] = jnp.zeros((1, 512), F32)
            ddtb_ref[g] = jnp.zeros((1, 512), F32)
            ddsk_ref[g] = jnp.zeros((1, 512), F32)

        cst = _chunk_consts()
        sl = [slice(128 * p, 128 * (p + 1)) for p in range(4)]
        na, dtb, dsk = na_ref[g], dtb_ref[g], dsk_ref[g]
        fn = functools.partial(_ssd_group, cst=cst)
        _, vjp = jax.vjp(fn, [x_ref[:, s] for s in sl], [dt_ref[:, s] for s in sl], b_ref[...], c_ref[...],
                         [hist_ref[0, s, :] for s in sl], [na[:, s] for s in sl], [dtb[:, s] for s in sl], [dsk[:, s] for s in sl])
        dxs, ddts, db, dc, dhs, dnas, ddtbs, ddsks = vjp((tuple(dy_ref[:, s] for s in sl), tuple(ds_ref[g, s, :] for s in sl)))
        db_ref[...] = db
        dc_ref[...] = dc
        for p, s in enumerate(sl):
            dx_ref[:, s] = dxs[p]
            ddt_ref[:, s] = ddts[p]
            ds_ref[g, s, :] = dhs[p]
            dna_ref[g, :, s] += dnas[p]
            ddtb_ref[g, :, s] += ddtbs[p]
            ddsk_ref[g, :, s] += ddsks[p]

    return pl.pallas_call(
        body, grid=(nc, 4), in_specs=[xg, dtg, bg, cg, par, par, par, hist_spec, og],
        out_specs=[og, og, o128, o128, par, par, par],
        out_shape=[jax.ShapeDtypeStruct((S_, 2048), F32), jax.ShapeDtypeStruct((S_, 2048), F32),
                   jax.ShapeDtypeStruct((S_, 512), F32), jax.ShapeDtypeStruct((S_, 512), F32)] +
                  [jax.ShapeDtypeStruct((4, 1, 512), F32)] * 3,
        scratch_shapes=[pltpu.VMEM((4, 512, 128), F32)], compiler_params=_cp(("arbitrary", "arbitrary")), name=name,
    )(xbc, proj, xbc, xbc, na_e, dtb_e, dsk_e, hist, dy)


ATT_T = 1024
ATT_SCALE = 192.0 ** -0.5
NEG = -1e30


def _chunk_mask(shape):
    return lax.broadcasted_iota(jnp.int32, shape, 1) // CH <= lax.broadcasted_iota(jnp.int32, shape, 0) // CH


def att_fwd(q, kv, kp, *, name):
    S_ = q.shape[0]
    T = min(ATT_T, S_)
    n = S_ // T

    def body(q_ref, kn_ref, kp_ref, v_ref, o_ref, lse_ref, m_ref, l_ref, acc_ref):
        i, j = pl.program_id(1), pl.program_id(2)

        @pl.when(j == 0)
        def _():
            m_ref[...] = jnp.full_like(m_ref, NEG)
            l_ref[...] = jnp.zeros_like(l_ref)
            acc_ref[...] = jnp.zeros_like(acc_ref)

        def step(diag):
            k2 = jnp.concatenate([kn_ref[...], kp_ref[...]], axis=1)
            s = _dot(q_ref[...], k2, _NT)
            if diag:
                s = jnp.where(_chunk_mask(s.shape), s, NEG)
            m_prev = m_ref[...]
            m_cur = jnp.maximum(m_prev, jnp.max(s, axis=-1, keepdims=True))
            p = jnp.exp(s - m_cur[:, :1])
            alpha = jnp.exp(m_prev - m_cur)
            l_ref[...] = alpha * l_ref[...] + jnp.sum(p, axis=-1, keepdims=True)
            acc_ref[...] = acc_ref[...] * alpha + _dot(p, v_ref[...])
            m_ref[...] = m_cur

        @pl.when(j < i)
        def _():
            step(False)

        @pl.when(j == i)
        def _():
            step(True)
            o_ref[...] = acc_ref[...] / l_ref[...]
            lse_ref[...] = m_ref[...] + jnp.log(l_ref[...])

    return pl.pallas_call(
        body, grid=(4, n, n),
        in_specs=[pl.BlockSpec((T, 256), lambda h, i, j: (i, h)), pl.BlockSpec((T, 128), lambda h, i, j: (jnp.minimum(j, i), h)),
                  pl.BlockSpec((T, 128), lambda h, i, j: (jnp.minimum(j, i), 0)),
                  pl.BlockSpec((T, 128), lambda h, i, j: (jnp.minimum(j, i), 4 + h))],
        out_specs=[pl.BlockSpec((T, 128), lambda h, i, j: (i, h))] * 2,
        out_shape=[jax.ShapeDtypeStruct((S_, 512), F32), jax.ShapeDtypeStruct((S_, 512), F32)],
        scratch_shapes=[pltpu.VMEM((T, 128), F32)] * 3,
        compiler_params=_cp(("parallel", "parallel", "arbitrary")), name=name,
    )(q, kv, kp, kv)


def att_bwd(q, kv, kp, lse, dsum, do, *, name):
    S_ = q.shape[0]
    T = min(ATT_T, S_)
    n = S_ // T

    def body(q_ref, kn_ref, kp_ref, v_ref, lse_ref, d_ref, do_ref, dq_hbm, dk_ref, dv_ref, dq_acc, sem):
        h, j, i = pl.program_id(0), pl.program_id(1), pl.program_id(2)

        @pl.when((j == 0) & (i == 0))
        def _():
            dq_acc[...] = jnp.zeros_like(dq_acc)

        @pl.when(i == 0)
        def _():
            dk_ref[...] = jnp.zeros_like(dk_ref)
            dv_ref[...] = jnp.zeros_like(dv_ref)

        def step(diag):
            k2 = jnp.concatenate([kn_ref[...], kp_ref[...]], axis=1)
            qb = q_ref[...]
            dob = do_ref[...].astype(BF16)
            s = _dot(qb, k2, _NT)
            p = jnp.exp(s - lse_ref[:, :1])
            if diag:
                p = jnp.where(_chunk_mask(s.shape), p, 0.0)
            dv_ref[...] += _dot(p, dob, _TN)
            ds = (p * (_dot(dob, v_ref[...], _NT) - d_ref[:, :1])).astype(BF16)
            dk_ref[...] += _dot(ds, qb, _TN)
            rows = pl.ds(pl.multiple_of(i * T, T), T)
            dq_acc[rows, :] += _dot(ds, k2)

        @pl.when(i > j)
        def _():
            step(False)

        @pl.when(i == j)
        def _():
            step(True)

        @pl.when((j == n - 1) & (i == n - 1))
        def _():
            cp = pltpu.make_async_copy(dq_acc, dq_hbm.at[h], sem)
            cp.start()
            cp.wait()

    qmap = lambda h, j, i: (jnp.maximum(i, j), h)
    return pl.pallas_call(
        body, grid=(4, n, n),
        in_specs=[pl.BlockSpec((T, 256), qmap), pl.BlockSpec((T, 128), lambda h, j, i: (j, h)),
                  pl.BlockSpec((T, 128), lambda h, j, i: (j, 0)), pl.BlockSpec((T, 128), lambda h, j, i: (j, 4 + h)),
                  pl.BlockSpec((T, 128), qmap), pl.BlockSpec((T, 128), qmap), pl.BlockSpec((T, 128), qmap)],
        out_specs=[pl.BlockSpec(memory_space=pl.ANY), pl.BlockSpec((T, 256), lambda h, j, i: (j, h)),
                   pl.BlockSpec((T, 128), lambda h, j, i: (j, h))],
        out_shape=[jax.ShapeDtypeStruct((4, S_, 256), F32), jax.ShapeDtypeStruct((S_, 1024), F32), jax.ShapeDtypeStruct((S_, 512), F32)],
        scratch_shapes=[pltpu.VMEM((S_, 256), F32), pltpu.SemaphoreType.DMA],
        compiler_params=_cp(("arbitrary", "arbitrary", "arbitrary")), name=name,
    )(q, kv, kp, kv, lse, dsum, do)


CONV_T = 256


def _shift_down(x, halo, s):
    sh = pltpu.roll(x, s, axis=0)
    hr = pltpu.roll(halo, s, axis=0)
    r8 = lax.broadcasted_iota(jnp.int32, hr.shape, 0)
    top = jnp.where(r8 < s, hr, sh[:8])
    return jnp.concatenate([top, sh[8:]], axis=0)


def _shift_up(x, halo, s):
    n = x.shape[0]
    sh = pltpu.roll(x, n - s, axis=0)
    hr = pltpu.roll(halo, 8 - s, axis=0)
    r8 = lax.broadcasted_iota(jnp.int32, hr.shape, 0)
    bot = jnp.where(r8 >= 8 - s, hr, sh[n - 8:])
    return jnp.concatenate([sh[:n - 8], bot], axis=0)


def _conv_pre(x, halo, w, b):
    y = x * w[3:4] + b
    for j in range(3):
        y = y + _shift_down(x, halo, 3 - j) * w[j:j + 1]
    return y


def conv_fwd(src, cb0, ncb, w, b, *, name):
    S_ = src.shape[0]
    T = min(CONV_T, S_)
    nt = S_ // T

    def body(x_ref, h_ref, w_ref, b_ref, o_ref):
        i = pl.program_id(1)
        halo = jnp.where(i > 0, h_ref[...], 0.0)
        y = _conv_pre(x_ref[...], halo, w_ref[...], b_ref[...])
        o_ref[...] = y * jax.nn.sigmoid(y)

    return pl.pallas_call(
        body, grid=(ncb, nt),
        in_specs=[pl.BlockSpec((T, 512), lambda c, i: (i, cb0 + c)),
                  pl.BlockSpec((8, 512), lambda c, i: (jnp.maximum(i * (T // 8) - 1, 0), cb0 + c)),
                  pl.BlockSpec((4, 512), lambda c, i: (0, c)), pl.BlockSpec((1, 512), lambda c, i: (0, c))],
        out_specs=pl.BlockSpec((T, 512), lambda c, i: (i, c)),
        out_shape=jax.ShapeDtypeStruct((S_, 512 * ncb), F32), compiler_params=_cp(("parallel", "parallel")), name=name,
    )(src, src, w, b)


def conv_bwd_pre(src, cb0, ncb, w, b, dy, *, name):
    S_ = src.shape[0]
    T = min(CONV_T, S_)
    nt = S_ // T

    def body(x_ref, h_ref, w_ref, b_ref, dy_ref, dp_ref, dw_ref, db_ref):
        i = pl.program_id(1)
        halo = jnp.where(i > 0, h_ref[...], 0.0)
        x = x_ref[...]
        y = _conv_pre(x, halo, w_ref[...], b_ref[...])
        sg = jax.nn.sigmoid(y)
        dpre = dy_ref[...] * (sg * (1.0 + y * (1.0 - sg)))
        dp_ref[...] = dpre
        rows = [jnp.sum(dpre * _shift_down(x, halo, 3 - j), axis=0, keepdims=True) for j in range(3)]
        rows.append(jnp.sum(dpre * x, axis=0, keepdims=True))
        dw = jnp.concatenate(rows, axis=0)
        db = jnp.sum(dpre, axis=0, keepdims=True)

        @pl.when(i == 0)
        def _():
            dw_ref[...] = dw
            db_ref[...] = db

        @pl.when(i > 0)
        def _():
            dw_ref[...] += dw
            db_ref[...] += db

    return pl.pallas_call(
        body, grid=(ncb, nt),
        in_specs=[pl.BlockSpec((T, 512), lambda c, i: (i, cb0 + c)),
                  pl.BlockSpec((8, 512), lambda c, i: (jnp.maximum(i * (T // 8) - 1, 0), cb0 + c)),
                  pl.BlockSpec((4, 512), lambda c, i: (0, c)), pl.BlockSpec((1, 512), lambda c, i: (0, c)),
                  pl.BlockSpec((T, 512), lambda c, i: (i, c))],
        out_specs=[pl.BlockSpec((T, 512), lambda c, i: (i, c)), pl.BlockSpec((4, 512), lambda c, i: (0, c)),
                   pl.BlockSpec((1, 512), lambda c, i: (0, c))],
        out_shape=[jax.ShapeDtypeStruct((S_, 512 * ncb), F32), jax.ShapeDtypeStruct((4, 512 * ncb), F32),
                   jax.ShapeDtypeStruct((1, 512 * ncb), F32)],
        compiler_params=_cp(("parallel", "arbitrary")), name=name,
    )(src, src, w, b, dy)


def conv_bwd_x(dpre, w, *, name, out_dtype=F32):
    S_, C = dpre.shape
    T = min(CONV_T, S_)
    nt = S_ // T
    ncb = C // 512

    def body(d_ref, h_ref, w_ref, o_ref):
        i = pl.program_id(1)
        halo = jnp.where(i < nt - 1, h_ref[...], 0.0)
        d = d_ref[...]
        w_ = w_ref[...]
        y = d * w_[3:4]
        for j in range(3):
            y = y + _shift_up(d, halo, 3 - j) * w_[j:j + 1]
        o_ref[...] = y.astype(o_ref.dtype)

    return pl.pallas_call(
        body, grid=(ncb, nt),
        in_specs=[pl.BlockSpec((T, 512), lambda c, i: (i, c)),
                  pl.BlockSpec((8, 512), lambda c, i: (jnp.minimum((i + 1) * (T // 8), S_ // 8 - 1), c)),
                  pl.BlockSpec((4, 512), lambda c, i: (0, c))],
        out_specs=pl.BlockSpec((T, 512), lambda c, i: (i, c)),
        out_shape=jax.ShapeDtypeStruct((S_, C), out_dtype), compiler_params=_cp(("parallel", "parallel")), name=name,
    )(dpre, dpre, w)


def ffn_mid_fwd(h, w1, w3, *, name):
    S_, D = h.shape
    F = w1.shape[1]
    tm, tn = _pick(S_, (1024, 512, 256)), 256

    def body(h_ref, w1_ref, w3_ref, a_ref):
        hb = h_ref[...]
        u = _dot(hb, w1_ref[...])
        v = _dot(hb, w3_ref[...])
        a_ref[...] = (u * jax.nn.sigmoid(u) * v).astype(a_ref.dtype)

    return pl.pallas_call(
        body, grid=(S_ // tm, F // tn),
        in_specs=[pl.BlockSpec((tm, D), lambda i, j: (i, 0)), pl.BlockSpec((D, tn), lambda i, j: (0, j)),
                  pl.BlockSpec((D, tn), lambda i, j: (0, j))],
        out_specs=pl.BlockSpec((tm, tn), lambda i, j: (i, j)), out_shape=jax.ShapeDtypeStruct((S_, F), BF16),
        compiler_params=_cp(("parallel", "parallel")), name=name,
    )(h, w1, w3)


def ffn_mid_bwd(h, dy, w1, w3, w2, *, name):
    S_, D = h.shape
    F = w1.shape[1]
    tm, tn = _pick(S_, (1024, 512, 256)), 256

    def body(h_ref, dy_ref, w1_ref, w3_ref, w2_ref, du_ref, dv_ref, a_ref):
        hb = h_ref[...]
        u = _dot(hb, w1_ref[...])
        v = _dot(hb, w3_ref[...])
        da = _dot(dy_ref[...], w2_ref[...], _NT)
        sg = jax.nn.sigmoid(u)
        si = u * sg
        a_ref[...] = (si * v).astype(a_ref.dtype)
        dv_ref[...] = (da * si).astype(dv_ref.dtype)
        du_ref[...] = (da * v * (sg * (1.0 + u * (1.0 - sg)))).astype(du_ref.dtype)

    o = pl.BlockSpec((tm, tn), lambda i, j: (i, j))
    return pl.pallas_call(
        body, grid=(S_ // tm, F // tn),
        in_specs=[pl.BlockSpec((tm, D), lambda i, j: (i, 0)), pl.BlockSpec((tm, D), lambda i, j: (i, 0)),
                  pl.BlockSpec((D, tn), lambda i, j: (0, j)), pl.BlockSpec((D, tn), lambda i, j: (0, j)),
                  pl.BlockSpec((tn, D), lambda i, j: (j, 0))],
        out_specs=[o, o, o], out_shape=[jax.ShapeDtypeStruct((S_, F), BF16)] * 3,
        compiler_params=_cp(("parallel", "parallel")), name=name,
    )(h, dy, w1, w3, w2)


MESH = pl.DeviceIdType.MESH
ANY = pl.BlockSpec(memory_space=pl.ANY)


def allgather8(x_shard, *, name):
    m_per, n = x_shard.shape

    def body(x_ref, out_ref, send_sems, recv_sems, local_sem):
        x, y, c = lax.axis_index("x"), lax.axis_index("y"), lax.axis_index("c")
        me, sibling = (x, y, c), (x, y, 1 - c)
        chips = [(1 - x, y), (x, 1 - y), (1 - x, 1 - y)]

        def rows(px, py, pc):
            return out_ref.at[pl.ds((4 * px + 2 * py + pc) * m_per, m_per), :]

        def copy(k, block, to, src=None):
            return pltpu.make_async_remote_copy(
                src_ref=rows(*block) if src is None else src, dst_ref=rows(*block),
                send_sem=send_sems.at[k], recv_sem=recv_sems.at[k], device_id=to, device_id_type=MESH)

        mine = pltpu.make_async_copy(x_ref, rows(*me), local_sem)
        mine.start()
        first = [copy(0, me, sibling, src=x_ref)]
        first += [copy(1 + j, me, (*chip, c), src=x_ref) for j, chip in enumerate(chips)]
        for cp in first:
            cp.start()
        passed = [copy(4 + j, (*chip, c), sibling) for j, chip in enumerate(chips)]
        for j, chip in enumerate(chips):
            copy(1 + j, (*chip, c), me).wait_recv()
            passed[j].start()
        copy(0, sibling, me).wait_recv()
        for j, chip in enumerate(chips):
            copy(4 + j, (*chip, 1 - c), me).wait_recv()
        for cp in first + passed:
            cp.wait_send()
        mine.wait()

    return pl.pallas_call(
        body, out_shape=jax.ShapeDtypeStruct((8 * m_per, n), x_shard.dtype),
        in_specs=[pl.BlockSpec(memory_space=pltpu.VMEM)], out_specs=pl.BlockSpec(memory_space=pltpu.VMEM),
        scratch_shapes=[pltpu.SemaphoreType.DMA((7,)), pltpu.SemaphoreType.DMA((7,)), pltpu.SemaphoreType.DMA],
        name=name,
    )(x_shard)


def _chip_peers():
    x, y, c = lax.axis_index("x"), lax.axis_index("y"), lax.axis_index("c")
    return x, y, c, [(1 - x, y), (x, 1 - y), (1 - x, 1 - y)]


def allgather_chips(x_shard, *, name):
    r, cdim = x_shard.shape

    def body(x_ref, out_ref, send_sems, recv_sems, local_sem):
        x, y, c, chips = _chip_peers()
        me = 2 * x + y
        mine = pltpu.make_async_copy(x_ref, out_ref.at[me], local_sem)
        mine.start()
        sends = []
        for k, (px, py) in enumerate(chips):
            cp = pltpu.make_async_remote_copy(src_ref=x_ref, dst_ref=out_ref.at[me], send_sem=send_sems.at[k],
                                              recv_sem=recv_sems.at[k], device_id=(px, py, c), device_id_type=MESH)
            cp.start()
            sends.append(cp)
        for k, (px, py) in enumerate(chips):
            pltpu.make_async_remote_copy(src_ref=x_ref, dst_ref=out_ref.at[2 * px + py], send_sem=send_sems.at[k],
                                         recv_sem=recv_sems.at[k], device_id=(px, py, c), device_id_type=MESH).wait_recv()
        for cp in sends:
            cp.wait_send()
        mine.wait()

    return pl.pallas_call(
        body, out_shape=jax.ShapeDtypeStruct((4, r, cdim), x_shard.dtype), in_specs=[ANY], out_specs=ANY,
        scratch_shapes=[pltpu.SemaphoreType.DMA((3,)), pltpu.SemaphoreType.DMA((3,)), pltpu.SemaphoreType.DMA],
        name=name,
    )(x_shard)


def exchange_chips(g, *, name):
    _, r, cdim = g.shape

    def body(g_ref, out_ref, send_sems, recv_sems, local_sem):
        x, y, c, chips = _chip_peers()
        me = 2 * x + y
        mine = pltpu.make_async_copy(g_ref.at[me], out_ref.at[me], local_sem)
        mine.start()
        sends = []
        for k, (px, py) in enumerate(chips):
            cp = pltpu.make_async_remote_copy(src_ref=g_ref.at[2 * px + py], dst_ref=out_ref.at[me], send_sem=send_sems.at[k],
                                              recv_sem=recv_sems.at[k], device_id=(px, py, c), device_id_type=MESH)
            cp.start()
            sends.append(cp)
        for k, (px, py) in enumerate(chips):
            pltpu.make_async_remote_copy(src_ref=g_ref.at[me], dst_ref=out_ref.at[2 * px + py], send_sem=send_sems.at[k],
                                         recv_sem=recv_sems.at[k], device_id=(px, py, c), device_id_type=MESH).wait_recv()
        for cp in sends:
            cp.wait_send()
        mine.wait()

    return pl.pallas_call(
        body, out_shape=jax.ShapeDtypeStruct(g.shape, g.dtype), in_specs=[ANY], out_specs=ANY,
        scratch_shapes=[pltpu.SemaphoreType.DMA((3,)), pltpu.SemaphoreType.DMA((3,)), pltpu.SemaphoreType.DMA],
        name=name,
    )(g)


def swap_sibling(p, *, name):
    def body(p_ref, out_ref, send_sem, recv_sem):
        x, y, c = lax.axis_index("x"), lax.axis_index("y"), lax.axis_index("c")
        cp = pltpu.make_async_remote_copy(src_ref=p_ref, dst_ref=out_ref, send_sem=send_sem, recv_sem=recv_sem,
                                          device_id=(x, y, 1 - c), device_id_type=MESH)
        cp.start()
        cp.wait()

    return pl.pallas_call(
        body, out_shape=jax.ShapeDtypeStruct(p.shape, p.dtype), in_specs=[ANY], out_specs=ANY,
        scratch_shapes=[pltpu.SemaphoreType.DMA, pltpu.SemaphoreType.DMA], name=name,
    )(p)


def sum_slots(r, *, name):
    n, rows, cdim = r.shape
    t = _pick(rows, (256, 128, 64, 32, 16, 8))

    def body(r_ref, o_ref):
        acc = r_ref[0].astype(F32)
        for s in range(1, n):
            acc = acc + r_ref[s].astype(F32)
        o_ref[...] = acc

    return pl.pallas_call(
        body, grid=(rows // t,), in_specs=[pl.BlockSpec((n, t, cdim), lambda i: (0, i, 0))],
        out_specs=pl.BlockSpec((t, cdim), lambda i: (i, 0)), out_shape=jax.ShapeDtypeStruct((rows, cdim), F32),
        compiler_params=_cp(("parallel",)), name=name,
    )(r)


def _rms(x, g):
    return x * lax.rsqrt(jnp.mean(x * x, axis=-1, keepdims=True) + NORM_EPS) * g


def _adaln(x, g, shift, scale):
    return _rms(x, g) * (1.0 + scale) + shift


def _silu(x):
    return x * jax.nn.sigmoid(x)


def _gdn_gate(o, z, g):
    return jnp.concatenate([_rms(o[:, 128 * h:128 * (h + 1)], g) * _silu(z[:, 128 * h:128 * (h + 1)]) for h in range(4)], axis=1)


def _ssd_gate(y, z0, z1, z2, z3, g):
    outs = []
    for k, z in enumerate((z0, z1, z2, z3)):
        t = y[:, 512 * k:512 * (k + 1)] * _silu(z)
        outs.append(t * lax.rsqrt(jnp.mean(t * t, axis=-1, keepdims=True) + NORM_EPS))
    return jnp.concatenate(outs, axis=1) * g


def _rope(x, cos, sin, rot):
    return x * cos + _dot(x, rot, hi=True) * sin


def _rope_q(q, cos, sin, rot):
    parts = []
    for h in range(4):
        parts += [q[:, 256 * h:256 * h + 128], _rope(q[:, 256 * h + 128:256 * (h + 1)], cos, sin, rot)]
    return jnp.concatenate(parts, axis=1) * ATT_SCALE


def _rope_t(d, cos, sin, rot):
    return d * cos + _dot(d * sin, rot, _NT, hi=True)


def _vjp_rows(fn, n_rows, n_pars, out_dtypes, rows, cts, pars, *, name, tile=256, extra=None):
    nct = len(cts)

    def bwd(*a):
        r, c, e, p = a[:n_rows], a[n_rows:n_rows + nct], a[n_rows + nct:len(a) - n_pars], a[len(a) - n_pars:]
        out, vjp = jax.vjp(fn, *[t.astype(F32) for t in r], *p)
        ct = tuple(t.astype(F32) for t in c)
        grads = vjp(ct[0] if not isinstance(out, tuple) else ct)
        drows = list(grads[:n_rows])
        if e:
            drows[0] = drows[0] + e[0]
        return (*drows, *grads[n_rows:])

    return rowmap(bwd, list(rows) + list(cts) + ([extra] if extra is not None else []), list(pars), out_dtypes,
                  name=name, tile=tile, n_reduce=n_pars)


ADAM_LR, ADAM_B1, ADAM_B2, ADAM_EPS, ADAM_WD, ADAM_STEP = 0.001, 0.9, 0.999, 1e-08, 0.01, 10


def _adam_math(w, g, m, v):
    m = ADAM_B1 * m + (1.0 - ADAM_B1) * g
    v = ADAM_B2 * v + (1.0 - ADAM_B2) * (g * g)
    m_hat = m / (1.0 - ADAM_B1 ** ADAM_STEP)
    v_hat = v / (1.0 - ADAM_B2 ** ADAM_STEP)
    delta = -ADAM_LR * (m_hat / (jnp.sqrt(v_hat) + ADAM_EPS) + ADAM_WD * w)
    return delta, m, v


def adamw(w, gs, m, v, *, name):
    shape = w.shape
    last = shape[-1]
    to2 = lambda a: a.reshape(-1, last)
    rows = w.size // last
    tile = _pick(rows, (256, 128, 64, 32, 16, 8))
    ng = len(gs)

    def fn(w_, *rest):
        g = rest[0]
        for t in rest[1:ng]:
            g = g + t
        m_, v_ = rest[ng], rest[ng + 1]
        return (g, *_adam_math(w_, g, m_, v_))

    outs = rowmap(fn, [to2(w)] + [to2(g) for g in gs] + [to2(m), to2(v)], [], (F32,) * 4, name=name, tile=tile)
    return tuple(o.reshape(shape) for o in outs)


PACK_W = 1024
BIG = (
    ("ffn_w1", (4, 2, 1024, 704), 3), ("ffn_w3", (4, 2, 1024, 704), 3), ("ffn_w2", (4, 2, 704, 1024), 2),
    ("ev_w_in", (2, 1024, 690), 2), ("mla_w_uq", (2, 96, 4, 192), 1), ("mla_w_ukv", (2, 64, 4, 256), 1),
    ("ev_w_out", (2, 256, 1024), 1), ("ssd_w_in", (2, 1024, 1288), 2), ("ssd_w_out", (2, 512, 1024), 1))


def _seg_rows(shape):
    n = math.prod(shape)
    return -(-n // (16 * PACK_W)) * 16


def _pack(shards, dtype):
    parts = []
    for (_, shape, _), a in zip(BIG, shards):
        flat = a.reshape(-1).astype(dtype)
        pad = _seg_rows(shape) * PACK_W - flat.shape[0]
        parts.append(jnp.pad(flat, (0, pad)) if pad else flat)
    return jnp.concatenate(parts).reshape(-1, PACK_W)


def _unpack(buf):
    out, r0 = [], 0
    for _, shape, _ in BIG:
        n = math.prod(shape)
        out.append(buf[r0:r0 + _seg_rows(shape)].reshape(-1)[:n].reshape(shape))
        r0 += _seg_rows(shape)
    return out


SMALL_SHARDED = (
    ("norm_g", (4, 3, 256), 2), ("gdn_conv_w", (2, 4, 384), 2), ("ssd_conv_w", (2, 4, 768), 2),
    ("ssd_conv_b", (2, 768), 1), ("ssd_norm_g", (2, 512), 1))


def _flat_pack(arrs, width, row_mult):
    flat = jnp.concatenate([a.reshape(-1).astype(F32) for a in arrs])
    n = flat.shape[0]
    tot = -(-n // (width * row_mult)) * width * row_mult
    return jnp.pad(flat, (0, tot - n)).reshape(-1, width)


def _flat_unpack(buf, shapes):
    flat = buf.reshape(-1)
    out, o = [], 0
    for s in shapes:
        n = math.prod(s)
        out.append(flat[o:o + n].reshape(s))
        o += n
    return out


def _rep(v, n):
    return jnp.repeat(v, n, axis=-1)


def kernel(x, c, positions, ada_w, ada_b, norm_g, ffn_w1, ffn_w3, ffn_w2, ev_w_in, gdn_conv_w, gdn_A_log, gdn_dt_bias, gdn_norm_g, mla_q_norm_g, mla_w_uq, mla_kv_norm_g, mla_w_ukv, ev_w_out, ssd_w_in, ssd_conv_w, ssd_conv_b, ssd_A_log, ssd_dt_bias, ssd_D, ssd_norm_g, ssd_w_out, final_g, loss_target, m_ada_w, m_ada_b, m_norm_g, m_ffn_w1, m_ffn_w3, m_ffn_w2, m_ev_w_in, m_gdn_conv_w, m_gdn_A_log, m_gdn_dt_bias, m_gdn_norm_g, m_mla_q_norm_g, m_mla_w_uq, m_mla_kv_norm_g, m_mla_w_ukv, m_ev_w_out, m_ssd_w_in, m_ssd_conv_w, m_ssd_conv_b, m_ssd_A_log, m_ssd_dt_bias, m_ssd_D, m_ssd_norm_g, m_ssd_w_out, m_final_g, v_ada_w, v_ada_b, v_norm_g, v_ffn_w1, v_ffn_w3, v_ffn_w2, v_ev_w_in, v_gdn_conv_w, v_gdn_A_log, v_gdn_dt_bias, v_gdn_norm_g, v_mla_q_norm_g, v_mla_w_uq, v_mla_kv_norm_g, v_mla_w_ukv, v_ev_w_out, v_ssd_w_in, v_ssd_conv_w, v_ssd_conv_b, v_ssd_A_log, v_ssd_dt_bias, v_ssd_D, v_ssd_norm_g, v_ssd_w_out, v_final_g):
    P = dict(ada_w=ada_w, ada_b=ada_b, norm_g=norm_g, ffn_w1=ffn_w1, ffn_w3=ffn_w3, ffn_w2=ffn_w2, ev_w_in=ev_w_in, gdn_conv_w=gdn_conv_w, gdn_A_log=gdn_A_log, gdn_dt_bias=gdn_dt_bias, gdn_norm_g=gdn_norm_g, mla_q_norm_g=mla_q_norm_g, mla_w_uq=mla_w_uq, mla_kv_norm_g=mla_kv_norm_g, mla_w_ukv=mla_w_ukv, ev_w_out=ev_w_out, ssd_w_in=ssd_w_in, ssd_conv_w=ssd_conv_w, ssd_conv_b=ssd_conv_b, ssd_A_log=ssd_A_log, ssd_dt_bias=ssd_dt_bias, ssd_D=ssd_D, ssd_norm_g=ssd_norm_g, ssd_w_out=ssd_w_out, final_g=final_g)
    M1 = dict(ada_w=m_ada_w, ada_b=m_ada_b, norm_g=m_norm_g, ffn_w1=m_ffn_w1, ffn_w3=m_ffn_w3, ffn_w2=m_ffn_w2, ev_w_in=m_ev_w_in, gdn_conv_w=m_gdn_conv_w, gdn_A_log=m_gdn_A_log, gdn_dt_bias=m_gdn_dt_bias, gdn_norm_g=m_gdn_norm_g, mla_q_norm_g=m_mla_q_norm_g, mla_w_uq=m_mla_w_uq, mla_kv_norm_g=m_mla_kv_norm_g, mla_w_ukv=m_mla_w_ukv, ev_w_out=m_ev_w_out, ssd_w_in=m_ssd_w_in, ssd_conv_w=m_ssd_conv_w, ssd_conv_b=m_ssd_conv_b, ssd_A_log=m_ssd_A_log, ssd_dt_bias=m_ssd_dt_bias, ssd_D=m_ssd_D, ssd_norm_g=m_ssd_norm_g, ssd_w_out=m_ssd_w_out, final_g=m_final_g)
    M2 = dict(ada_w=v_ada_w, ada_b=v_ada_b, norm_g=v_norm_g, ffn_w1=v_ffn_w1, ffn_w3=v_ffn_w3, ffn_w2=v_ffn_w2, ev_w_in=v_ev_w_in, gdn_conv_w=v_gdn_conv_w, gdn_A_log=v_gdn_A_log, gdn_dt_bias=v_gdn_dt_bias, gdn_norm_g=v_gdn_norm_g, mla_q_norm_g=v_mla_q_norm_g, mla_w_uq=v_mla_w_uq, mla_kv_norm_g=v_mla_kv_norm_g, mla_w_ukv=v_mla_w_ukv, ev_w_out=v_ev_w_out, ssd_w_in=v_ssd_w_in, ssd_conv_w=v_ssd_conv_w, ssd_conv_b=v_ssd_conv_b, ssd_A_log=v_ssd_A_log, ssd_dt_bias=v_ssd_dt_bias, ssd_D=v_ssd_D, ssd_norm_g=v_ssd_norm_g, ssd_w_out=v_ssd_w_out, final_g=v_final_g)
    names = list(P)
    xi, yi, ci = lax.axis_index("x"), lax.axis_index("y"), lax.axis_index("c")
    chip = 2 * xi + yi
    bidx = 4 * xi + 2 * yi + ci
    xa = x[0]
    S_, D = xa.shape
    tgt = loss_target[0]
    depth = ffn_w1.shape[0]

    wg = allgather_chips(_pack([P[n] for n, _, _ in BIG], BF16), name="gather_weights")
    per_chip = [_unpack(wg[s]) for s in range(4)]
    W = {n: jnp.concatenate([per_chip[s][k] for s in range(4)], axis=ax) for k, (n, _, ax) in enumerate(BIG)}
    sg = allgather_chips(_flat_pack([P[n] for n, _, _ in SMALL_SHARDED], 1024, 16), name="gather_small")
    per_chip_s = [_flat_unpack(sg[s], [sh for _, sh, _ in SMALL_SHARDED]) for s in range(4)]
    Wf = {n: jnp.concatenate([per_chip_s[s][k] for s in range(4)], axis=ax) for k, (n, _, ax) in enumerate(SMALL_SHARDED)}

    c_all = allgather8(jnp.pad(c, ((0, 7), (0, 0))), name="gather_c").reshape(8, 8, D)[:, 0]
    c_act, = rowmap(lambda t: (_silu(t),), [jnp.pad(c_all, ((0, 8), (0, 0)))], [], (F32,), name="c_act", tile=16)
    ncol = ada_w.shape[2]
    ada_b_loc = lax.dynamic_slice(ada_b, (0, chip * ncol), (depth, ncol))
    mod_loc = [mm((c_act, ada_w[l]), name=f"mod_{l}", epi=lambda acc, b: (acc + b,), epi_pars=(ada_b_loc[l][None],),
                  epi_out_dtypes=(F32,), tm=16, tn=256)[0][:8] for l in range(depth)]
    mod_g = allgather8(jnp.stack(mod_loc).reshape(-1, 1024), name="gather_mod").reshape(8, depth, 8, ncol)
    mod_b = lax.dynamic_index_in_dim(mod_g[0::2], bidx, axis=2, keepdims=False)
    mod = jnp.transpose(mod_b, (1, 0, 2)).reshape(depth, 3, 3, D)

    def ev_ext(w):
        z = lambda n: jnp.zeros((w.shape[0], n), w.dtype)
        return jnp.concatenate([w[:, :2048], _rep(w[:, 2048:2052], 128), _rep(w[:, 2052:2056], 128), w[:, 2056:2440], z(128),
                                w[:, 2440:2696], w[:, 2696:2760], z(192)], axis=1)

    def ev_ext_t(dw):
        return jnp.concatenate([dw[:, :2048], dw[:, 2048:2560].reshape(-1, 4, 128).sum(-1), dw[:, 2560:3072].reshape(-1, 4, 128).sum(-1),
                                dw[:, 3072:3456], dw[:, 3584:3840], dw[:, 3840:3904]], axis=1)

    def od_ext(w):
        return jnp.concatenate([w[:, 2048:5120], w[:, :2048], _rep(w[:, 5120:5152], 64)], axis=1)

    def od_ext_t(dw):
        return jnp.concatenate([dw[:, 3072:5120], dw[:, :3072], dw[:, 5120:].reshape(-1, 32, 64).sum(-1)], axis=1)

    def wq_ext(w):
        return jnp.pad(w, ((0, 0), (0, 0), (0, 64))).reshape(384, 1024)

    def wq_ext_t(dw):
        return dw.reshape(384, 4, 256)[:, :, :192]

    def wkv_ext(w):
        return jnp.concatenate([w[:, :, :128].reshape(256, 512), w[:, :, 128:].reshape(256, 512)], axis=1)

    def wkv_ext_t(dw):
        return jnp.concatenate([dw[:, :512].reshape(256, 4, 128), dw[:, 512:].reshape(256, 4, 128)], axis=2)

    half = 32
    inv_freq = 10000.0 ** (-jnp.arange(half, dtype=F32) / half)
    ang = positions[0].astype(F32)[:, None] * inv_freq
    zpad = jnp.zeros((S_, 64), F32)
    cos_t = jnp.concatenate([jnp.cos(ang), jnp.cos(ang), zpad], axis=1)
    sin_t = jnp.concatenate([jnp.sin(ang), jnp.sin(ang), zpad], axis=1)
    ii = jnp.arange(128)
    rot = (jnp.where((ii[:, None] < 32) & (ii[None, :] == ii[:, None] + 32), 1.0, 0.0)
           - jnp.where((ii[:, None] >= 32) & (ii[:, None] < 64) & (ii[None, :] == ii[:, None] - 32), 1.0, 0.0)).astype(F32)

    grads = {}
    dmod = [[[None] * 3 for _ in range(3)] for _ in range(depth)]
    dnorm_g = [[None] * 3 for _ in range(depth)]

    def acc(name, idx, val):
        grads.setdefault(name, {})[idx] = val

    def ffn_sub(xin, l, k, j):
        g, (shift, scale, gate) = Wf["norm_g"][l, k][None], [mod[l, k, t][None] for t in range(3)]
        w1, w3, w2 = W["ffn_w1"][l, j], W["ffn_w3"][l, j], W["ffn_w2"][l, j]
        tag = f"l{l}f{j}"
        h, = rowmap(lambda *a: (_adaln(*a),), [xin], [g, shift, scale], (BF16,), name=f"adaln_{tag}")
        a = ffn_mid_fwd(h, w1, w3, name=f"ffn_mid_{tag}")
        xn, y = mm((a, w2), name=f"ffn_out_{tag}", epi=lambda acc_, xr, gt: (xr + 0.5 * gt * acc_, acc_), epi_rows=(xin,),
                   epi_pars=(gate,), epi_out_dtypes=(F32, F32))

        def bwd(dxn):
            dy, dgate = rowmap(lambda d, y_, gt: ((0.5 * gt) * d, jnp.sum(0.5 * y_ * d, axis=0, keepdims=True)), [dxn, y], [gate],
                               (BF16, F32), name=f"dres_{tag}", n_reduce=1)
            du, dv, a_ = ffn_mid_bwd(h, dy, w1, w3, w2, name=f"ffn_midb_{tag}")
            dh = mm([(du, w1), (dv, w3)], tb=True, name=f"ffn_dh_{tag}")
            acc("ffn_w1", (l, j), mm((h, du), ta=True, name=f"ffn_dw1_{tag}"))
            acc("ffn_w3", (l, j), mm((h, dv), ta=True, name=f"ffn_dw3_{tag}"))
            acc("ffn_w2", (l, j), mm((a_, dy), ta=True, name=f"ffn_dw2_{tag}"))
            dx, dg, dsh, dsc = _vjp_rows(_adaln, 1, 3, (F32,), [xin], [dh], [g, shift, scale], name=f"adalnb_{tag}", extra=dxn)
            dnorm_g[l][k] = dg[0]
            dmod[l][k] = [dsh[0], dsc[0], dgate[0]]
            return dx

        return xn, bwd

    def mixer_tail(xin, l, tag, dh, dxn, g, shift, scale, dgate):
        dx, dg, dsh, dsc = _vjp_rows(_adaln, 1, 3, (F32,), [xin], [dh], [g, shift, scale], name=f"adalnb_{tag}", extra=dxn)
        dnorm_g[l][1] = dg[0]
        dmod[l][1] = [dsh[0], dsc[0], dgate[0]]
        return dx

    def even_sub(xin, l):
        e = l // 2
        tag = f"l{l}m"
        g, (shift, scale, gate) = Wf["norm_g"][l, 1][None], [mod[l, 1, t][None] for t in range(3)]
        wext, wq, wkv, wout = ev_ext(W["ev_w_in"][e]), wq_ext(W["mla_w_uq"][e]), wkv_ext(W["mla_w_ukv"][e]), W["ev_w_out"][e]
        conv_w, zb = Wf["gdn_conv_w"][e], jnp.zeros((1, 1536), F32)
        alog_e, dtb_e = _rep(gdn_A_log[e], 128)[None], _rep(gdn_dt_bias[e], 128)[None]
        gg, qg, kvg = gdn_norm_g[e][None], mla_q_norm_g[e][None], mla_kv_norm_g[e][None]
        h, = rowmap(lambda *a: (_adaln(*a),), [xin], [g, shift, scale], (BF16,), name=f"adaln_{tag}")
        proj = mm((h, wext), name=f"ev_in_{tag}")
        qkvc = conv_fwd(proj, 0, 3, conv_w, zb, name=f"gdn_conv_{tag}")
        o_g, hist = gdn_fwd(qkvc, proj, 4, 5, alog_e, dtb_e, name=f"gdn_{tag}")
        o_a, = rowmap(lambda o, z, g_: (_gdn_gate(o, z, g_),), [o_g, (proj, 512, 3)], [gg], (BF16,), name=f"gdn_gate_{tag}")
        cqn, = rowmap(lambda t, g_: (_rms(t, g_),), [(proj, 384, 8)], [qg], (BF16,), name=f"q_norm_{tag}")
        ckvn, = rowmap(lambda t, g_: (_rms(t, g_),), [(proj, 256, 14)], [kvg], (BF16,), name=f"kv_norm_{tag}")
        q0 = mm((cqn, wq), name=f"q_up_{tag}")
        kv = mm((ckvn, wkv), name=f"kv_up_{tag}", out_dtype=BF16)
        q, = rowmap(lambda t, cs, sn, r: (_rope_q(t, cs, sn, r),), [q0, cos_t, sin_t], [rot], (BF16,), name=f"rope_q_{tag}")
        kp, = rowmap(lambda t, cs, sn, r: (_rope(t, cs, sn, r),), [(proj, 128, 30), cos_t, sin_t], [rot], (BF16,), name=f"rope_k_{tag}")
        o_b, lse = att_fwd(q, kv, kp, name=f"att_{tag}")
        xn, y = mm([(o_a, wout[:512]), (o_b, wout[512:])], name=f"ev_out_{tag}", epi=lambda acc_, xr, gt: (xr + gt * acc_, acc_),
                   epi_rows=(xin,), epi_pars=(gate,), epi_out_dtypes=(F32, F32))

        def bwd(dxn):
            dy, dgate = rowmap(lambda d, y_, gt: (gt * d, jnp.sum(y_ * d, axis=0, keepdims=True)), [dxn, y], [gate],
                               (BF16, F32), name=f"dres_{tag}", n_reduce=1)
            do_a = mm((dy, wout[:512]), tb=True, name=f"ev_doa_{tag}")
            do_b = mm((dy, wout[512:]), tb=True, name=f"ev_dob_{tag}")
            acc("ev_w_out", e, jnp.concatenate([mm((o_a, dy), ta=True, name=f"ev_dwoa_{tag}"), mm((o_b, dy), ta=True, name=f"ev_dwob_{tag}")], axis=0))
            dsum, = rowmap(lambda d, o_: (jnp.concatenate([jnp.broadcast_to(jnp.sum(d[:, 128 * hh:128 * (hh + 1)] * o_[:, 128 * hh:128 * (hh + 1)],
                                                                                        axis=-1, keepdims=True), (d.shape[0], 128))
                                                  for hh in range(4)], axis=1),), [do_b, o_b], [], (F32,), name=f"att_dsum_{tag}")
            dq4, dk2, dv = att_bwd(q, kv, kp, lse, dsum, do_b, name=f"att_bwd_{tag}")

            def rope_qb(d0, d1, d2, d3, cs, sn, r):
                parts = []
                for d in (d0, d1, d2, d3):
                    parts += [d[:, :128], _rope_t(d[:, 128:], cs, sn, r)]
                return (jnp.concatenate(parts, axis=1) * ATT_SCALE,)

            dq0, = rowmap(rope_qb, [dq4[0], dq4[1], dq4[2], dq4[3], cos_t, sin_t], [rot], (BF16,), name=f"rope_qb_{tag}")

            def rope_kb(d, cs, sn, r):
                dkp = d[:, 128:256] + d[:, 384:512] + d[:, 640:768] + d[:, 896:1024]
                return jnp.concatenate([d[:, 256 * hh:256 * hh + 128] for hh in range(4)], axis=1), _rope_t(dkp, cs, sn, r)

            dkn, dkr = rowmap(rope_kb, [dk2, cos_t, sin_t], [rot], (BF16, BF16), name=f"rope_kb_{tag}")
            dcqn = mm((dq0, wq), tb=True, name=f"q_upb_{tag}")
            acc("mla_w_uq", e, wq_ext_t(mm((cqn, dq0), ta=True, name=f"q_dw_{tag}")))
            dckvn = mm([(dkn, wkv[:, :512]), (dv, wkv[:, 512:])], tb=True, name=f"kv_upb_{tag}")
            acc("mla_w_ukv", e, wkv_ext_t(jnp.concatenate([mm((ckvn, dkn), ta=True, name=f"kv_dwk_{tag}"), mm((ckvn, dv), ta=True, name=f"kv_dwv_{tag}")], axis=1)))
            dcq, dqg = _vjp_rows(_rms, 1, 1, (BF16,), [(proj, 384, 8)], [dcqn], [qg], name=f"q_normb_{tag}")
            dckv, dkvg = _vjp_rows(_rms, 1, 1, (BF16,), [(proj, 256, 14)], [dckvn], [kvg], name=f"kv_normb_{tag}")
            acc("mla_q_norm_g", e, dqg[0])
            acc("mla_kv_norm_g", e, dkvg[0])
            do_g, dz, dgg = _vjp_rows(_gdn_gate, 2, 1, (F32, BF16), [o_g, (proj, 512, 3)], [do_a], [gg], name=f"gdn_gateb_{tag}")
            acc("gdn_norm_g", e, dgg[0])
            dqkvc, dbe, dae, dal, ddt = gdn_bwd(qkvc, proj, 4, 5, alog_e, dtb_e, hist, do_g, name=f"gdnb_{tag}")
            acc("gdn_A_log", e, dal.reshape(4, 128).sum(-1))
            acc("gdn_dt_bias", e, ddt.reshape(4, 128).sum(-1))
            dpre, dcw, _ = conv_bwd_pre(proj, 0, 3, conv_w, zb, dqkvc, name=f"gdn_convb_{tag}")
            acc("gdn_conv_w", e, dcw)
            dqkv = conv_bwd_x(dpre, conv_w, name=f"gdn_convx_{tag}", out_dtype=BF16)
            zc = lambda n: jnp.zeros((S_, n), BF16)
            dproj = jnp.concatenate([dqkv, dz, dbe.astype(BF16), dae.astype(BF16), dcq, zc(128), dckv, dkr, zc(128)], axis=1)
            dh = mm((dproj, wext), tb=True, name=f"ev_inb_{tag}")
            acc("ev_w_in", e, ev_ext_t(mm((h, dproj), ta=True, name=f"ev_dwin_{tag}")))
            return mixer_tail(xin, l, tag, dh, dxn, g, shift, scale, dgate)

        return xn, bwd

    def odd_sub(xin, l):
        o = l // 2
        tag = f"l{l}m"
        g, (shift, scale, gate) = Wf["norm_g"][l, 1][None], [mod[l, 1, t][None] for t in range(3)]
        wext, wout = od_ext(W["ssd_w_in"][o]), W["ssd_w_out"][o]
        conv_w, conv_b, ng = Wf["ssd_conv_w"][o], Wf["ssd_conv_b"][o][None], Wf["ssd_norm_g"][o][None]
        ex = lambda v: _rep(v, 64).reshape(4, 1, 512)
        na_e, dtb_e, dsk_e = ex(-jnp.exp(ssd_A_log[o])), ex(ssd_dt_bias[o]), ex(ssd_D[o])
        h, = rowmap(lambda *a: (_adaln(*a),), [xin], [g, shift, scale], (BF16,), name=f"adaln_{tag}")
        proj = mm((h, wext), name=f"ssd_in_{tag}")
        zv = [(proj, 512, 6 + t) for t in range(4)]
        xbc = conv_fwd(proj, 0, 6, conv_w, conv_b, name=f"ssd_conv_{tag}")
        ys, hist = ssd_fwd(xbc, proj, na_e, dtb_e, dsk_e, name=f"ssd_{tag}")
        yn, = rowmap(lambda *a: (_ssd_gate(*a),), [ys] + zv, [ng], (BF16,), name=f"ssd_gate_{tag}", tile=128)
        xn, y = mm((yn, wout), name=f"ssd_out_{tag}", epi=lambda acc_, xr, gt: (xr + gt * acc_, acc_), epi_rows=(xin,),
                   epi_pars=(gate,), epi_out_dtypes=(F32, F32))

        def bwd(dxn):
            dy, dgate = rowmap(lambda d, y_, gt: (gt * d, jnp.sum(y_ * d, axis=0, keepdims=True)), [dxn, y], [gate],
                               (BF16, F32), name=f"dres_{tag}", n_reduce=1)
            dyn = mm((dy, wout), tb=True, name=f"ssd_dyn_{tag}", out_dtype=BF16)
            acc("ssd_w_out", o, mm((yn, dy), ta=True, name=f"ssd_dwout_{tag}"))
            dys, dz0, dz1, dz2, dz3, dng = _vjp_rows(_ssd_gate, 5, 1, (F32, BF16, BF16, BF16, BF16), [ys] + zv, [dyn], [ng],
                                                     name=f"ssd_gateb_{tag}", tile=128)
            acc("ssd_norm_g", o, dng[0])
            dxs, ddtx, db_, dc_, dna, ddtb, ddsk = ssd_bwd(xbc, proj, na_e, dtb_e, dsk_e, hist, dys, name=f"ssdb_{tag}")
            acc("ssd_A_log", o, dna.reshape(32, 64).sum(-1) * (-jnp.exp(ssd_A_log[o])))
            acc("ssd_dt_bias", o, ddtb.reshape(32, 64).sum(-1))
            acc("ssd_D", o, ddsk.reshape(32, 64).sum(-1))
            dxbc = jnp.concatenate([dxs, db_, dc_], axis=1)
            dpre, dcw, dcb = conv_bwd_pre(proj, 0, 6, conv_w, conv_b, dxbc, name=f"ssd_convb_{tag}")
            acc("ssd_conv_w", o, dcw)
            acc("ssd_conv_b", o, dcb[0])
            dxp = conv_bwd_x(dpre, conv_w, name=f"ssd_convx_{tag}", out_dtype=BF16)
            dproj = jnp.concatenate([dxp, dz0, dz1, dz2, dz3, ddtx.astype(BF16)], axis=1)
            dh = mm((dproj, wext), tb=True, name=f"ssd_inb_{tag}")
            acc("ssd_w_in", o, od_ext_t(mm((h, dproj), ta=True, name=f"ssd_dwin_{tag}")))
            return mixer_tail(xin, l, tag, dh, dxn, g, shift, scale, dgate)

        return xn, bwd

    tape = []
    xc = xa
    for l in range(depth):
        xc, b0 = ffn_sub(xc, l, 0, 0)
        xc, b1 = (even_sub if l % 2 == 0 else odd_sub)(xc, l)
        xc, b2 = ffn_sub(xc, l, 2, 1)
        tape += [b0, b1, b2]

    def head(xr, tg, g_):
        def f(xv, gv):
            err = _rms(xv, gv) - tg
            return 0.5 * jnp.sum(jnp.mean(err * err, axis=-1, keepdims=True), axis=0, keepdims=True)
        lo, vjp = jax.vjp(f, xr, g_)
        dxv, dgv = vjp(jnp.ones_like(lo))
        return dxv, jnp.broadcast_to(lo, (1, 128)), dgv

    dx, loss_p, dfg = rowmap(head, [xc, tgt], [final_g[None]], (F32,), name="loss_head", n_reduce=2)
    loss = lax.psum(loss_p[0, 0], ("x", "y", "c"))

    for b in reversed(tape):
        dx = b(dx)
    grad_x = dx[None]

    full = {n: jnp.stack([grads[n][k] for k in sorted(grads[n])]) for n in ("ev_w_in", "mla_w_uq", "mla_w_ukv", "ev_w_out", "ssd_w_in", "ssd_w_out")}
    for n in ("ffn_w1", "ffn_w3", "ffn_w2"):
        full[n] = jnp.stack([jnp.stack([grads[n][(l, j)] for j in range(2)]) for l in range(depth)])
    chunks = []
    for s in range(4):
        chunks.append(_pack([lax.slice_in_dim(full[n], s * sh[ax], (s + 1) * sh[ax], axis=ax) for n, sh, ax in BIG], BF16))
    recv = exchange_chips(jnp.stack(chunks), name="exchange_grads")
    part = sum_slots(recv, name="sum_chips")
    sib = swap_sibling(part, name="swap_sibling")
    g_mine, g_sib = _unpack(part), _unpack(sib)

    dmod_flat = jnp.stack([jnp.stack([jnp.stack(dmod[l][k]) for k in range(3)]) for l in range(depth)]).reshape(depth, 9 * D)
    small_names = ["norm_g", "gdn_conv_w", "gdn_A_log", "gdn_dt_bias", "gdn_norm_g", "mla_q_norm_g", "mla_kv_norm_g",
                   "ssd_conv_w", "ssd_conv_b", "ssd_A_log", "ssd_dt_bias", "ssd_D", "ssd_norm_g", "final_g"]
    small_full = {n: jnp.stack([grads[n][k] for k in sorted(grads[n])]) for n in small_names if n in grads}
    small_full["norm_g"] = jnp.stack([jnp.stack(dnorm_g[l]) for l in range(depth)])
    small_full["final_g"] = dfg[0]
    small_list = [dmod_flat] + [small_full[n] for n in small_names]
    small_shapes = [a.shape for a in small_list]
    sp = _flat_pack(small_list, 128, 8)
    sgath = allgather8(sp, name="gather_small_grads").reshape(8, sp.shape[0], 128)
    ssum = sum_slots(sgath, name="sum_small")
    tot = dict(zip(["ada_b"] + small_names, _flat_unpack(ssum, small_shapes)))
    dmod_all = sgath.reshape(8, -1)[:, :depth * 9 * D].reshape(8, depth, 9 * D)
    dmod_loc = lax.dynamic_slice(dmod_all, (0, 0, chip * ncol), (8, depth, ncol))
    g_ada_w = jnp.stack([mm((c_act, jnp.pad(dmod_loc[:, l], ((0, 8), (0, 0)))), ta=True, name=f"ada_dw_{l}", tk=16, tn=256)
                         for l in range(depth)])

    def own(n, a):
        for m_, sh, ax in SMALL_SHARDED:
            if m_ == n:
                return lax.dynamic_slice_in_dim(a, chip * sh[ax], sh[ax], axis=ax)
        return a

    res = {}
    for k, (n, _, _) in enumerate(BIG):
        res[n] = adamw(P[n], [g_mine[k], g_sib[k]], M1[n], M2[n], name=f"adamw_{n}")
    res["ada_w"] = adamw(ada_w, [g_ada_w], m_ada_w, v_ada_w, name="adamw_ada_w")
    sm = ["ada_b"] + small_names
    shapes = [P[n].shape for n in sm]
    pk = lambda d: _flat_pack([d[n] for n in sm], 128, 8)
    outs = adamw(pk(P), [pk({n: own(n, tot[n]).reshape(P[n].shape) for n in sm})], pk(M1), pk(M2), name="adamw_small")
    un = [_flat_unpack(o, shapes) for o in outs]
    for i, n in enumerate(sm):
        res[n] = tuple(un[t][i] for t in range(4))
    return (loss, grad_x, *[res[n][0] for n in names], *[res[n][1] for n in names], *[res[n][2] for n in names], *[res[n][3] for n in names])
```

```python
import functools
import math

import jax
import jax.numpy as jnp
from jax import lax
from jax.experimental import pallas as pl
from jax.experimental.pallas import tpu as pltpu

F32 = jnp.float32
BF16 = jnp.bfloat16
HI = lax.Precision.HIGHEST
HI3 = lax.Precision.HIGH
VMEM_LIMIT = 56 * 1024 * 1024
NORM_EPS = 1e-6
MM_VMEM_BUDGET = 40 * 1024 * 1024


def _cp(sem=None):
    if sem is None:
        return pltpu.CompilerParams(vmem_limit_bytes=VMEM_LIMIT)
    return pltpu.CompilerParams(dimension_semantics=sem, vmem_limit_bytes=VMEM_LIMIT)


def _pick(dim, prefs):
    for p in prefs:
        if dim % p == 0:
            return p
    return dim


def mm(pairs, *, ta=False, tb=False, out_dtype=F32, name, epi=None, epi_rows=(), epi_pars=(), epi_out_dtypes=None,
       tm=None, tn=None, tk=None):
    if not isinstance(pairs, (list, tuple)) or not isinstance(pairs[0], (list, tuple)):
        pairs = [pairs]
    npair = len(pairs)
    a0, b0 = pairs[0]
    M = a0.shape[1] if ta else a0.shape[0]
    K = a0.shape[0] if ta else a0.shape[1]
    N = b0.shape[0] if tb else b0.shape[1]
    for a, b in pairs:
        assert (a.shape == ((K, M) if ta else (M, K))), (a.shape, M, K)
        assert (b.shape == ((N, K) if tb else (K, N))), (b.shape, K, N)
    tm = tm or _pick(M, (1024, 1408, 512, 384, 256, 128))
    tk = tk or (K if K <= 1024 else _pick(K, (1024, 1408, 512, 256, 128)))
    if tn is None:
        n_epi_out = 1 if epi is None else len(epi_out_dtypes)
        for tn in (1024, 1408, 512, 384, 256, 128, N):
            if N % tn:
                continue
            need = sum(2 * tk * (tm * a.dtype.itemsize + tn * b.dtype.itemsize) for a, b in pairs)
            need += tm * tn * 4 * (1 + 2 * n_epi_out + 2 * len(epi_rows))
            if need <= MM_VMEM_BUDGET:
                break
    nk = K // tk
    assert M % tm == 0 and N % tn == 0 and K % tk == 0, (M, N, K, tm, tn, tk)
    n_rows, n_pars = len(epi_rows), len(epi_pars)
    if epi is None:
        out_dtypes = (out_dtype,)
    else:
        out_dtypes = tuple(epi_out_dtypes)
    n_out = len(out_dtypes)
    dn = (((0 if ta else 1,), (1 if tb else 0,)), ((), ()))

    def body(*refs):
        ab = refs[:2 * npair]
        rows = refs[2 * npair:2 * npair + n_rows]
        pars = refs[2 * npair + n_rows:2 * npair + n_rows + n_pars]
        outs = refs[2 * npair + n_rows + n_pars:2 * npair + n_rows + n_pars + n_out]
        acc_ref = refs[-1]
        k = pl.program_id(2)

        @pl.when(k == 0)
        def _():
            acc_ref[...] = jnp.zeros_like(acc_ref)

        acc = acc_ref[...]
        for p in range(npair):
            a = ab[2 * p][...].astype(BF16)
            b = ab[2 * p + 1][...].astype(BF16)
            acc = acc + lax.dot_general(a, b, dn, preferred_element_type=F32)
        acc_ref[...] = acc

        @pl.when(k == nk - 1)
        def _():
            r = acc_ref[...]
            if epi is None:
                outs[0][...] = r.astype(outs[0].dtype)
            else:
                res = epi(r, *[x[...] for x in rows], *[x[...] for x in pars])
                for o, v in zip(outs, res):
                    o[...] = v.astype(o.dtype)

    a_spec = pl.BlockSpec((tk, tm), lambda i, j, k: (k, i)) if ta else pl.BlockSpec((tm, tk), lambda i, j, k: (i, k))
    b_spec = pl.BlockSpec((tn, tk), lambda i, j, k: (j, k)) if tb else pl.BlockSpec((tk, tn), lambda i, j, k: (k, j))
    in_specs = []
    args = []
    for a, b in pairs:
        in_specs += [a_spec, b_spec]
        args += [a, b]
    for r in epi_rows:
        in_specs.append(pl.BlockSpec((tm, tn), lambda i, j, k: (i, j)))
        args.append(r)
    for p_ in epi_pars:
        in_specs.append(pl.BlockSpec((1, tn), lambda i, j, k: (0, j)))
        args.append(p_)
    out_specs = [pl.BlockSpec((tm, tn), lambda i, j, k: (i, j)) for _ in range(n_out)]
    out_shape = [jax.ShapeDtypeStruct((M, N), d) for d in out_dtypes]
    res = pl.pallas_call(
        body, grid=(M // tm, N // tn, nk), in_specs=in_specs, out_specs=out_specs, out_shape=out_shape,
        scratch_shapes=[pltpu.VMEM((tm, tn), F32)], compiler_params=_cp(("parallel", "parallel", "arbitrary")), name=name,
    )(*args)
    return res[0] if epi is None else tuple(res)


def rowmap(fn, rows, pars, out_dtypes, *, name, tile=256, n_reduce=0):
    views = []
    for r in rows:
        if isinstance(r, tuple):
            views.append(r)
        else:
            views.append((r, r.shape[1], 0))
    S = views[0][0].shape[0]
    assert S % tile == 0
    nt = S // tile
    row_structs = [jax.ShapeDtypeStruct((tile, w), a.dtype) for a, w, _ in views]
    par_structs = [jax.ShapeDtypeStruct(p.shape, p.dtype) for p in pars]
    out_structs = jax.eval_shape(fn, *row_structs, *par_structs)
    n_out = len(out_structs)
    n_row_out = n_out - n_reduce
    nr, npar = len(views), len(pars)

    def body(*refs):
        ins = [x[...] for x in refs[:nr + npar]]
        outs = refs[nr + npar:]
        res = fn(*ins)
        for o, v in zip(outs[:n_row_out], res[:n_row_out]):
            o[...] = v.astype(o.dtype)
        if n_reduce:
            i = pl.program_id(0)

            @pl.when(i == 0)
            def _():
                for o, v in zip(outs[n_row_out:], res[n_row_out:]):
                    o[...] = v.astype(o.dtype)

            @pl.when(i > 0)
            def _():
                for o, v in zip(outs[n_row_out:], res[n_row_out:]):
                    o[...] += v.astype(o.dtype)

    in_specs = [pl.BlockSpec((tile, w), functools.partial(lambda i, c: (i, c), c=c)) for _, w, c in views]
    in_specs += [pl.BlockSpec(p.shape, lambda i: (0, 0)) for p in pars]
    out_specs = [pl.BlockSpec((tile, s.shape[1]), lambda i: (i, 0)) for s in out_structs[:n_row_out]]
    out_specs += [pl.BlockSpec(s.shape, lambda i: (0, 0)) for s in out_structs[n_row_out:]]
    out_shape = [jax.ShapeDtypeStruct((S, s.shape[1]), d) for s, d in zip(out_structs[:n_row_out], out_dtypes[:n_row_out])]
    out_shape += [jax.ShapeDtypeStruct(s.shape, F32) for s in out_structs[n_row_out:]]
    res = pl.pallas_call(
        body, grid=(nt,), in_specs=in_specs, out_specs=out_specs, out_shape=out_shape,
        compiler_params=_cp(("arbitrary",) if n_reduce else ("parallel",)), name=name,
    )(*[v[0] for v in views], *pars)
    return tuple(res)


CH = 64


def _softplus(x):
    return jnp.where(x > 20.0, x, jnp.log(1.0 + jnp.exp(jnp.minimum(x, 20.0))))


def _dot(a, b, dn=(((1,), (0,)), ((), ())), hi=False):
    if hi:
        return lax.dot_general(a.astype(F32), b.astype(F32), dn, precision=HI if hi is True else hi, preferred_element_type=F32)
    return lax.dot_general(a.astype(BF16), b.astype(BF16), dn, preferred_element_type=F32)


_NT = (((1,), (1,)), ((), ()))
_TN = (((0,), (0,)), ((), ()))


def _chunk_consts():
    r = lax.broadcasted_iota(jnp.int32, (CH, 2 * CH), 0)
    c0 = lax.broadcasted_iota(jnp.int32, (CH, 2 * CH), 1)
    c = jnp.where(c0 >= CH, c0 - CH, c0)
    r1 = lax.broadcasted_iota(jnp.int32, (CH, CH), 0)
    c1 = lax.broadcasted_iota(jnp.int32, (CH, CH), 1)
    return dict(
        lower2=r >= c, strict2=r > c, U2=(r <= c).astype(F32), eye2=(r == c).astype(F32),
        L=(r1 >= c1).astype(F32), ones=jnp.ones((CH, CH), F32), Z=jnp.zeros((CH, 2 * CH), F32))


@jax.custom_vjp
def _tri_inv2(a2s, eye2, Z):
    def prod(x2, y):
        return _dot(x2, jnp.concatenate([y, Z], axis=0), hi=HI3)

    bs = [-a2 for a2 in a2s]
    ts = [eye2 + b for b in bs]
    for _ in range(5):
        bs = [prod(b, b) for b in bs]
        ts = [t + prod(t, b) for t, b in zip(ts, bs)]
    return tuple(ts)


def _tri_inv2_fwd(a2s, eye2, Z):
    ts = _tri_inv2(a2s, eye2, Z)
    return ts, (ts, eye2, Z)


def _tri_inv2_bwd(res, dts):
    ts, eye2, Z = res
    xs = [_dot(t2, dt2, _TN, hi=HI3)[:CH] for t2, dt2 in zip(ts, dts)]
    das = tuple(-_dot(x2, jnp.concatenate([t2, Z], axis=0), _NT, hi=HI3) for x2, t2 in zip(xs, ts))
    return das, jnp.zeros_like(eye2), jnp.zeros_like(Z)


_tri_inv2.defvjp(_tri_inv2_fwd, _tri_inv2_bwd)


def _gdn_heads(qs, ks, vs, bxs, axs, Ss, alogs, dtbs, cst):
    lower2, strict2, U2, eye2, L, ones, Z = (cst[n] for n in ("lower2", "strict2", "U2", "eye2", "L", "ones", "Z"))
    H = range(len(qs))

    def prod(x2, y):
        return _dot(x2, jnp.concatenate([y, Z], axis=0), hi=HI3)

    qn = [qs[h] * lax.rsqrt(jnp.sum(qs[h] * qs[h], axis=-1, keepdims=True) + NORM_EPS) * (128.0 ** -0.5) for h in H]
    kn = [ks[h] * lax.rsqrt(jnp.sum(ks[h] * ks[h], axis=-1, keepdims=True) + NORM_EPS) for h in H]
    beta = [jax.nn.sigmoid(bxs[h]) for h in H]
    g = [-jnp.exp(alogs[h]) * _softplus(axs[h] + dtbs[h]) for h in H]
    gc = [_dot(L, g[h], hi=HI3) for h in H]
    n2 = [_dot(ones, g[h] * U2, hi=HI3) for h in H]
    decay2 = [jnp.where(lower2, jnp.exp(jnp.where(lower2, gc[h] - n2[h], 0.0)), 0.0) for h in H]
    kb = [kn[h] * beta[h] for h in H]
    kn2 = [jnp.concatenate([kn[h], kn[h]], axis=0) for h in H]
    a2 = tuple(jnp.where(strict2, _dot(kb[h], kn2[h], _NT) * decay2[h], 0.0) for h in H)
    t2 = _tri_inv2(a2, eye2, Z)
    glast = [jnp.sum(g[h], axis=0, keepdims=True) for h in H]
    u = [prod(t2[h], vs[h] * beta[h]) for h in H]
    w = [prod(t2[h], kb[h] * jnp.exp(gc[h])) for h in H]
    attn2 = [jnp.where(lower2, _dot(qn[h], kn2[h], _NT) * decay2[h], 0.0) for h in H]
    k_end = [kn[h] * jnp.exp(glast[h] - gc[h]) for h in H]
    q_start = [qn[h] * jnp.exp(gc[h]) for h in H]
    v_new = [u[h] - _dot(w[h], Ss[h]) for h in H]
    o = [_dot(q_start[h], Ss[h]) + _dot(attn2[h], jnp.concatenate([v_new[h], Z], axis=0)) for h in H]
    s_new = [Ss[h] * jnp.exp(glast[h]) + _dot(k_end[h], v_new[h], _TN) for h in H]
    return tuple(o), tuple(s_new)


def gdn_fwd(qkv, proj, bcol, acol, alog_e, dtb_e, *, name):
    S_ = qkv.shape[0]
    nc = S_ // CH

    def body(q_ref, k_ref, v_ref, b_ref, a_ref, al_ref, dt_ref, o_ref, hist_ref, s_ref):
        i = pl.program_id(0)

        @pl.when(i == 0)
        def _():
            s_ref[...] = jnp.zeros_like(s_ref)

        cst = _chunk_consts()
        hist_ref[0] = s_ref[...]
        heads = [slice(128 * h, 128 * (h + 1)) for h in range(4)]
        rd = lambda ref: tuple(ref[:, ls] for ls in heads)
        os_, s_news = _gdn_heads(rd(q_ref), rd(k_ref), rd(v_ref), rd(b_ref), rd(a_ref), tuple(s_ref[ls, :] for ls in heads),
                                 rd(al_ref), rd(dt_ref), cst)
        for ls, o, s_new in zip(heads, os_, s_news):
            o_ref[:, ls] = o
            s_ref[ls, :] = s_new

    blk = lambda cb: pl.BlockSpec((CH, 512), functools.partial(lambda i, cb: (i, cb), cb=cb))
    par = pl.BlockSpec((1, 512), lambda i: (0, 0))
    return pl.pallas_call(
        body, grid=(nc,), in_specs=[blk(0), blk(1), blk(2), blk(bcol), blk(acol), par, par],
        out_specs=[pl.BlockSpec((CH, 512), lambda i: (i, 0)), pl.BlockSpec((1, 512, 128), lambda i: (i, 0, 0))],
        out_shape=[jax.ShapeDtypeStruct((S_, 512), F32), jax.ShapeDtypeStruct((nc, 512, 128), F32)],
        scratch_shapes=[pltpu.VMEM((512, 128), F32)], compiler_params=_cp(("arbitrary",)), name=name,
    )(qkv, qkv, qkv, proj, proj, alog_e, dtb_e)


def gdn_bwd(qkv, proj, bcol, acol, alog_e, dtb_e, hist, do, *, name):
    S_ = qkv.shape[0]
    nc = S_ // CH

    def body(q_ref, k_ref, v_ref, b_ref, a_ref, al_ref, dt_ref, hist_ref, do_ref, dqkv_ref, db_ref, da_ref, dal_ref, ddt_ref, ds_ref):
        i = pl.program_id(0)

        @pl.when(i == 0)
        def _():
            ds_ref[...] = jnp.zeros_like(ds_ref)
            dal_ref[...] = jnp.zeros_like(dal_ref)
            ddt_ref[...] = jnp.zeros_like(ddt_ref)

        cst = _chunk_consts()
        heads = [slice(128 * h, 128 * (h + 1)) for h in range(4)]
        rd = lambda ref: tuple(ref[:, ls] for ls in heads)
        fn = functools.partial(_gdn_heads, cst=cst)
        _, vjp = jax.vjp(fn, rd(q_ref), rd(k_ref), rd(v_ref), rd(b_ref), rd(a_ref), tuple(hist_ref[0, ls, :] for ls in heads),
                         rd(al_ref), rd(dt_ref))
        grads = vjp((rd(do_ref), tuple(ds_ref[ls, :] for ls in heads)))
        for h in range(4):
            ls = heads[h]
            dq, dk, dv, db, da, ds_in, dal, ddt = (t[h] for t in grads)
            dqkv_ref[:, 128 * h:128 * (h + 1)] = dq
            dqkv_ref[:, 512 + 128 * h:512 + 128 * (h + 1)] = dk
            dqkv_ref[:, 1024 + 128 * h:1024 + 128 * (h + 1)] = dv
            db_ref[:, ls] = db
            da_ref[:, ls] = da
            ds_ref[ls, :] = ds_in
            dal_ref[:, ls] += dal
            ddt_ref[:, ls] += ddt

    rblk = lambda cb: pl.BlockSpec((CH, 512), functools.partial(lambda i, cb: (nc - 1 - i, cb), cb=cb))
    par = pl.BlockSpec((1, 512), lambda i: (0, 0))
    return pl.pallas_call(
        body, grid=(nc,),
        in_specs=[rblk(0), rblk(1), rblk(2), rblk(bcol), rblk(acol), par, par,
                  pl.BlockSpec((1, 512, 128), lambda i: (nc - 1 - i, 0, 0)), rblk(0)],
        out_specs=[pl.BlockSpec((CH, 1536), lambda i: (nc - 1 - i, 0)), rblk(0), rblk(0), par, par],
        out_shape=[jax.ShapeDtypeStruct((S_, 1536), F32), jax.ShapeDtypeStruct((S_, 512), F32), jax.ShapeDtypeStruct((S_, 512), F32),
                   jax.ShapeDtypeStruct((1, 512), F32), jax.ShapeDtypeStruct((1, 512), F32)],
        scratch_shapes=[pltpu.VMEM((512, 128), F32)], compiler_params=_cp(("arbitrary",)), name=name,
    )(qkv, qkv, qkv, proj, proj, alog_e, dtb_e, hist, do)


def _ssd_group(xs, dtxs, bm, cm, hss, nas, dtbs, dsks, cst):
    lower2, U2, L, ones = cst["lower2"], cst["U2"], cst["L"], cst["ones"]
    lane = lax.broadcasted_iota(jnp.int32, (1, 2 * CH), 1)
    mask_l = (lane < CH).astype(F32)
    mask_r = 1.0 - mask_l
    ones_w = jnp.ones((CH, 2 * CH), F32)
    cb2 = _dot(cm, jnp.concatenate([bm, bm], axis=0), _NT)
    P_ = range(len(xs))
    dt = [_softplus(dtxs[p] + dtbs[p]) for p in P_]
    da = [dt[p] * nas[p] for p in P_]
    m = [_dot(L, da[p], hi=HI3) for p in P_]
    n2 = [_dot(ones, da[p] * U2, hi=HI3) for p in P_]
    lm2 = [jnp.where(lower2, jnp.exp(jnp.where(lower2, m[p] - n2[p], 0.0)), 0.0) for p in P_]
    xdt = [xs[p] * dt[p] for p in P_]
    x2 = [jnp.concatenate([xdt[p] * mask_l, xdt[p] * mask_r], axis=0) for p in P_]
    y_diag = [_dot(cb2 * lm2[p], x2[p]) for p in P_]
    alast = [jnp.sum(da[p], axis=0, keepdims=True) for p in P_]
    y_off = [_dot(cm, hss[p], _NT) * jnp.exp(m[p]) for p in P_]
    cd = [jnp.exp(_dot(da[p], ones_w, _TN, hi=HI3)) for p in P_]
    hs_new = [hss[p] * cd[p] + _dot(xdt[p] * jnp.exp(alast[p] - m[p]), bm, _TN) for p in P_]
    ys = [y_diag[p] + y_off[p] + dsks[p] * xs[p] for p in P_]
    return tuple(ys), tuple(hs_new)


def _ssd_specs(nc, rev):
    ci = (lambda i: nc - 1 - i) if rev else (lambda i: i)
    xg = pl.BlockSpec((CH, 512), lambda i, g: (ci(i), g))
    dtg = pl.BlockSpec((CH, 512), lambda i, g: (ci(i), 10 + g))
    bg = pl.BlockSpec((CH, 128), lambda i, g: (ci(i), 16 + g))
    cg = pl.BlockSpec((CH, 128), lambda i, g: (ci(i), 20 + g))
    par = pl.BlockSpec((4, 1, 512), lambda i, g: (0, 0, 0))
    hist = pl.BlockSpec((1, 512, 128), lambda i, g: (ci(i), g, 0))
    return xg, dtg, bg, cg, par, hist


def ssd_fwd(xbc, proj, na_e, dtb_e, dsk_e, *, name):
    S_ = xbc.shape[0]
    nc = S_ // CH
    xg, dtg, bg, cg, par, hist = _ssd_specs(nc, False)

    def body(x_ref, dt_ref, b_ref, c_ref, na_ref, dtb_ref, dsk_ref, y_ref, hist_ref, s_ref):
        i, g = pl.program_id(0), pl.program_id(1)

        @pl.when(i == 0)
        def _():
            s_ref[g] = jnp.zeros((512, 128), F32)

        cst = _chunk_consts()
        hist_ref[0] = s_ref[g]
        sl = [slice(128 * p, 128 * (p + 1)) for p in range(4)]
        na, dtb, dsk = na_ref[g], dtb_ref[g], dsk_ref[g]
        ys, hs_new = _ssd_group([x_ref[:, s] for s in sl], [dt_ref[:, s] for s in sl], b_ref[...], c_ref[...],
                                [s_ref[g, s, :] for s in sl], [na[:, s] for s in sl], [dtb[:, s] for s in sl],
                                [dsk[:, s] for s in sl], cst)
        for p, s in enumerate(sl):
            y_ref[:, s] = ys[p]
            s_ref[g, s, :] = hs_new[p]

    return pl.pallas_call(
        body, grid=(nc, 4), in_specs=[xg, dtg, bg, cg, par, par, par],
        out_specs=[pl.BlockSpec((CH, 512), lambda i, g: (i, g)), hist],
        out_shape=[jax.ShapeDtypeStruct((S_, 2048), F32), jax.ShapeDtypeStruct((nc, 2048, 128), F32)],
        scratch_shapes=[pltpu.VMEM((4, 512, 128), F32)], compiler_params=_cp(("arbitrary", "arbitrary")), name=name,
    )(xbc, proj, xbc, xbc, na_e, dtb_e, dsk_e)


def ssd_bwd(xbc, proj, na_e, dtb_e, dsk_e, hist, dy, *, name):
    S_ = xbc.shape[0]
    nc = S_ // CH
    xg, dtg, bg, cg, par, hist_spec = _ssd_specs(nc, True)
    og = pl.BlockSpec((CH, 512), lambda i, g: (nc - 1 - i, g))
    o128 = pl.BlockSpec((CH, 128), lambda i, g: (nc - 1 - i, g))

    def body(x_ref, dt_ref, b_ref, c_ref, na_ref, dtb_ref, dsk_ref, hist_ref, dy_ref,
             dx_ref, ddt_ref, db_ref, dc_ref, dna_ref, ddtb_ref, ddsk_ref, ds_ref):
        i, g = pl.program_id(0), pl.program_id(1)

        @pl.when(i == 0)
        def _():
            ds_ref[g] = jnp.zeros((512, 128), F32)
            dna_ref[g] = jnp.zeros((1, 512), F32)
            ddtb_ref[g] = jnp.zeros((1, 512), F32)
            ddsk_ref[g] = jnp.zeros((1, 512), F32)

        cst = _chunk_consts()
        sl = [slice(128 * p, 128 * (p + 1)) for p in range(4)]
        na, dtb, dsk = na_ref[g], dtb_ref[g], dsk_ref[g]
        fn = functools.partial(_ssd_group, cst=cst)
        _, vjp = jax.vjp(fn, [x_ref[:, s] for s in sl], [dt_ref[:, s] for s in sl], b_ref[...], c_ref[...],
                         [hist_ref[0, s, :] for s in sl], [na[:, s] for s in sl], [dtb[:, s] for s in sl], [dsk[:, s] for s in sl])
        dxs, ddts, db, dc, dhs, dnas, ddtbs, ddsks = vjp((tuple(dy_ref[:, s] for s in sl), tuple(ds_ref[g, s, :] for s in sl)))
        db_ref[...] = db
        dc_ref[...] = dc
        for p, s in enumerate(sl):
            dx_ref[:, s] = dxs[p]
            ddt_ref[:, s] = ddts[p]
            ds_ref[g, s, :] = dhs[p]
            dna_ref[g, :, s] += dnas[p]
            ddtb_ref[g, :, s] += ddtbs[p]
            ddsk_ref[g, :, s] += ddsks[p]

    return pl.pallas_call(
        body, grid=(nc, 4), in_specs=[xg, dtg, bg, cg, par, par, par, hist_spec, og],
        out_specs=[og, og, o128, o128, par, par, par],
        out_shape=[jax.ShapeDtypeStruct((S_, 2048), F32), jax.ShapeDtypeStruct((S_, 2048), F32),
                   jax.ShapeDtypeStruct((S_, 512), F32), jax.ShapeDtypeStruct((S_, 512), F32)] +
                  [jax.ShapeDtypeStruct((4, 1, 512), F32)] * 3,
        scratch_shapes=[pltpu.VMEM((4, 512, 128), F32)], compiler_params=_cp(("arbitrary", "arbitrary")), name=name,
    )(xbc, proj, xbc, xbc, na_e, dtb_e, dsk_e, hist, dy)


ATT_T = 1024
ATT_SCALE = 192.0 ** -0.5
NEG = -1e30


def _chunk_mask(shape):
    return lax.broadcasted_iota(jnp.int32, shape, 1) // CH <= lax.broadcasted_iota(jnp.int32, shape, 0) // CH


def att_fwd(q, kv, kp, *, name):
    S_ = q.shape[0]
    T = min(ATT_T, S_)
    n = S_ // T

    def body(q_ref, kn_ref, kp_ref, v_ref, o_ref, lse_ref, m_ref, l_ref, acc_ref):
        i, j = pl.program_id(1), pl.program_id(2)

        @pl.when(j == 0)
        def _():
            m_ref[...] = jnp.full_like(m_ref, NEG)
            l_ref[...] = jnp.zeros_like(l_ref)
            acc_ref[...] = jnp.zeros_like(acc_ref)

        def step(diag):
            k2 = jnp.concatenate([kn_ref[...], kp_ref[...]], axis=1)
            s = _dot(q_ref[...], k2, _NT)
            if diag:
                s = jnp.where(_chunk_mask(s.shape), s, NEG)
            m_prev = m_ref[...]
            m_cur = jnp.maximum(m_prev, jnp.max(s, axis=-1, keepdims=True))
            p = jnp.exp(s - m_cur[:, :1])
            alpha = jnp.exp(m_prev - m_cur)
            l_ref[...] = alpha * l_ref[...] + jnp.sum(p, axis=-1, keepdims=True)
            acc_ref[...] = acc_ref[...] * alpha + _dot(p, v_ref[...])
            m_ref[...] = m_cur

        @pl.when(j < i)
        def _():
            step(False)

        @pl.when(j == i)
        def _():
            step(True)
            o_ref[...] = acc_ref[...] / l_ref[...]
            lse_ref[...] = m_ref[...] + jnp.log(l_ref[...])

    return pl.pallas_call(
        body, grid=(4, n, n),
        in_specs=[pl.BlockSpec((T, 256), lambda h, i, j: (i, h)), pl.BlockSpec((T, 128), lambda h, i, j: (jnp.minimum(j, i), h)),
                  pl.BlockSpec((T, 128), lambda h, i, j: (jnp.minimum(j, i), 0)),
                  pl.BlockSpec((T, 128), lambda h, i, j: (jnp.minimum(j, i), 4 + h))],
        out_specs=[pl.BlockSpec((T, 128), lambda h, i, j: (i, h))] * 2,
        out_shape=[jax.ShapeDtypeStruct((S_, 512), F32), jax.ShapeDtypeStruct((S_, 512), F32)],
        scratch_shapes=[pltpu.VMEM((T, 128), F32)] * 3,
        compiler_params=_cp(("parallel", "parallel", "arbitrary")), name=name,
    )(q, kv, kp, kv)


def att_bwd(q, kv, kp, lse, dsum, do, *, name):
    S_ = q.shape[0]
    T = min(ATT_T, S_)
    n = S_ // T

    def body(q_ref, kn_ref, kp_ref, v_ref, lse_ref, d_ref, do_ref, dq_hbm, dk_ref, dv_ref, dq_acc, sem):
        h, j, i = pl.program_id(0), pl.program_id(1), pl.program_id(2)

        @pl.when((j == 0) & (i == 0))
        def _():
            dq_acc[...] = jnp.zeros_like(dq_acc)

        @pl.when(i == 0)
        def _():
            dk_ref[...] = jnp.zeros_like(dk_ref)
            dv_ref[...] = jnp.zeros_like(dv_ref)

        def step(diag):
            k2 = jnp.concatenate([kn_ref[...], kp_ref[...]], axis=1)
            qb = q_ref[...]
            dob = do_ref[...].astype(BF16)
            s = _dot(qb, k2, _NT)
            p = jnp.exp(s - lse_ref[:, :1])
            if diag:
                p = jnp.where(_chunk_mask(s.shape), p, 0.0)
            dv_ref[...] += _dot(p, dob, _TN)
            ds = (p * (_dot(dob, v_ref[...], _NT) - d_ref[:, :1])).astype(BF16)
            dk_ref[...] += _dot(ds, qb, _TN)
            rows = pl.ds(pl.multiple_of(i * T, T), T)
            dq_acc[rows, :] += _dot(ds, k2)

        @pl.when(i > j)
        def _():
            step(False)

        @pl.when(i == j)
        def _():
            step(True)

        @pl.when((j == n - 1) & (i == n - 1))
        def _():
            cp = pltpu.make_async_copy(dq_acc, dq_hbm.at[h], sem)
            cp.start()
            cp.wait()

    qmap = lambda h, j, i: (jnp.maximum(i, j), h)
    return pl.pallas_call(
        body, grid=(4, n, n),
        in_specs=[pl.BlockSpec((T, 256), qmap), pl.BlockSpec((T, 128), lambda h, j, i: (j, h)),
                  pl.BlockSpec((T, 128), lambda h, j, i: (j, 0)), pl.BlockSpec((T, 128), lambda h, j, i: (j, 4 + h)),
                  pl.BlockSpec((T, 128), qmap), pl.BlockSpec((T, 128), qmap), pl.BlockSpec((T, 128), qmap)],
        out_specs=[pl.BlockSpec(memory_space=pl.ANY), pl.BlockSpec((T, 256), lambda h, j, i: (j, h)),
                   pl.BlockSpec((T, 128), lambda h, j, i: (j, h))],
        out_shape=[jax.ShapeDtypeStruct((4, S_, 256), F32), jax.ShapeDtypeStruct((S_, 1024), F32), jax.ShapeDtypeStruct((S_, 512), F32)],
        scratch_shapes=[pltpu.VMEM((S_, 256), F32), pltpu.SemaphoreType.DMA],
        compiler_params=_cp(("arbitrary", "arbitrary", "arbitrary")), name=name,
    )(q, kv, kp, kv, lse, dsum, do)


CONV_T = 256


def _shift_down(x, halo, s):
    sh = pltpu.roll(x, s, axis=0)
    hr = pltpu.roll(halo, s, axis=0)
    r8 = lax.broadcasted_iota(jnp.int32, hr.shape, 0)
    top = jnp.where(r8 < s, hr, sh[:8])
    return jnp.concatenate([top, sh[8:]], axis=0)


def _shift_up(x, halo, s):
    n = x.shape[0]
    sh = pltpu.roll(x, n - s, axis=0)
    hr = pltpu.roll(halo, 8 - s, axis=0)
    r8 = lax.broadcasted_iota(jnp.int32, hr.shape, 0)
    bot = jnp.where(r8 >= 8 - s, hr, sh[n - 8:])
    return jnp.concatenate([sh[:n - 8], bot], axis=0)


def _conv_pre(x, halo, w, b):
    y = x * w[3:4] + b
    for j in range(3):
        y = y + _shift_down(x, halo, 3 - j) * w[j:j + 1]
    return y


def conv_fwd(src, cb0, ncb, w, b, *, name):
    S_ = src.shape[0]
    T = min(CONV_T, S_)
    nt = S_ // T

    def body(x_ref, h_ref, w_ref, b_ref, o_ref):
        i = pl.program_id(1)
        halo = jnp.where(i > 0, h_ref[...], 0.0)
        y = _conv_pre(x_ref[...], halo, w_ref[...], b_ref[...])
        o_ref[...] = y * jax.nn.sigmoid(y)

    return pl.pallas_call(
        body, grid=(ncb, nt),
        in_specs=[pl.BlockSpec((T, 512), lambda c, i: (i, cb0 + c)),
                  pl.BlockSpec((8, 512), lambda c, i: (jnp.maximum(i * (T // 8) - 1, 0), cb0 + c)),
                  pl.BlockSpec((4, 512), lambda c, i: (0, c)), pl.BlockSpec((1, 512), lambda c, i: (0, c))],
        out_specs=pl.BlockSpec((T, 512), lambda c, i: (i, c)),
        out_shape=jax.ShapeDtypeStruct((S_, 512 * ncb), F32), compiler_params=_cp(("parallel", "parallel")), name=name,
    )(src, src, w, b)


def conv_bwd_pre(src, cb0, ncb, w, b, dy, *, name):
    S_ = src.shape[0]
    T = min(CONV_T, S_)
    nt = S_ // T

    def body(x_ref, h_ref, w_ref, b_ref, dy_ref, dp_ref, dw_ref, db_ref):
        i = pl.program_id(1)
        halo = jnp.where(i > 0, h_ref[...], 0.0)
        x = x_ref[...]
        y = _conv_pre(x, halo, w_ref[...], b_ref[...])
        sg = jax.nn.sigmoid(y)
        dpre = dy_ref[...] * (sg * (1.0 + y * (1.0 - sg)))
        dp_ref[...] = dpre
        rows = [jnp.sum(dpre * _shift_down(x, halo, 3 - j), axis=0, keepdims=True) for j in range(3)]
        rows.append(jnp.sum(dpre * x, axis=0, keepdims=True))
        dw = jnp.concatenate(rows, axis=0)
        db = jnp.sum(dpre, axis=0, keepdims=True)

        @pl.when(i == 0)
        def _():
            dw_ref[...] = dw
            db_ref[...] = db

        @pl.when(i > 0)
        def _():
            dw_ref[...] += dw
            db_ref[...] += db

    return pl.pallas_call(
        body, grid=(ncb, nt),
        in_specs=[pl.BlockSpec((T, 512), lambda c, i: (i, cb0 + c)),
                  pl.BlockSpec((8, 512), lambda c, i: (jnp.maximum(i * (T // 8) - 1, 0), cb0 + c)),
                  pl.BlockSpec((4, 512), lambda c, i: (0, c)), pl.BlockSpec((1, 512), lambda c, i: (0, c)),
                  pl.BlockSpec((T, 512), lambda c, i: (i, c))],
        out_specs=[pl.BlockSpec((T, 512), lambda c, i: (i, c)), pl.BlockSpec((4, 512), lambda c, i: (0, c)),
                   pl.BlockSpec((1, 512), lambda c, i: (0, c))],
        out_shape=[jax.ShapeDtypeStruct((S_, 512 * ncb), F32), jax.ShapeDtypeStruct((4, 512 * ncb), F32),
                   jax.ShapeDtypeStruct((1, 512 * ncb), F32)],
        compiler_params=_cp(("parallel", "arbitrary")), name=name,
    )(src, src, w, b, dy)


def conv_bwd_x(dpre, w, *, name, out_dtype=F32):
    S_, C = dpre.shape
    T = min(CONV_T, S_)
    nt = S_ // T
    ncb = C // 512

    def body(d_ref, h_ref, w_ref, o_ref):
        i = pl.program_id(1)
        halo = jnp.where(i < nt - 1, h_ref[...], 0.0)
        d = d_ref[...]
        w_ = w_ref[...]
        y = d * w_[3:4]
        for j in range(3):
            y = y + _shift_up(d, halo, 3 - j) * w_[j:j + 1]
        o_ref[...] = y.astype(o_ref.dtype)

    return pl.pallas_call(
        body, grid=(ncb, nt),
        in_specs=[pl.BlockSpec((T, 512), lambda c, i: (i, c)),
                  pl.BlockSpec((8, 512), lambda c, i: (jnp.minimum((i + 1) * (T // 8), S_ // 8 - 1), c)),
                  pl.BlockSpec((4, 512), lambda c, i: (0, c))],
        out_specs=pl.BlockSpec((T, 512), lambda c, i: (i, c)),
        out_shape=jax.ShapeDtypeStruct((S_, C), out_dtype), compiler_params=_cp(("parallel", "parallel")), name=name,
    )(dpre, dpre, w)


def ffn_mid_fwd(h, w1, w3, *, name):
    S_, D = h.shape
    F = w1.shape[1]
    tm, tn = _pick(S_, (2048, 1024, 512, 256)), 256

    def body(h_ref, w1_ref, w3_ref, a_ref):
        hb = h_ref[...]
        u = _dot(hb, w1_ref[...])
        v = _dot(hb, w3_ref[...])
        a_ref[...] = (u * jax.nn.sigmoid(u) * v).astype(a_ref.dtype)

    return pl.pallas_call(
        body, grid=(S_ // tm, F // tn),
        in_specs=[pl.BlockSpec((tm, D), lambda i, j: (i, 0)), pl.BlockSpec((D, tn), lambda i, j: (0, j)),
                  pl.BlockSpec((D, tn), lambda i, j: (0, j))],
        out_specs=pl.BlockSpec((tm, tn), lambda i, j: (i, j)), out_shape=jax.ShapeDtypeStruct((S_, F), BF16),
        compiler_params=_cp(("parallel", "parallel")), name=name,
    )(h, w1, w3)


def ffn_mid_bwd(h, dy, w1, w3, w2, *, name):
    S_, D = h.shape
    F = w1.shape[1]
    tm, tn = _pick(S_, (2048, 1024, 512, 256)), 256

    def body(h_ref, dy_ref, w1_ref, w3_ref, w2_ref, du_ref, dv_ref, a_ref):
        hb = h_ref[...]
        u = _dot(hb, w1_ref[...])
        v = _dot(hb, w3_ref[...])
        da = _dot(dy_ref[...], w2_ref[...], _NT)
        sg = jax.nn.sigmoid(u)
        si = u * sg
        a_ref[...] = (si * v).astype(a_ref.dtype)
        dv_ref[...] = (da * si).astype(dv_ref.dtype)
        du_ref[...] = (da * v * (sg * (1.0 + u * (1.0 - sg)))).astype(du_ref.dtype)

    o = pl.BlockSpec((tm, tn), lambda i, j: (i, j))
    return pl.pallas_call(
        body, grid=(S_ // tm, F // tn),
        in_specs=[pl.BlockSpec((tm, D), lambda i, j: (i, 0)), pl.BlockSpec((tm, D), lambda i, j: (i, 0)),
                  pl.BlockSpec((D, tn), lambda i, j: (0, j)), pl.BlockSpec((D, tn), lambda i, j: (0, j)),
                  pl.BlockSpec((tn, D), lambda i, j: (j, 0))],
        out_specs=[o, o, o], out_shape=[jax.ShapeDtypeStruct((S_, F), BF16)] * 3,
        compiler_params=_cp(("parallel", "parallel")), name=name,
    )(h, dy, w1, w3, w2)


MESH = pl.DeviceIdType.MESH
ANY = pl.BlockSpec(memory_space=pl.ANY)


def allgather8(x_shard, *, name):
    m_per, n = x_shard.shape

    def body(x_ref, out_ref, send_sems, recv_sems, local_sem):
        x, y, c = lax.axis_index("x"), lax.axis_index("y"), lax.axis_index("c")
        me, sibling = (x, y, c), (x, y, 1 - c)
        chips = [(1 - x, y), (x, 1 - y), (1 - x, 1 - y)]

        def rows(px, py, pc):
            return out_ref.at[pl.ds((4 * px + 2 * py + pc) * m_per, m_per), :]

        def copy(k, block, to, src=None):
            return pltpu.make_async_remote_copy(
                src_ref=rows(*block) if src is None else src, dst_ref=rows(*block),
                send_sem=send_sems.at[k], recv_sem=recv_sems.at[k], device_id=to, device_id_type=MESH)

        mine = pltpu.make_async_copy(x_ref, rows(*me), local_sem)
        mine.start()
        first = [copy(0, me, sibling, src=x_ref)]
        first += [copy(1 + j, me, (*chip, c), src=x_ref) for j, chip in enumerate(chips)]
        for cp in first:
            cp.start()
        passed = [copy(4 + j, (*chip, c), sibling) for j, chip in enumerate(chips)]
        for j, chip in enumerate(chips):
            copy(1 + j, (*chip, c), me).wait_recv()
            passed[j].start()
        copy(0, sibling, me).wait_recv()
        for j, chip in enumerate(chips):
            copy(4 + j, (*chip, 1 - c), me).wait_recv()
        for cp in first + passed:
            cp.wait_send()
        mine.wait()

    return pl.pallas_call(
        body, out_shape=jax.ShapeDtypeStruct((8 * m_per, n), x_shard.dtype),
        in_specs=[pl.BlockSpec(memory_space=pltpu.VMEM)], out_specs=pl.BlockSpec(memory_space=pltpu.VMEM),
        scratch_shapes=[pltpu.SemaphoreType.DMA((7,)), pltpu.SemaphoreType.DMA((7,)), pltpu.SemaphoreType.DMA],
        name=name,
    )(x_shard)


def _chip_peers():
    x, y, c = lax.axis_index("x"), lax.axis_index("y"), lax.axis_index("c")
    return x, y, c, [(1 - x, y), (x, 1 - y), (1 - x, 1 - y)]


def allgather_chips(x_shard, *, name):
    r, cdim = x_shard.shape

    def body(x_ref, out_ref, send_sems, recv_sems, local_sem):
        x, y, c, chips = _chip_peers()
        me = 2 * x + y
        mine = pltpu.make_async_copy(x_ref, out_ref.at[me], local_sem)
        mine.start()
        sends = []
        for k, (px, py) in enumerate(chips):
            cp = pltpu.make_async_remote_copy(src_ref=x_ref, dst_ref=out_ref.at[me], send_sem=send_sems.at[k],
                                              recv_sem=recv_sems.at[k], device_id=(px, py, c), device_id_type=MESH)
            cp.start()
            sends.append(cp)
        for k, (px, py) in enumerate(chips):
            pltpu.make_async_remote_copy(src_ref=x_ref, dst_ref=out_ref.at[2 * px + py], send_sem=send_sems.at[k],
                                         recv_sem=recv_sems.at[k], device_id=(px, py, c), device_id_type=MESH).wait_recv()
        for cp in sends:
            cp.wait_send()
        mine.wait()

    return pl.pallas_call(
        body, out_shape=jax.ShapeDtypeStruct((4, r, cdim), x_shard.dtype), in_specs=[ANY], out_specs=ANY,
        scratch_shapes=[pltpu.SemaphoreType.DMA((3,)), pltpu.SemaphoreType.DMA((3,)), pltpu.SemaphoreType.DMA],
        name=name,
    )(x_shard)


def allgather_chips_2level(x_shard, *, name):
    r, cdim = x_shard.shape
    half = r // 2

    def body(x_ref, out_ref, send_sems, recv_sems, local_sem):
        x, y, c, chips = _chip_peers()
        me = 2 * x + y
        mine_rows = pl.ds(c * half, half)
        other_rows = pl.ds((1 - c) * half, half)
        mine = pltpu.make_async_copy(x_ref, out_ref.at[me], local_sem)
        mine.start()

        def copy(k, slot, rows, to, src=None):
            dst = out_ref.at[slot, rows, :]
            return pltpu.make_async_remote_copy(src_ref=dst if src is None else src, dst_ref=dst, send_sem=send_sems.at[k],
                                                recv_sem=recv_sems.at[k], device_id=to, device_id_type=MESH)

        first = [copy(k, me, mine_rows, (px, py, c), src=x_ref.at[mine_rows, :]) for k, (px, py) in enumerate(chips)]
        for cp in first:
            cp.start()
        passed = [copy(3 + k, 2 * px + py, mine_rows, (x, y, 1 - c)) for k, (px, py) in enumerate(chips)]
        for k, (px, py) in enumerate(chips):
            copy(k, 2 * px + py, mine_rows, (px, py, c)).wait_recv()
            passed[k].start()
        for k, (px, py) in enumerate(chips):
            copy(3 + k, 2 * px + py, other_rows, (x, y, 1 - c)).wait_recv()
        for cp in first + passed:
            cp.wait_send()
        mine.wait()

    return pl.pallas_call(
        body, out_shape=jax.ShapeDtypeStruct((4, r, cdim), x_shard.dtype), in_specs=[ANY], out_specs=ANY,
        scratch_shapes=[pltpu.SemaphoreType.DMA((6,)), pltpu.SemaphoreType.DMA((6,)), pltpu.SemaphoreType.DMA],
        name=name,
    )(x_shard)


def exchange_chips(g, *, name):
    _, r, cdim = g.shape

    def body(g_ref, out_ref, send_sems, recv_sems, local_sem):
        x, y, c, chips = _chip_peers()
        me = 2 * x + y
        mine = pltpu.make_async_copy(g_ref.at[me], out_ref.at[me], local_sem)
        mine.start()
        sends = []
        for k, (px, py) in enumerate(chips):
            cp = pltpu.make_async_remote_copy(src_ref=g_ref.at[2 * px + py], dst_ref=out_ref.at[me], send_sem=send_sems.at[k],
                                              recv_sem=recv_sems.at[k], device_id=(px, py, c), device_id_type=MESH)
            cp.start()
            sends.append(cp)
        for k, (px, py) in enumerate(chips):
            pltpu.make_async_remote_copy(src_ref=g_ref.at[me], dst_ref=out_ref.at[2 * px + py], send_sem=send_sems.at[k],
                                         recv_sem=recv_sems.at[k], device_id=(px, py, c), device_id_type=MESH).wait_recv()
        for cp in sends:
            cp.wait_send()
        mine.wait()

    return pl.pallas_call(
        body, out_shape=jax.ShapeDtypeStruct(g.shape, g.dtype), in_specs=[ANY], out_specs=ANY,
        scratch_shapes=[pltpu.SemaphoreType.DMA((3,)), pltpu.SemaphoreType.DMA((3,)), pltpu.SemaphoreType.DMA],
        name=name,
    )(g)


def swap_sibling(p, *, name):
    def body(p_ref, out_ref, send_sem, recv_sem):
        x, y, c = lax.axis_index("x"), lax.axis_index("y"), lax.axis_index("c")
        cp = pltpu.make_async_remote_copy(src_ref=p_ref, dst_ref=out_ref, send_sem=send_sem, recv_sem=recv_sem,
                                          device_id=(x, y, 1 - c), device_id_type=MESH)
        cp.start()
        cp.wait()

    return pl.pallas_call(
        body, out_shape=jax.ShapeDtypeStruct(p.shape, p.dtype), in_specs=[ANY], out_specs=ANY,
        scratch_shapes=[pltpu.SemaphoreType.DMA, pltpu.SemaphoreType.DMA], name=name,
    )(p)


def swap_other_half(g, *, name):
    n, r, cdim = g.shape
    half = r // 2

    def body(g_ref, out_ref, send_sem, recv_sem):
        x, y, c = lax.axis_index("x"), lax.axis_index("y"), lax.axis_index("c")
        cp = pltpu.make_async_remote_copy(src_ref=g_ref.at[:, pl.ds((1 - c) * half, half), :], dst_ref=out_ref, send_sem=send_sem,
                                          recv_sem=recv_sem, device_id=(x, y, 1 - c), device_id_type=MESH)
        cp.start()
        cp.wait()

    return pl.pallas_call(
        body, out_shape=jax.ShapeDtypeStruct((n, half, cdim), g.dtype), in_specs=[ANY], out_specs=ANY,
        scratch_shapes=[pltpu.SemaphoreType.DMA, pltpu.SemaphoreType.DMA], name=name,
    )(g)


def add_pairs(a, b, *, name, out_dtype):
    n, rows, cdim = a.shape
    t = _pick(rows, (256, 128, 64, 32, 16))

    def body(a_ref, b_ref, o_ref):
        o_ref[...] = (a_ref[...].astype(F32) + b_ref[...].astype(F32)).astype(o_ref.dtype)

    spec = pl.BlockSpec((n, t, cdim), lambda i: (0, i, 0))
    return pl.pallas_call(
        body, grid=(rows // t,), in_specs=[spec, spec], out_specs=spec, out_shape=jax.ShapeDtypeStruct(a.shape, out_dtype),
        compiler_params=_cp(("parallel",)), name=name,
    )(a, b)


def sum_slots(r, *, name):
    n, rows, cdim = r.shape
    t = _pick(rows, (256, 128, 64, 32, 16, 8))

    def body(r_ref, o_ref):
        acc = r_ref[0].astype(F32)
        for s in range(1, n):
            acc = acc + r_ref[s].astype(F32)
        o_ref[...] = acc

    return pl.pallas_call(
        body, grid=(rows // t,), in_specs=[pl.BlockSpec((n, t, cdim), lambda i: (0, i, 0))],
        out_specs=pl.BlockSpec((t, cdim), lambda i: (i, 0)), out_shape=jax.ShapeDtypeStruct((rows, cdim), F32),
        compiler_params=_cp(("parallel",)), name=name,
    )(r)


def _rms(x, g):
    return x * lax.rsqrt(jnp.mean(x * x, axis=-1, keepdims=True) + NORM_EPS) * g


def _adaln(x, g, shift, scale):
    return _rms(x, g) * (1.0 + scale) + shift


def _silu(x):
    return x * jax.nn.sigmoid(x)


def _gdn_gate(o, z, g):
    return jnp.concatenate([_rms(o[:, 128 * h:128 * (h + 1)], g) * _silu(z[:, 128 * h:128 * (h + 1)]) for h in range(4)], axis=1)


def _ssd_gate(y, z0, z1, z2, z3, g):
    outs = []
    for k, z in enumerate((z0, z1, z2, z3)):
        t = y[:, 512 * k:512 * (k + 1)] * _silu(z)
        outs.append(t * lax.rsqrt(jnp.mean(t * t, axis=-1, keepdims=True) + NORM_EPS))
    return jnp.concatenate(outs, axis=1) * g


def _rope(x, cos, sin, rot):
    return x * cos + _dot(x, rot, hi=True) * sin


def _rope_q(q, cos, sin, rot):
    parts = []
    for h in range(4):
        parts += [q[:, 256 * h:256 * h + 128], _rope(q[:, 256 * h + 128:256 * (h + 1)], cos, sin, rot)]
    return jnp.concatenate(parts, axis=1) * ATT_SCALE


def _rope_t(d, cos, sin, rot):
    return d * cos + _dot(d * sin, rot, _NT, hi=True)


def _vjp_rows(fn, n_rows, n_pars, out_dtypes, rows, cts, pars, *, name, tile=256, extra=None):
    nct = len(cts)

    def bwd(*a):
        r, c, e, p = a[:n_rows], a[n_rows:n_rows + nct], a[n_rows + nct:len(a) - n_pars], a[len(a) - n_pars:]
        out, vjp = jax.vjp(fn, *[t.astype(F32) for t in r], *p)
        ct = tuple(t.astype(F32) for t in c)
        grads = vjp(ct[0] if not isinstance(out, tuple) else ct)
        drows = list(grads[:n_rows])
        if e:
            drows[0] = drows[0] + e[0]
        return (*drows, *grads[n_rows:])

    return rowmap(bwd, list(rows) + list(cts) + ([extra] if extra is not None else []), list(pars), out_dtypes,
                  name=name, tile=tile, n_reduce=n_pars)


ADAM_LR, ADAM_B1, ADAM_B2, ADAM_EPS, ADAM_WD, ADAM_STEP = 0.001, 0.9, 0.999, 1e-08, 0.01, 10


def _adam_math(w, g, m, v):
    m = ADAM_B1 * m + (1.0 - ADAM_B1) * g
    v = ADAM_B2 * v + (1.0 - ADAM_B2) * (g * g)
    m_hat = m / (1.0 - ADAM_B1 ** ADAM_STEP)
    v_hat = v / (1.0 - ADAM_B2 ** ADAM_STEP)
    delta = -ADAM_LR * (m_hat / (jnp.sqrt(v_hat) + ADAM_EPS) + ADAM_WD * w)
    return delta, m, v


def adamw(w, gs, m, v, *, name):
    shape = w.shape
    last = shape[-1]
    to2 = lambda a: a.reshape(-1, last)
    rows = w.size // last
    tile = _pick(rows, (256, 128, 64, 32, 16, 8))
    ng = len(gs)

    def fn(w_, *rest):
        g = rest[0]
        for t in rest[1:ng]:
            g = g + t
        m_, v_ = rest[ng], rest[ng + 1]
        return (g, *_adam_math(w_, g, m_, v_))

    outs = rowmap(fn, [to2(w)] + [to2(g) for g in gs] + [to2(m), to2(v)], [], (F32,) * 4, name=name, tile=tile)
    return tuple(o.reshape(shape) for o in outs)


PACK_W = 1024
BIG = (
    ("ffn_w1", (4, 2, 1024, 704), 3), ("ffn_w3", (4, 2, 1024, 704), 3), ("ffn_w2", (4, 2, 704, 1024), 2),
    ("ev_w_in", (2, 1024, 690), 2), ("mla_w_uq", (2, 96, 4, 192), 1), ("mla_w_ukv", (2, 64, 4, 256), 1),
    ("ev_w_out", (2, 256, 1024), 1), ("ssd_w_in", (2, 1024, 1288), 2), ("ssd_w_out", (2, 512, 1024), 1))


def _seg_rows(shape):
    n = math.prod(shape)
    return -(-n // (16 * PACK_W)) * 16


PACK_ROWS = -(-sum(_seg_rows(sh) for _, sh, _ in BIG) // 512) * 512


def _pack(shards, dtype):
    parts = []
    for (_, shape, _), a in zip(BIG, shards):
        flat = a.reshape(-1).astype(dtype)
        pad = _seg_rows(shape) * PACK_W - flat.shape[0]
        parts.append(jnp.pad(flat, (0, pad)) if pad else flat)
    tail = PACK_ROWS - sum(_seg_rows(sh) for _, sh, _ in BIG)
    if tail:
        parts.append(jnp.zeros((tail * PACK_W,), dtype))
    return jnp.concatenate(parts).reshape(-1, PACK_W)


def _unpack(buf):
    out, r0 = [], 0
    for _, shape, _ in BIG:
        n = math.prod(shape)
        out.append(buf[r0:r0 + _seg_rows(shape)].reshape(-1)[:n].reshape(shape))
        r0 += _seg_rows(shape)
    return out


SMALL_SHARDED = (
    ("norm_g", (4, 3, 256), 2), ("gdn_conv_w", (2, 4, 384), 2), ("ssd_conv_w", (2, 4, 768), 2),
    ("ssd_conv_b", (2, 768), 1), ("ssd_norm_g", (2, 512), 1))


def _flat_pack(arrs, width, row_mult):
    flat = jnp.concatenate([a.reshape(-1).astype(F32) for a in arrs])
    n = flat.shape[0]
    tot = -(-n // (width * row_mult)) * width * row_mult
    return jnp.pad(flat, (0, tot - n)).reshape(-1, width)


def _flat_unpack(buf, shapes):
    flat = buf.reshape(-1)
    out, o = [], 0
    for s in shapes:
        n = math.prod(s)
        out.append(flat[o:o + n].reshape(s))
        o += n
    return out


def _rep(v, n):
    return jnp.repeat(v, n, axis=-1)


def kernel(x, c, positions, ada_w, ada_b, norm_g, ffn_w1, ffn_w3, ffn_w2, ev_w_in, gdn_conv_w, gdn_A_log, gdn_dt_bias, gdn_norm_g, mla_q_norm_g, mla_w_uq, mla_kv_norm_g, mla_w_ukv, ev_w_out, ssd_w_in, ssd_conv_w, ssd_conv_b, ssd_A_log, ssd_dt_bias, ssd_D, ssd_norm_g, ssd_w_out, final_g, loss_target, m_ada_w, m_ada_b, m_norm_g, m_ffn_w1, m_ffn_w3, m_ffn_w2, m_ev_w_in, m_gdn_conv_w, m_gdn_A_log, m_gdn_dt_bias, m_gdn_norm_g, m_mla_q_norm_g, m_mla_w_uq, m_mla_kv_norm_g, m_mla_w_ukv, m_ev_w_out, m_ssd_w_in, m_ssd_conv_w, m_ssd_conv_b, m_ssd_A_log, m_ssd_dt_bias, m_ssd_D, m_ssd_norm_g, m_ssd_w_out, m_final_g, v_ada_w, v_ada_b, v_norm_g, v_ffn_w1, v_ffn_w3, v_ffn_w2, v_ev_w_in, v_gdn_conv_w, v_gdn_A_log, v_gdn_dt_bias, v_gdn_norm_g, v_mla_q_norm_g, v_mla_w_uq, v_mla_kv_norm_g, v_mla_w_ukv, v_ev_w_out, v_ssd_w_in, v_ssd_conv_w, v_ssd_conv_b, v_ssd_A_log, v_ssd_dt_bias, v_ssd_D, v_ssd_norm_g, v_ssd_w_out, v_final_g):
    P = dict(ada_w=ada_w, ada_b=ada_b, norm_g=norm_g, ffn_w1=ffn_w1, ffn_w3=ffn_w3, ffn_w2=ffn_w2, ev_w_in=ev_w_in, gdn_conv_w=gdn_conv_w, gdn_A_log=gdn_A_log, gdn_dt_bias=gdn_dt_bias, gdn_norm_g=gdn_norm_g, mla_q_norm_g=mla_q_norm_g, mla_w_uq=mla_w_uq, mla_kv_norm_g=mla_kv_norm_g, mla_w_ukv=mla_w_ukv, ev_w_out=ev_w_out, ssd_w_in=ssd_w_in, ssd_conv_w=ssd_conv_w, ssd_conv_b=ssd_conv_b, ssd_A_log=ssd_A_log, ssd_dt_bias=ssd_dt_bias, ssd_D=ssd_D, ssd_norm_g=ssd_norm_g, ssd_w_out=ssd_w_out, final_g=final_g)
    M1 = dict(ada_w=m_ada_w, ada_b=m_ada_b, norm_g=m_norm_g, ffn_w1=m_ffn_w1, ffn_w3=m_ffn_w3, ffn_w2=m_ffn_w2, ev_w_in=m_ev_w_in, gdn_conv_w=m_gdn_conv_w, gdn_A_log=m_gdn_A_log, gdn_dt_bias=m_gdn_dt_bias, gdn_norm_g=m_gdn_norm_g, mla_q_norm_g=m_mla_q_norm_g, mla_w_uq=m_mla_w_uq, mla_kv_norm_g=m_mla_kv_norm_g, mla_w_ukv=m_mla_w_ukv, ev_w_out=m_ev_w_out, ssd_w_in=m_ssd_w_in, ssd_conv_w=m_ssd_conv_w, ssd_conv_b=m_ssd_conv_b, ssd_A_log=m_ssd_A_log, ssd_dt_bias=m_ssd_dt_bias, ssd_D=m_ssd_D, ssd_norm_g=m_ssd_norm_g, ssd_w_out=m_ssd_w_out, final_g=m_final_g)
    M2 = dict(ada_w=v_ada_w, ada_b=v_ada_b, norm_g=v_norm_g, ffn_w1=v_ffn_w1, ffn_w3=v_ffn_w3, ffn_w2=v_ffn_w2, ev_w_in=v_ev_w_in, gdn_conv_w=v_gdn_conv_w, gdn_A_log=v_gdn_A_log, gdn_dt_bias=v_gdn_dt_bias, gdn_norm_g=v_gdn_norm_g, mla_q_norm_g=v_mla_q_norm_g, mla_w_uq=v_mla_w_uq, mla_kv_norm_g=v_mla_kv_norm_g, mla_w_ukv=v_mla_w_ukv, ev_w_out=v_ev_w_out, ssd_w_in=v_ssd_w_in, ssd_conv_w=v_ssd_conv_w, ssd_conv_b=v_ssd_conv_b, ssd_A_log=v_ssd_A_log, ssd_dt_bias=v_ssd_dt_bias, ssd_D=v_ssd_D, ssd_norm_g=v_ssd_norm_g, ssd_w_out=v_ssd_w_out, final_g=v_final_g)
    names = list(P)
    xi, yi, ci = lax.axis_index("x"), lax.axis_index("y"), lax.axis_index("c")
    chip = 2 * xi + yi
    bidx = 4 * xi + 2 * yi + ci
    xa = x[0]
    S_, D = xa.shape
    tgt = loss_target[0]
    depth = ffn_w1.shape[0]

    wg = allgather_chips_2level(_pack([P[n] for n, _, _ in BIG], BF16), name="gather_weights")
    per_chip = [_unpack(wg[s]) for s in range(4)]
    W = {n: jnp.concatenate([per_chip[s][k] for s in range(4)], axis=ax) for k, (n, _, ax) in enumerate(BIG)}
    sg = allgather_chips(_flat_pack([P[n] for n, _, _ in SMALL_SHARDED], 1024, 16), name="gather_small")
    per_chip_s = [_flat_unpack(sg[s], [sh for _, sh, _ in SMALL_SHARDED]) for s in range(4)]
    Wf = {n: jnp.concatenate([per_chip_s[s][k] for s in range(4)], axis=ax) for k, (n, _, ax) in enumerate(SMALL_SHARDED)}

    c_all = allgather8(jnp.pad(c, ((0, 7), (0, 0))), name="gather_c").reshape(8, 8, D)[:, 0]
    c_act, = rowmap(lambda t: (_silu(t),), [jnp.pad(c_all, ((0, 8), (0, 0)))], [], (F32,), name="c_act", tile=16)
    ncol = ada_w.shape[2]
    ada_b_loc = lax.dynamic_slice(ada_b, (0, chip * ncol), (depth, ncol))
    mod_loc = [mm((c_act, ada_w[l]), name=f"mod_{l}", epi=lambda acc, b: (acc + b,), epi_pars=(ada_b_loc[l][None],),
                  epi_out_dtypes=(F32,), tm=16, tn=256)[0][:8] for l in range(depth)]
    mod_g = allgather8(jnp.stack(mod_loc).reshape(-1, 1024), name="gather_mod").reshape(8, depth, 8, ncol)
    mod_b = lax.dynamic_index_in_dim(mod_g[0::2], bidx, axis=2, keepdims=False)
    mod = jnp.transpose(mod_b, (1, 0, 2)).reshape(depth, 3, 3, D)

    def ev_ext(w):
        z = lambda n: jnp.zeros((w.shape[0], n), w.dtype)
        return jnp.concatenate([w[:, :2048], _rep(w[:, 2048:2052], 128), _rep(w[:, 2052:2056], 128), w[:, 2056:2440], z(128),
                                w[:, 2440:2696], w[:, 2696:2760], z(192)], axis=1)

    def ev_ext_t(dw):
        return jnp.concatenate([dw[:, :2048], dw[:, 2048:2560].reshape(-1, 4, 128).sum(-1), dw[:, 2560:3072].reshape(-1, 4, 128).sum(-1),
                                dw[:, 3072:3456], dw[:, 3584:3840], dw[:, 3840:3904]], axis=1)

    def od_ext(w):
        return jnp.concatenate([w[:, 2048:5120], w[:, :2048], _rep(w[:, 5120:5152], 64)], axis=1)

    def od_ext_t(dw):
        return jnp.concatenate([dw[:, 3072:5120], dw[:, :3072], dw[:, 5120:].reshape(-1, 32, 64).sum(-1)], axis=1)

    def wq_ext(w):
        return jnp.pad(w, ((0, 0), (0, 0), (0, 64))).reshape(384, 1024)

    def wq_ext_t(dw):
        return dw.reshape(384, 4, 256)[:, :, :192]

    def wkv_ext(w):
        return jnp.concatenate([w[:, :, :128].reshape(256, 512), w[:, :, 128:].reshape(256, 512)], axis=1)

    def wkv_ext_t(dw):
        return jnp.concatenate([dw[:, :512].reshape(256, 4, 128), dw[:, 512:].reshape(256, 4, 128)], axis=2)

    half = 32
    inv_freq = 10000.0 ** (-jnp.arange(half, dtype=F32) / half)
    ang = positions[0].astype(F32)[:, None] * inv_freq
    zpad = jnp.zeros((S_, 64), F32)
    cos_t = jnp.concatenate([jnp.cos(ang), jnp.cos(ang), zpad], axis=1)
    sin_t = jnp.concatenate([jnp.sin(ang), jnp.sin(ang), zpad], axis=1)
    ii = jnp.arange(128)
    rot = (jnp.where((ii[:, None] < 32) & (ii[None, :] == ii[:, None] + 32), 1.0, 0.0)
           - jnp.where((ii[:, None] >= 32) & (ii[:, None] < 64) & (ii[None, :] == ii[:, None] - 32), 1.0, 0.0)).astype(F32)

    grads = {}
    dmod = [[[None] * 3 for _ in range(3)] for _ in range(depth)]
    dnorm_g = [[None] * 3 for _ in range(depth)]

    def acc(name, idx, val):
        grads.setdefault(name, {})[idx] = val

    def ffn_sub(xin, l, k, j):
        g, (shift, scale, gate) = Wf["norm_g"][l, k][None], [mod[l, k, t][None] for t in range(3)]
        w1, w3, w2 = W["ffn_w1"][l, j], W["ffn_w3"][l, j], W["ffn_w2"][l, j]
        tag = f"l{l}f{j}"
        h, = rowmap(lambda *a: (_adaln(*a),), [xin], [g, shift, scale], (BF16,), name=f"adaln_{tag}")
        a = ffn_mid_fwd(h, w1, w3, name=f"ffn_mid_{tag}")
        xn, y = mm((a, w2), name=f"ffn_out_{tag}", epi=lambda acc_, xr, gt: (xr + 0.5 * gt * acc_, acc_), epi_rows=(xin,),
                   epi_pars=(gate,), epi_out_dtypes=(F32, F32))

        def bwd(dxn):
            dy, dgate = rowmap(lambda d, y_, gt: ((0.5 * gt) * d, jnp.sum(0.5 * y_ * d, axis=0, keepdims=True)), [dxn, y], [gate],
                               (BF16, F32), name=f"dres_{tag}", n_reduce=1)
            du, dv, a_ = ffn_mid_bwd(h, dy, w1, w3, w2, name=f"ffn_midb_{tag}")
            dh = mm([(du, w1), (dv, w3)], tb=True, name=f"ffn_dh_{tag}")
            acc("ffn_w1", (l, j), mm((h, du), ta=True, name=f"ffn_dw1_{tag}"))
            acc("ffn_w3", (l, j), mm((h, dv), ta=True, name=f"ffn_dw3_{tag}"))
            acc("ffn_w2", (l, j), mm((a_, dy), ta=True, name=f"ffn_dw2_{tag}"))
            dx, dg, dsh, dsc = _vjp_rows(_adaln, 1, 3, (F32,), [xin], [dh], [g, shift, scale], name=f"adalnb_{tag}", extra=dxn)
            dnorm_g[l][k] = dg[0]
            dmod[l][k] = [dsh[0], dsc[0], dgate[0]]
            return dx

        return xn, bwd

    def mixer_tail(xin, l, tag, dh, dxn, g, shift, scale, dgate):
        dx, dg, dsh, dsc = _vjp_rows(_adaln, 1, 3, (F32,), [xin], [dh], [g, shift, scale], name=f"adalnb_{tag}", extra=dxn)
        dnorm_g[l][1] = dg[0]
        dmod[l][1] = [dsh[0], dsc[0], dgate[0]]
        return dx

    def even_sub(xin, l):
        e = l // 2
        tag = f"l{l}m"
        g, (shift, scale, gate) = Wf["norm_g"][l, 1][None], [mod[l, 1, t][None] for t in range(3)]
        wext, wq, wkv, wout = ev_ext(W["ev_w_in"][e]), wq_ext(W["mla_w_uq"][e]), wkv_ext(W["mla_w_ukv"][e]), W["ev_w_out"][e]
        conv_w, zb = Wf["gdn_conv_w"][e], jnp.zeros((1, 1536), F32)
        alog_e, dtb_e = _rep(gdn_A_log[e], 128)[None], _rep(gdn_dt_bias[e], 128)[None]
        gg, qg, kvg = gdn_norm_g[e][None], mla_q_norm_g[e][None], mla_kv_norm_g[e][None]
        h, = rowmap(lambda *a: (_adaln(*a),), [xin], [g, shift, scale], (BF16,), name=f"adaln_{tag}")
        proj = mm((h, wext), name=f"ev_in_{tag}")
        qkvc = conv_fwd(proj, 0, 3, conv_w, zb, name=f"gdn_conv_{tag}")
        o_g, hist = gdn_fwd(qkvc, proj, 4, 5, alog_e, dtb_e, name=f"gdn_{tag}")
        o_a, = rowmap(lambda o, z, g_: (_gdn_gate(o, z, g_),), [o_g, (proj, 512, 3)], [gg], (BF16,), name=f"gdn_gate_{tag}")
        cqn, = rowmap(lambda t, g_: (_rms(t, g_),), [(proj, 384, 8)], [qg], (BF16,), name=f"q_norm_{tag}")
        ckvn, = rowmap(lambda t, g_: (_rms(t, g_),), [(proj, 256, 14)], [kvg], (BF16,), name=f"kv_norm_{tag}")
        q0 = mm((cqn, wq), name=f"q_up_{tag}")
        kv = mm((ckvn, wkv), name=f"kv_up_{tag}", out_dtype=BF16)
        q, = rowmap(lambda t, cs, sn, r: (_rope_q(t, cs, sn, r),), [q0, cos_t, sin_t], [rot], (BF16,), name=f"rope_q_{tag}")
        kp, = rowmap(lambda t, cs, sn, r: (_rope(t, cs, sn, r),), [(proj, 128, 30), cos_t, sin_t], [rot], (BF16,), name=f"rope_k_{tag}")
        o_b, lse = att_fwd(q, kv, kp, name=f"att_{tag}")
        xn, y = mm([(o_a, wout[:512]), (o_b, wout[512:])], name=f"ev_out_{tag}", epi=lambda acc_, xr, gt: (xr + gt * acc_, acc_),
                   epi_rows=(xin,), epi_pars=(gate,), epi_out_dtypes=(F32, F32))

        def bwd(dxn):
            dy, dgate = rowmap(lambda d, y_, gt: (gt * d, jnp.sum(y_ * d, axis=0, keepdims=True)), [dxn, y], [gate],
                               (BF16, F32), name=f"dres_{tag}", n_reduce=1)
            do_a = mm((dy, wout[:512]), tb=True, name=f"ev_doa_{tag}")
            do_b = mm((dy, wout[512:]), tb=True, name=f"ev_dob_{tag}")
            acc("ev_w_out", e, jnp.concatenate([mm((o_a, dy), ta=True, name=f"ev_dwoa_{tag}"), mm((o_b, dy), ta=True, name=f"ev_dwob_{tag}")], axis=0))
            dsum, = rowmap(lambda d, o_: (jnp.concatenate([jnp.broadcast_to(jnp.sum(d[:, 128 * hh:128 * (hh + 1)] * o_[:, 128 * hh:128 * (hh + 1)],
                                                                                        axis=-1, keepdims=True), (d.shape[0], 128))
                                                  for hh in range(4)], axis=1),), [do_b, o_b], [], (F32,), name=f"att_dsum_{tag}")
            dq4, dk2, dv = att_bwd(q, kv, kp, lse, dsum, do_b, name=f"att_bwd_{tag}")

            def rope_qb(d0, d1, d2, d3, cs, sn, r):
                parts = []
                for d in (d0, d1, d2, d3):
                    parts += [d[:, :128], _rope_t(d[:, 128:], cs, sn, r)]
                return (jnp.concatenate(parts, axis=1) * ATT_SCALE,)

            dq0, = rowmap(rope_qb, [dq4[0], dq4[1], dq4[2], dq4[3], cos_t, sin_t], [rot], (BF16,), name=f"rope_qb_{tag}")

            def rope_kb(d, cs, sn, r):
                dkp = d[:, 128:256] + d[:, 384:512] + d[:, 640:768] + d[:, 896:1024]
                return jnp.concatenate([d[:, 256 * hh:256 * hh + 128] for hh in range(4)], axis=1), _rope_t(dkp, cs, sn, r)

            dkn, dkr = rowmap(rope_kb, [dk2, cos_t, sin_t], [rot], (BF16, BF16), name=f"rope_kb_{tag}")
            dcqn = mm((dq0, wq), tb=True, name=f"q_upb_{tag}")
            acc("mla_w_uq", e, wq_ext_t(mm((cqn, dq0), ta=True, name=f"q_dw_{tag}")))
            dckvn = mm([(dkn, wkv[:, :512]), (dv, wkv[:, 512:])], tb=True, name=f"kv_upb_{tag}")
            acc("mla_w_ukv", e, wkv_ext_t(jnp.concatenate([mm((ckvn, dkn), ta=True, name=f"kv_dwk_{tag}"), mm((ckvn, dv), ta=True, name=f"kv_dwv_{tag}")], axis=1)))
            dcq, dqg = _vjp_rows(_rms, 1, 1, (BF16,), [(proj, 384, 8)], [dcqn], [qg], name=f"q_normb_{tag}")
            dckv, dkvg = _vjp_rows(_rms, 1, 1, (BF16,), [(proj, 256, 14)], [dckvn], [kvg], name=f"kv_normb_{tag}")
            acc("mla_q_norm_g", e, dqg[0])
            acc("mla_kv_norm_g", e, dkvg[0])
            do_g, dz, dgg = _vjp_rows(_gdn_gate, 2, 1, (F32, BF16), [o_g, (proj, 512, 3)], [do_a], [gg], name=f"gdn_gateb_{tag}")
            acc("gdn_norm_g", e, dgg[0])
            dqkvc, dbe, dae, dal, ddt = gdn_bwd(qkvc, proj, 4, 5, alog_e, dtb_e, hist, do_g, name=f"gdnb_{tag}")
            acc("gdn_A_log", e, dal.reshape(4, 128).sum(-1))
            acc("gdn_dt_bias", e, ddt.reshape(4, 128).sum(-1))
            dpre, dcw, _ = conv_bwd_pre(proj, 0, 3, conv_w, zb, dqkvc, name=f"gdn_convb_{tag}")
            acc("gdn_conv_w", e, dcw)
            dqkv = conv_bwd_x(dpre, conv_w, name=f"gdn_convx_{tag}", out_dtype=BF16)
            zc = lambda n: jnp.zeros((S_, n), BF16)
            dproj = jnp.concatenate([dqkv, dz, dbe.astype(BF16), dae.astype(BF16), dcq, zc(128), dckv, dkr, zc(128)], axis=1)
            dh = mm((dproj, wext), tb=True, name=f"ev_inb_{tag}")
            acc("ev_w_in", e, ev_ext_t(mm((h, dproj), ta=True, name=f"ev_dwin_{tag}")))
            return mixer_tail(xin, l, tag, dh, dxn, g, shift, scale, dgate)

        return xn, bwd

    def odd_sub(xin, l):
        o = l // 2
        tag = f"l{l}m"
        g, (shift, scale, gate) = Wf["norm_g"][l, 1][None], [mod[l, 1, t][None] for t in range(3)]
        wext, wout = od_ext(W["ssd_w_in"][o]), W["ssd_w_out"][o]
        conv_w, conv_b, ng = Wf["ssd_conv_w"][o], Wf["ssd_conv_b"][o][None], Wf["ssd_norm_g"][o][None]
        ex = lambda v: _rep(v, 64).reshape(4, 1, 512)
        na_e, dtb_e, dsk_e = ex(-jnp.exp(ssd_A_log[o])), ex(ssd_dt_bias[o]), ex(ssd_D[o])
        h, = rowmap(lambda *a: (_adaln(*a),), [xin], [g, shift, scale], (BF16,), name=f"adaln_{tag}")
        proj = mm((h, wext), name=f"ssd_in_{tag}")
        zv = [(proj, 512, 6 + t) for t in range(4)]
        xbc = conv_fwd(proj, 0, 6, conv_w, conv_b, name=f"ssd_conv_{tag}")
        ys, hist = ssd_fwd(xbc, proj, na_e, dtb_e, dsk_e, name=f"ssd_{tag}")
        yn, = rowmap(lambda *a: (_ssd_gate(*a),), [ys] + zv, [ng], (BF16,), name=f"ssd_gate_{tag}", tile=128)
        xn, y = mm((yn, wout), name=f"ssd_out_{tag}", epi=lambda acc_, xr, gt: (xr + gt * acc_, acc_), epi_rows=(xin,),
                   epi_pars=(gate,), epi_out_dtypes=(F32, F32))

        def bwd(dxn):
            dy, dgate = rowmap(lambda d, y_, gt: (gt * d, jnp.sum(y_ * d, axis=0, keepdims=True)), [dxn, y], [gate],
                               (BF16, F32), name=f"dres_{tag}", n_reduce=1)
            dyn = mm((dy, wout), tb=True, name=f"ssd_dyn_{tag}", out_dtype=BF16)
            acc("ssd_w_out", o, mm((yn, dy), ta=True, name=f"ssd_dwout_{tag}"))
            dys, dz0, dz1, dz2, dz3, dng = _vjp_rows(_ssd_gate, 5, 1, (F32, BF16, BF16, BF16, BF16), [ys] + zv, [dyn], [ng],
                                                     name=f"ssd_gateb_{tag}", tile=128)
            acc("ssd_norm_g", o, dng[0])
            dxs, ddtx, db_, dc_, dna, ddtb, ddsk = ssd_bwd(xbc, proj, na_e, dtb_e, dsk_e, hist, dys, name=f"ssdb_{tag}")
            acc("ssd_A_log", o, dna.reshape(32, 64).sum(-1) * (-jnp.exp(ssd_A_log[o])))
            acc("ssd_dt_bias", o, ddtb.reshape(32, 64).sum(-1))
            acc("ssd_D", o, ddsk.reshape(32, 64).sum(-1))
            dxbc = jnp.concatenate([dxs, db_, dc_], axis=1)
            dpre, dcw, dcb = conv_bwd_pre(proj, 0, 6, conv_w, conv_b, dxbc, name=f"ssd_convb_{tag}")
            acc("ssd_conv_w", o, dcw)
            acc("ssd_conv_b", o, dcb[0])
            dxp = conv_bwd_x(dpre, conv_w, name=f"ssd_convx_{tag}", out_dtype=BF16)
            dproj = jnp.concatenate([dxp, dz0, dz1, dz2, dz3, ddtx.astype(BF16)], axis=1)
            dh = mm((dproj, wext), tb=True, name=f"ssd_inb_{tag}")
            acc("ssd_w_in", o, od_ext_t(mm((h, dproj), ta=True, name=f"ssd_dwin_{tag}")))
            return mixer_tail(xin, l, tag, dh, dxn, g, shift, scale, dgate)

        return xn, bwd

    tape = []
    xc = xa
    for l in range(depth):
        xc, b0 = ffn_sub(xc, l, 0, 0)
        xc, b1 = (even_sub if l % 2 == 0 else odd_sub)(xc, l)
        xc, b2 = ffn_sub(xc, l, 2, 1)
        tape += [b0, b1, b2]

    def head(xr, tg, g_):
        def f(xv, gv):
            err = _rms(xv, gv) - tg
            return 0.5 * jnp.sum(jnp.mean(err * err, axis=-1, keepdims=True), axis=0, keepdims=True)
        lo, vjp = jax.vjp(f, xr, g_)
        dxv, dgv = vjp(jnp.ones_like(lo))
        return dxv, jnp.broadcast_to(lo, (1, 128)), dgv

    dx, loss_p, dfg = rowmap(head, [xc, tgt], [final_g[None]], (F32,), name="loss_head", n_reduce=2)
    loss = lax.psum(loss_p[0, 0], ("x", "y", "c"))

    for b in reversed(tape):
        dx = b(dx)
    grad_x = dx[None]

    full = {n: jnp.stack([grads[n][k] for k in sorted(grads[n])]) for n in ("ev_w_in", "mla_w_uq", "mla_w_ukv", "ev_w_out", "ssd_w_in", "ssd_w_out")}
    for n in ("ffn_w1", "ffn_w3", "ffn_w2"):
        full[n] = jnp.stack([jnp.stack([grads[n][(l, j)] for j in range(2)]) for l in range(depth)])
    chunks = []
    for s in range(4):
        chunks.append(_pack([lax.slice_in_dim(full[n], s * sh[ax], (s + 1) * sh[ax], axis=ax) for n, sh, ax in BIG], BF16))
    gall = jnp.stack(chunks)
    half = PACK_ROWS // 2
    from_sib = swap_other_half(gall, name="swap_half")
    pair = add_pairs(lax.dynamic_slice_in_dim(gall, ci * half, half, axis=1), from_sib, name="add_sibling", out_dtype=BF16)
    recv = exchange_chips(pair, name="exchange_grads")
    part = sum_slots(recv, name="sum_chips")
    sib = swap_sibling(part, name="swap_sibling")
    lo = jnp.where(ci == 0, part, sib)
    hi_ = jnp.where(ci == 0, sib, part)
    g_tot = _unpack(jnp.concatenate([lo, hi_], axis=0))

    dmod_flat = jnp.stack([jnp.stack([jnp.stack(dmod[l][k]) for k in range(3)]) for l in range(depth)]).reshape(depth, 9 * D)
    small_names = ["norm_g", "gdn_conv_w", "gdn_A_log", "gdn_dt_bias", "gdn_norm_g", "mla_q_norm_g", "mla_kv_norm_g",
                   "ssd_conv_w", "ssd_conv_b", "ssd_A_log", "ssd_dt_bias", "ssd_D", "ssd_norm_g", "final_g"]
    small_full = {n: jnp.stack([grads[n][k] for k in sorted(grads[n])]) for n in small_names if n in grads}
    small_full["norm_g"] = jnp.stack([jnp.stack(dnorm_g[l]) for l in range(depth)])
    small_full["final_g"] = dfg[0]
    small_list = [dmod_flat] + [small_full[n] for n in small_names]
    small_shapes = [a.shape for a in small_list]
    sp = _flat_pack(small_list, 128, 8)
    sgath = allgather8(sp, name="gather_small_grads").reshape(8, sp.shape[0], 128)
    ssum = sum_slots(sgath, name="sum_small")
    tot = dict(zip(["ada_b"] + small_names, _flat_unpack(ssum, small_shapes)))
    dmod_all = sgath.reshape(8, -1)[:, :depth * 9 * D].reshape(8, depth, 9 * D)
    dmod_loc = lax.dynamic_slice(dmod_all, (0, 0, chip * ncol), (8, depth, ncol))
    g_ada_w = jnp.stack([mm((c_act, jnp.pad(dmod_loc[:, l], ((0, 8), (0, 0)))), ta=True, name=f"ada_dw_{l}", tk=16, tn=256)
                         for l in range(depth)])

    def own(n, a):
        for m_, sh, ax in SMALL_SHARDED:
            if m_ == n:
                return lax.dynamic_slice_in_dim(a, chip * sh[ax], sh[ax], axis=ax)
        return a

    res = {}
    for k, (n, _, _) in enumerate(BIG):
        res[n] = adamw(P[n], [g_tot[k]], M1[n], M2[n], name=f"adamw_{n}")
    res["ada_w"] = adamw(ada_w, [g_ada_w], m_ada_w, v_ada_w, name="adamw_ada_w")
    sm = ["ada_b"] + small_names
    shapes = [P[n].shape for n in sm]
    pk = lambda d: _flat_pack([d[n] for n in sm], 128, 8)
    outs = adamw(pk(P), [pk({n: own(n, tot[n]).reshape(P[n].shape) for n in sm})], pk(M1), pk(M2), name="adamw_small")
    un = [_flat_unpack(o, shapes) for o in outs]
    for i, n in enumerate(sm):
        res[n] = tuple(un[t][i] for t in range(4))
    return (loss, grad_x, *[res[n][0] for n in names], *[res[n][1] for n in names], *[res[n][2] for n in names], *[res[n][3] for n in names])
```

```python
import functools
import math

import jax
import jax.numpy as jnp
from jax import lax
from jax.experimental import pallas as pl
from jax.experimental.pallas import tpu as pltpu

F32 = jnp.float32
BF16 = jnp.bfloat16
HI = lax.Precision.HIGHEST
HI3 = lax.Precision.HIGH
VMEM_LIMIT = 56 * 1024 * 1024
NORM_EPS = 1e-6
MM_VMEM_BUDGET = 40 * 1024 * 1024


def _cp(sem=None):
    if sem is None:
        return pltpu.CompilerParams(vmem_limit_bytes=VMEM_LIMIT)
    return pltpu.CompilerParams(dimension_semantics=sem, vmem_limit_bytes=VMEM_LIMIT)


def _pick(dim, prefs):
    for p in prefs:
        if dim % p == 0:
            return p
    return dim


def mm(pairs, *, ta=False, tb=False, out_dtype=F32, name, epi=None, epi_rows=(), epi_pars=(), epi_out_dtypes=None,
       tm=None, tn=None, tk=None):
    if not isinstance(pairs, (list, tuple)) or not isinstance(pairs[0], (list, tuple)):
        pairs = [pairs]
    npair = len(pairs)
    a0, b0 = pairs[0]
    M = a0.shape[1] if ta else a0.shape[0]
    K = a0.shape[0] if ta else a0.shape[1]
    N = b0.shape[0] if tb else b0.shape[1]
    for a, b in pairs:
        assert (a.shape == ((K, M) if ta else (M, K))), (a.shape, M, K)
        assert (b.shape == ((N, K) if tb else (K, N))), (b.shape, K, N)
    tm = tm or _pick(M, (1024, 1408, 512, 384, 256, 128))
    tk = tk or (K if K <= 1024 else _pick(K, (1024, 1408, 512, 256, 128)))
    if tn is None:
        n_epi_out = 1 if epi is None else len(epi_out_dtypes)
        for tn in (1024, 1408, 512, 384, 256, 128, N):
            if N % tn:
                continue
            need = sum(2 * tk * (tm * a.dtype.itemsize + tn * b.dtype.itemsize) for a, b in pairs)
            need += tm * tn * 4 * (1 + 2 * n_epi_out + 2 * len(epi_rows))
            if need <= MM_VMEM_BUDGET:
                break
    nk = K // tk
    assert M % tm == 0 and N % tn == 0 and K % tk == 0, (M, N, K, tm, tn, tk)
    n_rows, n_pars = len(epi_rows), len(epi_pars)
    if epi is None:
        out_dtypes = (out_dtype,)
    else:
        out_dtypes = tuple(epi_out_dtypes)
    n_out = len(out_dtypes)
    dn = (((0 if ta else 1,), (1 if tb else 0,)), ((), ()))

    def body(*refs):
        ab = refs[:2 * npair]
        rows = refs[2 * npair:2 * npair + n_rows]
        pars = refs[2 * npair + n_rows:2 * npair + n_rows + n_pars]
        outs = refs[2 * npair + n_rows + n_pars:2 * npair + n_rows + n_pars + n_out]
        acc_ref = refs[-1]
        k = pl.program_id(2)

        @pl.when(k == 0)
        def _():
            acc_ref[...] = jnp.zeros_like(acc_ref)

        acc = acc_ref[...]
        for p in range(npair):
            a = ab[2 * p][...].astype(BF16)
            b = ab[2 * p + 1][...].astype(BF16)
            acc = acc + lax.dot_general(a, b, dn, preferred_element_type=F32)
        acc_ref[...] = acc

        @pl.when(k == nk - 1)
        def _():
            r = acc_ref[...]
            if epi is None:
                outs[0][...] = r.astype(outs[0].dtype)
            else:
                res = epi(r, *[x[...] for x in rows], *[x[...] for x in pars])
                for o, v in zip(outs, res):
                    o[...] = v.astype(o.dtype)

    a_spec = pl.BlockSpec((tk, tm), lambda i, j, k: (k, i)) if ta else pl.BlockSpec((tm, tk), lambda i, j, k: (i, k))
    b_spec = pl.BlockSpec((tn, tk), lambda i, j, k: (j, k)) if tb else pl.BlockSpec((tk, tn), lambda i, j, k: (k, j))
    in_specs = []
    args = []
    for a, b in pairs:
        in_specs += [a_spec, b_spec]
        args += [a, b]
    for r in epi_rows:
        in_specs.append(pl.BlockSpec((tm, tn), lambda i, j, k: (i, j)))
        args.append(r)
    for p_ in epi_pars:
        in_specs.append(pl.BlockSpec((1, tn), lambda i, j, k: (0, j)))
        args.append(p_)
    out_specs = [pl.BlockSpec((tm, tn), lambda i, j, k: (i, j)) for _ in range(n_out)]
    out_shape = [jax.ShapeDtypeStruct((M, N), d) for d in out_dtypes]
    res = pl.pallas_call(
        body, grid=(M // tm, N // tn, nk), in_specs=in_specs, out_specs=out_specs, out_shape=out_shape,
        scratch_shapes=[pltpu.VMEM((tm, tn), F32)], compiler_params=_cp(("parallel", "parallel", "arbitrary")), name=name,
    )(*args)
    return res[0] if epi is None else tuple(res)


def rowmap(fn, rows, pars, out_dtypes, *, name, tile=512, n_reduce=0):
    views = []
    for r in rows:
        if isinstance(r, tuple):
            views.append(r)
        else:
            views.append((r, r.shape[1], 0))
    S = views[0][0].shape[0]
    tile = min(tile, S)
    assert S % tile == 0
    nt = S // tile
    row_structs = [jax.ShapeDtypeStruct((tile, w), a.dtype) for a, w, _ in views]
    par_structs = [jax.ShapeDtypeStruct(p.shape, p.dtype) for p in pars]
    out_structs = jax.eval_shape(fn, *row_structs, *par_structs)
    n_out = len(out_structs)
    n_row_out = n_out - n_reduce
    nr, npar = len(views), len(pars)

    def body(*refs):
        ins = [x[...] for x in refs[:nr + npar]]
        outs = refs[nr + npar:]
        res = fn(*ins)
        for o, v in zip(outs[:n_row_out], res[:n_row_out]):
            o[...] = v.astype(o.dtype)
        if n_reduce:
            i = pl.program_id(0)

            @pl.when(i == 0)
            def _():
                for o, v in zip(outs[n_row_out:], res[n_row_out:]):
                    o[...] = v.astype(o.dtype)

            @pl.when(i > 0)
            def _():
                for o, v in zip(outs[n_row_out:], res[n_row_out:]):
                    o[...] += v.astype(o.dtype)

    in_specs = [pl.BlockSpec((tile, w), functools.partial(lambda i, c: (i, c), c=c)) for _, w, c in views]
    in_specs += [pl.BlockSpec(p.shape, lambda i: (0, 0)) for p in pars]
    out_specs = [pl.BlockSpec((tile, s.shape[1]), lambda i: (i, 0)) for s in out_structs[:n_row_out]]
    out_specs += [pl.BlockSpec(s.shape, lambda i: (0, 0)) for s in out_structs[n_row_out:]]
    out_shape = [jax.ShapeDtypeStruct((S, s.shape[1]), d) for s, d in zip(out_structs[:n_row_out], out_dtypes[:n_row_out])]
    out_shape += [jax.ShapeDtypeStruct(s.shape, F32) for s in out_structs[n_row_out:]]
    res = pl.pallas_call(
        body, grid=(nt,), in_specs=in_specs, out_specs=out_specs, out_shape=out_shape,
        compiler_params=_cp(("arbitrary",) if n_reduce else ("parallel",)), name=name,
    )(*[v[0] for v in views], *pars)
    return tuple(res)


CH = 64


def _softplus(x):
    return jnp.where(x > 20.0, x, jnp.log(1.0 + jnp.exp(jnp.minimum(x, 20.0))))


def _dot(a, b, dn=(((1,), (0,)), ((), ())), hi=False):
    if hi:
        return lax.dot_general(a.astype(F32), b.astype(F32), dn, precision=HI if hi is True else hi, preferred_element_type=F32)
    return lax.dot_general(a.astype(BF16), b.astype(BF16), dn, preferred_element_type=F32)


_NT = (((1,), (1,)), ((), ()))
_TN = (((0,), (0,)), ((), ()))


def _chunk_consts():
    r = lax.broadcasted_iota(jnp.int32, (CH, 2 * CH), 0)
    c0 = lax.broadcasted_iota(jnp.int32, (CH, 2 * CH), 1)
    c = jnp.where(c0 >= CH, c0 - CH, c0)
    r1 = lax.broadcasted_iota(jnp.int32, (CH, CH), 0)
    c1 = lax.broadcasted_iota(jnp.int32, (CH, CH), 1)
    return dict(
        lower2=r >= c, strict2=r > c, U2=(r <= c).astype(F32), eye2=(r == c).astype(F32),
        L=(r1 >= c1).astype(F32), ones=jnp.ones((CH, CH), F32), Z=jnp.zeros((CH, 2 * CH), F32))


@jax.custom_vjp
def _tri_inv2(a2s, eye2, Z):
    def prod(x2, y):
        return _dot(x2, jnp.concatenate([y, Z], axis=0), hi=HI3)

    bs = [-a2 for a2 in a2s]
    ts = [eye2 + b for b in bs]
    for _ in range(5):
        bs = [prod(b, b) for b in bs]
        ts = [t + prod(t, b) for t, b in zip(ts, bs)]
    return tuple(ts)


def _tri_inv2_fwd(a2s, eye2, Z):
    ts = _tri_inv2(a2s, eye2, Z)
    return ts, (ts, eye2, Z)


def _tri_inv2_bwd(res, dts):
    ts, eye2, Z = res
    xs = [_dot(t2, dt2, _TN, hi=HI3)[:CH] for t2, dt2 in zip(ts, dts)]
    das = tuple(-_dot(x2, jnp.concatenate([t2, Z], axis=0), _NT, hi=HI3) for x2, t2 in zip(xs, ts))
    return das, jnp.zeros_like(eye2), jnp.zeros_like(Z)


_tri_inv2.defvjp(_tri_inv2_fwd, _tri_inv2_bwd)


def _gdn_heads(qs, ks, vs, bxs, axs, Ss, alogs, dtbs, cst):
    lower2, strict2, U2, eye2, L, ones, Z = (cst[n] for n in ("lower2", "strict2", "U2", "eye2", "L", "ones", "Z"))
    H = range(len(qs))

    def prod(x2, y):
        return _dot(x2, jnp.concatenate([y, Z], axis=0), hi=HI3)

    qn = [qs[h] * lax.rsqrt(jnp.sum(qs[h] * qs[h], axis=-1, keepdims=True) + NORM_EPS) * (128.0 ** -0.5) for h in H]
    kn = [ks[h] * lax.rsqrt(jnp.sum(ks[h] * ks[h], axis=-1, keepdims=True) + NORM_EPS) for h in H]
    beta = [jax.nn.sigmoid(bxs[h]) for h in H]
    g = [-jnp.exp(alogs[h]) * _softplus(axs[h] + dtbs[h]) for h in H]
    gc = [_dot(L, g[h], hi=HI3) for h in H]
    n2 = [_dot(ones, g[h] * U2, hi=HI3) for h in H]
    decay2 = [jnp.where(lower2, jnp.exp(jnp.where(lower2, gc[h] - n2[h], 0.0)), 0.0) for h in H]
    kb = [kn[h] * beta[h] for h in H]
    kn2 = [jnp.concatenate([kn[h], kn[h]], axis=0) for h in H]
    a2 = tuple(jnp.where(strict2, _dot(kb[h], kn2[h], _NT) * decay2[h], 0.0) for h in H)
    t2 = _tri_inv2(a2, eye2, Z)
    glast = [jnp.sum(g[h], axis=0, keepdims=True) for h in H]
    u = [prod(t2[h], vs[h] * beta[h]) for h in H]
    w = [prod(t2[h], kb[h] * jnp.exp(gc[h])) for h in H]
    attn2 = [jnp.where(lower2, _dot(qn[h], kn2[h], _NT) * decay2[h], 0.0) for h in H]
    k_end = [kn[h] * jnp.exp(glast[h] - gc[h]) for h in H]
    q_start = [qn[h] * jnp.exp(gc[h]) for h in H]
    v_new = [u[h] - _dot(w[h], Ss[h]) for h in H]
    o = [_dot(q_start[h], Ss[h]) + _dot(attn2[h], jnp.concatenate([v_new[h], Z], axis=0)) for h in H]
    s_new = [Ss[h] * jnp.exp(glast[h]) + _dot(k_end[h], v_new[h], _TN) for h in H]
    return tuple(o), tuple(s_new)


def gdn_fwd(qkv, proj, bcol, acol, alog_e, dtb_e, *, name):
    S_ = qkv.shape[0]
    nc = S_ // CH

    def body(q_ref, k_ref, v_ref, b_ref, a_ref, al_ref, dt_ref, o_ref, hist_ref, s_ref):
        i = pl.program_id(0)

        @pl.when(i == 0)
        def _():
            s_ref[...] = jnp.zeros_like(s_ref)

        cst = _chunk_consts()
        hist_ref[0] = s_ref[...]
        heads = [slice(128 * h, 128 * (h + 1)) for h in range(4)]
        rd = lambda ref: tuple(ref[:, ls] for ls in heads)
        os_, s_news = _gdn_heads(rd(q_ref), rd(k_ref), rd(v_ref), rd(b_ref), rd(a_ref), tuple(s_ref[ls, :] for ls in heads),
                                 rd(al_ref), rd(dt_ref), cst)
        for ls, o, s_new in zip(heads, os_, s_news):
            o_ref[:, ls] = o
            s_ref[ls, :] = s_new

    blk = lambda cb: pl.BlockSpec((CH, 512), functools.partial(lambda i, cb: (i, cb), cb=cb))
    par = pl.BlockSpec((1, 512), lambda i: (0, 0))
    return pl.pallas_call(
        body, grid=(nc,), in_specs=[blk(0), blk(1), blk(2), blk(bcol), blk(acol), par, par],
        out_specs=[pl.BlockSpec((CH, 512), lambda i: (i, 0)), pl.BlockSpec((1, 512, 128), lambda i: (i, 0, 0))],
        out_shape=[jax.ShapeDtypeStruct((S_, 512), F32), jax.ShapeDtypeStruct((nc, 512, 128), F32)],
        scratch_shapes=[pltpu.VMEM((512, 128), F32)], compiler_params=_cp(("arbitrary",)), name=name,
    )(qkv, qkv, qkv, proj, proj, alog_e, dtb_e)


def gdn_bwd(qkv, proj, bcol, acol, alog_e, dtb_e, hist, do, *, name):
    S_ = qkv.shape[0]
    nc = S_ // CH

    def body(q_ref, k_ref, v_ref, b_ref, a_ref, al_ref, dt_ref, hist_ref, do_ref, dqkv_ref, db_ref, da_ref, dal_ref, ddt_ref, ds_ref):
        i = pl.program_id(0)

        @pl.when(i == 0)
        def _():
            ds_ref[...] = jnp.zeros_like(ds_ref)
            dal_ref[...] = jnp.zeros_like(dal_ref)
            ddt_ref[...] = jnp.zeros_like(ddt_ref)

        cst = _chunk_consts()
        heads = [slice(128 * h, 128 * (h + 1)) for h in range(4)]
        rd = lambda ref: tuple(ref[:, ls] for ls in heads)
        fn = functools.partial(_gdn_heads, cst=cst)
        _, vjp = jax.vjp(fn, rd(q_ref), rd(k_ref), rd(v_ref), rd(b_ref), rd(a_ref), tuple(hist_ref[0, ls, :] for ls in heads),
                         rd(al_ref), rd(dt_ref))
        grads = vjp((rd(do_ref), tuple(ds_ref[ls, :] for ls in heads)))
        for h in range(4):
            ls = heads[h]
            dq, dk, dv, db, da, ds_in, dal, ddt = (t[h] for t in grads)
            dqkv_ref[:, 128 * h:128 * (h + 1)] = dq
            dqkv_ref[:, 512 + 128 * h:512 + 128 * (h + 1)] = dk
            dqkv_ref[:, 1024 + 128 * h:1024 + 128 * (h + 1)] = dv
            db_ref[:, ls] = db
            da_ref[:, ls] = da
            ds_ref[ls, :] = ds_in
            dal_ref[:, ls] += dal
            ddt_ref[:, ls] += ddt

    rblk = lambda cb: pl.BlockSpec((CH, 512), functools.partial(lambda i, cb: (nc - 1 - i, cb), cb=cb))
    par = pl.BlockSpec((1, 512), lambda i: (0, 0))
    return pl.pallas_call(
        body, grid=(nc,),
        in_specs=[rblk(0), rblk(1), rblk(2), rblk(bcol), rblk(acol), par, par,
                  pl.BlockSpec((1, 512, 128), lambda i: (nc - 1 - i, 0, 0)), rblk(0)],
        out_specs=[pl.BlockSpec((CH, 1536), lambda i: (nc - 1 - i, 0)), rblk(0), rblk(0), par, par],
        out_shape=[jax.ShapeDtypeStruct((S_, 1536), F32), jax.ShapeDtypeStruct((S_, 512), F32), jax.ShapeDtypeStruct((S_, 512), F32),
                   jax.ShapeDtypeStruct((1, 512), F32), jax.ShapeDtypeStruct((1, 512), F32)],
        scratch_shapes=[pltpu.VMEM((512, 128), F32)], compiler_params=_cp(("arbitrary",)), name=name,
    )(qkv, qkv, qkv, proj, proj, alog_e, dtb_e, hist, do)


def _ssd_pairs(xs, dtxs, bms, cms, hss, nas, dtbs, dsks, cst):
    lower2, U2, L, ones = cst["lower2"], cst["U2"], cst["L"], cst["ones"]
    lane = lax.broadcasted_iota(jnp.int32, (1, 2 * CH), 1)
    mask_l = (lane < CH).astype(F32)
    mask_r = 1.0 - mask_l
    ones_w = jnp.ones((CH, 2 * CH), F32)
    P_ = range(len(xs))
    G_ = range(len(bms))
    per = len(xs) // len(bms)
    cb2 = [_dot(cms[g], jnp.concatenate([bms[g], bms[g]], axis=0), _NT) for g in G_]
    dt = [_softplus(dtxs[p] + dtbs[p]) for p in P_]
    da = [dt[p] * nas[p] for p in P_]
    m = [_dot(L, da[p], hi=HI3) for p in P_]
    n2 = [_dot(ones, da[p] * U2, hi=HI3) for p in P_]
    lm2 = [jnp.where(lower2, jnp.exp(jnp.where(lower2, m[p] - n2[p], 0.0)), 0.0) for p in P_]
    xdt = [xs[p] * dt[p] for p in P_]
    x2 = [jnp.concatenate([xdt[p] * mask_l, xdt[p] * mask_r], axis=0) for p in P_]
    y_diag = [_dot(cb2[p // per] * lm2[p], x2[p]) for p in P_]
    alast = [jnp.sum(da[p], axis=0, keepdims=True) for p in P_]
    y_off = [_dot(cms[p // per], hss[p], _NT) * jnp.exp(m[p]) for p in P_]
    cd = [jnp.exp(_dot(da[p], ones_w, _TN, hi=HI3)) for p in P_]
    hs_new = [hss[p] * cd[p] + _dot(xdt[p] * jnp.exp(alast[p] - m[p]), bms[p // per], _TN) for p in P_]
    ys = [y_diag[p] + y_off[p] + dsks[p] * xs[p] for p in P_]
    return tuple(ys), tuple(hs_new)


def _ssd_specs(nc, rev):
    ci = (lambda i: nc - 1 - i) if rev else (lambda i: i)
    col = lambda w, c: pl.BlockSpec((CH, w), functools.partial(lambda i, c: (ci(i), c), c=c))
    xg = [col(512, g) for g in range(4)]
    dtg = [col(512, 10 + g) for g in range(4)]
    par = pl.BlockSpec((1, 2048), lambda i: (0, 0))
    hist = pl.BlockSpec((1, 2048, 128), lambda i: (ci(i), 0, 0))
    return xg, dtg, col(512, 4), col(512, 5), par, hist, col


def _ssd_read(x_refs, dt_refs, b_ref, c_ref, na_ref, dtb_ref, dsk_ref):
    sl = [slice(128 * p, 128 * (p + 1)) for p in range(4)]
    xs = tuple(x_refs[g][:, s] for g in range(4) for s in sl)
    dts = tuple(dt_refs[g][:, s] for g in range(4) for s in sl)
    bms = tuple(b_ref[:, s] for s in sl)
    cms = tuple(c_ref[:, s] for s in sl)
    lanes = [slice(128 * p, 128 * (p + 1)) for p in range(16)]
    pars = [tuple(r[:, s] for s in lanes) for r in (na_ref, dtb_ref, dsk_ref)]
    return xs, dts, bms, cms, pars, lanes


def ssd_fwd(xbc, proj, na_e, dtb_e, dsk_e, *, name):
    S_ = xbc.shape[0]
    nc = S_ // CH
    xg, dtg, bs, cs, par, hist, _ = _ssd_specs(nc, False)

    def body(*refs):
        x_refs, dt_refs = refs[0:4], refs[4:8]
        b_ref, c_ref, na_ref, dtb_ref, dsk_ref, y_ref, hist_ref, s_ref = refs[8:]
        i = pl.program_id(0)

        @pl.when(i == 0)
        def _():
            s_ref[...] = jnp.zeros_like(s_ref)

        cst = _chunk_consts()
        hist_ref[0] = s_ref[...]
        xs, dts, bms, cms, pars, lanes = _ssd_read(x_refs, dt_refs, b_ref, c_ref, na_ref, dtb_ref, dsk_ref)
        ys, hs_new = _ssd_pairs(xs, dts, bms, cms, tuple(s_ref[s, :] for s in lanes), *pars, cst)
        for p, s in enumerate(lanes):
            y_ref[:, s] = ys[p]
            s_ref[s, :] = hs_new[p]

    return pl.pallas_call(
        body, grid=(nc,), in_specs=xg + dtg + [bs, cs, par, par, par],
        out_specs=[pl.BlockSpec((CH, 2048), lambda i: (i, 0)), hist],
        out_shape=[jax.ShapeDtypeStruct((S_, 2048), F32), jax.ShapeDtypeStruct((nc, 2048, 128), F32)],
        scratch_shapes=[pltpu.VMEM((2048, 128), F32)], compiler_params=_cp(("arbitrary",)), name=name,
    )(xbc, xbc, xbc, xbc, proj, proj, proj, proj, xbc, xbc, na_e, dtb_e, dsk_e)


def ssd_bwd(xbc, proj, na_e, dtb_e, dsk_e, hist, dy, *, name):
    S_ = xbc.shape[0]
    nc = S_ // CH
    xg, dtg, bs, cs, par, hist_spec, col = _ssd_specs(nc, True)
    wide = pl.BlockSpec((CH, 2048), lambda i: (nc - 1 - i, 0))

    def body(*refs):
        x_refs, dt_refs = refs[0:4], refs[4:8]
        (b_ref, c_ref, na_ref, dtb_ref, dsk_ref, hist_ref, dy_ref,
         dx_ref, ddt_ref, db_ref, dc_ref, dna_ref, ddtb_ref, ddsk_ref, ds_ref) = refs[8:]
        i = pl.program_id(0)

        @pl.when(i == 0)
        def _():
            ds_ref[...] = jnp.zeros_like(ds_ref)
            dna_ref[...] = jnp.zeros_like(dna_ref)
            ddtb_ref[...] = jnp.zeros_like(ddtb_ref)
            ddsk_ref[...] = jnp.zeros_like(ddsk_ref)

        cst = _chunk_consts()
        xs, dts, bms, cms, pars, lanes = _ssd_read(x_refs, dt_refs, b_ref, c_ref, na_ref, dtb_ref, dsk_ref)
        fn = functools.partial(_ssd_pairs, cst=cst)
        _, vjp = jax.vjp(fn, xs, dts, bms, cms, tuple(hist_ref[0, s, :] for s in lanes), *pars)
        dxs, ddts, dbs, dcs, dhs, dnas, ddtbs, ddsks = vjp((tuple(dy_ref[:, s] for s in lanes), tuple(ds_ref[s, :] for s in lanes)))
        for g in range(4):
            db_ref[:, 128 * g:128 * (g + 1)] = dbs[g]
            dc_ref[:, 128 * g:128 * (g + 1)] = dcs[g]
        for p, s in enumerate(lanes):
            dx_ref[:, s] = dxs[p]
            ddt_ref[:, s] = ddts[p]
            ds_ref[s, :] = dhs[p]
            dna_ref[:, s] += dnas[p]
            ddtb_ref[:, s] += ddtbs[p]
            ddsk_ref[:, s] += ddsks[p]

    half = pl.BlockSpec((CH, 512), lambda i: (nc - 1 - i, 0))
    return pl.pallas_call(
        body, grid=(nc,), in_specs=xg + dtg + [bs, cs, par, par, par, hist_spec, wide],
        out_specs=[wide, wide, half, half, par, par, par],
        out_shape=[jax.ShapeDtypeStruct((S_, 2048), F32), jax.ShapeDtypeStruct((S_, 2048), F32),
                   jax.ShapeDtypeStruct((S_, 512), F32), jax.ShapeDtypeStruct((S_, 512), F32)] +
                  [jax.ShapeDtypeStruct((1, 2048), F32)] * 3,
        scratch_shapes=[pltpu.VMEM((2048, 128), F32)], compiler_params=_cp(("arbitrary",)), name=name,
    )(xbc, xbc, xbc, xbc, proj, proj, proj, proj, xbc, xbc, na_e, dtb_e, dsk_e, hist, dy)


ATT_T = 1024
ATT_SCALE = 192.0 ** -0.5
NEG = -1e30


def _chunk_mask(shape):
    return lax.broadcasted_iota(jnp.int32, shape, 1) // CH <= lax.broadcasted_iota(jnp.int32, shape, 0) // CH


def _tri_pairs(n, by_row):
    pairs = [(i, j) for i in range(n) for j in range(i + 1)] if by_row else [(i, j) for j in range(n) for i in range(j, n)]
    return jnp.asarray([p[0] for p in pairs], jnp.int32), jnp.asarray([p[1] for p in pairs], jnp.int32)


def att_fwd(q, kv, kp, *, name):
    S_ = q.shape[0]
    T = min(ATT_T, S_)
    n = S_ // T
    ii, jj = _tri_pairs(n, True)

    def body(ii_ref, jj_ref, q_ref, kn_ref, kp_ref, v_ref, o_ref, lse_ref, m_ref, l_ref, acc_ref):
        t = pl.program_id(1)
        i, j = ii_ref[t], jj_ref[t]

        @pl.when(j == 0)
        def _():
            m_ref[...] = jnp.full_like(m_ref, NEG)
            l_ref[...] = jnp.zeros_like(l_ref)
            acc_ref[...] = jnp.zeros_like(acc_ref)

        def step(diag):
            k2 = jnp.concatenate([kn_ref[...], kp_ref[...]], axis=1)
            s = _dot(q_ref[...], k2, _NT)
            if diag:
                s = jnp.where(_chunk_mask(s.shape), s, NEG)
            m_prev = m_ref[...]
            m_cur = jnp.maximum(m_prev, jnp.max(s, axis=-1, keepdims=True))
            p = jnp.exp(s - m_cur[:, :1])
            alpha = jnp.exp(m_prev - m_cur)
            l_ref[...] = alpha * l_ref[...] + jnp.sum(p, axis=-1, keepdims=True)
            acc_ref[...] = acc_ref[...] * alpha + _dot(p, v_ref[...])
            m_ref[...] = m_cur

        @pl.when(j < i)
        def _():
            step(False)

        @pl.when(j == i)
        def _():
            step(True)
            o_ref[...] = acc_ref[...] / l_ref[...]
            lse_ref[...] = m_ref[...] + jnp.log(l_ref[...])

    grid_spec = pltpu.PrefetchScalarGridSpec(
        num_scalar_prefetch=2, grid=(4, ii.shape[0]),
        in_specs=[pl.BlockSpec((T, 256), lambda h, t, ii_, jj_: (ii_[t], h)), pl.BlockSpec((T, 128), lambda h, t, ii_, jj_: (jj_[t], h)),
                  pl.BlockSpec((T, 128), lambda h, t, ii_, jj_: (jj_[t], 0)),
                  pl.BlockSpec((T, 128), lambda h, t, ii_, jj_: (jj_[t], 4 + h))],
        out_specs=[pl.BlockSpec((T, 128), lambda h, t, ii_, jj_: (ii_[t], h))] * 2,
        scratch_shapes=[pltpu.VMEM((T, 128), F32)] * 3)
    return pl.pallas_call(
        body, grid_spec=grid_spec, out_shape=[jax.ShapeDtypeStruct((S_, 512), F32), jax.ShapeDtypeStruct((S_, 512), F32)],
        compiler_params=_cp(("parallel", "arbitrary")), name=name,
    )(ii, jj, q, kv, kp, kv)


def att_bwd(q, kv, kp, lse, dsum, do, *, name):
    S_ = q.shape[0]
    T = min(ATT_T, S_)
    n = S_ // T
    ii, jj = _tri_pairs(n, False)
    last = ii.shape[0] - 1

    def body(ii_ref, jj_ref, q_ref, kn_ref, kp_ref, v_ref, lse_ref, d_ref, do_ref, dq_hbm, dk_ref, dv_ref, dq_acc, sem):
        h, t = pl.program_id(0), pl.program_id(1)
        i, j = ii_ref[t], jj_ref[t]

        @pl.when(t == 0)
        def _():
            dq_acc[...] = jnp.zeros_like(dq_acc)

        def step(diag):
            k2 = jnp.concatenate([kn_ref[...], kp_ref[...]], axis=1)
            qb = q_ref[...]
            dob = do_ref[...].astype(BF16)
            s = _dot(qb, k2, _NT)
            p = jnp.exp(s - lse_ref[:, :1])
            if diag:
                p = jnp.where(_chunk_mask(s.shape), p, 0.0)
            ds = (p * (_dot(dob, v_ref[...], _NT) - d_ref[:, :1])).astype(BF16)
            rows = pl.ds(pl.multiple_of(i * T, T), T)
            dq_acc[rows, :] += _dot(ds, k2)
            if diag:
                dv_ref[...] = _dot(p, dob, _TN)
                dk_ref[...] = _dot(ds, qb, _TN)
            else:
                dv_ref[...] += _dot(p, dob, _TN)
                dk_ref[...] += _dot(ds, qb, _TN)

        @pl.when(i > j)
        def _():
            step(False)

        @pl.when(i == j)
        def _():
            step(True)

        @pl.when(t == last)
        def _():
            cp = pltpu.make_async_copy(dq_acc, dq_hbm.at[h], sem)
            cp.start()
            cp.wait()

    qmap = lambda h, t, ii_, jj_: (ii_[t], h)
    grid_spec = pltpu.PrefetchScalarGridSpec(
        num_scalar_prefetch=2, grid=(4, ii.shape[0]),
        in_specs=[pl.BlockSpec((T, 256), qmap), pl.BlockSpec((T, 128), lambda h, t, ii_, jj_: (jj_[t], h)),
                  pl.BlockSpec((T, 128), lambda h, t, ii_, jj_: (jj_[t], 0)), pl.BlockSpec((T, 128), lambda h, t, ii_, jj_: (jj_[t], 4 + h)),
                  pl.BlockSpec((T, 128), qmap), pl.BlockSpec((T, 128), qmap), pl.BlockSpec((T, 128), qmap)],
        out_specs=[pl.BlockSpec(memory_space=pl.ANY), pl.BlockSpec((T, 256), lambda h, t, ii_, jj_: (jj_[t], h)),
                   pl.BlockSpec((T, 128), lambda h, t, ii_, jj_: (jj_[t], h))],
        scratch_shapes=[pltpu.VMEM((S_, 256), F32), pltpu.SemaphoreType.DMA])
    return pl.pallas_call(
        body, grid_spec=grid_spec,
        out_shape=[jax.ShapeDtypeStruct((4, S_, 256), F32), jax.ShapeDtypeStruct((S_, 1024), F32), jax.ShapeDtypeStruct((S_, 512), F32)],
        compiler_params=_cp(("arbitrary", "arbitrary")), name=name,
    )(ii, jj, q, kv, kp, kv, lse, dsum, do)


CONV_T = 512


def _shift_down(x, halo, s):
    sh = pltpu.roll(x, s, axis=0)
    hr = pltpu.roll(halo, s, axis=0)
    r8 = lax.broadcasted_iota(jnp.int32, hr.shape, 0)
    top = jnp.where(r8 < s, hr, sh[:8])
    return jnp.concatenate([top, sh[8:]], axis=0)


def _shift_up(x, halo, s):
    n = x.shape[0]
    sh = pltpu.roll(x, n - s, axis=0)
    hr = pltpu.roll(halo, 8 - s, axis=0)
    r8 = lax.broadcasted_iota(jnp.int32, hr.shape, 0)
    bot = jnp.where(r8 >= 8 - s, hr, sh[n - 8:])
    return jnp.concatenate([sh[:n - 8], bot], axis=0)


def _conv_pre(x, halo, w, b):
    y = x * w[3:4] + b
    for j in range(3):
        y = y + _shift_down(x, halo, 3 - j) * w[j:j + 1]
    return y


def conv_fwd(src, cb0, ncb, w, b, *, name):
    S_ = src.shape[0]
    T = min(CONV_T, S_)
    nt = S_ // T

    def body(x_ref, h_ref, w_ref, b_ref, o_ref):
        i = pl.program_id(1)
        halo = jnp.where(i > 0, h_ref[...], 0.0)
        y = _conv_pre(x_ref[...], halo, w_ref[...], b_ref[...])
        o_ref[...] = y * jax.nn.sigmoid(y)

    return pl.pallas_call(
        body, grid=(ncb, nt),
        in_specs=[pl.BlockSpec((T, 512), lambda c, i: (i, cb0 + c)),
                  pl.BlockSpec((8, 512), lambda c, i: (jnp.maximum(i * (T // 8) - 1, 0), cb0 + c)),
                  pl.BlockSpec((4, 512), lambda c, i: (0, c)), pl.BlockSpec((1, 512), lambda c, i: (0, c))],
        out_specs=pl.BlockSpec((T, 512), lambda c, i: (i, c)),
        out_shape=jax.ShapeDtypeStruct((S_, 512 * ncb), F32), compiler_params=_cp(("parallel", "parallel")), name=name,
    )(src, src, w, b)


def conv_bwd_pre(src, cb0, ncb, w, b, dy, *, name):
    S_ = src.shape[0]
    T = min(CONV_T, S_)
    nt = S_ // T

    def body(x_ref, h_ref, w_ref, b_ref, dy_ref, dp_ref, dw_ref, db_ref):
        i = pl.program_id(1)
        halo = jnp.where(i > 0, h_ref[...], 0.0)
        x = x_ref[...]
        y = _conv_pre(x, halo, w_ref[...], b_ref[...])
        sg = jax.nn.sigmoid(y)
        dpre = dy_ref[...] * (sg * (1.0 + y * (1.0 - sg)))
        dp_ref[...] = dpre
        rows = [jnp.sum(dpre * _shift_down(x, halo, 3 - j), axis=0, keepdims=True) for j in range(3)]
        rows.append(jnp.sum(dpre * x, axis=0, keepdims=True))
        dw = jnp.concatenate(rows, axis=0)
        db = jnp.sum(dpre, axis=0, keepdims=True)

        @pl.when(i == 0)
        def _():
            dw_ref[...] = dw
            db_ref[...] = db

        @pl.when(i > 0)
        def _():
            dw_ref[...] += dw
            db_ref[...] += db

    return pl.pallas_call(
        body, grid=(ncb, nt),
        in_specs=[pl.BlockSpec((T, 512), lambda c, i: (i, cb0 + c)),
                  pl.BlockSpec((8, 512), lambda c, i: (jnp.maximum(i * (T // 8) - 1, 0), cb0 + c)),
                  pl.BlockSpec((4, 512), lambda c, i: (0, c)), pl.BlockSpec((1, 512), lambda c, i: (0, c)),
                  pl.BlockSpec((T, 512), lambda c, i: (i, c))],
        out_specs=[pl.BlockSpec((T, 512), lambda c, i: (i, c)), pl.BlockSpec((4, 512), lambda c, i: (0, c)),
                   pl.BlockSpec((1, 512), lambda c, i: (0, c))],
        out_shape=[jax.ShapeDtypeStruct((S_, 512 * ncb), F32), jax.ShapeDtypeStruct((4, 512 * ncb), F32),
                   jax.ShapeDtypeStruct((1, 512 * ncb), F32)],
        compiler_params=_cp(("parallel", "arbitrary")), name=name,
    )(src, src, w, b, dy)


def conv_bwd_x(dpre, w, *, name, out_dtype=F32):
    S_, C = dpre.shape
    T = min(CONV_T, S_)
    nt = S_ // T
    ncb = C // 512

    def body(d_ref, h_ref, w_ref, o_ref):
        i = pl.program_id(1)
        halo = jnp.where(i < nt - 1, h_ref[...], 0.0)
        d = d_ref[...]
        w_ = w_ref[...]
        y = d * w_[3:4]
        for j in range(3):
            y = y + _shift_up(d, halo, 3 - j) * w_[j:j + 1]
        o_ref[...] = y.astype(o_ref.dtype)

    return pl.pallas_call(
        body, grid=(ncb, nt),
        in_specs=[pl.BlockSpec((T, 512), lambda c, i: (i, c)),
                  pl.BlockSpec((8, 512), lambda c, i: (jnp.minimum((i + 1) * (T // 8), S_ // 8 - 1), c)),
                  pl.BlockSpec((4, 512), lambda c, i: (0, c))],
        out_specs=pl.BlockSpec((T, 512), lambda c, i: (i, c)),
        out_shape=jax.ShapeDtypeStruct((S_, C), out_dtype), compiler_params=_cp(("parallel", "parallel")), name=name,
    )(dpre, dpre, w)


def ffn_mid_fwd(h, w1, w3, *, name):
    S_, D = h.shape
    F = w1.shape[1]
    tm, tn = _pick(S_, (2048, 1024, 512, 256)), 256

    def body(h_ref, w1_ref, w3_ref, a_ref):
        hb = h_ref[...]
        u = _dot(hb, w1_ref[...])
        v = _dot(hb, w3_ref[...])
        a_ref[...] = (u * jax.nn.sigmoid(u) * v).astype(a_ref.dtype)

    return pl.pallas_call(
        body, grid=(S_ // tm, F // tn),
        in_specs=[pl.BlockSpec((tm, D), lambda i, j: (i, 0)), pl.BlockSpec((D, tn), lambda i, j: (0, j)),
                  pl.BlockSpec((D, tn), lambda i, j: (0, j))],
        out_specs=pl.BlockSpec((tm, tn), lambda i, j: (i, j)), out_shape=jax.ShapeDtypeStruct((S_, F), BF16),
        compiler_params=_cp(("parallel", "parallel")), name=name,
    )(h, w1, w3)


def ffn_mid_bwd(h, dy, w1, w3, w2, *, name):
    S_, D = h.shape
    F = w1.shape[1]
    tm, tn = _pick(S_, (2048, 1024, 512, 256)), 256

    def body(h_ref, dy_ref, w1_ref, w3_ref, w2_ref, du_ref, dv_ref, a_ref):
        hb = h_ref[...]
        u = _dot(hb, w1_ref[...])
        v = _dot(hb, w3_ref[...])
        da = _dot(dy_ref[...], w2_ref[...], _NT)
        sg = jax.nn.sigmoid(u)
        si = u * sg
        a_ref[...] = (si * v).astype(a_ref.dtype)
        dv_ref[...] = (da * si).astype(dv_ref.dtype)
        du_ref[...] = (da * v * (sg * (1.0 + u * (1.0 - sg)))).astype(du_ref.dtype)

    o = pl.BlockSpec((tm, tn), lambda i, j: (i, j))
    return pl.pallas_call(
        body, grid=(S_ // tm, F // tn),
        in_specs=[pl.BlockSpec((tm, D), lambda i, j: (i, 0)), pl.BlockSpec((tm, D), lambda i, j: (i, 0)),
                  pl.BlockSpec((D, tn), lambda i, j: (0, j)), pl.BlockSpec((D, tn), lambda i, j: (0, j)),
                  pl.BlockSpec((tn, D), lambda i, j: (j, 0))],
        out_specs=[o, o, o], out_shape=[jax.ShapeDtypeStruct((S_, F), BF16)] * 3,
        compiler_params=_cp(("parallel", "parallel")), name=name,
    )(h, dy, w1, w3, w2)


MESH = pl.DeviceIdType.MESH
ANY = pl.BlockSpec(memory_space=pl.ANY)


def allgather8(x_shard, *, name):
    m_per, n = x_shard.shape

    def body(x_ref, out_ref, send_sems, recv_sems, local_sem):
        x, y, c = lax.axis_index("x"), lax.axis_index("y"), lax.axis_index("c")
        me, sibling = (x, y, c), (x, y, 1 - c)
        chips = [(1 - x, y), (x, 1 - y), (1 - x, 1 - y)]

        def rows(px, py, pc):
            return out_ref.at[pl.ds((4 * px + 2 * py + pc) * m_per, m_per), :]

        def copy(k, block, to, src=None):
            return pltpu.make_async_remote_copy(
                src_ref=rows(*block) if src is None else src, dst_ref=rows(*block),
                send_sem=send_sems.at[k], recv_sem=recv_sems.at[k], device_id=to, device_id_type=MESH)

        mine = pltpu.make_async_copy(x_ref, rows(*me), local_sem)
        mine.start()
        first = [copy(0, me, sibling, src=x_ref)]
        first += [copy(1 + j, me, (*chip, c), src=x_ref) for j, chip in enumerate(chips)]
        for cp in first:
            cp.start()
        passed = [copy(4 + j, (*chip, c), sibling) for j, chip in enumerate(chips)]
        for j, chip in enumerate(chips):
            copy(1 + j, (*chip, c), me).wait_recv()
            passed[j].start()
        copy(0, sibling, me).wait_recv()
        for j, chip in enumerate(chips):
            copy(4 + j, (*chip, 1 - c), me).wait_recv()
        for cp in first + passed:
            cp.wait_send()
        mine.wait()

    return pl.pallas_call(
        body, out_shape=jax.ShapeDtypeStruct((8 * m_per, n), x_shard.dtype),
        in_specs=[pl.BlockSpec(memory_space=pltpu.VMEM)], out_specs=pl.BlockSpec(memory_space=pltpu.VMEM),
        scratch_shapes=[pltpu.SemaphoreType.DMA((7,)), pltpu.SemaphoreType.DMA((7,)), pltpu.SemaphoreType.DMA],
        name=name,
    )(x_shard)


def _chip_peers():
    x, y, c = lax.axis_index("x"), lax.axis_index("y"), lax.axis_index("c")
    return x, y, c, [(1 - x, y), (x, 1 - y), (1 - x, 1 - y)]


def allgather_chips(x_shard, *, name):
    r, cdim = x_shard.shape

    def body(x_ref, out_ref, send_sems, recv_sems, local_sem):
        x, y, c, chips = _chip_peers()
        me = 2 * x + y
        mine = pltpu.make_async_copy(x_ref, out_ref.at[me], local_sem)
        mine.start()
        sends = []
        for k, (px, py) in enumerate(chips):
            cp = pltpu.make_async_remote_copy(src_ref=x_ref, dst_ref=out_ref.at[me], send_sem=send_sems.at[k],
                                              recv_sem=recv_sems.at[k], device_id=(px, py, c), device_id_type=MESH)
            cp.start()
            sends.append(cp)
        for k, (px, py) in enumerate(chips):
            pltpu.make_async_remote_copy(src_ref=x_ref, dst_ref=out_ref.at[2 * px + py], send_sem=send_sems.at[k],
                                         recv_sem=recv_sems.at[k], device_id=(px, py, c), device_id_type=MESH).wait_recv()
        for cp in sends:
            cp.wait_send()
        mine.wait()

    return pl.pallas_call(
        body, out_shape=jax.ShapeDtypeStruct((4, r, cdim), x_shard.dtype), in_specs=[ANY], out_specs=ANY,
        scratch_shapes=[pltpu.SemaphoreType.DMA((3,)), pltpu.SemaphoreType.DMA((3,)), pltpu.SemaphoreType.DMA],
        name=name,
    )(x_shard)


def allgather_chips_2level(x_shard, *, name):
    r, cdim = x_shard.shape
    half = r // 2

    def body(x_ref, out_ref, send_sems, recv_sems, local_sem):
        x, y, c, chips = _chip_peers()
        me = 2 * x + y
        mine_rows = pl.ds(c * half, half)
        other_rows = pl.ds((1 - c) * half, half)
        mine = pltpu.make_async_copy(x_ref, out_ref.at[me], local_sem)
        mine.start()

        def copy(k, slot, rows, to, src=None):
            dst = out_ref.at[slot, rows, :]
            return pltpu.make_async_remote_copy(src_ref=dst if src is None else src, dst_ref=dst, send_sem=send_sems.at[k],
                                                recv_sem=recv_sems.at[k], device_id=to, device_id_type=MESH)

        first = [copy(k, me, mine_rows, (px, py, c), src=x_ref.at[mine_rows, :]) for k, (px, py) in enumerate(chips)]
        for cp in first:
            cp.start()
        passed = [copy(3 + k, 2 * px + py, mine_rows, (x, y, 1 - c)) for k, (px, py) in enumerate(chips)]
        for k, (px, py) in enumerate(chips):
            copy(k, 2 * px + py, mine_rows, (px, py, c)).wait_recv()
            passed[k].start()
        for k, (px, py) in enumerate(chips):
            copy(3 + k, 2 * px + py, other_rows, (x, y, 1 - c)).wait_recv()
        for cp in first + passed:
            cp.wait_send()
        mine.wait()

    return pl.pallas_call(
        body, out_shape=jax.ShapeDtypeStruct((4, r, cdim), x_shard.dtype), in_specs=[ANY], out_specs=ANY,
        scratch_shapes=[pltpu.SemaphoreType.DMA((6,)), pltpu.SemaphoreType.DMA((6,)), pltpu.SemaphoreType.DMA],
        name=name,
    )(x_shard)


def exchange_chips(g, *, name):
    _, r, cdim = g.shape

    def body(g_ref, out_ref, send_sems, recv_sems, local_sem):
        x, y, c, chips = _chip_peers()
        me = 2 * x + y
        mine = pltpu.make_async_copy(g_ref.at[me], out_ref.at[me], local_sem)
        mine.start()
        sends = []
        for k, (px, py) in enumerate(chips):
            cp = pltpu.make_async_remote_copy(src_ref=g_ref.at[2 * px + py], dst_ref=out_ref.at[me], send_sem=send_sems.at[k],
                                              recv_sem=recv_sems.at[k], device_id=(px, py, c), device_id_type=MESH)
            cp.start()
            sends.append(cp)
        for k, (px, py) in enumerate(chips):
            pltpu.make_async_remote_copy(src_ref=g_ref.at[me], dst_ref=out_ref.at[2 * px + py], send_sem=send_sems.at[k],
                                         recv_sem=recv_sems.at[k], device_id=(px, py, c), device_id_type=MESH).wait_recv()
        for cp in sends:
            cp.wait_send()
        mine.wait()

    return pl.pallas_call(
        body, out_shape=jax.ShapeDtypeStruct(g.shape, g.dtype), in_specs=[ANY], out_specs=ANY,
        scratch_shapes=[pltpu.SemaphoreType.DMA((3,)), pltpu.SemaphoreType.DMA((3,)), pltpu.SemaphoreType.DMA],
        name=name,
    )(g)


def swap_sibling(p, *, name):
    def body(p_ref, out_ref, send_sem, recv_sem):
        x, y, c = lax.axis_index("x"), lax.axis_index("y"), lax.axis_index("c")
        cp = pltpu.make_async_remote_copy(src_ref=p_ref, dst_ref=out_ref, send_sem=send_sem, recv_sem=recv_sem,
                                          device_id=(x, y, 1 - c), device_id_type=MESH)
        cp.start()
        cp.wait()

    return pl.pallas_call(
        body, out_shape=jax.ShapeDtypeStruct(p.shape, p.dtype), in_specs=[ANY], out_specs=ANY,
        scratch_shapes=[pltpu.SemaphoreType.DMA, pltpu.SemaphoreType.DMA], name=name,
    )(p)


def swap_other_half(g, *, name):
    n, r, cdim = g.shape
    half = r // 2

    def body(g_ref, out_ref, send_sem, recv_sem):
        x, y, c = lax.axis_index("x"), lax.axis_index("y"), lax.axis_index("c")
        cp = pltpu.make_async_remote_copy(src_ref=g_ref.at[:, pl.ds((1 - c) * half, half), :], dst_ref=out_ref, send_sem=send_sem,
                                          recv_sem=recv_sem, device_id=(x, y, 1 - c), device_id_type=MESH)
        cp.start()
        cp.wait()

    return pl.pallas_call(
        body, out_shape=jax.ShapeDtypeStruct((n, half, cdim), g.dtype), in_specs=[ANY], out_specs=ANY,
        scratch_shapes=[pltpu.SemaphoreType.DMA, pltpu.SemaphoreType.DMA], name=name,
    )(g)


def add_pairs(a, b, *, name, out_dtype):
    n, rows, cdim = a.shape
    t = _pick(rows, (256, 128, 64, 32, 16))

    def body(a_ref, b_ref, o_ref):
        o_ref[...] = (a_ref[...].astype(F32) + b_ref[...].astype(F32)).astype(o_ref.dtype)

    spec = pl.BlockSpec((n, t, cdim), lambda i: (0, i, 0))
    return pl.pallas_call(
        body, grid=(rows // t,), in_specs=[spec, spec], out_specs=spec, out_shape=jax.ShapeDtypeStruct(a.shape, out_dtype),
        compiler_params=_cp(("parallel",)), name=name,
    )(a, b)


def sum_slots(r, *, name):
    n, rows, cdim = r.shape
    t = _pick(rows, (256, 128, 64, 32, 16, 8))

    def body(r_ref, o_ref):
        acc = r_ref[0].astype(F32)
        for s in range(1, n):
            acc = acc + r_ref[s].astype(F32)
        o_ref[...] = acc

    return pl.pallas_call(
        body, grid=(rows // t,), in_specs=[pl.BlockSpec((n, t, cdim), lambda i: (0, i, 0))],
        out_specs=pl.BlockSpec((t, cdim), lambda i: (i, 0)), out_shape=jax.ShapeDtypeStruct((rows, cdim), F32),
        compiler_params=_cp(("parallel",)), name=name,
    )(r)


def _rms(x, g):
    return x * lax.rsqrt(jnp.mean(x * x, axis=-1, keepdims=True) + NORM_EPS) * g


def _adaln(x, g, shift, scale):
    return _rms(x, g) * (1.0 + scale) + shift


def _silu(x):
    return x * jax.nn.sigmoid(x)


def _gdn_gate(o, z, g):
    return jnp.concatenate([_rms(o[:, 128 * h:128 * (h + 1)], g) * _silu(z[:, 128 * h:128 * (h + 1)]) for h in range(4)], axis=1)


def _ssd_gate(y, z0, z1, z2, z3, g):
    outs = []
    for k, z in enumerate((z0, z1, z2, z3)):
        t = y[:, 512 * k:512 * (k + 1)] * _silu(z)
        outs.append(t * lax.rsqrt(jnp.mean(t * t, axis=-1, keepdims=True) + NORM_EPS))
    return jnp.concatenate(outs, axis=1) * g


def _rope(x, cos, sin, rot):
    return x * cos + _dot(x, rot, hi=True) * sin


def _rope_q(q, cos, sin, rot):
    parts = []
    for h in range(4):
        parts += [q[:, 256 * h:256 * h + 128], _rope(q[:, 256 * h + 128:256 * (h + 1)], cos, sin, rot)]
    return jnp.concatenate(parts, axis=1) * ATT_SCALE


def _rope_t(d, cos, sin, rot):
    return d * cos + _dot(d * sin, rot, _NT, hi=True)


def _vjp_rows(fn, n_rows, n_pars, out_dtypes, rows, cts, pars, *, name, tile=512, extra=None):
    nct = len(cts)

    def bwd(*a):
        r, c, e, p = a[:n_rows], a[n_rows:n_rows + nct], a[n_rows + nct:len(a) - n_pars], a[len(a) - n_pars:]
        out, vjp = jax.vjp(fn, *[t.astype(F32) for t in r], *p)
        ct = tuple(t.astype(F32) for t in c)
        grads = vjp(ct[0] if not isinstance(out, tuple) else ct)
        drows = list(grads[:n_rows])
        if e:
            drows[0] = drows[0] + e[0]
        return (*drows, *grads[n_rows:])

    return rowmap(bwd, list(rows) + list(cts) + ([extra] if extra is not None else []), list(pars), out_dtypes,
                  name=name, tile=tile, n_reduce=n_pars)


ADAM_LR, ADAM_B1, ADAM_B2, ADAM_EPS, ADAM_WD, ADAM_STEP = 0.001, 0.9, 0.999, 1e-08, 0.01, 10


def _adam_math(w, g, m, v):
    m = ADAM_B1 * m + (1.0 - ADAM_B1) * g
    v = ADAM_B2 * v + (1.0 - ADAM_B2) * (g * g)
    m_hat = m / (1.0 - ADAM_B1 ** ADAM_STEP)
    v_hat = v / (1.0 - ADAM_B2 ** ADAM_STEP)
    delta = -ADAM_LR * (m_hat / (jnp.sqrt(v_hat) + ADAM_EPS) + ADAM_WD * w)
    return delta, m, v


def adamw(w, gs, m, v, *, name):
    shape = w.shape
    last = shape[-1]
    to2 = lambda a: a.reshape(-1, last)
    rows = w.size // last
    tile = _pick(rows, (256, 128, 64, 32, 16, 8))
    ng = len(gs)

    def fn(w_, *rest):
        g = rest[0]
        for t in rest[1:ng]:
            g = g + t
        m_, v_ = rest[ng], rest[ng + 1]
        return (g, *_adam_math(w_, g, m_, v_))

    outs = rowmap(fn, [to2(w)] + [to2(g) for g in gs] + [to2(m), to2(v)], [], (F32,) * 4, name=name, tile=tile)
    return tuple(o.reshape(shape) for o in outs)


PACK_W = 1024
BIG = (
    ("ffn_w1", (4, 2, 1024, 704), 3), ("ffn_w3", (4, 2, 1024, 704), 3), ("ffn_w2", (4, 2, 704, 1024), 2),
    ("ev_w_in", (2, 1024, 690), 2), ("mla_w_uq", (2, 96, 4, 192), 1), ("mla_w_ukv", (2, 64, 4, 256), 1),
    ("ev_w_out", (2, 256, 1024), 1), ("ssd_w_in", (2, 1024, 1288), 2), ("ssd_w_out", (2, 512, 1024), 1))


def _seg_rows(shape):
    n = math.prod(shape)
    return -(-n // (16 * PACK_W)) * 16


PACK_ROWS = -(-sum(_seg_rows(sh) for _, sh, _ in BIG) // 512) * 512


def _pack(shards, dtype):
    parts = []
    for (_, shape, _), a in zip(BIG, shards):
        flat = a.reshape(-1).astype(dtype)
        pad = _seg_rows(shape) * PACK_W - flat.shape[0]
        parts.append(jnp.pad(flat, (0, pad)) if pad else flat)
    tail = PACK_ROWS - sum(_seg_rows(sh) for _, sh, _ in BIG)
    if tail:
        parts.append(jnp.zeros((tail * PACK_W,), dtype))
    return jnp.concatenate(parts).reshape(-1, PACK_W)


def _unpack(buf):
    out, r0 = [], 0
    for _, shape, _ in BIG:
        n = math.prod(shape)
        out.append(buf[r0:r0 + _seg_rows(shape)].reshape(-1)[:n].reshape(shape))
        r0 += _seg_rows(shape)
    return out


SMALL_SHARDED = (
    ("norm_g", (4, 3, 256), 2), ("gdn_conv_w", (2, 4, 384), 2), ("ssd_conv_w", (2, 4, 768), 2),
    ("ssd_conv_b", (2, 768), 1), ("ssd_norm_g", (2, 512), 1))


def _flat_pack(arrs, width, row_mult):
    flat = jnp.concatenate([a.reshape(-1).astype(F32) for a in arrs])
    n = flat.shape[0]
    tot = -(-n // (width * row_mult)) * width * row_mult
    return jnp.pad(flat, (0, tot - n)).reshape(-1, width)


def _flat_unpack(buf, shapes):
    flat = buf.reshape(-1)
    out, o = [], 0
    for s in shapes:
        n = math.prod(s)
        out.append(flat[o:o + n].reshape(s))
        o += n
    return out


def _rep(v, n):
    return jnp.repeat(v, n, axis=-1)


def kernel(x, c, positions, ada_w, ada_b, norm_g, ffn_w1, ffn_w3, ffn_w2, ev_w_in, gdn_conv_w, gdn_A_log, gdn_dt_bias, gdn_norm_g, mla_q_norm_g, mla_w_uq, mla_kv_norm_g, mla_w_ukv, ev_w_out, ssd_w_in, ssd_conv_w, ssd_conv_b, ssd_A_log, ssd_dt_bias, ssd_D, ssd_norm_g, ssd_w_out, final_g, loss_target, m_ada_w, m_ada_b, m_norm_g, m_ffn_w1, m_ffn_w3, m_ffn_w2, m_ev_w_in, m_gdn_conv_w, m_gdn_A_log, m_gdn_dt_bias, m_gdn_norm_g, m_mla_q_norm_g, m_mla_w_uq, m_mla_kv_norm_g, m_mla_w_ukv, m_ev_w_out, m_ssd_w_in, m_ssd_conv_w, m_ssd_conv_b, m_ssd_A_log, m_ssd_dt_bias, m_ssd_D, m_ssd_norm_g, m_ssd_w_out, m_final_g, v_ada_w, v_ada_b, v_norm_g, v_ffn_w1, v_ffn_w3, v_ffn_w2, v_ev_w_in, v_gdn_conv_w, v_gdn_A_log, v_gdn_dt_bias, v_gdn_norm_g, v_mla_q_norm_g, v_mla_w_uq, v_mla_kv_norm_g, v_mla_w_ukv, v_ev_w_out, v_ssd_w_in, v_ssd_conv_w, v_ssd_conv_b, v_ssd_A_log, v_ssd_dt_bias, v_ssd_D, v_ssd_norm_g, v_ssd_w_out, v_final_g):
    P = dict(ada_w=ada_w, ada_b=ada_b, norm_g=norm_g, ffn_w1=ffn_w1, ffn_w3=ffn_w3, ffn_w2=ffn_w2, ev_w_in=ev_w_in, gdn_conv_w=gdn_conv_w, gdn_A_log=gdn_A_log, gdn_dt_bias=gdn_dt_bias, gdn_norm_g=gdn_norm_g, mla_q_norm_g=mla_q_norm_g, mla_w_uq=mla_w_uq, mla_kv_norm_g=mla_kv_norm_g, mla_w_ukv=mla_w_ukv, ev_w_out=ev_w_out, ssd_w_in=ssd_w_in, ssd_conv_w=ssd_conv_w, ssd_conv_b=ssd_conv_b, ssd_A_log=ssd_A_log, ssd_dt_bias=ssd_dt_bias, ssd_D=ssd_D, ssd_norm_g=ssd_norm_g, ssd_w_out=ssd_w_out, final_g=final_g)
    M1 = dict(ada_w=m_ada_w, ada_b=m_ada_b, norm_g=m_norm_g, ffn_w1=m_ffn_w1, ffn_w3=m_ffn_w3, ffn_w2=m_ffn_w2, ev_w_in=m_ev_w_in, gdn_conv_w=m_gdn_conv_w, gdn_A_log=m_gdn_A_log, gdn_dt_bias=m_gdn_dt_bias, gdn_norm_g=m_gdn_norm_g, mla_q_norm_g=m_mla_q_norm_g, mla_w_uq=m_mla_w_uq, mla_kv_norm_g=m_mla_kv_norm_g, mla_w_ukv=m_mla_w_ukv, ev_w_out=m_ev_w_out, ssd_w_in=m_ssd_w_in, ssd_conv_w=m_ssd_conv_w, ssd_conv_b=m_ssd_conv_b, ssd_A_log=m_ssd_A_log, ssd_dt_bias=m_ssd_dt_bias, ssd_D=m_ssd_D, ssd_norm_g=m_ssd_norm_g, ssd_w_out=m_ssd_w_out, final_g=m_final_g)
    M2 = dict(ada_w=v_ada_w, ada_b=v_ada_b, norm_g=v_norm_g, ffn_w1=v_ffn_w1, ffn_w3=v_ffn_w3, ffn_w2=v_ffn_w2, ev_w_in=v_ev_w_in, gdn_conv_w=v_gdn_conv_w, gdn_A_log=v_gdn_A_log, gdn_dt_bias=v_gdn_dt_bias, gdn_norm_g=v_gdn_norm_g, mla_q_norm_g=v_mla_q_norm_g, mla_w_uq=v_mla_w_uq, mla_kv_norm_g=v_mla_kv_norm_g, mla_w_ukv=v_mla_w_ukv, ev_w_out=v_ev_w_out, ssd_w_in=v_ssd_w_in, ssd_conv_w=v_ssd_conv_w, ssd_conv_b=v_ssd_conv_b, ssd_A_log=v_ssd_A_log, ssd_dt_bias=v_ssd_dt_bias, ssd_D=v_ssd_D, ssd_norm_g=v_ssd_norm_g, ssd_w_out=v_ssd_w_out, final_g=v_final_g)
    names = list(P)
    xi, yi, ci = lax.axis_index("x"), lax.axis_index("y"), lax.axis_index("c")
    chip = 2 * xi + yi
    bidx = 4 * xi + 2 * yi + ci
    xa = x[0]
    S_, D = xa.shape
    tgt = loss_target[0]
    depth = ffn_w1.shape[0]

    wg = allgather_chips_2level(_pack([P[n] for n, _, _ in BIG], BF16), name="gather_weights")
    per_chip = [_unpack(wg[s]) for s in range(4)]
    W = {n: jnp.concatenate([per_chip[s][k] for s in range(4)], axis=ax) for k, (n, _, ax) in enumerate(BIG)}
    sg = allgather_chips(_flat_pack([P[n] for n, _, _ in SMALL_SHARDED], 1024, 16), name="gather_small")
    per_chip_s = [_flat_unpack(sg[s], [sh for _, sh, _ in SMALL_SHARDED]) for s in range(4)]
    Wf = {n: jnp.concatenate([per_chip_s[s][k] for s in range(4)], axis=ax) for k, (n, _, ax) in enumerate(SMALL_SHARDED)}

    c_all = allgather8(jnp.pad(c, ((0, 7), (0, 0))), name="gather_c").reshape(8, 8, D)[:, 0]
    c_act, = rowmap(lambda t: (_silu(t),), [jnp.pad(c_all, ((0, 8), (0, 0)))], [], (F32,), name="c_act", tile=16)
    ncol = ada_w.shape[2]
    ada_b_loc = lax.dynamic_slice(ada_b, (0, chip * ncol), (depth, ncol))
    mod_loc = [mm((c_act, ada_w[l]), name=f"mod_{l}", epi=lambda acc, b: (acc + b,), epi_pars=(ada_b_loc[l][None],),
                  epi_out_dtypes=(F32,), tm=16, tn=256)[0][:8] for l in range(depth)]
    mod_g = allgather8(jnp.stack(mod_loc).reshape(-1, 1024), name="gather_mod").reshape(8, depth, 8, ncol)
    mod_b = lax.dynamic_index_in_dim(mod_g[0::2], bidx, axis=2, keepdims=False)
    mod = jnp.transpose(mod_b, (1, 0, 2)).reshape(depth, 3, 3, D)

    def ev_ext(w):
        z = lambda n: jnp.zeros((w.shape[0], n), w.dtype)
        return jnp.concatenate([w[:, :2048], _rep(w[:, 2048:2052], 128), _rep(w[:, 2052:2056], 128), w[:, 2056:2440], z(128),
                                w[:, 2440:2696], w[:, 2696:2760], z(192)], axis=1)

    def ev_ext_t(dw):
        return jnp.concatenate([dw[:, :2048], dw[:, 2048:2560].reshape(-1, 4, 128).sum(-1), dw[:, 2560:3072].reshape(-1, 4, 128).sum(-1),
                                dw[:, 3072:3456], dw[:, 3584:3840], dw[:, 3840:3904]], axis=1)

    def od_ext(w):
        return jnp.concatenate([w[:, 2048:5120], w[:, :2048], _rep(w[:, 5120:5152], 64)], axis=1)

    def od_ext_t(dw):
        return jnp.concatenate([dw[:, 3072:5120], dw[:, :3072], dw[:, 5120:].reshape(-1, 32, 64).sum(-1)], axis=1)

    def wq_ext(w):
        return jnp.pad(w, ((0, 0), (0, 0), (0, 64))).reshape(384, 1024)

    def wq_ext_t(dw):
        return dw.reshape(384, 4, 256)[:, :, :192]

    def wkv_ext(w):
        return jnp.concatenate([w[:, :, :128].reshape(256, 512), w[:, :, 128:].reshape(256, 512)], axis=1)

    def wkv_ext_t(dw):
        return jnp.concatenate([dw[:, :512].reshape(256, 4, 128), dw[:, 512:].reshape(256, 4, 128)], axis=2)

    half = 32
    inv_freq = 10000.0 ** (-jnp.arange(half, dtype=F32) / half)
    ang = positions[0].astype(F32)[:, None] * inv_freq
    zpad = jnp.zeros((S_, 64), F32)
    cos_t = jnp.concatenate([jnp.cos(ang), jnp.cos(ang), zpad], axis=1)
    sin_t = jnp.concatenate([jnp.sin(ang), jnp.sin(ang), zpad], axis=1)
    ii = jnp.arange(128)
    rot = (jnp.where((ii[:, None] < 32) & (ii[None, :] == ii[:, None] + 32), 1.0, 0.0)
           - jnp.where((ii[:, None] >= 32) & (ii[:, None] < 64) & (ii[None, :] == ii[:, None] - 32), 1.0, 0.0)).astype(F32)

    grads = {}
    dmod = [[[None] * 3 for _ in range(3)] for _ in range(depth)]
    dnorm_g = [[None] * 3 for _ in range(depth)]

    def acc(name, idx, val):
        grads.setdefault(name, {})[idx] = val

    def ffn_sub(xin, l, k, j):
        g, (shift, scale, gate) = Wf["norm_g"][l, k][None], [mod[l, k, t][None] for t in range(3)]
        w1, w3, w2 = W["ffn_w1"][l, j], W["ffn_w3"][l, j], W["ffn_w2"][l, j]
        tag = f"l{l}f{j}"
        h, = rowmap(lambda *a: (_adaln(*a),), [xin], [g, shift, scale], (BF16,), name=f"adaln_{tag}")
        a = ffn_mid_fwd(h, w1, w3, name=f"ffn_mid_{tag}")
        xn, y = mm((a, w2), name=f"ffn_out_{tag}", epi=lambda acc_, xr, gt: (xr + 0.5 * gt * acc_, acc_), epi_rows=(xin,),
                   epi_pars=(gate,), epi_out_dtypes=(F32, F32))

        def bwd(dxn):
            dy, dgate = rowmap(lambda d, y_, gt: ((0.5 * gt) * d, jnp.sum(0.5 * y_ * d, axis=0, keepdims=True)), [dxn, y], [gate],
                               (BF16, F32), name=f"dres_{tag}", n_reduce=1)
            du, dv, a_ = ffn_mid_bwd(h, dy, w1, w3, w2, name=f"ffn_midb_{tag}")
            dh = mm([(du, w1), (dv, w3)], tb=True, name=f"ffn_dh_{tag}")
            acc("ffn_w1", (l, j), mm((h, du), ta=True, name=f"ffn_dw1_{tag}"))
            acc("ffn_w3", (l, j), mm((h, dv), ta=True, name=f"ffn_dw3_{tag}"))
            acc("ffn_w2", (l, j), mm((a_, dy), ta=True, name=f"ffn_dw2_{tag}"))
            dx, dg, dsh, dsc = _vjp_rows(_adaln, 1, 3, (F32,), [xin], [dh], [g, shift, scale], name=f"adalnb_{tag}", extra=dxn)
            dnorm_g[l][k] = dg[0]
            dmod[l][k] = [dsh[0], dsc[0], dgate[0]]
            return dx

        return xn, bwd

    def mixer_tail(xin, l, tag, dh, dxn, g, shift, scale, dgate):
        dx, dg, dsh, dsc = _vjp_rows(_adaln, 1, 3, (F32,), [xin], [dh], [g, shift, scale], name=f"adalnb_{tag}", extra=dxn)
        dnorm_g[l][1] = dg[0]
        dmod[l][1] = [dsh[0], dsc[0], dgate[0]]
        return dx

    def even_sub(xin, l):
        e = l // 2
        tag = f"l{l}m"
        g, (shift, scale, gate) = Wf["norm_g"][l, 1][None], [mod[l, 1, t][None] for t in range(3)]
        wext, wq, wkv, wout = ev_ext(W["ev_w_in"][e]), wq_ext(W["mla_w_uq"][e]), wkv_ext(W["mla_w_ukv"][e]), W["ev_w_out"][e]
        conv_w, zb = Wf["gdn_conv_w"][e], jnp.zeros((1, 1536), F32)
        alog_e, dtb_e = _rep(gdn_A_log[e], 128)[None], _rep(gdn_dt_bias[e], 128)[None]
        gg, qg, kvg = gdn_norm_g[e][None], mla_q_norm_g[e][None], mla_kv_norm_g[e][None]
        h, = rowmap(lambda *a: (_adaln(*a),), [xin], [g, shift, scale], (BF16,), name=f"adaln_{tag}")
        proj = mm((h, wext), name=f"ev_in_{tag}")
        qkvc = conv_fwd(proj, 0, 3, conv_w, zb, name=f"gdn_conv_{tag}")
        o_g, hist = gdn_fwd(qkvc, proj, 4, 5, alog_e, dtb_e, name=f"gdn_{tag}")
        o_a, = rowmap(lambda o, z, g_: (_gdn_gate(o, z, g_),), [o_g, (proj, 512, 3)], [gg], (BF16,), name=f"gdn_gate_{tag}")
        cqn, = rowmap(lambda t, g_: (_rms(t, g_),), [(proj, 384, 8)], [qg], (BF16,), name=f"q_norm_{tag}")
        ckvn, = rowmap(lambda t, g_: (_rms(t, g_),), [(proj, 256, 14)], [kvg], (BF16,), name=f"kv_norm_{tag}")
        q0 = mm((cqn, wq), name=f"q_up_{tag}")
        kv = mm((ckvn, wkv), name=f"kv_up_{tag}", out_dtype=BF16)
        q, = rowmap(lambda t, cs, sn, r: (_rope_q(t, cs, sn, r),), [q0, cos_t, sin_t], [rot], (BF16,), name=f"rope_q_{tag}")
        kp, = rowmap(lambda t, cs, sn, r: (_rope(t, cs, sn, r),), [(proj, 128, 30), cos_t, sin_t], [rot], (BF16,), name=f"rope_k_{tag}")
        o_b, lse = att_fwd(q, kv, kp, name=f"att_{tag}")
        xn, y = mm([(o_a, wout[:512]), (o_b, wout[512:])], name=f"ev_out_{tag}", epi=lambda acc_, xr, gt: (xr + gt * acc_, acc_),
                   epi_rows=(xin,), epi_pars=(gate,), epi_out_dtypes=(F32, F32))

        def bwd(dxn):
            dy, dgate = rowmap(lambda d, y_, gt: (gt * d, jnp.sum(y_ * d, axis=0, keepdims=True)), [dxn, y], [gate],
                               (BF16, F32), name=f"dres_{tag}", n_reduce=1)
            do_a = mm((dy, wout[:512]), tb=True, name=f"ev_doa_{tag}")
            do_b = mm((dy, wout[512:]), tb=True, name=f"ev_dob_{tag}")
            acc("ev_w_out", e, jnp.concatenate([mm((o_a, dy), ta=True, name=f"ev_dwoa_{tag}"), mm((o_b, dy), ta=True, name=f"ev_dwob_{tag}")], axis=0))
            dsum, = rowmap(lambda d, o_: (jnp.concatenate([jnp.broadcast_to(jnp.sum(d[:, 128 * hh:128 * (hh + 1)] * o_[:, 128 * hh:128 * (hh + 1)],
                                                                                        axis=-1, keepdims=True), (d.shape[0], 128))
                                                  for hh in range(4)], axis=1),), [do_b, o_b], [], (F32,), name=f"att_dsum_{tag}")
            dq4, dk2, dv = att_bwd(q, kv, kp, lse, dsum, do_b, name=f"att_bwd_{tag}")

            def rope_qb(d0, d1, d2, d3, cs, sn, r):
                parts = []
                for d in (d0, d1, d2, d3):
                    parts += [d[:, :128], _rope_t(d[:, 128:], cs, sn, r)]
                return (jnp.concatenate(parts, axis=1) * ATT_SCALE,)

            dq0, = rowmap(rope_qb, [dq4[0], dq4[1], dq4[2], dq4[3], cos_t, sin_t], [rot], (BF16,), name=f"rope_qb_{tag}")

            def rope_kb(d, cs, sn, r):
                dkp = d[:, 128:256] + d[:, 384:512] + d[:, 640:768] + d[:, 896:1024]
                return jnp.concatenate([d[:, 256 * hh:256 * hh + 128] for hh in range(4)], axis=1), _rope_t(dkp, cs, sn, r)

            dkn, dkr = rowmap(rope_kb, [dk2, cos_t, sin_t], [rot], (BF16, BF16), name=f"rope_kb_{tag}")
            dcqn = mm((dq0, wq), tb=True, name=f"q_upb_{tag}")
            acc("mla_w_uq", e, wq_ext_t(mm((cqn, dq0), ta=True, name=f"q_dw_{tag}")))
            dckvn = mm([(dkn, wkv[:, :512]), (dv, wkv[:, 512:])], tb=True, name=f"kv_upb_{tag}")
            acc("mla_w_ukv", e, wkv_ext_t(jnp.concatenate([mm((ckvn, dkn), ta=True, name=f"kv_dwk_{tag}"), mm((ckvn, dv), ta=True, name=f"kv_dwv_{tag}")], axis=1)))
            dcq, dqg = _vjp_rows(_rms, 1, 1, (BF16,), [(proj, 384, 8)], [dcqn], [qg], name=f"q_normb_{tag}")
            dckv, dkvg = _vjp_rows(_rms, 1, 1, (BF16,), [(proj, 256, 14)], [dckvn], [kvg], name=f"kv_normb_{tag}")
            acc("mla_q_norm_g", e, dqg[0])
            acc("mla_kv_norm_g", e, dkvg[0])
            do_g, dz, dgg = _vjp_rows(_gdn_gate, 2, 1, (F32, BF16), [o_g, (proj, 512, 3)], [do_a], [gg], name=f"gdn_gateb_{tag}")
            acc("gdn_norm_g", e, dgg[0])
            dqkvc, dbe, dae, dal, ddt = gdn_bwd(qkvc, proj, 4, 5, alog_e, dtb_e, hist, do_g, name=f"gdnb_{tag}")
            acc("gdn_A_log", e, dal.reshape(4, 128).sum(-1))
            acc("gdn_dt_bias", e, ddt.reshape(4, 128).sum(-1))
            dpre, dcw, _ = conv_bwd_pre(proj, 0, 3, conv_w, zb, dqkvc, name=f"gdn_convb_{tag}")
            acc("gdn_conv_w", e, dcw)
            dqkv = conv_bwd_x(dpre, conv_w, name=f"gdn_convx_{tag}", out_dtype=BF16)
            zc = lambda n: jnp.zeros((S_, n), BF16)
            dproj = jnp.concatenate([dqkv, dz, dbe.astype(BF16), dae.astype(BF16), dcq, zc(128), dckv, dkr, zc(128)], axis=1)
            dh = mm((dproj, wext), tb=True, name=f"ev_inb_{tag}")
            acc("ev_w_in", e, ev_ext_t(mm((h, dproj), ta=True, name=f"ev_dwin_{tag}")))
            return mixer_tail(xin, l, tag, dh, dxn, g, shift, scale, dgate)

        return xn, bwd

    def odd_sub(xin, l):
        o = l // 2
        tag = f"l{l}m"
        g, (shift, scale, gate) = Wf["norm_g"][l, 1][None], [mod[l, 1, t][None] for t in range(3)]
        wext, wout = od_ext(W["ssd_w_in"][o]), W["ssd_w_out"][o]
        conv_w, conv_b, ng = Wf["ssd_conv_w"][o], Wf["ssd_conv_b"][o][None], Wf["ssd_norm_g"][o][None]
        ex = lambda v: _rep(v, 64)[None]
        na_e, dtb_e, dsk_e = ex(-jnp.exp(ssd_A_log[o])), ex(ssd_dt_bias[o]), ex(ssd_D[o])
        h, = rowmap(lambda *a: (_adaln(*a),), [xin], [g, shift, scale], (BF16,), name=f"adaln_{tag}")
        proj = mm((h, wext), name=f"ssd_in_{tag}")
        zv = [(proj, 512, 6 + t) for t in range(4)]
        xbc = conv_fwd(proj, 0, 6, conv_w, conv_b, name=f"ssd_conv_{tag}")
        ys, hist = ssd_fwd(xbc, proj, na_e, dtb_e, dsk_e, name=f"ssd_{tag}")
        yn, = rowmap(lambda *a: (_ssd_gate(*a),), [ys] + zv, [ng], (BF16,), name=f"ssd_gate_{tag}", tile=256)
        xn, y = mm((yn, wout), name=f"ssd_out_{tag}", epi=lambda acc_, xr, gt: (xr + gt * acc_, acc_), epi_rows=(xin,),
                   epi_pars=(gate,), epi_out_dtypes=(F32, F32))

        def bwd(dxn):
            dy, dgate = rowmap(lambda d, y_, gt: (gt * d, jnp.sum(y_ * d, axis=0, keepdims=True)), [dxn, y], [gate],
                               (BF16, F32), name=f"dres_{tag}", n_reduce=1)
            dyn = mm((dy, wout), tb=True, name=f"ssd_dyn_{tag}", out_dtype=BF16)
            acc("ssd_w_out", o, mm((yn, dy), ta=True, name=f"ssd_dwout_{tag}"))
            dys, dz0, dz1, dz2, dz3, dng = _vjp_rows(_ssd_gate, 5, 1, (F32, BF16, BF16, BF16, BF16), [ys] + zv, [dyn], [ng],
                                                     name=f"ssd_gateb_{tag}", tile=256)
            acc("ssd_norm_g", o, dng[0])
            dxs, ddtx, db_, dc_, dna, ddtb, ddsk = ssd_bwd(xbc, proj, na_e, dtb_e, dsk_e, hist, dys, name=f"ssdb_{tag}")
            acc("ssd_A_log", o, dna.reshape(32, 64).sum(-1) * (-jnp.exp(ssd_A_log[o])))
            acc("ssd_dt_bias", o, ddtb.reshape(32, 64).sum(-1))
            acc("ssd_D", o, ddsk.reshape(32, 64).sum(-1))
            dxbc = jnp.concatenate([dxs, db_, dc_], axis=1)
            dpre, dcw, dcb = conv_bwd_pre(proj, 0, 6, conv_w, conv_b, dxbc, name=f"ssd_convb_{tag}")
            acc("ssd_conv_w", o, dcw)
            acc("ssd_conv_b", o, dcb[0])
            dxp = conv_bwd_x(dpre, conv_w, name=f"ssd_convx_{tag}", out_dtype=BF16)
            dproj = jnp.concatenate([dxp, dz0, dz1, dz2, dz3, ddtx.astype(BF16)], axis=1)
            dh = mm((dproj, wext), tb=True, name=f"ssd_inb_{tag}")
            acc("ssd_w_in", o, od_ext_t(mm((h, dproj), ta=True, name=f"ssd_dwin_{tag}")))
            return mixer_tail(xin, l, tag, dh, dxn, g, shift, scale, dgate)

        return xn, bwd

    tape = []
    xc = xa
    for l in range(depth):
        xc, b0 = ffn_sub(xc, l, 0, 0)
        xc, b1 = (even_sub if l % 2 == 0 else odd_sub)(xc, l)
        xc, b2 = ffn_sub(xc, l, 2, 1)
        tape += [b0, b1, b2]

    def head(xr, tg, g_):
        def f(xv, gv):
            err = _rms(xv, gv) - tg
            return 0.5 * jnp.sum(jnp.mean(err * err, axis=-1, keepdims=True), axis=0, keepdims=True)
        lo, vjp = jax.vjp(f, xr, g_)
        dxv, dgv = vjp(jnp.ones_like(lo))
        return dxv, jnp.broadcast_to(lo, (1, 128)), dgv

    dx, loss_p, dfg = rowmap(head, [xc, tgt], [final_g[None]], (F32,), name="loss_head", n_reduce=2)
    loss = lax.psum(loss_p[0, 0], ("x", "y", "c"))

    for b in reversed(tape):
        dx = b(dx)
    grad_x = dx[None]

    full = {n: jnp.stack([grads[n][k] for k in sorted(grads[n])]) for n in ("ev_w_in", "mla_w_uq", "mla_w_ukv", "ev_w_out", "ssd_w_in", "ssd_w_out")}
    for n in ("ffn_w1", "ffn_w3", "ffn_w2"):
        full[n] = jnp.stack([jnp.stack([grads[n][(l, j)] for j in range(2)]) for l in range(depth)])
    chunks = []
    for s in range(4):
        chunks.append(_pack([lax.slice_in_dim(full[n], s * sh[ax], (s + 1) * sh[ax], axis=ax) for n, sh, ax in BIG], BF16))
    gall = jnp.stack(chunks)
    half = PACK_ROWS // 2
    from_sib = swap_other_half(gall, name="swap_half")
    pair = add_pairs(lax.dynamic_slice_in_dim(gall, ci * half, half, axis=1), from_sib, name="add_sibling", out_dtype=BF16)
    recv = exchange_chips(pair, name="exchange_grads")
    part = sum_slots(recv, name="sum_chips")
    sib = swap_sibling(part, name="swap_sibling")
    lo = jnp.where(ci == 0, part, sib)
    hi_ = jnp.where(ci == 0, sib, part)
    g_tot = _unpack(jnp.concatenate([lo, hi_], axis=0))

    dmod_flat = jnp.stack([jnp.stack([jnp.stack(dmod[l][k]) for k in range(3)]) for l in range(depth)]).reshape(depth, 9 * D)
    small_names = ["norm_g", "gdn_conv_w", "gdn_A_log", "gdn_dt_bias", "gdn_norm_g", "mla_q_norm_g", "mla_kv_norm_g",
                   "ssd_conv_w", "ssd_conv_b", "ssd_A_log", "ssd_dt_bias", "ssd_D", "ssd_norm_g", "final_g"]
    small_full = {n: jnp.stack([grads[n][k] for k in sorted(grads[n])]) for n in small_names if n in grads}
    small_full["norm_g"] = jnp.stack([jnp.stack(dnorm_g[l]) for l in range(depth)])
    small_full["final_g"] = dfg[0]
    small_list = [dmod_flat] + [small_full[n] for n in small_names]
    small_shapes = [a.shape for a in small_list]
    sp = _flat_pack(small_list, 128, 8)
    sgath = allgather8(sp, name="gather_small_grads").reshape(8, sp.shape[0], 128)
    ssum = sum_slots(sgath, name="sum_small")
    tot = dict(zip(["ada_b"] + small_names, _flat_unpack(ssum, small_shapes)))
    dmod_all = sgath.reshape(8, -1)[:, :depth * 9 * D].reshape(8, depth, 9 * D)
    dmod_loc = lax.dynamic_slice(dmod_all, (0, 0, chip * ncol), (8, depth, ncol))
    g_ada_w = jnp.stack([mm((c_act, jnp.pad(dmod_loc[:, l], ((0, 8), (0, 0)))), ta=True, name=f"ada_dw_{l}", tk=16, tn=256)
                         for l in range(depth)])

    def own(n, a):
        for m_, sh, ax in SMALL_SHARDED:
            if m_ == n:
                return lax.dynamic_slice_in_dim(a, chip * sh[ax], sh[ax], axis=ax)
        return a

    res = {}
    for k, (n, _, _) in enumerate(BIG):
        res[n] = adamw(P[n], [g_tot[k]], M1[n], M2[n], name=f"adamw_{n}")
    res["ada_w"] = adamw(ada_w, [g_ada_w], m_ada_w, v_ada_w, name="adamw_ada_w")
    sm = ["ada_b"] + small_names
    shapes = [P[n].shape for n in sm]
    pk = lambda d: _flat_pack([d[n] for n in sm], 128, 8)
    outs = adamw(pk(P), [pk({n: own(n, tot[n]).reshape(P[n].shape) for n in sm})], pk(M1), pk(M2), name="adamw_small")
    un = [_flat_unpack(o, shapes) for o in outs]
    for i, n in enumerate(sm):
        res[n] = tuple(un[t][i] for t in range(4))
    return (loss, grad_x, *[res[n][0] for n in names], *[res[n][1] for n in names], *[res[n][2] for n in names], *[res[n][3] for n in names])
```

```python
import functools
import math

import jax
import jax.numpy as jnp
from jax import lax
from jax.experimental import pallas as pl
from jax.experimental.pallas import tpu as pltpu

F32 = jnp.float32
BF16 = jnp.bfloat16
HI = lax.Precision.HIGHEST
HI3 = lax.Precision.HIGH
VMEM_LIMIT = 56 * 1024 * 1024
NORM_EPS = 1e-6
MM_VMEM_BUDGET = 40 * 1024 * 1024


def _cp(sem=None):
    if sem is None:
        return pltpu.CompilerParams(vmem_limit_bytes=VMEM_LIMIT)
    return pltpu.CompilerParams(dimension_semantics=sem, vmem_limit_bytes=VMEM_LIMIT)


def _pick(dim, prefs):
    for p in prefs:
        if dim % p == 0:
            return p
    return dim


def mm(pairs, *, ta=False, tb=False, out_dtype=F32, name, epi=None, epi_rows=(), epi_pars=(), epi_out_dtypes=None,
       tm=None, tn=None, tk=None):
    if not isinstance(pairs, (list, tuple)) or not isinstance(pairs[0], (list, tuple)):
        pairs = [pairs]
    npair = len(pairs)
    a0, b0 = pairs[0]
    M = a0.shape[1] if ta else a0.shape[0]
    K = a0.shape[0] if ta else a0.shape[1]
    N = b0.shape[0] if tb else b0.shape[1]
    for a, b in pairs:
        assert (a.shape == ((K, M) if ta else (M, K))), (a.shape, M, K)
        assert (b.shape == ((N, K) if tb else (K, N))), (b.shape, K, N)
    tm = tm or _pick(M, (1024, 1408, 512, 384, 256, 128))
    tk = tk or (K if K <= 1024 else _pick(K, (1024, 1408, 512, 256, 128)))
    if tn is None:
        n_epi_out = 1 if epi is None else len(epi_out_dtypes)
        for tn in (1024, 1408, 512, 384, 256, 128, N):
            if N % tn:
                continue
            need = sum(2 * tk * (tm * a.dtype.itemsize + tn * b.dtype.itemsize) for a, b in pairs)
            need += tm * tn * 4 * (1 + 2 * n_epi_out + 2 * len(epi_rows))
            if need <= MM_VMEM_BUDGET:
                break
    nk = K // tk
    assert M % tm == 0 and N % tn == 0 and K % tk == 0, (M, N, K, tm, tn, tk)
    n_rows, n_pars = len(epi_rows), len(epi_pars)
    if epi is None:
        out_dtypes = (out_dtype,)
    else:
        out_dtypes = tuple(epi_out_dtypes)
    n_out = len(out_dtypes)
    dn = (((0 if ta else 1,), (1 if tb else 0,)), ((), ()))

    def body(*refs):
        ab = refs[:2 * npair]
        rows = refs[2 * npair:2 * npair + n_rows]
        pars = refs[2 * npair + n_rows:2 * npair + n_rows + n_pars]
        outs = refs[2 * npair + n_rows + n_pars:2 * npair + n_rows + n_pars + n_out]
        acc_ref = refs[-1]
        k = pl.program_id(2)

        @pl.when(k == 0)
        def _():
            acc_ref[...] = jnp.zeros_like(acc_ref)

        acc = acc_ref[...]
        for p in range(npair):
            a = ab[2 * p][...].astype(BF16)
            b = ab[2 * p + 1][...].astype(BF16)
            acc = acc + lax.dot_general(a, b, dn, preferred_element_type=F32)
        acc_ref[...] = acc

        @pl.when(k == nk - 1)
        def _():
            r = acc_ref[...]
            if epi is None:
                outs[0][...] = r.astype(outs[0].dtype)
            else:
                res = epi(r, *[x[...] for x in rows], *[x[...] for x in pars])
                for o, v in zip(outs, res):
                    o[...] = v.astype(o.dtype)

    a_spec = pl.BlockSpec((tk, tm), lambda i, j, k: (k, i)) if ta else pl.BlockSpec((tm, tk), lambda i, j, k: (i, k))
    b_spec = pl.BlockSpec((tn, tk), lambda i, j, k: (j, k)) if tb else pl.BlockSpec((tk, tn), lambda i, j, k: (k, j))
    in_specs = []
    args = []
    for a, b in pairs:
        in_specs += [a_spec, b_spec]
        args += [a, b]
    for r in epi_rows:
        in_specs.append(pl.BlockSpec((tm, tn), lambda i, j, k: (i, j)))
        args.append(r)
    for p_ in epi_pars:
        in_specs.append(pl.BlockSpec((1, tn), lambda i, j, k: (0, j)))
        args.append(p_)
    out_specs = [pl.BlockSpec((tm, tn), lambda i, j, k: (i, j)) for _ in range(n_out)]
    out_shape = [jax.ShapeDtypeStruct((M, N), d) for d in out_dtypes]
    res = pl.pallas_call(
        body, grid=(M // tm, N // tn, nk), in_specs=in_specs, out_specs=out_specs, out_shape=out_shape,
        scratch_shapes=[pltpu.VMEM((tm, tn), F32)], compiler_params=_cp(("parallel", "parallel", "arbitrary")), name=name,
    )(*args)
    return res[0] if epi is None else tuple(res)


def rowmap(fn, rows, pars, out_dtypes, *, name, tile=512, n_reduce=0):
    views = []
    for r in rows:
        if isinstance(r, tuple):
            views.append(r)
        else:
            views.append((r, r.shape[1], 0))
    S = views[0][0].shape[0]
    tile = min(tile, S)
    assert S % tile == 0
    nt = S // tile
    row_structs = [jax.ShapeDtypeStruct((tile, w), a.dtype) for a, w, _ in views]
    par_structs = [jax.ShapeDtypeStruct(p.shape, p.dtype) for p in pars]
    out_structs = jax.eval_shape(fn, *row_structs, *par_structs)
    n_out = len(out_structs)
    n_row_out = n_out - n_reduce
    nr, npar = len(views), len(pars)

    def body(*refs):
        ins = [x[...] for x in refs[:nr + npar]]
        outs = refs[nr + npar:]
        res = fn(*ins)
        for o, v in zip(outs[:n_row_out], res[:n_row_out]):
            o[...] = v.astype(o.dtype)
        if n_reduce:
            i = pl.program_id(0)

            @pl.when(i == 0)
            def _():
                for o, v in zip(outs[n_row_out:], res[n_row_out:]):
                    o[...] = v.astype(o.dtype)

            @pl.when(i > 0)
            def _():
                for o, v in zip(outs[n_row_out:], res[n_row_out:]):
                    o[...] += v.astype(o.dtype)

    in_specs = [pl.BlockSpec((tile, w), functools.partial(lambda i, c: (i, c), c=c)) for _, w, c in views]
    in_specs += [pl.BlockSpec(p.shape, lambda i: (0, 0)) for p in pars]
    out_specs = [pl.BlockSpec((tile, s.shape[1]), lambda i: (i, 0)) for s in out_structs[:n_row_out]]
    out_specs += [pl.BlockSpec(s.shape, lambda i: (0, 0)) for s in out_structs[n_row_out:]]
    out_shape = [jax.ShapeDtypeStruct((S, s.shape[1]), d) for s, d in zip(out_structs[:n_row_out], out_dtypes[:n_row_out])]
    out_shape += [jax.ShapeDtypeStruct(s.shape, F32) for s in out_structs[n_row_out:]]
    res = pl.pallas_call(
        body, grid=(nt,), in_specs=in_specs, out_specs=out_specs, out_shape=out_shape,
        compiler_params=_cp(("arbitrary",) if n_reduce else ("parallel",)), name=name,
    )(*[v[0] for v in views], *pars)
    return tuple(res)


CH = 64


def _softplus(x):
    return jnp.where(x > 20.0, x, jnp.log(1.0 + jnp.exp(jnp.minimum(x, 20.0))))


def _dot(a, b, dn=(((1,), (0,)), ((), ())), hi=False):
    if hi:
        return lax.dot_general(a.astype(F32), b.astype(F32), dn, precision=HI if hi is True else hi, preferred_element_type=F32)
    return lax.dot_general(a.astype(BF16), b.astype(BF16), dn, preferred_element_type=F32)


_NT = (((1,), (1,)), ((), ()))
_TN = (((0,), (0,)), ((), ()))


def _chunk_consts():
    r = lax.broadcasted_iota(jnp.int32, (CH, 2 * CH), 0)
    c0 = lax.broadcasted_iota(jnp.int32, (CH, 2 * CH), 1)
    c = jnp.where(c0 >= CH, c0 - CH, c0)
    r1 = lax.broadcasted_iota(jnp.int32, (CH, CH), 0)
    c1 = lax.broadcasted_iota(jnp.int32, (CH, CH), 1)
    return dict(
        lower2=r >= c, strict2=r > c, U2=(r <= c).astype(F32), eye2=(r == c).astype(F32),
        L=(r1 >= c1).astype(F32), ones=jnp.ones((CH, CH), F32), Z=jnp.zeros((CH, 2 * CH), F32))


@jax.custom_vjp
def _tri_inv2(a2s, eye2, Z):
    def prod(x2, y):
        return _dot(x2, jnp.concatenate([y, Z], axis=0), hi=HI3)

    bs = [-a2 for a2 in a2s]
    ts = [eye2 + b for b in bs]
    for _ in range(5):
        bs = [prod(b, b) for b in bs]
        ts = [t + prod(t, b) for t, b in zip(ts, bs)]
    return tuple(ts)


def _tri_inv2_fwd(a2s, eye2, Z):
    ts = _tri_inv2(a2s, eye2, Z)
    return ts, (ts, eye2, Z)


def _tri_inv2_bwd(res, dts):
    ts, eye2, Z = res
    xs = [_dot(t2, dt2, _TN, hi=HI3)[:CH] for t2, dt2 in zip(ts, dts)]
    das = tuple(-_dot(x2, jnp.concatenate([t2, Z], axis=0), _NT, hi=HI3) for x2, t2 in zip(xs, ts))
    return das, jnp.zeros_like(eye2), jnp.zeros_like(Z)


_tri_inv2.defvjp(_tri_inv2_fwd, _tri_inv2_bwd)


def _gdn_heads(qs, ks, vs, bxs, axs, Ss, alogs, dtbs, cst):
    lower2, strict2, U2, eye2, L, ones, Z = (cst[n] for n in ("lower2", "strict2", "U2", "eye2", "L", "ones", "Z"))
    H = range(len(qs))

    def prod(x2, y):
        return _dot(x2, jnp.concatenate([y, Z], axis=0), hi=HI3)

    qn = [qs[h] * lax.rsqrt(jnp.sum(qs[h] * qs[h], axis=-1, keepdims=True) + NORM_EPS) * (128.0 ** -0.5) for h in H]
    kn = [ks[h] * lax.rsqrt(jnp.sum(ks[h] * ks[h], axis=-1, keepdims=True) + NORM_EPS) for h in H]
    beta = [jax.nn.sigmoid(bxs[h]) for h in H]
    g = [-jnp.exp(alogs[h]) * _softplus(axs[h] + dtbs[h]) for h in H]
    gc = [_dot(L, g[h], hi=HI3) for h in H]
    n2 = [_dot(ones, g[h] * U2, hi=HI3) for h in H]
    decay2 = [jnp.where(lower2, jnp.exp(jnp.where(lower2, gc[h] - n2[h], 0.0)), 0.0) for h in H]
    kb = [kn[h] * beta[h] for h in H]
    kn2 = [jnp.concatenate([kn[h], kn[h]], axis=0) for h in H]
    a2 = tuple(jnp.where(strict2, _dot(kb[h], kn2[h], _NT) * decay2[h], 0.0) for h in H)
    t2 = _tri_inv2(a2, eye2, Z)
    glast = [jnp.sum(g[h], axis=0, keepdims=True) for h in H]
    u = [prod(t2[h], vs[h] * beta[h]) for h in H]
    w = [prod(t2[h], kb[h] * jnp.exp(gc[h])) for h in H]
    attn2 = [jnp.where(lower2, _dot(qn[h], kn2[h], _NT) * decay2[h], 0.0) for h in H]
    k_end = [kn[h] * jnp.exp(glast[h] - gc[h]) for h in H]
    q_start = [qn[h] * jnp.exp(gc[h]) for h in H]
    v_new = [u[h] - _dot(w[h], Ss[h]) for h in H]
    o = [_dot(q_start[h], Ss[h]) + _dot(attn2[h], jnp.concatenate([v_new[h], Z], axis=0)) for h in H]
    s_new = [Ss[h] * jnp.exp(glast[h]) + _dot(k_end[h], v_new[h], _TN) for h in H]
    return tuple(o), tuple(s_new)


def gdn_fwd(qkv, proj, bcol, acol, alog_e, dtb_e, *, name):
    S_ = qkv.shape[0]
    nc = S_ // CH

    def body(q_ref, k_ref, v_ref, b_ref, a_ref, al_ref, dt_ref, o_ref, hist_ref, s_ref):
        i = pl.program_id(0)

        @pl.when(i == 0)
        def _():
            s_ref[...] = jnp.zeros_like(s_ref)

        cst = _chunk_consts()
        hist_ref[0] = s_ref[...]
        heads = [slice(128 * h, 128 * (h + 1)) for h in range(4)]
        rd = lambda ref: tuple(ref[:, ls] for ls in heads)
        os_, s_news = _gdn_heads(rd(q_ref), rd(k_ref), rd(v_ref), rd(b_ref), rd(a_ref), tuple(s_ref[ls, :] for ls in heads),
                                 rd(al_ref), rd(dt_ref), cst)
        for ls, o, s_new in zip(heads, os_, s_news):
            o_ref[:, ls] = o
            s_ref[ls, :] = s_new

    blk = lambda cb: pl.BlockSpec((CH, 512), functools.partial(lambda i, cb: (i, cb), cb=cb))
    par = pl.BlockSpec((1, 512), lambda i: (0, 0))
    return pl.pallas_call(
        body, grid=(nc,), in_specs=[blk(0), blk(1), blk(2), blk(bcol), blk(acol), par, par],
        out_specs=[pl.BlockSpec((CH, 512), lambda i: (i, 0)), pl.BlockSpec((1, 512, 128), lambda i: (i, 0, 0))],
        out_shape=[jax.ShapeDtypeStruct((S_, 512), F32), jax.ShapeDtypeStruct((nc, 512, 128), F32)],
        scratch_shapes=[pltpu.VMEM((512, 128), F32)], compiler_params=_cp(("arbitrary",)), name=name,
    )(qkv, qkv, qkv, proj, proj, alog_e, dtb_e)


def gdn_bwd(qkv, proj, bcol, acol, alog_e, dtb_e, hist, do, *, name):
    S_ = qkv.shape[0]
    nc = S_ // CH

    def body(q_ref, k_ref, v_ref, b_ref, a_ref, al_ref, dt_ref, hist_ref, do_ref, dqkv_ref, db_ref, da_ref, dal_ref, ddt_ref, ds_ref):
        i = pl.program_id(0)

        @pl.when(i == 0)
        def _():
            ds_ref[...] = jnp.zeros_like(ds_ref)
            dal_ref[...] = jnp.zeros_like(dal_ref)
            ddt_ref[...] = jnp.zeros_like(ddt_ref)

        cst = _chunk_consts()
        heads = [slice(128 * h, 128 * (h + 1)) for h in range(4)]
        rd = lambda ref: tuple(ref[:, ls] for ls in heads)
        fn = functools.partial(_gdn_heads, cst=cst)
        _, vjp = jax.vjp(fn, rd(q_ref), rd(k_ref), rd(v_ref), rd(b_ref), rd(a_ref), tuple(hist_ref[0, ls, :] for ls in heads),
                         rd(al_ref), rd(dt_ref))
        grads = vjp((rd(do_ref), tuple(ds_ref[ls, :] for ls in heads)))
        for h in range(4):
            ls = heads[h]
            dq, dk, dv, db, da, ds_in, dal, ddt = (t[h] for t in grads)
            dqkv_ref[:, 128 * h:128 * (h + 1)] = dq
            dqkv_ref[:, 512 + 128 * h:512 + 128 * (h + 1)] = dk
            dqkv_ref[:, 1024 + 128 * h:1024 + 128 * (h + 1)] = dv
            db_ref[:, ls] = db.astype(db_ref.dtype)
            da_ref[:, ls] = da.astype(da_ref.dtype)
            ds_ref[ls, :] = ds_in
            dal_ref[:, ls] += dal
            ddt_ref[:, ls] += ddt

    rblk = lambda cb: pl.BlockSpec((CH, 512), functools.partial(lambda i, cb: (nc - 1 - i, cb), cb=cb))
    par = pl.BlockSpec((1, 512), lambda i: (0, 0))
    return pl.pallas_call(
        body, grid=(nc,),
        in_specs=[rblk(0), rblk(1), rblk(2), rblk(bcol), rblk(acol), par, par,
                  pl.BlockSpec((1, 512, 128), lambda i: (nc - 1 - i, 0, 0)), rblk(0)],
        out_specs=[pl.BlockSpec((CH, 1536), lambda i: (nc - 1 - i, 0)), rblk(0), rblk(0), par, par],
        out_shape=[jax.ShapeDtypeStruct((S_, 1536), F32), jax.ShapeDtypeStruct((S_, 512), BF16), jax.ShapeDtypeStruct((S_, 512), BF16),
                   jax.ShapeDtypeStruct((1, 512), F32), jax.ShapeDtypeStruct((1, 512), F32)],
        scratch_shapes=[pltpu.VMEM((512, 128), F32)], compiler_params=_cp(("arbitrary",)), name=name,
    )(qkv, qkv, qkv, proj, proj, alog_e, dtb_e, hist, do)


def _ssd_pairs(xs, dtxs, bms, cms, hss, nas, dtbs, dsks, cst):
    lower2, U2, L, ones = cst["lower2"], cst["U2"], cst["L"], cst["ones"]
    lane = lax.broadcasted_iota(jnp.int32, (1, 2 * CH), 1)
    mask_l = (lane < CH).astype(F32)
    mask_r = 1.0 - mask_l
    ones_w = jnp.ones((CH, 2 * CH), F32)
    P_ = range(len(xs))
    G_ = range(len(bms))
    per = len(xs) // len(bms)
    cb2 = [_dot(cms[g], jnp.concatenate([bms[g], bms[g]], axis=0), _NT) for g in G_]
    dt = [_softplus(dtxs[p] + dtbs[p]) for p in P_]
    da = [dt[p] * nas[p] for p in P_]
    m = [_dot(L, da[p], hi=HI3) for p in P_]
    n2 = [_dot(ones, da[p] * U2, hi=HI3) for p in P_]
    lm2 = [jnp.where(lower2, jnp.exp(jnp.where(lower2, m[p] - n2[p], 0.0)), 0.0) for p in P_]
    xdt = [xs[p] * dt[p] for p in P_]
    x2 = [jnp.concatenate([xdt[p] * mask_l, xdt[p] * mask_r], axis=0) for p in P_]
    y_diag = [_dot(cb2[p // per] * lm2[p], x2[p]) for p in P_]
    alast = [jnp.sum(da[p], axis=0, keepdims=True) for p in P_]
    y_off = [_dot(cms[p // per], hss[p], _NT) * jnp.exp(m[p]) for p in P_]
    cd = [jnp.exp(_dot(da[p], ones_w, _TN, hi=HI3)) for p in P_]
    hs_new = [hss[p] * cd[p] + _dot(xdt[p] * jnp.exp(alast[p] - m[p]), bms[p // per], _TN) for p in P_]
    ys = [y_diag[p] + y_off[p] + dsks[p] * xs[p] for p in P_]
    return tuple(ys), tuple(hs_new)


def _ssd_specs(nc, rev):
    ci = (lambda i: nc - 1 - i) if rev else (lambda i: i)
    col = lambda w, c: pl.BlockSpec((CH, w), functools.partial(lambda i, c: (ci(i), c), c=c))
    xg = [col(512, g) for g in range(4)]
    dtg = [col(512, 10 + g) for g in range(4)]
    par = pl.BlockSpec((1, 2048), lambda i: (0, 0))
    hist = pl.BlockSpec((1, 2048, 128), lambda i: (ci(i), 0, 0))
    return xg, dtg, col(512, 4), col(512, 5), par, hist, col


def _ssd_read(x_refs, dt_refs, b_ref, c_ref, na_ref, dtb_ref, dsk_ref):
    sl = [slice(128 * p, 128 * (p + 1)) for p in range(4)]
    xs = tuple(x_refs[g][:, s] for g in range(4) for s in sl)
    dts = tuple(dt_refs[g][:, s] for g in range(4) for s in sl)
    bms = tuple(b_ref[:, s] for s in sl)
    cms = tuple(c_ref[:, s] for s in sl)
    lanes = [slice(128 * p, 128 * (p + 1)) for p in range(16)]
    pars = [tuple(r[:, s] for s in lanes) for r in (na_ref, dtb_ref, dsk_ref)]
    return xs, dts, bms, cms, pars, lanes


def ssd_fwd(xbc, proj, na_e, dtb_e, dsk_e, *, name):
    S_ = xbc.shape[0]
    nc = S_ // CH
    xg, dtg, bs, cs, par, hist, _ = _ssd_specs(nc, False)

    def body(*refs):
        x_refs, dt_refs = refs[0:4], refs[4:8]
        b_ref, c_ref, na_ref, dtb_ref, dsk_ref, y_ref, hist_ref, s_ref = refs[8:]
        i = pl.program_id(0)

        @pl.when(i == 0)
        def _():
            s_ref[...] = jnp.zeros_like(s_ref)

        cst = _chunk_consts()
        hist_ref[0] = s_ref[...]
        xs, dts, bms, cms, pars, lanes = _ssd_read(x_refs, dt_refs, b_ref, c_ref, na_ref, dtb_ref, dsk_ref)
        ys, hs_new = _ssd_pairs(xs, dts, bms, cms, tuple(s_ref[s, :] for s in lanes), *pars, cst)
        for p, s in enumerate(lanes):
            y_ref[:, s] = ys[p]
            s_ref[s, :] = hs_new[p]

    return pl.pallas_call(
        body, grid=(nc,), in_specs=xg + dtg + [bs, cs, par, par, par],
        out_specs=[pl.BlockSpec((CH, 2048), lambda i: (i, 0)), hist],
        out_shape=[jax.ShapeDtypeStruct((S_, 2048), F32), jax.ShapeDtypeStruct((nc, 2048, 128), F32)],
        scratch_shapes=[pltpu.VMEM((2048, 128), F32)], compiler_params=_cp(("arbitrary",)), name=name,
    )(xbc, xbc, xbc, xbc, proj, proj, proj, proj, xbc, xbc, na_e, dtb_e, dsk_e)


def ssd_bwd(xbc, proj, na_e, dtb_e, dsk_e, hist, dy, *, name):
    S_ = xbc.shape[0]
    nc = S_ // CH
    xg, dtg, bs, cs, par, hist_spec, col = _ssd_specs(nc, True)
    wide = pl.BlockSpec((CH, 2048), lambda i: (nc - 1 - i, 0))

    def body(*refs):
        x_refs, dt_refs = refs[0:4], refs[4:8]
        (b_ref, c_ref, na_ref, dtb_ref, dsk_ref, hist_ref, dy_ref,
         dx_ref, ddt_ref, db_ref, dc_ref, dna_ref, ddtb_ref, ddsk_ref, ds_ref) = refs[8:]
        i = pl.program_id(0)

        @pl.when(i == 0)
        def _():
            ds_ref[...] = jnp.zeros_like(ds_ref)
            dna_ref[...] = jnp.zeros_like(dna_ref)
            ddtb_ref[...] = jnp.zeros_like(ddtb_ref)
            ddsk_ref[...] = jnp.zeros_like(ddsk_ref)

        cst = _chunk_consts()
        xs, dts, bms, cms, pars, lanes = _ssd_read(x_refs, dt_refs, b_ref, c_ref, na_ref, dtb_ref, dsk_ref)
        fn = functools.partial(_ssd_pairs, cst=cst)
        _, vjp = jax.vjp(fn, xs, dts, bms, cms, tuple(hist_ref[0, s, :] for s in lanes), *pars)
        dxs, ddts, dbs, dcs, dhs, dnas, ddtbs, ddsks = vjp((tuple(dy_ref[:, s] for s in lanes), tuple(ds_ref[s, :] for s in lanes)))
        for g in range(4):
            db_ref[:, 128 * g:128 * (g + 1)] = dbs[g]
            dc_ref[:, 128 * g:128 * (g + 1)] = dcs[g]
        for p, s in enumerate(lanes):
            dx_ref[:, s] = dxs[p]
            ddt_ref[:, s] = ddts[p].astype(ddt_ref.dtype)
            ds_ref[s, :] = dhs[p]
            dna_ref[:, s] += dnas[p]
            ddtb_ref[:, s] += ddtbs[p]
            ddsk_ref[:, s] += ddsks[p]

    half = pl.BlockSpec((CH, 512), lambda i: (nc - 1 - i, 0))
    return pl.pallas_call(
        body, grid=(nc,), in_specs=xg + dtg + [bs, cs, par, par, par, hist_spec, wide],
        out_specs=[wide, wide, half, half, par, par, par],
        out_shape=[jax.ShapeDtypeStruct((S_, 2048), F32), jax.ShapeDtypeStruct((S_, 2048), BF16),
                   jax.ShapeDtypeStruct((S_, 512), F32), jax.ShapeDtypeStruct((S_, 512), F32)] +
                  [jax.ShapeDtypeStruct((1, 2048), F32)] * 3,
        scratch_shapes=[pltpu.VMEM((2048, 128), F32)], compiler_params=_cp(("arbitrary",)), name=name,
    )(xbc, xbc, xbc, xbc, proj, proj, proj, proj, xbc, xbc, na_e, dtb_e, dsk_e, hist, dy)


ATT_T = 1024
ATT_SCALE = 192.0 ** -0.5
NEG = -1e30


def _chunk_mask(shape):
    return lax.broadcasted_iota(jnp.int32, shape, 1) // CH <= lax.broadcasted_iota(jnp.int32, shape, 0) // CH


def _tri_pairs(n, by_row):
    pairs = [(i, j) for i in range(n) for j in range(i + 1)] if by_row else [(i, j) for j in range(n) for i in range(j, n)]
    return jnp.asarray([p[0] for p in pairs], jnp.int32), jnp.asarray([p[1] for p in pairs], jnp.int32)


def att_fwd(q, kv, kp, *, name):
    S_ = q.shape[0]
    T = min(ATT_T, S_)
    n = S_ // T
    ii, jj = _tri_pairs(n, True)

    def body(ii_ref, jj_ref, q_ref, kn_ref, kp_ref, v_ref, o_ref, lse_ref, m_ref, l_ref, acc_ref):
        t = pl.program_id(1)
        i, j = ii_ref[t], jj_ref[t]

        @pl.when(j == 0)
        def _():
            m_ref[...] = jnp.full_like(m_ref, NEG)
            l_ref[...] = jnp.zeros_like(l_ref)
            acc_ref[...] = jnp.zeros_like(acc_ref)

        def step(diag):
            k2 = jnp.concatenate([kn_ref[...], kp_ref[...]], axis=1)
            s = _dot(q_ref[...], k2, _NT)
            if diag:
                s = jnp.where(_chunk_mask(s.shape), s, NEG)
            m_prev = m_ref[...]
            m_cur = jnp.maximum(m_prev, jnp.max(s, axis=-1, keepdims=True))
            p = jnp.exp(s - m_cur[:, :1])
            alpha = jnp.exp(m_prev - m_cur)
            l_ref[...] = alpha * l_ref[...] + jnp.sum(p, axis=-1, keepdims=True)
            acc_ref[...] = acc_ref[...] * alpha + _dot(p, v_ref[...])
            m_ref[...] = m_cur

        @pl.when(j < i)
        def _():
            step(False)

        @pl.when(j == i)
        def _():
            step(True)
            o_ref[...] = acc_ref[...] / l_ref[...]
            lse_ref[...] = m_ref[...] + jnp.log(l_ref[...])

    grid_spec = pltpu.PrefetchScalarGridSpec(
        num_scalar_prefetch=2, grid=(4, ii.shape[0]),
        in_specs=[pl.BlockSpec((T, 256), lambda h, t, ii_, jj_: (ii_[t], h)), pl.BlockSpec((T, 128), lambda h, t, ii_, jj_: (jj_[t], h)),
                  pl.BlockSpec((T, 128), lambda h, t, ii_, jj_: (jj_[t], 0)),
                  pl.BlockSpec((T, 128), lambda h, t, ii_, jj_: (jj_[t], 4 + h))],
        out_specs=[pl.BlockSpec((T, 128), lambda h, t, ii_, jj_: (ii_[t], h))] * 2,
        scratch_shapes=[pltpu.VMEM((T, 128), F32)] * 3)
    return pl.pallas_call(
        body, grid_spec=grid_spec, out_shape=[jax.ShapeDtypeStruct((S_, 512), F32), jax.ShapeDtypeStruct((S_, 512), F32)],
        compiler_params=_cp(("parallel", "arbitrary")), name=name,
    )(ii, jj, q, kv, kp, kv)


def att_bwd(q, kv, kp, lse, dsum, do, *, name):
    S_ = q.shape[0]
    T = min(ATT_T, S_)
    n = S_ // T
    ii, jj = _tri_pairs(n, False)
    last = ii.shape[0] - 1

    def body(ii_ref, jj_ref, q_ref, kn_ref, kp_ref, v_ref, lse_ref, d_ref, do_ref, dq_hbm, dk_ref, dv_ref, dq_acc, sem):
        h, t = pl.program_id(0), pl.program_id(1)
        i, j = ii_ref[t], jj_ref[t]

        @pl.when(t == 0)
        def _():
            dq_acc[...] = jnp.zeros_like(dq_acc)

        def step(diag):
            k2 = jnp.concatenate([kn_ref[...], kp_ref[...]], axis=1)
            qb = q_ref[...]
            dob = do_ref[...].astype(BF16)
            s = _dot(qb, k2, _NT)
            p = jnp.exp(s - lse_ref[:, :1])
            if diag:
                p = jnp.where(_chunk_mask(s.shape), p, 0.0)
            ds = (p * (_dot(dob, v_ref[...], _NT) - d_ref[:, :1])).astype(BF16)
            rows = pl.ds(pl.multiple_of(i * T, T), T)
            dq_acc[rows, :] += _dot(ds, k2)
            if diag:
                dv_ref[...] = _dot(p, dob, _TN)
                dk_ref[...] = _dot(ds, qb, _TN)
            else:
                dv_ref[...] += _dot(p, dob, _TN)
                dk_ref[...] += _dot(ds, qb, _TN)

        @pl.when(i > j)
        def _():
            step(False)

        @pl.when(i == j)
        def _():
            step(True)

        @pl.when(t == last)
        def _():
            cp = pltpu.make_async_copy(dq_acc, dq_hbm.at[h], sem)
            cp.start()
            cp.wait()

    qmap = lambda h, t, ii_, jj_: (ii_[t], h)
    grid_spec = pltpu.PrefetchScalarGridSpec(
        num_scalar_prefetch=2, grid=(4, ii.shape[0]),
        in_specs=[pl.BlockSpec((T, 256), qmap), pl.BlockSpec((T, 128), lambda h, t, ii_, jj_: (jj_[t], h)),
                  pl.BlockSpec((T, 128), lambda h, t, ii_, jj_: (jj_[t], 0)), pl.BlockSpec((T, 128), lambda h, t, ii_, jj_: (jj_[t], 4 + h)),
                  pl.BlockSpec((T, 128), qmap), pl.BlockSpec((T, 128), qmap), pl.BlockSpec((T, 128), qmap)],
        out_specs=[pl.BlockSpec(memory_space=pl.ANY), pl.BlockSpec((T, 256), lambda h, t, ii_, jj_: (jj_[t], h)),
                   pl.BlockSpec((T, 128), lambda h, t, ii_, jj_: (jj_[t], h))],
        scratch_shapes=[pltpu.VMEM((S_, 256), F32), pltpu.SemaphoreType.DMA])
    return pl.pallas_call(
        body, grid_spec=grid_spec,
        out_shape=[jax.ShapeDtypeStruct((4, S_, 256), F32), jax.ShapeDtypeStruct((S_, 1024), F32), jax.ShapeDtypeStruct((S_, 512), F32)],
        compiler_params=_cp(("arbitrary", "arbitrary")), name=name,
    )(ii, jj, q, kv, kp, kv, lse, dsum, do)


CONV_T = 512


def _shift_down(x, halo, s):
    sh = pltpu.roll(x, s, axis=0)
    hr = pltpu.roll(halo, s, axis=0)
    r8 = lax.broadcasted_iota(jnp.int32, hr.shape, 0)
    top = jnp.where(r8 < s, hr, sh[:8])
    return jnp.concatenate([top, sh[8:]], axis=0)


def _shift_up(x, halo, s):
    n = x.shape[0]
    sh = pltpu.roll(x, n - s, axis=0)
    hr = pltpu.roll(halo, 8 - s, axis=0)
    r8 = lax.broadcasted_iota(jnp.int32, hr.shape, 0)
    bot = jnp.where(r8 >= 8 - s, hr, sh[n - 8:])
    return jnp.concatenate([sh[:n - 8], bot], axis=0)


def _conv_pre(x, halo, w, b):
    y = x * w[3:4] + b
    for j in range(3):
        y = y + _shift_down(x, halo, 3 - j) * w[j:j + 1]
    return y


def conv_fwd(src, cb0, ncb, w, b, *, name):
    S_ = src.shape[0]
    T = min(CONV_T, S_)
    nt = S_ // T

    def body(x_ref, h_ref, w_ref, b_ref, o_ref):
        i = pl.program_id(1)
        halo = jnp.where(i > 0, h_ref[...], 0.0)
        y = _conv_pre(x_ref[...], halo, w_ref[...], b_ref[...])
        o_ref[...] = y * jax.nn.sigmoid(y)

    return pl.pallas_call(
        body, grid=(ncb, nt),
        in_specs=[pl.BlockSpec((T, 512), lambda c, i: (i, cb0 + c)),
                  pl.BlockSpec((8, 512), lambda c, i: (jnp.maximum(i * (T // 8) - 1, 0), cb0 + c)),
                  pl.BlockSpec((4, 512), lambda c, i: (0, c)), pl.BlockSpec((1, 512), lambda c, i: (0, c))],
        out_specs=pl.BlockSpec((T, 512), lambda c, i: (i, c)),
        out_shape=jax.ShapeDtypeStruct((S_, 512 * ncb), F32), compiler_params=_cp(("parallel", "parallel")), name=name,
    )(src, src, w, b)


def conv_bwd_pre(src, cb0, ncb, w, b, dy, *, name):
    S_ = src.shape[0]
    T = min(CONV_T, S_)
    nt = S_ // T

    def body(x_ref, h_ref, w_ref, b_ref, dy_ref, dp_ref, dw_ref, db_ref):
        i = pl.program_id(1)
        halo = jnp.where(i > 0, h_ref[...], 0.0)
        x = x_ref[...]
        y = _conv_pre(x, halo, w_ref[...], b_ref[...])
        sg = jax.nn.sigmoid(y)
        dpre = dy_ref[...] * (sg * (1.0 + y * (1.0 - sg)))
        dp_ref[...] = dpre
        rows = [jnp.sum(dpre * _shift_down(x, halo, 3 - j), axis=0, keepdims=True) for j in range(3)]
        rows.append(jnp.sum(dpre * x, axis=0, keepdims=True))
        dw = jnp.concatenate(rows, axis=0)
        db = jnp.sum(dpre, axis=0, keepdims=True)

        @pl.when(i == 0)
        def _():
            dw_ref[...] = dw
            db_ref[...] = db

        @pl.when(i > 0)
        def _():
            dw_ref[...] += dw
            db_ref[...] += db

    return pl.pallas_call(
        body, grid=(ncb, nt),
        in_specs=[pl.BlockSpec((T, 512), lambda c, i: (i, cb0 + c)),
                  pl.BlockSpec((8, 512), lambda c, i: (jnp.maximum(i * (T // 8) - 1, 0), cb0 + c)),
                  pl.BlockSpec((4, 512), lambda c, i: (0, c)), pl.BlockSpec((1, 512), lambda c, i: (0, c)),
                  pl.BlockSpec((T, 512), lambda c, i: (i, c))],
        out_specs=[pl.BlockSpec((T, 512), lambda c, i: (i, c)), pl.BlockSpec((4, 512), lambda c, i: (0, c)),
                   pl.BlockSpec((1, 512), lambda c, i: (0, c))],
        out_shape=[jax.ShapeDtypeStruct((S_, 512 * ncb), F32), jax.ShapeDtypeStruct((4, 512 * ncb), F32),
                   jax.ShapeDtypeStruct((1, 512 * ncb), F32)],
        compiler_params=_cp(("parallel", "arbitrary")), name=name,
    )(src, src, w, b, dy)


def conv_bwd_x(dpre, w, *, name, out_dtype=F32):
    S_, C = dpre.shape
    T = min(CONV_T, S_)
    nt = S_ // T
    ncb = C // 512

    def body(d_ref, h_ref, w_ref, o_ref):
        i = pl.program_id(1)
        halo = jnp.where(i < nt - 1, h_ref[...], 0.0)
        d = d_ref[...]
        w_ = w_ref[...]
        y = d * w_[3:4]
        for j in range(3):
            y = y + _shift_up(d, halo, 3 - j) * w_[j:j + 1]
        o_ref[...] = y.astype(o_ref.dtype)

    return pl.pallas_call(
        body, grid=(ncb, nt),
        in_specs=[pl.BlockSpec((T, 512), lambda c, i: (i, c)),
                  pl.BlockSpec((8, 512), lambda c, i: (jnp.minimum((i + 1) * (T // 8), S_ // 8 - 1), c)),
                  pl.BlockSpec((4, 512), lambda c, i: (0, c))],
        out_specs=pl.BlockSpec((T, 512), lambda c, i: (i, c)),
        out_shape=jax.ShapeDtypeStruct((S_, C), out_dtype), compiler_params=_cp(("parallel", "parallel")), name=name,
    )(dpre, dpre, w)


def ffn_mid_fwd(h, w1, w3, *, name):
    S_, D = h.shape
    F = w1.shape[1]
    tm, tn = _pick(S_, (2048, 1024, 512, 256)), 256

    def body(h_ref, w1_ref, w3_ref, a_ref, u_ref, v_ref):
        hb = h_ref[...]
        u = _dot(hb, w1_ref[...])
        v = _dot(hb, w3_ref[...])
        a_ref[...] = (u * jax.nn.sigmoid(u) * v).astype(a_ref.dtype)
        u_ref[...] = u.astype(u_ref.dtype)
        v_ref[...] = v.astype(v_ref.dtype)

    o = pl.BlockSpec((tm, tn), lambda i, j: (i, j))
    return pl.pallas_call(
        body, grid=(S_ // tm, F // tn),
        in_specs=[pl.BlockSpec((tm, D), lambda i, j: (i, 0)), pl.BlockSpec((D, tn), lambda i, j: (0, j)),
                  pl.BlockSpec((D, tn), lambda i, j: (0, j))],
        out_specs=[o, o, o], out_shape=[jax.ShapeDtypeStruct((S_, F), BF16)] * 3,
        compiler_params=_cp(("parallel", "parallel")), name=name,
    )(h, w1, w3)


def ffn_mid_bwd(u, v, dy, w2, *, name):
    S_, F = u.shape
    D = dy.shape[1]
    tm, tn = _pick(S_, (2048, 1024, 512, 256)), 256

    def body(u_ref, v_ref, dy_ref, w2_ref, du_ref, dv_ref):
        u_ = u_ref[...].astype(F32)
        v_ = v_ref[...].astype(F32)
        da = _dot(dy_ref[...], w2_ref[...], _NT)
        sg = jax.nn.sigmoid(u_)
        dv_ref[...] = (da * (u_ * sg)).astype(dv_ref.dtype)
        du_ref[...] = (da * v_ * (sg * (1.0 + u_ * (1.0 - sg)))).astype(du_ref.dtype)

    o = pl.BlockSpec((tm, tn), lambda i, j: (i, j))
    return pl.pallas_call(
        body, grid=(S_ // tm, F // tn),
        in_specs=[o, o, pl.BlockSpec((tm, D), lambda i, j: (i, 0)), pl.BlockSpec((tn, D), lambda i, j: (j, 0))],
        out_specs=[o, o], out_shape=[jax.ShapeDtypeStruct((S_, F), BF16)] * 2,
        compiler_params=_cp(("parallel", "parallel")), name=name,
    )(u, v, dy, w2)


MESH = pl.DeviceIdType.MESH
ANY = pl.BlockSpec(memory_space=pl.ANY)


def allgather8(x_shard, *, name):
    m_per, n = x_shard.shape

    def body(x_ref, out_ref, send_sems, recv_sems, local_sem):
        x, y, c = lax.axis_index("x"), lax.axis_index("y"), lax.axis_index("c")
        me, sibling = (x, y, c), (x, y, 1 - c)
        chips = [(1 - x, y), (x, 1 - y), (1 - x, 1 - y)]

        def rows(px, py, pc):
            return out_ref.at[pl.ds((4 * px + 2 * py + pc) * m_per, m_per), :]

        def copy(k, block, to, src=None):
            return pltpu.make_async_remote_copy(
                src_ref=rows(*block) if src is None else src, dst_ref=rows(*block),
                send_sem=send_sems.at[k], recv_sem=recv_sems.at[k], device_id=to, device_id_type=MESH)

        mine = pltpu.make_async_copy(x_ref, rows(*me), local_sem)
        mine.start()
        first = [copy(0, me, sibling, src=x_ref)]
        first += [copy(1 + j, me, (*chip, c), src=x_ref) for j, chip in enumerate(chips)]
        for cp in first:
            cp.start()
        passed = [copy(4 + j, (*chip, c), sibling) for j, chip in enumerate(chips)]
        for j, chip in enumerate(chips):
            copy(1 + j, (*chip, c), me).wait_recv()
            passed[j].start()
        copy(0, sibling, me).wait_recv()
        for j, chip in enumerate(chips):
            copy(4 + j, (*chip, 1 - c), me).wait_recv()
        for cp in first + passed:
            cp.wait_send()
        mine.wait()

    return pl.pallas_call(
        body, out_shape=jax.ShapeDtypeStruct((8 * m_per, n), x_shard.dtype),
        in_specs=[pl.BlockSpec(memory_space=pltpu.VMEM)], out_specs=pl.BlockSpec(memory_space=pltpu.VMEM),
        scratch_shapes=[pltpu.SemaphoreType.DMA((7,)), pltpu.SemaphoreType.DMA((7,)), pltpu.SemaphoreType.DMA],
        name=name,
    )(x_shard)


def _chip_peers():
    x, y, c = lax.axis_index("x"), lax.axis_index("y"), lax.axis_index("c")
    return x, y, c, [(1 - x, y), (x, 1 - y), (1 - x, 1 - y)]


def allgather_chips(x_shard, *, name):
    r, cdim = x_shard.shape

    def body(x_ref, out_ref, send_sems, recv_sems, local_sem):
        x, y, c, chips = _chip_peers()
        me = 2 * x + y
        mine = pltpu.make_async_copy(x_ref, out_ref.at[me], local_sem)
        mine.start()
        sends = []
        for k, (px, py) in enumerate(chips):
            cp = pltpu.make_async_remote_copy(src_ref=x_ref, dst_ref=out_ref.at[me], send_sem=send_sems.at[k],
                                              recv_sem=recv_sems.at[k], device_id=(px, py, c), device_id_type=MESH)
            cp.start()
            sends.append(cp)
        for k, (px, py) in enumerate(chips):
            pltpu.make_async_remote_copy(src_ref=x_ref, dst_ref=out_ref.at[2 * px + py], send_sem=send_sems.at[k],
                                         recv_sem=recv_sems.at[k], device_id=(px, py, c), device_id_type=MESH).wait_recv()
        for cp in sends:
            cp.wait_send()
        mine.wait()

    return pl.pallas_call(
        body, out_shape=jax.ShapeDtypeStruct((4, r, cdim), x_shard.dtype), in_specs=[ANY], out_specs=ANY,
        scratch_shapes=[pltpu.SemaphoreType.DMA((3,)), pltpu.SemaphoreType.DMA((3,)), pltpu.SemaphoreType.DMA],
        name=name,
    )(x_shard)


def allgather_chips_2level(x_shard, *, name):
    r, cdim = x_shard.shape
    half = r // 2

    def body(x_ref, out_ref, send_sems, recv_sems, local_sem):
        x, y, c, chips = _chip_peers()
        me = 2 * x + y
        mine_rows = pl.ds(c * half, half)
        other_rows = pl.ds((1 - c) * half, half)
        mine = pltpu.make_async_copy(x_ref, out_ref.at[me], local_sem)
        mine.start()

        def copy(k, slot, rows, to, src=None):
            dst = out_ref.at[slot, rows, :]
            return pltpu.make_async_remote_copy(src_ref=dst if src is None else src, dst_ref=dst, send_sem=send_sems.at[k],
                                                recv_sem=recv_sems.at[k], device_id=to, device_id_type=MESH)

        first = [copy(k, me, mine_rows, (px, py, c), src=x_ref.at[mine_rows, :]) for k, (px, py) in enumerate(chips)]
        for cp in first:
            cp.start()
        passed = [copy(3 + k, 2 * px + py, mine_rows, (x, y, 1 - c)) for k, (px, py) in enumerate(chips)]
        for k, (px, py) in enumerate(chips):
            copy(k, 2 * px + py, mine_rows, (px, py, c)).wait_recv()
            passed[k].start()
        for k, (px, py) in enumerate(chips):
            copy(3 + k, 2 * px + py, other_rows, (x, y, 1 - c)).wait_recv()
        for cp in first + passed:
            cp.wait_send()
        mine.wait()

    return pl.pallas_call(
        body, out_shape=jax.ShapeDtypeStruct((4, r, cdim), x_shard.dtype), in_specs=[ANY], out_specs=ANY,
        scratch_shapes=[pltpu.SemaphoreType.DMA((6,)), pltpu.SemaphoreType.DMA((6,)), pltpu.SemaphoreType.DMA],
        name=name,
    )(x_shard)


def exchange_chips(g, *, name):
    _, r, cdim = g.shape

    def body(g_ref, out_ref, send_sems, recv_sems, local_sem):
        x, y, c, chips = _chip_peers()
        me = 2 * x + y
        mine = pltpu.make_async_copy(g_ref.at[me], out_ref.at[me], local_sem)
        mine.start()
        sends = []
        for k, (px, py) in enumerate(chips):
            cp = pltpu.make_async_remote_copy(src_ref=g_ref.at[2 * px + py], dst_ref=out_ref.at[me], send_sem=send_sems.at[k],
                                              recv_sem=recv_sems.at[k], device_id=(px, py, c), device_id_type=MESH)
            cp.start()
            sends.append(cp)
        for k, (px, py) in enumerate(chips):
            pltpu.make_async_remote_copy(src_ref=g_ref.at[me], dst_ref=out_ref.at[2 * px + py], send_sem=send_sems.at[k],
                                         recv_sem=recv_sems.at[k], device_id=(px, py, c), device_id_type=MESH).wait_recv()
        for cp in sends:
            cp.wait_send()
        mine.wait()

    return pl.pallas_call(
        body, out_shape=jax.ShapeDtypeStruct(g.shape, g.dtype), in_specs=[ANY], out_specs=ANY,
        scratch_shapes=[pltpu.SemaphoreType.DMA((3,)), pltpu.SemaphoreType.DMA((3,)), pltpu.SemaphoreType.DMA],
        name=name,
    )(g)


def swap_sibling(p, *, name):
    def body(p_ref, out_ref, send_sem, recv_sem):
        x, y, c = lax.axis_index("x"), lax.axis_index("y"), lax.axis_index("c")
        cp = pltpu.make_async_remote_copy(src_ref=p_ref, dst_ref=out_ref, send_sem=send_sem, recv_sem=recv_sem,
                                          device_id=(x, y, 1 - c), device_id_type=MESH)
        cp.start()
        cp.wait()

    return pl.pallas_call(
        body, out_shape=jax.ShapeDtypeStruct(p.shape, p.dtype), in_specs=[ANY], out_specs=ANY,
        scratch_shapes=[pltpu.SemaphoreType.DMA, pltpu.SemaphoreType.DMA], name=name,
    )(p)


def swap_other_half(g, *, name):
    n, r, cdim = g.shape
    half = r // 2

    def body(g_ref, out_ref, send_sem, recv_sem):
        x, y, c = lax.axis_index("x"), lax.axis_index("y"), lax.axis_index("c")
        cp = pltpu.make_async_remote_copy(src_ref=g_ref.at[:, pl.ds((1 - c) * half, half), :], dst_ref=out_ref, send_sem=send_sem,
                                          recv_sem=recv_sem, device_id=(x, y, 1 - c), device_id_type=MESH)
        cp.start()
        cp.wait()

    return pl.pallas_call(
        body, out_shape=jax.ShapeDtypeStruct((n, half, cdim), g.dtype), in_specs=[ANY], out_specs=ANY,
        scratch_shapes=[pltpu.SemaphoreType.DMA, pltpu.SemaphoreType.DMA], name=name,
    )(g)


def add_pairs(a, b, *, name, out_dtype):
    n, rows, cdim = a.shape
    t = _pick(rows, (256, 128, 64, 32, 16))

    def body(a_ref, b_ref, o_ref):
        o_ref[...] = (a_ref[...].astype(F32) + b_ref[...].astype(F32)).astype(o_ref.dtype)

    spec = pl.BlockSpec((n, t, cdim), lambda i: (0, i, 0))
    return pl.pallas_call(
        body, grid=(rows // t,), in_specs=[spec, spec], out_specs=spec, out_shape=jax.ShapeDtypeStruct(a.shape, out_dtype),
        compiler_params=_cp(("parallel",)), name=name,
    )(a, b)


def sum_slots(r, *, name):
    n, rows, cdim = r.shape
    t = _pick(rows, (256, 128, 64, 32, 16, 8))

    def body(r_ref, o_ref):
        acc = r_ref[0].astype(F32)
        for s in range(1, n):
            acc = acc + r_ref[s].astype(F32)
        o_ref[...] = acc

    return pl.pallas_call(
        body, grid=(rows // t,), in_specs=[pl.BlockSpec((n, t, cdim), lambda i: (0, i, 0))],
        out_specs=pl.BlockSpec((t, cdim), lambda i: (i, 0)), out_shape=jax.ShapeDtypeStruct((rows, cdim), F32),
        compiler_params=_cp(("parallel",)), name=name,
    )(r)


def _rms(x, g):
    return x * lax.rsqrt(jnp.mean(x * x, axis=-1, keepdims=True) + NORM_EPS) * g


def _adaln(x, g, shift, scale):
    return _rms(x, g) * (1.0 + scale) + shift


def _silu(x):
    return x * jax.nn.sigmoid(x)


def _gdn_gate(o, z, g):
    return jnp.concatenate([_rms(o[:, 128 * h:128 * (h + 1)], g) * _silu(z[:, 128 * h:128 * (h + 1)]) for h in range(4)], axis=1)


def _ssd_gate(y, z0, z1, z2, z3, g):
    outs = []
    for k, z in enumerate((z0, z1, z2, z3)):
        t = y[:, 512 * k:512 * (k + 1)] * _silu(z)
        outs.append(t * lax.rsqrt(jnp.mean(t * t, axis=-1, keepdims=True) + NORM_EPS))
    return jnp.concatenate(outs, axis=1) * g


def _rope(x, cos, sin, rot):
    return x * cos + _dot(x, rot, hi=True) * sin


def _rope_q(q, cos, sin, rot):
    parts = []
    for h in range(4):
        parts += [q[:, 256 * h:256 * h + 128], _rope(q[:, 256 * h + 128:256 * (h + 1)], cos, sin, rot)]
    return jnp.concatenate(parts, axis=1) * ATT_SCALE


def _rope_t(d, cos, sin, rot):
    return d * cos + _dot(d * sin, rot, _NT, hi=True)


def _vjp_rows(fn, n_rows, n_pars, out_dtypes, rows, cts, pars, *, name, tile=512, extra=None):
    nct = len(cts)

    def bwd(*a):
        r, c, e, p = a[:n_rows], a[n_rows:n_rows + nct], a[n_rows + nct:len(a) - n_pars], a[len(a) - n_pars:]
        out, vjp = jax.vjp(fn, *[t.astype(F32) for t in r], *p)
        ct = tuple(t.astype(F32) for t in c)
        grads = vjp(ct[0] if not isinstance(out, tuple) else ct)
        drows = list(grads[:n_rows])
        if e:
            drows[0] = drows[0] + e[0]
        return (*drows, *grads[n_rows:])

    return rowmap(bwd, list(rows) + list(cts) + ([extra] if extra is not None else []), list(pars), out_dtypes,
                  name=name, tile=tile, n_reduce=n_pars)


ADAM_LR, ADAM_B1, ADAM_B2, ADAM_EPS, ADAM_WD, ADAM_STEP = 0.001, 0.9, 0.999, 1e-08, 0.01, 10


def _adam_math(w, g, m, v):
    m = ADAM_B1 * m + (1.0 - ADAM_B1) * g
    v = ADAM_B2 * v + (1.0 - ADAM_B2) * (g * g)
    m_hat = m / (1.0 - ADAM_B1 ** ADAM_STEP)
    v_hat = v / (1.0 - ADAM_B2 ** ADAM_STEP)
    delta = -ADAM_LR * (m_hat / (jnp.sqrt(v_hat) + ADAM_EPS) + ADAM_WD * w)
    return delta, m, v


def adamw(w, gs, m, v, *, name):
    shape = w.shape
    last = shape[-1]
    to2 = lambda a: a.reshape(-1, last)
    rows = w.size // last
    tile = _pick(rows, (256, 128, 64, 32, 16, 8))
    ng = len(gs)

    def fn(w_, *rest):
        g = rest[0]
        for t in rest[1:ng]:
            g = g + t
        m_, v_ = rest[ng], rest[ng + 1]
        return (g, *_adam_math(w_, g, m_, v_))

    outs = rowmap(fn, [to2(w)] + [to2(g) for g in gs] + [to2(m), to2(v)], [], (F32,) * 4, name=name, tile=tile)
    return tuple(o.reshape(shape) for o in outs)


PACK_W = 1024
BIG = (
    ("ffn_w1", (4, 2, 1024, 704), 3), ("ffn_w3", (4, 2, 1024, 704), 3), ("ffn_w2", (4, 2, 704, 1024), 2),
    ("ev_w_in", (2, 1024, 690), 2), ("mla_w_uq", (2, 96, 4, 192), 1), ("mla_w_ukv", (2, 64, 4, 256), 1),
    ("ev_w_out", (2, 256, 1024), 1), ("ssd_w_in", (2, 1024, 1288), 2), ("ssd_w_out", (2, 512, 1024), 1))


def _seg_rows(shape):
    n = math.prod(shape)
    return -(-n // (16 * PACK_W)) * 16


PACK_ROWS = -(-sum(_seg_rows(sh) for _, sh, _ in BIG) // 512) * 512


def _pack(shards, dtype):
    parts = []
    for (_, shape, _), a in zip(BIG, shards):
        flat = a.reshape(-1).astype(dtype)
        pad = _seg_rows(shape) * PACK_W - flat.shape[0]
        parts.append(jnp.pad(flat, (0, pad)) if pad else flat)
    tail = PACK_ROWS - sum(_seg_rows(sh) for _, sh, _ in BIG)
    if tail:
        parts.append(jnp.zeros((tail * PACK_W,), dtype))
    return jnp.concatenate(parts).reshape(-1, PACK_W)


def _pack_by_owner(fulls):
    cols = []
    for (_, shape, ax), f in zip(BIG, fulls):
        blk = jnp.stack([lax.slice_in_dim(f, s * shape[ax], (s + 1) * shape[ax], axis=ax).reshape(-1).astype(BF16) for s in range(4)])
        pad = _seg_rows(shape) * PACK_W - blk.shape[1]
        cols.append((jnp.pad(blk, ((0, 0), (0, pad))) if pad else blk).reshape(4, -1, PACK_W))
    tail = PACK_ROWS - sum(_seg_rows(sh) for _, sh, _ in BIG)
    if tail:
        cols.append(jnp.zeros((4, tail, PACK_W), BF16))
    return jnp.concatenate(cols, axis=1)


def _unpack(buf):
    out, r0 = [], 0
    for _, shape, _ in BIG:
        n = math.prod(shape)
        out.append(buf[r0:r0 + _seg_rows(shape)].reshape(-1)[:n].reshape(shape))
        r0 += _seg_rows(shape)
    return out


SMALL_SHARDED = (
    ("norm_g", (4, 3, 256), 2), ("gdn_conv_w", (2, 4, 384), 2), ("ssd_conv_w", (2, 4, 768), 2),
    ("ssd_conv_b", (2, 768), 1), ("ssd_norm_g", (2, 512), 1))


def _flat_pack(arrs, width, row_mult):
    flat = jnp.concatenate([a.reshape(-1).astype(F32) for a in arrs])
    n = flat.shape[0]
    tot = -(-n // (width * row_mult)) * width * row_mult
    return jnp.pad(flat, (0, tot - n)).reshape(-1, width)


def _flat_unpack(buf, shapes):
    flat = buf.reshape(-1)
    out, o = [], 0
    for s in shapes:
        n = math.prod(s)
        out.append(flat[o:o + n].reshape(s))
        o += n
    return out


def _rep(v, n):
    return jnp.repeat(v, n, axis=-1)


def kernel(x, c, positions, ada_w, ada_b, norm_g, ffn_w1, ffn_w3, ffn_w2, ev_w_in, gdn_conv_w, gdn_A_log, gdn_dt_bias, gdn_norm_g, mla_q_norm_g, mla_w_uq, mla_kv_norm_g, mla_w_ukv, ev_w_out, ssd_w_in, ssd_conv_w, ssd_conv_b, ssd_A_log, ssd_dt_bias, ssd_D, ssd_norm_g, ssd_w_out, final_g, loss_target, m_ada_w, m_ada_b, m_norm_g, m_ffn_w1, m_ffn_w3, m_ffn_w2, m_ev_w_in, m_gdn_conv_w, m_gdn_A_log, m_gdn_dt_bias, m_gdn_norm_g, m_mla_q_norm_g, m_mla_w_uq, m_mla_kv_norm_g, m_mla_w_ukv, m_ev_w_out, m_ssd_w_in, m_ssd_conv_w, m_ssd_conv_b, m_ssd_A_log, m_ssd_dt_bias, m_ssd_D, m_ssd_norm_g, m_ssd_w_out, m_final_g, v_ada_w, v_ada_b, v_norm_g, v_ffn_w1, v_ffn_w3, v_ffn_w2, v_ev_w_in, v_gdn_conv_w, v_gdn_A_log, v_gdn_dt_bias, v_gdn_norm_g, v_mla_q_norm_g, v_mla_w_uq, v_mla_kv_norm_g, v_mla_w_ukv, v_ev_w_out, v_ssd_w_in, v_ssd_conv_w, v_ssd_conv_b, v_ssd_A_log, v_ssd_dt_bias, v_ssd_D, v_ssd_norm_g, v_ssd_w_out, v_final_g):
    P = dict(ada_w=ada_w, ada_b=ada_b, norm_g=norm_g, ffn_w1=ffn_w1, ffn_w3=ffn_w3, ffn_w2=ffn_w2, ev_w_in=ev_w_in, gdn_conv_w=gdn_conv_w, gdn_A_log=gdn_A_log, gdn_dt_bias=gdn_dt_bias, gdn_norm_g=gdn_norm_g, mla_q_norm_g=mla_q_norm_g, mla_w_uq=mla_w_uq, mla_kv_norm_g=mla_kv_norm_g, mla_w_ukv=mla_w_ukv, ev_w_out=ev_w_out, ssd_w_in=ssd_w_in, ssd_conv_w=ssd_conv_w, ssd_conv_b=ssd_conv_b, ssd_A_log=ssd_A_log, ssd_dt_bias=ssd_dt_bias, ssd_D=ssd_D, ssd_norm_g=ssd_norm_g, ssd_w_out=ssd_w_out, final_g=final_g)
    M1 = dict(ada_w=m_ada_w, ada_b=m_ada_b, norm_g=m_norm_g, ffn_w1=m_ffn_w1, ffn_w3=m_ffn_w3, ffn_w2=m_ffn_w2, ev_w_in=m_ev_w_in, gdn_conv_w=m_gdn_conv_w, gdn_A_log=m_gdn_A_log, gdn_dt_bias=m_gdn_dt_bias, gdn_norm_g=m_gdn_norm_g, mla_q_norm_g=m_mla_q_norm_g, mla_w_uq=m_mla_w_uq, mla_kv_norm_g=m_mla_kv_norm_g, mla_w_ukv=m_mla_w_ukv, ev_w_out=m_ev_w_out, ssd_w_in=m_ssd_w_in, ssd_conv_w=m_ssd_conv_w, ssd_conv_b=m_ssd_conv_b, ssd_A_log=m_ssd_A_log, ssd_dt_bias=m_ssd_dt_bias, ssd_D=m_ssd_D, ssd_norm_g=m_ssd_norm_g, ssd_w_out=m_ssd_w_out, final_g=m_final_g)
    M2 = dict(ada_w=v_ada_w, ada_b=v_ada_b, norm_g=v_norm_g, ffn_w1=v_ffn_w1, ffn_w3=v_ffn_w3, ffn_w2=v_ffn_w2, ev_w_in=v_ev_w_in, gdn_conv_w=v_gdn_conv_w, gdn_A_log=v_gdn_A_log, gdn_dt_bias=v_gdn_dt_bias, gdn_norm_g=v_gdn_norm_g, mla_q_norm_g=v_mla_q_norm_g, mla_w_uq=v_mla_w_uq, mla_kv_norm_g=v_mla_kv_norm_g, mla_w_ukv=v_mla_w_ukv, ev_w_out=v_ev_w_out, ssd_w_in=v_ssd_w_in, ssd_conv_w=v_ssd_conv_w, ssd_conv_b=v_ssd_conv_b, ssd_A_log=v_ssd_A_log, ssd_dt_bias=v_ssd_dt_bias, ssd_D=v_ssd_D, ssd_norm_g=v_ssd_norm_g, ssd_w_out=v_ssd_w_out, final_g=v_final_g)
    names = list(P)
    xi, yi, ci = lax.axis_index("x"), lax.axis_index("y"), lax.axis_index("c")
    chip = 2 * xi + yi
    bidx = 4 * xi + 2 * yi + ci
    xa = x[0]
    S_, D = xa.shape
    tgt = loss_target[0]
    depth = ffn_w1.shape[0]

    wg = allgather_chips_2level(_pack([P[n] for n, _, _ in BIG], BF16), name="gather_weights")
    per_chip = [_unpack(wg[s]) for s in range(4)]
    W = {n: jnp.concatenate([per_chip[s][k] for s in range(4)], axis=ax) for k, (n, _, ax) in enumerate(BIG)}
    sg = allgather_chips(_flat_pack([P[n] for n, _, _ in SMALL_SHARDED], 1024, 16), name="gather_small")
    per_chip_s = [_flat_unpack(sg[s], [sh for _, sh, _ in SMALL_SHARDED]) for s in range(4)]
    Wf = {n: jnp.concatenate([per_chip_s[s][k] for s in range(4)], axis=ax) for k, (n, _, ax) in enumerate(SMALL_SHARDED)}

    c_all = allgather8(jnp.pad(c, ((0, 7), (0, 0))), name="gather_c").reshape(8, 8, D)[:, 0]
    c_act, = rowmap(lambda t: (_silu(t),), [jnp.pad(c_all, ((0, 8), (0, 0)))], [], (F32,), name="c_act", tile=16)
    ncol = ada_w.shape[2]
    ada_b_loc = lax.dynamic_slice(ada_b, (0, chip * ncol), (depth, ncol))
    mod_loc = [mm((c_act, ada_w[l]), name=f"mod_{l}", epi=lambda acc, b: (acc + b,), epi_pars=(ada_b_loc[l][None],),
                  epi_out_dtypes=(F32,), tm=16, tn=256)[0][:8] for l in range(depth)]
    mod_g = allgather8(jnp.stack(mod_loc).reshape(-1, 1024), name="gather_mod").reshape(8, depth, 8, ncol)
    mod_b = lax.dynamic_index_in_dim(mod_g[0::2], bidx, axis=2, keepdims=False)
    mod = jnp.transpose(mod_b, (1, 0, 2)).reshape(depth, 3, 3, D)

    def ev_ext(w):
        z = lambda n: jnp.zeros((w.shape[0], n), w.dtype)
        return jnp.concatenate([w[:, :2048], _rep(w[:, 2048:2052], 128), _rep(w[:, 2052:2056], 128), w[:, 2056:2440], z(128),
                                w[:, 2440:2696], w[:, 2696:2760], z(192)], axis=1)

    def ev_ext_t(dw):
        return jnp.concatenate([dw[:, :2048], dw[:, 2048:2560].reshape(-1, 4, 128).sum(-1), dw[:, 2560:3072].reshape(-1, 4, 128).sum(-1),
                                dw[:, 3072:3456], dw[:, 3584:3840], dw[:, 3840:3904]], axis=1)

    def od_ext(w):
        return jnp.concatenate([w[:, 2048:5120], w[:, :2048], _rep(w[:, 5120:5152], 64)], axis=1)

    def od_ext_t(dw):
        return jnp.concatenate([dw[:, 3072:5120], dw[:, :3072], dw[:, 5120:].reshape(-1, 32, 64).sum(-1)], axis=1)

    def wq_ext(w):
        return jnp.pad(w, ((0, 0), (0, 0), (0, 64))).reshape(384, 1024)

    def wq_ext_t(dw):
        return dw.reshape(384, 4, 256)[:, :, :192]

    def wkv_ext(w):
        return jnp.concatenate([w[:, :, :128].reshape(256, 512), w[:, :, 128:].reshape(256, 512)], axis=1)

    def wkv_ext_t(dw):
        return jnp.concatenate([dw[:, :512].reshape(256, 4, 128), dw[:, 512:].reshape(256, 4, 128)], axis=2)

    half = 32
    inv_freq = 10000.0 ** (-jnp.arange(half, dtype=F32) / half)
    ang = positions[0].astype(F32)[:, None] * inv_freq
    zpad = jnp.zeros((S_, 64), F32)
    cos_t = jnp.concatenate([jnp.cos(ang), jnp.cos(ang), zpad], axis=1)
    sin_t = jnp.concatenate([jnp.sin(ang), jnp.sin(ang), zpad], axis=1)
    ii = jnp.arange(128)
    rot = (jnp.where((ii[:, None] < 32) & (ii[None, :] == ii[:, None] + 32), 1.0, 0.0)
           - jnp.where((ii[:, None] >= 32) & (ii[:, None] < 64) & (ii[None, :] == ii[:, None] - 32), 1.0, 0.0)).astype(F32)

    grads = {}
    dmod = [[[None] * 3 for _ in range(3)] for _ in range(depth)]
    dnorm_g = [[None] * 3 for _ in range(depth)]

    def acc(name, idx, val):
        grads.setdefault(name, {})[idx] = val

    def ffn_sub(xin, l, k, j):
        g, (shift, scale, gate) = Wf["norm_g"][l, k][None], [mod[l, k, t][None] for t in range(3)]
        w1, w3, w2 = W["ffn_w1"][l, j], W["ffn_w3"][l, j], W["ffn_w2"][l, j]
        tag = f"l{l}f{j}"
        h, = rowmap(lambda *a: (_adaln(*a),), [xin], [g, shift, scale], (BF16,), name=f"adaln_{tag}")
        a, u16, v16 = ffn_mid_fwd(h, w1, w3, name=f"ffn_mid_{tag}")
        xn, y = mm((a, w2), name=f"ffn_out_{tag}", epi=lambda acc_, xr, gt: (xr + 0.5 * gt * acc_, acc_), epi_rows=(xin,),
                   epi_pars=(gate,), epi_out_dtypes=(F32, F32))

        def bwd(dxn):
            dy, dgate = rowmap(lambda d, y_, gt: ((0.5 * gt) * d, jnp.sum(0.5 * y_ * d, axis=0, keepdims=True)), [dxn, y], [gate],
                               (BF16, F32), name=f"dres_{tag}", n_reduce=1)
            du, dv = ffn_mid_bwd(u16, v16, dy, w2, name=f"ffn_midb_{tag}")
            dh = mm([(du, w1), (dv, w3)], tb=True, name=f"ffn_dh_{tag}")
            acc("ffn_w1", (l, j), mm((h, du), ta=True, name=f"ffn_dw1_{tag}"))
            acc("ffn_w3", (l, j), mm((h, dv), ta=True, name=f"ffn_dw3_{tag}"))
            acc("ffn_w2", (l, j), mm((a, dy), ta=True, name=f"ffn_dw2_{tag}"))
            dx, dg, dsh, dsc = _vjp_rows(_adaln, 1, 3, (F32,), [xin], [dh], [g, shift, scale], name=f"adalnb_{tag}", extra=dxn)
            dnorm_g[l][k] = dg[0]
            dmod[l][k] = [dsh[0], dsc[0], dgate[0]]
            return dx

        return xn, bwd

    def mixer_tail(xin, l, tag, dh, dxn, g, shift, scale, dgate):
        dx, dg, dsh, dsc = _vjp_rows(_adaln, 1, 3, (F32,), [xin], [dh], [g, shift, scale], name=f"adalnb_{tag}", extra=dxn)
        dnorm_g[l][1] = dg[0]
        dmod[l][1] = [dsh[0], dsc[0], dgate[0]]
        return dx

    def even_sub(xin, l):
        e = l // 2
        tag = f"l{l}m"
        g, (shift, scale, gate) = Wf["norm_g"][l, 1][None], [mod[l, 1, t][None] for t in range(3)]
        wext, wq, wkv, wout = ev_ext(W["ev_w_in"][e]), wq_ext(W["mla_w_uq"][e]), wkv_ext(W["mla_w_ukv"][e]), W["ev_w_out"][e]
        conv_w, zb = Wf["gdn_conv_w"][e], jnp.zeros((1, 1536), F32)
        alog_e, dtb_e = _rep(gdn_A_log[e], 128)[None], _rep(gdn_dt_bias[e], 128)[None]
        gg, qg, kvg = gdn_norm_g[e][None], mla_q_norm_g[e][None], mla_kv_norm_g[e][None]
        h, = rowmap(lambda *a: (_adaln(*a),), [xin], [g, shift, scale], (BF16,), name=f"adaln_{tag}")
        proj = mm((h, wext), name=f"ev_in_{tag}")
        qkvc = conv_fwd(proj, 0, 3, conv_w, zb, name=f"gdn_conv_{tag}")
        o_g, hist = gdn_fwd(qkvc, proj, 4, 5, alog_e, dtb_e, name=f"gdn_{tag}")
        o_a, = rowmap(lambda o, z, g_: (_gdn_gate(o, z, g_),), [o_g, (proj, 512, 3)], [gg], (BF16,), name=f"gdn_gate_{tag}")
        cqn, = rowmap(lambda t, g_: (_rms(t, g_),), [(proj, 384, 8)], [qg], (BF16,), name=f"q_norm_{tag}")
        ckvn, = rowmap(lambda t, g_: (_rms(t, g_),), [(proj, 256, 14)], [kvg], (BF16,), name=f"kv_norm_{tag}")
        q0 = mm((cqn, wq), name=f"q_up_{tag}")
        kv = mm((ckvn, wkv), name=f"kv_up_{tag}", out_dtype=BF16)
        q, = rowmap(lambda t, cs, sn, r: (_rope_q(t, cs, sn, r),), [q0, cos_t, sin_t], [rot], (BF16,), name=f"rope_q_{tag}")
        kp, = rowmap(lambda t, cs, sn, r: (_rope(t, cs, sn, r),), [(proj, 128, 30), cos_t, sin_t], [rot], (BF16,), name=f"rope_k_{tag}")
        o_b, lse = att_fwd(q, kv, kp, name=f"att_{tag}")
        xn, y = mm([(o_a, wout[:512]), (o_b, wout[512:])], name=f"ev_out_{tag}", epi=lambda acc_, xr, gt: (xr + gt * acc_, acc_),
                   epi_rows=(xin,), epi_pars=(gate,), epi_out_dtypes=(F32, F32))

        def bwd(dxn):
            dy, dgate = rowmap(lambda d, y_, gt: (gt * d, jnp.sum(y_ * d, axis=0, keepdims=True)), [dxn, y], [gate],
                               (BF16, F32), name=f"dres_{tag}", n_reduce=1)
            do_a = mm((dy, wout[:512]), tb=True, name=f"ev_doa_{tag}")
            do_b = mm((dy, wout[512:]), tb=True, name=f"ev_dob_{tag}")
            acc("ev_w_out", e, jnp.concatenate([mm((o_a, dy), ta=True, name=f"ev_dwoa_{tag}"), mm((o_b, dy), ta=True, name=f"ev_dwob_{tag}")], axis=0))
            dsum, = rowmap(lambda d, o_: (jnp.concatenate([jnp.broadcast_to(jnp.sum(d[:, 128 * hh:128 * (hh + 1)] * o_[:, 128 * hh:128 * (hh + 1)],
                                                                                        axis=-1, keepdims=True), (d.shape[0], 128))
                                                  for hh in range(4)], axis=1),), [do_b, o_b], [], (F32,), name=f"att_dsum_{tag}")
            dq4, dk2, dv = att_bwd(q, kv, kp, lse, dsum, do_b, name=f"att_bwd_{tag}")

            def rope_qb(d0, d1, d2, d3, cs, sn, r):
                parts = []
                for d in (d0, d1, d2, d3):
                    parts += [d[:, :128], _rope_t(d[:, 128:], cs, sn, r)]
                return (jnp.concatenate(parts, axis=1) * ATT_SCALE,)

            dq0, = rowmap(rope_qb, [dq4[0], dq4[1], dq4[2], dq4[3], cos_t, sin_t], [rot], (BF16,), name=f"rope_qb_{tag}")

            def rope_kb(d, cs, sn, r):
                dkp = d[:, 128:256] + d[:, 384:512] + d[:, 640:768] + d[:, 896:1024]
                return jnp.concatenate([d[:, 256 * hh:256 * hh + 128] for hh in range(4)], axis=1), _rope_t(dkp, cs, sn, r)

            dkn, dkr = rowmap(rope_kb, [dk2, cos_t, sin_t], [rot], (BF16, BF16), name=f"rope_kb_{tag}")
            dcqn = mm((dq0, wq), tb=True, name=f"q_upb_{tag}")
            acc("mla_w_uq", e, wq_ext_t(mm((cqn, dq0), ta=True, name=f"q_dw_{tag}")))
            dckvn = mm([(dkn, wkv[:, :512]), (dv, wkv[:, 512:])], tb=True, name=f"kv_upb_{tag}")
            acc("mla_w_ukv", e, wkv_ext_t(jnp.concatenate([mm((ckvn, dkn), ta=True, name=f"kv_dwk_{tag}"), mm((ckvn, dv), ta=True, name=f"kv_dwv_{tag}")], axis=1)))
            dcq, dqg = _vjp_rows(_rms, 1, 1, (BF16,), [(proj, 384, 8)], [dcqn], [qg], name=f"q_normb_{tag}")
            dckv, dkvg = _vjp_rows(_rms, 1, 1, (BF16,), [(proj, 256, 14)], [dckvn], [kvg], name=f"kv_normb_{tag}")
            acc("mla_q_norm_g", e, dqg[0])
            acc("mla_kv_norm_g", e, dkvg[0])
            do_g, dz, dgg = _vjp_rows(_gdn_gate, 2, 1, (F32, BF16), [o_g, (proj, 512, 3)], [do_a], [gg], name=f"gdn_gateb_{tag}")
            acc("gdn_norm_g", e, dgg[0])
            dqkvc, dbe, dae, dal, ddt = gdn_bwd(qkvc, proj, 4, 5, alog_e, dtb_e, hist, do_g, name=f"gdnb_{tag}")
            acc("gdn_A_log", e, dal.reshape(4, 128).sum(-1))
            acc("gdn_dt_bias", e, ddt.reshape(4, 128).sum(-1))
            dpre, dcw, _ = conv_bwd_pre(proj, 0, 3, conv_w, zb, dqkvc, name=f"gdn_convb_{tag}")
            acc("gdn_conv_w", e, dcw)
            dqkv = conv_bwd_x(dpre, conv_w, name=f"gdn_convx_{tag}", out_dtype=BF16)
            zc = lambda n: jnp.zeros((S_, n), BF16)
            dproj = jnp.concatenate([dqkv, dz, dbe, dae, dcq, zc(128), dckv, dkr, zc(128)], axis=1)
            dh = mm((dproj, wext), tb=True, name=f"ev_inb_{tag}")
            acc("ev_w_in", e, ev_ext_t(mm((h, dproj), ta=True, name=f"ev_dwin_{tag}")))
            return mixer_tail(xin, l, tag, dh, dxn, g, shift, scale, dgate)

        return xn, bwd

    def odd_sub(xin, l):
        o = l // 2
        tag = f"l{l}m"
        g, (shift, scale, gate) = Wf["norm_g"][l, 1][None], [mod[l, 1, t][None] for t in range(3)]
        wext, wout = od_ext(W["ssd_w_in"][o]), W["ssd_w_out"][o]
        conv_w, conv_b, ng = Wf["ssd_conv_w"][o], Wf["ssd_conv_b"][o][None], Wf["ssd_norm_g"][o][None]
        ex = lambda v: _rep(v, 64)[None]
        na_e, dtb_e, dsk_e = ex(-jnp.exp(ssd_A_log[o])), ex(ssd_dt_bias[o]), ex(ssd_D[o])
        h, = rowmap(lambda *a: (_adaln(*a),), [xin], [g, shift, scale], (BF16,), name=f"adaln_{tag}")
        proj = mm((h, wext), name=f"ssd_in_{tag}")
        zv = [(proj, 512, 6 + t) for t in range(4)]
        xbc = conv_fwd(proj, 0, 6, conv_w, conv_b, name=f"ssd_conv_{tag}")
        ys, hist = ssd_fwd(xbc, proj, na_e, dtb_e, dsk_e, name=f"ssd_{tag}")
        yn, = rowmap(lambda *a: (_ssd_gate(*a),), [ys] + zv, [ng], (BF16,), name=f"ssd_gate_{tag}", tile=256)
        xn, y = mm((yn, wout), name=f"ssd_out_{tag}", epi=lambda acc_, xr, gt: (xr + gt * acc_, acc_), epi_rows=(xin,),
                   epi_pars=(gate,), epi_out_dtypes=(F32, F32))

        def bwd(dxn):
            dy, dgate = rowmap(lambda d, y_, gt: (gt * d, jnp.sum(y_ * d, axis=0, keepdims=True)), [dxn, y], [gate],
                               (BF16, F32), name=f"dres_{tag}", n_reduce=1)
            dyn = mm((dy, wout), tb=True, name=f"ssd_dyn_{tag}", out_dtype=BF16)
            acc("ssd_w_out", o, mm((yn, dy), ta=True, name=f"ssd_dwout_{tag}"))
            dys, dz0, dz1, dz2, dz3, dng = _vjp_rows(_ssd_gate, 5, 1, (F32, BF16, BF16, BF16, BF16), [ys] + zv, [dyn], [ng],
                                                     name=f"ssd_gateb_{tag}", tile=256)
            acc("ssd_norm_g", o, dng[0])
            dxs, ddtx, db_, dc_, dna, ddtb, ddsk = ssd_bwd(xbc, proj, na_e, dtb_e, dsk_e, hist, dys, name=f"ssdb_{tag}")
            acc("ssd_A_log", o, dna.reshape(32, 64).sum(-1) * (-jnp.exp(ssd_A_log[o])))
            acc("ssd_dt_bias", o, ddtb.reshape(32, 64).sum(-1))
            acc("ssd_D", o, ddsk.reshape(32, 64).sum(-1))
            dxp, dcws, dcbs = [], [], []
            for part, (cb0, ncb, dpart) in enumerate(((0, 4, dxs), (4, 1, db_), (5, 1, dc_))):
                cols = slice(512 * cb0, 512 * (cb0 + ncb))
                dpre, dcw, dcb = conv_bwd_pre(proj, cb0, ncb, conv_w[:, cols], conv_b[:, cols], dpart, name=f"ssd_convb{part}_{tag}")
                dxp.append(conv_bwd_x(dpre, conv_w[:, cols], name=f"ssd_convx{part}_{tag}", out_dtype=BF16))
                dcws.append(dcw)
                dcbs.append(dcb[0])
            acc("ssd_conv_w", o, jnp.concatenate(dcws, axis=1))
            acc("ssd_conv_b", o, jnp.concatenate(dcbs))
            dproj = jnp.concatenate(dxp + [dz0, dz1, dz2, dz3, ddtx], axis=1)
            dh = mm((dproj, wext), tb=True, name=f"ssd_inb_{tag}")
            acc("ssd_w_in", o, od_ext_t(mm((h, dproj), ta=True, name=f"ssd_dwin_{tag}")))
            return mixer_tail(xin, l, tag, dh, dxn, g, shift, scale, dgate)

        return xn, bwd

    tape = []
    xc = xa
    for l in range(depth):
        xc, b0 = ffn_sub(xc, l, 0, 0)
        xc, b1 = (even_sub if l % 2 == 0 else odd_sub)(xc, l)
        xc, b2 = ffn_sub(xc, l, 2, 1)
        tape += [b0, b1, b2]

    def head(xr, tg, g_):
        def f(xv, gv):
            err = _rms(xv, gv) - tg
            return 0.5 * jnp.sum(jnp.mean(err * err, axis=-1, keepdims=True), axis=0, keepdims=True)
        lo, vjp = jax.vjp(f, xr, g_)
        dxv, dgv = vjp(jnp.ones_like(lo))
        return dxv, jnp.broadcast_to(lo, (1, 128)), dgv

    dx, loss_p, dfg = rowmap(head, [xc, tgt], [final_g[None]], (F32,), name="loss_head", n_reduce=2)
    loss = lax.psum(loss_p[0, 0], ("x", "y", "c"))

    for b in reversed(tape):
        dx = b(dx)
    grad_x = dx[None]

    full = {n: jnp.stack([grads[n][k] for k in sorted(grads[n])]) for n in ("ev_w_in", "mla_w_uq", "mla_w_ukv", "ev_w_out", "ssd_w_in", "ssd_w_out")}
    for n in ("ffn_w1", "ffn_w3", "ffn_w2"):
        full[n] = jnp.stack([jnp.stack([grads[n][(l, j)] for j in range(2)]) for l in range(depth)])
    gall = _pack_by_owner([full[n] for n, _, _ in BIG])
    half = PACK_ROWS // 2
    from_sib = swap_other_half(gall, name="swap_half")
    pair = add_pairs(lax.dynamic_slice_in_dim(gall, ci * half, half, axis=1), from_sib, name="add_sibling", out_dtype=BF16)
    recv = exchange_chips(pair, name="exchange_grads")
    part = sum_slots(recv, name="sum_chips")
    sib = swap_sibling(part, name="swap_sibling")
    lo = jnp.where(ci == 0, part, sib)
    hi_ = jnp.where(ci == 0, sib, part)
    g_tot = _unpack(jnp.concatenate([lo, hi_], axis=0))

    dmod_flat = jnp.stack([jnp.stack([jnp.stack(dmod[l][k]) for k in range(3)]) for l in range(depth)]).reshape(depth, 9 * D)
    small_names = ["norm_g", "gdn_conv_w", "gdn_A_log", "gdn_dt_bias", "gdn_norm_g", "mla_q_norm_g", "mla_kv_norm_g",
                   "ssd_conv_w", "ssd_conv_b", "ssd_A_log", "ssd_dt_bias", "ssd_D", "ssd_norm_g", "final_g"]
    small_full = {n: jnp.stack([grads[n][k] for k in sorted(grads[n])]) for n in small_names if n in grads}
    small_full["norm_g"] = jnp.stack([jnp.stack(dnorm_g[l]) for l in range(depth)])
    small_full["final_g"] = dfg[0]
    small_list = [dmod_flat] + [small_full[n] for n in small_names]
    small_shapes = [a.shape for a in small_list]
    sp = _flat_pack(small_list, 128, 8)
    sgath = allgather8(sp, name="gather_small_grads").reshape(8, sp.shape[0], 128)
    ssum = sum_slots(sgath, name="sum_small")
    tot = dict(zip(["ada_b"] + small_names, _flat_unpack(ssum, small_shapes)))
    dmod_all = sgath.reshape(8, -1)[:, :depth * 9 * D].reshape(8, depth, 9 * D)
    dmod_loc = lax.dynamic_slice(dmod_all, (0, 0, chip * ncol), (8, depth, ncol))
    g_ada_w = jnp.stack([mm((c_act, jnp.pad(dmod_loc[:, l], ((0, 8), (0, 0)))), ta=True, name=f"ada_dw_{l}", tk=16, tn=256)
                         for l in range(depth)])

    def own(n, a):
        for m_, sh, ax in SMALL_SHARDED:
            if m_ == n:
                return lax.dynamic_slice_in_dim(a, chip * sh[ax], sh[ax], axis=ax)
        return a

    res = {}
    for k, (n, _, _) in enumerate(BIG):
        res[n] = adamw(P[n], [g_tot[k]], M1[n], M2[n], name=f"adamw_{n}")
    res["ada_w"] = adamw(ada_w, [g_ada_w], m_ada_w, v_ada_w, name="adamw_ada_w")
    sm = ["ada_b"] + small_names
    shapes = [P[n].shape for n in sm]
    pk = lambda d: _flat_pack([d[n] for n in sm], 128, 8)
    outs = adamw(pk(P), [pk({n: own(n, tot[n]).reshape(P[n].shape) for n in sm})], pk(M1), pk(M2), name="adamw_small")
    un = [_flat_unpack(o, shapes) for o in outs]
    for i, n in enumerate(sm):
        res[n] = tuple(un[t][i] for t in range(4))
    return (loss, grad_x, *[res[n][0] for n in names], *[res[n][1] for n in names], *[res[n][2] for n in names], *[res[n][3] for n in names])
```

```python
import functools
import math

import jax
import jax.numpy as jnp
from jax import lax
from jax.experimental import pallas as pl
from jax.experimental.pallas import tpu as pltpu

F32 = jnp.float32
BF16 = jnp.bfloat16
HI = lax.Precision.HIGHEST
HI3 = lax.Precision.HIGH
VMEM_LIMIT = 56 * 1024 * 1024
NORM_EPS = 1e-6
MM_VMEM_BUDGET = 40 * 1024 * 1024


def _cp(sem=None):
    if sem is None:
        return pltpu.CompilerParams(vmem_limit_bytes=VMEM_LIMIT)
    return pltpu.CompilerParams(dimension_semantics=sem, vmem_limit_bytes=VMEM_LIMIT)


def _pick(dim, prefs):
    for p in prefs:
        if dim % p == 0:
            return p
    return dim


def mm(pairs, *, ta=False, tb=False, out_dtype=F32, name, epi=None, epi_rows=(), epi_pars=(), epi_out_dtypes=None,
       tm=None, tn=None, tk=None):
    if not isinstance(pairs, (list, tuple)) or not isinstance(pairs[0], (list, tuple)):
        pairs = [pairs]
    npair = len(pairs)
    a0, b0 = pairs[0]
    M = a0.shape[1] if ta else a0.shape[0]
    K = a0.shape[0] if ta else a0.shape[1]
    N = b0.shape[0] if tb else b0.shape[1]
    for a, b in pairs:
        assert (a.shape == ((K, M) if ta else (M, K))), (a.shape, M, K)
        assert (b.shape == ((N, K) if tb else (K, N))), (b.shape, K, N)
    tm = tm or _pick(M, (1024, 1408, 512, 384, 256, 128))
    tk = tk or (K if K <= 1024 else _pick(K, (1024, 1408, 512, 256, 128)))
    if tn is None:
        n_epi_out = 1 if epi is None else len(epi_out_dtypes)
        for tn in (1024, 1408, 512, 384, 256, 128, N):
            if N % tn:
                continue
            need = sum(2 * tk * (tm * a.dtype.itemsize + tn * b.dtype.itemsize) for a, b in pairs)
            need += tm * tn * 4 * (1 + 2 * n_epi_out + 2 * len(epi_rows))
            if need <= MM_VMEM_BUDGET:
                break
    nk = K // tk
    assert M % tm == 0 and N % tn == 0 and K % tk == 0, (M, N, K, tm, tn, tk)
    n_rows, n_pars = len(epi_rows), len(epi_pars)
    if epi is None:
        out_dtypes = (out_dtype,)
    else:
        out_dtypes = tuple(epi_out_dtypes)
    n_out = len(out_dtypes)
    dn = (((0 if ta else 1,), (1 if tb else 0,)), ((), ()))

    def body(*refs):
        ab = refs[:2 * npair]
        rows = refs[2 * npair:2 * npair + n_rows]
        pars = refs[2 * npair + n_rows:2 * npair + n_rows + n_pars]
        outs = refs[2 * npair + n_rows + n_pars:2 * npair + n_rows + n_pars + n_out]
        acc_ref = refs[-1]
        k = pl.program_id(2)

        @pl.when(k == 0)
        def _():
            acc_ref[...] = jnp.zeros_like(acc_ref)

        acc = acc_ref[...]
        for p in range(npair):
            a = ab[2 * p][...].astype(BF16)
            b = ab[2 * p + 1][...].astype(BF16)
            acc = acc + lax.dot_general(a, b, dn, preferred_element_type=F32)
        acc_ref[...] = acc

        @pl.when(k == nk - 1)
        def _():
            r = acc_ref[...]
            if epi is None:
                outs[0][...] = r.astype(outs[0].dtype)
            else:
                res = epi(r, *[x[...] for x in rows], *[x[...] for x in pars])
                for o, v in zip(outs, res):
                    o[...] = v.astype(o.dtype)

    a_spec = pl.BlockSpec((tk, tm), lambda i, j, k: (k, i)) if ta else pl.BlockSpec((tm, tk), lambda i, j, k: (i, k))
    b_spec = pl.BlockSpec((tn, tk), lambda i, j, k: (j, k)) if tb else pl.BlockSpec((tk, tn), lambda i, j, k: (k, j))
    in_specs = []
    args = []
    for a, b in pairs:
        in_specs += [a_spec, b_spec]
        args += [a, b]
    for r in epi_rows:
        in_specs.append(pl.BlockSpec((tm, tn), lambda i, j, k: (i, j)))
        args.append(r)
    for p_ in epi_pars:
        in_specs.append(pl.BlockSpec((1, tn), lambda i, j, k: (0, j)))
        args.append(p_)
    out_specs = [pl.BlockSpec((tm, tn), lambda i, j, k: (i, j)) for _ in range(n_out)]
    out_shape = [jax.ShapeDtypeStruct((M, N), d) for d in out_dtypes]
    res = pl.pallas_call(
        body, grid=(M // tm, N // tn, nk), in_specs=in_specs, out_specs=out_specs, out_shape=out_shape,
        scratch_shapes=[pltpu.VMEM((tm, tn), F32)], compiler_params=_cp(("parallel", "parallel", "arbitrary")), name=name,
    )(*args)
    return res[0] if epi is None else tuple(res)


def rowmap(fn, rows, pars, out_dtypes, *, name, tile=512, n_reduce=0):
    views = []
    for r in rows:
        if isinstance(r, tuple):
            views.append(r)
        else:
            views.append((r, r.shape[1], 0))
    S = views[0][0].shape[0]
    tile = min(tile, S)
    assert S % tile == 0
    nt = S // tile
    row_structs = [jax.ShapeDtypeStruct((tile, w), a.dtype) for a, w, _ in views]
    par_structs = [jax.ShapeDtypeStruct(p.shape, p.dtype) for p in pars]
    out_structs = jax.eval_shape(fn, *row_structs, *par_structs)
    n_out = len(out_structs)
    n_row_out = n_out - n_reduce
    nr, npar = len(views), len(pars)

    def body(*refs):
        ins = [x[...] for x in refs[:nr + npar]]
        outs = refs[nr + npar:]
        res = fn(*ins)
        for o, v in zip(outs[:n_row_out], res[:n_row_out]):
            o[...] = v.astype(o.dtype)
        if n_reduce:
            i = pl.program_id(0)

            @pl.when(i == 0)
            def _():
                for o, v in zip(outs[n_row_out:], res[n_row_out:]):
                    o[...] = v.astype(o.dtype)

            @pl.when(i > 0)
            def _():
                for o, v in zip(outs[n_row_out:], res[n_row_out:]):
                    o[...] += v.astype(o.dtype)

    in_specs = [pl.BlockSpec((tile, w), functools.partial(lambda i, c: (i, c), c=c)) for _, w, c in views]
    in_specs += [pl.BlockSpec(p.shape, lambda i: (0, 0)) for p in pars]
    out_specs = [pl.BlockSpec((tile, s.shape[1]), lambda i: (i, 0)) for s in out_structs[:n_row_out]]
    out_specs += [pl.BlockSpec(s.shape, lambda i: (0, 0)) for s in out_structs[n_row_out:]]
    out_shape = [jax.ShapeDtypeStruct((S, s.shape[1]), d) for s, d in zip(out_structs[:n_row_out], out_dtypes[:n_row_out])]
    out_shape += [jax.ShapeDtypeStruct(s.shape, F32) for s in out_structs[n_row_out:]]
    res = pl.pallas_call(
        body, grid=(nt,), in_specs=in_specs, out_specs=out_specs, out_shape=out_shape,
        compiler_params=_cp(("arbitrary",) if n_reduce else ("parallel",)), name=name,
    )(*[v[0] for v in views], *pars)
    return tuple(res)


CH = 64


def _softplus(x):
    return jnp.where(x > 20.0, x, jnp.log(1.0 + jnp.exp(jnp.minimum(x, 20.0))))


def _dot(a, b, dn=(((1,), (0,)), ((), ())), hi=False):
    if hi:
        return lax.dot_general(a.astype(F32), b.astype(F32), dn, precision=HI if hi is True else hi, preferred_element_type=F32)
    return lax.dot_general(a.astype(BF16), b.astype(BF16), dn, preferred_element_type=F32)


_NT = (((1,), (1,)), ((), ()))
_TN = (((0,), (0,)), ((), ()))


def _chunk_consts():
    r = lax.broadcasted_iota(jnp.int32, (CH, 2 * CH), 0)
    c0 = lax.broadcasted_iota(jnp.int32, (CH, 2 * CH), 1)
    c = jnp.where(c0 >= CH, c0 - CH, c0)
    r1 = lax.broadcasted_iota(jnp.int32, (CH, CH), 0)
    c1 = lax.broadcasted_iota(jnp.int32, (CH, CH), 1)
    return dict(
        lower2=r >= c, strict2=r > c, U2=(r <= c).astype(F32), eye2=(r == c).astype(F32),
        L=(r1 >= c1).astype(F32), ones=jnp.ones((CH, CH), F32), Z=jnp.zeros((CH, 2 * CH), F32))


@jax.custom_vjp
def _tri_inv2(a2s, eye2, Z):
    def prod(x2, y):
        return _dot(x2, jnp.concatenate([y, Z], axis=0), hi=HI3)

    bs = [-a2 for a2 in a2s]
    ts = [eye2 + b for b in bs]
    for _ in range(5):
        bs = [prod(b, b) for b in bs]
        ts = [t + prod(t, b) for t, b in zip(ts, bs)]
    return tuple(ts)


def _tri_inv2_fwd(a2s, eye2, Z):
    ts = _tri_inv2(a2s, eye2, Z)
    return ts, (ts, eye2, Z)


def _tri_inv2_bwd(res, dts):
    ts, eye2, Z = res
    xs = [_dot(t2, dt2, _TN, hi=HI3)[:CH] for t2, dt2 in zip(ts, dts)]
    das = tuple(-_dot(x2, jnp.concatenate([t2, Z], axis=0), _NT, hi=HI3) for x2, t2 in zip(xs, ts))
    return das, jnp.zeros_like(eye2), jnp.zeros_like(Z)


_tri_inv2.defvjp(_tri_inv2_fwd, _tri_inv2_bwd)


def _gdn_heads(qs, ks, vs, bxs, axs, Ss, alogs, dtbs, cst):
    lower2, strict2, U2, eye2, L, ones, Z = (cst[n] for n in ("lower2", "strict2", "U2", "eye2", "L", "ones", "Z"))
    H = range(len(qs))

    def prod(x2, y):
        return _dot(x2, jnp.concatenate([y, Z], axis=0), hi=HI3)

    qn = [qs[h] * lax.rsqrt(jnp.sum(qs[h] * qs[h], axis=-1, keepdims=True) + NORM_EPS) * (128.0 ** -0.5) for h in H]
    kn = [ks[h] * lax.rsqrt(jnp.sum(ks[h] * ks[h], axis=-1, keepdims=True) + NORM_EPS) for h in H]
    beta = [jax.nn.sigmoid(bxs[h]) for h in H]
    g = [-jnp.exp(alogs[h]) * _softplus(axs[h] + dtbs[h]) for h in H]
    gc = [_dot(L, g[h], hi=HI3) for h in H]
    n2 = [_dot(ones, g[h] * U2, hi=HI3) for h in H]
    decay2 = [jnp.where(lower2, jnp.exp(jnp.where(lower2, gc[h] - n2[h], 0.0)), 0.0) for h in H]
    kb = [kn[h] * beta[h] for h in H]
    kn2 = [jnp.concatenate([kn[h], kn[h]], axis=0) for h in H]
    a2 = tuple(jnp.where(strict2, _dot(kb[h], kn2[h], _NT) * decay2[h], 0.0) for h in H)
    t2 = _tri_inv2(a2, eye2, Z)
    glast = [jnp.sum(g[h], axis=0, keepdims=True) for h in H]
    u = [prod(t2[h], vs[h] * beta[h]) for h in H]
    w = [prod(t2[h], kb[h] * jnp.exp(gc[h])) for h in H]
    attn2 = [jnp.where(lower2, _dot(qn[h], kn2[h], _NT) * decay2[h], 0.0) for h in H]
    k_end = [kn[h] * jnp.exp(glast[h] - gc[h]) for h in H]
    q_start = [qn[h] * jnp.exp(gc[h]) for h in H]
    v_new = [u[h] - _dot(w[h], Ss[h]) for h in H]
    o = [_dot(q_start[h], Ss[h]) + _dot(attn2[h], jnp.concatenate([v_new[h], Z], axis=0)) for h in H]
    s_new = [Ss[h] * jnp.exp(glast[h]) + _dot(k_end[h], v_new[h], _TN) for h in H]
    return tuple(o), tuple(s_new)


def gdn_fwd(qkv, proj, bcol, acol, alog_e, dtb_e, *, name):
    S_ = qkv.shape[0]
    nc = S_ // CH

    def body(q_ref, k_ref, v_ref, b_ref, a_ref, al_ref, dt_ref, o_ref, hist_ref, s_ref):
        i = pl.program_id(0)

        @pl.when(i == 0)
        def _():
            s_ref[...] = jnp.zeros_like(s_ref)

        cst = _chunk_consts()
        hist_ref[0] = s_ref[...]
        heads = [slice(128 * h, 128 * (h + 1)) for h in range(4)]
        rd = lambda ref: tuple(ref[:, ls] for ls in heads)
        os_, s_news = _gdn_heads(rd(q_ref), rd(k_ref), rd(v_ref), rd(b_ref), rd(a_ref), tuple(s_ref[ls, :] for ls in heads),
                                 rd(al_ref), rd(dt_ref), cst)
        for ls, o, s_new in zip(heads, os_, s_news):
            o_ref[:, ls] = o
            s_ref[ls, :] = s_new

    blk = lambda cb: pl.BlockSpec((CH, 512), functools.partial(lambda i, cb: (i, cb), cb=cb))
    par = pl.BlockSpec((1, 512), lambda i: (0, 0))
    return pl.pallas_call(
        body, grid=(nc,), in_specs=[blk(0), blk(1), blk(2), blk(bcol), blk(acol), par, par],
        out_specs=[pl.BlockSpec((CH, 512), lambda i: (i, 0)), pl.BlockSpec((1, 512, 128), lambda i: (i, 0, 0))],
        out_shape=[jax.ShapeDtypeStruct((S_, 512), F32), jax.ShapeDtypeStruct((nc, 512, 128), F32)],
        scratch_shapes=[pltpu.VMEM((512, 128), F32)], compiler_params=_cp(("arbitrary",)), name=name,
    )(qkv, qkv, qkv, proj, proj, alog_e, dtb_e)


def gdn_bwd(qkv, proj, bcol, acol, alog_e, dtb_e, hist, do, *, name):
    S_ = qkv.shape[0]
    nc = S_ // CH

    def body(q_ref, k_ref, v_ref, b_ref, a_ref, al_ref, dt_ref, hist_ref, do_ref, dqkv_ref, db_ref, da_ref, dal_ref, ddt_ref, ds_ref):
        i = pl.program_id(0)

        @pl.when(i == 0)
        def _():
            ds_ref[...] = jnp.zeros_like(ds_ref)
            dal_ref[...] = jnp.zeros_like(dal_ref)
            ddt_ref[...] = jnp.zeros_like(ddt_ref)

        cst = _chunk_consts()
        heads = [slice(128 * h, 128 * (h + 1)) for h in range(4)]
        rd = lambda ref: tuple(ref[:, ls] for ls in heads)
        fn = functools.partial(_gdn_heads, cst=cst)
        _, vjp = jax.vjp(fn, rd(q_ref), rd(k_ref), rd(v_ref), rd(b_ref), rd(a_ref), tuple(hist_ref[0, ls, :] for ls in heads),
                         rd(al_ref), rd(dt_ref))
        grads = vjp((rd(do_ref), tuple(ds_ref[ls, :] for ls in heads)))
        for h in range(4):
            ls = heads[h]
            dq, dk, dv, db, da, ds_in, dal, ddt = (t[h] for t in grads)
            dqkv_ref[:, 128 * h:128 * (h + 1)] = dq
            dqkv_ref[:, 512 + 128 * h:512 + 128 * (h + 1)] = dk
            dqkv_ref[:, 1024 + 128 * h:1024 + 128 * (h + 1)] = dv
            db_ref[:, ls] = db.astype(db_ref.dtype)
            da_ref[:, ls] = da.astype(da_ref.dtype)
            ds_ref[ls, :] = ds_in
            dal_ref[:, ls] += dal
            ddt_ref[:, ls] += ddt

    rblk = lambda cb: pl.BlockSpec((CH, 512), functools.partial(lambda i, cb: (nc - 1 - i, cb), cb=cb))
    par = pl.BlockSpec((1, 512), lambda i: (0, 0))
    return pl.pallas_call(
        body, grid=(nc,),
        in_specs=[rblk(0), rblk(1), rblk(2), rblk(bcol), rblk(acol), par, par,
                  pl.BlockSpec((1, 512, 128), lambda i: (nc - 1 - i, 0, 0)), rblk(0)],
        out_specs=[pl.BlockSpec((CH, 1536), lambda i: (nc - 1 - i, 0)), rblk(0), rblk(0), par, par],
        out_shape=[jax.ShapeDtypeStruct((S_, 1536), F32), jax.ShapeDtypeStruct((S_, 512), BF16), jax.ShapeDtypeStruct((S_, 512), BF16),
                   jax.ShapeDtypeStruct((1, 512), F32), jax.ShapeDtypeStruct((1, 512), F32)],
        scratch_shapes=[pltpu.VMEM((512, 128), F32)], compiler_params=_cp(("arbitrary",)), name=name,
    )(qkv, qkv, qkv, proj, proj, alog_e, dtb_e, hist, do)


def _ssd_pairs(xs, dtxs, bms, cms, hss, nas, dtbs, dsks, cst):
    lower2, U2, L, ones = cst["lower2"], cst["U2"], cst["L"], cst["ones"]
    lane = lax.broadcasted_iota(jnp.int32, (1, 2 * CH), 1)
    mask_l = (lane < CH).astype(F32)
    mask_r = 1.0 - mask_l
    ones_w = jnp.ones((CH, 2 * CH), F32)
    P_ = range(len(xs))
    G_ = range(len(bms))
    per = len(xs) // len(bms)
    cb2 = [_dot(cms[g], jnp.concatenate([bms[g], bms[g]], axis=0), _NT) for g in G_]
    dt = [_softplus(dtxs[p] + dtbs[p]) for p in P_]
    da = [dt[p] * nas[p] for p in P_]
    m = [_dot(L, da[p], hi=HI3) for p in P_]
    n2 = [_dot(ones, da[p] * U2, hi=HI3) for p in P_]
    lm2 = [jnp.where(lower2, jnp.exp(jnp.where(lower2, m[p] - n2[p], 0.0)), 0.0) for p in P_]
    xdt = [xs[p] * dt[p] for p in P_]
    x2 = [jnp.concatenate([xdt[p] * mask_l, xdt[p] * mask_r], axis=0) for p in P_]
    y_diag = [_dot(cb2[p // per] * lm2[p], x2[p]) for p in P_]
    alast = [jnp.sum(da[p], axis=0, keepdims=True) for p in P_]
    y_off = [_dot(cms[p // per], hss[p], _NT) * jnp.exp(m[p]) for p in P_]
    cd = [jnp.exp(_dot(da[p], ones_w, _TN, hi=HI3)) for p in P_]
    hs_new = [hss[p] * cd[p] + _dot(xdt[p] * jnp.exp(alast[p] - m[p]), bms[p // per], _TN) for p in P_]
    ys = [y_diag[p] + y_off[p] + dsks[p] * xs[p] for p in P_]
    return tuple(ys), tuple(hs_new)


def _ssd_specs(nc, rev):
    ci = (lambda i: nc - 1 - i) if rev else (lambda i: i)
    col = lambda w, c: pl.BlockSpec((CH, w), functools.partial(lambda i, c: (ci(i), c), c=c))
    xg = [col(512, g) for g in range(4)]
    dtg = [col(512, 10 + g) for g in range(4)]
    par = pl.BlockSpec((1, 2048), lambda i: (0, 0))
    hist = pl.BlockSpec((1, 2048, 128), lambda i: (ci(i), 0, 0))
    return xg, dtg, col(512, 4), col(512, 5), par, hist, col


def _ssd_read(x_refs, dt_refs, b_ref, c_ref, na_ref, dtb_ref, dsk_ref):
    sl = [slice(128 * p, 128 * (p + 1)) for p in range(4)]
    xs = tuple(x_refs[g][:, s] for g in range(4) for s in sl)
    dts = tuple(dt_refs[g][:, s] for g in range(4) for s in sl)
    bms = tuple(b_ref[:, s] for s in sl)
    cms = tuple(c_ref[:, s] for s in sl)
    lanes = [slice(128 * p, 128 * (p + 1)) for p in range(16)]
    pars = [tuple(r[:, s] for s in lanes) for r in (na_ref, dtb_ref, dsk_ref)]
    return xs, dts, bms, cms, pars, lanes


def ssd_fwd(xbc, proj, na_e, dtb_e, dsk_e, *, name):
    S_ = xbc.shape[0]
    nc = S_ // CH
    xg, dtg, bs, cs, par, hist, _ = _ssd_specs(nc, False)

    def body(*refs):
        x_refs, dt_refs = refs[0:4], refs[4:8]
        b_ref, c_ref, na_ref, dtb_ref, dsk_ref, y_ref, hist_ref, s_ref = refs[8:]
        i = pl.program_id(0)

        @pl.when(i == 0)
        def _():
            s_ref[...] = jnp.zeros_like(s_ref)

        cst = _chunk_consts()
        hist_ref[0] = s_ref[...]
        xs, dts, bms, cms, pars, lanes = _ssd_read(x_refs, dt_refs, b_ref, c_ref, na_ref, dtb_ref, dsk_ref)
        ys, hs_new = _ssd_pairs(xs, dts, bms, cms, tuple(s_ref[s, :] for s in lanes), *pars, cst)
        for p, s in enumerate(lanes):
            y_ref[:, s] = ys[p]
            s_ref[s, :] = hs_new[p]

    return pl.pallas_call(
        body, grid=(nc,), in_specs=xg + dtg + [bs, cs, par, par, par],
        out_specs=[pl.BlockSpec((CH, 2048), lambda i: (i, 0)), hist],
        out_shape=[jax.ShapeDtypeStruct((S_, 2048), F32), jax.ShapeDtypeStruct((nc, 2048, 128), F32)],
        scratch_shapes=[pltpu.VMEM((2048, 128), F32)], compiler_params=_cp(("arbitrary",)), name=name,
    )(xbc, xbc, xbc, xbc, proj, proj, proj, proj, xbc, xbc, na_e, dtb_e, dsk_e)


def ssd_bwd(xbc, proj, na_e, dtb_e, dsk_e, hist, dy, *, name):
    S_ = xbc.shape[0]
    nc = S_ // CH
    xg, dtg, bs, cs, par, hist_spec, col = _ssd_specs(nc, True)
    wide = pl.BlockSpec((CH, 2048), lambda i: (nc - 1 - i, 0))

    def body(*refs):
        x_refs, dt_refs = refs[0:4], refs[4:8]
        (b_ref, c_ref, na_ref, dtb_ref, dsk_ref, hist_ref, dy_ref,
         dx_ref, ddt_ref, db_ref, dc_ref, dna_ref, ddtb_ref, ddsk_ref, ds_ref) = refs[8:]
        i = pl.program_id(0)

        @pl.when(i == 0)
        def _():
            ds_ref[...] = jnp.zeros_like(ds_ref)
            dna_ref[...] = jnp.zeros_like(dna_ref)
            ddtb_ref[...] = jnp.zeros_like(ddtb_ref)
            ddsk_ref[...] = jnp.zeros_like(ddsk_ref)

        cst = _chunk_consts()
        xs, dts, bms, cms, pars, lanes = _ssd_read(x_refs, dt_refs, b_ref, c_ref, na_ref, dtb_ref, dsk_ref)
        fn = functools.partial(_ssd_pairs, cst=cst)
        _, vjp = jax.vjp(fn, xs, dts, bms, cms, tuple(hist_ref[0, s, :] for s in lanes), *pars)
        dxs, ddts, dbs, dcs, dhs, dnas, ddtbs, ddsks = vjp((tuple(dy_ref[:, s] for s in lanes), tuple(ds_ref[s, :] for s in lanes)))
        for g in range(4):
            db_ref[:, 128 * g:128 * (g + 1)] = dbs[g]
            dc_ref[:, 128 * g:128 * (g + 1)] = dcs[g]
        for p, s in enumerate(lanes):
            dx_ref[:, s] = dxs[p]
            ddt_ref[:, s] = ddts[p].astype(ddt_ref.dtype)
            ds_ref[s, :] = dhs[p]
            dna_ref[:, s] += dnas[p]
            ddtb_ref[:, s] += ddtbs[p]
            ddsk_ref[:, s] += ddsks[p]

    half = pl.BlockSpec((CH, 512), lambda i: (nc - 1 - i, 0))
    return pl.pallas_call(
        body, grid=(nc,), in_specs=xg + dtg + [bs, cs, par, par, par, hist_spec, wide],
        out_specs=[wide, wide, half, half, par, par, par],
        out_shape=[jax.ShapeDtypeStruct((S_, 2048), F32), jax.ShapeDtypeStruct((S_, 2048), BF16),
                   jax.ShapeDtypeStruct((S_, 512), F32), jax.ShapeDtypeStruct((S_, 512), F32)] +
                  [jax.ShapeDtypeStruct((1, 2048), F32)] * 3,
        scratch_shapes=[pltpu.VMEM((2048, 128), F32)], compiler_params=_cp(("arbitrary",)), name=name,
    )(xbc, xbc, xbc, xbc, proj, proj, proj, proj, xbc, xbc, na_e, dtb_e, dsk_e, hist, dy)


ATT_T = 1024
ATT_SCALE = 192.0 ** -0.5
NEG = -1e30


def _chunk_mask(shape):
    return lax.broadcasted_iota(jnp.int32, shape, 1) // CH <= lax.broadcasted_iota(jnp.int32, shape, 0) // CH


def _tri_pairs(n, by_row):
    pairs = [(i, j) for i in range(n) for j in range(i + 1)] if by_row else [(i, j) for j in range(n) for i in range(j, n)]
    return jnp.asarray([p[0] for p in pairs], jnp.int32), jnp.asarray([p[1] for p in pairs], jnp.int32)


def att_fwd(q, kv, kp, *, name):
    S_ = q.shape[0]
    T = min(ATT_T, S_)
    n = S_ // T
    ii, jj = _tri_pairs(n, True)

    def body(ii_ref, jj_ref, q_ref, kn_ref, kp_ref, v_ref, o_ref, lse_ref, m_ref, l_ref, acc_ref):
        t = pl.program_id(1)
        i, j = ii_ref[t], jj_ref[t]

        @pl.when(j == 0)
        def _():
            m_ref[...] = jnp.full_like(m_ref, NEG)
            l_ref[...] = jnp.zeros_like(l_ref)
            acc_ref[...] = jnp.zeros_like(acc_ref)

        def step(diag):
            k2 = jnp.concatenate([kn_ref[...], kp_ref[...]], axis=1)
            s = _dot(q_ref[...], k2, _NT)
            if diag:
                s = jnp.where(_chunk_mask(s.shape), s, NEG)
            m_prev = m_ref[...]
            m_cur = jnp.maximum(m_prev, jnp.max(s, axis=-1, keepdims=True))
            p = jnp.exp(s - m_cur[:, :1])
            alpha = jnp.exp(m_prev - m_cur)
            l_ref[...] = alpha * l_ref[...] + jnp.sum(p, axis=-1, keepdims=True)
            acc_ref[...] = acc_ref[...] * alpha + _dot(p, v_ref[...])
            m_ref[...] = m_cur

        @pl.when(j < i)
        def _():
            step(False)

        @pl.when(j == i)
        def _():
            step(True)
            o_ref[...] = acc_ref[...] / l_ref[...]
            lse_ref[...] = m_ref[...] + jnp.log(l_ref[...])

    grid_spec = pltpu.PrefetchScalarGridSpec(
        num_scalar_prefetch=2, grid=(4, ii.shape[0]),
        in_specs=[pl.BlockSpec((T, 256), lambda h, t, ii_, jj_: (ii_[t], h)), pl.BlockSpec((T, 128), lambda h, t, ii_, jj_: (jj_[t], h)),
                  pl.BlockSpec((T, 128), lambda h, t, ii_, jj_: (jj_[t], 0)),
                  pl.BlockSpec((T, 128), lambda h, t, ii_, jj_: (jj_[t], 4 + h))],
        out_specs=[pl.BlockSpec((T, 128), lambda h, t, ii_, jj_: (ii_[t], h))] * 2,
        scratch_shapes=[pltpu.VMEM((T, 128), F32)] * 3)
    return pl.pallas_call(
        body, grid_spec=grid_spec, out_shape=[jax.ShapeDtypeStruct((S_, 512), F32), jax.ShapeDtypeStruct((S_, 512), F32)],
        compiler_params=_cp(("parallel", "arbitrary")), name=name,
    )(ii, jj, q, kv, kp, kv)


def att_bwd(q, kv, kp, lse, dsum, do, *, name):
    S_ = q.shape[0]
    T = min(ATT_T, S_)
    n = S_ // T
    ii, jj = _tri_pairs(n, False)
    last = ii.shape[0] - 1

    def body(ii_ref, jj_ref, q_ref, kn_ref, kp_ref, v_ref, lse_ref, d_ref, do_ref, dq_hbm, dk_ref, dv_ref, dq_acc, sem):
        h, t = pl.program_id(0), pl.program_id(1)
        i, j = ii_ref[t], jj_ref[t]

        @pl.when(t == 0)
        def _():
            dq_acc[...] = jnp.zeros_like(dq_acc)

        def step(diag):
            k2 = jnp.concatenate([kn_ref[...], kp_ref[...]], axis=1)
            qb = q_ref[...]
            dob = do_ref[...].astype(BF16)
            s = _dot(qb, k2, _NT)
            p = jnp.exp(s - lse_ref[:, :1])
            if diag:
                p = jnp.where(_chunk_mask(s.shape), p, 0.0)
            ds = (p * (_dot(dob, v_ref[...], _NT) - d_ref[:, :1])).astype(BF16)
            rows = pl.ds(pl.multiple_of(i * T, T), T)
            dq_acc[rows, :] += _dot(ds, k2)
            if diag:
                dv_ref[...] = _dot(p, dob, _TN)
                dk_ref[...] = _dot(ds, qb, _TN)
            else:
                dv_ref[...] += _dot(p, dob, _TN)
                dk_ref[...] += _dot(ds, qb, _TN)

        @pl.when(i > j)
        def _():
            step(False)

        @pl.when(i == j)
        def _():
            step(True)

        @pl.when(t == last)
        def _():
            cp = pltpu.make_async_copy(dq_acc, dq_hbm.at[h], sem)
            cp.start()
            cp.wait()

    qmap = lambda h, t, ii_, jj_: (ii_[t], h)
    grid_spec = pltpu.PrefetchScalarGridSpec(
        num_scalar_prefetch=2, grid=(4, ii.shape[0]),
        in_specs=[pl.BlockSpec((T, 256), qmap), pl.BlockSpec((T, 128), lambda h, t, ii_, jj_: (jj_[t], h)),
                  pl.BlockSpec((T, 128), lambda h, t, ii_, jj_: (jj_[t], 0)), pl.BlockSpec((T, 128), lambda h, t, ii_, jj_: (jj_[t], 4 + h)),
                  pl.BlockSpec((T, 128), qmap), pl.BlockSpec((T, 128), qmap), pl.BlockSpec((T, 128), qmap)],
        out_specs=[pl.BlockSpec(memory_space=pl.ANY), pl.BlockSpec((T, 256), lambda h, t, ii_, jj_: (jj_[t], h)),
                   pl.BlockSpec((T, 128), lambda h, t, ii_, jj_: (jj_[t], h))],
        scratch_shapes=[pltpu.VMEM((S_, 256), F32), pltpu.SemaphoreType.DMA])
    return pl.pallas_call(
        body, grid_spec=grid_spec,
        out_shape=[jax.ShapeDtypeStruct((4, S_, 256), F32), jax.ShapeDtypeStruct((S_, 1024), F32), jax.ShapeDtypeStruct((S_, 512), F32)],
        compiler_params=_cp(("arbitrary", "arbitrary")), name=name,
    )(ii, jj, q, kv, kp, kv, lse, dsum, do)


CONV_T = 512


def _shift_down(x, halo, s):
    sh = pltpu.roll(x, s, axis=0)
    hr = pltpu.roll(halo, s, axis=0)
    r8 = lax.broadcasted_iota(jnp.int32, hr.shape, 0)
    top = jnp.where(r8 < s, hr, sh[:8])
    return jnp.concatenate([top, sh[8:]], axis=0)


def _shift_up(x, halo, s):
    n = x.shape[0]
    sh = pltpu.roll(x, n - s, axis=0)
    hr = pltpu.roll(halo, 8 - s, axis=0)
    r8 = lax.broadcasted_iota(jnp.int32, hr.shape, 0)
    bot = jnp.where(r8 >= 8 - s, hr, sh[n - 8:])
    return jnp.concatenate([sh[:n - 8], bot], axis=0)


def _conv_pre(x, halo, w, b):
    y = x * w[3:4] + b
    for j in range(3):
        y = y + _shift_down(x, halo, 3 - j) * w[j:j + 1]
    return y


def conv_fwd(src, cb0, ncb, w, b, *, name):
    S_ = src.shape[0]
    T = min(CONV_T, S_)
    nt = S_ // T

    def body(x_ref, h_ref, w_ref, b_ref, o_ref):
        i = pl.program_id(1)
        halo = jnp.where(i > 0, h_ref[...], 0.0)
        y = _conv_pre(x_ref[...], halo, w_ref[...], b_ref[...])
        o_ref[...] = y * jax.nn.sigmoid(y)

    return pl.pallas_call(
        body, grid=(ncb, nt),
        in_specs=[pl.BlockSpec((T, 512), lambda c, i: (i, cb0 + c)),
                  pl.BlockSpec((8, 512), lambda c, i: (jnp.maximum(i * (T // 8) - 1, 0), cb0 + c)),
                  pl.BlockSpec((4, 512), lambda c, i: (0, c)), pl.BlockSpec((1, 512), lambda c, i: (0, c))],
        out_specs=pl.BlockSpec((T, 512), lambda c, i: (i, c)),
        out_shape=jax.ShapeDtypeStruct((S_, 512 * ncb), F32), compiler_params=_cp(("parallel", "parallel")), name=name,
    )(src, src, w, b)


def conv_bwd_pre(src, cb0, ncb, w, b, dy, *, name):
    S_ = src.shape[0]
    T = min(CONV_T, S_)
    nt = S_ // T

    def body(x_ref, h_ref, w_ref, b_ref, dy_ref, dp_ref, dw_ref, db_ref):
        i = pl.program_id(1)
        halo = jnp.where(i > 0, h_ref[...], 0.0)
        x = x_ref[...]
        y = _conv_pre(x, halo, w_ref[...], b_ref[...])
        sg = jax.nn.sigmoid(y)
        dpre = dy_ref[...] * (sg * (1.0 + y * (1.0 - sg)))
        dp_ref[...] = dpre
        rows = [jnp.sum(dpre * _shift_down(x, halo, 3 - j), axis=0, keepdims=True) for j in range(3)]
        rows.append(jnp.sum(dpre * x, axis=0, keepdims=True))
        dw = jnp.concatenate(rows, axis=0)
        db = jnp.sum(dpre, axis=0, keepdims=True)

        @pl.when(i == 0)
        def _():
            dw_ref[...] = dw
            db_ref[...] = db

        @pl.when(i > 0)
        def _():
            dw_ref[...] += dw
            db_ref[...] += db

    return pl.pallas_call(
        body, grid=(ncb, nt),
        in_specs=[pl.BlockSpec((T, 512), lambda c, i: (i, cb0 + c)),
                  pl.BlockSpec((8, 512), lambda c, i: (jnp.maximum(i * (T // 8) - 1, 0), cb0 + c)),
                  pl.BlockSpec((4, 512), lambda c, i: (0, c)), pl.BlockSpec((1, 512), lambda c, i: (0, c)),
                  pl.BlockSpec((T, 512), lambda c, i: (i, c))],
        out_specs=[pl.BlockSpec((T, 512), lambda c, i: (i, c)), pl.BlockSpec((4, 512), lambda c, i: (0, c)),
                   pl.BlockSpec((1, 512), lambda c, i: (0, c))],
        out_shape=[jax.ShapeDtypeStruct((S_, 512 * ncb), F32), jax.ShapeDtypeStruct((4, 512 * ncb), F32),
                   jax.ShapeDtypeStruct((1, 512 * ncb), F32)],
        compiler_params=_cp(("parallel", "arbitrary")), name=name,
    )(src, src, w, b, dy)


def conv_bwd_x(dpre, w, *, name, out_dtype=F32):
    S_, C = dpre.shape
    T = min(CONV_T, S_)
    nt = S_ // T
    ncb = C // 512

    def body(d_ref, h_ref, w_ref, o_ref):
        i = pl.program_id(1)
        halo = jnp.where(i < nt - 1, h_ref[...], 0.0)
        d = d_ref[...]
        w_ = w_ref[...]
        y = d * w_[3:4]
        for j in range(3):
            y = y + _shift_up(d, halo, 3 - j) * w_[j:j + 1]
        o_ref[...] = y.astype(o_ref.dtype)

    return pl.pallas_call(
        body, grid=(ncb, nt),
        in_specs=[pl.BlockSpec((T, 512), lambda c, i: (i, c)),
                  pl.BlockSpec((8, 512), lambda c, i: (jnp.minimum((i + 1) * (T // 8), S_ // 8 - 1), c)),
                  pl.BlockSpec((4, 512), lambda c, i: (0, c))],
        out_specs=pl.BlockSpec((T, 512), lambda c, i: (i, c)),
        out_shape=jax.ShapeDtypeStruct((S_, C), out_dtype), compiler_params=_cp(("parallel", "parallel")), name=name,
    )(dpre, dpre, w)


def ffn_mid_fwd(h, w1, w3, *, name):
    S_, D = h.shape
    F = w1.shape[1]
    tm, tn = _pick(S_, (2048, 1024, 512, 256)), 256

    def body(h_ref, w1_ref, w3_ref, a_ref, u_ref, v_ref):
        hb = h_ref[...]
        u = _dot(hb, w1_ref[...])
        v = _dot(hb, w3_ref[...])
        a_ref[...] = (u * jax.nn.sigmoid(u) * v).astype(a_ref.dtype)
        u_ref[...] = u.astype(u_ref.dtype)
        v_ref[...] = v.astype(v_ref.dtype)

    o = pl.BlockSpec((tm, tn), lambda i, j: (i, j))
    return pl.pallas_call(
        body, grid=(S_ // tm, F // tn),
        in_specs=[pl.BlockSpec((tm, D), lambda i, j: (i, 0)), pl.BlockSpec((D, tn), lambda i, j: (0, j)),
                  pl.BlockSpec((D, tn), lambda i, j: (0, j))],
        out_specs=[o, o, o], out_shape=[jax.ShapeDtypeStruct((S_, F), BF16)] * 3,
        compiler_params=_cp(("parallel", "parallel")), name=name,
    )(h, w1, w3)


def ffn_mid_bwd(u, v, dy, w2, *, name):
    S_, F = u.shape
    D = dy.shape[1]
    tm, tn = _pick(S_, (2048, 1024, 512, 256)), 256

    def body(u_ref, v_ref, dy_ref, w2_ref, du_ref, dv_ref):
        u_ = u_ref[...].astype(F32)
        v_ = v_ref[...].astype(F32)
        da = _dot(dy_ref[...], w2_ref[...], _NT)
        sg = jax.nn.sigmoid(u_)
        dv_ref[...] = (da * (u_ * sg)).astype(dv_ref.dtype)
        du_ref[...] = (da * v_ * (sg * (1.0 + u_ * (1.0 - sg)))).astype(du_ref.dtype)

    o = pl.BlockSpec((tm, tn), lambda i, j: (i, j))
    return pl.pallas_call(
        body, grid=(S_ // tm, F // tn),
        in_specs=[o, o, pl.BlockSpec((tm, D), lambda i, j: (i, 0)), pl.BlockSpec((tn, D), lambda i, j: (j, 0))],
        out_specs=[o, o], out_shape=[jax.ShapeDtypeStruct((S_, F), BF16)] * 2,
        compiler_params=_cp(("parallel", "parallel")), name=name,
    )(u, v, dy, w2)


MESH = pl.DeviceIdType.MESH
ANY = pl.BlockSpec(memory_space=pl.ANY)


def allgather8(x_shard, *, name):
    m_per, n = x_shard.shape

    def body(x_ref, out_ref, send_sems, recv_sems, local_sem):
        x, y, c = lax.axis_index("x"), lax.axis_index("y"), lax.axis_index("c")
        me, sibling = (x, y, c), (x, y, 1 - c)
        chips = [(1 - x, y), (x, 1 - y), (1 - x, 1 - y)]

        def rows(px, py, pc):
            return out_ref.at[pl.ds((4 * px + 2 * py + pc) * m_per, m_per), :]

        def copy(k, block, to, src=None):
            return pltpu.make_async_remote_copy(
                src_ref=rows(*block) if src is None else src, dst_ref=rows(*block),
                send_sem=send_sems.at[k], recv_sem=recv_sems.at[k], device_id=to, device_id_type=MESH)

        mine = pltpu.make_async_copy(x_ref, rows(*me), local_sem)
        mine.start()
        first = [copy(0, me, sibling, src=x_ref)]
        first += [copy(1 + j, me, (*chip, c), src=x_ref) for j, chip in enumerate(chips)]
        for cp in first:
            cp.start()
        passed = [copy(4 + j, (*chip, c), sibling) for j, chip in enumerate(chips)]
        for j, chip in enumerate(chips):
            copy(1 + j, (*chip, c), me).wait_recv()
            passed[j].start()
        copy(0, sibling, me).wait_recv()
        for j, chip in enumerate(chips):
            copy(4 + j, (*chip, 1 - c), me).wait_recv()
        for cp in first + passed:
            cp.wait_send()
        mine.wait()

    return pl.pallas_call(
        body, out_shape=jax.ShapeDtypeStruct((8 * m_per, n), x_shard.dtype),
        in_specs=[pl.BlockSpec(memory_space=pltpu.VMEM)], out_specs=pl.BlockSpec(memory_space=pltpu.VMEM),
        scratch_shapes=[pltpu.SemaphoreType.DMA((7,)), pltpu.SemaphoreType.DMA((7,)), pltpu.SemaphoreType.DMA],
        name=name,
    )(x_shard)


def _chip_peers():
    x, y, c = lax.axis_index("x"), lax.axis_index("y"), lax.axis_index("c")
    return x, y, c, [(1 - x, y), (x, 1 - y), (1 - x, 1 - y)]


def allgather_chips(x_shard, *, name):
    r, cdim = x_shard.shape

    def body(x_ref, out_ref, send_sems, recv_sems, local_sem):
        x, y, c, chips = _chip_peers()
        me = 2 * x + y
        mine = pltpu.make_async_copy(x_ref, out_ref.at[me], local_sem)
        mine.start()
        sends = []
        for k, (px, py) in enumerate(chips):
            cp = pltpu.make_async_remote_copy(src_ref=x_ref, dst_ref=out_ref.at[me], send_sem=send_sems.at[k],
                                              recv_sem=recv_sems.at[k], device_id=(px, py, c), device_id_type=MESH)
            cp.start()
            sends.append(cp)
        for k, (px, py) in enumerate(chips):
            pltpu.make_async_remote_copy(src_ref=x_ref, dst_ref=out_ref.at[2 * px + py], send_sem=send_sems.at[k],
                                         recv_sem=recv_sems.at[k], device_id=(px, py, c), device_id_type=MESH).wait_recv()
        for cp in sends:
            cp.wait_send()
        mine.wait()

    return pl.pallas_call(
        body, out_shape=jax.ShapeDtypeStruct((4, r, cdim), x_shard.dtype), in_specs=[ANY], out_specs=ANY,
        scratch_shapes=[pltpu.SemaphoreType.DMA((3,)), pltpu.SemaphoreType.DMA((3,)), pltpu.SemaphoreType.DMA],
        name=name,
    )(x_shard)


def allgather_chips_2level(x_shard, *, name):
    r, cdim = x_shard.shape
    half = r // 2

    def body(x_ref, out_ref, send_sems, recv_sems, local_sem):
        x, y, c, chips = _chip_peers()
        me = 2 * x + y
        mine_rows = pl.ds(c * half, half)
        other_rows = pl.ds((1 - c) * half, half)
        mine = pltpu.make_async_copy(x_ref, out_ref.at[me], local_sem)
        mine.start()

        def copy(k, slot, rows, to, src=None):
            dst = out_ref.at[slot, rows, :]
            return pltpu.make_async_remote_copy(src_ref=dst if src is None else src, dst_ref=dst, send_sem=send_sems.at[k],
                                                recv_sem=recv_sems.at[k], device_id=to, device_id_type=MESH)

        first = [copy(k, me, mine_rows, (px, py, c), src=x_ref.at[mine_rows, :]) for k, (px, py) in enumerate(chips)]
        for cp in first:
            cp.start()
        passed = [copy(3 + k, 2 * px + py, mine_rows, (x, y, 1 - c)) for k, (px, py) in enumerate(chips)]
        for k, (px, py) in enumerate(chips):
            copy(k, 2 * px + py, mine_rows, (px, py, c)).wait_recv()
            passed[k].start()
        for k, (px, py) in enumerate(chips):
            copy(3 + k, 2 * px + py, other_rows, (x, y, 1 - c)).wait_recv()
        for cp in first + passed:
            cp.wait_send()
        mine.wait()

    return pl.pallas_call(
        body, out_shape=jax.ShapeDtypeStruct((4, r, cdim), x_shard.dtype), in_specs=[ANY], out_specs=ANY,
        scratch_shapes=[pltpu.SemaphoreType.DMA((6,)), pltpu.SemaphoreType.DMA((6,)), pltpu.SemaphoreType.DMA],
        name=name,
    )(x_shard)


def exchange_chips(g, *, name):
    _, r, cdim = g.shape

    def body(g_ref, out_ref, send_sems, recv_sems, local_sem):
        x, y, c, chips = _chip_peers()
        me = 2 * x + y
        mine = pltpu.make_async_copy(g_ref.at[me], out_ref.at[me], local_sem)
        mine.start()
        sends = []
        for k, (px, py) in enumerate(chips):
            cp = pltpu.make_async_remote_copy(src_ref=g_ref.at[2 * px + py], dst_ref=out_ref.at[me], send_sem=send_sems.at[k],
                                              recv_sem=recv_sems.at[k], device_id=(px, py, c), device_id_type=MESH)
            cp.start()
            sends.append(cp)
        for k, (px, py) in enumerate(chips):
            pltpu.make_async_remote_copy(src_ref=g_ref.at[me], dst_ref=out_ref.at[2 * px + py], send_sem=send_sems.at[k],
                                         recv_sem=recv_sems.at[k], device_id=(px, py, c), device_id_type=MESH).wait_recv()
        for cp in sends:
            cp.wait_send()
        mine.wait()

    return pl.pallas_call(
        body, out_shape=jax.ShapeDtypeStruct(g.shape, g.dtype), in_specs=[ANY], out_specs=ANY,
        scratch_shapes=[pltpu.SemaphoreType.DMA((3,)), pltpu.SemaphoreType.DMA((3,)), pltpu.SemaphoreType.DMA],
        name=name,
    )(g)


def swap_sibling(p, *, name):
    def body(p_ref, out_ref, send_sem, recv_sem):
        x, y, c = lax.axis_index("x"), lax.axis_index("y"), lax.axis_index("c")
        cp = pltpu.make_async_remote_copy(src_ref=p_ref, dst_ref=out_ref, send_sem=send_sem, recv_sem=recv_sem,
                                          device_id=(x, y, 1 - c), device_id_type=MESH)
        cp.start()
        cp.wait()

    return pl.pallas_call(
        body, out_shape=jax.ShapeDtypeStruct(p.shape, p.dtype), in_specs=[ANY], out_specs=ANY,
        scratch_shapes=[pltpu.SemaphoreType.DMA, pltpu.SemaphoreType.DMA], name=name,
    )(p)


def swap_other_half(g, *, name):
    n, r, cdim = g.shape
    half = r // 2

    def body(g_ref, out_ref, send_sem, recv_sem):
        x, y, c = lax.axis_index("x"), lax.axis_index("y"), lax.axis_index("c")
        cp = pltpu.make_async_remote_copy(src_ref=g_ref.at[:, pl.ds((1 - c) * half, half), :], dst_ref=out_ref, send_sem=send_sem,
                                          recv_sem=recv_sem, device_id=(x, y, 1 - c), device_id_type=MESH)
        cp.start()
        cp.wait()

    return pl.pallas_call(
        body, out_shape=jax.ShapeDtypeStruct((n, half, cdim), g.dtype), in_specs=[ANY], out_specs=ANY,
        scratch_shapes=[pltpu.SemaphoreType.DMA, pltpu.SemaphoreType.DMA], name=name,
    )(g)


def add_pairs(a, b, *, name, out_dtype):
    n, rows, cdim = a.shape
    t = _pick(rows, (256, 128, 64, 32, 16))

    def body(a_ref, b_ref, o_ref):
        o_ref[...] = (a_ref[...].astype(F32) + b_ref[...].astype(F32)).astype(o_ref.dtype)

    spec = pl.BlockSpec((n, t, cdim), lambda i: (0, i, 0))
    return pl.pallas_call(
        body, grid=(rows // t,), in_specs=[spec, spec], out_specs=spec, out_shape=jax.ShapeDtypeStruct(a.shape, out_dtype),
        compiler_params=_cp(("parallel",)), name=name,
    )(a, b)


def sum_slots(r, *, name):
    n, rows, cdim = r.shape
    t = _pick(rows, (256, 128, 64, 32, 16, 8))

    def body(r_ref, o_ref):
        acc = r_ref[0].astype(F32)
        for s in range(1, n):
            acc = acc + r_ref[s].astype(F32)
        o_ref[...] = acc

    return pl.pallas_call(
        body, grid=(rows // t,), in_specs=[pl.BlockSpec((n, t, cdim), lambda i: (0, i, 0))],
        out_specs=pl.BlockSpec((t, cdim), lambda i: (i, 0)), out_shape=jax.ShapeDtypeStruct((rows, cdim), F32),
        compiler_params=_cp(("parallel",)), name=name,
    )(r)


def _rms(x, g):
    return x * lax.rsqrt(jnp.mean(x * x, axis=-1, keepdims=True) + NORM_EPS) * g


def _adaln(x, g, shift, scale):
    return _rms(x, g) * (1.0 + scale) + shift


def _silu(x):
    return x * jax.nn.sigmoid(x)


def _gdn_gate(o, z, g):
    return jnp.concatenate([_rms(o[:, 128 * h:128 * (h + 1)], g) * _silu(z[:, 128 * h:128 * (h + 1)]) for h in range(4)], axis=1)


def _ssd_gate(y, z0, z1, z2, z3, g):
    outs = []
    for k, z in enumerate((z0, z1, z2, z3)):
        t = y[:, 512 * k:512 * (k + 1)] * _silu(z)
        outs.append(t * lax.rsqrt(jnp.mean(t * t, axis=-1, keepdims=True) + NORM_EPS))
    return jnp.concatenate(outs, axis=1) * g


def _rope(x, cos, sin, rot):
    return x * cos + _dot(x, rot, hi=True) * sin


def _rope_q(q, cos, sin, rot):
    parts = []
    for h in range(4):
        parts += [q[:, 256 * h:256 * h + 128], _rope(q[:, 256 * h + 128:256 * (h + 1)], cos, sin, rot)]
    return jnp.concatenate(parts, axis=1) * ATT_SCALE


def _rope_t(d, cos, sin, rot):
    return d * cos + _dot(d * sin, rot, _NT, hi=True)


def _vjp_rows(fn, n_rows, n_pars, out_dtypes, rows, cts, pars, *, name, tile=512, extra=None):
    nct = len(cts)

    def bwd(*a):
        r, c, e, p = a[:n_rows], a[n_rows:n_rows + nct], a[n_rows + nct:len(a) - n_pars], a[len(a) - n_pars:]
        out, vjp = jax.vjp(fn, *[t.astype(F32) for t in r], *p)
        ct = tuple(t.astype(F32) for t in c)
        grads = vjp(ct[0] if not isinstance(out, tuple) else ct)
        drows = list(grads[:n_rows])
        if e:
            drows[0] = drows[0] + e[0]
        return (*drows, *grads[n_rows:])

    return rowmap(bwd, list(rows) + list(cts) + ([extra] if extra is not None else []), list(pars), out_dtypes,
                  name=name, tile=tile, n_reduce=n_pars)


def _gate_grads(dw_raw, w, gate, scale, *, name):
    dw, dg = rowmap(lambda r, w_, gt: ((scale * gt) * r, jnp.sum((scale * w_.astype(F32)) * r, axis=0, keepdims=True)),
                    [dw_raw, w], [gate], (F32, F32), name=name, tile=256, n_reduce=1)
    return dw, dg[0]


ADAM_LR, ADAM_B1, ADAM_B2, ADAM_EPS, ADAM_WD, ADAM_STEP = 0.001, 0.9, 0.999, 1e-08, 0.01, 10


def _adam_math(w, g, m, v):
    m = ADAM_B1 * m + (1.0 - ADAM_B1) * g
    v = ADAM_B2 * v + (1.0 - ADAM_B2) * (g * g)
    m_hat = m / (1.0 - ADAM_B1 ** ADAM_STEP)
    v_hat = v / (1.0 - ADAM_B2 ** ADAM_STEP)
    delta = -ADAM_LR * (m_hat / (jnp.sqrt(v_hat) + ADAM_EPS) + ADAM_WD * w)
    return delta, m, v


def adamw(w, gs, m, v, *, name):
    shape = w.shape
    last = shape[-1]
    to2 = lambda a: a.reshape(-1, last)
    rows = w.size // last
    tile = _pick(rows, (256, 128, 64, 32, 16, 8))
    ng = len(gs)

    def fn(w_, *rest):
        g = rest[0]
        for t in rest[1:ng]:
            g = g + t
        m_, v_ = rest[ng], rest[ng + 1]
        return (g, *_adam_math(w_, g, m_, v_))

    outs = rowmap(fn, [to2(w)] + [to2(g) for g in gs] + [to2(m), to2(v)], [], (F32,) * 4, name=name, tile=tile)
    return tuple(o.reshape(shape) for o in outs)


PACK_W = 1024
BIG = (
    ("ffn_w1", (4, 2, 1024, 704), 3), ("ffn_w3", (4, 2, 1024, 704), 3), ("ffn_w2", (4, 2, 704, 1024), 2),
    ("ev_w_in", (2, 1024, 690), 2), ("mla_w_uq", (2, 96, 4, 192), 1), ("mla_w_ukv", (2, 64, 4, 256), 1),
    ("ev_w_out", (2, 256, 1024), 1), ("ssd_w_in", (2, 1024, 1288), 2), ("ssd_w_out", (2, 512, 1024), 1))


def _seg_rows(shape):
    n = math.prod(shape)
    return -(-n // (16 * PACK_W)) * 16


PACK_ROWS = -(-sum(_seg_rows(sh) for _, sh, _ in BIG) // 512) * 512


def _pack(shards, dtype):
    parts = []
    for (_, shape, _), a in zip(BIG, shards):
        flat = a.reshape(-1).astype(dtype)
        pad = _seg_rows(shape) * PACK_W - flat.shape[0]
        parts.append(jnp.pad(flat, (0, pad)) if pad else flat)
    tail = PACK_ROWS - sum(_seg_rows(sh) for _, sh, _ in BIG)
    if tail:
        parts.append(jnp.zeros((tail * PACK_W,), dtype))
    return jnp.concatenate(parts).reshape(-1, PACK_W)


def _pack_by_owner(fulls):
    cols = []
    for (_, shape, ax), f in zip(BIG, fulls):
        blk = jnp.stack([lax.slice_in_dim(f, s * shape[ax], (s + 1) * shape[ax], axis=ax).reshape(-1).astype(BF16) for s in range(4)])
        pad = _seg_rows(shape) * PACK_W - blk.shape[1]
        cols.append((jnp.pad(blk, ((0, 0), (0, pad))) if pad else blk).reshape(4, -1, PACK_W))
    tail = PACK_ROWS - sum(_seg_rows(sh) for _, sh, _ in BIG)
    if tail:
        cols.append(jnp.zeros((4, tail, PACK_W), BF16))
    return jnp.concatenate(cols, axis=1)


def _unpack(buf):
    out, r0 = [], 0
    for _, shape, _ in BIG:
        n = math.prod(shape)
        out.append(buf[r0:r0 + _seg_rows(shape)].reshape(-1)[:n].reshape(shape))
        r0 += _seg_rows(shape)
    return out


SMALL_SHARDED = (
    ("norm_g", (4, 3, 256), 2), ("gdn_conv_w", (2, 4, 384), 2), ("ssd_conv_w", (2, 4, 768), 2),
    ("ssd_conv_b", (2, 768), 1), ("ssd_norm_g", (2, 512), 1))


def _flat_pack(arrs, width, row_mult):
    flat = jnp.concatenate([a.reshape(-1).astype(F32) for a in arrs])
    n = flat.shape[0]
    tot = -(-n // (width * row_mult)) * width * row_mult
    return jnp.pad(flat, (0, tot - n)).reshape(-1, width)


def _flat_unpack(buf, shapes):
    flat = buf.reshape(-1)
    out, o = [], 0
    for s in shapes:
        n = math.prod(s)
        out.append(flat[o:o + n].reshape(s))
        o += n
    return out


def _rep(v, n):
    return jnp.repeat(v, n, axis=-1)


def kernel(x, c, positions, ada_w, ada_b, norm_g, ffn_w1, ffn_w3, ffn_w2, ev_w_in, gdn_conv_w, gdn_A_log, gdn_dt_bias, gdn_norm_g, mla_q_norm_g, mla_w_uq, mla_kv_norm_g, mla_w_ukv, ev_w_out, ssd_w_in, ssd_conv_w, ssd_conv_b, ssd_A_log, ssd_dt_bias, ssd_D, ssd_norm_g, ssd_w_out, final_g, loss_target, m_ada_w, m_ada_b, m_norm_g, m_ffn_w1, m_ffn_w3, m_ffn_w2, m_ev_w_in, m_gdn_conv_w, m_gdn_A_log, m_gdn_dt_bias, m_gdn_norm_g, m_mla_q_norm_g, m_mla_w_uq, m_mla_kv_norm_g, m_mla_w_ukv, m_ev_w_out, m_ssd_w_in, m_ssd_conv_w, m_ssd_conv_b, m_ssd_A_log, m_ssd_dt_bias, m_ssd_D, m_ssd_norm_g, m_ssd_w_out, m_final_g, v_ada_w, v_ada_b, v_norm_g, v_ffn_w1, v_ffn_w3, v_ffn_w2, v_ev_w_in, v_gdn_conv_w, v_gdn_A_log, v_gdn_dt_bias, v_gdn_norm_g, v_mla_q_norm_g, v_mla_w_uq, v_mla_kv_norm_g, v_mla_w_ukv, v_ev_w_out, v_ssd_w_in, v_ssd_conv_w, v_ssd_conv_b, v_ssd_A_log, v_ssd_dt_bias, v_ssd_D, v_ssd_norm_g, v_ssd_w_out, v_final_g):
    P = dict(ada_w=ada_w, ada_b=ada_b, norm_g=norm_g, ffn_w1=ffn_w1, ffn_w3=ffn_w3, ffn_w2=ffn_w2, ev_w_in=ev_w_in, gdn_conv_w=gdn_conv_w, gdn_A_log=gdn_A_log, gdn_dt_bias=gdn_dt_bias, gdn_norm_g=gdn_norm_g, mla_q_norm_g=mla_q_norm_g, mla_w_uq=mla_w_uq, mla_kv_norm_g=mla_kv_norm_g, mla_w_ukv=mla_w_ukv, ev_w_out=ev_w_out, ssd_w_in=ssd_w_in, ssd_conv_w=ssd_conv_w, ssd_conv_b=ssd_conv_b, ssd_A_log=ssd_A_log, ssd_dt_bias=ssd_dt_bias, ssd_D=ssd_D, ssd_norm_g=ssd_norm_g, ssd_w_out=ssd_w_out, final_g=final_g)
    M1 = dict(ada_w=m_ada_w, ada_b=m_ada_b, norm_g=m_norm_g, ffn_w1=m_ffn_w1, ffn_w3=m_ffn_w3, ffn_w2=m_ffn_w2, ev_w_in=m_ev_w_in, gdn_conv_w=m_gdn_conv_w, gdn_A_log=m_gdn_A_log, gdn_dt_bias=m_gdn_dt_bias, gdn_norm_g=m_gdn_norm_g, mla_q_norm_g=m_mla_q_norm_g, mla_w_uq=m_mla_w_uq, mla_kv_norm_g=m_mla_kv_norm_g, mla_w_ukv=m_mla_w_ukv, ev_w_out=m_ev_w_out, ssd_w_in=m_ssd_w_in, ssd_conv_w=m_ssd_conv_w, ssd_conv_b=m_ssd_conv_b, ssd_A_log=m_ssd_A_log, ssd_dt_bias=m_ssd_dt_bias, ssd_D=m_ssd_D, ssd_norm_g=m_ssd_norm_g, ssd_w_out=m_ssd_w_out, final_g=m_final_g)
    M2 = dict(ada_w=v_ada_w, ada_b=v_ada_b, norm_g=v_norm_g, ffn_w1=v_ffn_w1, ffn_w3=v_ffn_w3, ffn_w2=v_ffn_w2, ev_w_in=v_ev_w_in, gdn_conv_w=v_gdn_conv_w, gdn_A_log=v_gdn_A_log, gdn_dt_bias=v_gdn_dt_bias, gdn_norm_g=v_gdn_norm_g, mla_q_norm_g=v_mla_q_norm_g, mla_w_uq=v_mla_w_uq, mla_kv_norm_g=v_mla_kv_norm_g, mla_w_ukv=v_mla_w_ukv, ev_w_out=v_ev_w_out, ssd_w_in=v_ssd_w_in, ssd_conv_w=v_ssd_conv_w, ssd_conv_b=v_ssd_conv_b, ssd_A_log=v_ssd_A_log, ssd_dt_bias=v_ssd_dt_bias, ssd_D=v_ssd_D, ssd_norm_g=v_ssd_norm_g, ssd_w_out=v_ssd_w_out, final_g=v_final_g)
    names = list(P)
    xi, yi, ci = lax.axis_index("x"), lax.axis_index("y"), lax.axis_index("c")
    chip = 2 * xi + yi
    bidx = 4 * xi + 2 * yi + ci
    xa = x[0]
    S_, D = xa.shape
    tgt = loss_target[0]
    depth = ffn_w1.shape[0]

    wg = allgather_chips_2level(_pack([P[n] for n, _, _ in BIG], BF16), name="gather_weights")
    per_chip = [_unpack(wg[s]) for s in range(4)]
    W = {n: jnp.concatenate([per_chip[s][k] for s in range(4)], axis=ax) for k, (n, _, ax) in enumerate(BIG)}
    sg = allgather_chips(_flat_pack([P[n] for n, _, _ in SMALL_SHARDED], 1024, 16), name="gather_small")
    per_chip_s = [_flat_unpack(sg[s], [sh for _, sh, _ in SMALL_SHARDED]) for s in range(4)]
    Wf = {n: jnp.concatenate([per_chip_s[s][k] for s in range(4)], axis=ax) for k, (n, _, ax) in enumerate(SMALL_SHARDED)}

    c_all = allgather8(jnp.pad(c, ((0, 7), (0, 0))), name="gather_c").reshape(8, 8, D)[:, 0]
    c_act, = rowmap(lambda t: (_silu(t),), [jnp.pad(c_all, ((0, 8), (0, 0)))], [], (F32,), name="c_act", tile=16)
    ncol = ada_w.shape[2]
    ada_b_loc = lax.dynamic_slice(ada_b, (0, chip * ncol), (depth, ncol))
    mod_loc = [mm((c_act, ada_w[l]), name=f"mod_{l}", epi=lambda acc, b: (acc + b,), epi_pars=(ada_b_loc[l][None],),
                  epi_out_dtypes=(F32,), tm=16, tn=256)[0][:8] for l in range(depth)]
    mod_g = allgather8(jnp.stack(mod_loc).reshape(-1, 1024), name="gather_mod").reshape(8, depth, 8, ncol)
    mod_b = lax.dynamic_index_in_dim(mod_g[0::2], bidx, axis=2, keepdims=False)
    mod = jnp.transpose(mod_b, (1, 0, 2)).reshape(depth, 3, 3, D)

    def ev_ext(w):
        z = lambda n: jnp.zeros((w.shape[0], n), w.dtype)
        return jnp.concatenate([w[:, :2048], _rep(w[:, 2048:2052], 128), _rep(w[:, 2052:2056], 128), w[:, 2056:2440], z(128),
                                w[:, 2440:2696], w[:, 2696:2760], z(192)], axis=1)

    def ev_ext_t(dw):
        return jnp.concatenate([dw[:, :2048], dw[:, 2048:2560].reshape(-1, 4, 128).sum(-1), dw[:, 2560:3072].reshape(-1, 4, 128).sum(-1),
                                dw[:, 3072:3456], dw[:, 3584:3840], dw[:, 3840:3904]], axis=1)

    def od_ext(w):
        return jnp.concatenate([w[:, 2048:5120], w[:, :2048], _rep(w[:, 5120:5152], 64)], axis=1)

    def od_ext_t(dw):
        return jnp.concatenate([dw[:, 3072:5120], dw[:, :3072], dw[:, 5120:].reshape(-1, 32, 64).sum(-1)], axis=1)

    def wq_ext(w):
        return jnp.pad(w, ((0, 0), (0, 0), (0, 64))).reshape(384, 1024)

    def wq_ext_t(dw):
        return dw.reshape(384, 4, 256)[:, :, :192]

    def wkv_ext(w):
        return jnp.concatenate([w[:, :, :128].reshape(256, 512), w[:, :, 128:].reshape(256, 512)], axis=1)

    def wkv_ext_t(dw):
        return jnp.concatenate([dw[:, :512].reshape(256, 4, 128), dw[:, 512:].reshape(256, 4, 128)], axis=2)

    half = 32
    inv_freq = 10000.0 ** (-jnp.arange(half, dtype=F32) / half)
    ang = positions[0].astype(F32)[:, None] * inv_freq
    zpad = jnp.zeros((S_, 64), F32)
    cos_t = jnp.concatenate([jnp.cos(ang), jnp.cos(ang), zpad], axis=1)
    sin_t = jnp.concatenate([jnp.sin(ang), jnp.sin(ang), zpad], axis=1)
    ii = jnp.arange(128)
    rot = (jnp.where((ii[:, None] < 32) & (ii[None, :] == ii[:, None] + 32), 1.0, 0.0)
           - jnp.where((ii[:, None] >= 32) & (ii[:, None] < 64) & (ii[None, :] == ii[:, None] - 32), 1.0, 0.0)).astype(F32)

    grads = {}
    dmod = [[[None] * 3 for _ in range(3)] for _ in range(depth)]
    dnorm_g = [[None] * 3 for _ in range(depth)]

    def acc(name, idx, val):
        grads.setdefault(name, {})[idx] = val

    def ffn_sub(xin, l, k, j):
        g, (shift, scale, gate) = Wf["norm_g"][l, k][None], [mod[l, k, t][None] for t in range(3)]
        w1, w3, w2 = W["ffn_w1"][l, j], W["ffn_w3"][l, j], W["ffn_w2"][l, j]
        tag = f"l{l}f{j}"
        h, = rowmap(lambda *a: (_adaln(*a),), [xin], [g, shift, scale], (BF16,), name=f"adaln_{tag}")
        a, u16, v16 = ffn_mid_fwd(h, w1, w3, name=f"ffn_mid_{tag}")
        xn, = mm((a, w2), name=f"ffn_out_{tag}", epi=lambda acc_, xr, gt: (xr + 0.5 * gt * acc_,), epi_rows=(xin,),
                 epi_pars=(gate,), epi_out_dtypes=(F32,))

        def bwd(dxn):
            du, dv = ffn_mid_bwd(u16, v16, dxn, ((0.5 * gate) * w2).astype(BF16), name=f"ffn_midb_{tag}")
            dh = mm([(du, w1), (dv, w3)], tb=True, name=f"ffn_dh_{tag}")
            acc("ffn_w1", (l, j), mm((h, du), ta=True, name=f"ffn_dw1_{tag}"))
            acc("ffn_w3", (l, j), mm((h, dv), ta=True, name=f"ffn_dw3_{tag}"))
            dw2, dgate = _gate_grads(mm((a, dxn), ta=True, name=f"ffn_dw2_{tag}"), w2, gate, 0.5, name=f"ffn_dgate_{tag}")
            acc("ffn_w2", (l, j), dw2)
            dx, dg, dsh, dsc = _vjp_rows(_adaln, 1, 3, (F32,), [xin], [dh], [g, shift, scale], name=f"adalnb_{tag}", extra=dxn)
            dnorm_g[l][k] = dg[0]
            dmod[l][k] = [dsh[0], dsc[0], dgate]
            return dx

        return xn, bwd

    def mixer_tail(xin, l, tag, dh, dxn, g, shift, scale, dgate):
        dx, dg, dsh, dsc = _vjp_rows(_adaln, 1, 3, (F32,), [xin], [dh], [g, shift, scale], name=f"adalnb_{tag}", extra=dxn)
        dnorm_g[l][1] = dg[0]
        dmod[l][1] = [dsh[0], dsc[0], dgate]
        return dx

    def even_sub(xin, l):
        e = l // 2
        tag = f"l{l}m"
        g, (shift, scale, gate) = Wf["norm_g"][l, 1][None], [mod[l, 1, t][None] for t in range(3)]
        wext, wq, wkv, wout = ev_ext(W["ev_w_in"][e]), wq_ext(W["mla_w_uq"][e]), wkv_ext(W["mla_w_ukv"][e]), W["ev_w_out"][e]
        conv_w, zb = Wf["gdn_conv_w"][e], jnp.zeros((1, 1536), F32)
        alog_e, dtb_e = _rep(gdn_A_log[e], 128)[None], _rep(gdn_dt_bias[e], 128)[None]
        gg, qg, kvg = gdn_norm_g[e][None], mla_q_norm_g[e][None], mla_kv_norm_g[e][None]
        h, = rowmap(lambda *a: (_adaln(*a),), [xin], [g, shift, scale], (BF16,), name=f"adaln_{tag}")
        proj = mm((h, wext), name=f"ev_in_{tag}")
        qkvc = conv_fwd(proj, 0, 3, conv_w, zb, name=f"gdn_conv_{tag}")
        o_g, hist = gdn_fwd(qkvc, proj, 4, 5, alog_e, dtb_e, name=f"gdn_{tag}")
        o_a, = rowmap(lambda o, z, g_: (_gdn_gate(o, z, g_),), [o_g, (proj, 512, 3)], [gg], (BF16,), name=f"gdn_gate_{tag}")
        cqn, = rowmap(lambda t, g_: (_rms(t, g_),), [(proj, 384, 8)], [qg], (BF16,), name=f"q_norm_{tag}")
        ckvn, = rowmap(lambda t, g_: (_rms(t, g_),), [(proj, 256, 14)], [kvg], (BF16,), name=f"kv_norm_{tag}")
        q0 = mm((cqn, wq), name=f"q_up_{tag}")
        kv = mm((ckvn, wkv), name=f"kv_up_{tag}", out_dtype=BF16)
        q, = rowmap(lambda t, cs, sn, r: (_rope_q(t, cs, sn, r),), [q0, cos_t, sin_t], [rot], (BF16,), name=f"rope_q_{tag}")
        kp, = rowmap(lambda t, cs, sn, r: (_rope(t, cs, sn, r),), [(proj, 128, 30), cos_t, sin_t], [rot], (BF16,), name=f"rope_k_{tag}")
        o_b, lse = att_fwd(q, kv, kp, name=f"att_{tag}")
        xn, = mm([(o_a, wout[:512]), (o_b, wout[512:])], name=f"ev_out_{tag}", epi=lambda acc_, xr, gt: (xr + gt * acc_,),
                 epi_rows=(xin,), epi_pars=(gate,), epi_out_dtypes=(F32,))

        def bwd(dxn):
            wout_g = (gate * wout).astype(BF16)
            do_a = mm((dxn, wout_g[:512]), tb=True, name=f"ev_doa_{tag}")
            do_b = mm((dxn, wout_g[512:]), tb=True, name=f"ev_dob_{tag}")
            dw_raw = jnp.concatenate([mm((o_a, dxn), ta=True, name=f"ev_dwoa_{tag}"), mm((o_b, dxn), ta=True, name=f"ev_dwob_{tag}")], axis=0)
            dwo, dgate = _gate_grads(dw_raw, wout, gate, 1.0, name=f"ev_dgate_{tag}")
            acc("ev_w_out", e, dwo)
            dsum, = rowmap(lambda d, o_: (jnp.concatenate([jnp.broadcast_to(jnp.sum(d[:, 128 * hh:128 * (hh + 1)] * o_[:, 128 * hh:128 * (hh + 1)],
                                                                                        axis=-1, keepdims=True), (d.shape[0], 128))
                                                  for hh in range(4)], axis=1),), [do_b, o_b], [], (F32,), name=f"att_dsum_{tag}")
            dq4, dk2, dv = att_bwd(q, kv, kp, lse, dsum, do_b, name=f"att_bwd_{tag}")

            def rope_qb(d0, d1, d2, d3, cs, sn, r):
                parts = []
                for d in (d0, d1, d2, d3):
                    parts += [d[:, :128], _rope_t(d[:, 128:], cs, sn, r)]
                return (jnp.concatenate(parts, axis=1) * ATT_SCALE,)

            dq0, = rowmap(rope_qb, [dq4[0], dq4[1], dq4[2], dq4[3], cos_t, sin_t], [rot], (BF16,), name=f"rope_qb_{tag}")

            def rope_kb(d, cs, sn, r):
                dkp = d[:, 128:256] + d[:, 384:512] + d[:, 640:768] + d[:, 896:1024]
                return jnp.concatenate([d[:, 256 * hh:256 * hh + 128] for hh in range(4)], axis=1), _rope_t(dkp, cs, sn, r)

            dkn, dkr = rowmap(rope_kb, [dk2, cos_t, sin_t], [rot], (BF16, BF16), name=f"rope_kb_{tag}")
            dcqn = mm((dq0, wq), tb=True, name=f"q_upb_{tag}")
            acc("mla_w_uq", e, wq_ext_t(mm((cqn, dq0), ta=True, name=f"q_dw_{tag}")))
            dckvn = mm([(dkn, wkv[:, :512]), (dv, wkv[:, 512:])], tb=True, name=f"kv_upb_{tag}")
            acc("mla_w_ukv", e, wkv_ext_t(jnp.concatenate([mm((ckvn, dkn), ta=True, name=f"kv_dwk_{tag}"), mm((ckvn, dv), ta=True, name=f"kv_dwv_{tag}")], axis=1)))
            dcq, dqg = _vjp_rows(_rms, 1, 1, (BF16,), [(proj, 384, 8)], [dcqn], [qg], name=f"q_normb_{tag}")
            dckv, dkvg = _vjp_rows(_rms, 1, 1, (BF16,), [(proj, 256, 14)], [dckvn], [kvg], name=f"kv_normb_{tag}")
            acc("mla_q_norm_g", e, dqg[0])
            acc("mla_kv_norm_g", e, dkvg[0])
            do_g, dz, dgg = _vjp_rows(_gdn_gate, 2, 1, (F32, BF16), [o_g, (proj, 512, 3)], [do_a], [gg], name=f"gdn_gateb_{tag}")
            acc("gdn_norm_g", e, dgg[0])
            dqkvc, dbe, dae, dal, ddt = gdn_bwd(qkvc, proj, 4, 5, alog_e, dtb_e, hist, do_g, name=f"gdnb_{tag}")
            acc("gdn_A_log", e, dal.reshape(4, 128).sum(-1))
            acc("gdn_dt_bias", e, ddt.reshape(4, 128).sum(-1))
            dpre, dcw, _ = conv_bwd_pre(proj, 0, 3, conv_w, zb, dqkvc, name=f"gdn_convb_{tag}")
            acc("gdn_conv_w", e, dcw)
            dqkv = conv_bwd_x(dpre, conv_w, name=f"gdn_convx_{tag}", out_dtype=BF16)
            zc = lambda n: jnp.zeros((S_, n), BF16)
            dproj = jnp.concatenate([dqkv, dz, dbe, dae, dcq, zc(128), dckv, dkr, zc(128)], axis=1)
            dh = mm((dproj, wext), tb=True, name=f"ev_inb_{tag}")
            acc("ev_w_in", e, ev_ext_t(mm((h, dproj), ta=True, name=f"ev_dwin_{tag}")))
            return mixer_tail(xin, l, tag, dh, dxn, g, shift, scale, dgate)

        return xn, bwd

    def odd_sub(xin, l):
        o = l // 2
        tag = f"l{l}m"
        g, (shift, scale, gate) = Wf["norm_g"][l, 1][None], [mod[l, 1, t][None] for t in range(3)]
        wext, wout = od_ext(W["ssd_w_in"][o]), W["ssd_w_out"][o]
        conv_w, conv_b, ng = Wf["ssd_conv_w"][o], Wf["ssd_conv_b"][o][None], Wf["ssd_norm_g"][o][None]
        ex = lambda v: _rep(v, 64)[None]
        na_e, dtb_e, dsk_e = ex(-jnp.exp(ssd_A_log[o])), ex(ssd_dt_bias[o]), ex(ssd_D[o])
        h, = rowmap(lambda *a: (_adaln(*a),), [xin], [g, shift, scale], (BF16,), name=f"adaln_{tag}")
        proj = mm((h, wext), name=f"ssd_in_{tag}")
        zv = [(proj, 512, 6 + t) for t in range(4)]
        xbc = conv_fwd(proj, 0, 6, conv_w, conv_b, name=f"ssd_conv_{tag}")
        ys, hist = ssd_fwd(xbc, proj, na_e, dtb_e, dsk_e, name=f"ssd_{tag}")
        yn, = rowmap(lambda *a: (_ssd_gate(*a),), [ys] + zv, [ng], (BF16,), name=f"ssd_gate_{tag}", tile=256)
        xn, = mm((yn, wout), name=f"ssd_out_{tag}", epi=lambda acc_, xr, gt: (xr + gt * acc_,), epi_rows=(xin,),
                 epi_pars=(gate,), epi_out_dtypes=(F32,))

        def bwd(dxn):
            dyn = mm((dxn, (gate * wout).astype(BF16)), tb=True, name=f"ssd_dyn_{tag}", out_dtype=BF16)
            dwo, dgate = _gate_grads(mm((yn, dxn), ta=True, name=f"ssd_dwout_{tag}"), wout, gate, 1.0, name=f"ssd_dgate_{tag}")
            acc("ssd_w_out", o, dwo)
            dys, dz0, dz1, dz2, dz3, dng = _vjp_rows(_ssd_gate, 5, 1, (F32, BF16, BF16, BF16, BF16), [ys] + zv, [dyn], [ng],
                                                     name=f"ssd_gateb_{tag}", tile=256)
            acc("ssd_norm_g", o, dng[0])
            dxs, ddtx, db_, dc_, dna, ddtb, ddsk = ssd_bwd(xbc, proj, na_e, dtb_e, dsk_e, hist, dys, name=f"ssdb_{tag}")
            acc("ssd_A_log", o, dna.reshape(32, 64).sum(-1) * (-jnp.exp(ssd_A_log[o])))
            acc("ssd_dt_bias", o, ddtb.reshape(32, 64).sum(-1))
            acc("ssd_D", o, ddsk.reshape(32, 64).sum(-1))
            dxp, dcws, dcbs = [], [], []
            for part, (cb0, ncb, dpart) in enumerate(((0, 4, dxs), (4, 1, db_), (5, 1, dc_))):
                cols = slice(512 * cb0, 512 * (cb0 + ncb))
                dpre, dcw, dcb = conv_bwd_pre(proj, cb0, ncb, conv_w[:, cols], conv_b[:, cols], dpart, name=f"ssd_convb{part}_{tag}")
                dxp.append(conv_bwd_x(dpre, conv_w[:, cols], name=f"ssd_convx{part}_{tag}", out_dtype=BF16))
                dcws.append(dcw)
                dcbs.append(dcb[0])
            acc("ssd_conv_w", o, jnp.concatenate(dcws, axis=1))
            acc("ssd_conv_b", o, jnp.concatenate(dcbs))
            dproj = jnp.concatenate(dxp + [dz0, dz1, dz2, dz3, ddtx], axis=1)
            dh = mm((dproj, wext), tb=True, name=f"ssd_inb_{tag}")
            acc("ssd_w_in", o, od_ext_t(mm((h, dproj), ta=True, name=f"ssd_dwin_{tag}")))
            return mixer_tail(xin, l, tag, dh, dxn, g, shift, scale, dgate)

        return xn, bwd

    tape = []
    xc = xa
    for l in range(depth):
        xc, b0 = ffn_sub(xc, l, 0, 0)
        xc, b1 = (even_sub if l % 2 == 0 else odd_sub)(xc, l)
        xc, b2 = ffn_sub(xc, l, 2, 1)
        tape += [b0, b1, b2]

    def head(xr, tg, g_):
        def f(xv, gv):
            err = _rms(xv, gv) - tg
            return 0.5 * jnp.sum(jnp.mean(err * err, axis=-1, keepdims=True), axis=0, keepdims=True)
        lo, vjp = jax.vjp(f, xr, g_)
        dxv, dgv = vjp(jnp.ones_like(lo))
        return dxv, jnp.broadcast_to(lo, (1, 128)), dgv

    dx, loss_p, dfg = rowmap(head, [xc, tgt], [final_g[None]], (F32,), name="loss_head", n_reduce=2)
    loss = lax.psum(loss_p[0, 0], ("x", "y", "c"))

    for b in reversed(tape):
        dx = b(dx)
    grad_x = dx[None]

    full = {n: jnp.stack([grads[n][k] for k in sorted(grads[n])]) for n in ("ev_w_in", "mla_w_uq", "mla_w_ukv", "ev_w_out", "ssd_w_in", "ssd_w_out")}
    for n in ("ffn_w1", "ffn_w3", "ffn_w2"):
        full[n] = jnp.stack([jnp.stack([grads[n][(l, j)] for j in range(2)]) for l in range(depth)])
    gall = _pack_by_owner([full[n] for n, _, _ in BIG])
    half = PACK_ROWS // 2
    from_sib = swap_other_half(gall, name="swap_half")
    pair = add_pairs(lax.dynamic_slice_in_dim(gall, ci * half, half, axis=1), from_sib, name="add_sibling", out_dtype=BF16)
    recv = exchange_chips(pair, name="exchange_grads")
    part = sum_slots(recv, name="sum_chips")
    sib = swap_sibling(part, name="swap_sibling")
    lo = jnp.where(ci == 0, part, sib)
    hi_ = jnp.where(ci == 0, sib, part)
    g_tot = _unpack(jnp.concatenate([lo, hi_], axis=0))

    dmod_flat = jnp.stack([jnp.stack([jnp.stack(dmod[l][k]) for k in range(3)]) for l in range(depth)]).reshape(depth, 9 * D)
    small_names = ["norm_g", "gdn_conv_w", "gdn_A_log", "gdn_dt_bias", "gdn_norm_g", "mla_q_norm_g", "mla_kv_norm_g",
                   "ssd_conv_w", "ssd_conv_b", "ssd_A_log", "ssd_dt_bias", "ssd_D", "ssd_norm_g", "final_g"]
    small_full = {n: jnp.stack([grads[n][k] for k in sorted(grads[n])]) for n in small_names if n in grads}
    small_full["norm_g"] = jnp.stack([jnp.stack(dnorm_g[l]) for l in range(depth)])
    small_full["final_g"] = dfg[0]
    small_list = [dmod_flat] + [small_full[n] for n in small_names]
    small_shapes = [a.shape for a in small_list]
    sp = _flat_pack(small_list, 128, 8)
    sgath = allgather8(sp, name="gather_small_grads").reshape(8, sp.shape[0], 128)
    ssum = sum_slots(sgath, name="sum_small")
    tot = dict(zip(["ada_b"] + small_names, _flat_unpack(ssum, small_shapes)))
    dmod_all = sgath.reshape(8, -1)[:, :depth * 9 * D].reshape(8, depth, 9 * D)
    dmod_loc = lax.dynamic_slice(dmod_all, (0, 0, chip * ncol), (8, depth, ncol))
    g_ada_w = jnp.stack([mm((c_act, jnp.pad(dmod_loc[:, l], ((0, 8), (0, 0)))), ta=True, name=f"ada_dw_{l}", tk=16, tn=256)
                         for l in range(depth)])

    def own(n, a):
        for m_, sh, ax in SMALL_SHARDED:
            if m_ == n:
                return lax.dynamic_slice_in_dim(a, chip * sh[ax], sh[ax], axis=ax)
        return a

    res = {}
    for k, (n, _, _) in enumerate(BIG):
        res[n] = adamw(P[n], [g_tot[k]], M1[n], M2[n], name=f"adamw_{n}")
    res["ada_w"] = adamw(ada_w, [g_ada_w], m_ada_w, v_ada_w, name="adamw_ada_w")
    sm = ["ada_b"] + small_names
    shapes = [P[n].shape for n in sm]
    pk = lambda d: _flat_pack([d[n] for n in sm], 128, 8)
    outs = adamw(pk(P), [pk({n: own(n, tot[n]).reshape(P[n].shape) for n in sm})], pk(M1), pk(M2), name="adamw_small")
    un = [_flat_unpack(o, shapes) for o in outs]
    for i, n in enumerate(sm):
        res[n] = tuple(un[t][i] for t in range(4))
    return (loss, grad_x, *[res[n][0] for n in names], *[res[n][1] for n in names], *[res[n][2] for n in names], *[res[n][3] for n in names])
```

```python
import functools
import math

import jax
import jax.numpy as jnp
from jax import lax
from jax.experimental import pallas as pl
from jax.experimental.pallas import tpu as pltpu

F32 = jnp.float32
BF16 = jnp.bfloat16
HI = lax.Precision.HIGHEST
HI3 = lax.Precision.HIGH
VMEM_LIMIT = 56 * 1024 * 1024
NORM_EPS = 1e-6
MM_VMEM_BUDGET = 40 * 1024 * 1024


def _cp(sem=None):
    if sem is None:
        return pltpu.CompilerParams(vmem_limit_bytes=VMEM_LIMIT)
    return pltpu.CompilerParams(dimension_semantics=sem, vmem_limit_bytes=VMEM_LIMIT)


def _pick(dim, prefs):
    for p in prefs:
        if dim % p == 0:
            return p
    return dim


def mm(pairs, *, ta=False, tb=False, out_dtype=F32, name, epi=None, epi_rows=(), epi_pars=(), epi_out_dtypes=None,
       tm=None, tn=None, tk=None):
    if not isinstance(pairs, (list, tuple)) or not isinstance(pairs[0], (list, tuple)):
        pairs = [pairs]
    npair = len(pairs)
    a0, b0 = pairs[0]
    M = a0.shape[1] if ta else a0.shape[0]
    K = a0.shape[0] if ta else a0.shape[1]
    N = b0.shape[0] if tb else b0.shape[1]
    for a, b in pairs:
        assert (a.shape == ((K, M) if ta else (M, K))), (a.shape, M, K)
        assert (b.shape == ((N, K) if tb else (K, N))), (b.shape, K, N)
    tm = tm or _pick(M, (1024, 1408, 512, 384, 256, 128))
    tk = tk or (K if K <= 1024 else _pick(K, (1024, 1408, 512, 256, 128)))
    if tn is None:
        n_epi_out = 1 if epi is None else len(epi_out_dtypes)
        for tn in (1024, 1408, 512, 384, 256, 128, N):
            if N % tn:
                continue
            need = sum(2 * tk * (tm * a.dtype.itemsize + tn * b.dtype.itemsize) for a, b in pairs)
            need += tm * tn * 4 * (1 + 2 * n_epi_out + 2 * len(epi_rows))
            if need <= MM_VMEM_BUDGET:
                break
    nk = K // tk
    assert M % tm == 0 and N % tn == 0 and K % tk == 0, (M, N, K, tm, tn, tk)
    n_rows, n_pars = len(epi_rows), len(epi_pars)
    if epi is None:
        out_dtypes = (out_dtype,)
    else:
        out_dtypes = tuple(epi_out_dtypes)
    n_out = len(out_dtypes)
    dn = (((0 if ta else 1,), (1 if tb else 0,)), ((), ()))

    def body(*refs):
        ab = refs[:2 * npair]
        rows = refs[2 * npair:2 * npair + n_rows]
        pars = refs[2 * npair + n_rows:2 * npair + n_rows + n_pars]
        outs = refs[2 * npair + n_rows + n_pars:2 * npair + n_rows + n_pars + n_out]
        acc_ref = refs[-1]
        k = pl.program_id(2)

        @pl.when(k == 0)
        def _():
            acc_ref[...] = jnp.zeros_like(acc_ref)

        acc = acc_ref[...]
        for p in range(npair):
            a = ab[2 * p][...].astype(BF16)
            b = ab[2 * p + 1][...].astype(BF16)
            acc = acc + lax.dot_general(a, b, dn, preferred_element_type=F32)
        acc_ref[...] = acc

        @pl.when(k == nk - 1)
        def _():
            r = acc_ref[...]
            if epi is None:
                outs[0][...] = r.astype(outs[0].dtype)
            else:
                res = epi(r, *[x[...] for x in rows], *[x[...] for x in pars])
                for o, v in zip(outs, res):
                    o[...] = v.astype(o.dtype)

    a_spec = pl.BlockSpec((tk, tm), lambda i, j, k: (k, i)) if ta else pl.BlockSpec((tm, tk), lambda i, j, k: (i, k))
    b_spec = pl.BlockSpec((tn, tk), lambda i, j, k: (j, k)) if tb else pl.BlockSpec((tk, tn), lambda i, j, k: (k, j))
    in_specs = []
    args = []
    for a, b in pairs:
        in_specs += [a_spec, b_spec]
        args += [a, b]
    for r in epi_rows:
        in_specs.append(pl.BlockSpec((tm, tn), lambda i, j, k: (i, j)))
        args.append(r)
    for p_ in epi_pars:
        in_specs.append(pl.BlockSpec((1, tn), lambda i, j, k: (0, j)))
        args.append(p_)
    out_specs = [pl.BlockSpec((tm, tn), lambda i, j, k: (i, j)) for _ in range(n_out)]
    out_shape = [jax.ShapeDtypeStruct((M, N), d) for d in out_dtypes]
    res = pl.pallas_call(
        body, grid=(M // tm, N // tn, nk), in_specs=in_specs, out_specs=out_specs, out_shape=out_shape,
        scratch_shapes=[pltpu.VMEM((tm, tn), F32)], compiler_params=_cp(("parallel", "parallel", "arbitrary")), name=name,
    )(*args)
    return res[0] if epi is None else tuple(res)


def rowmap(fn, rows, pars, out_dtypes, *, name, tile=512, n_reduce=0):
    views = []
    for r in rows:
        if isinstance(r, tuple):
            views.append(r)
        else:
            views.append((r, r.shape[1], 0))
    S = views[0][0].shape[0]
    tile = min(tile, S)
    assert S % tile == 0
    nt = S // tile
    row_structs = [jax.ShapeDtypeStruct((tile, w), a.dtype) for a, w, _ in views]
    par_structs = [jax.ShapeDtypeStruct(p.shape, p.dtype) for p in pars]
    out_structs = jax.eval_shape(fn, *row_structs, *par_structs)
    n_out = len(out_structs)
    n_row_out = n_out - n_reduce
    nr, npar = len(views), len(pars)

    def body(*refs):
        ins = [x[...] for x in refs[:nr + npar]]
        outs = refs[nr + npar:]
        res = fn(*ins)
        for o, v in zip(outs[:n_row_out], res[:n_row_out]):
            o[...] = v.astype(o.dtype)
        if n_reduce:
            i = pl.program_id(0)

            @pl.when(i == 0)
            def _():
                for o, v in zip(outs[n_row_out:], res[n_row_out:]):
                    o[...] = v.astype(o.dtype)

            @pl.when(i > 0)
            def _():
                for o, v in zip(outs[n_row_out:], res[n_row_out:]):
                    o[...] += v.astype(o.dtype)

    in_specs = [pl.BlockSpec((tile, w), functools.partial(lambda i, c: (i, c), c=c)) for _, w, c in views]
    in_specs += [pl.BlockSpec(p.shape, lambda i: (0, 0)) for p in pars]
    out_specs = [pl.BlockSpec((tile, s.shape[1]), lambda i: (i, 0)) for s in out_structs[:n_row_out]]
    out_specs += [pl.BlockSpec(s.shape, lambda i: (0, 0)) for s in out_structs[n_row_out:]]
    out_shape = [jax.ShapeDtypeStruct((S, s.shape[1]), d) for s, d in zip(out_structs[:n_row_out], out_dtypes[:n_row_out])]
    out_shape += [jax.ShapeDtypeStruct(s.shape, F32) for s in out_structs[n_row_out:]]
    res = pl.pallas_call(
        body, grid=(nt,), in_specs=in_specs, out_specs=out_specs, out_shape=out_shape,
        compiler_params=_cp(("arbitrary",) if n_reduce else ("parallel",)), name=name,
    )(*[v[0] for v in views], *pars)
    return tuple(res)


CH = 64


def _softplus(x):
    return jnp.where(x > 20.0, x, jnp.log(1.0 + jnp.exp(jnp.minimum(x, 20.0))))


def _dot(a, b, dn=(((1,), (0,)), ((), ())), hi=False):
    if hi:
        return lax.dot_general(a.astype(F32), b.astype(F32), dn, precision=HI if hi is True else hi, preferred_element_type=F32)
    return lax.dot_general(a.astype(BF16), b.astype(BF16), dn, preferred_element_type=F32)


_NT = (((1,), (1,)), ((), ()))
_TN = (((0,), (0,)), ((), ()))


def _chunk_consts():
    r = lax.broadcasted_iota(jnp.int32, (CH, 2 * CH), 0)
    c0 = lax.broadcasted_iota(jnp.int32, (CH, 2 * CH), 1)
    c = jnp.where(c0 >= CH, c0 - CH, c0)
    r1 = lax.broadcasted_iota(jnp.int32, (CH, CH), 0)
    c1 = lax.broadcasted_iota(jnp.int32, (CH, CH), 1)
    return dict(
        lower2=r >= c, strict2=r > c, U2=(r <= c).astype(F32), eye2=(r == c).astype(F32),
        L=(r1 >= c1).astype(F32), ones=jnp.ones((CH, CH), F32), Z=jnp.zeros((CH, 2 * CH), F32))


@jax.custom_vjp
def _tri_inv2(a2s, eye2, Z):
    def prod(x2, y):
        return _dot(x2, jnp.concatenate([y, Z], axis=0), hi=HI3)

    bs = [-a2 for a2 in a2s]
    ts = [eye2 + b for b in bs]
    for _ in range(5):
        bs = [prod(b, b) for b in bs]
        ts = [t + prod(t, b) for t, b in zip(ts, bs)]
    return tuple(ts)


def _tri_inv2_fwd(a2s, eye2, Z):
    ts = _tri_inv2(a2s, eye2, Z)
    return ts, (ts, eye2, Z)


def _tri_inv2_bwd(res, dts):
    ts, eye2, Z = res
    xs = [_dot(t2, dt2, _TN, hi=HI3)[:CH] for t2, dt2 in zip(ts, dts)]
    das = tuple(-_dot(x2, jnp.concatenate([t2, Z], axis=0), _NT, hi=HI3) for x2, t2 in zip(xs, ts))
    return das, jnp.zeros_like(eye2), jnp.zeros_like(Z)


_tri_inv2.defvjp(_tri_inv2_fwd, _tri_inv2_bwd)


def _gdn_heads(qs, ks, vs, bxs, axs, Ss, alogs, dtbs, cst):
    lower2, strict2, U2, eye2, L, ones, Z = (cst[n] for n in ("lower2", "strict2", "U2", "eye2", "L", "ones", "Z"))
    H = range(len(qs))

    def prod(x2, y):
        return _dot(x2, jnp.concatenate([y, Z], axis=0), hi=HI3)

    qn = [qs[h] * lax.rsqrt(jnp.sum(qs[h] * qs[h], axis=-1, keepdims=True) + NORM_EPS) * (128.0 ** -0.5) for h in H]
    kn = [ks[h] * lax.rsqrt(jnp.sum(ks[h] * ks[h], axis=-1, keepdims=True) + NORM_EPS) for h in H]
    beta = [jax.nn.sigmoid(bxs[h]) for h in H]
    g = [-jnp.exp(alogs[h]) * _softplus(axs[h] + dtbs[h]) for h in H]
    gc = [_dot(L, g[h], hi=HI3) for h in H]
    n2 = [_dot(ones, g[h] * U2, hi=HI3) for h in H]
    decay2 = [jnp.where(lower2, jnp.exp(jnp.where(lower2, gc[h] - n2[h], 0.0)), 0.0) for h in H]
    kb = [kn[h] * beta[h] for h in H]
    kn2 = [jnp.concatenate([kn[h], kn[h]], axis=0) for h in H]
    a2 = tuple(jnp.where(strict2, _dot(kb[h], kn2[h], _NT) * decay2[h], 0.0) for h in H)
    t2 = _tri_inv2(a2, eye2, Z)
    glast = [jnp.sum(g[h], axis=0, keepdims=True) for h in H]
    u = [prod(t2[h], vs[h] * beta[h]) for h in H]
    w = [prod(t2[h], kb[h] * jnp.exp(gc[h])) for h in H]
    attn2 = [jnp.where(lower2, _dot(qn[h], kn2[h], _NT) * decay2[h], 0.0) for h in H]
    k_end = [kn[h] * jnp.exp(glast[h] - gc[h]) for h in H]
    q_start = [qn[h] * jnp.exp(gc[h]) for h in H]
    v_new = [u[h] - _dot(w[h], Ss[h]) for h in H]
    o = [_dot(q_start[h], Ss[h]) + _dot(attn2[h], jnp.concatenate([v_new[h], Z], axis=0)) for h in H]
    s_new = [Ss[h] * jnp.exp(glast[h]) + _dot(k_end[h], v_new[h], _TN) for h in H]
    return tuple(o), tuple(s_new)


def gdn_fwd(qkv, proj, bcol, acol, alog_e, dtb_e, *, name):
    S_ = qkv.shape[0]
    nc = S_ // CH

    def body(q_ref, k_ref, v_ref, b_ref, a_ref, al_ref, dt_ref, o_ref, hist_ref, s_ref):
        i = pl.program_id(0)

        @pl.when(i == 0)
        def _():
            s_ref[...] = jnp.zeros_like(s_ref)

        cst = _chunk_consts()
        hist_ref[0] = s_ref[...]
        heads = [slice(128 * h, 128 * (h + 1)) for h in range(4)]
        rd = lambda ref: tuple(ref[:, ls] for ls in heads)
        os_, s_news = _gdn_heads(rd(q_ref), rd(k_ref), rd(v_ref), rd(b_ref), rd(a_ref), tuple(s_ref[ls, :] for ls in heads),
                                 rd(al_ref), rd(dt_ref), cst)
        for ls, o, s_new in zip(heads, os_, s_news):
            o_ref[:, ls] = o
            s_ref[ls, :] = s_new

    blk = lambda cb: pl.BlockSpec((CH, 512), functools.partial(lambda i, cb: (i, cb), cb=cb))
    par = pl.BlockSpec((1, 512), lambda i: (0, 0))
    return pl.pallas_call(
        body, grid=(nc,), in_specs=[blk(0), blk(1), blk(2), blk(bcol), blk(acol), par, par],
        out_specs=[pl.BlockSpec((CH, 512), lambda i: (i, 0)), pl.BlockSpec((1, 512, 128), lambda i: (i, 0, 0))],
        out_shape=[jax.ShapeDtypeStruct((S_, 512), F32), jax.ShapeDtypeStruct((nc, 512, 128), F32)],
        scratch_shapes=[pltpu.VMEM((512, 128), F32)], compiler_params=_cp(("arbitrary",)), name=name,
    )(qkv, qkv, qkv, proj, proj, alog_e, dtb_e)


def gdn_bwd(qkv, proj, bcol, acol, alog_e, dtb_e, hist, do, *, name):
    S_ = qkv.shape[0]
    nc = S_ // CH

    def body(q_ref, k_ref, v_ref, b_ref, a_ref, al_ref, dt_ref, hist_ref, do_ref, dqkv_ref, db_ref, da_ref, dal_ref, ddt_ref, ds_ref):
        i = pl.program_id(0)

        @pl.when(i == 0)
        def _():
            ds_ref[...] = jnp.zeros_like(ds_ref)
            dal_ref[...] = jnp.zeros_like(dal_ref)
            ddt_ref[...] = jnp.zeros_like(ddt_ref)

        cst = _chunk_consts()
        heads = [slice(128 * h, 128 * (h + 1)) for h in range(4)]
        rd = lambda ref: tuple(ref[:, ls] for ls in heads)
        fn = functools.partial(_gdn_heads, cst=cst)
        _, vjp = jax.vjp(fn, rd(q_ref), rd(k_ref), rd(v_ref), rd(b_ref), rd(a_ref), tuple(hist_ref[0, ls, :] for ls in heads),
                         rd(al_ref), rd(dt_ref))
        grads = vjp((rd(do_ref), tuple(ds_ref[ls, :] for ls in heads)))
        for h in range(4):
            ls = heads[h]
            dq, dk, dv, db, da, ds_in, dal, ddt = (t[h] for t in grads)
            dqkv_ref[:, 128 * h:128 * (h + 1)] = dq
            dqkv_ref[:, 512 + 128 * h:512 + 128 * (h + 1)] = dk
            dqkv_ref[:, 1024 + 128 * h:1024 + 128 * (h + 1)] = dv
            db_ref[:, ls] = db.astype(db_ref.dtype)
            da_ref[:, ls] = da.astype(da_ref.dtype)
            ds_ref[ls, :] = ds_in
            dal_ref[:, ls] += dal
            ddt_ref[:, ls] += ddt

    rblk = lambda cb: pl.BlockSpec((CH, 512), functools.partial(lambda i, cb: (nc - 1 - i, cb), cb=cb))
    par = pl.BlockSpec((1, 512), lambda i: (0, 0))
    return pl.pallas_call(
        body, grid=(nc,),
        in_specs=[rblk(0), rblk(1), rblk(2), rblk(bcol), rblk(acol), par, par,
                  pl.BlockSpec((1, 512, 128), lambda i: (nc - 1 - i, 0, 0)), rblk(0)],
        out_specs=[pl.BlockSpec((CH, 1536), lambda i: (nc - 1 - i, 0)), rblk(0), rblk(0), par, par],
        out_shape=[jax.ShapeDtypeStruct((S_, 1536), F32), jax.ShapeDtypeStruct((S_, 512), BF16), jax.ShapeDtypeStruct((S_, 512), BF16),
                   jax.ShapeDtypeStruct((1, 512), F32), jax.ShapeDtypeStruct((1, 512), F32)],
        scratch_shapes=[pltpu.VMEM((512, 128), F32)], compiler_params=_cp(("arbitrary",)), name=name,
    )(qkv, qkv, qkv, proj, proj, alog_e, dtb_e, hist, do)


def _ssd_pairs(xs, dtxs, bms, cms, hss, nas, dtbs, dsks, cst):
    lower2, U2, L, ones = cst["lower2"], cst["U2"], cst["L"], cst["ones"]
    lane = lax.broadcasted_iota(jnp.int32, (1, 2 * CH), 1)
    mask_l = (lane < CH).astype(F32)
    mask_r = 1.0 - mask_l
    ones_w = jnp.ones((CH, 2 * CH), F32)
    P_ = range(len(xs))
    G_ = range(len(bms))
    per = len(xs) // len(bms)
    cb2 = [_dot(cms[g], jnp.concatenate([bms[g], bms[g]], axis=0), _NT) for g in G_]
    dt = [_softplus(dtxs[p] + dtbs[p]) for p in P_]
    da = [dt[p] * nas[p] for p in P_]
    m = [_dot(L, da[p], hi=HI3) for p in P_]
    n2 = [_dot(ones, da[p] * U2, hi=HI3) for p in P_]
    lm2 = [jnp.where(lower2, jnp.exp(jnp.where(lower2, m[p] - n2[p], 0.0)), 0.0) for p in P_]
    xdt = [xs[p] * dt[p] for p in P_]
    x2 = [jnp.concatenate([xdt[p] * mask_l, xdt[p] * mask_r], axis=0) for p in P_]
    y_diag = [_dot(cb2[p // per] * lm2[p], x2[p]) for p in P_]
    alast = [jnp.sum(da[p], axis=0, keepdims=True) for p in P_]
    y_off = [_dot(cms[p // per], hss[p], _NT) * jnp.exp(m[p]) for p in P_]
    cd = [jnp.exp(_dot(da[p], ones_w, _TN, hi=HI3)) for p in P_]
    hs_new = [hss[p] * cd[p] + _dot(xdt[p] * jnp.exp(alast[p] - m[p]), bms[p // per], _TN) for p in P_]
    ys = [y_diag[p] + y_off[p] + dsks[p] * xs[p] for p in P_]
    return tuple(ys), tuple(hs_new)


def _ssd_specs(nc, rev):
    ci = (lambda i: nc - 1 - i) if rev else (lambda i: i)
    col = lambda w, c: pl.BlockSpec((CH, w), functools.partial(lambda i, c: (ci(i), c), c=c))
    xg = [col(512, g) for g in range(4)]
    dtg = [col(512, 10 + g) for g in range(4)]
    par = pl.BlockSpec((1, 2048), lambda i: (0, 0))
    hist = pl.BlockSpec((1, 2048, 128), lambda i: (ci(i), 0, 0))
    return xg, dtg, col(512, 4), col(512, 5), par, hist, col


def _ssd_read(x_refs, dt_refs, b_ref, c_ref, na_ref, dtb_ref, dsk_ref):
    sl = [slice(128 * p, 128 * (p + 1)) for p in range(4)]
    xs = tuple(x_refs[g][:, s] for g in range(4) for s in sl)
    dts = tuple(dt_refs[g][:, s] for g in range(4) for s in sl)
    bms = tuple(b_ref[:, s] for s in sl)
    cms = tuple(c_ref[:, s] for s in sl)
    lanes = [slice(128 * p, 128 * (p + 1)) for p in range(16)]
    pars = [tuple(r[:, s] for s in lanes) for r in (na_ref, dtb_ref, dsk_ref)]
    return xs, dts, bms, cms, pars, lanes


def ssd_fwd(xbc, proj, na_e, dtb_e, dsk_e, *, name):
    S_ = xbc.shape[0]
    nc = S_ // CH
    xg, dtg, bs, cs, par, hist, _ = _ssd_specs(nc, False)

    def body(*refs):
        x_refs, dt_refs = refs[0:4], refs[4:8]
        b_ref, c_ref, na_ref, dtb_ref, dsk_ref, y_ref, hist_ref, s_ref = refs[8:]
        i = pl.program_id(0)

        @pl.when(i == 0)
        def _():
            s_ref[...] = jnp.zeros_like(s_ref)

        cst = _chunk_consts()
        hist_ref[0] = s_ref[...]
        xs, dts, bms, cms, pars, lanes = _ssd_read(x_refs, dt_refs, b_ref, c_ref, na_ref, dtb_ref, dsk_ref)
        ys, hs_new = _ssd_pairs(xs, dts, bms, cms, tuple(s_ref[s, :] for s in lanes), *pars, cst)
        for p, s in enumerate(lanes):
            y_ref[:, s] = ys[p]
            s_ref[s, :] = hs_new[p]

    return pl.pallas_call(
        body, grid=(nc,), in_specs=xg + dtg + [bs, cs, par, par, par],
        out_specs=[pl.BlockSpec((CH, 2048), lambda i: (i, 0)), hist],
        out_shape=[jax.ShapeDtypeStruct((S_, 2048), F32), jax.ShapeDtypeStruct((nc, 2048, 128), F32)],
        scratch_shapes=[pltpu.VMEM((2048, 128), F32)], compiler_params=_cp(("arbitrary",)), name=name,
    )(xbc, xbc, xbc, xbc, proj, proj, proj, proj, xbc, xbc, na_e, dtb_e, dsk_e)


def ssd_bwd(xbc, proj, na_e, dtb_e, dsk_e, hist, dy, *, name):
    S_ = xbc.shape[0]
    nc = S_ // CH
    xg, dtg, bs, cs, par, hist_spec, col = _ssd_specs(nc, True)
    wide = pl.BlockSpec((CH, 2048), lambda i: (nc - 1 - i, 0))

    def body(*refs):
        x_refs, dt_refs = refs[0:4], refs[4:8]
        (b_ref, c_ref, na_ref, dtb_ref, dsk_ref, hist_ref, dy_ref,
         dx_ref, ddt_ref, db_ref, dc_ref, dna_ref, ddtb_ref, ddsk_ref, ds_ref) = refs[8:]
        i = pl.program_id(0)

        @pl.when(i == 0)
        def _():
            ds_ref[...] = jnp.zeros_like(ds_ref)
            dna_ref[...] = jnp.zeros_like(dna_ref)
            ddtb_ref[...] = jnp.zeros_like(ddtb_ref)
            ddsk_ref[...] = jnp.zeros_like(ddsk_ref)

        cst = _chunk_consts()
        xs, dts, bms, cms, pars, lanes = _ssd_read(x_refs, dt_refs, b_ref, c_ref, na_ref, dtb_ref, dsk_ref)
        fn = functools.partial(_ssd_pairs, cst=cst)
        _, vjp = jax.vjp(fn, xs, dts, bms, cms, tuple(hist_ref[0, s, :] for s in lanes), *pars)
        dxs, ddts, dbs, dcs, dhs, dnas, ddtbs, ddsks = vjp((tuple(dy_ref[:, s] for s in lanes), tuple(ds_ref[s, :] for s in lanes)))
        for g in range(4):
            db_ref[:, 128 * g:128 * (g + 1)] = dbs[g]
            dc_ref[:, 128 * g:128 * (g + 1)] = dcs[g]
        for p, s in enumerate(lanes):
            dx_ref[:, s] = dxs[p]
            ddt_ref[:, s] = ddts[p].astype(ddt_ref.dtype)
            ds_ref[s, :] = dhs[p]
            dna_ref[:, s] += dnas[p]
            ddtb_ref[:, s] += ddtbs[p]
            ddsk_ref[:, s] += ddsks[p]

    half = pl.BlockSpec((CH, 512), lambda i: (nc - 1 - i, 0))
    return pl.pallas_call(
        body, grid=(nc,), in_specs=xg + dtg + [bs, cs, par, par, par, hist_spec, wide],
        out_specs=[wide, wide, half, half, par, par, par],
        out_shape=[jax.ShapeDtypeStruct((S_, 2048), F32), jax.ShapeDtypeStruct((S_, 2048), BF16),
                   jax.ShapeDtypeStruct((S_, 512), F32), jax.ShapeDtypeStruct((S_, 512), F32)] +
                  [jax.ShapeDtypeStruct((1, 2048), F32)] * 3,
        scratch_shapes=[pltpu.VMEM((2048, 128), F32)], compiler_params=_cp(("arbitrary",)), name=name,
    )(xbc, xbc, xbc, xbc, proj, proj, proj, proj, xbc, xbc, na_e, dtb_e, dsk_e, hist, dy)


ATT_T = 1024
ATT_SCALE = 192.0 ** -0.5
NEG = -1e30


def _chunk_mask(shape):
    return lax.broadcasted_iota(jnp.int32, shape, 1) // CH <= lax.broadcasted_iota(jnp.int32, shape, 0) // CH


def _tri_pairs(n, by_row):
    pairs = [(i, j) for i in range(n) for j in range(i + 1)] if by_row else [(i, j) for j in range(n) for i in range(j, n)]
    return jnp.asarray([p[0] for p in pairs], jnp.int32), jnp.asarray([p[1] for p in pairs], jnp.int32)


def att_fwd(q, kv, kp, *, name):
    S_ = q.shape[0]
    T = min(ATT_T, S_)
    n = S_ // T
    ii, jj = _tri_pairs(n, True)
    HP = 2

    def body(ii_ref, jj_ref, q_ref, kn_ref, kp_ref, v_ref, o_ref, lse_ref, m_ref, l_ref, acc_ref):
        t = pl.program_id(1)
        i, j = ii_ref[t], jj_ref[t]

        @pl.when(j == 0)
        def _():
            m_ref[...] = jnp.full_like(m_ref, NEG)
            l_ref[...] = jnp.zeros_like(l_ref)
            acc_ref[...] = jnp.zeros_like(acc_ref)

        H = range(HP)
        hs = [slice(128 * h, 128 * (h + 1)) for h in H]

        def step(diag):
            kp_ = kp_ref[...]
            k2 = [jnp.concatenate([kn_ref[:, hs[h]], kp_], axis=1) for h in H]
            s = [_dot(q_ref[:, 256 * h:256 * (h + 1)], k2[h], _NT) for h in H]
            if diag:
                mask = _chunk_mask(s[0].shape)
                s = [jnp.where(mask, s[h], NEG) for h in H]
            m_prev = [m_ref[:, hs[h]] for h in H]
            m_cur = [jnp.maximum(m_prev[h], jnp.max(s[h], axis=-1, keepdims=True)) for h in H]
            p = [jnp.exp(s[h] - m_cur[h][:, :1]) for h in H]
            alpha = [jnp.exp(m_prev[h] - m_cur[h]) for h in H]
            for h in H:
                l_ref[:, hs[h]] = alpha[h] * l_ref[:, hs[h]] + jnp.sum(p[h], axis=-1, keepdims=True)
            pv = [_dot(p[h], v_ref[:, hs[h]]) for h in H]
            for h in H:
                acc_ref[:, hs[h]] = acc_ref[:, hs[h]] * alpha[h] + pv[h]
                m_ref[:, hs[h]] = m_cur[h]

        @pl.when(j < i)
        def _():
            step(False)

        @pl.when(j == i)
        def _():
            step(True)
            o_ref[...] = acc_ref[...] / l_ref[...]
            lse_ref[...] = m_ref[...] + jnp.log(l_ref[...])

    W = 128 * HP
    grid_spec = pltpu.PrefetchScalarGridSpec(
        num_scalar_prefetch=2, grid=(4 // HP, ii.shape[0]),
        in_specs=[pl.BlockSpec((T, 2 * W), lambda h, t, ii_, jj_: (ii_[t], h)), pl.BlockSpec((T, W), lambda h, t, ii_, jj_: (jj_[t], h)),
                  pl.BlockSpec((T, 128), lambda h, t, ii_, jj_: (jj_[t], 0)),
                  pl.BlockSpec((T, W), lambda h, t, ii_, jj_: (jj_[t], 4 // HP + h))],
        out_specs=[pl.BlockSpec((T, W), lambda h, t, ii_, jj_: (ii_[t], h))] * 2,
        scratch_shapes=[pltpu.VMEM((T, W), F32)] * 3)
    return pl.pallas_call(
        body, grid_spec=grid_spec, out_shape=[jax.ShapeDtypeStruct((S_, 512), F32), jax.ShapeDtypeStruct((S_, 512), F32)],
        compiler_params=_cp(("parallel", "arbitrary")), name=name,
    )(ii, jj, q, kv, kp, kv)


def att_bwd(q, kv, kp, lse, dsum, do, *, name):
    S_ = q.shape[0]
    T = min(ATT_T, S_)
    n = S_ // T
    ii, jj = _tri_pairs(n, False)
    last = ii.shape[0] - 1

    def body(ii_ref, jj_ref, q_ref, kn_ref, kp_ref, v_ref, lse_ref, d_ref, do_ref, dq_hbm, dk_ref, dv_ref, dq_acc, sem):
        h, t = pl.program_id(0), pl.program_id(1)
        i, j = ii_ref[t], jj_ref[t]

        @pl.when(t == 0)
        def _():
            dq_acc[...] = jnp.zeros_like(dq_acc)

        def step(diag):
            k2 = jnp.concatenate([kn_ref[...], kp_ref[...]], axis=1)
            qb = q_ref[...]
            dob = do_ref[...].astype(BF16)
            s = _dot(qb, k2, _NT)
            p = jnp.exp(s - lse_ref[:, :1])
            if diag:
                p = jnp.where(_chunk_mask(s.shape), p, 0.0)
            if diag:
                dv_ref[...] = _dot(p, dob, _TN)
            else:
                dv_ref[...] += _dot(p, dob, _TN)
            ds = (p * (_dot(dob, v_ref[...], _NT) - d_ref[:, :1])).astype(BF16)
            if diag:
                dk_ref[...] = _dot(ds, qb, _TN)
            else:
                dk_ref[...] += _dot(ds, qb, _TN)
            rows = pl.ds(pl.multiple_of(i * T, T), T)
            dq_acc[rows, :] += _dot(ds, k2)

        @pl.when(i > j)
        def _():
            step(False)

        @pl.when(i == j)
        def _():
            step(True)

        @pl.when(t == last)
        def _():
            cp = pltpu.make_async_copy(dq_acc, dq_hbm.at[h], sem)
            cp.start()
            cp.wait()

    qmap = lambda h, t, ii_, jj_: (ii_[t], h)
    grid_spec = pltpu.PrefetchScalarGridSpec(
        num_scalar_prefetch=2, grid=(4, ii.shape[0]),
        in_specs=[pl.BlockSpec((T, 256), qmap), pl.BlockSpec((T, 128), lambda h, t, ii_, jj_: (jj_[t], h)),
                  pl.BlockSpec((T, 128), lambda h, t, ii_, jj_: (jj_[t], 0)), pl.BlockSpec((T, 128), lambda h, t, ii_, jj_: (jj_[t], 4 + h)),
                  pl.BlockSpec((T, 128), qmap), pl.BlockSpec((T, 128), qmap), pl.BlockSpec((T, 128), qmap)],
        out_specs=[pl.BlockSpec(memory_space=pl.ANY), pl.BlockSpec((T, 256), lambda h, t, ii_, jj_: (jj_[t], h)),
                   pl.BlockSpec((T, 128), lambda h, t, ii_, jj_: (jj_[t], h))],
        scratch_shapes=[pltpu.VMEM((S_, 256), F32), pltpu.SemaphoreType.DMA])
    return pl.pallas_call(
        body, grid_spec=grid_spec,
        out_shape=[jax.ShapeDtypeStruct((4, S_, 256), F32), jax.ShapeDtypeStruct((S_, 1024), F32), jax.ShapeDtypeStruct((S_, 512), F32)],
        compiler_params=_cp(("arbitrary", "arbitrary")), name=name,
    )(ii, jj, q, kv, kp, kv, lse, dsum, do)


CONV_T = 512


def _shift_down(x, halo, s):
    sh = pltpu.roll(x, s, axis=0)
    hr = pltpu.roll(halo, s, axis=0)
    r8 = lax.broadcasted_iota(jnp.int32, hr.shape, 0)
    top = jnp.where(r8 < s, hr, sh[:8])
    return jnp.concatenate([top, sh[8:]], axis=0)


def _shift_up(x, halo, s):
    n = x.shape[0]
    sh = pltpu.roll(x, n - s, axis=0)
    hr = pltpu.roll(halo, 8 - s, axis=0)
    r8 = lax.broadcasted_iota(jnp.int32, hr.shape, 0)
    bot = jnp.where(r8 >= 8 - s, hr, sh[n - 8:])
    return jnp.concatenate([sh[:n - 8], bot], axis=0)


def _conv_pre(x, halo, w, b):
    y = x * w[3:4] + b
    for j in range(3):
        y = y + _shift_down(x, halo, 3 - j) * w[j:j + 1]
    return y


def conv_fwd(src, cb0, ncb, w, b, *, name):
    S_ = src.shape[0]
    T = min(CONV_T, S_)
    nt = S_ // T

    def body(x_ref, h_ref, w_ref, b_ref, o_ref):
        i = pl.program_id(1)
        halo = jnp.where(i > 0, h_ref[...], 0.0)
        y = _conv_pre(x_ref[...], halo, w_ref[...], b_ref[...])
        o_ref[...] = y * jax.nn.sigmoid(y)

    return pl.pallas_call(
        body, grid=(ncb, nt),
        in_specs=[pl.BlockSpec((T, 512), lambda c, i: (i, cb0 + c)),
                  pl.BlockSpec((8, 512), lambda c, i: (jnp.maximum(i * (T // 8) - 1, 0), cb0 + c)),
                  pl.BlockSpec((4, 512), lambda c, i: (0, c)), pl.BlockSpec((1, 512), lambda c, i: (0, c))],
        out_specs=pl.BlockSpec((T, 512), lambda c, i: (i, c)),
        out_shape=jax.ShapeDtypeStruct((S_, 512 * ncb), F32), compiler_params=_cp(("parallel", "parallel")), name=name,
    )(src, src, w, b)


def conv_bwd_pre(src, cb0, ncb, w, b, dy, *, name):
    S_ = src.shape[0]
    T = min(CONV_T, S_)
    nt = S_ // T

    def body(x_ref, h_ref, w_ref, b_ref, dy_ref, dp_ref, dw_ref, db_ref):
        i = pl.program_id(1)
        halo = jnp.where(i > 0, h_ref[...], 0.0)
        x = x_ref[...]
        y = _conv_pre(x, halo, w_ref[...], b_ref[...])
        sg = jax.nn.sigmoid(y)
        dpre = dy_ref[...] * (sg * (1.0 + y * (1.0 - sg)))
        dp_ref[...] = dpre
        rows = [jnp.sum(dpre * _shift_down(x, halo, 3 - j), axis=0, keepdims=True) for j in range(3)]
        rows.append(jnp.sum(dpre * x, axis=0, keepdims=True))
        dw = jnp.concatenate(rows, axis=0)
        db = jnp.sum(dpre, axis=0, keepdims=True)

        @pl.when(i == 0)
        def _():
            dw_ref[...] = dw
            db_ref[...] = db

        @pl.when(i > 0)
        def _():
            dw_ref[...] += dw
            db_ref[...] += db

    return pl.pallas_call(
        body, grid=(ncb, nt),
        in_specs=[pl.BlockSpec((T, 512), lambda c, i: (i, cb0 + c)),
                  pl.BlockSpec((8, 512), lambda c, i: (jnp.maximum(i * (T // 8) - 1, 0), cb0 + c)),
                  pl.BlockSpec((4, 512), lambda c, i: (0, c)), pl.BlockSpec((1, 512), lambda c, i: (0, c)),
                  pl.BlockSpec((T, 512), lambda c, i: (i, c))],
        out_specs=[pl.BlockSpec((T, 512), lambda c, i: (i, c)), pl.BlockSpec((4, 512), lambda c, i: (0, c)),
                   pl.BlockSpec((1, 512), lambda c, i: (0, c))],
        out_shape=[jax.ShapeDtypeStruct((S_, 512 * ncb), F32), jax.ShapeDtypeStruct((4, 512 * ncb), F32),
                   jax.ShapeDtypeStruct((1, 512 * ncb), F32)],
        compiler_params=_cp(("parallel", "arbitrary")), name=name,
    )(src, src, w, b, dy)


def conv_bwd_x(dpre, w, *, name, out_dtype=F32):
    S_, C = dpre.shape
    T = min(CONV_T, S_)
    nt = S_ // T
    ncb = C // 512

    def body(d_ref, h_ref, w_ref, o_ref):
        i = pl.program_id(1)
        halo = jnp.where(i < nt - 1, h_ref[...], 0.0)
        d = d_ref[...]
        w_ = w_ref[...]
        y = d * w_[3:4]
        for j in range(3):
            y = y + _shift_up(d, halo, 3 - j) * w_[j:j + 1]
        o_ref[...] = y.astype(o_ref.dtype)

    return pl.pallas_call(
        body, grid=(ncb, nt),
        in_specs=[pl.BlockSpec((T, 512), lambda c, i: (i, c)),
                  pl.BlockSpec((8, 512), lambda c, i: (jnp.minimum((i + 1) * (T // 8), S_ // 8 - 1), c)),
                  pl.BlockSpec((4, 512), lambda c, i: (0, c))],
        out_specs=pl.BlockSpec((T, 512), lambda c, i: (i, c)),
        out_shape=jax.ShapeDtypeStruct((S_, C), out_dtype), compiler_params=_cp(("parallel", "parallel")), name=name,
    )(dpre, dpre, w)


def ffn_mid_fwd(h, w1, w3, *, name):
    S_, D = h.shape
    F = w1.shape[1]
    tm, tn = _pick(S_, (2048, 1024, 512, 256)), 256

    def body(h_ref, w1_ref, w3_ref, a_ref, u_ref, v_ref):
        hb = h_ref[...]
        u = _dot(hb, w1_ref[...])
        v = _dot(hb, w3_ref[...])
        a_ref[...] = (u * jax.nn.sigmoid(u) * v).astype(a_ref.dtype)
        u_ref[...] = u.astype(u_ref.dtype)
        v_ref[...] = v.astype(v_ref.dtype)

    o = pl.BlockSpec((tm, tn), lambda i, j: (i, j))
    return pl.pallas_call(
        body, grid=(S_ // tm, F // tn),
        in_specs=[pl.BlockSpec((tm, D), lambda i, j: (i, 0)), pl.BlockSpec((D, tn), lambda i, j: (0, j)),
                  pl.BlockSpec((D, tn), lambda i, j: (0, j))],
        out_specs=[o, o, o], out_shape=[jax.ShapeDtypeStruct((S_, F), BF16)] * 3,
        compiler_params=_cp(("parallel", "parallel")), name=name,
    )(h, w1, w3)


def ffn_mid_bwd(u, v, dy, w2, *, name):
    S_, F = u.shape
    D = dy.shape[1]
    tm, tn = _pick(S_, (2048, 1024, 512, 256)), 256

    def body(u_ref, v_ref, dy_ref, w2_ref, du_ref, dv_ref):
        u_ = u_ref[...].astype(F32)
        v_ = v_ref[...].astype(F32)
        da = _dot(dy_ref[...], w2_ref[...], _NT)
        sg = jax.nn.sigmoid(u_)
        dv_ref[...] = (da * (u_ * sg)).astype(dv_ref.dtype)
        du_ref[...] = (da * v_ * (sg * (1.0 + u_ * (1.0 - sg)))).astype(du_ref.dtype)

    o = pl.BlockSpec((tm, tn), lambda i, j: (i, j))
    return pl.pallas_call(
        body, grid=(S_ // tm, F // tn),
        in_specs=[o, o, pl.BlockSpec((tm, D), lambda i, j: (i, 0)), pl.BlockSpec((tn, D), lambda i, j: (j, 0))],
        out_specs=[o, o], out_shape=[jax.ShapeDtypeStruct((S_, F), BF16)] * 2,
        compiler_params=_cp(("parallel", "parallel")), name=name,
    )(u, v, dy, w2)


MESH = pl.DeviceIdType.MESH
ANY = pl.BlockSpec(memory_space=pl.ANY)


def allgather8(x_shard, *, name):
    m_per, n = x_shard.shape

    def body(x_ref, out_ref, send_sems, recv_sems, local_sem):
        x, y, c = lax.axis_index("x"), lax.axis_index("y"), lax.axis_index("c")
        me, sibling = (x, y, c), (x, y, 1 - c)
        chips = [(1 - x, y), (x, 1 - y), (1 - x, 1 - y)]

        def rows(px, py, pc):
            return out_ref.at[pl.ds((4 * px + 2 * py + pc) * m_per, m_per), :]

        def copy(k, block, to, src=None):
            return pltpu.make_async_remote_copy(
                src_ref=rows(*block) if src is None else src, dst_ref=rows(*block),
                send_sem=send_sems.at[k], recv_sem=recv_sems.at[k], device_id=to, device_id_type=MESH)

        mine = pltpu.make_async_copy(x_ref, rows(*me), local_sem)
        mine.start()
        first = [copy(0, me, sibling, src=x_ref)]
        first += [copy(1 + j, me, (*chip, c), src=x_ref) for j, chip in enumerate(chips)]
        for cp in first:
            cp.start()
        passed = [copy(4 + j, (*chip, c), sibling) for j, chip in enumerate(chips)]
        for j, chip in enumerate(chips):
            copy(1 + j, (*chip, c), me).wait_recv()
            passed[j].start()
        copy(0, sibling, me).wait_recv()
        for j, chip in enumerate(chips):
            copy(4 + j, (*chip, 1 - c), me).wait_recv()
        for cp in first + passed:
            cp.wait_send()
        mine.wait()

    return pl.pallas_call(
        body, out_shape=jax.ShapeDtypeStruct((8 * m_per, n), x_shard.dtype),
        in_specs=[pl.BlockSpec(memory_space=pltpu.VMEM)], out_specs=pl.BlockSpec(memory_space=pltpu.VMEM),
        scratch_shapes=[pltpu.SemaphoreType.DMA((7,)), pltpu.SemaphoreType.DMA((7,)), pltpu.SemaphoreType.DMA],
        name=name,
    )(x_shard)


def _chip_peers():
    x, y, c = lax.axis_index("x"), lax.axis_index("y"), lax.axis_index("c")
    return x, y, c, [(1 - x, y), (x, 1 - y), (1 - x, 1 - y)]


def allgather_chips(x_shard, *, name):
    r, cdim = x_shard.shape

    def body(x_ref, out_ref, send_sems, recv_sems, local_sem):
        x, y, c, chips = _chip_peers()
        me = 2 * x + y
        mine = pltpu.make_async_copy(x_ref, out_ref.at[me], local_sem)
        mine.start()
        sends = []
        for k, (px, py) in enumerate(chips):
            cp = pltpu.make_async_remote_copy(src_ref=x_ref, dst_ref=out_ref.at[me], send_sem=send_sems.at[k],
                                              recv_sem=recv_sems.at[k], device_id=(px, py, c), device_id_type=MESH)
            cp.start()
            sends.append(cp)
        for k, (px, py) in enumerate(chips):
            pltpu.make_async_remote_copy(src_ref=x_ref, dst_ref=out_ref.at[2 * px + py], send_sem=send_sems.at[k],
                                         recv_sem=recv_sems.at[k], device_id=(px, py, c), device_id_type=MESH).wait_recv()
        for cp in sends:
            cp.wait_send()
        mine.wait()

    return pl.pallas_call(
        body, out_shape=jax.ShapeDtypeStruct((4, r, cdim), x_shard.dtype), in_specs=[ANY], out_specs=ANY,
        scratch_shapes=[pltpu.SemaphoreType.DMA((3,)), pltpu.SemaphoreType.DMA((3,)), pltpu.SemaphoreType.DMA],
        name=name,
    )(x_shard)


def allgather_chips_2level(x_shard, *, name):
    r, cdim = x_shard.shape
    half = r // 2

    def body(x_ref, out_ref, send_sems, recv_sems, local_sem):
        x, y, c, chips = _chip_peers()
        me = 2 * x + y
        mine_rows = pl.ds(c * half, half)
        other_rows = pl.ds((1 - c) * half, half)
        mine = pltpu.make_async_copy(x_ref, out_ref.at[me], local_sem)
        mine.start()

        def copy(k, slot, rows, to, src=None):
            dst = out_ref.at[slot, rows, :]
            return pltpu.make_async_remote_copy(src_ref=dst if src is None else src, dst_ref=dst, send_sem=send_sems.at[k],
                                                recv_sem=recv_sems.at[k], device_id=to, device_id_type=MESH)

        first = [copy(k, me, mine_rows, (px, py, c), src=x_ref.at[mine_rows, :]) for k, (px, py) in enumerate(chips)]
        for cp in first:
            cp.start()
        passed = [copy(3 + k, 2 * px + py, mine_rows, (x, y, 1 - c)) for k, (px, py) in enumerate(chips)]
        for k, (px, py) in enumerate(chips):
            copy(k, 2 * px + py, mine_rows, (px, py, c)).wait_recv()
            passed[k].start()
        for k, (px, py) in enumerate(chips):
            copy(3 + k, 2 * px + py, other_rows, (x, y, 1 - c)).wait_recv()
        for cp in first + passed:
            cp.wait_send()
        mine.wait()

    return pl.pallas_call(
        body, out_shape=jax.ShapeDtypeStruct((4, r, cdim), x_shard.dtype), in_specs=[ANY], out_specs=ANY,
        scratch_shapes=[pltpu.SemaphoreType.DMA((6,)), pltpu.SemaphoreType.DMA((6,)), pltpu.SemaphoreType.DMA],
        name=name,
    )(x_shard)


def exchange_chips(g, *, name):
    _, r, cdim = g.shape

    def body(g_ref, out_ref, send_sems, recv_sems, local_sem):
        x, y, c, chips = _chip_peers()
        me = 2 * x + y
        mine = pltpu.make_async_copy(g_ref.at[me], out_ref.at[me], local_sem)
        mine.start()
        sends = []
        for k, (px, py) in enumerate(chips):
            cp = pltpu.make_async_remote_copy(src_ref=g_ref.at[2 * px + py], dst_ref=out_ref.at[me], send_sem=send_sems.at[k],
                                              recv_sem=recv_sems.at[k], device_id=(px, py, c), device_id_type=MESH)
            cp.start()
            sends.append(cp)
        for k, (px, py) in enumerate(chips):
            pltpu.make_async_remote_copy(src_ref=g_ref.at[me], dst_ref=out_ref.at[2 * px + py], send_sem=send_sems.at[k],
                                         recv_sem=recv_sems.at[k], device_id=(px, py, c), device_id_type=MESH).wait_recv()
        for cp in sends:
            cp.wait_send()
        mine.wait()

    return pl.pallas_call(
        body, out_shape=jax.ShapeDtypeStruct(g.shape, g.dtype), in_specs=[ANY], out_specs=ANY,
        scratch_shapes=[pltpu.SemaphoreType.DMA((3,)), pltpu.SemaphoreType.DMA((3,)), pltpu.SemaphoreType.DMA],
        name=name,
    )(g)


def swap_sibling(p, *, name):
    def body(p_ref, out_ref, send_sem, recv_sem):
        x, y, c = lax.axis_index("x"), lax.axis_index("y"), lax.axis_index("c")
        cp = pltpu.make_async_remote_copy(src_ref=p_ref, dst_ref=out_ref, send_sem=send_sem, recv_sem=recv_sem,
                                          device_id=(x, y, 1 - c), device_id_type=MESH)
        cp.start()
        cp.wait()

    return pl.pallas_call(
        body, out_shape=jax.ShapeDtypeStruct(p.shape, p.dtype), in_specs=[ANY], out_specs=ANY,
        scratch_shapes=[pltpu.SemaphoreType.DMA, pltpu.SemaphoreType.DMA], name=name,
    )(p)


def swap_other_half(g, *, name):
    n, r, cdim = g.shape
    half = r // 2

    def body(g_ref, out_ref, send_sem, recv_sem):
        x, y, c = lax.axis_index("x"), lax.axis_index("y"), lax.axis_index("c")
        cp = pltpu.make_async_remote_copy(src_ref=g_ref.at[:, pl.ds((1 - c) * half, half), :], dst_ref=out_ref, send_sem=send_sem,
                                          recv_sem=recv_sem, device_id=(x, y, 1 - c), device_id_type=MESH)
        cp.start()
        cp.wait()

    return pl.pallas_call(
        body, out_shape=jax.ShapeDtypeStruct((n, half, cdim), g.dtype), in_specs=[ANY], out_specs=ANY,
        scratch_shapes=[pltpu.SemaphoreType.DMA, pltpu.SemaphoreType.DMA], name=name,
    )(g)


def add_pairs(a, b, *, name, out_dtype):
    n, rows, cdim = a.shape
    t = _pick(rows, (256, 128, 64, 32, 16))

    def body(a_ref, b_ref, o_ref):
        o_ref[...] = (a_ref[...].astype(F32) + b_ref[...].astype(F32)).astype(o_ref.dtype)

    spec = pl.BlockSpec((n, t, cdim), lambda i: (0, i, 0))
    return pl.pallas_call(
        body, grid=(rows // t,), in_specs=[spec, spec], out_specs=spec, out_shape=jax.ShapeDtypeStruct(a.shape, out_dtype),
        compiler_params=_cp(("parallel",)), name=name,
    )(a, b)


def sum_slots(r, *, name):
    n, rows, cdim = r.shape
    t = _pick(rows, (256, 128, 64, 32, 16, 8))

    def body(r_ref, o_ref):
        acc = r_ref[0].astype(F32)
        for s in range(1, n):
            acc = acc + r_ref[s].astype(F32)
        o_ref[...] = acc

    return pl.pallas_call(
        body, grid=(rows // t,), in_specs=[pl.BlockSpec((n, t, cdim), lambda i: (0, i, 0))],
        out_specs=pl.BlockSpec((t, cdim), lambda i: (i, 0)), out_shape=jax.ShapeDtypeStruct((rows, cdim), F32),
        compiler_params=_cp(("parallel",)), name=name,
    )(r)


def _rms(x, g):
    return x * lax.rsqrt(jnp.mean(x * x, axis=-1, keepdims=True) + NORM_EPS) * g


def _adaln(x, g, shift, scale):
    return _rms(x, g) * (1.0 + scale) + shift


def _silu(x):
    return x * jax.nn.sigmoid(x)


def _gdn_gate(o, z, g):
    return jnp.concatenate([_rms(o[:, 128 * h:128 * (h + 1)], g) * _silu(z[:, 128 * h:128 * (h + 1)]) for h in range(4)], axis=1)


def _ssd_gate(y, z0, z1, z2, z3, g):
    outs = []
    for k, z in enumerate((z0, z1, z2, z3)):
        t = y[:, 512 * k:512 * (k + 1)] * _silu(z)
        outs.append(t * lax.rsqrt(jnp.mean(t * t, axis=-1, keepdims=True) + NORM_EPS))
    return jnp.concatenate(outs, axis=1) * g


def _rope(x, cos, sin, rot):
    return x * cos + _dot(x, rot, hi=True) * sin


def _rope_q(q, cos, sin, rot):
    parts = []
    for h in range(4):
        parts += [q[:, 256 * h:256 * h + 128], _rope(q[:, 256 * h + 128:256 * (h + 1)], cos, sin, rot)]
    return jnp.concatenate(parts, axis=1) * ATT_SCALE


def _rope_t(d, cos, sin, rot):
    return d * cos + _dot(d * sin, rot, _NT, hi=True)


def _vjp_rows(fn, n_rows, n_pars, out_dtypes, rows, cts, pars, *, name, tile=512, extra=None):
    nct = len(cts)

    def bwd(*a):
        r, c, e, p = a[:n_rows], a[n_rows:n_rows + nct], a[n_rows + nct:len(a) - n_pars], a[len(a) - n_pars:]
        out, vjp = jax.vjp(fn, *[t.astype(F32) for t in r], *p)
        ct = tuple(t.astype(F32) for t in c)
        grads = vjp(ct[0] if not isinstance(out, tuple) else ct)
        drows = list(grads[:n_rows])
        if e:
            drows[0] = drows[0] + e[0]
        return (*drows, *grads[n_rows:])

    return rowmap(bwd, list(rows) + list(cts) + ([extra] if extra is not None else []), list(pars), out_dtypes,
                  name=name, tile=tile, n_reduce=n_pars)


def _gate_grads(dw_raw, w, gate, scale, *, name):
    dw, dg = rowmap(lambda r, w_, gt: ((scale * gt) * r, jnp.sum((scale * w_.astype(F32)) * r, axis=0, keepdims=True)),
                    [dw_raw, w], [gate], (F32, F32), name=name, tile=256, n_reduce=1)
    return dw, dg[0]


ADAM_LR, ADAM_B1, ADAM_B2, ADAM_EPS, ADAM_WD, ADAM_STEP = 0.001, 0.9, 0.999, 1e-08, 0.01, 10


def _adam_math(w, g, m, v):
    m = ADAM_B1 * m + (1.0 - ADAM_B1) * g
    v = ADAM_B2 * v + (1.0 - ADAM_B2) * (g * g)
    m_hat = m / (1.0 - ADAM_B1 ** ADAM_STEP)
    v_hat = v / (1.0 - ADAM_B2 ** ADAM_STEP)
    delta = -ADAM_LR * (m_hat / (jnp.sqrt(v_hat) + ADAM_EPS) + ADAM_WD * w)
    return delta, m, v


def adamw(w, gs, m, v, *, name):
    shape = w.shape
    last = shape[-1]
    to2 = lambda a: a.reshape(-1, last)
    rows = w.size // last
    tile = _pick(rows, (256, 128, 64, 32, 16, 8))
    ng = len(gs)

    def fn(w_, *rest):
        g = rest[0]
        for t in rest[1:ng]:
            g = g + t
        m_, v_ = rest[ng], rest[ng + 1]
        return (g, *_adam_math(w_, g, m_, v_))

    outs = rowmap(fn, [to2(w)] + [to2(g) for g in gs] + [to2(m), to2(v)], [], (F32,) * 4, name=name, tile=tile)
    return tuple(o.reshape(shape) for o in outs)


PACK_W = 1024
BIG = (
    ("ffn_w1", (4, 2, 1024, 704), 3), ("ffn_w3", (4, 2, 1024, 704), 3), ("ffn_w2", (4, 2, 704, 1024), 2),
    ("ev_w_in", (2, 1024, 690), 2), ("mla_w_uq", (2, 96, 4, 192), 1), ("mla_w_ukv", (2, 64, 4, 256), 1),
    ("ev_w_out", (2, 256, 1024), 1), ("ssd_w_in", (2, 1024, 1288), 2), ("ssd_w_out", (2, 512, 1024), 1))


def _seg_rows(shape):
    n = math.prod(shape)
    return -(-n // (16 * PACK_W)) * 16


PACK_ROWS = -(-sum(_seg_rows(sh) for _, sh, _ in BIG) // 512) * 512


def _pack(shards, dtype):
    parts = []
    for (_, shape, _), a in zip(BIG, shards):
        flat = a.reshape(-1).astype(dtype)
        pad = _seg_rows(shape) * PACK_W - flat.shape[0]
        parts.append(jnp.pad(flat, (0, pad)) if pad else flat)
    tail = PACK_ROWS - sum(_seg_rows(sh) for _, sh, _ in BIG)
    if tail:
        parts.append(jnp.zeros((tail * PACK_W,), dtype))
    return jnp.concatenate(parts).reshape(-1, PACK_W)


def _pack_by_owner(fulls):
    cols = []
    for (_, shape, ax), f in zip(BIG, fulls):
        blk = jnp.stack([lax.slice_in_dim(f, s * shape[ax], (s + 1) * shape[ax], axis=ax).reshape(-1).astype(BF16) for s in range(4)])
        pad = _seg_rows(shape) * PACK_W - blk.shape[1]
        cols.append((jnp.pad(blk, ((0, 0), (0, pad))) if pad else blk).reshape(4, -1, PACK_W))
    tail = PACK_ROWS - sum(_seg_rows(sh) for _, sh, _ in BIG)
    if tail:
        cols.append(jnp.zeros((4, tail, PACK_W), BF16))
    return jnp.concatenate(cols, axis=1)


def _unpack(buf):
    out, r0 = [], 0
    for _, shape, _ in BIG:
        n = math.prod(shape)
        out.append(buf[r0:r0 + _seg_rows(shape)].reshape(-1)[:n].reshape(shape))
        r0 += _seg_rows(shape)
    return out


SMALL_SHARDED = (
    ("norm_g", (4, 3, 256), 2), ("gdn_conv_w", (2, 4, 384), 2), ("ssd_conv_w", (2, 4, 768), 2),
    ("ssd_conv_b", (2, 768), 1), ("ssd_norm_g", (2, 512), 1))


def _flat_pack(arrs, width, row_mult):
    flat = jnp.concatenate([a.reshape(-1).astype(F32) for a in arrs])
    n = flat.shape[0]
    tot = -(-n // (width * row_mult)) * width * row_mult
    return jnp.pad(flat, (0, tot - n)).reshape(-1, width)


def _flat_unpack(buf, shapes):
    flat = buf.reshape(-1)
    out, o = [], 0
    for s in shapes:
        n = math.prod(s)
        out.append(flat[o:o + n].reshape(s))
        o += n
    return out


def _rep(v, n):
    return jnp.repeat(v, n, axis=-1)


def kernel(x, c, positions, ada_w, ada_b, norm_g, ffn_w1, ffn_w3, ffn_w2, ev_w_in, gdn_conv_w, gdn_A_log, gdn_dt_bias, gdn_norm_g, mla_q_norm_g, mla_w_uq, mla_kv_norm_g, mla_w_ukv, ev_w_out, ssd_w_in, ssd_conv_w, ssd_conv_b, ssd_A_log, ssd_dt_bias, ssd_D, ssd_norm_g, ssd_w_out, final_g, loss_target, m_ada_w, m_ada_b, m_norm_g, m_ffn_w1, m_ffn_w3, m_ffn_w2, m_ev_w_in, m_gdn_conv_w, m_gdn_A_log, m_gdn_dt_bias, m_gdn_norm_g, m_mla_q_norm_g, m_mla_w_uq, m_mla_kv_norm_g, m_mla_w_ukv, m_ev_w_out, m_ssd_w_in, m_ssd_conv_w, m_ssd_conv_b, m_ssd_A_log, m_ssd_dt_bias, m_ssd_D, m_ssd_norm_g, m_ssd_w_out, m_final_g, v_ada_w, v_ada_b, v_norm_g, v_ffn_w1, v_ffn_w3, v_ffn_w2, v_ev_w_in, v_gdn_conv_w, v_gdn_A_log, v_gdn_dt_bias, v_gdn_norm_g, v_mla_q_norm_g, v_mla_w_uq, v_mla_kv_norm_g, v_mla_w_ukv, v_ev_w_out, v_ssd_w_in, v_ssd_conv_w, v_ssd_conv_b, v_ssd_A_log, v_ssd_dt_bias, v_ssd_D, v_ssd_norm_g, v_ssd_w_out, v_final_g):
    P = dict(ada_w=ada_w, ada_b=ada_b, norm_g=norm_g, ffn_w1=ffn_w1, ffn_w3=ffn_w3, ffn_w2=ffn_w2, ev_w_in=ev_w_in, gdn_conv_w=gdn_conv_w, gdn_A_log=gdn_A_log, gdn_dt_bias=gdn_dt_bias, gdn_norm_g=gdn_norm_g, mla_q_norm_g=mla_q_norm_g, mla_w_uq=mla_w_uq, mla_kv_norm_g=mla_kv_norm_g, mla_w_ukv=mla_w_ukv, ev_w_out=ev_w_out, ssd_w_in=ssd_w_in, ssd_conv_w=ssd_conv_w, ssd_conv_b=ssd_conv_b, ssd_A_log=ssd_A_log, ssd_dt_bias=ssd_dt_bias, ssd_D=ssd_D, ssd_norm_g=ssd_norm_g, ssd_w_out=ssd_w_out, final_g=final_g)
    M1 = dict(ada_w=m_ada_w, ada_b=m_ada_b, norm_g=m_norm_g, ffn_w1=m_ffn_w1, ffn_w3=m_ffn_w3, ffn_w2=m_ffn_w2, ev_w_in=m_ev_w_in, gdn_conv_w=m_gdn_conv_w, gdn_A_log=m_gdn_A_log, gdn_dt_bias=m_gdn_dt_bias, gdn_norm_g=m_gdn_norm_g, mla_q_norm_g=m_mla_q_norm_g, mla_w_uq=m_mla_w_uq, mla_kv_norm_g=m_mla_kv_norm_g, mla_w_ukv=m_mla_w_ukv, ev_w_out=m_ev_w_out, ssd_w_in=m_ssd_w_in, ssd_conv_w=m_ssd_conv_w, ssd_conv_b=m_ssd_conv_b, ssd_A_log=m_ssd_A_log, ssd_dt_bias=m_ssd_dt_bias, ssd_D=m_ssd_D, ssd_norm_g=m_ssd_norm_g, ssd_w_out=m_ssd_w_out, final_g=m_final_g)
    M2 = dict(ada_w=v_ada_w, ada_b=v_ada_b, norm_g=v_norm_g, ffn_w1=v_ffn_w1, ffn_w3=v_ffn_w3, ffn_w2=v_ffn_w2, ev_w_in=v_ev_w_in, gdn_conv_w=v_gdn_conv_w, gdn_A_log=v_gdn_A_log, gdn_dt_bias=v_gdn_dt_bias, gdn_norm_g=v_gdn_norm_g, mla_q_norm_g=v_mla_q_norm_g, mla_w_uq=v_mla_w_uq, mla_kv_norm_g=v_mla_kv_norm_g, mla_w_ukv=v_mla_w_ukv, ev_w_out=v_ev_w_out, ssd_w_in=v_ssd_w_in, ssd_conv_w=v_ssd_conv_w, ssd_conv_b=v_ssd_conv_b, ssd_A_log=v_ssd_A_log, ssd_dt_bias=v_ssd_dt_bias, ssd_D=v_ssd_D, ssd_norm_g=v_ssd_norm_g, ssd_w_out=v_ssd_w_out, final_g=v_final_g)
    names = list(P)
    xi, yi, ci = lax.axis_index("x"), lax.axis_index("y"), lax.axis_index("c")
    chip = 2 * xi + yi
    bidx = 4 * xi + 2 * yi + ci
    xa = x[0]
    S_, D = xa.shape
    tgt = loss_target[0]
    depth = ffn_w1.shape[0]

    wg = allgather_chips_2level(_pack([P[n] for n, _, _ in BIG], BF16), name="gather_weights")
    per_chip = [_unpack(wg[s]) for s in range(4)]
    W = {n: jnp.concatenate([per_chip[s][k] for s in range(4)], axis=ax) for k, (n, _, ax) in enumerate(BIG)}
    sg = allgather_chips(_flat_pack([P[n] for n, _, _ in SMALL_SHARDED], 1024, 16), name="gather_small")
    per_chip_s = [_flat_unpack(sg[s], [sh for _, sh, _ in SMALL_SHARDED]) for s in range(4)]
    Wf = {n: jnp.concatenate([per_chip_s[s][k] for s in range(4)], axis=ax) for k, (n, _, ax) in enumerate(SMALL_SHARDED)}

    c_all = allgather8(jnp.pad(c, ((0, 7), (0, 0))), name="gather_c").reshape(8, 8, D)[:, 0]
    c_act, = rowmap(lambda t: (_silu(t),), [jnp.pad(c_all, ((0, 8), (0, 0)))], [], (F32,), name="c_act", tile=16)
    ncol = ada_w.shape[2]
    ada_b_loc = lax.dynamic_slice(ada_b, (0, chip * ncol), (depth, ncol))
    mod_loc = [mm((c_act, ada_w[l]), name=f"mod_{l}", epi=lambda acc, b: (acc + b,), epi_pars=(ada_b_loc[l][None],),
                  epi_out_dtypes=(F32,), tm=16, tn=256)[0][:8] for l in range(depth)]
    mod_g = allgather8(jnp.stack(mod_loc).reshape(-1, 1024), name="gather_mod").reshape(8, depth, 8, ncol)
    mod_b = lax.dynamic_index_in_dim(mod_g[0::2], bidx, axis=2, keepdims=False)
    mod = jnp.transpose(mod_b, (1, 0, 2)).reshape(depth, 3, 3, D)

    def ev_ext(w):
        z = lambda n: jnp.zeros((w.shape[0], n), w.dtype)
        return jnp.concatenate([w[:, :2048], _rep(w[:, 2048:2052], 128), _rep(w[:, 2052:2056], 128), w[:, 2056:2440], z(128),
                                w[:, 2440:2696], w[:, 2696:2760], z(192)], axis=1)

    def ev_ext_t(dw):
        return jnp.concatenate([dw[:, :2048], dw[:, 2048:2560].reshape(-1, 4, 128).sum(-1), dw[:, 2560:3072].reshape(-1, 4, 128).sum(-1),
                                dw[:, 3072:3456], dw[:, 3584:3840], dw[:, 3840:3904]], axis=1)

    def od_ext(w):
        return jnp.concatenate([w[:, 2048:5120], w[:, :2048], _rep(w[:, 5120:5152], 64)], axis=1)

    def od_ext_t(dw):
        return jnp.concatenate([dw[:, 3072:5120], dw[:, :3072], dw[:, 5120:].reshape(-1, 32, 64).sum(-1)], axis=1)

    def wq_ext(w):
        return jnp.pad(w, ((0, 0), (0, 0), (0, 64))).reshape(384, 1024)

    def wq_ext_t(dw):
        return dw.reshape(384, 4, 256)[:, :, :192]

    def wkv_ext(w):
        return jnp.concatenate([w[:, :, :128].reshape(256, 512), w[:, :, 128:].reshape(256, 512)], axis=1)

    def wkv_ext_t(dw):
        return jnp.concatenate([dw[:, :512].reshape(256, 4, 128), dw[:, 512:].reshape(256, 4, 128)], axis=2)

    half = 32
    inv_freq = 10000.0 ** (-jnp.arange(half, dtype=F32) / half)
    ang = positions[0].astype(F32)[:, None] * inv_freq
    zpad = jnp.zeros((S_, 64), F32)
    cos_t = jnp.concatenate([jnp.cos(ang), jnp.cos(ang), zpad], axis=1)
    sin_t = jnp.concatenate([jnp.sin(ang), jnp.sin(ang), zpad], axis=1)
    ii = jnp.arange(128)
    rot = (jnp.where((ii[:, None] < 32) & (ii[None, :] == ii[:, None] + 32), 1.0, 0.0)
           - jnp.where((ii[:, None] >= 32) & (ii[:, None] < 64) & (ii[None, :] == ii[:, None] - 32), 1.0, 0.0)).astype(F32)

    grads = {}
    dmod = [[[None] * 3 for _ in range(3)] for _ in range(depth)]
    dnorm_g = [[None] * 3 for _ in range(depth)]

    def acc(name, idx, val):
        grads.setdefault(name, {})[idx] = val

    def ffn_sub(xin, l, k, j):
        g, (shift, scale, gate) = Wf["norm_g"][l, k][None], [mod[l, k, t][None] for t in range(3)]
        w1, w3, w2 = W["ffn_w1"][l, j], W["ffn_w3"][l, j], W["ffn_w2"][l, j]
        tag = f"l{l}f{j}"
        h, = rowmap(lambda *a: (_adaln(*a),), [xin], [g, shift, scale], (BF16,), name=f"adaln_{tag}")
        a, u16, v16 = ffn_mid_fwd(h, w1, w3, name=f"ffn_mid_{tag}")
        xn, = mm((a, w2), name=f"ffn_out_{tag}", epi=lambda acc_, xr, gt: (xr + 0.5 * gt * acc_,), epi_rows=(xin,),
                 epi_pars=(gate,), epi_out_dtypes=(F32,))

        def bwd(dxn):
            du, dv = ffn_mid_bwd(u16, v16, dxn, ((0.5 * gate) * w2).astype(BF16), name=f"ffn_midb_{tag}")
            dh = mm([(du, w1), (dv, w3)], tb=True, name=f"ffn_dh_{tag}", out_dtype=BF16)
            acc("ffn_w1", (l, j), mm((h, du), ta=True, name=f"ffn_dw1_{tag}"))
            acc("ffn_w3", (l, j), mm((h, dv), ta=True, name=f"ffn_dw3_{tag}"))
            dw2, dgate = _gate_grads(mm((a, dxn), ta=True, name=f"ffn_dw2_{tag}"), w2, gate, 0.5, name=f"ffn_dgate_{tag}")
            acc("ffn_w2", (l, j), dw2)
            dx, dg, dsh, dsc = _vjp_rows(_adaln, 1, 3, (F32,), [xin], [dh], [g, shift, scale], name=f"adalnb_{tag}", extra=dxn)
            dnorm_g[l][k] = dg[0]
            dmod[l][k] = [dsh[0], dsc[0], dgate]
            return dx

        return xn, bwd

    def mixer_tail(xin, l, tag, dh, dxn, g, shift, scale, dgate):
        dx, dg, dsh, dsc = _vjp_rows(_adaln, 1, 3, (F32,), [xin], [dh], [g, shift, scale], name=f"adalnb_{tag}", extra=dxn)
        dnorm_g[l][1] = dg[0]
        dmod[l][1] = [dsh[0], dsc[0], dgate]
        return dx

    def even_sub(xin, l):
        e = l // 2
        tag = f"l{l}m"
        g, (shift, scale, gate) = Wf["norm_g"][l, 1][None], [mod[l, 1, t][None] for t in range(3)]
        wext, wq, wkv, wout = ev_ext(W["ev_w_in"][e]), wq_ext(W["mla_w_uq"][e]), wkv_ext(W["mla_w_ukv"][e]), W["ev_w_out"][e]
        conv_w, zb = Wf["gdn_conv_w"][e], jnp.zeros((1, 1536), F32)
        alog_e, dtb_e = _rep(gdn_A_log[e], 128)[None], _rep(gdn_dt_bias[e], 128)[None]
        gg, qg, kvg = gdn_norm_g[e][None], mla_q_norm_g[e][None], mla_kv_norm_g[e][None]
        h, = rowmap(lambda *a: (_adaln(*a),), [xin], [g, shift, scale], (BF16,), name=f"adaln_{tag}")
        proj = mm((h, wext), name=f"ev_in_{tag}")
        qkvc = conv_fwd(proj, 0, 3, conv_w, zb, name=f"gdn_conv_{tag}")
        o_g, hist = gdn_fwd(qkvc, proj, 4, 5, alog_e, dtb_e, name=f"gdn_{tag}")
        o_a, = rowmap(lambda o, z, g_: (_gdn_gate(o, z, g_),), [o_g, (proj, 512, 3)], [gg], (BF16,), name=f"gdn_gate_{tag}")
        cqn, = rowmap(lambda t, g_: (_rms(t, g_),), [(proj, 384, 8)], [qg], (BF16,), name=f"q_norm_{tag}")
        ckvn, = rowmap(lambda t, g_: (_rms(t, g_),), [(proj, 256, 14)], [kvg], (BF16,), name=f"kv_norm_{tag}")
        q0 = mm((cqn, wq), name=f"q_up_{tag}")
        kv = mm((ckvn, wkv), name=f"kv_up_{tag}", out_dtype=BF16)
        q, = rowmap(lambda t, cs, sn, r: (_rope_q(t, cs, sn, r),), [q0, cos_t, sin_t], [rot], (BF16,), name=f"rope_q_{tag}")
        kp, = rowmap(lambda t, cs, sn, r: (_rope(t, cs, sn, r),), [(proj, 128, 30), cos_t, sin_t], [rot], (BF16,), name=f"rope_k_{tag}")
        o_b, lse = att_fwd(q, kv, kp, name=f"att_{tag}")
        xn, = mm([(o_a, wout[:512]), (o_b, wout[512:])], name=f"ev_out_{tag}", epi=lambda acc_, xr, gt: (xr + gt * acc_,),
                 epi_rows=(xin,), epi_pars=(gate,), epi_out_dtypes=(F32,))

        def bwd(dxn):
            wout_g = (gate * wout).astype(BF16)
            do_a = mm((dxn, wout_g[:512]), tb=True, name=f"ev_doa_{tag}")
            do_b = mm((dxn, wout_g[512:]), tb=True, name=f"ev_dob_{tag}")
            dw_raw = jnp.concatenate([mm((o_a, dxn), ta=True, name=f"ev_dwoa_{tag}"), mm((o_b, dxn), ta=True, name=f"ev_dwob_{tag}")], axis=0)
            dwo, dgate = _gate_grads(dw_raw, wout, gate, 1.0, name=f"ev_dgate_{tag}")
            acc("ev_w_out", e, dwo)
            dsum, = rowmap(lambda d, o_: (jnp.concatenate([jnp.broadcast_to(jnp.sum(d[:, 128 * hh:128 * (hh + 1)] * o_[:, 128 * hh:128 * (hh + 1)],
                                                                                        axis=-1, keepdims=True), (d.shape[0], 128))
                                                  for hh in range(4)], axis=1),), [do_b, o_b], [], (F32,), name=f"att_dsum_{tag}")
            dq4, dk2, dv = att_bwd(q, kv, kp, lse, dsum, do_b, name=f"att_bwd_{tag}")

            def rope_qb(d0, d1, d2, d3, cs, sn, r):
                parts = []
                for d in (d0, d1, d2, d3):
                    parts += [d[:, :128], _rope_t(d[:, 128:], cs, sn, r)]
                return (jnp.concatenate(parts, axis=1) * ATT_SCALE,)

            dq0, = rowmap(rope_qb, [dq4[0], dq4[1], dq4[2], dq4[3], cos_t, sin_t], [rot], (BF16,), name=f"rope_qb_{tag}")

            def rope_kb(d, cs, sn, r):
                dkp = d[:, 128:256] + d[:, 384:512] + d[:, 640:768] + d[:, 896:1024]
                return jnp.concatenate([d[:, 256 * hh:256 * hh + 128] for hh in range(4)], axis=1), _rope_t(dkp, cs, sn, r)

            dkn, dkr = rowmap(rope_kb, [dk2, cos_t, sin_t], [rot], (BF16, BF16), name=f"rope_kb_{tag}")
            dcqn = mm((dq0, wq), tb=True, name=f"q_upb_{tag}")
            acc("mla_w_uq", e, wq_ext_t(mm((cqn, dq0), ta=True, name=f"q_dw_{tag}")))
            dckvn = mm([(dkn, wkv[:, :512]), (dv, wkv[:, 512:])], tb=True, name=f"kv_upb_{tag}")
            acc("mla_w_ukv", e, wkv_ext_t(jnp.concatenate([mm((ckvn, dkn), ta=True, name=f"kv_dwk_{tag}"), mm((ckvn, dv), ta=True, name=f"kv_dwv_{tag}")], axis=1)))
            dcq, dqg = _vjp_rows(_rms, 1, 1, (BF16,), [(proj, 384, 8)], [dcqn], [qg], name=f"q_normb_{tag}")
            dckv, dkvg = _vjp_rows(_rms, 1, 1, (BF16,), [(proj, 256, 14)], [dckvn], [kvg], name=f"kv_normb_{tag}")
            acc("mla_q_norm_g", e, dqg[0])
            acc("mla_kv_norm_g", e, dkvg[0])
            do_g, dz, dgg = _vjp_rows(_gdn_gate, 2, 1, (F32, BF16), [o_g, (proj, 512, 3)], [do_a], [gg], name=f"gdn_gateb_{tag}")
            acc("gdn_norm_g", e, dgg[0])
            dqkvc, dbe, dae, dal, ddt = gdn_bwd(qkvc, proj, 4, 5, alog_e, dtb_e, hist, do_g, name=f"gdnb_{tag}")
            acc("gdn_A_log", e, dal.reshape(4, 128).sum(-1))
            acc("gdn_dt_bias", e, ddt.reshape(4, 128).sum(-1))
            dpre, dcw, _ = conv_bwd_pre(proj, 0, 3, conv_w, zb, dqkvc, name=f"gdn_convb_{tag}")
            acc("gdn_conv_w", e, dcw)
            dqkv = conv_bwd_x(dpre, conv_w, name=f"gdn_convx_{tag}", out_dtype=BF16)
            zc = lambda n: jnp.zeros((S_, n), BF16)
            dproj = jnp.concatenate([dqkv, dz, dbe, dae, dcq, zc(128), dckv, dkr, zc(128)], axis=1)
            dh = mm((dproj, wext), tb=True, name=f"ev_inb_{tag}", out_dtype=BF16)
            acc("ev_w_in", e, ev_ext_t(mm((h, dproj), ta=True, name=f"ev_dwin_{tag}")))
            return mixer_tail(xin, l, tag, dh, dxn, g, shift, scale, dgate)

        return xn, bwd

    def odd_sub(xin, l):
        o = l // 2
        tag = f"l{l}m"
        g, (shift, scale, gate) = Wf["norm_g"][l, 1][None], [mod[l, 1, t][None] for t in range(3)]
        wext, wout = od_ext(W["ssd_w_in"][o]), W["ssd_w_out"][o]
        conv_w, conv_b, ng = Wf["ssd_conv_w"][o], Wf["ssd_conv_b"][o][None], Wf["ssd_norm_g"][o][None]
        ex = lambda v: _rep(v, 64)[None]
        na_e, dtb_e, dsk_e = ex(-jnp.exp(ssd_A_log[o])), ex(ssd_dt_bias[o]), ex(ssd_D[o])
        h, = rowmap(lambda *a: (_adaln(*a),), [xin], [g, shift, scale], (BF16,), name=f"adaln_{tag}")
        proj = mm((h, wext), name=f"ssd_in_{tag}")
        zv = [(proj, 512, 6 + t) for t in range(4)]
        xbc = conv_fwd(proj, 0, 6, conv_w, conv_b, name=f"ssd_conv_{tag}")
        ys, hist = ssd_fwd(xbc, proj, na_e, dtb_e, dsk_e, name=f"ssd_{tag}")
        yn, = rowmap(lambda *a: (_ssd_gate(*a),), [ys] + zv, [ng], (BF16,), name=f"ssd_gate_{tag}", tile=256)
        xn, = mm((yn, wout), name=f"ssd_out_{tag}", epi=lambda acc_, xr, gt: (xr + gt * acc_,), epi_rows=(xin,),
                 epi_pars=(gate,), epi_out_dtypes=(F32,))

        def bwd(dxn):
            dyn = mm((dxn, (gate * wout).astype(BF16)), tb=True, name=f"ssd_dyn_{tag}", out_dtype=BF16)
            dwo, dgate = _gate_grads(mm((yn, dxn), ta=True, name=f"ssd_dwout_{tag}"), wout, gate, 1.0, name=f"ssd_dgate_{tag}")
            acc("ssd_w_out", o, dwo)
            dys, dz0, dz1, dz2, dz3, dng = _vjp_rows(_ssd_gate, 5, 1, (F32, BF16, BF16, BF16, BF16), [ys] + zv, [dyn], [ng],
                                                     name=f"ssd_gateb_{tag}", tile=256)
            acc("ssd_norm_g", o, dng[0])
            dxs, ddtx, db_, dc_, dna, ddtb, ddsk = ssd_bwd(xbc, proj, na_e, dtb_e, dsk_e, hist, dys, name=f"ssdb_{tag}")
            acc("ssd_A_log", o, dna.reshape(32, 64).sum(-1) * (-jnp.exp(ssd_A_log[o])))
            acc("ssd_dt_bias", o, ddtb.reshape(32, 64).sum(-1))
            acc("ssd_D", o, ddsk.reshape(32, 64).sum(-1))
            dxp, dcws, dcbs = [], [], []
            for part, (cb0, ncb, dpart) in enumerate(((0, 4, dxs), (4, 1, db_), (5, 1, dc_))):
                cols = slice(512 * cb0, 512 * (cb0 + ncb))
                dpre, dcw, dcb = conv_bwd_pre(proj, cb0, ncb, conv_w[:, cols], conv_b[:, cols], dpart, name=f"ssd_convb{part}_{tag}")
                dxp.append(conv_bwd_x(dpre, conv_w[:, cols], name=f"ssd_convx{part}_{tag}", out_dtype=BF16))
                dcws.append(dcw)
                dcbs.append(dcb[0])
            acc("ssd_conv_w", o, jnp.concatenate(dcws, axis=1))
            acc("ssd_conv_b", o, jnp.concatenate(dcbs))
            dproj = jnp.concatenate(dxp + [dz0, dz1, dz2, dz3, ddtx], axis=1)
            dh = mm((dproj, wext), tb=True, name=f"ssd_inb_{tag}", out_dtype=BF16)
            acc("ssd_w_in", o, od_ext_t(mm((h, dproj), ta=True, name=f"ssd_dwin_{tag}")))
            return mixer_tail(xin, l, tag, dh, dxn, g, shift, scale, dgate)

        return xn, bwd

    tape = []
    xc = xa
    for l in range(depth):
        xc, b0 = ffn_sub(xc, l, 0, 0)
        xc, b1 = (even_sub if l % 2 == 0 else odd_sub)(xc, l)
        xc, b2 = ffn_sub(xc, l, 2, 1)
        tape += [b0, b1, b2]

    def head(xr, tg, g_):
        def f(xv, gv):
            err = _rms(xv, gv) - tg
            return 0.5 * jnp.sum(jnp.mean(err * err, axis=-1, keepdims=True), axis=0, keepdims=True)
        lo, vjp = jax.vjp(f, xr, g_)
        dxv, dgv = vjp(jnp.ones_like(lo))
        return dxv, jnp.broadcast_to(lo, (1, 128)), dgv

    dx, loss_p, dfg = rowmap(head, [xc, tgt], [final_g[None]], (F32,), name="loss_head", n_reduce=2)
    loss = lax.psum(loss_p[0, 0], ("x", "y", "c"))

    for b in reversed(tape):
        dx = b(dx)
    grad_x = dx[None]

    full = {n: jnp.stack([grads[n][k] for k in sorted(grads[n])]) for n in ("ev_w_in", "mla_w_uq", "mla_w_ukv", "ev_w_out", "ssd_w_in", "ssd_w_out")}
    for n in ("ffn_w1", "ffn_w3", "ffn_w2"):
        full[n] = jnp.stack([jnp.stack([grads[n][(l, j)] for j in range(2)]) for l in range(depth)])
    gall = _pack_by_owner([full[n] for n, _, _ in BIG])
    half = PACK_ROWS // 2
    from_sib = swap_other_half(gall, name="swap_half")
    pair = add_pairs(lax.dynamic_slice_in_dim(gall, ci * half, half, axis=1), from_sib, name="add_sibling", out_dtype=BF16)
    recv = exchange_chips(pair, name="exchange_grads")
    part = sum_slots(recv, name="sum_chips")
    sib = swap_sibling(part, name="swap_sibling")
    lo = jnp.where(ci == 0, part, sib)
    hi_ = jnp.where(ci == 0, sib, part)
    g_tot = _unpack(jnp.concatenate([lo, hi_], axis=0))

    dmod_flat = jnp.stack([jnp.stack([jnp.stack(dmod[l][k]) for k in range(3)]) for l in range(depth)]).reshape(depth, 9 * D)
    small_names = ["norm_g", "gdn_conv_w", "gdn_A_log", "gdn_dt_bias", "gdn_norm_g", "mla_q_norm_g", "mla_kv_norm_g",
                   "ssd_conv_w", "ssd_conv_b", "ssd_A_log", "ssd_dt_bias", "ssd_D", "ssd_norm_g", "final_g"]
    small_full = {n: jnp.stack([grads[n][k] for k in sorted(grads[n])]) for n in small_names if n in grads}
    small_full["norm_g"] = jnp.stack([jnp.stack(dnorm_g[l]) for l in range(depth)])
    small_full["final_g"] = dfg[0]
    small_list = [dmod_flat] + [small_full[n] for n in small_names]
    small_shapes = [a.shape for a in small_list]
    sp = _flat_pack(small_list, 128, 8)
    sgath = allgather8(sp, name="gather_small_grads").reshape(8, sp.shape[0], 128)
    ssum = sum_slots(sgath, name="sum_small")
    tot = dict(zip(["ada_b"] + small_names, _flat_unpack(ssum, small_shapes)))
    dmod_all = sgath.reshape(8, -1)[:, :depth * 9 * D].reshape(8, depth, 9 * D)
    dmod_loc = lax.dynamic_slice(dmod_all, (0, 0, chip * ncol), (8, depth, ncol))
    g_ada_w = jnp.stack([mm((c_act, jnp.pad(dmod_loc[:, l], ((0, 8), (0, 0)))), ta=True, name=f"ada_dw_{l}", tk=16, tn=256)
                         for l in range(depth)])

    def own(n, a):
        for m_, sh, ax in SMALL_SHARDED:
            if m_ == n:
                return lax.dynamic_slice_in_dim(a, chip * sh[ax], sh[ax], axis=ax)
        return a

    res = {}
    for k, (n, _, _) in enumerate(BIG):
        res[n] = adamw(P[n], [g_tot[k]], M1[n], M2[n], name=f"adamw_{n}")
    res["ada_w"] = adamw(ada_w, [g_ada_w], m_ada_w, v_ada_w, name="adamw_ada_w")
    sm = ["ada_b"] + small_names
    shapes = [P[n].shape for n in sm]
    pk = lambda d: _flat_pack([d[n] for n in sm], 128, 8)
    outs = adamw(pk(P), [pk({n: own(n, tot[n]).reshape(P[n].shape) for n in sm})], pk(M1), pk(M2), name="adamw_small")
    un = [_flat_unpack(o, shapes) for o in outs]
    for i, n in enumerate(sm):
        res[n] = tuple(un[t][i] for t in range(4))
    return (loss, grad_x, *[res[n][0] for n in names], *[res[n][1] for n in names], *[res[n][2] for n in names], *[res[n][3] for n in names])
```

```python
import functools
import math

import jax
import jax.numpy as jnp
from jax import lax
from jax.experimental import pallas as pl
from jax.experimental.pallas import tpu as pltpu

F32 = jnp.float32
BF16 = jnp.bfloat16
HI = lax.Precision.HIGHEST
HI3 = lax.Precision.HIGH
VMEM_LIMIT = 56 * 1024 * 1024
NORM_EPS = 1e-6
MM_VMEM_BUDGET = 40 * 1024 * 1024


def _cp(sem=None):
    if sem is None:
        return pltpu.CompilerParams(vmem_limit_bytes=VMEM_LIMIT)
    return pltpu.CompilerParams(dimension_semantics=sem, vmem_limit_bytes=VMEM_LIMIT)


def _pick(dim, prefs):
    for p in prefs:
        if dim % p == 0:
            return p
    return dim


def mm(pairs, *, ta=False, tb=False, out_dtype=F32, name, epi=None, epi_rows=(), epi_pars=(), epi_out_dtypes=None,
       tm=None, tn=None, tk=None):
    if not isinstance(pairs, (list, tuple)) or not isinstance(pairs[0], (list, tuple)):
        pairs = [pairs]
    npair = len(pairs)
    a0, b0 = pairs[0]
    M = a0.shape[1] if ta else a0.shape[0]
    K = a0.shape[0] if ta else a0.shape[1]
    N = b0.shape[0] if tb else b0.shape[1]
    for a, b in pairs:
        assert (a.shape == ((K, M) if ta else (M, K))), (a.shape, M, K)
        assert (b.shape == ((N, K) if tb else (K, N))), (b.shape, K, N)
    tm = tm or _pick(M, (1024, 1408, 512, 384, 256, 128))
    tk = tk or (K if K <= 1024 else _pick(K, (1024, 1408, 512, 256, 128)))
    if tn is None:
        n_epi_out = 1 if epi is None else len(epi_out_dtypes)
        for tn in (1024, 1408, 512, 384, 256, 128, N):
            if N % tn:
                continue
            need = sum(2 * tk * (tm * a.dtype.itemsize + tn * b.dtype.itemsize) for a, b in pairs)
            need += tm * tn * 4 * (1 + 2 * n_epi_out + 2 * len(epi_rows))
            if need <= MM_VMEM_BUDGET:
                break
    nk = K // tk
    assert M % tm == 0 and N % tn == 0 and K % tk == 0, (M, N, K, tm, tn, tk)
    n_rows, n_pars = len(epi_rows), len(epi_pars)
    if epi is None:
        out_dtypes = (out_dtype,)
    else:
        out_dtypes = tuple(epi_out_dtypes)
    n_out = len(out_dtypes)
    dn = (((0 if ta else 1,), (1 if tb else 0,)), ((), ()))

    def body(*refs):
        ab = refs[:2 * npair]
        rows = refs[2 * npair:2 * npair + n_rows]
        pars = refs[2 * npair + n_rows:2 * npair + n_rows + n_pars]
        outs = refs[2 * npair + n_rows + n_pars:2 * npair + n_rows + n_pars + n_out]
        acc_ref = refs[-1]
        k = pl.program_id(2)

        @pl.when(k == 0)
        def _():
            acc_ref[...] = jnp.zeros_like(acc_ref)

        acc = acc_ref[...]
        for p in range(npair):
            a = ab[2 * p][...].astype(BF16)
            b = ab[2 * p + 1][...].astype(BF16)
            acc = acc + lax.dot_general(a, b, dn, preferred_element_type=F32)
        acc_ref[...] = acc

        @pl.when(k == nk - 1)
        def _():
            r = acc_ref[...]
            if epi is None:
                outs[0][...] = r.astype(outs[0].dtype)
            else:
                res = epi(r, *[x[...] for x in rows], *[x[...] for x in pars])
                for o, v in zip(outs, res):
                    o[...] = v.astype(o.dtype)

    a_spec = pl.BlockSpec((tk, tm), lambda i, j, k: (k, i)) if ta else pl.BlockSpec((tm, tk), lambda i, j, k: (i, k))
    b_spec = pl.BlockSpec((tn, tk), lambda i, j, k: (j, k)) if tb else pl.BlockSpec((tk, tn), lambda i, j, k: (k, j))
    in_specs = []
    args = []
    for a, b in pairs:
        in_specs += [a_spec, b_spec]
        args += [a, b]
    for r in epi_rows:
        in_specs.append(pl.BlockSpec((tm, tn), lambda i, j, k: (i, j)))
        args.append(r)
    for p_ in epi_pars:
        in_specs.append(pl.BlockSpec((1, tn), lambda i, j, k: (0, j)))
        args.append(p_)
    out_specs = [pl.BlockSpec((tm, tn), lambda i, j, k: (i, j)) for _ in range(n_out)]
    out_shape = [jax.ShapeDtypeStruct((M, N), d) for d in out_dtypes]
    res = pl.pallas_call(
        body, grid=(M // tm, N // tn, nk), in_specs=in_specs, out_specs=out_specs, out_shape=out_shape,
        scratch_shapes=[pltpu.VMEM((tm, tn), F32)], compiler_params=_cp(("parallel", "parallel", "arbitrary")), name=name,
    )(*args)
    return res[0] if epi is None else tuple(res)


def rowmap(fn, rows, pars, out_dtypes, *, name, tile=512, n_reduce=0):
    views = []
    for r in rows:
        if isinstance(r, tuple):
            views.append(r)
        else:
            views.append((r, r.shape[1], 0))
    S = views[0][0].shape[0]
    tile = min(tile, S)
    assert S % tile == 0
    nt = S // tile
    row_structs = [jax.ShapeDtypeStruct((tile, w), a.dtype) for a, w, _ in views]
    par_structs = [jax.ShapeDtypeStruct(p.shape, p.dtype) for p in pars]
    out_structs = jax.eval_shape(fn, *row_structs, *par_structs)
    n_out = len(out_structs)
    n_row_out = n_out - n_reduce
    nr, npar = len(views), len(pars)

    def body(*refs):
        ins = [x[...] for x in refs[:nr + npar]]
        outs = refs[nr + npar:]
        res = fn(*ins)
        for o, v in zip(outs[:n_row_out], res[:n_row_out]):
            o[...] = v.astype(o.dtype)
        if n_reduce:
            i = pl.program_id(0)

            @pl.when(i == 0)
            def _():
                for o, v in zip(outs[n_row_out:], res[n_row_out:]):
                    o[...] = v.astype(o.dtype)

            @pl.when(i > 0)
            def _():
                for o, v in zip(outs[n_row_out:], res[n_row_out:]):
                    o[...] += v.astype(o.dtype)

    in_specs = [pl.BlockSpec((tile, w), functools.partial(lambda i, c: (i, c), c=c)) for _, w, c in views]
    in_specs += [pl.BlockSpec(p.shape, lambda i: (0, 0)) for p in pars]
    out_specs = [pl.BlockSpec((tile, s.shape[1]), lambda i: (i, 0)) for s in out_structs[:n_row_out]]
    out_specs += [pl.BlockSpec(s.shape, lambda i: (0, 0)) for s in out_structs[n_row_out:]]
    out_shape = [jax.ShapeDtypeStruct((S, s.shape[1]), d) for s, d in zip(out_structs[:n_row_out], out_dtypes[:n_row_out])]
    out_shape += [jax.ShapeDtypeStruct(s.shape, F32) for s in out_structs[n_row_out:]]
    res = pl.pallas_call(
        body, grid=(nt,), in_specs=in_specs, out_specs=out_specs, out_shape=out_shape,
        compiler_params=_cp(("arbitrary",) if n_reduce else ("parallel",)), name=name,
    )(*[v[0] for v in views], *pars)
    return tuple(res)


CH = 64


def _softplus(x):
    return jnp.where(x > 20.0, x, jnp.log(1.0 + jnp.exp(jnp.minimum(x, 20.0))))


def _dot(a, b, dn=(((1,), (0,)), ((), ())), hi=False):
    if hi:
        return lax.dot_general(a.astype(F32), b.astype(F32), dn, precision=HI if hi is True else hi, preferred_element_type=F32)
    return lax.dot_general(a.astype(BF16), b.astype(BF16), dn, preferred_element_type=F32)


_NT = (((1,), (1,)), ((), ()))
_TN = (((0,), (0,)), ((), ()))


def _chunk_consts():
    r = lax.broadcasted_iota(jnp.int32, (CH, 2 * CH), 0)
    c0 = lax.broadcasted_iota(jnp.int32, (CH, 2 * CH), 1)
    c = jnp.where(c0 >= CH, c0 - CH, c0)
    r1 = lax.broadcasted_iota(jnp.int32, (CH, CH), 0)
    c1 = lax.broadcasted_iota(jnp.int32, (CH, CH), 1)
    return dict(
        lower2=r >= c, strict2=r > c, U2=(r <= c).astype(F32), eye2=(r == c).astype(F32),
        L=(r1 >= c1).astype(F32), ones=jnp.ones((CH, CH), F32), Z=jnp.zeros((CH, 2 * CH), F32))


@jax.custom_vjp
def _tri_inv2(a2s, eye2, Z):
    def prod(x2, y):
        return _dot(x2, jnp.concatenate([y, Z], axis=0), hi=HI3)

    bs = [-a2 for a2 in a2s]
    ts = [eye2 + b for b in bs]
    for _ in range(5):
        bs = [prod(b, b) for b in bs]
        ts = [t + prod(t, b) for t, b in zip(ts, bs)]
    return tuple(ts)


def _tri_inv2_fwd(a2s, eye2, Z):
    ts = _tri_inv2(a2s, eye2, Z)
    return ts, (ts, eye2, Z)


def _tri_inv2_bwd(res, dts):
    ts, eye2, Z = res
    xs = [_dot(t2, dt2, _TN, hi=HI3)[:CH] for t2, dt2 in zip(ts, dts)]
    das = tuple(-_dot(x2, jnp.concatenate([t2, Z], axis=0), _NT, hi=HI3) for x2, t2 in zip(xs, ts))
    return das, jnp.zeros_like(eye2), jnp.zeros_like(Z)


_tri_inv2.defvjp(_tri_inv2_fwd, _tri_inv2_bwd)


def _gdn_heads(qs, ks, vs, bxs, axs, Ss, alogs, dtbs, cst):
    lower2, strict2, U2, eye2, L, ones, Z = (cst[n] for n in ("lower2", "strict2", "U2", "eye2", "L", "ones", "Z"))
    H = range(len(qs))

    def prod(x2, y):
        return _dot(x2, jnp.concatenate([y, Z], axis=0), hi=HI3)

    qn = [qs[h] * lax.rsqrt(jnp.sum(qs[h] * qs[h], axis=-1, keepdims=True) + NORM_EPS) * (128.0 ** -0.5) for h in H]
    kn = [ks[h] * lax.rsqrt(jnp.sum(ks[h] * ks[h], axis=-1, keepdims=True) + NORM_EPS) for h in H]
    beta = [jax.nn.sigmoid(bxs[h]) for h in H]
    g = [-jnp.exp(alogs[h]) * _softplus(axs[h] + dtbs[h]) for h in H]
    gc = [_dot(L, g[h], hi=HI3) for h in H]
    n2 = [_dot(ones, g[h] * U2, hi=HI3) for h in H]
    decay2 = [jnp.where(lower2, jnp.exp(jnp.where(lower2, gc[h] - n2[h], 0.0)), 0.0) for h in H]
    kb = [kn[h] * beta[h] for h in H]
    kn2 = [jnp.concatenate([kn[h], kn[h]], axis=0) for h in H]
    a2 = tuple(jnp.where(strict2, _dot(kb[h], kn2[h], _NT) * decay2[h], 0.0) for h in H)
    t2 = _tri_inv2(a2, eye2, Z)
    glast = [jnp.sum(g[h], axis=0, keepdims=True) for h in H]
    u = [prod(t2[h], vs[h] * beta[h]) for h in H]
    w = [prod(t2[h], kb[h] * jnp.exp(gc[h])) for h in H]
    attn2 = [jnp.where(lower2, _dot(qn[h], kn2[h], _NT) * decay2[h], 0.0) for h in H]
    k_end = [kn[h] * jnp.exp(glast[h] - gc[h]) for h in H]
    q_start = [qn[h] * jnp.exp(gc[h]) for h in H]
    v_new = [u[h] - _dot(w[h], Ss[h]) for h in H]
    o = [_dot(q_start[h], Ss[h]) + _dot(attn2[h], jnp.concatenate([v_new[h], Z], axis=0)) for h in H]
    s_new = [Ss[h] * jnp.exp(glast[h]) + _dot(k_end[h], v_new[h], _TN) for h in H]
    return tuple(o), tuple(s_new)


def gdn_fwd(qkv, proj, bcol, acol, alog_e, dtb_e, *, name):
    S_ = qkv.shape[0]
    nc = S_ // CH

    def body(q_ref, k_ref, v_ref, b_ref, a_ref, al_ref, dt_ref, o_ref, hist_ref, s_ref):
        i = pl.program_id(0)

        @pl.when(i == 0)
        def _():
            s_ref[...] = jnp.zeros_like(s_ref)

        cst = _chunk_consts()
        hist_ref[0] = s_ref[...]
        heads = [slice(128 * h, 128 * (h + 1)) for h in range(4)]
        rd = lambda ref: tuple(ref[:, ls] for ls in heads)
        os_, s_news = _gdn_heads(rd(q_ref), rd(k_ref), rd(v_ref), rd(b_ref), rd(a_ref), tuple(s_ref[ls, :] for ls in heads),
                                 rd(al_ref), rd(dt_ref), cst)
        for ls, o, s_new in zip(heads, os_, s_news):
            o_ref[:, ls] = o
            s_ref[ls, :] = s_new

    blk = lambda cb: pl.BlockSpec((CH, 512), functools.partial(lambda i, cb: (i, cb), cb=cb))
    par = pl.BlockSpec((1, 512), lambda i: (0, 0))
    return pl.pallas_call(
        body, grid=(nc,), in_specs=[blk(0), blk(1), blk(2), blk(bcol), blk(acol), par, par],
        out_specs=[pl.BlockSpec((CH, 512), lambda i: (i, 0)), pl.BlockSpec((1, 512, 128), lambda i: (i, 0, 0))],
        out_shape=[jax.ShapeDtypeStruct((S_, 512), F32), jax.ShapeDtypeStruct((nc, 512, 128), F32)],
        scratch_shapes=[pltpu.VMEM((512, 128), F32)], compiler_params=_cp(("arbitrary",)), name=name,
    )(qkv, qkv, qkv, proj, proj, alog_e, dtb_e)


def gdn_bwd(qkv, proj, bcol, acol, alog_e, dtb_e, hist, do, *, name):
    S_ = qkv.shape[0]
    nc = S_ // CH

    def body(q_ref, k_ref, v_ref, b_ref, a_ref, al_ref, dt_ref, hist_ref, do_ref, dqkv_ref, db_ref, da_ref, dal_ref, ddt_ref, ds_ref):
        i = pl.program_id(0)

        @pl.when(i == 0)
        def _():
            ds_ref[...] = jnp.zeros_like(ds_ref)
            dal_ref[...] = jnp.zeros_like(dal_ref)
            ddt_ref[...] = jnp.zeros_like(ddt_ref)

        cst = _chunk_consts()
        heads = [slice(128 * h, 128 * (h + 1)) for h in range(4)]
        rd = lambda ref: tuple(ref[:, ls] for ls in heads)
        fn = functools.partial(_gdn_heads, cst=cst)
        _, vjp = jax.vjp(fn, rd(q_ref), rd(k_ref), rd(v_ref), rd(b_ref), rd(a_ref), tuple(hist_ref[0, ls, :] for ls in heads),
                         rd(al_ref), rd(dt_ref))
        grads = vjp((rd(do_ref), tuple(ds_ref[ls, :] for ls in heads)))
        for h in range(4):
            ls = heads[h]
            dq, dk, dv, db, da, ds_in, dal, ddt = (t[h] for t in grads)
            dqkv_ref[:, 128 * h:128 * (h + 1)] = dq
            dqkv_ref[:, 512 + 128 * h:512 + 128 * (h + 1)] = dk
            dqkv_ref[:, 1024 + 128 * h:1024 + 128 * (h + 1)] = dv
            db_ref[:, ls] = db.astype(db_ref.dtype)
            da_ref[:, ls] = da.astype(da_ref.dtype)
            ds_ref[ls, :] = ds_in
            dal_ref[:, ls] += dal
            ddt_ref[:, ls] += ddt

    rblk = lambda cb: pl.BlockSpec((CH, 512), functools.partial(lambda i, cb: (nc - 1 - i, cb), cb=cb))
    par = pl.BlockSpec((1, 512), lambda i: (0, 0))
    return pl.pallas_call(
        body, grid=(nc,),
        in_specs=[rblk(0), rblk(1), rblk(2), rblk(bcol), rblk(acol), par, par,
                  pl.BlockSpec((1, 512, 128), lambda i: (nc - 1 - i, 0, 0)), rblk(0)],
        out_specs=[pl.BlockSpec((CH, 1536), lambda i: (nc - 1 - i, 0)), rblk(0), rblk(0), par, par],
        out_shape=[jax.ShapeDtypeStruct((S_, 1536), F32), jax.ShapeDtypeStruct((S_, 512), BF16), jax.ShapeDtypeStruct((S_, 512), BF16),
                   jax.ShapeDtypeStruct((1, 512), F32), jax.ShapeDtypeStruct((1, 512), F32)],
        scratch_shapes=[pltpu.VMEM((512, 128), F32)], compiler_params=_cp(("arbitrary",)), name=name,
    )(qkv, qkv, qkv, proj, proj, alog_e, dtb_e, hist, do)


def _ssd_pairs(xs, dtxs, bms, cms, hss, nas, dtbs, dsks, cst):
    lower2, U2, L, ones = cst["lower2"], cst["U2"], cst["L"], cst["ones"]
    lane = lax.broadcasted_iota(jnp.int32, (1, 2 * CH), 1)
    mask_l = (lane < CH).astype(F32)
    mask_r = 1.0 - mask_l
    ones_w = jnp.ones((CH, 2 * CH), F32)
    P_ = range(len(xs))
    G_ = range(len(bms))
    per = len(xs) // len(bms)
    cb2 = [_dot(cms[g], jnp.concatenate([bms[g], bms[g]], axis=0), _NT) for g in G_]
    dt = [_softplus(dtxs[p] + dtbs[p]) for p in P_]
    da = [dt[p] * nas[p] for p in P_]
    m = [_dot(L, da[p], hi=HI3) for p in P_]
    n2 = [_dot(ones, da[p] * U2, hi=HI3) for p in P_]
    lm2 = [jnp.where(lower2, jnp.exp(jnp.where(lower2, m[p] - n2[p], 0.0)), 0.0) for p in P_]
    xdt = [xs[p] * dt[p] for p in P_]
    x2 = [jnp.concatenate([xdt[p] * mask_l, xdt[p] * mask_r], axis=0) for p in P_]
    y_diag = [_dot(cb2[p // per] * lm2[p], x2[p]) for p in P_]
    alast = [jnp.sum(da[p], axis=0, keepdims=True) for p in P_]
    y_off = [_dot(cms[p // per], hss[p], _NT) * jnp.exp(m[p]) for p in P_]
    cd = [jnp.exp(_dot(da[p], ones_w, _TN, hi=HI3)) for p in P_]
    hs_new = [hss[p] * cd[p] + _dot(xdt[p] * jnp.exp(alast[p] - m[p]), bms[p // per], _TN) for p in P_]
    ys = [y_diag[p] + y_off[p] + dsks[p] * xs[p] for p in P_]
    return tuple(ys), tuple(hs_new)


def _ssd_specs(nc, rev):
    ci = (lambda i: nc - 1 - i) if rev else (lambda i: i)
    col = lambda w, c: pl.BlockSpec((CH, w), functools.partial(lambda i, c: (ci(i), c), c=c))
    xg = [col(512, g) for g in range(4)]
    dtg = [col(512, 10 + g) for g in range(4)]
    par = pl.BlockSpec((1, 2048), lambda i: (0, 0))
    hist = pl.BlockSpec((1, 2048, 128), lambda i: (ci(i), 0, 0))
    return xg, dtg, col(512, 4), col(512, 5), par, hist, col


def _ssd_read(x_refs, dt_refs, b_ref, c_ref, na_ref, dtb_ref, dsk_ref):
    sl = [slice(128 * p, 128 * (p + 1)) for p in range(4)]
    xs = tuple(x_refs[g][:, s] for g in range(4) for s in sl)
    dts = tuple(dt_refs[g][:, s] for g in range(4) for s in sl)
    bms = tuple(b_ref[:, s] for s in sl)
    cms = tuple(c_ref[:, s] for s in sl)
    lanes = [slice(128 * p, 128 * (p + 1)) for p in range(16)]
    pars = [tuple(r[:, s] for s in lanes) for r in (na_ref, dtb_ref, dsk_ref)]
    return xs, dts, bms, cms, pars, lanes


def ssd_fwd(xbc, proj, na_e, dtb_e, dsk_e, *, name):
    S_ = xbc.shape[0]
    nc = S_ // CH
    xg, dtg, bs, cs, par, hist, _ = _ssd_specs(nc, False)

    def body(*refs):
        x_refs, dt_refs = refs[0:4], refs[4:8]
        b_ref, c_ref, na_ref, dtb_ref, dsk_ref, y_ref, hist_ref, s_ref = refs[8:]
        i = pl.program_id(0)

        @pl.when(i == 0)
        def _():
            s_ref[...] = jnp.zeros_like(s_ref)

        cst = _chunk_consts()
        hist_ref[0] = s_ref[...]
        xs, dts, bms, cms, pars, lanes = _ssd_read(x_refs, dt_refs, b_ref, c_ref, na_ref, dtb_ref, dsk_ref)
        ys, hs_new = _ssd_pairs(xs, dts, bms, cms, tuple(s_ref[s, :] for s in lanes), *pars, cst)
        for p, s in enumerate(lanes):
            y_ref[:, s] = ys[p]
            s_ref[s, :] = hs_new[p]

    return pl.pallas_call(
        body, grid=(nc,), in_specs=xg + dtg + [bs, cs, par, par, par],
        out_specs=[pl.BlockSpec((CH, 2048), lambda i: (i, 0)), hist],
        out_shape=[jax.ShapeDtypeStruct((S_, 2048), F32), jax.ShapeDtypeStruct((nc, 2048, 128), F32)],
        scratch_shapes=[pltpu.VMEM((2048, 128), F32)], compiler_params=_cp(("arbitrary",)), name=name,
    )(xbc, xbc, xbc, xbc, proj, proj, proj, proj, xbc, xbc, na_e, dtb_e, dsk_e)


def ssd_bwd(xbc, proj, na_e, dtb_e, dsk_e, hist, dy, *, name):
    S_ = xbc.shape[0]
    nc = S_ // CH
    xg, dtg, bs, cs, par, hist_spec, col = _ssd_specs(nc, True)
    wide = pl.BlockSpec((CH, 2048), lambda i: (nc - 1 - i, 0))

    def body(*refs):
        x_refs, dt_refs = refs[0:4], refs[4:8]
        (b_ref, c_ref, na_ref, dtb_ref, dsk_ref, hist_ref, dy_ref,
         dx_ref, ddt_ref, db_ref, dc_ref, dna_ref, ddtb_ref, ddsk_ref, ds_ref) = refs[8:]
        i = pl.program_id(0)

        @pl.when(i == 0)
        def _():
            ds_ref[...] = jnp.zeros_like(ds_ref)
            dna_ref[...] = jnp.zeros_like(dna_ref)
            ddtb_ref[...] = jnp.zeros_like(ddtb_ref)
            ddsk_ref[...] = jnp.zeros_like(ddsk_ref)

        cst = _chunk_consts()
        xs, dts, bms, cms, pars, lanes = _ssd_read(x_refs, dt_refs, b_ref, c_ref, na_ref, dtb_ref, dsk_ref)
        fn = functools.partial(_ssd_pairs, cst=cst)
        _, vjp = jax.vjp(fn, xs, dts, bms, cms, tuple(hist_ref[0, s, :] for s in lanes), *pars)
        dxs, ddts, dbs, dcs, dhs, dnas, ddtbs, ddsks = vjp((tuple(dy_ref[:, s] for s in lanes), tuple(ds_ref[s, :] for s in lanes)))
        for g in range(4):
            db_ref[:, 128 * g:128 * (g + 1)] = dbs[g]
            dc_ref[:, 128 * g:128 * (g + 1)] = dcs[g]
        for p, s in enumerate(lanes):
            dx_ref[:, s] = dxs[p]
            ddt_ref[:, s] = ddts[p].astype(ddt_ref.dtype)
            ds_ref[s, :] = dhs[p]
            dna_ref[:, s] += dnas[p]
            ddtb_ref[:, s] += ddtbs[p]
            ddsk_ref[:, s] += ddsks[p]

    half = pl.BlockSpec((CH, 512), lambda i: (nc - 1 - i, 0))
    return pl.pallas_call(
        body, grid=(nc,), in_specs=xg + dtg + [bs, cs, par, par, par, hist_spec, wide],
        out_specs=[wide, wide, half, half, par, par, par],
        out_shape=[jax.ShapeDtypeStruct((S_, 2048), F32), jax.ShapeDtypeStruct((S_, 2048), BF16),
                   jax.ShapeDtypeStruct((S_, 512), F32), jax.ShapeDtypeStruct((S_, 512), F32)] +
                  [jax.ShapeDtypeStruct((1, 2048), F32)] * 3,
        scratch_shapes=[pltpu.VMEM((2048, 128), F32)], compiler_params=_cp(("arbitrary",)), name=name,
    )(xbc, xbc, xbc, xbc, proj, proj, proj, proj, xbc, xbc, na_e, dtb_e, dsk_e, hist, dy)


ATT_T = 1024
ATT_SCALE = 192.0 ** -0.5
NEG = -1e30


def _chunk_mask(shape):
    return lax.broadcasted_iota(jnp.int32, shape, 1) // CH <= lax.broadcasted_iota(jnp.int32, shape, 0) // CH


def _tri_pairs(n, by_row):
    pairs = [(i, j) for i in range(n) for j in range(i + 1)] if by_row else [(i, j) for j in range(n) for i in range(j, n)]
    return jnp.asarray([p[0] for p in pairs], jnp.int32), jnp.asarray([p[1] for p in pairs], jnp.int32)


def att_fwd(q, kv, kp, *, name):
    S_ = q.shape[0]
    T = min(ATT_T, S_)
    n = S_ // T
    ii, jj = _tri_pairs(n, True)
    HP = 2

    def body(ii_ref, jj_ref, q_ref, kn_ref, kp_ref, v_ref, o_ref, lse_ref, m_ref, l_ref, acc_ref):
        t = pl.program_id(1)
        i, j = ii_ref[t], jj_ref[t]

        @pl.when(j == 0)
        def _():
            m_ref[...] = jnp.full_like(m_ref, NEG)
            l_ref[...] = jnp.zeros_like(l_ref)
            acc_ref[...] = jnp.zeros_like(acc_ref)

        H = range(HP)
        hs = [slice(128 * h, 128 * (h + 1)) for h in H]

        def step(diag):
            kp_ = kp_ref[...]
            k2 = [jnp.concatenate([kn_ref[:, hs[h]], kp_], axis=1) for h in H]
            s = [_dot(q_ref[:, 256 * h:256 * (h + 1)], k2[h], _NT) for h in H]
            if diag:
                mask = _chunk_mask(s[0].shape)
                s = [jnp.where(mask, s[h], NEG) for h in H]
            m_prev = [m_ref[:, hs[h]] for h in H]
            m_cur = [jnp.maximum(m_prev[h], jnp.max(s[h], axis=-1, keepdims=True)) for h in H]
            p = [jnp.exp(s[h] - m_cur[h][:, :1]) for h in H]
            alpha = [jnp.exp(m_prev[h] - m_cur[h]) for h in H]
            for h in H:
                l_ref[:, hs[h]] = alpha[h] * l_ref[:, hs[h]] + jnp.sum(p[h], axis=-1, keepdims=True)
            pv = [_dot(p[h], v_ref[:, hs[h]]) for h in H]
            for h in H:
                acc_ref[:, hs[h]] = acc_ref[:, hs[h]] * alpha[h] + pv[h]
                m_ref[:, hs[h]] = m_cur[h]

        @pl.when(j < i)
        def _():
            step(False)

        @pl.when(j == i)
        def _():
            step(True)
            o_ref[...] = acc_ref[...] / l_ref[...]
            lse_ref[...] = m_ref[...] + jnp.log(l_ref[...])

    W = 128 * HP
    grid_spec = pltpu.PrefetchScalarGridSpec(
        num_scalar_prefetch=2, grid=(4 // HP, ii.shape[0]),
        in_specs=[pl.BlockSpec((T, 2 * W), lambda h, t, ii_, jj_: (ii_[t], h)), pl.BlockSpec((T, W), lambda h, t, ii_, jj_: (jj_[t], h)),
                  pl.BlockSpec((T, 128), lambda h, t, ii_, jj_: (jj_[t], 0)),
                  pl.BlockSpec((T, W), lambda h, t, ii_, jj_: (jj_[t], 4 // HP + h))],
        out_specs=[pl.BlockSpec((T, W), lambda h, t, ii_, jj_: (ii_[t], h))] * 2,
        scratch_shapes=[pltpu.VMEM((T, W), F32)] * 3)
    return pl.pallas_call(
        body, grid_spec=grid_spec, out_shape=[jax.ShapeDtypeStruct((S_, 512), F32), jax.ShapeDtypeStruct((S_, 512), F32)],
        compiler_params=_cp(("parallel", "arbitrary")), name=name,
    )(ii, jj, q, kv, kp, kv)


def att_bwd(q, kv, kp, lse, dsum, do, *, name):
    S_ = q.shape[0]
    T = min(ATT_T, S_)
    n = S_ // T
    ii, jj = _tri_pairs(n, False)
    last = ii.shape[0] - 1

    def body(ii_ref, jj_ref, q_ref, kn_ref, kp_ref, v_ref, lse_ref, d_ref, do_ref, dq_hbm, dk_ref, dv_ref, dq_acc, sem):
        h, t = pl.program_id(0), pl.program_id(1)
        i, j = ii_ref[t], jj_ref[t]

        @pl.when(t == 0)
        def _():
            dq_acc[...] = jnp.zeros_like(dq_acc)

        def step(diag):
            k2 = jnp.concatenate([kn_ref[...], kp_ref[...]], axis=1)
            qb = q_ref[...]
            dob = do_ref[...].astype(BF16)
            s = _dot(qb, k2, _NT)
            p = jnp.exp(s - lse_ref[:, :1])
            if diag:
                p = jnp.where(_chunk_mask(s.shape), p, 0.0)
            if diag:
                dv_ref[...] = _dot(p, dob, _TN)
            else:
                dv_ref[...] += _dot(p, dob, _TN)
            ds = (p * (_dot(dob, v_ref[...], _NT) - d_ref[:, :1])).astype(BF16)
            if diag:
                dk_ref[...] = _dot(ds, qb, _TN)
            else:
                dk_ref[...] += _dot(ds, qb, _TN)
            rows = pl.ds(pl.multiple_of(i * T, T), T)
            dq_acc[rows, :] += _dot(ds, k2)

        @pl.when(i > j)
        def _():
            step(False)

        @pl.when(i == j)
        def _():
            step(True)

        @pl.when(t == last)
        def _():
            cp = pltpu.make_async_copy(dq_acc, dq_hbm.at[h], sem)
            cp.start()
            cp.wait()

    qmap = lambda h, t, ii_, jj_: (ii_[t], h)
    grid_spec = pltpu.PrefetchScalarGridSpec(
        num_scalar_prefetch=2, grid=(4, ii.shape[0]),
        in_specs=[pl.BlockSpec((T, 256), qmap), pl.BlockSpec((T, 128), lambda h, t, ii_, jj_: (jj_[t], h)),
                  pl.BlockSpec((T, 128), lambda h, t, ii_, jj_: (jj_[t], 0)), pl.BlockSpec((T, 128), lambda h, t, ii_, jj_: (jj_[t], 4 + h)),
                  pl.BlockSpec((T, 128), qmap), pl.BlockSpec((T, 128), qmap), pl.BlockSpec((T, 128), qmap)],
        out_specs=[pl.BlockSpec(memory_space=pl.ANY), pl.BlockSpec((T, 256), lambda h, t, ii_, jj_: (jj_[t], h)),
                   pl.BlockSpec((T, 128), lambda h, t, ii_, jj_: (jj_[t], h))],
        scratch_shapes=[pltpu.VMEM((S_, 256), F32), pltpu.SemaphoreType.DMA])
    return pl.pallas_call(
        body, grid_spec=grid_spec,
        out_shape=[jax.ShapeDtypeStruct((4, S_, 256), F32), jax.ShapeDtypeStruct((S_, 1024), F32), jax.ShapeDtypeStruct((S_, 512), F32)],
        compiler_params=_cp(("arbitrary", "arbitrary")), name=name,
    )(ii, jj, q, kv, kp, kv, lse, dsum, do)


CONV_T = 512


def _shift_down(x, halo, s):
    sh = pltpu.roll(x, s, axis=0)
    hr = pltpu.roll(halo, s, axis=0)
    r8 = lax.broadcasted_iota(jnp.int32, hr.shape, 0)
    top = jnp.where(r8 < s, hr, sh[:8])
    return jnp.concatenate([top, sh[8:]], axis=0)


def _shift_up(x, halo, s):
    n = x.shape[0]
    sh = pltpu.roll(x, n - s, axis=0)
    hr = pltpu.roll(halo, 8 - s, axis=0)
    r8 = lax.broadcasted_iota(jnp.int32, hr.shape, 0)
    bot = jnp.where(r8 >= 8 - s, hr, sh[n - 8:])
    return jnp.concatenate([sh[:n - 8], bot], axis=0)


def _conv_pre(x, halo, w, b):
    y = x * w[3:4] + b
    for j in range(3):
        y = y + _shift_down(x, halo, 3 - j) * w[j:j + 1]
    return y


def conv_fwd(src, cb0, ncb, w, b, *, name):
    S_ = src.shape[0]
    T = min(CONV_T, S_)
    nt = S_ // T

    def body(x_ref, h_ref, w_ref, b_ref, o_ref):
        i = pl.program_id(1)
        halo = jnp.where(i > 0, h_ref[...], 0.0)
        y = _conv_pre(x_ref[...], halo, w_ref[...], b_ref[...])
        o_ref[...] = y * jax.nn.sigmoid(y)

    return pl.pallas_call(
        body, grid=(ncb, nt),
        in_specs=[pl.BlockSpec((T, 512), lambda c, i: (i, cb0 + c)),
                  pl.BlockSpec((8, 512), lambda c, i: (jnp.maximum(i * (T // 8) - 1, 0), cb0 + c)),
                  pl.BlockSpec((4, 512), lambda c, i: (0, c)), pl.BlockSpec((1, 512), lambda c, i: (0, c))],
        out_specs=pl.BlockSpec((T, 512), lambda c, i: (i, c)),
        out_shape=jax.ShapeDtypeStruct((S_, 512 * ncb), F32), compiler_params=_cp(("parallel", "parallel")), name=name,
    )(src, src, w, b)


def conv_bwd_pre(src, cb0, ncb, w, b, dy, *, name):
    S_ = src.shape[0]
    T = min(CONV_T, S_)
    nt = S_ // T

    def body(x_ref, h_ref, w_ref, b_ref, dy_ref, dp_ref, dw_ref, db_ref):
        i = pl.program_id(1)
        halo = jnp.where(i > 0, h_ref[...], 0.0)
        x = x_ref[...]
        y = _conv_pre(x, halo, w_ref[...], b_ref[...])
        sg = jax.nn.sigmoid(y)
        dpre = dy_ref[...] * (sg * (1.0 + y * (1.0 - sg)))
        dp_ref[...] = dpre
        rows = [jnp.sum(dpre * _shift_down(x, halo, 3 - j), axis=0, keepdims=True) for j in range(3)]
        rows.append(jnp.sum(dpre * x, axis=0, keepdims=True))
        dw = jnp.concatenate(rows, axis=0)
        db = jnp.sum(dpre, axis=0, keepdims=True)

        @pl.when(i == 0)
        def _():
            dw_ref[...] = dw
            db_ref[...] = db

        @pl.when(i > 0)
        def _():
            dw_ref[...] += dw
            db_ref[...] += db

    return pl.pallas_call(
        body, grid=(ncb, nt),
        in_specs=[pl.BlockSpec((T, 512), lambda c, i: (i, cb0 + c)),
                  pl.BlockSpec((8, 512), lambda c, i: (jnp.maximum(i * (T // 8) - 1, 0), cb0 + c)),
                  pl.BlockSpec((4, 512), lambda c, i: (0, c)), pl.BlockSpec((1, 512), lambda c, i: (0, c)),
                  pl.BlockSpec((T, 512), lambda c, i: (i, c))],
        out_specs=[pl.BlockSpec((T, 512), lambda c, i: (i, c)), pl.BlockSpec((4, 512), lambda c, i: (0, c)),
                   pl.BlockSpec((1, 512), lambda c, i: (0, c))],
        out_shape=[jax.ShapeDtypeStruct((S_, 512 * ncb), F32), jax.ShapeDtypeStruct((4, 512 * ncb), F32),
                   jax.ShapeDtypeStruct((1, 512 * ncb), F32)],
        compiler_params=_cp(("parallel", "arbitrary")), name=name,
    )(src, src, w, b, dy)


def conv_bwd_x(dpre, w, *, name, out_dtype=F32):
    S_, C = dpre.shape
    T = min(CONV_T, S_)
    nt = S_ // T
    ncb = C // 512

    def body(d_ref, h_ref, w_ref, o_ref):
        i = pl.program_id(1)
        halo = jnp.where(i < nt - 1, h_ref[...], 0.0)
        d = d_ref[...]
        w_ = w_ref[...]
        y = d * w_[3:4]
        for j in range(3):
            y = y + _shift_up(d, halo, 3 - j) * w_[j:j + 1]
        o_ref[...] = y.astype(o_ref.dtype)

    return pl.pallas_call(
        body, grid=(ncb, nt),
        in_specs=[pl.BlockSpec((T, 512), lambda c, i: (i, c)),
                  pl.BlockSpec((8, 512), lambda c, i: (jnp.minimum((i + 1) * (T // 8), S_ // 8 - 1), c)),
                  pl.BlockSpec((4, 512), lambda c, i: (0, c))],
        out_specs=pl.BlockSpec((T, 512), lambda c, i: (i, c)),
        out_shape=jax.ShapeDtypeStruct((S_, C), out_dtype), compiler_params=_cp(("parallel", "parallel")), name=name,
    )(dpre, dpre, w)


def ffn_mid_fwd(h, w1, w3, *, name):
    S_, D = h.shape
    F = w1.shape[1]
    tm, tn = _pick(S_, (2048, 1024, 512, 256)), 256

    def body(h_ref, w1_ref, w3_ref, a_ref, u_ref, v_ref):
        hb = h_ref[...]
        u = _dot(hb, w1_ref[...])
        v = _dot(hb, w3_ref[...])
        a_ref[...] = (u * jax.nn.sigmoid(u) * v).astype(a_ref.dtype)
        u_ref[...] = u.astype(u_ref.dtype)
        v_ref[...] = v.astype(v_ref.dtype)

    o = pl.BlockSpec((tm, tn), lambda i, j: (i, j))
    return pl.pallas_call(
        body, grid=(S_ // tm, F // tn),
        in_specs=[pl.BlockSpec((tm, D), lambda i, j: (i, 0)), pl.BlockSpec((D, tn), lambda i, j: (0, j)),
                  pl.BlockSpec((D, tn), lambda i, j: (0, j))],
        out_specs=[o, o, o], out_shape=[jax.ShapeDtypeStruct((S_, F), BF16)] * 3,
        compiler_params=_cp(("parallel", "parallel")), name=name,
    )(h, w1, w3)


def ffn_mid_bwd(u, v, dy, w2, *, name):
    S_, F = u.shape
    D = dy.shape[1]
    tm, tn = _pick(S_, (2048, 1024, 512, 256)), 256

    def body(u_ref, v_ref, dy_ref, w2_ref, du_ref, dv_ref):
        u_ = u_ref[...].astype(F32)
        v_ = v_ref[...].astype(F32)
        da = _dot(dy_ref[...], w2_ref[...], _NT)
        sg = jax.nn.sigmoid(u_)
        dv_ref[...] = (da * (u_ * sg)).astype(dv_ref.dtype)
        du_ref[...] = (da * v_ * (sg * (1.0 + u_ * (1.0 - sg)))).astype(du_ref.dtype)

    o = pl.BlockSpec((tm, tn), lambda i, j: (i, j))
    return pl.pallas_call(
        body, grid=(S_ // tm, F // tn),
        in_specs=[o, o, pl.BlockSpec((tm, D), lambda i, j: (i, 0)), pl.BlockSpec((tn, D), lambda i, j: (j, 0))],
        out_specs=[o, o], out_shape=[jax.ShapeDtypeStruct((S_, F), BF16)] * 2,
        compiler_params=_cp(("parallel", "parallel")), name=name,
    )(u, v, dy, w2)


MESH = pl.DeviceIdType.MESH
ANY = pl.BlockSpec(memory_space=pl.ANY)


def allgather8(x_shard, *, name):
    m_per, n = x_shard.shape

    def body(x_ref, out_ref, send_sems, recv_sems, local_sem):
        x, y, c = lax.axis_index("x"), lax.axis_index("y"), lax.axis_index("c")
        me, sibling = (x, y, c), (x, y, 1 - c)
        chips = [(1 - x, y), (x, 1 - y), (1 - x, 1 - y)]

        def rows(px, py, pc):
            return out_ref.at[pl.ds((4 * px + 2 * py + pc) * m_per, m_per), :]

        def copy(k, block, to, src=None):
            return pltpu.make_async_remote_copy(
                src_ref=rows(*block) if src is None else src, dst_ref=rows(*block),
                send_sem=send_sems.at[k], recv_sem=recv_sems.at[k], device_id=to, device_id_type=MESH)

        mine = pltpu.make_async_copy(x_ref, rows(*me), local_sem)
        mine.start()
        first = [copy(0, me, sibling, src=x_ref)]
        first += [copy(1 + j, me, (*chip, c), src=x_ref) for j, chip in enumerate(chips)]
        for cp in first:
            cp.start()
        passed = [copy(4 + j, (*chip, c), sibling) for j, chip in enumerate(chips)]
        for j, chip in enumerate(chips):
            copy(1 + j, (*chip, c), me).wait_recv()
            passed[j].start()
        copy(0, sibling, me).wait_recv()
        for j, chip in enumerate(chips):
            copy(4 + j, (*chip, 1 - c), me).wait_recv()
        for cp in first + passed:
            cp.wait_send()
        mine.wait()

    return pl.pallas_call(
        body, out_shape=jax.ShapeDtypeStruct((8 * m_per, n), x_shard.dtype),
        in_specs=[pl.BlockSpec(memory_space=pltpu.VMEM)], out_specs=pl.BlockSpec(memory_space=pltpu.VMEM),
        scratch_shapes=[pltpu.SemaphoreType.DMA((7,)), pltpu.SemaphoreType.DMA((7,)), pltpu.SemaphoreType.DMA],
        name=name,
    )(x_shard)


def _chip_peers():
    x, y, c = lax.axis_index("x"), lax.axis_index("y"), lax.axis_index("c")
    return x, y, c, [(1 - x, y), (x, 1 - y), (1 - x, 1 - y)]


def allgather_chips(x_shard, *, name):
    r, cdim = x_shard.shape

    def body(x_ref, out_ref, send_sems, recv_sems, local_sem):
        x, y, c, chips = _chip_peers()
        me = 2 * x + y
        mine = pltpu.make_async_copy(x_ref, out_ref.at[me], local_sem)
        mine.start()
        sends = []
        for k, (px, py) in enumerate(chips):
            cp = pltpu.make_async_remote_copy(src_ref=x_ref, dst_ref=out_ref.at[me], send_sem=send_sems.at[k],
                                              recv_sem=recv_sems.at[k], device_id=(px, py, c), device_id_type=MESH)
            cp.start()
            sends.append(cp)
        for k, (px, py) in enumerate(chips):
            pltpu.make_async_remote_copy(src_ref=x_ref, dst_ref=out_ref.at[2 * px + py], send_sem=send_sems.at[k],
                                         recv_sem=recv_sems.at[k], device_id=(px, py, c), device_id_type=MESH).wait_recv()
        for cp in sends:
            cp.wait_send()
        mine.wait()

    return pl.pallas_call(
        body, out_shape=jax.ShapeDtypeStruct((4, r, cdim), x_shard.dtype), in_specs=[ANY], out_specs=ANY,
        scratch_shapes=[pltpu.SemaphoreType.DMA((3,)), pltpu.SemaphoreType.DMA((3,)), pltpu.SemaphoreType.DMA],
        name=name,
    )(x_shard)


def allgather_chips_2level(x_shard, *, name):
    r, cdim = x_shard.shape
    half = r // 2

    def body(x_ref, out_ref, send_sems, recv_sems, local_sem):
        x, y, c, chips = _chip_peers()
        me = 2 * x + y
        mine_rows = pl.ds(c * half, half)
        other_rows = pl.ds((1 - c) * half, half)
        mine = pltpu.make_async_copy(x_ref, out_ref.at[me], local_sem)
        mine.start()

        def copy(k, slot, rows, to, src=None):
            dst = out_ref.at[slot, rows, :]
            return pltpu.make_async_remote_copy(src_ref=dst if src is None else src, dst_ref=dst, send_sem=send_sems.at[k],
                                                recv_sem=recv_sems.at[k], device_id=to, device_id_type=MESH)

        first = [copy(k, me, mine_rows, (px, py, c), src=x_ref.at[mine_rows, :]) for k, (px, py) in enumerate(chips)]
        for cp in first:
            cp.start()
        passed = [copy(3 + k, 2 * px + py, mine_rows, (x, y, 1 - c)) for k, (px, py) in enumerate(chips)]
        for k, (px, py) in enumerate(chips):
            copy(k, 2 * px + py, mine_rows, (px, py, c)).wait_recv()
            passed[k].start()
        for k, (px, py) in enumerate(chips):
            copy(3 + k, 2 * px + py, other_rows, (x, y, 1 - c)).wait_recv()
        for cp in first + passed:
            cp.wait_send()
        mine.wait()

    return pl.pallas_call(
        body, out_shape=jax.ShapeDtypeStruct((4, r, cdim), x_shard.dtype), in_specs=[ANY], out_specs=ANY,
        scratch_shapes=[pltpu.SemaphoreType.DMA((6,)), pltpu.SemaphoreType.DMA((6,)), pltpu.SemaphoreType.DMA],
        name=name,
    )(x_shard)


def exchange_chips(g, *, name):
    _, r, cdim = g.shape

    def body(g_ref, out_ref, send_sems, recv_sems, local_sem):
        x, y, c, chips = _chip_peers()
        me = 2 * x + y
        mine = pltpu.make_async_copy(g_ref.at[me], out_ref.at[me], local_sem)
        mine.start()
        sends = []
        for k, (px, py) in enumerate(chips):
            cp = pltpu.make_async_remote_copy(src_ref=g_ref.at[2 * px + py], dst_ref=out_ref.at[me], send_sem=send_sems.at[k],
                                              recv_sem=recv_sems.at[k], device_id=(px, py, c), device_id_type=MESH)
            cp.start()
            sends.append(cp)
        for k, (px, py) in enumerate(chips):
            pltpu.make_async_remote_copy(src_ref=g_ref.at[me], dst_ref=out_ref.at[2 * px + py], send_sem=send_sems.at[k],
                                         recv_sem=recv_sems.at[k], device_id=(px, py, c), device_id_type=MESH).wait_recv()
        for cp in sends:
            cp.wait_send()
        mine.wait()

    return pl.pallas_call(
        body, out_shape=jax.ShapeDtypeStruct(g.shape, g.dtype), in_specs=[ANY], out_specs=ANY,
        scratch_shapes=[pltpu.SemaphoreType.DMA((3,)), pltpu.SemaphoreType.DMA((3,)), pltpu.SemaphoreType.DMA],
        name=name,
    )(g)


def swap_sibling(p, *, name):
    def body(p_ref, out_ref, send_sem, recv_sem):
        x, y, c = lax.axis_index("x"), lax.axis_index("y"), lax.axis_index("c")
        cp = pltpu.make_async_remote_copy(src_ref=p_ref, dst_ref=out_ref, send_sem=send_sem, recv_sem=recv_sem,
                                          device_id=(x, y, 1 - c), device_id_type=MESH)
        cp.start()
        cp.wait()

    return pl.pallas_call(
        body, out_shape=jax.ShapeDtypeStruct(p.shape, p.dtype), in_specs=[ANY], out_specs=ANY,
        scratch_shapes=[pltpu.SemaphoreType.DMA, pltpu.SemaphoreType.DMA], name=name,
    )(p)


def swap_other_half(g, *, name):
    n, r, cdim = g.shape
    half = r // 2

    def body(g_ref, out_ref, send_sem, recv_sem):
        x, y, c = lax.axis_index("x"), lax.axis_index("y"), lax.axis_index("c")
        cp = pltpu.make_async_remote_copy(src_ref=g_ref.at[:, pl.ds((1 - c) * half, half), :], dst_ref=out_ref, send_sem=send_sem,
                                          recv_sem=recv_sem, device_id=(x, y, 1 - c), device_id_type=MESH)
        cp.start()
        cp.wait()

    return pl.pallas_call(
        body, out_shape=jax.ShapeDtypeStruct((n, half, cdim), g.dtype), in_specs=[ANY], out_specs=ANY,
        scratch_shapes=[pltpu.SemaphoreType.DMA, pltpu.SemaphoreType.DMA], name=name,
    )(g)


def add_pairs(a, b, *, name, out_dtype):
    n, rows, cdim = a.shape
    t = _pick(rows, (256, 128, 64, 32, 16))

    def body(a_ref, b_ref, o_ref):
        o_ref[...] = (a_ref[...].astype(F32) + b_ref[...].astype(F32)).astype(o_ref.dtype)

    spec = pl.BlockSpec((n, t, cdim), lambda i: (0, i, 0))
    return pl.pallas_call(
        body, grid=(rows // t,), in_specs=[spec, spec], out_specs=spec, out_shape=jax.ShapeDtypeStruct(a.shape, out_dtype),
        compiler_params=_cp(("parallel",)), name=name,
    )(a, b)


def sum_slots(r, *, name):
    n, rows, cdim = r.shape
    t = _pick(rows, (256, 128, 64, 32, 16, 8))

    def body(r_ref, o_ref):
        acc = r_ref[0].astype(F32)
        for s in range(1, n):
            acc = acc + r_ref[s].astype(F32)
        o_ref[...] = acc

    return pl.pallas_call(
        body, grid=(rows // t,), in_specs=[pl.BlockSpec((n, t, cdim), lambda i: (0, i, 0))],
        out_specs=pl.BlockSpec((t, cdim), lambda i: (i, 0)), out_shape=jax.ShapeDtypeStruct((rows, cdim), F32),
        compiler_params=_cp(("parallel",)), name=name,
    )(r)


def _rms(x, g):
    return x * lax.rsqrt(jnp.mean(x * x, axis=-1, keepdims=True) + NORM_EPS) * g


def _adaln(x, g, shift, scale):
    return _rms(x, g) * (1.0 + scale) + shift


def _silu(x):
    return x * jax.nn.sigmoid(x)


def _gdn_gate(o, z, g):
    return jnp.concatenate([_rms(o[:, 128 * h:128 * (h + 1)], g) * _silu(z[:, 128 * h:128 * (h + 1)]) for h in range(4)], axis=1)


def _ssd_gate(y, z0, z1, z2, z3, g):
    outs = []
    for k, z in enumerate((z0, z1, z2, z3)):
        t = y[:, 512 * k:512 * (k + 1)] * _silu(z)
        outs.append(t * lax.rsqrt(jnp.mean(t * t, axis=-1, keepdims=True) + NORM_EPS))
    return jnp.concatenate(outs, axis=1) * g


def _rope(x, cos, sin, rot):
    return x * cos + _dot(x, rot, hi=True) * sin


def _rope_q(q, cos, sin, rot):
    parts = []
    for h in range(4):
        parts += [q[:, 256 * h:256 * h + 128], _rope(q[:, 256 * h + 128:256 * (h + 1)], cos, sin, rot)]
    return jnp.concatenate(parts, axis=1) * ATT_SCALE


def _rope_t(d, cos, sin, rot):
    return d * cos + _dot(d * sin, rot, _NT, hi=True)


def _vjp_rows(fn, n_rows, n_pars, out_dtypes, rows, cts, pars, *, name, tile=512, extra=None):
    nct = len(cts)

    def bwd(*a):
        r, c, e, p = a[:n_rows], a[n_rows:n_rows + nct], a[n_rows + nct:len(a) - n_pars], a[len(a) - n_pars:]
        out, vjp = jax.vjp(fn, *[t.astype(F32) for t in r], *p)
        ct = tuple(t.astype(F32) for t in c)
        grads = vjp(ct[0] if not isinstance(out, tuple) else ct)
        drows = list(grads[:n_rows])
        if e:
            drows[0] = drows[0] + e[0]
        return (*drows, *grads[n_rows:])

    return rowmap(bwd, list(rows) + list(cts) + ([extra] if extra is not None else []), list(pars), out_dtypes,
                  name=name, tile=tile, n_reduce=n_pars)


def _gate_grads(dw_raw, w, gate, scale, *, name):
    dw, dg = rowmap(lambda r, w_, gt: ((scale * gt) * r, jnp.sum((scale * w_.astype(F32)) * r, axis=0, keepdims=True)),
                    [dw_raw, w], [gate], (F32, F32), name=name, tile=256, n_reduce=1)
    return dw, dg[0]


ADAM_LR, ADAM_B1, ADAM_B2, ADAM_EPS, ADAM_WD, ADAM_STEP = 0.001, 0.9, 0.999, 1e-08, 0.01, 10


def _adam_math(w, g, m, v):
    m = ADAM_B1 * m + (1.0 - ADAM_B1) * g
    v = ADAM_B2 * v + (1.0 - ADAM_B2) * (g * g)
    m_hat = m / (1.0 - ADAM_B1 ** ADAM_STEP)
    v_hat = v / (1.0 - ADAM_B2 ** ADAM_STEP)
    delta = -ADAM_LR * (m_hat / (jnp.sqrt(v_hat) + ADAM_EPS) + ADAM_WD * w)
    return delta, m, v


def adamw(w, gs, m, v, *, name):
    shape = w.shape
    last = shape[-1]
    to2 = lambda a: a.reshape(-1, last)
    rows = w.size // last
    tile = _pick(rows, (256, 128, 64, 32, 16, 8))
    ng = len(gs)

    def fn(w_, *rest):
        g = rest[0]
        for t in rest[1:ng]:
            g = g + t
        m_, v_ = rest[ng], rest[ng + 1]
        return (g, *_adam_math(w_, g, m_, v_))

    outs = rowmap(fn, [to2(w)] + [to2(g) for g in gs] + [to2(m), to2(v)], [], (F32,) * 4, name=name, tile=tile)
    return tuple(o.reshape(shape) for o in outs)


PACK_W = 1024
BIG = (
    ("ffn_w1", (4, 2, 1024, 704), 3), ("ffn_w3", (4, 2, 1024, 704), 3), ("ffn_w2", (4, 2, 704, 1024), 2),
    ("ev_w_in", (2, 1024, 690), 2), ("mla_w_uq", (2, 96, 4, 192), 1), ("mla_w_ukv", (2, 64, 4, 256), 1),
    ("ev_w_out", (2, 256, 1024), 1), ("ssd_w_in", (2, 1024, 1288), 2), ("ssd_w_out", (2, 512, 1024), 1))


def _seg_rows(shape):
    n = math.prod(shape)
    return -(-n // (16 * PACK_W)) * 16


PACK_ROWS = -(-sum(_seg_rows(sh) for _, sh, _ in BIG) // 512) * 512


def _pack(shards, dtype):
    parts = []
    for (_, shape, _), a in zip(BIG, shards):
        flat = a.reshape(-1).astype(dtype)
        pad = _seg_rows(shape) * PACK_W - flat.shape[0]
        parts.append(jnp.pad(flat, (0, pad)) if pad else flat)
    tail = PACK_ROWS - sum(_seg_rows(sh) for _, sh, _ in BIG)
    if tail:
        parts.append(jnp.zeros((tail * PACK_W,), dtype))
    return jnp.concatenate(parts).reshape(-1, PACK_W)


def _pack_by_owner(fulls):
    cols = []
    for (_, shape, ax), f in zip(BIG, fulls):
        blk = jnp.stack([lax.slice_in_dim(f, s * shape[ax], (s + 1) * shape[ax], axis=ax).reshape(-1).astype(BF16) for s in range(4)])
        pad = _seg_rows(shape) * PACK_W - blk.shape[1]
        cols.append((jnp.pad(blk, ((0, 0), (0, pad))) if pad else blk).reshape(4, -1, PACK_W))
    tail = PACK_ROWS - sum(_seg_rows(sh) for _, sh, _ in BIG)
    if tail:
        cols.append(jnp.zeros((4, tail, PACK_W), BF16))
    return jnp.concatenate(cols, axis=1)


def _unpack(buf):
    out, r0 = [], 0
    for _, shape, _ in BIG:
        n = math.prod(shape)
        out.append(buf[r0:r0 + _seg_rows(shape)].reshape(-1)[:n].reshape(shape))
        r0 += _seg_rows(shape)
    return out


SMALL_SHARDED = (
    ("norm_g", (4, 3, 256), 2), ("gdn_conv_w", (2, 4, 384), 2), ("ssd_conv_w", (2, 4, 768), 2),
    ("ssd_conv_b", (2, 768), 1), ("ssd_norm_g", (2, 512), 1))


def _flat_pack(arrs, width, row_mult):
    flat = jnp.concatenate([a.reshape(-1).astype(F32) for a in arrs])
    n = flat.shape[0]
    tot = -(-n // (width * row_mult)) * width * row_mult
    return jnp.pad(flat, (0, tot - n)).reshape(-1, width)


def _flat_unpack(buf, shapes):
    flat = buf.reshape(-1)
    out, o = [], 0
    for s in shapes:
        n = math.prod(s)
        out.append(flat[o:o + n].reshape(s))
        o += n
    return out


def _rep(v, n):
    return jnp.repeat(v, n, axis=-1)


def kernel(x, c, positions, ada_w, ada_b, norm_g, ffn_w1, ffn_w3, ffn_w2, ev_w_in, gdn_conv_w, gdn_A_log, gdn_dt_bias, gdn_norm_g, mla_q_norm_g, mla_w_uq, mla_kv_norm_g, mla_w_ukv, ev_w_out, ssd_w_in, ssd_conv_w, ssd_conv_b, ssd_A_log, ssd_dt_bias, ssd_D, ssd_norm_g, ssd_w_out, final_g, loss_target, m_ada_w, m_ada_b, m_norm_g, m_ffn_w1, m_ffn_w3, m_ffn_w2, m_ev_w_in, m_gdn_conv_w, m_gdn_A_log, m_gdn_dt_bias, m_gdn_norm_g, m_mla_q_norm_g, m_mla_w_uq, m_mla_kv_norm_g, m_mla_w_ukv, m_ev_w_out, m_ssd_w_in, m_ssd_conv_w, m_ssd_conv_b, m_ssd_A_log, m_ssd_dt_bias, m_ssd_D, m_ssd_norm_g, m_ssd_w_out, m_final_g, v_ada_w, v_ada_b, v_norm_g, v_ffn_w1, v_ffn_w3, v_ffn_w2, v_ev_w_in, v_gdn_conv_w, v_gdn_A_log, v_gdn_dt_bias, v_gdn_norm_g, v_mla_q_norm_g, v_mla_w_uq, v_mla_kv_norm_g, v_mla_w_ukv, v_ev_w_out, v_ssd_w_in, v_ssd_conv_w, v_ssd_conv_b, v_ssd_A_log, v_ssd_dt_bias, v_ssd_D, v_ssd_norm_g, v_ssd_w_out, v_final_g):
    P = dict(ada_w=ada_w, ada_b=ada_b, norm_g=norm_g, ffn_w1=ffn_w1, ffn_w3=ffn_w3, ffn_w2=ffn_w2, ev_w_in=ev_w_in, gdn_conv_w=gdn_conv_w, gdn_A_log=gdn_A_log, gdn_dt_bias=gdn_dt_bias, gdn_norm_g=gdn_norm_g, mla_q_norm_g=mla_q_norm_g, mla_w_uq=mla_w_uq, mla_kv_norm_g=mla_kv_norm_g, mla_w_ukv=mla_w_ukv, ev_w_out=ev_w_out, ssd_w_in=ssd_w_in, ssd_conv_w=ssd_conv_w, ssd_conv_b=ssd_conv_b, ssd_A_log=ssd_A_log, ssd_dt_bias=ssd_dt_bias, ssd_D=ssd_D, ssd_norm_g=ssd_norm_g, ssd_w_out=ssd_w_out, final_g=final_g)
    M1 = dict(ada_w=m_ada_w, ada_b=m_ada_b, norm_g=m_norm_g, ffn_w1=m_ffn_w1, ffn_w3=m_ffn_w3, ffn_w2=m_ffn_w2, ev_w_in=m_ev_w_in, gdn_conv_w=m_gdn_conv_w, gdn_A_log=m_gdn_A_log, gdn_dt_bias=m_gdn_dt_bias, gdn_norm_g=m_gdn_norm_g, mla_q_norm_g=m_mla_q_norm_g, mla_w_uq=m_mla_w_uq, mla_kv_norm_g=m_mla_kv_norm_g, mla_w_ukv=m_mla_w_ukv, ev_w_out=m_ev_w_out, ssd_w_in=m_ssd_w_in, ssd_conv_w=m_ssd_conv_w, ssd_conv_b=m_ssd_conv_b, ssd_A_log=m_ssd_A_log, ssd_dt_bias=m_ssd_dt_bias, ssd_D=m_ssd_D, ssd_norm_g=m_ssd_norm_g, ssd_w_out=m_ssd_w_out, final_g=m_final_g)
    M2 = dict(ada_w=v_ada_w, ada_b=v_ada_b, norm_g=v_norm_g, ffn_w1=v_ffn_w1, ffn_w3=v_ffn_w3, ffn_w2=v_ffn_w2, ev_w_in=v_ev_w_in, gdn_conv_w=v_gdn_conv_w, gdn_A_log=v_gdn_A_log, gdn_dt_bias=v_gdn_dt_bias, gdn_norm_g=v_gdn_norm_g, mla_q_norm_g=v_mla_q_norm_g, mla_w_uq=v_mla_w_uq, mla_kv_norm_g=v_mla_kv_norm_g, mla_w_ukv=v_mla_w_ukv, ev_w_out=v_ev_w_out, ssd_w_in=v_ssd_w_in, ssd_conv_w=v_ssd_conv_w, ssd_conv_b=v_ssd_conv_b, ssd_A_log=v_ssd_A_log, ssd_dt_bias=v_ssd_dt_bias, ssd_D=v_ssd_D, ssd_norm_g=v_ssd_norm_g, ssd_w_out=v_ssd_w_out, final_g=v_final_g)
    names = list(P)
    xi, yi, ci = lax.axis_index("x"), lax.axis_index("y"), lax.axis_index("c")
    chip = 2 * xi + yi
    bidx = 4 * xi + 2 * yi + ci
    xa = x[0]
    S_, D = xa.shape
    tgt = loss_target[0]
    depth = ffn_w1.shape[0]

    wg = allgather_chips_2level(_pack([P[n] for n, _, _ in BIG], BF16), name="gather_weights")
    per_chip = [_unpack(wg[s]) for s in range(4)]
    W = {n: jnp.concatenate([per_chip[s][k] for s in range(4)], axis=ax) for k, (n, _, ax) in enumerate(BIG)}
    sg = allgather_chips(_flat_pack([P[n] for n, _, _ in SMALL_SHARDED], 1024, 16), name="gather_small")
    per_chip_s = [_flat_unpack(sg[s], [sh for _, sh, _ in SMALL_SHARDED]) for s in range(4)]
    Wf = {n: jnp.concatenate([per_chip_s[s][k] for s in range(4)], axis=ax) for k, (n, _, ax) in enumerate(SMALL_SHARDED)}

    c_all = allgather8(jnp.pad(c, ((0, 7), (0, 0))), name="gather_c").reshape(8, 8, D)[:, 0]
    c_act, = rowmap(lambda t: (_silu(t),), [jnp.pad(c_all, ((0, 8), (0, 0)))], [], (F32,), name="c_act", tile=16)
    ncol = ada_w.shape[2]
    ada_b_loc = lax.dynamic_slice(ada_b, (0, chip * ncol), (depth, ncol))
    mod_loc = [mm((c_act, ada_w[l]), name=f"mod_{l}", epi=lambda acc, b: (acc + b,), epi_pars=(ada_b_loc[l][None],),
                  epi_out_dtypes=(F32,), tm=16, tn=256)[0][:8] for l in range(depth)]
    mod_g = allgather8(jnp.stack(mod_loc).reshape(-1, 1024), name="gather_mod").reshape(8, depth, 8, ncol)
    mod_b = lax.dynamic_index_in_dim(mod_g[0::2], bidx, axis=2, keepdims=False)
    mod = jnp.transpose(mod_b, (1, 0, 2)).reshape(depth, 3, 3, D)

    def ev_ext(w):
        z = lambda n: jnp.zeros((w.shape[0], n), w.dtype)
        return jnp.concatenate([w[:, :2048], _rep(w[:, 2048:2052], 128), _rep(w[:, 2052:2056], 128), w[:, 2056:2440], z(128),
                                w[:, 2440:2696], w[:, 2696:2760], z(192)], axis=1)

    def ev_ext_t(dw):
        return jnp.concatenate([dw[:, :2048], dw[:, 2048:2560].reshape(-1, 4, 128).sum(-1), dw[:, 2560:3072].reshape(-1, 4, 128).sum(-1),
                                dw[:, 3072:3456], dw[:, 3584:3840], dw[:, 3840:3904]], axis=1)

    def od_ext(w):
        return jnp.concatenate([w[:, 2048:5120], w[:, :2048], _rep(w[:, 5120:5152], 64)], axis=1)

    def od_ext_t(dw):
        return jnp.concatenate([dw[:, 3072:5120], dw[:, :3072], dw[:, 5120:].reshape(-1, 32, 64).sum(-1)], axis=1)

    def wq_ext(w):
        return jnp.pad(w, ((0, 0), (0, 0), (0, 64))).reshape(384, 1024)

    def wq_ext_t(dw):
        return dw.reshape(384, 4, 256)[:, :, :192]

    def wkv_ext(w):
        return jnp.concatenate([w[:, :, :128].reshape(256, 512), w[:, :, 128:].reshape(256, 512)], axis=1)

    def wkv_ext_t(dw):
        return jnp.concatenate([dw[:, :512].reshape(256, 4, 128), dw[:, 512:].reshape(256, 4, 128)], axis=2)

    half = 32
    inv_freq = 10000.0 ** (-jnp.arange(half, dtype=F32) / half)
    ang = positions[0].astype(F32)[:, None] * inv_freq
    zpad = jnp.zeros((S_, 64), F32)
    cos_t = jnp.concatenate([jnp.cos(ang), jnp.cos(ang), zpad], axis=1)
    sin_t = jnp.concatenate([jnp.sin(ang), jnp.sin(ang), zpad], axis=1)
    ii = jnp.arange(128)
    rot = (jnp.where((ii[:, None] < 32) & (ii[None, :] == ii[:, None] + 32), 1.0, 0.0)
           - jnp.where((ii[:, None] >= 32) & (ii[:, None] < 64) & (ii[None, :] == ii[:, None] - 32), 1.0, 0.0)).astype(F32)

    grads = {}
    dmod = [[[None] * 3 for _ in range(3)] for _ in range(depth)]
    dnorm_g = [[None] * 3 for _ in range(depth)]

    def acc(name, idx, val):
        grads.setdefault(name, {})[idx] = val

    def ffn_sub(xin, l, k, j):
        g, (shift, scale, gate) = Wf["norm_g"][l, k][None], [mod[l, k, t][None] for t in range(3)]
        w1, w3, w2 = W["ffn_w1"][l, j], W["ffn_w3"][l, j], W["ffn_w2"][l, j]
        tag = f"l{l}f{j}"
        h, = rowmap(lambda *a: (_adaln(*a),), [xin], [g, shift, scale], (BF16,), name=f"adaln_{tag}", tile=1024)
        a, u16, v16 = ffn_mid_fwd(h, w1, w3, name=f"ffn_mid_{tag}")
        xn, = mm((a, w2), name=f"ffn_out_{tag}", epi=lambda acc_, xr, gt: (xr + 0.5 * gt * acc_,), epi_rows=(xin,),
                 epi_pars=(gate,), epi_out_dtypes=(F32,))

        def bwd(dxn):
            du, dv = ffn_mid_bwd(u16, v16, dxn, ((0.5 * gate) * w2).astype(BF16), name=f"ffn_midb_{tag}")
            dh = mm([(du, w1), (dv, w3)], tb=True, name=f"ffn_dh_{tag}")
            acc("ffn_w1", (l, j), mm((h, du), ta=True, name=f"ffn_dw1_{tag}"))
            acc("ffn_w3", (l, j), mm((h, dv), ta=True, name=f"ffn_dw3_{tag}"))
            dw2, dgate = _gate_grads(mm((a, dxn), ta=True, name=f"ffn_dw2_{tag}"), w2, gate, 0.5, name=f"ffn_dgate_{tag}")
            acc("ffn_w2", (l, j), dw2)
            dx, dg, dsh, dsc = _vjp_rows(_adaln, 1, 3, (F32,), [xin], [dh], [g, shift, scale], name=f"adalnb_{tag}", extra=dxn, tile=1024)
            dnorm_g[l][k] = dg[0]
            dmod[l][k] = [dsh[0], dsc[0], dgate]
            return dx

        return xn, bwd

    def mixer_tail(xin, l, tag, dh, dxn, g, shift, scale, dgate):
        dx, dg, dsh, dsc = _vjp_rows(_adaln, 1, 3, (F32,), [xin], [dh], [g, shift, scale], name=f"adalnb_{tag}", extra=dxn, tile=1024)
        dnorm_g[l][1] = dg[0]
        dmod[l][1] = [dsh[0], dsc[0], dgate]
        return dx

    def even_sub(xin, l):
        e = l // 2
        tag = f"l{l}m"
        g, (shift, scale, gate) = Wf["norm_g"][l, 1][None], [mod[l, 1, t][None] for t in range(3)]
        wext, wq, wkv, wout = ev_ext(W["ev_w_in"][e]), wq_ext(W["mla_w_uq"][e]), wkv_ext(W["mla_w_ukv"][e]), W["ev_w_out"][e]
        conv_w, zb = Wf["gdn_conv_w"][e], jnp.zeros((1, 1536), F32)
        alog_e, dtb_e = _rep(gdn_A_log[e], 128)[None], _rep(gdn_dt_bias[e], 128)[None]
        gg, qg, kvg = gdn_norm_g[e][None], mla_q_norm_g[e][None], mla_kv_norm_g[e][None]
        h, = rowmap(lambda *a: (_adaln(*a),), [xin], [g, shift, scale], (BF16,), name=f"adaln_{tag}", tile=1024)
        proj = mm((h, wext), name=f"ev_in_{tag}")
        qkvc = conv_fwd(proj, 0, 3, conv_w, zb, name=f"gdn_conv_{tag}")
        o_g, hist = gdn_fwd(qkvc, proj, 4, 5, alog_e, dtb_e, name=f"gdn_{tag}")
        o_a, = rowmap(lambda o, z, g_: (_gdn_gate(o, z, g_),), [o_g, (proj, 512, 3)], [gg], (BF16,), name=f"gdn_gate_{tag}")
        cqn, = rowmap(lambda t, g_: (_rms(t, g_),), [(proj, 384, 8)], [qg], (BF16,), name=f"q_norm_{tag}")
        ckvn, = rowmap(lambda t, g_: (_rms(t, g_),), [(proj, 256, 14)], [kvg], (BF16,), name=f"kv_norm_{tag}")
        q0 = mm((cqn, wq), name=f"q_up_{tag}")
        kv = mm((ckvn, wkv), name=f"kv_up_{tag}", out_dtype=BF16)
        q, = rowmap(lambda t, cs, sn, r: (_rope_q(t, cs, sn, r),), [q0, cos_t, sin_t], [rot], (BF16,), name=f"rope_q_{tag}")
        kp, = rowmap(lambda t, cs, sn, r: (_rope(t, cs, sn, r),), [(proj, 128, 30), cos_t, sin_t], [rot], (BF16,), name=f"rope_k_{tag}")
        o_b, lse = att_fwd(q, kv, kp, name=f"att_{tag}")
        xn, = mm([(o_a, wout[:512]), (o_b, wout[512:])], name=f"ev_out_{tag}", epi=lambda acc_, xr, gt: (xr + gt * acc_,),
                 epi_rows=(xin,), epi_pars=(gate,), epi_out_dtypes=(F32,))

        def bwd(dxn):
            wout_g = (gate * wout).astype(BF16)
            do_a = mm((dxn, wout_g[:512]), tb=True, name=f"ev_doa_{tag}")
            do_b = mm((dxn, wout_g[512:]), tb=True, name=f"ev_dob_{tag}")
            dw_raw = jnp.concatenate([mm((o_a, dxn), ta=True, name=f"ev_dwoa_{tag}"), mm((o_b, dxn), ta=True, name=f"ev_dwob_{tag}")], axis=0)
            dwo, dgate = _gate_grads(dw_raw, wout, gate, 1.0, name=f"ev_dgate_{tag}")
            acc("ev_w_out", e, dwo)
            dsum, = rowmap(lambda d, o_: (jnp.concatenate([jnp.broadcast_to(jnp.sum(d[:, 128 * hh:128 * (hh + 1)] * o_[:, 128 * hh:128 * (hh + 1)],
                                                                                        axis=-1, keepdims=True), (d.shape[0], 128))
                                                  for hh in range(4)], axis=1),), [do_b, o_b], [], (F32,), name=f"att_dsum_{tag}")
            dq4, dk2, dv = att_bwd(q, kv, kp, lse, dsum, do_b, name=f"att_bwd_{tag}")

            def rope_qb(d0, d1, d2, d3, cs, sn, r):
                parts = []
                for d in (d0, d1, d2, d3):
                    parts += [d[:, :128], _rope_t(d[:, 128:], cs, sn, r)]
                return (jnp.concatenate(parts, axis=1) * ATT_SCALE,)

            dq0, = rowmap(rope_qb, [dq4[0], dq4[1], dq4[2], dq4[3], cos_t, sin_t], [rot], (BF16,), name=f"rope_qb_{tag}")

            def rope_kb(d, cs, sn, r):
                dkp = d[:, 128:256] + d[:, 384:512] + d[:, 640:768] + d[:, 896:1024]
                return jnp.concatenate([d[:, 256 * hh:256 * hh + 128] for hh in range(4)], axis=1), _rope_t(dkp, cs, sn, r)

            dkn, dkr = rowmap(rope_kb, [dk2, cos_t, sin_t], [rot], (BF16, BF16), name=f"rope_kb_{tag}")
            dcqn = mm((dq0, wq), tb=True, name=f"q_upb_{tag}")
            acc("mla_w_uq", e, wq_ext_t(mm((cqn, dq0), ta=True, name=f"q_dw_{tag}")))
            dckvn = mm([(dkn, wkv[:, :512]), (dv, wkv[:, 512:])], tb=True, name=f"kv_upb_{tag}")
            acc("mla_w_ukv", e, wkv_ext_t(jnp.concatenate([mm((ckvn, dkn), ta=True, name=f"kv_dwk_{tag}"), mm((ckvn, dv), ta=True, name=f"kv_dwv_{tag}")], axis=1)))
            dcq, dqg = _vjp_rows(_rms, 1, 1, (BF16,), [(proj, 384, 8)], [dcqn], [qg], name=f"q_normb_{tag}")
            dckv, dkvg = _vjp_rows(_rms, 1, 1, (BF16,), [(proj, 256, 14)], [dckvn], [kvg], name=f"kv_normb_{tag}")
            acc("mla_q_norm_g", e, dqg[0])
            acc("mla_kv_norm_g", e, dkvg[0])
            do_g, dz, dgg = _vjp_rows(_gdn_gate, 2, 1, (F32, BF16), [o_g, (proj, 512, 3)], [do_a], [gg], name=f"gdn_gateb_{tag}")
            acc("gdn_norm_g", e, dgg[0])
            dqkvc, dbe, dae, dal, ddt = gdn_bwd(qkvc, proj, 4, 5, alog_e, dtb_e, hist, do_g, name=f"gdnb_{tag}")
            acc("gdn_A_log", e, dal.reshape(4, 128).sum(-1))
            acc("gdn_dt_bias", e, ddt.reshape(4, 128).sum(-1))
            dpre, dcw, _ = conv_bwd_pre(proj, 0, 3, conv_w, zb, dqkvc, name=f"gdn_convb_{tag}")
            acc("gdn_conv_w", e, dcw)
            dqkv = conv_bwd_x(dpre, conv_w, name=f"gdn_convx_{tag}", out_dtype=BF16)
            zc = lambda n: jnp.zeros((S_, n), BF16)
            dproj = jnp.concatenate([dqkv, dz, dbe, dae, dcq, zc(128), dckv, dkr, zc(128)], axis=1)
            dh = mm((dproj, wext), tb=True, name=f"ev_inb_{tag}")
            acc("ev_w_in", e, ev_ext_t(mm((h, dproj), ta=True, name=f"ev_dwin_{tag}")))
            return mixer_tail(xin, l, tag, dh, dxn, g, shift, scale, dgate)

        return xn, bwd

    def odd_sub(xin, l):
        o = l // 2
        tag = f"l{l}m"
        g, (shift, scale, gate) = Wf["norm_g"][l, 1][None], [mod[l, 1, t][None] for t in range(3)]
        wext, wout = od_ext(W["ssd_w_in"][o]), W["ssd_w_out"][o]
        conv_w, conv_b, ng = Wf["ssd_conv_w"][o], Wf["ssd_conv_b"][o][None], Wf["ssd_norm_g"][o][None]
        ex = lambda v: _rep(v, 64)[None]
        na_e, dtb_e, dsk_e = ex(-jnp.exp(ssd_A_log[o])), ex(ssd_dt_bias[o]), ex(ssd_D[o])
        h, = rowmap(lambda *a: (_adaln(*a),), [xin], [g, shift, scale], (BF16,), name=f"adaln_{tag}", tile=1024)
        proj = mm((h, wext), name=f"ssd_in_{tag}")
        zv = [(proj, 512, 6 + t) for t in range(4)]
        xbc = conv_fwd(proj, 0, 6, conv_w, conv_b, name=f"ssd_conv_{tag}")
        ys, hist = ssd_fwd(xbc, proj, na_e, dtb_e, dsk_e, name=f"ssd_{tag}")
        yn, = rowmap(lambda *a: (_ssd_gate(*a),), [ys] + zv, [ng], (BF16,), name=f"ssd_gate_{tag}", tile=256)
        xn, = mm((yn, wout), name=f"ssd_out_{tag}", epi=lambda acc_, xr, gt: (xr + gt * acc_,), epi_rows=(xin,),
                 epi_pars=(gate,), epi_out_dtypes=(F32,))

        def bwd(dxn):
            dyn = mm((dxn, (gate * wout).astype(BF16)), tb=True, name=f"ssd_dyn_{tag}", out_dtype=BF16)
            dwo, dgate = _gate_grads(mm((yn, dxn), ta=True, name=f"ssd_dwout_{tag}"), wout, gate, 1.0, name=f"ssd_dgate_{tag}")
            acc("ssd_w_out", o, dwo)
            dys, dz0, dz1, dz2, dz3, dng = _vjp_rows(_ssd_gate, 5, 1, (F32, BF16, BF16, BF16, BF16), [ys] + zv, [dyn], [ng],
                                                     name=f"ssd_gateb_{tag}", tile=256)
            acc("ssd_norm_g", o, dng[0])
            dxs, ddtx, db_, dc_, dna, ddtb, ddsk = ssd_bwd(xbc, proj, na_e, dtb_e, dsk_e, hist, dys, name=f"ssdb_{tag}")
            acc("ssd_A_log", o, dna.reshape(32, 64).sum(-1) * (-jnp.exp(ssd_A_log[o])))
            acc("ssd_dt_bias", o, ddtb.reshape(32, 64).sum(-1))
            acc("ssd_D", o, ddsk.reshape(32, 64).sum(-1))
            dxp, dcws, dcbs = [], [], []
            for part, (cb0, ncb, dpart) in enumerate(((0, 4, dxs), (4, 1, db_), (5, 1, dc_))):
                cols = slice(512 * cb0, 512 * (cb0 + ncb))
                dpre, dcw, dcb = conv_bwd_pre(proj, cb0, ncb, conv_w[:, cols], conv_b[:, cols], dpart, name=f"ssd_convb{part}_{tag}")
                dxp.append(conv_bwd_x(dpre, conv_w[:, cols], name=f"ssd_convx{part}_{tag}", out_dtype=BF16))
                dcws.append(dcw)
                dcbs.append(dcb[0])
            acc("ssd_conv_w", o, jnp.concatenate(dcws, axis=1))
            acc("ssd_conv_b", o, jnp.concatenate(dcbs))
            dproj = jnp.concatenate(dxp + [dz0, dz1, dz2, dz3, ddtx], axis=1)
            dh = mm((dproj, wext), tb=True, name=f"ssd_inb_{tag}")
            acc("ssd_w_in", o, od_ext_t(mm((h, dproj), ta=True, name=f"ssd_dwin_{tag}")))
            return mixer_tail(xin, l, tag, dh, dxn, g, shift, scale, dgate)

        return xn, bwd

    tape = []
    xc = xa
    for l in range(depth):
        xc, b0 = ffn_sub(xc, l, 0, 0)
        xc, b1 = (even_sub if l % 2 == 0 else odd_sub)(xc, l)
        xc, b2 = ffn_sub(xc, l, 2, 1)
        tape += [b0, b1, b2]

    def head(xr, tg, g_):
        def f(xv, gv):
            err = _rms(xv, gv) - tg
            return 0.5 * jnp.sum(jnp.mean(err * err, axis=-1, keepdims=True), axis=0, keepdims=True)
        lo, vjp = jax.vjp(f, xr, g_)
        dxv, dgv = vjp(jnp.ones_like(lo))
        return dxv, jnp.broadcast_to(lo, (1, 128)), dgv

    dx, loss_p, dfg = rowmap(head, [xc, tgt], [final_g[None]], (F32,), name="loss_head", n_reduce=2)
    loss = lax.psum(loss_p[0, 0], ("x", "y", "c"))

    for b in reversed(tape):
        dx = b(dx)
    grad_x = dx[None]

    full = {n: jnp.stack([grads[n][k] for k in sorted(grads[n])]) for n in ("ev_w_in", "mla_w_uq", "mla_w_ukv", "ev_w_out", "ssd_w_in", "ssd_w_out")}
    for n in ("ffn_w1", "ffn_w3", "ffn_w2"):
        full[n] = jnp.stack([jnp.stack([grads[n][(l, j)] for j in range(2)]) for l in range(depth)])
    gall = _pack_by_owner([full[n] for n, _, _ in BIG])
    half = PACK_ROWS // 2
    from_sib = swap_other_half(gall, name="swap_half")
    pair = add_pairs(lax.dynamic_slice_in_dim(gall, ci * half, half, axis=1), from_sib, name="add_sibling", out_dtype=BF16)
    recv = exchange_chips(pair, name="exchange_grads")
    part = sum_slots(recv, name="sum_chips")
    sib = swap_sibling(part, name="swap_sibling")
    lo = jnp.where(ci == 0, part, sib)
    hi_ = jnp.where(ci == 0, sib, part)
    g_tot = _unpack(jnp.concatenate([lo, hi_], axis=0))

    dmod_flat = jnp.stack([jnp.stack([jnp.stack(dmod[l][k]) for k in range(3)]) for l in range(depth)]).reshape(depth, 9 * D)
    small_names = ["norm_g", "gdn_conv_w", "gdn_A_log", "gdn_dt_bias", "gdn_norm_g", "mla_q_norm_g", "mla_kv_norm_g",
                   "ssd_conv_w", "ssd_conv_b", "ssd_A_log", "ssd_dt_bias", "ssd_D", "ssd_norm_g", "final_g"]
    small_full = {n: jnp.stack([grads[n][k] for k in sorted(grads[n])]) for n in small_names if n in grads}
    small_full["norm_g"] = jnp.stack([jnp.stack(dnorm_g[l]) for l in range(depth)])
    small_full["final_g"] = dfg[0]
    small_list = [dmod_flat] + [small_full[n] for n in small_names]
    small_shapes = [a.shape for a in small_list]
    sp = _flat_pack(small_list, 128, 8)
    sgath = allgather8(sp, name="gather_small_grads").reshape(8, sp.shape[0], 128)
    ssum = sum_slots(sgath, name="sum_small")
    tot = dict(zip(["ada_b"] + small_names, _flat_unpack(ssum, small_shapes)))
    dmod_all = sgath.reshape(8, -1)[:, :depth * 9 * D].reshape(8, depth, 9 * D)
    dmod_loc = lax.dynamic_slice(dmod_all, (0, 0, chip * ncol), (8, depth, ncol))
    g_ada_w = jnp.stack([mm((c_act, jnp.pad(dmod_loc[:, l], ((0, 8), (0, 0)))), ta=True, name=f"ada_dw_{l}", tk=16, tn=256)
                         for l in range(depth)])

    def own(n, a):
        for m_, sh, ax in SMALL_SHARDED:
            if m_ == n:
                return lax.dynamic_slice_in_dim(a, chip * sh[ax], sh[ax], axis=ax)
        return a

    res = {}
    for k, (n, _, _) in enumerate(BIG):
        res[n] = adamw(P[n], [g_tot[k]], M1[n], M2[n], name=f"adamw_{n}")
    res["ada_w"] = adamw(ada_w, [g_ada_w], m_ada_w, v_ada_w, name="adamw_ada_w")
    sm = ["ada_b"] + small_names
    shapes = [P[n].shape for n in sm]
    pk = lambda d: _flat_pack([d[n] for n in sm], 128, 8)
    outs = adamw(pk(P), [pk({n: own(n, tot[n]).reshape(P[n].shape) for n in sm})], pk(M1), pk(M2), name="adamw_small")
    un = [_flat_unpack(o, shapes) for o in outs]
    for i, n in enumerate(sm):
        res[n] = tuple(un[t][i] for t in range(4))
    return (loss, grad_x, *[res[n][0] for n in names], *[res[n][1] for n in names], *[res[n][2] for n in names], *[res[n][3] for n in names])
```

```python
import functools
import math

import jax
import jax.numpy as jnp
from jax import lax
from jax.experimental import pallas as pl
from jax.experimental.pallas import tpu as pltpu

F32 = jnp.float32
BF16 = jnp.bfloat16
HI = lax.Precision.HIGHEST
HI3 = lax.Precision.HIGH
VMEM_LIMIT = 56 * 1024 * 1024
NORM_EPS = 1e-6
MM_VMEM_BUDGET = 40 * 1024 * 1024


def _cp(sem=None):
    if sem is None:
        return pltpu.CompilerParams(vmem_limit_bytes=VMEM_LIMIT)
    return pltpu.CompilerParams(dimension_semantics=sem, vmem_limit_bytes=VMEM_LIMIT)


def _pick(dim, prefs):
    for p in prefs:
        if dim % p == 0:
            return p
    return dim


def mm(pairs, *, ta=False, tb=False, out_dtype=F32, name, epi=None, epi_rows=(), epi_pars=(), epi_out_dtypes=None,
       tm=None, tn=None, tk=None):
    if not isinstance(pairs, (list, tuple)) or not isinstance(pairs[0], (list, tuple)):
        pairs = [pairs]
    npair = len(pairs)
    a0, b0 = pairs[0]
    M = a0.shape[1] if ta else a0.shape[0]
    K = a0.shape[0] if ta else a0.shape[1]
    N = b0.shape[0] if tb else b0.shape[1]
    for a, b in pairs:
        assert (a.shape == ((K, M) if ta else (M, K))), (a.shape, M, K)
        assert (b.shape == ((N, K) if tb else (K, N))), (b.shape, K, N)
    tm = tm or _pick(M, (1024, 1408, 512, 384, 256, 128))
    tk = tk or (K if K <= 1024 else _pick(K, (1024, 1408, 512, 256, 128)))
    if tn is None:
        n_epi_out = 1 if epi is None else len(epi_out_dtypes)
        for tn in (1024, 1408, 512, 384, 256, 128, N):
            if N % tn:
                continue
            need = sum(2 * tk * (tm * a.dtype.itemsize + tn * b.dtype.itemsize) for a, b in pairs)
            need += tm * tn * 4 * (1 + 2 * n_epi_out + 2 * len(epi_rows))
            if need <= MM_VMEM_BUDGET:
                break
    nk = K // tk
    assert M % tm == 0 and N % tn == 0 and K % tk == 0, (M, N, K, tm, tn, tk)
    n_rows, n_pars = len(epi_rows), len(epi_pars)
    if epi is None:
        out_dtypes = (out_dtype,)
    else:
        out_dtypes = tuple(epi_out_dtypes)
    n_out = len(out_dtypes)
    dn = (((0 if ta else 1,), (1 if tb else 0,)), ((), ()))

    def body(*refs):
        ab = refs[:2 * npair]
        rows = refs[2 * npair:2 * npair + n_rows]
        pars = refs[2 * npair + n_rows:2 * npair + n_rows + n_pars]
        outs = refs[2 * npair + n_rows + n_pars:2 * npair + n_rows + n_pars + n_out]
        acc_ref = refs[-1]
        k = pl.program_id(2)

        @pl.when(k == 0)
        def _():
            acc_ref[...] = jnp.zeros_like(acc_ref)

        acc = acc_ref[...]
        for p in range(npair):
            a = ab[2 * p][...].astype(BF16)
            b = ab[2 * p + 1][...].astype(BF16)
            acc = acc + lax.dot_general(a, b, dn, preferred_element_type=F32)
        acc_ref[...] = acc

        @pl.when(k == nk - 1)
        def _():
            r = acc_ref[...]
            if epi is None:
                outs[0][...] = r.astype(outs[0].dtype)
            else:
                res = epi(r, *[x[...] for x in rows], *[x[...] for x in pars])
                for o, v in zip(outs, res):
                    o[...] = v.astype(o.dtype)

    a_spec = pl.BlockSpec((tk, tm), lambda i, j, k: (k, i)) if ta else pl.BlockSpec((tm, tk), lambda i, j, k: (i, k))
    b_spec = pl.BlockSpec((tn, tk), lambda i, j, k: (j, k)) if tb else pl.BlockSpec((tk, tn), lambda i, j, k: (k, j))
    in_specs = []
    args = []
    for a, b in pairs:
        in_specs += [a_spec, b_spec]
        args += [a, b]
    for r in epi_rows:
        in_specs.append(pl.BlockSpec((tm, tn), lambda i, j, k: (i, j)))
        args.append(r)
    for p_ in epi_pars:
        in_specs.append(pl.BlockSpec((1, tn), lambda i, j, k: (0, j)))
        args.append(p_)
    out_specs = [pl.BlockSpec((tm, tn), lambda i, j, k: (i, j)) for _ in range(n_out)]
    out_shape = [jax.ShapeDtypeStruct((M, N), d) for d in out_dtypes]
    res = pl.pallas_call(
        body, grid=(M // tm, N // tn, nk), in_specs=in_specs, out_specs=out_specs, out_shape=out_shape,
        scratch_shapes=[pltpu.VMEM((tm, tn), F32)], compiler_params=_cp(("parallel", "parallel", "arbitrary")), name=name,
    )(*args)
    return res[0] if epi is None else tuple(res)


def rowmap(fn, rows, pars, out_dtypes, *, name, tile=512, n_reduce=0):
    views = []
    for r in rows:
        if isinstance(r, tuple):
            views.append(r)
        else:
            views.append((r, r.shape[1], 0))
    S = views[0][0].shape[0]
    tile = min(tile, S)
    assert S % tile == 0
    nt = S // tile
    row_structs = [jax.ShapeDtypeStruct((tile, w), a.dtype) for a, w, _ in views]
    par_structs = [jax.ShapeDtypeStruct(p.shape, p.dtype) for p in pars]
    out_structs = jax.eval_shape(fn, *row_structs, *par_structs)
    n_out = len(out_structs)
    n_row_out = n_out - n_reduce
    nr, npar = len(views), len(pars)

    def body(*refs):
        ins = [x[...] for x in refs[:nr + npar]]
        outs = refs[nr + npar:]
        res = fn(*ins)
        for o, v in zip(outs[:n_row_out], res[:n_row_out]):
            o[...] = v.astype(o.dtype)
        if n_reduce:
            i = pl.program_id(0)

            @pl.when(i == 0)
            def _():
                for o, v in zip(outs[n_row_out:], res[n_row_out:]):
                    o[...] = v.astype(o.dtype)

            @pl.when(i > 0)
            def _():
                for o, v in zip(outs[n_row_out:], res[n_row_out:]):
                    o[...] += v.astype(o.dtype)

    in_specs = [pl.BlockSpec((tile, w), functools.partial(lambda i, c: (i, c), c=c)) for _, w, c in views]
    in_specs += [pl.BlockSpec(p.shape, lambda i: (0, 0)) for p in pars]
    out_specs = [pl.BlockSpec((tile, s.shape[1]), lambda i: (i, 0)) for s in out_structs[:n_row_out]]
    out_specs += [pl.BlockSpec(s.shape, lambda i: (0, 0)) for s in out_structs[n_row_out:]]
    out_shape = [jax.ShapeDtypeStruct((S, s.shape[1]), d) for s, d in zip(out_structs[:n_row_out], out_dtypes[:n_row_out])]
    out_shape += [jax.ShapeDtypeStruct(s.shape, F32) for s in out_structs[n_row_out:]]
    res = pl.pallas_call(
        body, grid=(nt,), in_specs=in_specs, out_specs=out_specs, out_shape=out_shape,
        compiler_params=_cp(("arbitrary",) if n_reduce else ("parallel",)), name=name,
    )(*[v[0] for v in views], *pars)
    return tuple(res)


CH = 64


def _softplus(x):
    return jnp.where(x > 20.0, x, jnp.log(1.0 + jnp.exp(jnp.minimum(x, 20.0))))


def _dot(a, b, dn=(((1,), (0,)), ((), ())), hi=False):
    if hi:
        return lax.dot_general(a.astype(F32), b.astype(F32), dn, precision=HI if hi is True else hi, preferred_element_type=F32)
    return lax.dot_general(a.astype(BF16), b.astype(BF16), dn, preferred_element_type=F32)


_NT = (((1,), (1,)), ((), ()))
_TN = (((0,), (0,)), ((), ()))


def _chunk_consts():
    r = lax.broadcasted_iota(jnp.int32, (CH, 2 * CH), 0)
    c0 = lax.broadcasted_iota(jnp.int32, (CH, 2 * CH), 1)
    c = jnp.where(c0 >= CH, c0 - CH, c0)
    r1 = lax.broadcasted_iota(jnp.int32, (CH, CH), 0)
    c1 = lax.broadcasted_iota(jnp.int32, (CH, CH), 1)
    return dict(
        lower2=r >= c, strict2=r > c, U2=(r <= c).astype(F32), eye2=(r == c).astype(F32),
        L=(r1 >= c1).astype(F32), ones=jnp.ones((CH, CH), F32), Z=jnp.zeros((CH, 2 * CH), F32))


@jax.custom_vjp
def _tri_inv2(a2s, eye2, Z):
    def prod(x2, y):
        return _dot(x2, jnp.concatenate([y, Z], axis=0), hi=HI3)

    bs = [-a2 for a2 in a2s]
    ts = [eye2 + b for b in bs]
    for _ in range(5):
        bs = [prod(b, b) for b in bs]
        ts = [t + prod(t, b) for t, b in zip(ts, bs)]
    return tuple(ts)


def _tri_inv2_fwd(a2s, eye2, Z):
    ts = _tri_inv2(a2s, eye2, Z)
    return ts, (ts, eye2, Z)


def _tri_inv2_bwd(res, dts):
    ts, eye2, Z = res
    xs = [_dot(t2, dt2, _TN, hi=HI3)[:CH] for t2, dt2 in zip(ts, dts)]
    das = tuple(-_dot(x2, jnp.concatenate([t2, Z], axis=0), _NT, hi=HI3) for x2, t2 in zip(xs, ts))
    return das, jnp.zeros_like(eye2), jnp.zeros_like(Z)


_tri_inv2.defvjp(_tri_inv2_fwd, _tri_inv2_bwd)


def _gdn_heads(qs, ks, vs, bxs, axs, Ss, alogs, dtbs, cst):
    lower2, strict2, U2, eye2, L, ones, Z = (cst[n] for n in ("lower2", "strict2", "U2", "eye2", "L", "ones", "Z"))
    H = range(len(qs))

    def prod(x2, y):
        return _dot(x2, jnp.concatenate([y, Z], axis=0), hi=HI3)

    qn = [qs[h] * lax.rsqrt(jnp.sum(qs[h] * qs[h], axis=-1, keepdims=True) + NORM_EPS) * (128.0 ** -0.5) for h in H]
    kn = [ks[h] * lax.rsqrt(jnp.sum(ks[h] * ks[h], axis=-1, keepdims=True) + NORM_EPS) for h in H]
    beta = [jax.nn.sigmoid(bxs[h]) for h in H]
    g = [-jnp.exp(alogs[h]) * _softplus(axs[h] + dtbs[h]) for h in H]
    gc = [_dot(L, g[h], hi=HI3) for h in H]
    n2 = [_dot(ones, g[h] * U2, hi=HI3) for h in H]
    decay2 = [jnp.where(lower2, jnp.exp(jnp.where(lower2, gc[h] - n2[h], 0.0)), 0.0) for h in H]
    kb = [kn[h] * beta[h] for h in H]
    kn2 = [jnp.concatenate([kn[h], kn[h]], axis=0) for h in H]
    a2 = tuple(jnp.where(strict2, _dot(kb[h], kn2[h], _NT) * decay2[h], 0.0) for h in H)
    t2 = _tri_inv2(a2, eye2, Z)
    glast = [jnp.sum(g[h], axis=0, keepdims=True) for h in H]
    u = [prod(t2[h], vs[h] * beta[h]) for h in H]
    w = [prod(t2[h], kb[h] * jnp.exp(gc[h])) for h in H]
    attn2 = [jnp.where(lower2, _dot(qn[h], kn2[h], _NT) * decay2[h], 0.0) for h in H]
    k_end = [kn[h] * jnp.exp(glast[h] - gc[h]) for h in H]
    q_start = [qn[h] * jnp.exp(gc[h]) for h in H]
    v_new = [u[h] - _dot(w[h], Ss[h]) for h in H]
    o = [_dot(q_start[h], Ss[h]) + _dot(attn2[h], jnp.concatenate([v_new[h], Z], axis=0)) for h in H]
    s_new = [Ss[h] * jnp.exp(glast[h]) + _dot(k_end[h], v_new[h], _TN) for h in H]
    return tuple(o), tuple(s_new)


def gdn_fwd(qkv, proj, bcol, acol, alog_e, dtb_e, *, name):
    S_ = qkv.shape[0]
    nc = S_ // CH

    def body(q_ref, k_ref, v_ref, b_ref, a_ref, al_ref, dt_ref, o_ref, hist_ref, s_ref):
        i = pl.program_id(0)

        @pl.when(i == 0)
        def _():
            s_ref[...] = jnp.zeros_like(s_ref)

        cst = _chunk_consts()
        hist_ref[0] = s_ref[...]
        heads = [slice(128 * h, 128 * (h + 1)) for h in range(4)]
        rd = lambda ref: tuple(ref[:, ls] for ls in heads)
        os_, s_news = _gdn_heads(rd(q_ref), rd(k_ref), rd(v_ref), rd(b_ref), rd(a_ref), tuple(s_ref[ls, :] for ls in heads),
                                 rd(al_ref), rd(dt_ref), cst)
        for ls, o, s_new in zip(heads, os_, s_news):
            o_ref[:, ls] = o
            s_ref[ls, :] = s_new

    blk = lambda cb: pl.BlockSpec((CH, 512), functools.partial(lambda i, cb: (i, cb), cb=cb))
    par = pl.BlockSpec((1, 512), lambda i: (0, 0))
    return pl.pallas_call(
        body, grid=(nc,), in_specs=[blk(0), blk(1), blk(2), blk(bcol), blk(acol), par, par],
        out_specs=[pl.BlockSpec((CH, 512), lambda i: (i, 0)), pl.BlockSpec((1, 512, 128), lambda i: (i, 0, 0))],
        out_shape=[jax.ShapeDtypeStruct((S_, 512), F32), jax.ShapeDtypeStruct((nc, 512, 128), F32)],
        scratch_shapes=[pltpu.VMEM((512, 128), F32)], compiler_params=_cp(("arbitrary",)), name=name,
    )(qkv, qkv, qkv, proj, proj, alog_e, dtb_e)


def gdn_bwd(qkv, proj, bcol, acol, alog_e, dtb_e, hist, do, *, name):
    S_ = qkv.shape[0]
    nc = S_ // CH

    def body(q_ref, k_ref, v_ref, b_ref, a_ref, al_ref, dt_ref, hist_ref, do_ref, dqkv_ref, db_ref, da_ref, dal_ref, ddt_ref, ds_ref):
        i = pl.program_id(0)

        @pl.when(i == 0)
        def _():
            ds_ref[...] = jnp.zeros_like(ds_ref)
            dal_ref[...] = jnp.zeros_like(dal_ref)
            ddt_ref[...] = jnp.zeros_like(ddt_ref)

        cst = _chunk_consts()
        heads = [slice(128 * h, 128 * (h + 1)) for h in range(4)]
        rd = lambda ref: tuple(ref[:, ls] for ls in heads)
        fn = functools.partial(_gdn_heads, cst=cst)
        _, vjp = jax.vjp(fn, rd(q_ref), rd(k_ref), rd(v_ref), rd(b_ref), rd(a_ref), tuple(hist_ref[0, ls, :] for ls in heads),
                         rd(al_ref), rd(dt_ref))
        grads = vjp((rd(do_ref), tuple(ds_ref[ls, :] for ls in heads)))
        for h in range(4):
            ls = heads[h]
            dq, dk, dv, db, da, ds_in, dal, ddt = (t[h] for t in grads)
            dqkv_ref[:, 128 * h:128 * (h + 1)] = dq
            dqkv_ref[:, 512 + 128 * h:512 + 128 * (h + 1)] = dk
            dqkv_ref[:, 1024 + 128 * h:1024 + 128 * (h + 1)] = dv
            db_ref[:, ls] = db.astype(db_ref.dtype)
            da_ref[:, ls] = da.astype(da_ref.dtype)
            ds_ref[ls, :] = ds_in
            dal_ref[:, ls] += dal
            ddt_ref[:, ls] += ddt

    rblk = lambda cb: pl.BlockSpec((CH, 512), functools.partial(lambda i, cb: (nc - 1 - i, cb), cb=cb))
    par = pl.BlockSpec((1, 512), lambda i: (0, 0))
    return pl.pallas_call(
        body, grid=(nc,),
        in_specs=[rblk(0), rblk(1), rblk(2), rblk(bcol), rblk(acol), par, par,
                  pl.BlockSpec((1, 512, 128), lambda i: (nc - 1 - i, 0, 0)), rblk(0)],
        out_specs=[pl.BlockSpec((CH, 1536), lambda i: (nc - 1 - i, 0)), rblk(0), rblk(0), par, par],
        out_shape=[jax.ShapeDtypeStruct((S_, 1536), F32), jax.ShapeDtypeStruct((S_, 512), BF16), jax.ShapeDtypeStruct((S_, 512), BF16),
                   jax.ShapeDtypeStruct((1, 512), F32), jax.ShapeDtypeStruct((1, 512), F32)],
        scratch_shapes=[pltpu.VMEM((512, 128), F32)], compiler_params=_cp(("arbitrary",)), name=name,
    )(qkv, qkv, qkv, proj, proj, alog_e, dtb_e, hist, do)


def _ssd_pairs(xs, dtxs, bms, cms, hss, nas, dtbs, dsks, cst):
    lower2, U2, L, ones = cst["lower2"], cst["U2"], cst["L"], cst["ones"]
    lane = lax.broadcasted_iota(jnp.int32, (1, 2 * CH), 1)
    mask_l = (lane < CH).astype(F32)
    mask_r = 1.0 - mask_l
    ones_w = jnp.ones((CH, 2 * CH), F32)
    P_ = range(len(xs))
    G_ = range(len(bms))
    per = len(xs) // len(bms)
    cb2 = [_dot(cms[g], jnp.concatenate([bms[g], bms[g]], axis=0), _NT) for g in G_]
    dt = [_softplus(dtxs[p] + dtbs[p]) for p in P_]
    da = [dt[p] * nas[p] for p in P_]
    m = [_dot(L, da[p], hi=HI3) for p in P_]
    n2 = [_dot(ones, da[p] * U2, hi=HI3) for p in P_]
    lm2 = [jnp.where(lower2, jnp.exp(jnp.where(lower2, m[p] - n2[p], 0.0)), 0.0) for p in P_]
    xdt = [xs[p] * dt[p] for p in P_]
    x2 = [jnp.concatenate([xdt[p] * mask_l, xdt[p] * mask_r], axis=0) for p in P_]
    y_diag = [_dot(cb2[p // per] * lm2[p], x2[p]) for p in P_]
    alast = [jnp.sum(da[p], axis=0, keepdims=True) for p in P_]
    y_off = [_dot(cms[p // per], hss[p], _NT) * jnp.exp(m[p]) for p in P_]
    cd = [jnp.exp(_dot(da[p], ones_w, _TN, hi=HI3)) for p in P_]
    hs_new = [hss[p] * cd[p] + _dot(xdt[p] * jnp.exp(alast[p] - m[p]), bms[p // per], _TN) for p in P_]
    ys = [y_diag[p] + y_off[p] + dsks[p] * xs[p] for p in P_]
    return tuple(ys), tuple(hs_new)


def _ssd_specs(nc, rev):
    ci = (lambda i: nc - 1 - i) if rev else (lambda i: i)
    col = lambda w, c: pl.BlockSpec((CH, w), functools.partial(lambda i, c: (ci(i), c), c=c))
    xg = [col(512, g) for g in range(4)]
    dtg = [col(512, 10 + g) for g in range(4)]
    par = pl.BlockSpec((1, 2048), lambda i: (0, 0))
    hist = pl.BlockSpec((1, 2048, 128), lambda i: (ci(i), 0, 0))
    return xg, dtg, col(512, 4), col(512, 5), par, hist, col


def _ssd_read(x_refs, dt_refs, b_ref, c_ref, na_ref, dtb_ref, dsk_ref):
    sl = [slice(128 * p, 128 * (p + 1)) for p in range(4)]
    xs = tuple(x_refs[g][:, s] for g in range(4) for s in sl)
    dts = tuple(dt_refs[g][:, s] for g in range(4) for s in sl)
    bms = tuple(b_ref[:, s] for s in sl)
    cms = tuple(c_ref[:, s] for s in sl)
    lanes = [slice(128 * p, 128 * (p + 1)) for p in range(16)]
    pars = [tuple(r[:, s] for s in lanes) for r in (na_ref, dtb_ref, dsk_ref)]
    return xs, dts, bms, cms, pars, lanes


def ssd_fwd(xbc, proj, na_e, dtb_e, dsk_e, *, name):
    S_ = xbc.shape[0]
    nc = S_ // CH
    xg, dtg, bs, cs, par, hist, _ = _ssd_specs(nc, False)

    def body(*refs):
        x_refs, dt_refs = refs[0:4], refs[4:8]
        b_ref, c_ref, na_ref, dtb_ref, dsk_ref, y_ref, hist_ref, s_ref = refs[8:]
        i = pl.program_id(0)

        @pl.when(i == 0)
        def _():
            s_ref[...] = jnp.zeros_like(s_ref)

        cst = _chunk_consts()
        hist_ref[0] = s_ref[...]
        xs, dts, bms, cms, pars, lanes = _ssd_read(x_refs, dt_refs, b_ref, c_ref, na_ref, dtb_ref, dsk_ref)
        ys, hs_new = _ssd_pairs(xs, dts, bms, cms, tuple(s_ref[s, :] for s in lanes), *pars, cst)
        for p, s in enumerate(lanes):
            y_ref[:, s] = ys[p]
            s_ref[s, :] = hs_new[p]

    return pl.pallas_call(
        body, grid=(nc,), in_specs=xg + dtg + [bs, cs, par, par, par],
        out_specs=[pl.BlockSpec((CH, 2048), lambda i: (i, 0)), hist],
        out_shape=[jax.ShapeDtypeStruct((S_, 2048), F32), jax.ShapeDtypeStruct((nc, 2048, 128), F32)],
        scratch_shapes=[pltpu.VMEM((2048, 128), F32)], compiler_params=_cp(("arbitrary",)), name=name,
    )(xbc, xbc, xbc, xbc, proj, proj, proj, proj, xbc, xbc, na_e, dtb_e, dsk_e)


def ssd_bwd(xbc, proj, na_e, dtb_e, dsk_e, hist, dy, *, name):
    S_ = xbc.shape[0]
    nc = S_ // CH
    xg, dtg, bs, cs, par, hist_spec, col = _ssd_specs(nc, True)
    wide = pl.BlockSpec((CH, 2048), lambda i: (nc - 1 - i, 0))

    def body(*refs):
        x_refs, dt_refs = refs[0:4], refs[4:8]
        (b_ref, c_ref, na_ref, dtb_ref, dsk_ref, hist_ref, dy_ref,
         dx_ref, ddt_ref, db_ref, dc_ref, dna_ref, ddtb_ref, ddsk_ref, ds_ref) = refs[8:]
        i = pl.program_id(0)

        @pl.when(i == 0)
        def _():
            ds_ref[...] = jnp.zeros_like(ds_ref)
            dna_ref[...] = jnp.zeros_like(dna_ref)
            ddtb_ref[...] = jnp.zeros_like(ddtb_ref)
            ddsk_ref[...] = jnp.zeros_like(ddsk_ref)

        cst = _chunk_consts()
        xs, dts, bms, cms, pars, lanes = _ssd_read(x_refs, dt_refs, b_ref, c_ref, na_ref, dtb_ref, dsk_ref)
        fn = functools.partial(_ssd_pairs, cst=cst)
        _, vjp = jax.vjp(fn, xs, dts, bms, cms, tuple(hist_ref[0, s, :] for s in lanes), *pars)
        dxs, ddts, dbs, dcs, dhs, dnas, ddtbs, ddsks = vjp((tuple(dy_ref[:, s] for s in lanes), tuple(ds_ref[s, :] for s in lanes)))
        for g in range(4):
            db_ref[:, 128 * g:128 * (g + 1)] = dbs[g]
            dc_ref[:, 128 * g:128 * (g + 1)] = dcs[g]
        for p, s in enumerate(lanes):
            dx_ref[:, s] = dxs[p]
            ddt_ref[:, s] = ddts[p].astype(ddt_ref.dtype)
            ds_ref[s, :] = dhs[p]
            dna_ref[:, s] += dnas[p]
            ddtb_ref[:, s] += ddtbs[p]
            ddsk_ref[:, s] += ddsks[p]

    half = pl.BlockSpec((CH, 512), lambda i: (nc - 1 - i, 0))
    return pl.pallas_call(
        body, grid=(nc,), in_specs=xg + dtg + [bs, cs, par, par, par, hist_spec, wide],
        out_specs=[wide, wide, half, half, par, par, par],
        out_shape=[jax.ShapeDtypeStruct((S_, 2048), F32), jax.ShapeDtypeStruct((S_, 2048), BF16),
                   jax.ShapeDtypeStruct((S_, 512), F32), jax.ShapeDtypeStruct((S_, 512), F32)] +
                  [jax.ShapeDtypeStruct((1, 2048), F32)] * 3,
        scratch_shapes=[pltpu.VMEM((2048, 128), F32)], compiler_params=_cp(("arbitrary",)), name=name,
    )(xbc, xbc, xbc, xbc, proj, proj, proj, proj, xbc, xbc, na_e, dtb_e, dsk_e, hist, dy)


ATT_T = 1024
ATT_SCALE = 192.0 ** -0.5
NEG = -1e30


def _chunk_mask(shape):
    return lax.broadcasted_iota(jnp.int32, shape, 1) // CH <= lax.broadcasted_iota(jnp.int32, shape, 0) // CH


def _tri_pairs(n, by_row):
    pairs = [(i, j) for i in range(n) for j in range(i + 1)] if by_row else [(i, j) for j in range(n) for i in range(j, n)]
    return jnp.asarray([p[0] for p in pairs], jnp.int32), jnp.asarray([p[1] for p in pairs], jnp.int32)


def att_fwd(q, kv, kp, *, name):
    S_ = q.shape[0]
    T = min(ATT_T, S_)
    n = S_ // T
    ii, jj = _tri_pairs(n, True)
    HP = 2

    def body(ii_ref, jj_ref, q_ref, kn_ref, kp_ref, v_ref, o_ref, lse_ref, m_ref, l_ref, acc_ref):
        t = pl.program_id(1)
        i, j = ii_ref[t], jj_ref[t]

        @pl.when(j == 0)
        def _():
            m_ref[...] = jnp.full_like(m_ref, NEG)
            l_ref[...] = jnp.zeros_like(l_ref)
            acc_ref[...] = jnp.zeros_like(acc_ref)

        H = range(HP)
        hs = [slice(128 * h, 128 * (h + 1)) for h in H]

        def step(diag):
            kp_ = kp_ref[...]
            k2 = [jnp.concatenate([kn_ref[:, hs[h]], kp_], axis=1) for h in H]
            s = [_dot(q_ref[:, 256 * h:256 * (h + 1)], k2[h], _NT) for h in H]
            if diag:
                mask = _chunk_mask(s[0].shape)
                s = [jnp.where(mask, s[h], NEG) for h in H]
            m_prev = [m_ref[:, hs[h]] for h in H]
            m_cur = [jnp.maximum(m_prev[h], jnp.max(s[h], axis=-1, keepdims=True)) for h in H]
            p = [jnp.exp(s[h] - m_cur[h][:, :1]) for h in H]
            alpha = [jnp.exp(m_prev[h] - m_cur[h]) for h in H]
            for h in H:
                l_ref[:, hs[h]] = alpha[h] * l_ref[:, hs[h]] + jnp.sum(p[h], axis=-1, keepdims=True)
            pv = [_dot(p[h], v_ref[:, hs[h]]) for h in H]
            for h in H:
                acc_ref[:, hs[h]] = acc_ref[:, hs[h]] * alpha[h] + pv[h]
                m_ref[:, hs[h]] = m_cur[h]

        @pl.when(j < i)
        def _():
            step(False)

        @pl.when(j == i)
        def _():
            step(True)
            o_ref[...] = acc_ref[...] / l_ref[...]
            lse_ref[...] = m_ref[...] + jnp.log(l_ref[...])

    W = 128 * HP
    grid_spec = pltpu.PrefetchScalarGridSpec(
        num_scalar_prefetch=2, grid=(4 // HP, ii.shape[0]),
        in_specs=[pl.BlockSpec((T, 2 * W), lambda h, t, ii_, jj_: (ii_[t], h)), pl.BlockSpec((T, W), lambda h, t, ii_, jj_: (jj_[t], h)),
                  pl.BlockSpec((T, 128), lambda h, t, ii_, jj_: (jj_[t], 0)),
                  pl.BlockSpec((T, W), lambda h, t, ii_, jj_: (jj_[t], 4 // HP + h))],
        out_specs=[pl.BlockSpec((T, W), lambda h, t, ii_, jj_: (ii_[t], h))] * 2,
        scratch_shapes=[pltpu.VMEM((T, W), F32)] * 3)
    return pl.pallas_call(
        body, grid_spec=grid_spec, out_shape=[jax.ShapeDtypeStruct((S_, 512), F32), jax.ShapeDtypeStruct((S_, 512), F32)],
        compiler_params=_cp(("parallel", "arbitrary")), name=name,
    )(ii, jj, q, kv, kp, kv)


def att_bwd(q, kv, kp, lse, dsum, do, *, name):
    S_ = q.shape[0]
    T = min(ATT_T, S_)
    n = S_ // T
    ii, jj = _tri_pairs(n, False)
    last = ii.shape[0] - 1

    def body(ii_ref, jj_ref, q_ref, kn_ref, kp_ref, v_ref, lse_ref, d_ref, do_ref, dq_hbm, dk_ref, dv_ref, dq_acc, sem):
        h, t = pl.program_id(0), pl.program_id(1)
        i, j = ii_ref[t], jj_ref[t]

        @pl.when(t == 0)
        def _():
            dq_acc[...] = jnp.zeros_like(dq_acc)

        def step(diag):
            k2 = jnp.concatenate([kn_ref[...], kp_ref[...]], axis=1)
            qb = q_ref[...]
            dob = do_ref[...].astype(BF16)
            s = _dot(qb, k2, _NT)
            p = jnp.exp(s - lse_ref[:, :1])
            if diag:
                p = jnp.where(_chunk_mask(s.shape), p, 0.0)
            if diag:
                dv_ref[...] = _dot(p, dob, _TN)
            else:
                dv_ref[...] += _dot(p, dob, _TN)
            ds = (p * (_dot(dob, v_ref[...], _NT) - d_ref[:, :1])).astype(BF16)
            if diag:
                dk_ref[...] = _dot(ds, qb, _TN)
            else:
                dk_ref[...] += _dot(ds, qb, _TN)
            rows = pl.ds(pl.multiple_of(i * T, T), T)
            dq_acc[rows, :] += _dot(ds, k2)

        @pl.when(i > j)
        def _():
            step(False)

        @pl.when(i == j)
        def _():
            step(True)

        @pl.when(t == last)
        def _():
            cp = pltpu.make_async_copy(dq_acc, dq_hbm.at[h], sem)
            cp.start()
            cp.wait()

    qmap = lambda h, t, ii_, jj_: (ii_[t], h)
    grid_spec = pltpu.PrefetchScalarGridSpec(
        num_scalar_prefetch=2, grid=(4, ii.shape[0]),
        in_specs=[pl.BlockSpec((T, 256), qmap), pl.BlockSpec((T, 128), lambda h, t, ii_, jj_: (jj_[t], h)),
                  pl.BlockSpec((T, 128), lambda h, t, ii_, jj_: (jj_[t], 0)), pl.BlockSpec((T, 128), lambda h, t, ii_, jj_: (jj_[t], 4 + h)),
                  pl.BlockSpec((T, 128), qmap), pl.BlockSpec((T, 128), qmap), pl.BlockSpec((T, 128), qmap)],
        out_specs=[pl.BlockSpec(memory_space=pl.ANY), pl.BlockSpec((T, 256), lambda h, t, ii_, jj_: (jj_[t], h)),
                   pl.BlockSpec((T, 128), lambda h, t, ii_, jj_: (jj_[t], h))],
        scratch_shapes=[pltpu.VMEM((S_, 256), F32), pltpu.SemaphoreType.DMA])
    return pl.pallas_call(
        body, grid_spec=grid_spec,
        out_shape=[jax.ShapeDtypeStruct((4, S_, 256), F32), jax.ShapeDtypeStruct((S_, 1024), F32), jax.ShapeDtypeStruct((S_, 512), F32)],
        compiler_params=_cp(("arbitrary", "arbitrary")), name=name,
    )(ii, jj, q, kv, kp, kv, lse, dsum, do)


CONV_T = 1024


def _shift_down(x, halo, s):
    sh = pltpu.roll(x, s, axis=0)
    hr = pltpu.roll(halo, s, axis=0)
    r8 = lax.broadcasted_iota(jnp.int32, hr.shape, 0)
    top = jnp.where(r8 < s, hr, sh[:8])
    return jnp.concatenate([top, sh[8:]], axis=0)


def _shift_up(x, halo, s):
    n = x.shape[0]
    sh = pltpu.roll(x, n - s, axis=0)
    hr = pltpu.roll(halo, 8 - s, axis=0)
    r8 = lax.broadcasted_iota(jnp.int32, hr.shape, 0)
    bot = jnp.where(r8 >= 8 - s, hr, sh[n - 8:])
    return jnp.concatenate([sh[:n - 8], bot], axis=0)


def _conv_pre(x, halo, w, b):
    y = x * w[3:4] + b
    for j in range(3):
        y = y + _shift_down(x, halo, 3 - j) * w[j:j + 1]
    return y


def conv_fwd(src, cb0, ncb, w, b, *, name):
    S_ = src.shape[0]
    T = min(CONV_T, S_)
    nt = S_ // T

    def body(x_ref, h_ref, w_ref, b_ref, o_ref):
        i = pl.program_id(1)
        halo = jnp.where(i > 0, h_ref[...], 0.0)
        y = _conv_pre(x_ref[...], halo, w_ref[...], b_ref[...])
        o_ref[...] = y * jax.nn.sigmoid(y)

    return pl.pallas_call(
        body, grid=(ncb, nt),
        in_specs=[pl.BlockSpec((T, 512), lambda c, i: (i, cb0 + c)),
                  pl.BlockSpec((8, 512), lambda c, i: (jnp.maximum(i * (T // 8) - 1, 0), cb0 + c)),
                  pl.BlockSpec((4, 512), lambda c, i: (0, c)), pl.BlockSpec((1, 512), lambda c, i: (0, c))],
        out_specs=pl.BlockSpec((T, 512), lambda c, i: (i, c)),
        out_shape=jax.ShapeDtypeStruct((S_, 512 * ncb), F32), compiler_params=_cp(("parallel", "parallel")), name=name,
    )(src, src, w, b)


def conv_bwd_pre(src, cb0, ncb, w, b, dy, *, name):
    S_ = src.shape[0]
    T = min(CONV_T, S_)
    nt = S_ // T

    def body(x_ref, h_ref, w_ref, b_ref, dy_ref, dp_ref, dw_ref, db_ref):
        i = pl.program_id(1)
        halo = jnp.where(i > 0, h_ref[...], 0.0)
        x = x_ref[...]
        y = _conv_pre(x, halo, w_ref[...], b_ref[...])
        sg = jax.nn.sigmoid(y)
        dpre = dy_ref[...] * (sg * (1.0 + y * (1.0 - sg)))
        dp_ref[...] = dpre
        rows = [jnp.sum(dpre * _shift_down(x, halo, 3 - j), axis=0, keepdims=True) for j in range(3)]
        rows.append(jnp.sum(dpre * x, axis=0, keepdims=True))
        dw = jnp.concatenate(rows, axis=0)
        db = jnp.sum(dpre, axis=0, keepdims=True)

        @pl.when(i == 0)
        def _():
            dw_ref[...] = dw
            db_ref[...] = db

        @pl.when(i > 0)
        def _():
            dw_ref[...] += dw
            db_ref[...] += db

    return pl.pallas_call(
        body, grid=(ncb, nt),
        in_specs=[pl.BlockSpec((T, 512), lambda c, i: (i, cb0 + c)),
                  pl.BlockSpec((8, 512), lambda c, i: (jnp.maximum(i * (T // 8) - 1, 0), cb0 + c)),
                  pl.BlockSpec((4, 512), lambda c, i: (0, c)), pl.BlockSpec((1, 512), lambda c, i: (0, c)),
                  pl.BlockSpec((T, 512), lambda c, i: (i, c))],
        out_specs=[pl.BlockSpec((T, 512), lambda c, i: (i, c)), pl.BlockSpec((4, 512), lambda c, i: (0, c)),
                   pl.BlockSpec((1, 512), lambda c, i: (0, c))],
        out_shape=[jax.ShapeDtypeStruct((S_, 512 * ncb), F32), jax.ShapeDtypeStruct((4, 512 * ncb), F32),
                   jax.ShapeDtypeStruct((1, 512 * ncb), F32)],
        compiler_params=_cp(("parallel", "arbitrary")), name=name,
    )(src, src, w, b, dy)


def conv_bwd_x(dpre, w, *, name, out_dtype=F32):
    S_, C = dpre.shape
    T = min(CONV_T, S_)
    nt = S_ // T
    ncb = C // 512

    def body(d_ref, h_ref, w_ref, o_ref):
        i = pl.program_id(1)
        halo = jnp.where(i < nt - 1, h_ref[...], 0.0)
        d = d_ref[...]
        w_ = w_ref[...]
        y = d * w_[3:4]
        for j in range(3):
            y = y + _shift_up(d, halo, 3 - j) * w_[j:j + 1]
        o_ref[...] = y.astype(o_ref.dtype)

    return pl.pallas_call(
        body, grid=(ncb, nt),
        in_specs=[pl.BlockSpec((T, 512), lambda c, i: (i, c)),
                  pl.BlockSpec((8, 512), lambda c, i: (jnp.minimum((i + 1) * (T // 8), S_ // 8 - 1), c)),
                  pl.BlockSpec((4, 512), lambda c, i: (0, c))],
        out_specs=pl.BlockSpec((T, 512), lambda c, i: (i, c)),
        out_shape=jax.ShapeDtypeStruct((S_, C), out_dtype), compiler_params=_cp(("parallel", "parallel")), name=name,
    )(dpre, dpre, w)


def ffn_mid_fwd(h, w1, w3, *, name):
    S_, D = h.shape
    F = w1.shape[1]
    tm, tn = _pick(S_, (2048, 1024, 512, 256)), 256

    def body(h_ref, w1_ref, w3_ref, a_ref, u_ref, v_ref):
        hb = h_ref[...]
        u = _dot(hb, w1_ref[...])
        v = _dot(hb, w3_ref[...])
        a_ref[...] = (u * jax.nn.sigmoid(u) * v).astype(a_ref.dtype)
        u_ref[...] = u.astype(u_ref.dtype)
        v_ref[...] = v.astype(v_ref.dtype)

    o = pl.BlockSpec((tm, tn), lambda i, j: (i, j))
    return pl.pallas_call(
        body, grid=(S_ // tm, F // tn),
        in_specs=[pl.BlockSpec((tm, D), lambda i, j: (i, 0)), pl.BlockSpec((D, tn), lambda i, j: (0, j)),
                  pl.BlockSpec((D, tn), lambda i, j: (0, j))],
        out_specs=[o, o, o], out_shape=[jax.ShapeDtypeStruct((S_, F), BF16)] * 3,
        compiler_params=_cp(("parallel", "parallel")), name=name,
    )(h, w1, w3)


def ffn_mid_bwd(u, v, dy, w2, *, name):
    S_, F = u.shape
    D = dy.shape[1]
    tm, tn = _pick(S_, (2048, 1024, 512, 256)), 256

    def body(u_ref, v_ref, dy_ref, w2_ref, du_ref, dv_ref):
        u_ = u_ref[...].astype(F32)
        v_ = v_ref[...].astype(F32)
        da = _dot(dy_ref[...], w2_ref[...], _NT)
        sg = jax.nn.sigmoid(u_)
        dv_ref[...] = (da * (u_ * sg)).astype(dv_ref.dtype)
        du_ref[...] = (da * v_ * (sg * (1.0 + u_ * (1.0 - sg)))).astype(du_ref.dtype)

    o = pl.BlockSpec((tm, tn), lambda i, j: (i, j))
    return pl.pallas_call(
        body, grid=(S_ // tm, F // tn),
        in_specs=[o, o, pl.BlockSpec((tm, D), lambda i, j: (i, 0)), pl.BlockSpec((tn, D), lambda i, j: (j, 0))],
        out_specs=[o, o], out_shape=[jax.ShapeDtypeStruct((S_, F), BF16)] * 2,
        compiler_params=_cp(("parallel", "parallel")), name=name,
    )(u, v, dy, w2)


MESH = pl.DeviceIdType.MESH
ANY = pl.BlockSpec(memory_space=pl.ANY)


def allgather8(x_shard, *, name):
    m_per, n = x_shard.shape

    def body(x_ref, out_ref, send_sems, recv_sems, local_sem):
        x, y, c = lax.axis_index("x"), lax.axis_index("y"), lax.axis_index("c")
        me, sibling = (x, y, c), (x, y, 1 - c)
        chips = [(1 - x, y), (x, 1 - y), (1 - x, 1 - y)]

        def rows(px, py, pc):
            return out_ref.at[pl.ds((4 * px + 2 * py + pc) * m_per, m_per), :]

        def copy(k, block, to, src=None):
            return pltpu.make_async_remote_copy(
                src_ref=rows(*block) if src is None else src, dst_ref=rows(*block),
                send_sem=send_sems.at[k], recv_sem=recv_sems.at[k], device_id=to, device_id_type=MESH)

        mine = pltpu.make_async_copy(x_ref, rows(*me), local_sem)
        mine.start()
        first = [copy(0, me, sibling, src=x_ref)]
        first += [copy(1 + j, me, (*chip, c), src=x_ref) for j, chip in enumerate(chips)]
        for cp in first:
            cp.start()
        passed = [copy(4 + j, (*chip, c), sibling) for j, chip in enumerate(chips)]
        for j, chip in enumerate(chips):
            copy(1 + j, (*chip, c), me).wait_recv()
            passed[j].start()
        copy(0, sibling, me).wait_recv()
        for j, chip in enumerate(chips):
            copy(4 + j, (*chip, 1 - c), me).wait_recv()
        for cp in first + passed:
            cp.wait_send()
        mine.wait()

    return pl.pallas_call(
        body, out_shape=jax.ShapeDtypeStruct((8 * m_per, n), x_shard.dtype),
        in_specs=[pl.BlockSpec(memory_space=pltpu.VMEM)], out_specs=pl.BlockSpec(memory_space=pltpu.VMEM),
        scratch_shapes=[pltpu.SemaphoreType.DMA((7,)), pltpu.SemaphoreType.DMA((7,)), pltpu.SemaphoreType.DMA],
        name=name,
    )(x_shard)


def _chip_peers():
    x, y, c = lax.axis_index("x"), lax.axis_index("y"), lax.axis_index("c")
    return x, y, c, [(1 - x, y), (x, 1 - y), (1 - x, 1 - y)]


def allgather_chips(x_shard, *, name):
    r, cdim = x_shard.shape

    def body(x_ref, out_ref, send_sems, recv_sems, local_sem):
        x, y, c, chips = _chip_peers()
        me = 2 * x + y
        mine = pltpu.make_async_copy(x_ref, out_ref.at[me], local_sem)
        mine.start()
        sends = []
        for k, (px, py) in enumerate(chips):
            cp = pltpu.make_async_remote_copy(src_ref=x_ref, dst_ref=out_ref.at[me], send_sem=send_sems.at[k],
                                              recv_sem=recv_sems.at[k], device_id=(px, py, c), device_id_type=MESH)
            cp.start()
            sends.append(cp)
        for k, (px, py) in enumerate(chips):
            pltpu.make_async_remote_copy(src_ref=x_ref, dst_ref=out_ref.at[2 * px + py], send_sem=send_sems.at[k],
                                         recv_sem=recv_sems.at[k], device_id=(px, py, c), device_id_type=MESH).wait_recv()
        for cp in sends:
            cp.wait_send()
        mine.wait()

    return pl.pallas_call(
        body, out_shape=jax.ShapeDtypeStruct((4, r, cdim), x_shard.dtype), in_specs=[ANY], out_specs=ANY,
        scratch_shapes=[pltpu.SemaphoreType.DMA((3,)), pltpu.SemaphoreType.DMA((3,)), pltpu.SemaphoreType.DMA],
        name=name,
    )(x_shard)


def allgather_chips_2level(x_shard, *, name):
    r, cdim = x_shard.shape
    half = r // 2

    def body(x_ref, out_ref, send_sems, recv_sems, local_sem):
        x, y, c, chips = _chip_peers()
        me = 2 * x + y
        mine_rows = pl.ds(c * half, half)
        other_rows = pl.ds((1 - c) * half, half)
        mine = pltpu.make_async_copy(x_ref, out_ref.at[me], local_sem)
        mine.start()

        def copy(k, slot, rows, to, src=None):
            dst = out_ref.at[slot, rows, :]
            return pltpu.make_async_remote_copy(src_ref=dst if src is None else src, dst_ref=dst, send_sem=send_sems.at[k],
                                                recv_sem=recv_sems.at[k], device_id=to, device_id_type=MESH)

        first = [copy(k, me, mine_rows, (px, py, c), src=x_ref.at[mine_rows, :]) for k, (px, py) in enumerate(chips)]
        for cp in first:
            cp.start()
        passed = [copy(3 + k, 2 * px + py, mine_rows, (x, y, 1 - c)) for k, (px, py) in enumerate(chips)]
        for k, (px, py) in enumerate(chips):
            copy(k, 2 * px + py, mine_rows, (px, py, c)).wait_recv()
            passed[k].start()
        for k, (px, py) in enumerate(chips):
            copy(3 + k, 2 * px + py, other_rows, (x, y, 1 - c)).wait_recv()
        for cp in first + passed:
            cp.wait_send()
        mine.wait()

    return pl.pallas_call(
        body, out_shape=jax.ShapeDtypeStruct((4, r, cdim), x_shard.dtype), in_specs=[ANY], out_specs=ANY,
        scratch_shapes=[pltpu.SemaphoreType.DMA((6,)), pltpu.SemaphoreType.DMA((6,)), pltpu.SemaphoreType.DMA],
        name=name,
    )(x_shard)


def exchange_chips(g, *, name):
    _, r, cdim = g.shape

    def body(g_ref, out_ref, send_sems, recv_sems, local_sem):
        x, y, c, chips = _chip_peers()
        me = 2 * x + y
        mine = pltpu.make_async_copy(g_ref.at[me], out_ref.at[me], local_sem)
        mine.start()
        sends = []
        for k, (px, py) in enumerate(chips):
            cp = pltpu.make_async_remote_copy(src_ref=g_ref.at[2 * px + py], dst_ref=out_ref.at[me], send_sem=send_sems.at[k],
                                              recv_sem=recv_sems.at[k], device_id=(px, py, c), device_id_type=MESH)
            cp.start()
            sends.append(cp)
        for k, (px, py) in enumerate(chips):
            pltpu.make_async_remote_copy(src_ref=g_ref.at[me], dst_ref=out_ref.at[2 * px + py], send_sem=send_sems.at[k],
                                         recv_sem=recv_sems.at[k], device_id=(px, py, c), device_id_type=MESH).wait_recv()
        for cp in sends:
            cp.wait_send()
        mine.wait()

    return pl.pallas_call(
        body, out_shape=jax.ShapeDtypeStruct(g.shape, g.dtype), in_specs=[ANY], out_specs=ANY,
        scratch_shapes=[pltpu.SemaphoreType.DMA((3,)), pltpu.SemaphoreType.DMA((3,)), pltpu.SemaphoreType.DMA],
        name=name,
    )(g)


def swap_sibling(p, *, name):
    def body(p_ref, out_ref, send_sem, recv_sem):
        x, y, c = lax.axis_index("x"), lax.axis_index("y"), lax.axis_index("c")
        cp = pltpu.make_async_remote_copy(src_ref=p_ref, dst_ref=out_ref, send_sem=send_sem, recv_sem=recv_sem,
                                          device_id=(x, y, 1 - c), device_id_type=MESH)
        cp.start()
        cp.wait()

    return pl.pallas_call(
        body, out_shape=jax.ShapeDtypeStruct(p.shape, p.dtype), in_specs=[ANY], out_specs=ANY,
        scratch_shapes=[pltpu.SemaphoreType.DMA, pltpu.SemaphoreType.DMA], name=name,
    )(p)


def swap_other_half(g, *, name):
    n, r, cdim = g.shape
    half = r // 2

    def body(g_ref, out_ref, send_sem, recv_sem):
        x, y, c = lax.axis_index("x"), lax.axis_index("y"), lax.axis_index("c")
        cp = pltpu.make_async_remote_copy(src_ref=g_ref.at[:, pl.ds((1 - c) * half, half), :], dst_ref=out_ref, send_sem=send_sem,
                                          recv_sem=recv_sem, device_id=(x, y, 1 - c), device_id_type=MESH)
        cp.start()
        cp.wait()

    return pl.pallas_call(
        body, out_shape=jax.ShapeDtypeStruct((n, half, cdim), g.dtype), in_specs=[ANY], out_specs=ANY,
        scratch_shapes=[pltpu.SemaphoreType.DMA, pltpu.SemaphoreType.DMA], name=name,
    )(g)


def add_pairs(a, b, *, name, out_dtype):
    n, rows, cdim = a.shape
    t = _pick(rows, (256, 128, 64, 32, 16))

    def body(a_ref, b_ref, o_ref):
        o_ref[...] = (a_ref[...].astype(F32) + b_ref[...].astype(F32)).astype(o_ref.dtype)

    spec = pl.BlockSpec((n, t, cdim), lambda i: (0, i, 0))
    return pl.pallas_call(
        body, grid=(rows // t,), in_specs=[spec, spec], out_specs=spec, out_shape=jax.ShapeDtypeStruct(a.shape, out_dtype),
        compiler_params=_cp(("parallel",)), name=name,
    )(a, b)


def sum_slots(r, *, name):
    n, rows, cdim = r.shape
    t = _pick(rows, (256, 128, 64, 32, 16, 8))

    def body(r_ref, o_ref):
        acc = r_ref[0].astype(F32)
        for s in range(1, n):
            acc = acc + r_ref[s].astype(F32)
        o_ref[...] = acc

    return pl.pallas_call(
        body, grid=(rows // t,), in_specs=[pl.BlockSpec((n, t, cdim), lambda i: (0, i, 0))],
        out_specs=pl.BlockSpec((t, cdim), lambda i: (i, 0)), out_shape=jax.ShapeDtypeStruct((rows, cdim), F32),
        compiler_params=_cp(("parallel",)), name=name,
    )(r)


def _rms(x, g):
    return x * lax.rsqrt(jnp.mean(x * x, axis=-1, keepdims=True) + NORM_EPS) * g


def _adaln(x, g, shift, scale):
    return _rms(x, g) * (1.0 + scale) + shift


def _silu(x):
    return x * jax.nn.sigmoid(x)


def _gdn_gate(o, z, g):
    return jnp.concatenate([_rms(o[:, 128 * h:128 * (h + 1)], g) * _silu(z[:, 128 * h:128 * (h + 1)]) for h in range(4)], axis=1)


def _ssd_gate(y, z0, z1, z2, z3, g):
    outs = []
    for k, z in enumerate((z0, z1, z2, z3)):
        t = y[:, 512 * k:512 * (k + 1)] * _silu(z)
        outs.append(t * lax.rsqrt(jnp.mean(t * t, axis=-1, keepdims=True) + NORM_EPS))
    return jnp.concatenate(outs, axis=1) * g


def _rope(x, cos, sin, rot):
    return x * cos + _dot(x, rot, hi=True) * sin


def _rope_q(q, cos, sin, rot):
    parts = []
    for h in range(4):
        parts += [q[:, 256 * h:256 * h + 128], _rope(q[:, 256 * h + 128:256 * (h + 1)], cos, sin, rot)]
    return jnp.concatenate(parts, axis=1) * ATT_SCALE


def _rope_t(d, cos, sin, rot):
    return d * cos + _dot(d * sin, rot, _NT, hi=True)


def _vjp_rows(fn, n_rows, n_pars, out_dtypes, rows, cts, pars, *, name, tile=512, extra=None):
    nct = len(cts)

    def bwd(*a):
        r, c, e, p = a[:n_rows], a[n_rows:n_rows + nct], a[n_rows + nct:len(a) - n_pars], a[len(a) - n_pars:]
        out, vjp = jax.vjp(fn, *[t.astype(F32) for t in r], *p)
        ct = tuple(t.astype(F32) for t in c)
        grads = vjp(ct[0] if not isinstance(out, tuple) else ct)
        drows = list(grads[:n_rows])
        if e:
            drows[0] = drows[0] + e[0]
        return (*drows, *grads[n_rows:])

    return rowmap(bwd, list(rows) + list(cts) + ([extra] if extra is not None else []), list(pars), out_dtypes,
                  name=name, tile=tile, n_reduce=n_pars)


def _gate_grads(dw_raw, w, gate, scale, *, name):
    dw, dg = rowmap(lambda r, w_, gt: ((scale * gt) * r, jnp.sum((scale * w_.astype(F32)) * r, axis=0, keepdims=True)),
                    [dw_raw, w], [gate], (F32, F32), name=name, tile=256, n_reduce=1)
    return dw, dg[0]


ADAM_LR, ADAM_B1, ADAM_B2, ADAM_EPS, ADAM_WD, ADAM_STEP = 0.001, 0.9, 0.999, 1e-08, 0.01, 10


def _adam_math(w, g, m, v):
    m = ADAM_B1 * m + (1.0 - ADAM_B1) * g
    v = ADAM_B2 * v + (1.0 - ADAM_B2) * (g * g)
    m_hat = m / (1.0 - ADAM_B1 ** ADAM_STEP)
    v_hat = v / (1.0 - ADAM_B2 ** ADAM_STEP)
    delta = -ADAM_LR * (m_hat / (jnp.sqrt(v_hat) + ADAM_EPS) + ADAM_WD * w)
    return delta, m, v


def adamw(w, gs, m, v, *, name):
    shape = w.shape
    last = shape[-1]
    to2 = lambda a: a.reshape(-1, last)
    rows = w.size // last
    tile = _pick(rows, (256, 128, 64, 32, 16, 8))
    ng = len(gs)

    def fn(w_, *rest):
        g = rest[0]
        for t in rest[1:ng]:
            g = g + t
        m_, v_ = rest[ng], rest[ng + 1]
        return (g, *_adam_math(w_, g, m_, v_))

    outs = rowmap(fn, [to2(w)] + [to2(g) for g in gs] + [to2(m), to2(v)], [], (F32,) * 4, name=name, tile=tile)
    return tuple(o.reshape(shape) for o in outs)


PACK_W = 1024
BIG = (
    ("ffn_w1", (4, 2, 1024, 704), 3), ("ffn_w3", (4, 2, 1024, 704), 3), ("ffn_w2", (4, 2, 704, 1024), 2),
    ("ev_w_in", (2, 1024, 690), 2), ("mla_w_uq", (2, 96, 4, 192), 1), ("mla_w_ukv", (2, 64, 4, 256), 1),
    ("ev_w_out", (2, 256, 1024), 1), ("ssd_w_in", (2, 1024, 1288), 2), ("ssd_w_out", (2, 512, 1024), 1))


def _seg_rows(shape):
    n = math.prod(shape)
    return -(-n // (16 * PACK_W)) * 16


PACK_ROWS = -(-sum(_seg_rows(sh) for _, sh, _ in BIG) // 512) * 512


def _pack(shards, dtype):
    parts = []
    for (_, shape, _), a in zip(BIG, shards):
        flat = a.reshape(-1).astype(dtype)
        pad = _seg_rows(shape) * PACK_W - flat.shape[0]
        parts.append(jnp.pad(flat, (0, pad)) if pad else flat)
    tail = PACK_ROWS - sum(_seg_rows(sh) for _, sh, _ in BIG)
    if tail:
        parts.append(jnp.zeros((tail * PACK_W,), dtype))
    return jnp.concatenate(parts).reshape(-1, PACK_W)


def _pack_by_owner(fulls):
    cols = []
    for (_, shape, ax), f in zip(BIG, fulls):
        blk = jnp.stack([lax.slice_in_dim(f, s * shape[ax], (s + 1) * shape[ax], axis=ax).reshape(-1).astype(BF16) for s in range(4)])
        pad = _seg_rows(shape) * PACK_W - blk.shape[1]
        cols.append((jnp.pad(blk, ((0, 0), (0, pad))) if pad else blk).reshape(4, -1, PACK_W))
    tail = PACK_ROWS - sum(_seg_rows(sh) for _, sh, _ in BIG)
    if tail:
        cols.append(jnp.zeros((4, tail, PACK_W), BF16))
    return jnp.concatenate(cols, axis=1)


def _unpack(buf):
    out, r0 = [], 0
    for _, shape, _ in BIG:
        n = math.prod(shape)
        out.append(buf[r0:r0 + _seg_rows(shape)].reshape(-1)[:n].reshape(shape))
        r0 += _seg_rows(shape)
    return out


SMALL_SHARDED = (
    ("norm_g", (4, 3, 256), 2), ("gdn_conv_w", (2, 4, 384), 2), ("ssd_conv_w", (2, 4, 768), 2),
    ("ssd_conv_b", (2, 768), 1), ("ssd_norm_g", (2, 512), 1))


def _flat_pack(arrs, width, row_mult):
    flat = jnp.concatenate([a.reshape(-1).astype(F32) for a in arrs])
    n = flat.shape[0]
    tot = -(-n // (width * row_mult)) * width * row_mult
    return jnp.pad(flat, (0, tot - n)).reshape(-1, width)


def _flat_unpack(buf, shapes):
    flat = buf.reshape(-1)
    out, o = [], 0
    for s in shapes:
        n = math.prod(s)
        out.append(flat[o:o + n].reshape(s))
        o += n
    return out


def _rep(v, n):
    return jnp.repeat(v, n, axis=-1)


def kernel(x, c, positions, ada_w, ada_b, norm_g, ffn_w1, ffn_w3, ffn_w2, ev_w_in, gdn_conv_w, gdn_A_log, gdn_dt_bias, gdn_norm_g, mla_q_norm_g, mla_w_uq, mla_kv_norm_g, mla_w_ukv, ev_w_out, ssd_w_in, ssd_conv_w, ssd_conv_b, ssd_A_log, ssd_dt_bias, ssd_D, ssd_norm_g, ssd_w_out, final_g, loss_target, m_ada_w, m_ada_b, m_norm_g, m_ffn_w1, m_ffn_w3, m_ffn_w2, m_ev_w_in, m_gdn_conv_w, m_gdn_A_log, m_gdn_dt_bias, m_gdn_norm_g, m_mla_q_norm_g, m_mla_w_uq, m_mla_kv_norm_g, m_mla_w_ukv, m_ev_w_out, m_ssd_w_in, m_ssd_conv_w, m_ssd_conv_b, m_ssd_A_log, m_ssd_dt_bias, m_ssd_D, m_ssd_norm_g, m_ssd_w_out, m_final_g, v_ada_w, v_ada_b, v_norm_g, v_ffn_w1, v_ffn_w3, v_ffn_w2, v_ev_w_in, v_gdn_conv_w, v_gdn_A_log, v_gdn_dt_bias, v_gdn_norm_g, v_mla_q_norm_g, v_mla_w_uq, v_mla_kv_norm_g, v_mla_w_ukv, v_ev_w_out, v_ssd_w_in, v_ssd_conv_w, v_ssd_conv_b, v_ssd_A_log, v_ssd_dt_bias, v_ssd_D, v_ssd_norm_g, v_ssd_w_out, v_final_g):
    P = dict(ada_w=ada_w, ada_b=ada_b, norm_g=norm_g, ffn_w1=ffn_w1, ffn_w3=ffn_w3, ffn_w2=ffn_w2, ev_w_in=ev_w_in, gdn_conv_w=gdn_conv_w, gdn_A_log=gdn_A_log, gdn_dt_bias=gdn_dt_bias, gdn_norm_g=gdn_norm_g, mla_q_norm_g=mla_q_norm_g, mla_w_uq=mla_w_uq, mla_kv_norm_g=mla_kv_norm_g, mla_w_ukv=mla_w_ukv, ev_w_out=ev_w_out, ssd_w_in=ssd_w_in, ssd_conv_w=ssd_conv_w, ssd_conv_b=ssd_conv_b, ssd_A_log=ssd_A_log, ssd_dt_bias=ssd_dt_bias, ssd_D=ssd_D, ssd_norm_g=ssd_norm_g, ssd_w_out=ssd_w_out, final_g=final_g)
    M1 = dict(ada_w=m_ada_w, ada_b=m_ada_b, norm_g=m_norm_g, ffn_w1=m_ffn_w1, ffn_w3=m_ffn_w3, ffn_w2=m_ffn_w2, ev_w_in=m_ev_w_in, gdn_conv_w=m_gdn_conv_w, gdn_A_log=m_gdn_A_log, gdn_dt_bias=m_gdn_dt_bias, gdn_norm_g=m_gdn_norm_g, mla_q_norm_g=m_mla_q_norm_g, mla_w_uq=m_mla_w_uq, mla_kv_norm_g=m_mla_kv_norm_g, mla_w_ukv=m_mla_w_ukv, ev_w_out=m_ev_w_out, ssd_w_in=m_ssd_w_in, ssd_conv_w=m_ssd_conv_w, ssd_conv_b=m_ssd_conv_b, ssd_A_log=m_ssd_A_log, ssd_dt_bias=m_ssd_dt_bias, ssd_D=m_ssd_D, ssd_norm_g=m_ssd_norm_g, ssd_w_out=m_ssd_w_out, final_g=m_final_g)
    M2 = dict(ada_w=v_ada_w, ada_b=v_ada_b, norm_g=v_norm_g, ffn_w1=v_ffn_w1, ffn_w3=v_ffn_w3, ffn_w2=v_ffn_w2, ev_w_in=v_ev_w_in, gdn_conv_w=v_gdn_conv_w, gdn_A_log=v_gdn_A_log, gdn_dt_bias=v_gdn_dt_bias, gdn_norm_g=v_gdn_norm_g, mla_q_norm_g=v_mla_q_norm_g, mla_w_uq=v_mla_w_uq, mla_kv_norm_g=v_mla_kv_norm_g, mla_w_ukv=v_mla_w_ukv, ev_w_out=v_ev_w_out, ssd_w_in=v_ssd_w_in, ssd_conv_w=v_ssd_conv_w, ssd_conv_b=v_ssd_conv_b, ssd_A_log=v_ssd_A_log, ssd_dt_bias=v_ssd_dt_bias, ssd_D=v_ssd_D, ssd_norm_g=v_ssd_norm_g, ssd_w_out=v_ssd_w_out, final_g=v_final_g)
    names = list(P)
    xi, yi, ci = lax.axis_index("x"), lax.axis_index("y"), lax.axis_index("c")
    chip = 2 * xi + yi
    bidx = 4 * xi + 2 * yi + ci
    xa = x[0]
    S_, D = xa.shape
    tgt = loss_target[0]
    depth = ffn_w1.shape[0]

    wg = allgather_chips_2level(_pack([P[n] for n, _, _ in BIG], BF16), name="gather_weights")
    per_chip = [_unpack(wg[s]) for s in range(4)]
    W = {n: jnp.concatenate([per_chip[s][k] for s in range(4)], axis=ax) for k, (n, _, ax) in enumerate(BIG)}
    sg = allgather_chips(_flat_pack([P[n] for n, _, _ in SMALL_SHARDED], 1024, 16), name="gather_small")
    per_chip_s = [_flat_unpack(sg[s], [sh for _, sh, _ in SMALL_SHARDED]) for s in range(4)]
    Wf = {n: jnp.concatenate([per_chip_s[s][k] for s in range(4)], axis=ax) for k, (n, _, ax) in enumerate(SMALL_SHARDED)}

    c_all = allgather8(jnp.pad(c, ((0, 7), (0, 0))), name="gather_c").reshape(8, 8, D)[:, 0]
    c_act, = rowmap(lambda t: (_silu(t),), [jnp.pad(c_all, ((0, 8), (0, 0)))], [], (F32,), name="c_act", tile=16)
    ncol = ada_w.shape[2]
    ada_b_loc = lax.dynamic_slice(ada_b, (0, chip * ncol), (depth, ncol))
    mod_loc = [mm((c_act, ada_w[l]), name=f"mod_{l}", epi=lambda acc, b: (acc + b,), epi_pars=(ada_b_loc[l][None],),
                  epi_out_dtypes=(F32,), tm=16, tn=256)[0][:8] for l in range(depth)]
    mod_g = allgather8(jnp.stack(mod_loc).reshape(-1, 1024), name="gather_mod").reshape(8, depth, 8, ncol)
    mod_b = lax.dynamic_index_in_dim(mod_g[0::2], bidx, axis=2, keepdims=False)
    mod = jnp.transpose(mod_b, (1, 0, 2)).reshape(depth, 3, 3, D)

    def ev_ext(w):
        z = lambda n: jnp.zeros((w.shape[0], n), w.dtype)
        return jnp.concatenate([w[:, :2048], _rep(w[:, 2048:2052], 128), _rep(w[:, 2052:2056], 128), w[:, 2056:2440], z(128),
                                w[:, 2440:2696], w[:, 2696:2760], z(192)], axis=1)

    def ev_ext_t(dw):
        return jnp.concatenate([dw[:, :2048], dw[:, 2048:2560].reshape(-1, 4, 128).sum(-1), dw[:, 2560:3072].reshape(-1, 4, 128).sum(-1),
                                dw[:, 3072:3456], dw[:, 3584:3840], dw[:, 3840:3904]], axis=1)

    def od_ext(w):
        return jnp.concatenate([w[:, 2048:5120], w[:, :2048], _rep(w[:, 5120:5152], 64)], axis=1)

    def od_ext_t(dw):
        return jnp.concatenate([dw[:, 3072:5120], dw[:, :3072], dw[:, 5120:].reshape(-1, 32, 64).sum(-1)], axis=1)

    def wq_ext(w):
        return jnp.pad(w, ((0, 0), (0, 0), (0, 64))).reshape(384, 1024)

    def wq_ext_t(dw):
        return dw.reshape(384, 4, 256)[:, :, :192]

    def wkv_ext(w):
        return jnp.concatenate([w[:, :, :128].reshape(256, 512), w[:, :, 128:].reshape(256, 512)], axis=1)

    def wkv_ext_t(dw):
        return jnp.concatenate([dw[:, :512].reshape(256, 4, 128), dw[:, 512:].reshape(256, 4, 128)], axis=2)

    half = 32
    inv_freq = 10000.0 ** (-jnp.arange(half, dtype=F32) / half)
    ang = positions[0].astype(F32)[:, None] * inv_freq
    zpad = jnp.zeros((S_, 64), F32)
    cos_t = jnp.concatenate([jnp.cos(ang), jnp.cos(ang), zpad], axis=1)
    sin_t = jnp.concatenate([jnp.sin(ang), jnp.sin(ang), zpad], axis=1)
    ii = jnp.arange(128)
    rot = (jnp.where((ii[:, None] < 32) & (ii[None, :] == ii[:, None] + 32), 1.0, 0.0)
           - jnp.where((ii[:, None] >= 32) & (ii[:, None] < 64) & (ii[None, :] == ii[:, None] - 32), 1.0, 0.0)).astype(F32)

    grads = {}
    dmod = [[[None] * 3 for _ in range(3)] for _ in range(depth)]
    dnorm_g = [[None] * 3 for _ in range(depth)]

    def acc(name, idx, val):
        grads.setdefault(name, {})[idx] = val

    def ffn_sub(xin, l, k, j):
        g, (shift, scale, gate) = Wf["norm_g"][l, k][None], [mod[l, k, t][None] for t in range(3)]
        w1, w3, w2 = W["ffn_w1"][l, j], W["ffn_w3"][l, j], W["ffn_w2"][l, j]
        tag = f"l{l}f{j}"
        h, = rowmap(lambda *a: (_adaln(*a),), [xin], [g, shift, scale], (BF16,), name=f"adaln_{tag}", tile=1024)
        a, u16, v16 = ffn_mid_fwd(h, w1, w3, name=f"ffn_mid_{tag}")
        xn, = mm((a, w2), name=f"ffn_out_{tag}", epi=lambda acc_, xr, gt: (xr + 0.5 * gt * acc_,), epi_rows=(xin,),
                 epi_pars=(gate,), epi_out_dtypes=(F32,))

        def bwd(dxn):
            du, dv = ffn_mid_bwd(u16, v16, dxn, ((0.5 * gate) * w2).astype(BF16), name=f"ffn_midb_{tag}")
            dh = mm([(du, w1), (dv, w3)], tb=True, name=f"ffn_dh_{tag}")
            acc("ffn_w1", (l, j), mm((h, du), ta=True, name=f"ffn_dw1_{tag}"))
            acc("ffn_w3", (l, j), mm((h, dv), ta=True, name=f"ffn_dw3_{tag}"))
            dw2, dgate = _gate_grads(mm((a, dxn), ta=True, name=f"ffn_dw2_{tag}"), w2, gate, 0.5, name=f"ffn_dgate_{tag}")
            acc("ffn_w2", (l, j), dw2)
            dx, dg, dsh, dsc = _vjp_rows(_adaln, 1, 3, (F32,), [xin], [dh], [g, shift, scale], name=f"adalnb_{tag}", extra=dxn, tile=1024)
            dnorm_g[l][k] = dg[0]
            dmod[l][k] = [dsh[0], dsc[0], dgate]
            return dx

        return xn, bwd

    def mixer_tail(xin, l, tag, dh, dxn, g, shift, scale, dgate):
        dx, dg, dsh, dsc = _vjp_rows(_adaln, 1, 3, (F32,), [xin], [dh], [g, shift, scale], name=f"adalnb_{tag}", extra=dxn, tile=1024)
        dnorm_g[l][1] = dg[0]
        dmod[l][1] = [dsh[0], dsc[0], dgate]
        return dx

    def even_sub(xin, l):
        e = l // 2
        tag = f"l{l}m"
        g, (shift, scale, gate) = Wf["norm_g"][l, 1][None], [mod[l, 1, t][None] for t in range(3)]
        wext, wq, wkv, wout = ev_ext(W["ev_w_in"][e]), wq_ext(W["mla_w_uq"][e]), wkv_ext(W["mla_w_ukv"][e]), W["ev_w_out"][e]
        conv_w, zb = Wf["gdn_conv_w"][e], jnp.zeros((1, 1536), F32)
        alog_e, dtb_e = _rep(gdn_A_log[e], 128)[None], _rep(gdn_dt_bias[e], 128)[None]
        gg, qg, kvg = gdn_norm_g[e][None], mla_q_norm_g[e][None], mla_kv_norm_g[e][None]
        h, = rowmap(lambda *a: (_adaln(*a),), [xin], [g, shift, scale], (BF16,), name=f"adaln_{tag}", tile=1024)
        proj = mm((h, wext), name=f"ev_in_{tag}")
        qkvc = conv_fwd(proj, 0, 3, conv_w, zb, name=f"gdn_conv_{tag}")
        o_g, hist = gdn_fwd(qkvc, proj, 4, 5, alog_e, dtb_e, name=f"gdn_{tag}")
        o_a, = rowmap(lambda o, z, g_: (_gdn_gate(o, z, g_),), [o_g, (proj, 512, 3)], [gg], (BF16,), name=f"gdn_gate_{tag}")
        cqn, = rowmap(lambda t, g_: (_rms(t, g_),), [(proj, 384, 8)], [qg], (BF16,), name=f"q_norm_{tag}")
        ckvn, = rowmap(lambda t, g_: (_rms(t, g_),), [(proj, 256, 14)], [kvg], (BF16,), name=f"kv_norm_{tag}")
        q0 = mm((cqn, wq), name=f"q_up_{tag}")
        kv = mm((ckvn, wkv), name=f"kv_up_{tag}", out_dtype=BF16)
        q, = rowmap(lambda t, cs, sn, r: (_rope_q(t, cs, sn, r),), [q0, cos_t, sin_t], [rot], (BF16,), name=f"rope_q_{tag}")
        kp, = rowmap(lambda t, cs, sn, r: (_rope(t, cs, sn, r),), [(proj, 128, 30), cos_t, sin_t], [rot], (BF16,), name=f"rope_k_{tag}")
        o_b, lse = att_fwd(q, kv, kp, name=f"att_{tag}")
        xn, = mm([(o_a, wout[:512]), (o_b, wout[512:])], name=f"ev_out_{tag}", epi=lambda acc_, xr, gt: (xr + gt * acc_,),
                 epi_rows=(xin,), epi_pars=(gate,), epi_out_dtypes=(F32,))

        def bwd(dxn):
            wout_g = (gate * wout).astype(BF16)
            do_a = mm((dxn, wout_g[:512]), tb=True, name=f"ev_doa_{tag}")
            do_b = mm((dxn, wout_g[512:]), tb=True, name=f"ev_dob_{tag}")
            dw_raw = jnp.concatenate([mm((o_a, dxn), ta=True, name=f"ev_dwoa_{tag}"), mm((o_b, dxn), ta=True, name=f"ev_dwob_{tag}")], axis=0)
            dwo, dgate = _gate_grads(dw_raw, wout, gate, 1.0, name=f"ev_dgate_{tag}")
            acc("ev_w_out", e, dwo)
            dsum, = rowmap(lambda d, o_: (jnp.concatenate([jnp.broadcast_to(jnp.sum(d[:, 128 * hh:128 * (hh + 1)] * o_[:, 128 * hh:128 * (hh + 1)],
                                                                                        axis=-1, keepdims=True), (d.shape[0], 128))
                                                  for hh in range(4)], axis=1),), [do_b, o_b], [], (F32,), name=f"att_dsum_{tag}")
            dq4, dk2, dv = att_bwd(q, kv, kp, lse, dsum, do_b, name=f"att_bwd_{tag}")

            def rope_qb(d0, d1, d2, d3, cs, sn, r):
                parts = []
                for d in (d0, d1, d2, d3):
                    parts += [d[:, :128], _rope_t(d[:, 128:], cs, sn, r)]
                return (jnp.concatenate(parts, axis=1) * ATT_SCALE,)

            dq0, = rowmap(rope_qb, [dq4[0], dq4[1], dq4[2], dq4[3], cos_t, sin_t], [rot], (BF16,), name=f"rope_qb_{tag}")

            def rope_kb(d, cs, sn, r):
                dkp = d[:, 128:256] + d[:, 384:512] + d[:, 640:768] + d[:, 896:1024]
                return jnp.concatenate([d[:, 256 * hh:256 * hh + 128] for hh in range(4)], axis=1), _rope_t(dkp, cs, sn, r)

            dkn, dkr = rowmap(rope_kb, [dk2, cos_t, sin_t], [rot], (BF16, BF16), name=f"rope_kb_{tag}")
            dcqn = mm((dq0, wq), tb=True, name=f"q_upb_{tag}")
            acc("mla_w_uq", e, wq_ext_t(mm((cqn, dq0), ta=True, name=f"q_dw_{tag}")))
            dckvn = mm([(dkn, wkv[:, :512]), (dv, wkv[:, 512:])], tb=True, name=f"kv_upb_{tag}")
            acc("mla_w_ukv", e, wkv_ext_t(jnp.concatenate([mm((ckvn, dkn), ta=True, name=f"kv_dwk_{tag}"), mm((ckvn, dv), ta=True, name=f"kv_dwv_{tag}")], axis=1)))
            dcq, dqg = _vjp_rows(_rms, 1, 1, (BF16,), [(proj, 384, 8)], [dcqn], [qg], name=f"q_normb_{tag}")
            dckv, dkvg = _vjp_rows(_rms, 1, 1, (BF16,), [(proj, 256, 14)], [dckvn], [kvg], name=f"kv_normb_{tag}")
            acc("mla_q_norm_g", e, dqg[0])
            acc("mla_kv_norm_g", e, dkvg[0])
            do_g, dz, dgg = _vjp_rows(_gdn_gate, 2, 1, (F32, BF16), [o_g, (proj, 512, 3)], [do_a], [gg], name=f"gdn_gateb_{tag}")
            acc("gdn_norm_g", e, dgg[0])
            dqkvc, dbe, dae, dal, ddt = gdn_bwd(qkvc, proj, 4, 5, alog_e, dtb_e, hist, do_g, name=f"gdnb_{tag}")
            acc("gdn_A_log", e, dal.reshape(4, 128).sum(-1))
            acc("gdn_dt_bias", e, ddt.reshape(4, 128).sum(-1))
            dpre, dcw, _ = conv_bwd_pre(proj, 0, 3, conv_w, zb, dqkvc, name=f"gdn_convb_{tag}")
            acc("gdn_conv_w", e, dcw)
            dqkv = conv_bwd_x(dpre, conv_w, name=f"gdn_convx_{tag}", out_dtype=BF16)
            zc = lambda n: jnp.zeros((S_, n), BF16)
            dproj = jnp.concatenate([dqkv, dz, dbe, dae, dcq, zc(128), dckv, dkr, zc(128)], axis=1)
            dh = mm((dproj, wext), tb=True, name=f"ev_inb_{tag}")
            acc("ev_w_in", e, ev_ext_t(mm((h, dproj), ta=True, name=f"ev_dwin_{tag}")))
            return mixer_tail(xin, l, tag, dh, dxn, g, shift, scale, dgate)

        return xn, bwd

    def odd_sub(xin, l):
        o = l // 2
        tag = f"l{l}m"
        g, (shift, scale, gate) = Wf["norm_g"][l, 1][None], [mod[l, 1, t][None] for t in range(3)]
        wext, wout = od_ext(W["ssd_w_in"][o]), W["ssd_w_out"][o]
        conv_w, conv_b, ng = Wf["ssd_conv_w"][o], Wf["ssd_conv_b"][o][None], Wf["ssd_norm_g"][o][None]
        ex = lambda v: _rep(v, 64)[None]
        na_e, dtb_e, dsk_e = ex(-jnp.exp(ssd_A_log[o])), ex(ssd_dt_bias[o]), ex(ssd_D[o])
        h, = rowmap(lambda *a: (_adaln(*a),), [xin], [g, shift, scale], (BF16,), name=f"adaln_{tag}", tile=1024)
        proj = mm((h, wext), name=f"ssd_in_{tag}")
        zv = [(proj, 512, 6 + t) for t in range(4)]
        xbc = conv_fwd(proj, 0, 6, conv_w, conv_b, name=f"ssd_conv_{tag}")
        ys, hist = ssd_fwd(xbc, proj, na_e, dtb_e, dsk_e, name=f"ssd_{tag}")
        yn, = rowmap(lambda *a: (_ssd_gate(*a),), [ys] + zv, [ng], (BF16,), name=f"ssd_gate_{tag}", tile=512)
        xn, = mm((yn, wout), name=f"ssd_out_{tag}", epi=lambda acc_, xr, gt: (xr + gt * acc_,), epi_rows=(xin,),
                 epi_pars=(gate,), epi_out_dtypes=(F32,))

        def bwd(dxn):
            dyn = mm((dxn, (gate * wout).astype(BF16)), tb=True, name=f"ssd_dyn_{tag}", out_dtype=BF16)
            dwo, dgate = _gate_grads(mm((yn, dxn), ta=True, name=f"ssd_dwout_{tag}"), wout, gate, 1.0, name=f"ssd_dgate_{tag}")
            acc("ssd_w_out", o, dwo)
            dys, dz0, dz1, dz2, dz3, dng = _vjp_rows(_ssd_gate, 5, 1, (F32, BF16, BF16, BF16, BF16), [ys] + zv, [dyn], [ng],
                                                     name=f"ssd_gateb_{tag}", tile=512)
            acc("ssd_norm_g", o, dng[0])
            dxs, ddtx, db_, dc_, dna, ddtb, ddsk = ssd_bwd(xbc, proj, na_e, dtb_e, dsk_e, hist, dys, name=f"ssdb_{tag}")
            acc("ssd_A_log", o, dna.reshape(32, 64).sum(-1) * (-jnp.exp(ssd_A_log[o])))
            acc("ssd_dt_bias", o, ddtb.reshape(32, 64).sum(-1))
            acc("ssd_D", o, ddsk.reshape(32, 64).sum(-1))
            dxp, dcws, dcbs = [], [], []
            for part, (cb0, ncb, dpart) in enumerate(((0, 4, dxs), (4, 1, db_), (5, 1, dc_))):
                cols = slice(512 * cb0, 512 * (cb0 + ncb))
                dpre, dcw, dcb = conv_bwd_pre(proj, cb0, ncb, conv_w[:, cols], conv_b[:, cols], dpart, name=f"ssd_convb{part}_{tag}")
                dxp.append(conv_bwd_x(dpre, conv_w[:, cols], name=f"ssd_convx{part}_{tag}", out_dtype=BF16))
                dcws.append(dcw)
                dcbs.append(dcb[0])
            acc("ssd_conv_w", o, jnp.concatenate(dcws, axis=1))
            acc("ssd_conv_b", o, jnp.concatenate(dcbs))
            dproj = jnp.concatenate(dxp + [dz0, dz1, dz2, dz3, ddtx], axis=1)
            dh = mm((dproj, wext), tb=True, name=f"ssd_inb_{tag}")
            acc("ssd_w_in", o, od_ext_t(mm((h, dproj), ta=True, name=f"ssd_dwin_{tag}")))
            return mixer_tail(xin, l, tag, dh, dxn, g, shift, scale, dgate)

        return xn, bwd

    tape = []
    xc = xa
    for l in range(depth):
        xc, b0 = ffn_sub(xc, l, 0, 0)
        xc, b1 = (even_sub if l % 2 == 0 else odd_sub)(xc, l)
        xc, b2 = ffn_sub(xc, l, 2, 1)
        tape += [b0, b1, b2]

    def head(xr, tg, g_):
        def f(xv, gv):
            err = _rms(xv, gv) - tg
            return 0.5 * jnp.sum(jnp.mean(err * err, axis=-1, keepdims=True), axis=0, keepdims=True)
        lo, vjp = jax.vjp(f, xr, g_)
        dxv, dgv = vjp(jnp.ones_like(lo))
        return dxv, jnp.broadcast_to(lo, (1, 128)), dgv

    dx, loss_p, dfg = rowmap(head, [xc, tgt], [final_g[None]], (F32,), name="loss_head", n_reduce=2)
    loss = lax.psum(loss_p[0, 0], ("x", "y", "c"))

    for b in reversed(tape):
        dx = b(dx)
    grad_x = dx[None]

    full = {n: jnp.stack([grads[n][k] for k in sorted(grads[n])]) for n in ("ev_w_in", "mla_w_uq", "mla_w_ukv", "ev_w_out", "ssd_w_in", "ssd_w_out")}
    for n in ("ffn_w1", "ffn_w3", "ffn_w2"):
        full[n] = jnp.stack([jnp.stack([grads[n][(l, j)] for j in range(2)]) for l in range(depth)])
    gall = _pack_by_owner([full[n] for n, _, _ in BIG])
    half = PACK_ROWS // 2
    from_sib = swap_other_half(gall, name="swap_half")
    pair = add_pairs(lax.dynamic_slice_in_dim(gall, ci * half, half, axis=1), from_sib, name="add_sibling", out_dtype=BF16)
    recv = exchange_chips(pair, name="exchange_grads")
    part = sum_slots(recv, name="sum_chips")
    sib = swap_sibling(part, name="swap_sibling")
    lo = jnp.where(ci == 0, part, sib)
    hi_ = jnp.where(ci == 0, sib, part)
    g_tot = _unpack(jnp.concatenate([lo, hi_], axis=0))

    dmod_flat = jnp.stack([jnp.stack([jnp.stack(dmod[l][k]) for k in range(3)]) for l in range(depth)]).reshape(depth, 9 * D)
    small_names = ["norm_g", "gdn_conv_w", "gdn_A_log", "gdn_dt_bias", "gdn_norm_g", "mla_q_norm_g", "mla_kv_norm_g",
                   "ssd_conv_w", "ssd_conv_b", "ssd_A_log", "ssd_dt_bias", "ssd_D", "ssd_norm_g", "final_g"]
    small_full = {n: jnp.stack([grads[n][k] for k in sorted(grads[n])]) for n in small_names if n in grads}
    small_full["norm_g"] = jnp.stack([jnp.stack(dnorm_g[l]) for l in range(depth)])
    small_full["final_g"] = dfg[0]
    small_list = [dmod_flat] + [small_full[n] for n in small_names]
    small_shapes = [a.shape for a in small_list]
    sp = _flat_pack(small_list, 128, 8)
    sgath = allgather8(sp, name="gather_small_grads").reshape(8, sp.shape[0], 128)
    ssum = sum_slots(sgath, name="sum_small")
    tot = dict(zip(["ada_b"] + small_names, _flat_unpack(ssum, small_shapes)))
    dmod_all = sgath.reshape(8, -1)[:, :depth * 9 * D].reshape(8, depth, 9 * D)
    dmod_loc = lax.dynamic_slice(dmod_all, (0, 0, chip * ncol), (8, depth, ncol))
    g_ada_w = jnp.stack([mm((c_act, jnp.pad(dmod_loc[:, l], ((0, 8), (0, 0)))), ta=True, name=f"ada_dw_{l}", tk=16, tn=256)
                         for l in range(depth)])

    def own(n, a):
        for m_, sh, ax in SMALL_SHARDED:
            if m_ == n:
                return lax.dynamic_slice_in_dim(a, chip * sh[ax], sh[ax], axis=ax)
        return a

    res = {}
    for k, (n, _, _) in enumerate(BIG):
        res[n] = adamw(P[n], [g_tot[k]], M1[n], M2[n], name=f"adamw_{n}")
    res["ada_w"] = adamw(ada_w, [g_ada_w], m_ada_w, v_ada_w, name="adamw_ada_w")
    sm = ["ada_b"] + small_names
    shapes = [P[n].shape for n in sm]
    pk = lambda d: _flat_pack([d[n] for n in sm], 128, 8)
    outs = adamw(pk(P), [pk({n: own(n, tot[n]).reshape(P[n].shape) for n in sm})], pk(M1), pk(M2), name="adamw_small")
    un = [_flat_unpack(o, shapes) for o in outs]
    for i, n in enumerate(sm):
        res[n] = tuple(un[t][i] for t in range(4))
    return (loss, grad_x, *[res[n][0] for n in names], *[res[n][1] for n in names], *[res[n][2] for n in names], *[res[n][3] for n in names])
```

```python
import functools
import math

import jax
import jax.numpy as jnp
from jax import lax
from jax.experimental import pallas as pl
from jax.experimental.pallas import tpu as pltpu

F32 = jnp.float32
BF16 = jnp.bfloat16
HI = lax.Precision.HIGHEST
HI3 = lax.Precision.HIGH
VMEM_LIMIT = 56 * 1024 * 1024
NORM_EPS = 1e-6
MM_VMEM_BUDGET = 40 * 1024 * 1024


def _cp(sem=None):
    if sem is None:
        return pltpu.CompilerParams(vmem_limit_bytes=VMEM_LIMIT)
    return pltpu.CompilerParams(dimension_semantics=sem, vmem_limit_bytes=VMEM_LIMIT)


def _pick(dim, prefs):
    for p in prefs:
        if dim % p == 0:
            return p
    return dim


def mm(pairs, *, ta=False, tb=False, out_dtype=F32, name, epi=None, epi_rows=(), epi_pars=(), epi_out_dtypes=None,
       tm=None, tn=None, tk=None):
    if not isinstance(pairs, (list, tuple)) or not isinstance(pairs[0], (list, tuple)):
        pairs = [pairs]
    npair = len(pairs)
    a0, b0 = pairs[0]
    M = a0.shape[1] if ta else a0.shape[0]
    K = a0.shape[0] if ta else a0.shape[1]
    N = b0.shape[0] if tb else b0.shape[1]
    for a, b in pairs:
        assert (a.shape == ((K, M) if ta else (M, K))), (a.shape, M, K)
        assert (b.shape == ((N, K) if tb else (K, N))), (b.shape, K, N)
    tm = tm or _pick(M, (1024, 1408, 512, 384, 256, 128))
    tk = tk or (K if K <= 1024 else _pick(K, (1024, 1408, 512, 256, 128)))
    if tn is None:
        n_epi_out = 1 if epi is None else len(epi_out_dtypes)
        for tn in (1024, 1408, 512, 384, 256, 128, N):
            if N % tn:
                continue
            need = sum(2 * tk * (tm * a.dtype.itemsize + tn * b.dtype.itemsize) for a, b in pairs)
            need += tm * tn * 4 * (1 + 2 * n_epi_out + 2 * len(epi_rows))
            if need <= MM_VMEM_BUDGET:
                break
    nk = K // tk
    assert M % tm == 0 and N % tn == 0 and K % tk == 0, (M, N, K, tm, tn, tk)
    n_rows, n_pars = len(epi_rows), len(epi_pars)
    if epi is None:
        out_dtypes = (out_dtype,)
    else:
        out_dtypes = tuple(epi_out_dtypes)
    n_out = len(out_dtypes)
    dn = (((0 if ta else 1,), (1 if tb else 0,)), ((), ()))

    def body(*refs):
        ab = refs[:2 * npair]
        rows = refs[2 * npair:2 * npair + n_rows]
        pars = refs[2 * npair + n_rows:2 * npair + n_rows + n_pars]
        outs = refs[2 * npair + n_rows + n_pars:2 * npair + n_rows + n_pars + n_out]
        acc_ref = refs[-1]
        k = pl.program_id(2)

        @pl.when(k == 0)
        def _():
            acc_ref[...] = jnp.zeros_like(acc_ref)

        acc = acc_ref[...]
        for p in range(npair):
            a = ab[2 * p][...].astype(BF16)
            b = ab[2 * p + 1][...].astype(BF16)
            acc = acc + lax.dot_general(a, b, dn, preferred_element_type=F32)
        acc_ref[...] = acc

        @pl.when(k == nk - 1)
        def _():
            r = acc_ref[...]
            if epi is None:
                outs[0][...] = r.astype(outs[0].dtype)
            else:
                res = epi(r, *[x[...] for x in rows], *[x[...] for x in pars])
                for o, v in zip(outs, res):
                    o[...] = v.astype(o.dtype)

    a_spec = pl.BlockSpec((tk, tm), lambda i, j, k: (k, i)) if ta else pl.BlockSpec((tm, tk), lambda i, j, k: (i, k))
    b_spec = pl.BlockSpec((tn, tk), lambda i, j, k: (j, k)) if tb else pl.BlockSpec((tk, tn), lambda i, j, k: (k, j))
    in_specs = []
    args = []
    for a, b in pairs:
        in_specs += [a_spec, b_spec]
        args += [a, b]
    for r in epi_rows:
        in_specs.append(pl.BlockSpec((tm, tn), lambda i, j, k: (i, j)))
        args.append(r)
    for p_ in epi_pars:
        in_specs.append(pl.BlockSpec((1, tn), lambda i, j, k: (0, j)))
        args.append(p_)
    out_specs = [pl.BlockSpec((tm, tn), lambda i, j, k: (i, j)) for _ in range(n_out)]
    out_shape = [jax.ShapeDtypeStruct((M, N), d) for d in out_dtypes]
    res = pl.pallas_call(
        body, grid=(M // tm, N // tn, nk), in_specs=in_specs, out_specs=out_specs, out_shape=out_shape,
        scratch_shapes=[pltpu.VMEM((tm, tn), F32)], compiler_params=_cp(("parallel", "parallel", "arbitrary")), name=name,
    )(*args)
    return res[0] if epi is None else tuple(res)


def rowmap(fn, rows, pars, out_dtypes, *, name, tile=1024, n_reduce=0):
    views = []
    for r in rows:
        if isinstance(r, tuple):
            views.append(r)
        else:
            views.append((r, r.shape[1], 0))
    S = views[0][0].shape[0]
    tile = min(tile, S)
    assert S % tile == 0
    nt = S // tile
    row_structs = [jax.ShapeDtypeStruct((tile, w), a.dtype) for a, w, _ in views]
    par_structs = [jax.ShapeDtypeStruct(p.shape, p.dtype) for p in pars]
    out_structs = jax.eval_shape(fn, *row_structs, *par_structs)
    n_out = len(out_structs)
    n_row_out = n_out - n_reduce
    nr, npar = len(views), len(pars)

    def body(*refs):
        ins = [x[...] for x in refs[:nr + npar]]
        outs = refs[nr + npar:]
        res = fn(*ins)
        for o, v in zip(outs[:n_row_out], res[:n_row_out]):
            o[...] = v.astype(o.dtype)
        if n_reduce:
            i = pl.program_id(0)

            @pl.when(i == 0)
            def _():
                for o, v in zip(outs[n_row_out:], res[n_row_out:]):
                    o[...] = v.astype(o.dtype)

            @pl.when(i > 0)
            def _():
                for o, v in zip(outs[n_row_out:], res[n_row_out:]):
                    o[...] += v.astype(o.dtype)

    in_specs = [pl.BlockSpec((tile, w), functools.partial(lambda i, c: (i, c), c=c)) for _, w, c in views]
    in_specs += [pl.BlockSpec(p.shape, lambda i: (0, 0)) for p in pars]
    out_specs = [pl.BlockSpec((tile, s.shape[1]), lambda i: (i, 0)) for s in out_structs[:n_row_out]]
    out_specs += [pl.BlockSpec(s.shape, lambda i: (0, 0)) for s in out_structs[n_row_out:]]
    out_shape = [jax.ShapeDtypeStruct((S, s.shape[1]), d) for s, d in zip(out_structs[:n_row_out], out_dtypes[:n_row_out])]
    out_shape += [jax.ShapeDtypeStruct(s.shape, F32) for s in out_structs[n_row_out:]]
    res = pl.pallas_call(
        body, grid=(nt,), in_specs=in_specs, out_specs=out_specs, out_shape=out_shape,
        compiler_params=_cp(("arbitrary",) if n_reduce else ("parallel",)), name=name,
    )(*[v[0] for v in views], *pars)
    return tuple(res)


CH = 64


def _softplus(x):
    return jnp.where(x > 20.0, x, jnp.log(1.0 + jnp.exp(jnp.minimum(x, 20.0))))


def _dot(a, b, dn=(((1,), (0,)), ((), ())), hi=False):
    if hi:
        return lax.dot_general(a.astype(F32), b.astype(F32), dn, precision=HI if hi is True else hi, preferred_element_type=F32)
    return lax.dot_general(a.astype(BF16), b.astype(BF16), dn, preferred_element_type=F32)


_NT = (((1,), (1,)), ((), ()))
_TN = (((0,), (0,)), ((), ()))


def _chunk_consts():
    r = lax.broadcasted_iota(jnp.int32, (CH, 2 * CH), 0)
    c0 = lax.broadcasted_iota(jnp.int32, (CH, 2 * CH), 1)
    c = jnp.where(c0 >= CH, c0 - CH, c0)
    r1 = lax.broadcasted_iota(jnp.int32, (CH, CH), 0)
    c1 = lax.broadcasted_iota(jnp.int32, (CH, CH), 1)
    return dict(
        lower2=r >= c, strict2=r > c, U2=(r <= c).astype(F32), eye2=(r == c).astype(F32),
        L=(r1 >= c1).astype(F32), ones=jnp.ones((CH, CH), F32), Z=jnp.zeros((CH, 2 * CH), F32))


@jax.custom_vjp
def _tri_inv2(a2s, eye2, Z):
    def prod(x2, y):
        return _dot(x2, jnp.concatenate([y, Z], axis=0), hi=HI3)

    bs = [-a2 for a2 in a2s]
    ts = [eye2 + b for b in bs]
    for _ in range(5):
        bs = [prod(b, b) for b in bs]
        ts = [t + prod(t, b) for t, b in zip(ts, bs)]
    return tuple(ts)


def _tri_inv2_fwd(a2s, eye2, Z):
    ts = _tri_inv2(a2s, eye2, Z)
    return ts, (ts, eye2, Z)


def _tri_inv2_bwd(res, dts):
    ts, eye2, Z = res
    xs = [_dot(t2, dt2, _TN, hi=HI3)[:CH] for t2, dt2 in zip(ts, dts)]
    das = tuple(-_dot(x2, jnp.concatenate([t2, Z], axis=0), _NT, hi=HI3) for x2, t2 in zip(xs, ts))
    return das, jnp.zeros_like(eye2), jnp.zeros_like(Z)


_tri_inv2.defvjp(_tri_inv2_fwd, _tri_inv2_bwd)


def _gdn_heads(qs, ks, vs, bxs, axs, Ss, alogs, dtbs, cst):
    lower2, strict2, U2, eye2, L, ones, Z = (cst[n] for n in ("lower2", "strict2", "U2", "eye2", "L", "ones", "Z"))
    H = range(len(qs))

    def prod(x2, y):
        return _dot(x2, jnp.concatenate([y, Z], axis=0), hi=HI3)

    qn = [qs[h] * lax.rsqrt(jnp.sum(qs[h] * qs[h], axis=-1, keepdims=True) + NORM_EPS) * (128.0 ** -0.5) for h in H]
    kn = [ks[h] * lax.rsqrt(jnp.sum(ks[h] * ks[h], axis=-1, keepdims=True) + NORM_EPS) for h in H]
    beta = [jax.nn.sigmoid(bxs[h]) for h in H]
    g = [-jnp.exp(alogs[h]) * _softplus(axs[h] + dtbs[h]) for h in H]
    gc = [_dot(L, g[h], hi=HI3) for h in H]
    n2 = [_dot(ones, g[h] * U2, hi=HI3) for h in H]
    decay2 = [jnp.where(lower2, jnp.exp(jnp.where(lower2, gc[h] - n2[h], 0.0)), 0.0) for h in H]
    kb = [kn[h] * beta[h] for h in H]
    kn2 = [jnp.concatenate([kn[h], kn[h]], axis=0) for h in H]
    a2 = tuple(jnp.where(strict2, _dot(kb[h], kn2[h], _NT) * decay2[h], 0.0) for h in H)
    t2 = _tri_inv2(a2, eye2, Z)
    glast = [jnp.sum(g[h], axis=0, keepdims=True) for h in H]
    u = [prod(t2[h], vs[h] * beta[h]) for h in H]
    w = [prod(t2[h], kb[h] * jnp.exp(gc[h])) for h in H]
    attn2 = [jnp.where(lower2, _dot(qn[h], kn2[h], _NT) * decay2[h], 0.0) for h in H]
    k_end = [kn[h] * jnp.exp(glast[h] - gc[h]) for h in H]
    q_start = [qn[h] * jnp.exp(gc[h]) for h in H]
    v_new = [u[h] - _dot(w[h], Ss[h]) for h in H]
    o = [_dot(q_start[h], Ss[h]) + _dot(attn2[h], jnp.concatenate([v_new[h], Z], axis=0)) for h in H]
    s_new = [Ss[h] * jnp.exp(glast[h]) + _dot(k_end[h], v_new[h], _TN) for h in H]
    return tuple(o), tuple(s_new)


def gdn_fwd(qkv, proj, bcol, acol, alog_e, dtb_e, *, name):
    S_ = qkv.shape[0]
    nc = S_ // CH

    def body(q_ref, k_ref, v_ref, b_ref, a_ref, al_ref, dt_ref, o_ref, hist_ref, s_ref):
        i = pl.program_id(0)

        @pl.when(i == 0)
        def _():
            s_ref[...] = jnp.zeros_like(s_ref)

        cst = _chunk_consts()
        hist_ref[0] = s_ref[...]
        heads = [slice(128 * h, 128 * (h + 1)) for h in range(4)]
        rd = lambda ref: tuple(ref[:, ls] for ls in heads)
        os_, s_news = _gdn_heads(rd(q_ref), rd(k_ref), rd(v_ref), rd(b_ref), rd(a_ref), tuple(s_ref[ls, :] for ls in heads),
                                 rd(al_ref), rd(dt_ref), cst)
        for ls, o, s_new in zip(heads, os_, s_news):
            o_ref[:, ls] = o
            s_ref[ls, :] = s_new

    blk = lambda cb: pl.BlockSpec((CH, 512), functools.partial(lambda i, cb: (i, cb), cb=cb))
    par = pl.BlockSpec((1, 512), lambda i: (0, 0))
    return pl.pallas_call(
        body, grid=(nc,), in_specs=[blk(0), blk(1), blk(2), blk(bcol), blk(acol), par, par],
        out_specs=[pl.BlockSpec((CH, 512), lambda i: (i, 0)), pl.BlockSpec((1, 512, 128), lambda i: (i, 0, 0))],
        out_shape=[jax.ShapeDtypeStruct((S_, 512), F32), jax.ShapeDtypeStruct((nc, 512, 128), F32)],
        scratch_shapes=[pltpu.VMEM((512, 128), F32)], compiler_params=_cp(("arbitrary",)), name=name,
    )(qkv, qkv, qkv, proj, proj, alog_e, dtb_e)


def gdn_bwd(qkv, proj, bcol, acol, alog_e, dtb_e, hist, do, *, name):
    S_ = qkv.shape[0]
    nc = S_ // CH

    def body(q_ref, k_ref, v_ref, b_ref, a_ref, al_ref, dt_ref, hist_ref, do_ref, dqkv_ref, db_ref, da_ref, dal_ref, ddt_ref, ds_ref):
        i = pl.program_id(0)

        @pl.when(i == 0)
        def _():
            ds_ref[...] = jnp.zeros_like(ds_ref)
            dal_ref[...] = jnp.zeros_like(dal_ref)
            ddt_ref[...] = jnp.zeros_like(ddt_ref)

        cst = _chunk_consts()
        heads = [slice(128 * h, 128 * (h + 1)) for h in range(4)]
        rd = lambda ref: tuple(ref[:, ls] for ls in heads)
        fn = functools.partial(_gdn_heads, cst=cst)
        _, vjp = jax.vjp(fn, rd(q_ref), rd(k_ref), rd(v_ref), rd(b_ref), rd(a_ref), tuple(hist_ref[0, ls, :] for ls in heads),
                         rd(al_ref), rd(dt_ref))
        grads = vjp((rd(do_ref), tuple(ds_ref[ls, :] for ls in heads)))
        for h in range(4):
            ls = heads[h]
            dq, dk, dv, db, da, ds_in, dal, ddt = (t[h] for t in grads)
            dqkv_ref[:, 128 * h:128 * (h + 1)] = dq
            dqkv_ref[:, 512 + 128 * h:512 + 128 * (h + 1)] = dk
            dqkv_ref[:, 1024 + 128 * h:1024 + 128 * (h + 1)] = dv
            db_ref[:, ls] = db.astype(db_ref.dtype)
            da_ref[:, ls] = da.astype(da_ref.dtype)
            ds_ref[ls, :] = ds_in
            dal_ref[:, ls] += dal
            ddt_ref[:, ls] += ddt

    rblk = lambda cb: pl.BlockSpec((CH, 512), functools.partial(lambda i, cb: (nc - 1 - i, cb), cb=cb))
    par = pl.BlockSpec((1, 512), lambda i: (0, 0))
    return pl.pallas_call(
        body, grid=(nc,),
        in_specs=[rblk(0), rblk(1), rblk(2), rblk(bcol), rblk(acol), par, par,
                  pl.BlockSpec((1, 512, 128), lambda i: (nc - 1 - i, 0, 0)), rblk(0)],
        out_specs=[pl.BlockSpec((CH, 1536), lambda i: (nc - 1 - i, 0)), rblk(0), rblk(0), par, par],
        out_shape=[jax.ShapeDtypeStruct((S_, 1536), F32), jax.ShapeDtypeStruct((S_, 512), BF16), jax.ShapeDtypeStruct((S_, 512), BF16),
                   jax.ShapeDtypeStruct((1, 512), F32), jax.ShapeDtypeStruct((1, 512), F32)],
        scratch_shapes=[pltpu.VMEM((512, 128), F32)], compiler_params=_cp(("arbitrary",)), name=name,
    )(qkv, qkv, qkv, proj, proj, alog_e, dtb_e, hist, do)


def _ssd_pairs(xs, dtxs, bms, cms, hss, nas, dtbs, dsks, cst):
    lower2, U2, L, ones = cst["lower2"], cst["U2"], cst["L"], cst["ones"]
    lane = lax.broadcasted_iota(jnp.int32, (1, 2 * CH), 1)
    mask_l = (lane < CH).astype(F32)
    mask_r = 1.0 - mask_l
    ones_w = jnp.ones((CH, 2 * CH), F32)
    P_ = range(len(xs))
    G_ = range(len(bms))
    per = len(xs) // len(bms)
    cb2 = [_dot(cms[g], jnp.concatenate([bms[g], bms[g]], axis=0), _NT) for g in G_]
    dt = [_softplus(dtxs[p] + dtbs[p]) for p in P_]
    da = [dt[p] * nas[p] for p in P_]
    m = [_dot(L, da[p], hi=HI3) for p in P_]
    n2 = [_dot(ones, da[p] * U2, hi=HI3) for p in P_]
    lm2 = [jnp.where(lower2, jnp.exp(jnp.where(lower2, m[p] - n2[p], 0.0)), 0.0) for p in P_]
    xdt = [xs[p] * dt[p] for p in P_]
    x2 = [jnp.concatenate([xdt[p] * mask_l, xdt[p] * mask_r], axis=0) for p in P_]
    y_diag = [_dot(cb2[p // per] * lm2[p], x2[p]) for p in P_]
    alast = [jnp.sum(da[p], axis=0, keepdims=True) for p in P_]
    y_off = [_dot(cms[p // per], hss[p], _NT) * jnp.exp(m[p]) for p in P_]
    cd = [jnp.exp(_dot(da[p], ones_w, _TN, hi=HI3)) for p in P_]
    hs_new = [hss[p] * cd[p] + _dot(xdt[p] * jnp.exp(alast[p] - m[p]), bms[p // per], _TN) for p in P_]
    ys = [y_diag[p] + y_off[p] + dsks[p] * xs[p] for p in P_]
    return tuple(ys), tuple(hs_new)


def _ssd_specs(nc, rev):
    ci = (lambda i: nc - 1 - i) if rev else (lambda i: i)
    col = lambda w, c: pl.BlockSpec((CH, w), functools.partial(lambda i, c: (ci(i), c), c=c))
    xg = [col(512, g) for g in range(4)]
    dtg = [col(512, 10 + g) for g in range(4)]
    par = pl.BlockSpec((1, 2048), lambda i: (0, 0))
    hist = pl.BlockSpec((1, 2048, 128), lambda i: (ci(i), 0, 0))
    return xg, dtg, col(512, 4), col(512, 5), par, hist, col


def _ssd_read(x_refs, dt_refs, b_ref, c_ref, na_ref, dtb_ref, dsk_ref):
    sl = [slice(128 * p, 128 * (p + 1)) for p in range(4)]
    xs = tuple(x_refs[g][:, s] for g in range(4) for s in sl)
    dts = tuple(dt_refs[g][:, s] for g in range(4) for s in sl)
    bms = tuple(b_ref[:, s] for s in sl)
    cms = tuple(c_ref[:, s] for s in sl)
    lanes = [slice(128 * p, 128 * (p + 1)) for p in range(16)]
    pars = [tuple(r[:, s] for s in lanes) for r in (na_ref, dtb_ref, dsk_ref)]
    return xs, dts, bms, cms, pars, lanes


def ssd_fwd(xbc, proj, na_e, dtb_e, dsk_e, *, name):
    S_ = xbc.shape[0]
    nc = S_ // CH
    xg, dtg, bs, cs, par, hist, _ = _ssd_specs(nc, False)

    def body(*refs):
        x_refs, dt_refs = refs[0:4], refs[4:8]
        b_ref, c_ref, na_ref, dtb_ref, dsk_ref, y_ref, hist_ref, s_ref = refs[8:]
        i = pl.program_id(0)

        @pl.when(i == 0)
        def _():
            s_ref[...] = jnp.zeros_like(s_ref)

        cst = _chunk_consts()
        hist_ref[0] = s_ref[...]
        xs, dts, bms, cms, pars, lanes = _ssd_read(x_refs, dt_refs, b_ref, c_ref, na_ref, dtb_ref, dsk_ref)
        ys, hs_new = _ssd_pairs(xs, dts, bms, cms, tuple(s_ref[s, :] for s in lanes), *pars, cst)
        for p, s in enumerate(lanes):
            y_ref[:, s] = ys[p]
            s_ref[s, :] = hs_new[p]

    return pl.pallas_call(
        body, grid=(nc,), in_specs=xg + dtg + [bs, cs, par, par, par],
        out_specs=[pl.BlockSpec((CH, 2048), lambda i: (i, 0)), hist],
        out_shape=[jax.ShapeDtypeStruct((S_, 2048), F32), jax.ShapeDtypeStruct((nc, 2048, 128), F32)],
        scratch_shapes=[pltpu.VMEM((2048, 128), F32)], compiler_params=_cp(("arbitrary",)), name=name,
    )(xbc, xbc, xbc, xbc, proj, proj, proj, proj, xbc, xbc, na_e, dtb_e, dsk_e)


def ssd_bwd(xbc, proj, na_e, dtb_e, dsk_e, hist, dy, *, name):
    S_ = xbc.shape[0]
    nc = S_ // CH
    xg, dtg, bs, cs, par, hist_spec, col = _ssd_specs(nc, True)
    wide = pl.BlockSpec((CH, 2048), lambda i: (nc - 1 - i, 0))

    def body(*refs):
        x_refs, dt_refs = refs[0:4], refs[4:8]
        (b_ref, c_ref, na_ref, dtb_ref, dsk_ref, hist_ref, dy_ref,
         dx_ref, ddt_ref, db_ref, dc_ref, dna_ref, ddtb_ref, ddsk_ref, ds_ref) = refs[8:]
        i = pl.program_id(0)

        @pl.when(i == 0)
        def _():
            ds_ref[...] = jnp.zeros_like(ds_ref)
            dna_ref[...] = jnp.zeros_like(dna_ref)
            ddtb_ref[...] = jnp.zeros_like(ddtb_ref)
            ddsk_ref[...] = jnp.zeros_like(ddsk_ref)

        cst = _chunk_consts()
        xs, dts, bms, cms, pars, lanes = _ssd_read(x_refs, dt_refs, b_ref, c_ref, na_ref, dtb_ref, dsk_ref)
        fn = functools.partial(_ssd_pairs, cst=cst)
        _, vjp = jax.vjp(fn, xs, dts, bms, cms, tuple(hist_ref[0, s, :] for s in lanes), *pars)
        dxs, ddts, dbs, dcs, dhs, dnas, ddtbs, ddsks = vjp((tuple(dy_ref[:, s] for s in lanes), tuple(ds_ref[s, :] for s in lanes)))
        for g in range(4):
            db_ref[:, 128 * g:128 * (g + 1)] = dbs[g]
            dc_ref[:, 128 * g:128 * (g + 1)] = dcs[g]
        for p, s in enumerate(lanes):
            dx_ref[:, s] = dxs[p]
            ddt_ref[:, s] = ddts[p].astype(ddt_ref.dtype)
            ds_ref[s, :] = dhs[p]
            dna_ref[:, s] += dnas[p]
            ddtb_ref[:, s] += ddtbs[p]
            ddsk_ref[:, s] += ddsks[p]

    half = pl.BlockSpec((CH, 512), lambda i: (nc - 1 - i, 0))
    return pl.pallas_call(
        body, grid=(nc,), in_specs=xg + dtg + [bs, cs, par, par, par, hist_spec, wide],
        out_specs=[wide, wide, half, half, par, par, par],
        out_shape=[jax.ShapeDtypeStruct((S_, 2048), F32), jax.ShapeDtypeStruct((S_, 2048), BF16),
                   jax.ShapeDtypeStruct((S_, 512), F32), jax.ShapeDtypeStruct((S_, 512), F32)] +
                  [jax.ShapeDtypeStruct((1, 2048), F32)] * 3,
        scratch_shapes=[pltpu.VMEM((2048, 128), F32)], compiler_params=_cp(("arbitrary",)), name=name,
    )(xbc, xbc, xbc, xbc, proj, proj, proj, proj, xbc, xbc, na_e, dtb_e, dsk_e, hist, dy)


ATT_T = 1024
ATT_SCALE = 192.0 ** -0.5
NEG = -1e30


def _chunk_mask(shape):
    return lax.broadcasted_iota(jnp.int32, shape, 1) // CH <= lax.broadcasted_iota(jnp.int32, shape, 0) // CH


def _tri_pairs(n, by_row):
    pairs = [(i, j) for i in range(n) for j in range(i + 1)] if by_row else [(i, j) for j in range(n) for i in range(j, n)]
    return jnp.asarray([p[0] for p in pairs], jnp.int32), jnp.asarray([p[1] for p in pairs], jnp.int32)


def att_fwd(q, kv, kp, *, name):
    S_ = q.shape[0]
    T = min(ATT_T, S_)
    n = S_ // T
    ii, jj = _tri_pairs(n, True)
    HP = 2

    def body(ii_ref, jj_ref, q_ref, kn_ref, kp_ref, v_ref, o_ref, lse_ref, m_ref, l_ref, acc_ref):
        t = pl.program_id(1)
        i, j = ii_ref[t], jj_ref[t]

        @pl.when(j == 0)
        def _():
            m_ref[...] = jnp.full_like(m_ref, NEG)
            l_ref[...] = jnp.zeros_like(l_ref)
            acc_ref[...] = jnp.zeros_like(acc_ref)

        H = range(HP)
        hs = [slice(128 * h, 128 * (h + 1)) for h in H]

        def step(diag):
            kp_ = kp_ref[...]
            k2 = [jnp.concatenate([kn_ref[:, hs[h]], kp_], axis=1) for h in H]
            s = [_dot(q_ref[:, 256 * h:256 * (h + 1)], k2[h], _NT) for h in H]
            if diag:
                mask = _chunk_mask(s[0].shape)
                s = [jnp.where(mask, s[h], NEG) for h in H]
            m_prev = [m_ref[:, hs[h]] for h in H]
            m_cur = [jnp.maximum(m_prev[h], jnp.max(s[h], axis=-1, keepdims=True)) for h in H]
            p = [jnp.exp(s[h] - m_cur[h][:, :1]) for h in H]
            alpha = [jnp.exp(m_prev[h] - m_cur[h]) for h in H]
            for h in H:
                l_ref[:, hs[h]] = alpha[h] * l_ref[:, hs[h]] + jnp.sum(p[h], axis=-1, keepdims=True)
            pv = [_dot(p[h], v_ref[:, hs[h]]) for h in H]
            for h in H:
                acc_ref[:, hs[h]] = acc_ref[:, hs[h]] * alpha[h] + pv[h]
                m_ref[:, hs[h]] = m_cur[h]

        @pl.when(j < i)
        def _():
            step(False)

        @pl.when(j == i)
        def _():
            step(True)
            o_ref[...] = acc_ref[...] / l_ref[...]
            lse_ref[...] = m_ref[...] + jnp.log(l_ref[...])

    W = 128 * HP
    grid_spec = pltpu.PrefetchScalarGridSpec(
        num_scalar_prefetch=2, grid=(4 // HP, ii.shape[0]),
        in_specs=[pl.BlockSpec((T, 2 * W), lambda h, t, ii_, jj_: (ii_[t], h)), pl.BlockSpec((T, W), lambda h, t, ii_, jj_: (jj_[t], h)),
                  pl.BlockSpec((T, 128), lambda h, t, ii_, jj_: (jj_[t], 0)),
                  pl.BlockSpec((T, W), lambda h, t, ii_, jj_: (jj_[t], 4 // HP + h))],
        out_specs=[pl.BlockSpec((T, W), lambda h, t, ii_, jj_: (ii_[t], h))] * 2,
        scratch_shapes=[pltpu.VMEM((T, W), F32)] * 3)
    return pl.pallas_call(
        body, grid_spec=grid_spec, out_shape=[jax.ShapeDtypeStruct((S_, 512), F32), jax.ShapeDtypeStruct((S_, 512), F32)],
        compiler_params=_cp(("parallel", "arbitrary")), name=name,
    )(ii, jj, q, kv, kp, kv)


def att_bwd(q, kv, kp, lse, dsum, do, *, name):
    S_ = q.shape[0]
    T = min(ATT_T, S_)
    n = S_ // T
    ii, jj = _tri_pairs(n, False)
    last = ii.shape[0] - 1

    def body(ii_ref, jj_ref, q_ref, kn_ref, kp_ref, v_ref, lse_ref, d_ref, do_ref, dq_hbm, dk_ref, dv_ref, dq_acc, sem):
        h, t = pl.program_id(0), pl.program_id(1)
        i, j = ii_ref[t], jj_ref[t]

        @pl.when(t == 0)
        def _():
            dq_acc[...] = jnp.zeros_like(dq_acc)

        def step(diag):
            k2 = jnp.concatenate([kn_ref[...], kp_ref[...]], axis=1)
            qb = q_ref[...]
            dob = do_ref[...].astype(BF16)
            s = _dot(qb, k2, _NT)
            p = jnp.exp(s - lse_ref[:, :1])
            if diag:
                p = jnp.where(_chunk_mask(s.shape), p, 0.0)
            if diag:
                dv_ref[...] = _dot(p, dob, _TN)
            else:
                dv_ref[...] += _dot(p, dob, _TN)
            ds = (p * (_dot(dob, v_ref[...], _NT) - d_ref[:, :1])).astype(BF16)
            if diag:
                dk_ref[...] = _dot(ds, qb, _TN)
            else:
                dk_ref[...] += _dot(ds, qb, _TN)
            rows = pl.ds(pl.multiple_of(i * T, T), T)
            dq_acc[rows, :] += _dot(ds, k2)

        @pl.when(i > j)
        def _():
            step(False)

        @pl.when(i == j)
        def _():
            step(True)

        @pl.when(t == last)
        def _():
            cp = pltpu.make_async_copy(dq_acc, dq_hbm.at[h], sem)
            cp.start()
            cp.wait()

    qmap = lambda h, t, ii_, jj_: (ii_[t], h)
    grid_spec = pltpu.PrefetchScalarGridSpec(
        num_scalar_prefetch=2, grid=(4, ii.shape[0]),
        in_specs=[pl.BlockSpec((T, 256), qmap), pl.BlockSpec((T, 128), lambda h, t, ii_, jj_: (jj_[t], h)),
                  pl.BlockSpec((T, 128), lambda h, t, ii_, jj_: (jj_[t], 0)), pl.BlockSpec((T, 128), lambda h, t, ii_, jj_: (jj_[t], 4 + h)),
                  pl.BlockSpec((T, 128), qmap), pl.BlockSpec((T, 128), qmap), pl.BlockSpec((T, 128), qmap)],
        out_specs=[pl.BlockSpec(memory_space=pl.ANY), pl.BlockSpec((T, 256), lambda h, t, ii_, jj_: (jj_[t], h)),
                   pl.BlockSpec((T, 128), lambda h, t, ii_, jj_: (jj_[t], h))],
        scratch_shapes=[pltpu.VMEM((S_, 256), F32), pltpu.SemaphoreType.DMA])
    return pl.pallas_call(
        body, grid_spec=grid_spec,
        out_shape=[jax.ShapeDtypeStruct((4, S_, 256), F32), jax.ShapeDtypeStruct((S_, 1024), F32), jax.ShapeDtypeStruct((S_, 512), F32)],
        compiler_params=_cp(("arbitrary", "arbitrary")), name=name,
    )(ii, jj, q, kv, kp, kv, lse, dsum, do)


CONV_T = 1024


def _shift_down(x, halo, s):
    sh = pltpu.roll(x, s, axis=0)
    hr = pltpu.roll(halo, s, axis=0)
    r8 = lax.broadcasted_iota(jnp.int32, hr.shape, 0)
    top = jnp.where(r8 < s, hr, sh[:8])
    return jnp.concatenate([top, sh[8:]], axis=0)


def _shift_up(x, halo, s):
    n = x.shape[0]
    sh = pltpu.roll(x, n - s, axis=0)
    hr = pltpu.roll(halo, 8 - s, axis=0)
    r8 = lax.broadcasted_iota(jnp.int32, hr.shape, 0)
    bot = jnp.where(r8 >= 8 - s, hr, sh[n - 8:])
    return jnp.concatenate([sh[:n - 8], bot], axis=0)


def _conv_pre(x, halo, w, b):
    y = x * w[3:4] + b
    for j in range(3):
        y = y + _shift_down(x, halo, 3 - j) * w[j:j + 1]
    return y


def conv_fwd(src, cb0, ncb, w, b, *, name):
    S_ = src.shape[0]
    T = min(CONV_T, S_)
    nt = S_ // T

    def body(x_ref, h_ref, w_ref, b_ref, o_ref):
        i = pl.program_id(1)
        halo = jnp.where(i > 0, h_ref[...], 0.0)
        y = _conv_pre(x_ref[...], halo, w_ref[...], b_ref[...])
        o_ref[...] = y * jax.nn.sigmoid(y)

    return pl.pallas_call(
        body, grid=(ncb, nt),
        in_specs=[pl.BlockSpec((T, 512), lambda c, i: (i, cb0 + c)),
                  pl.BlockSpec((8, 512), lambda c, i: (jnp.maximum(i * (T // 8) - 1, 0), cb0 + c)),
                  pl.BlockSpec((4, 512), lambda c, i: (0, c)), pl.BlockSpec((1, 512), lambda c, i: (0, c))],
        out_specs=pl.BlockSpec((T, 512), lambda c, i: (i, c)),
        out_shape=jax.ShapeDtypeStruct((S_, 512 * ncb), F32), compiler_params=_cp(("parallel", "parallel")), name=name,
    )(src, src, w, b)


def conv_bwd_pre(src, cb0, ncb, w, b, dy, *, name):
    S_ = src.shape[0]
    T = min(CONV_T, S_)
    nt = S_ // T

    def body(x_ref, h_ref, w_ref, b_ref, dy_ref, dp_ref, dw_ref, db_ref):
        i = pl.program_id(1)
        halo = jnp.where(i > 0, h_ref[...], 0.0)
        x = x_ref[...]
        y = _conv_pre(x, halo, w_ref[...], b_ref[...])
        sg = jax.nn.sigmoid(y)
        dpre = dy_ref[...] * (sg * (1.0 + y * (1.0 - sg)))
        dp_ref[...] = dpre
        rows = [jnp.sum(dpre * _shift_down(x, halo, 3 - j), axis=0, keepdims=True) for j in range(3)]
        rows.append(jnp.sum(dpre * x, axis=0, keepdims=True))
        dw = jnp.concatenate(rows, axis=0)
        db = jnp.sum(dpre, axis=0, keepdims=True)

        @pl.when(i == 0)
        def _():
            dw_ref[...] = dw
            db_ref[...] = db

        @pl.when(i > 0)
        def _():
            dw_ref[...] += dw
            db_ref[...] += db

    return pl.pallas_call(
        body, grid=(ncb, nt),
        in_specs=[pl.BlockSpec((T, 512), lambda c, i: (i, cb0 + c)),
                  pl.BlockSpec((8, 512), lambda c, i: (jnp.maximum(i * (T // 8) - 1, 0), cb0 + c)),
                  pl.BlockSpec((4, 512), lambda c, i: (0, c)), pl.BlockSpec((1, 512), lambda c, i: (0, c)),
                  pl.BlockSpec((T, 512), lambda c, i: (i, c))],
        out_specs=[pl.BlockSpec((T, 512), lambda c, i: (i, c)), pl.BlockSpec((4, 512), lambda c, i: (0, c)),
                   pl.BlockSpec((1, 512), lambda c, i: (0, c))],
        out_shape=[jax.ShapeDtypeStruct((S_, 512 * ncb), F32), jax.ShapeDtypeStruct((4, 512 * ncb), F32),
                   jax.ShapeDtypeStruct((1, 512 * ncb), F32)],
        compiler_params=_cp(("parallel", "arbitrary")), name=name,
    )(src, src, w, b, dy)


def conv_bwd_x(dpre, w, *, name, out_dtype=F32):
    S_, C = dpre.shape
    T = min(CONV_T, S_)
    nt = S_ // T
    ncb = C // 512

    def body(d_ref, h_ref, w_ref, o_ref):
        i = pl.program_id(1)
        halo = jnp.where(i < nt - 1, h_ref[...], 0.0)
        d = d_ref[...]
        w_ = w_ref[...]
        y = d * w_[3:4]
        for j in range(3):
            y = y + _shift_up(d, halo, 3 - j) * w_[j:j + 1]
        o_ref[...] = y.astype(o_ref.dtype)

    return pl.pallas_call(
        body, grid=(ncb, nt),
        in_specs=[pl.BlockSpec((T, 512), lambda c, i: (i, c)),
                  pl.BlockSpec((8, 512), lambda c, i: (jnp.minimum((i + 1) * (T // 8), S_ // 8 - 1), c)),
                  pl.BlockSpec((4, 512), lambda c, i: (0, c))],
        out_specs=pl.BlockSpec((T, 512), lambda c, i: (i, c)),
        out_shape=jax.ShapeDtypeStruct((S_, C), out_dtype), compiler_params=_cp(("parallel", "parallel")), name=name,
    )(dpre, dpre, w)


def ffn_mid_fwd(h, w1, w3, *, name):
    S_, D = h.shape
    F = w1.shape[1]
    tm, tn = _pick(S_, (2048, 1024, 512, 256)), 256

    def body(h_ref, w1_ref, w3_ref, a_ref, u_ref, v_ref):
        hb = h_ref[...]
        u = _dot(hb, w1_ref[...])
        v = _dot(hb, w3_ref[...])
        a_ref[...] = (u * jax.nn.sigmoid(u) * v).astype(a_ref.dtype)
        u_ref[...] = u.astype(u_ref.dtype)
        v_ref[...] = v.astype(v_ref.dtype)

    o = pl.BlockSpec((tm, tn), lambda i, j: (i, j))
    return pl.pallas_call(
        body, grid=(S_ // tm, F // tn),
        in_specs=[pl.BlockSpec((tm, D), lambda i, j: (i, 0)), pl.BlockSpec((D, tn), lambda i, j: (0, j)),
                  pl.BlockSpec((D, tn), lambda i, j: (0, j))],
        out_specs=[o, o, o], out_shape=[jax.ShapeDtypeStruct((S_, F), BF16)] * 3,
        compiler_params=_cp(("parallel", "parallel")), name=name,
    )(h, w1, w3)


def ffn_mid_bwd(u, v, dy, w2, *, name):
    S_, F = u.shape
    D = dy.shape[1]
    tm, tn = _pick(S_, (2048, 1024, 512, 256)), 256

    def body(u_ref, v_ref, dy_ref, w2_ref, du_ref, dv_ref):
        u_ = u_ref[...].astype(F32)
        v_ = v_ref[...].astype(F32)
        da = _dot(dy_ref[...], w2_ref[...], _NT)
        sg = jax.nn.sigmoid(u_)
        dv_ref[...] = (da * (u_ * sg)).astype(dv_ref.dtype)
        du_ref[...] = (da * v_ * (sg * (1.0 + u_ * (1.0 - sg)))).astype(du_ref.dtype)

    o = pl.BlockSpec((tm, tn), lambda i, j: (i, j))
    return pl.pallas_call(
        body, grid=(S_ // tm, F // tn),
        in_specs=[o, o, pl.BlockSpec((tm, D), lambda i, j: (i, 0)), pl.BlockSpec((tn, D), lambda i, j: (j, 0))],
        out_specs=[o, o], out_shape=[jax.ShapeDtypeStruct((S_, F), BF16)] * 2,
        compiler_params=_cp(("parallel", "parallel")), name=name,
    )(u, v, dy, w2)


MESH = pl.DeviceIdType.MESH
ANY = pl.BlockSpec(memory_space=pl.ANY)


def allgather8(x_shard, *, name):
    m_per, n = x_shard.shape

    def body(x_ref, out_ref, send_sems, recv_sems, local_sem):
        x, y, c = lax.axis_index("x"), lax.axis_index("y"), lax.axis_index("c")
        me, sibling = (x, y, c), (x, y, 1 - c)
        chips = [(1 - x, y), (x, 1 - y), (1 - x, 1 - y)]

        def rows(px, py, pc):
            return out_ref.at[pl.ds((4 * px + 2 * py + pc) * m_per, m_per), :]

        def copy(k, block, to, src=None):
            return pltpu.make_async_remote_copy(
                src_ref=rows(*block) if src is None else src, dst_ref=rows(*block),
                send_sem=send_sems.at[k], recv_sem=recv_sems.at[k], device_id=to, device_id_type=MESH)

        mine = pltpu.make_async_copy(x_ref, rows(*me), local_sem)
        mine.start()
        first = [copy(0, me, sibling, src=x_ref)]
        first += [copy(1 + j, me, (*chip, c), src=x_ref) for j, chip in enumerate(chips)]
        for cp in first:
            cp.start()
        passed = [copy(4 + j, (*chip, c), sibling) for j, chip in enumerate(chips)]
        for j, chip in enumerate(chips):
            copy(1 + j, (*chip, c), me).wait_recv()
            passed[j].start()
        copy(0, sibling, me).wait_recv()
        for j, chip in enumerate(chips):
            copy(4 + j, (*chip, 1 - c), me).wait_recv()
        for cp in first + passed:
            cp.wait_send()
        mine.wait()

    return pl.pallas_call(
        body, out_shape=jax.ShapeDtypeStruct((8 * m_per, n), x_shard.dtype),
        in_specs=[pl.BlockSpec(memory_space=pltpu.VMEM)], out_specs=pl.BlockSpec(memory_space=pltpu.VMEM),
        scratch_shapes=[pltpu.SemaphoreType.DMA((7,)), pltpu.SemaphoreType.DMA((7,)), pltpu.SemaphoreType.DMA],
        name=name,
    )(x_shard)


def _chip_peers():
    x, y, c = lax.axis_index("x"), lax.axis_index("y"), lax.axis_index("c")
    return x, y, c, [(1 - x, y), (x, 1 - y), (1 - x, 1 - y)]


def allgather_chips(x_shard, *, name):
    r, cdim = x_shard.shape

    def body(x_ref, out_ref, send_sems, recv_sems, local_sem):
        x, y, c, chips = _chip_peers()
        me = 2 * x + y
        mine = pltpu.make_async_copy(x_ref, out_ref.at[me], local_sem)
        mine.start()
        sends = []
        for k, (px, py) in enumerate(chips):
            cp = pltpu.make_async_remote_copy(src_ref=x_ref, dst_ref=out_ref.at[me], send_sem=send_sems.at[k],
                                              recv_sem=recv_sems.at[k], device_id=(px, py, c), device_id_type=MESH)
            cp.start()
            sends.append(cp)
        for k, (px, py) in enumerate(chips):
            pltpu.make_async_remote_copy(src_ref=x_ref, dst_ref=out_ref.at[2 * px + py], send_sem=send_sems.at[k],
                                         recv_sem=recv_sems.at[k], device_id=(px, py, c), device_id_type=MESH).wait_recv()
        for cp in sends:
            cp.wait_send()
        mine.wait()

    return pl.pallas_call(
        body, out_shape=jax.ShapeDtypeStruct((4, r, cdim), x_shard.dtype), in_specs=[ANY], out_specs=ANY,
        scratch_shapes=[pltpu.SemaphoreType.DMA((3,)), pltpu.SemaphoreType.DMA((3,)), pltpu.SemaphoreType.DMA],
        name=name,
    )(x_shard)


def allgather_chips_2level(x_shard, *, name):
    r, cdim = x_shard.shape
    half = r // 2

    def body(x_ref, out_ref, send_sems, recv_sems, local_sem):
        x, y, c, chips = _chip_peers()
        me = 2 * x + y
        mine_rows = pl.ds(c * half, half)
        other_rows = pl.ds((1 - c) * half, half)
        mine = pltpu.make_async_copy(x_ref, out_ref.at[me], local_sem)
        mine.start()

        def copy(k, slot, rows, to, src=None):
            dst = out_ref.at[slot, rows, :]
            return pltpu.make_async_remote_copy(src_ref=dst if src is None else src, dst_ref=dst, send_sem=send_sems.at[k],
                                                recv_sem=recv_sems.at[k], device_id=to, device_id_type=MESH)

        first = [copy(k, me, mine_rows, (px, py, c), src=x_ref.at[mine_rows, :]) for k, (px, py) in enumerate(chips)]
        for cp in first:
            cp.start()
        passed = [copy(3 + k, 2 * px + py, mine_rows, (x, y, 1 - c)) for k, (px, py) in enumerate(chips)]
        for k, (px, py) in enumerate(chips):
            copy(k, 2 * px + py, mine_rows, (px, py, c)).wait_recv()
            passed[k].start()
        for k, (px, py) in enumerate(chips):
            copy(3 + k, 2 * px + py, other_rows, (x, y, 1 - c)).wait_recv()
        for cp in first + passed:
            cp.wait_send()
        mine.wait()

    return pl.pallas_call(
        body, out_shape=jax.ShapeDtypeStruct((4, r, cdim), x_shard.dtype), in_specs=[ANY], out_specs=ANY,
        scratch_shapes=[pltpu.SemaphoreType.DMA((6,)), pltpu.SemaphoreType.DMA((6,)), pltpu.SemaphoreType.DMA],
        name=name,
    )(x_shard)


def exchange_chips(g, *, name):
    _, r, cdim = g.shape

    def body(g_ref, out_ref, send_sems, recv_sems, local_sem):
        x, y, c, chips = _chip_peers()
        me = 2 * x + y
        mine = pltpu.make_async_copy(g_ref.at[me], out_ref.at[me], local_sem)
        mine.start()
        sends = []
        for k, (px, py) in enumerate(chips):
            cp = pltpu.make_async_remote_copy(src_ref=g_ref.at[2 * px + py], dst_ref=out_ref.at[me], send_sem=send_sems.at[k],
                                              recv_sem=recv_sems.at[k], device_id=(px, py, c), device_id_type=MESH)
            cp.start()
            sends.append(cp)
        for k, (px, py) in enumerate(chips):
            pltpu.make_async_remote_copy(src_ref=g_ref.at[me], dst_ref=out_ref.at[2 * px + py], send_sem=send_sems.at[k],
                                         recv_sem=recv_sems.at[k], device_id=(px, py, c), device_id_type=MESH).wait_recv()
        for cp in sends:
            cp.wait_send()
        mine.wait()

    return pl.pallas_call(
        body, out_shape=jax.ShapeDtypeStruct(g.shape, g.dtype), in_specs=[ANY], out_specs=ANY,
        scratch_shapes=[pltpu.SemaphoreType.DMA((3,)), pltpu.SemaphoreType.DMA((3,)), pltpu.SemaphoreType.DMA],
        name=name,
    )(g)


def swap_sibling(p, *, name):
    def body(p_ref, out_ref, send_sem, recv_sem):
        x, y, c = lax.axis_index("x"), lax.axis_index("y"), lax.axis_index("c")
        cp = pltpu.make_async_remote_copy(src_ref=p_ref, dst_ref=out_ref, send_sem=send_sem, recv_sem=recv_sem,
                                          device_id=(x, y, 1 - c), device_id_type=MESH)
        cp.start()
        cp.wait()

    return pl.pallas_call(
        body, out_shape=jax.ShapeDtypeStruct(p.shape, p.dtype), in_specs=[ANY], out_specs=ANY,
        scratch_shapes=[pltpu.SemaphoreType.DMA, pltpu.SemaphoreType.DMA], name=name,
    )(p)


def swap_other_half(g, *, name):
    n, r, cdim = g.shape
    half = r // 2

    def body(g_ref, out_ref, send_sem, recv_sem):
        x, y, c = lax.axis_index("x"), lax.axis_index("y"), lax.axis_index("c")
        cp = pltpu.make_async_remote_copy(src_ref=g_ref.at[:, pl.ds((1 - c) * half, half), :], dst_ref=out_ref, send_sem=send_sem,
                                          recv_sem=recv_sem, device_id=(x, y, 1 - c), device_id_type=MESH)
        cp.start()
        cp.wait()

    return pl.pallas_call(
        body, out_shape=jax.ShapeDtypeStruct((n, half, cdim), g.dtype), in_specs=[ANY], out_specs=ANY,
        scratch_shapes=[pltpu.SemaphoreType.DMA, pltpu.SemaphoreType.DMA], name=name,
    )(g)


def add_pairs(a, b, *, name, out_dtype):
    n, rows, cdim = a.shape
    t = _pick(rows, (256, 128, 64, 32, 16))

    def body(a_ref, b_ref, o_ref):
        o_ref[...] = (a_ref[...].astype(F32) + b_ref[...].astype(F32)).astype(o_ref.dtype)

    spec = pl.BlockSpec((n, t, cdim), lambda i: (0, i, 0))
    return pl.pallas_call(
        body, grid=(rows // t,), in_specs=[spec, spec], out_specs=spec, out_shape=jax.ShapeDtypeStruct(a.shape, out_dtype),
        compiler_params=_cp(("parallel",)), name=name,
    )(a, b)


def sum_slots(r, *, name):
    n, rows, cdim = r.shape
    t = _pick(rows, (256, 128, 64, 32, 16, 8))

    def body(r_ref, o_ref):
        acc = r_ref[0].astype(F32)
        for s in range(1, n):
            acc = acc + r_ref[s].astype(F32)
        o_ref[...] = acc

    return pl.pallas_call(
        body, grid=(rows // t,), in_specs=[pl.BlockSpec((n, t, cdim), lambda i: (0, i, 0))],
        out_specs=pl.BlockSpec((t, cdim), lambda i: (i, 0)), out_shape=jax.ShapeDtypeStruct((rows, cdim), F32),
        compiler_params=_cp(("parallel",)), name=name,
    )(r)


def _rms(x, g):
    return x * lax.rsqrt(jnp.mean(x * x, axis=-1, keepdims=True) + NORM_EPS) * g


def _adaln(x, g, shift, scale):
    return _rms(x, g) * (1.0 + scale) + shift


def _silu(x):
    return x * jax.nn.sigmoid(x)


def _gdn_gate(o, z, g):
    return jnp.concatenate([_rms(o[:, 128 * h:128 * (h + 1)], g) * _silu(z[:, 128 * h:128 * (h + 1)]) for h in range(4)], axis=1)


def _ssd_gate(y, z0, z1, z2, z3, g):
    outs = []
    for k, z in enumerate((z0, z1, z2, z3)):
        t = y[:, 512 * k:512 * (k + 1)] * _silu(z)
        outs.append(t * lax.rsqrt(jnp.mean(t * t, axis=-1, keepdims=True) + NORM_EPS))
    return jnp.concatenate(outs, axis=1) * g


def _rope(x, cos, sin, rot):
    return x * cos + _dot(x, rot, hi=True) * sin


def _rope_q(q, cos, sin, rot):
    parts = []
    for h in range(4):
        parts += [q[:, 256 * h:256 * h + 128], _rope(q[:, 256 * h + 128:256 * (h + 1)], cos, sin, rot)]
    return jnp.concatenate(parts, axis=1) * ATT_SCALE


def _rope_t(d, cos, sin, rot):
    return d * cos + _dot(d * sin, rot, _NT, hi=True)


def _vjp_rows(fn, n_rows, n_pars, out_dtypes, rows, cts, pars, *, name, tile=1024, extra=None):
    nct = len(cts)

    def bwd(*a):
        r, c, e, p = a[:n_rows], a[n_rows:n_rows + nct], a[n_rows + nct:len(a) - n_pars], a[len(a) - n_pars:]
        out, vjp = jax.vjp(fn, *[t.astype(F32) for t in r], *p)
        ct = tuple(t.astype(F32) for t in c)
        grads = vjp(ct[0] if not isinstance(out, tuple) else ct)
        drows = list(grads[:n_rows])
        if e:
            drows[0] = drows[0] + e[0]
        return (*drows, *grads[n_rows:])

    return rowmap(bwd, list(rows) + list(cts) + ([extra] if extra is not None else []), list(pars), out_dtypes,
                  name=name, tile=tile, n_reduce=n_pars)


def _gate_grads(dw_raw, w, gate, scale, *, name):
    dw, dg = rowmap(lambda r, w_, gt: ((scale * gt) * r, jnp.sum((scale * w_.astype(F32)) * r, axis=0, keepdims=True)),
                    [dw_raw, w], [gate], (F32, F32), name=name, tile=256, n_reduce=1)
    return dw, dg[0]


ADAM_LR, ADAM_B1, ADAM_B2, ADAM_EPS, ADAM_WD, ADAM_STEP = 0.001, 0.9, 0.999, 1e-08, 0.01, 10


def _adam_math(w, g, m, v):
    m = ADAM_B1 * m + (1.0 - ADAM_B1) * g
    v = ADAM_B2 * v + (1.0 - ADAM_B2) * (g * g)
    m_hat = m / (1.0 - ADAM_B1 ** ADAM_STEP)
    v_hat = v / (1.0 - ADAM_B2 ** ADAM_STEP)
    delta = -ADAM_LR * (m_hat / (jnp.sqrt(v_hat) + ADAM_EPS) + ADAM_WD * w)
    return delta, m, v


def adamw(w, gs, m, v, *, name):
    shape = w.shape
    last = shape[-1]
    to2 = lambda a: a.reshape(-1, last)
    rows = w.size // last
    tile = _pick(rows, (256, 128, 64, 32, 16, 8))
    ng = len(gs)

    def fn(w_, *rest):
        g = rest[0]
        for t in rest[1:ng]:
            g = g + t
        m_, v_ = rest[ng], rest[ng + 1]
        return (g, *_adam_math(w_, g, m_, v_))

    outs = rowmap(fn, [to2(w)] + [to2(g) for g in gs] + [to2(m), to2(v)], [], (F32,) * 4, name=name, tile=tile)
    return tuple(o.reshape(shape) for o in outs)


PACK_W = 1024
BIG = (
    ("ffn_w1", (4, 2, 1024, 704), 3), ("ffn_w3", (4, 2, 1024, 704), 3), ("ffn_w2", (4, 2, 704, 1024), 2),
    ("ev_w_in", (2, 1024, 690), 2), ("mla_w_uq", (2, 96, 4, 192), 1), ("mla_w_ukv", (2, 64, 4, 256), 1),
    ("ev_w_out", (2, 256, 1024), 1), ("ssd_w_in", (2, 1024, 1288), 2), ("ssd_w_out", (2, 512, 1024), 1))


def _seg_rows(shape):
    n = math.prod(shape)
    return -(-n // (16 * PACK_W)) * 16


PACK_ROWS = -(-sum(_seg_rows(sh) for _, sh, _ in BIG) // 512) * 512


def _pack(shards, dtype):
    parts = []
    for (_, shape, _), a in zip(BIG, shards):
        flat = a.reshape(-1).astype(dtype)
        pad = _seg_rows(shape) * PACK_W - flat.shape[0]
        parts.append(jnp.pad(flat, (0, pad)) if pad else flat)
    tail = PACK_ROWS - sum(_seg_rows(sh) for _, sh, _ in BIG)
    if tail:
        parts.append(jnp.zeros((tail * PACK_W,), dtype))
    return jnp.concatenate(parts).reshape(-1, PACK_W)


def _pack_by_owner(fulls):
    cols = []
    for (_, shape, ax), f in zip(BIG, fulls):
        blk = jnp.stack([lax.slice_in_dim(f, s * shape[ax], (s + 1) * shape[ax], axis=ax).reshape(-1).astype(BF16) for s in range(4)])
        pad = _seg_rows(shape) * PACK_W - blk.shape[1]
        cols.append((jnp.pad(blk, ((0, 0), (0, pad))) if pad else blk).reshape(4, -1, PACK_W))
    tail = PACK_ROWS - sum(_seg_rows(sh) for _, sh, _ in BIG)
    if tail:
        cols.append(jnp.zeros((4, tail, PACK_W), BF16))
    return jnp.concatenate(cols, axis=1)


def _unpack(buf):
    out, r0 = [], 0
    for _, shape, _ in BIG:
        n = math.prod(shape)
        out.append(buf[r0:r0 + _seg_rows(shape)].reshape(-1)[:n].reshape(shape))
        r0 += _seg_rows(shape)
    return out


SMALL_SHARDED = (
    ("norm_g", (4, 3, 256), 2), ("gdn_conv_w", (2, 4, 384), 2), ("ssd_conv_w", (2, 4, 768), 2),
    ("ssd_conv_b", (2, 768), 1), ("ssd_norm_g", (2, 512), 1))


def _flat_pack(arrs, width, row_mult):
    flat = jnp.concatenate([a.reshape(-1).astype(F32) for a in arrs])
    n = flat.shape[0]
    tot = -(-n // (width * row_mult)) * width * row_mult
    return jnp.pad(flat, (0, tot - n)).reshape(-1, width)


def _flat_unpack(buf, shapes):
    flat = buf.reshape(-1)
    out, o = [], 0
    for s in shapes:
        n = math.prod(s)
        out.append(flat[o:o + n].reshape(s))
        o += n
    return out


def _rep(v, n):
    return jnp.repeat(v, n, axis=-1)


def kernel(x, c, positions, ada_w, ada_b, norm_g, ffn_w1, ffn_w3, ffn_w2, ev_w_in, gdn_conv_w, gdn_A_log, gdn_dt_bias, gdn_norm_g, mla_q_norm_g, mla_w_uq, mla_kv_norm_g, mla_w_ukv, ev_w_out, ssd_w_in, ssd_conv_w, ssd_conv_b, ssd_A_log, ssd_dt_bias, ssd_D, ssd_norm_g, ssd_w_out, final_g, loss_target, m_ada_w, m_ada_b, m_norm_g, m_ffn_w1, m_ffn_w3, m_ffn_w2, m_ev_w_in, m_gdn_conv_w, m_gdn_A_log, m_gdn_dt_bias, m_gdn_norm_g, m_mla_q_norm_g, m_mla_w_uq, m_mla_kv_norm_g, m_mla_w_ukv, m_ev_w_out, m_ssd_w_in, m_ssd_conv_w, m_ssd_conv_b, m_ssd_A_log, m_ssd_dt_bias, m_ssd_D, m_ssd_norm_g, m_ssd_w_out, m_final_g, v_ada_w, v_ada_b, v_norm_g, v_ffn_w1, v_ffn_w3, v_ffn_w2, v_ev_w_in, v_gdn_conv_w, v_gdn_A_log, v_gdn_dt_bias, v_gdn_norm_g, v_mla_q_norm_g, v_mla_w_uq, v_mla_kv_norm_g, v_mla_w_ukv, v_ev_w_out, v_ssd_w_in, v_ssd_conv_w, v_ssd_conv_b, v_ssd_A_log, v_ssd_dt_bias, v_ssd_D, v_ssd_norm_g, v_ssd_w_out, v_final_g):
    P = dict(ada_w=ada_w, ada_b=ada_b, norm_g=norm_g, ffn_w1=ffn_w1, ffn_w3=ffn_w3, ffn_w2=ffn_w2, ev_w_in=ev_w_in, gdn_conv_w=gdn_conv_w, gdn_A_log=gdn_A_log, gdn_dt_bias=gdn_dt_bias, gdn_norm_g=gdn_norm_g, mla_q_norm_g=mla_q_norm_g, mla_w_uq=mla_w_uq, mla_kv_norm_g=mla_kv_norm_g, mla_w_ukv=mla_w_ukv, ev_w_out=ev_w_out, ssd_w_in=ssd_w_in, ssd_conv_w=ssd_conv_w, ssd_conv_b=ssd_conv_b, ssd_A_log=ssd_A_log, ssd_dt_bias=ssd_dt_bias, ssd_D=ssd_D, ssd_norm_g=ssd_norm_g, ssd_w_out=ssd_w_out, final_g=final_g)
    M1 = dict(ada_w=m_ada_w, ada_b=m_ada_b, norm_g=m_norm_g, ffn_w1=m_ffn_w1, ffn_w3=m_ffn_w3, ffn_w2=m_ffn_w2, ev_w_in=m_ev_w_in, gdn_conv_w=m_gdn_conv_w, gdn_A_log=m_gdn_A_log, gdn_dt_bias=m_gdn_dt_bias, gdn_norm_g=m_gdn_norm_g, mla_q_norm_g=m_mla_q_norm_g, mla_w_uq=m_mla_w_uq, mla_kv_norm_g=m_mla_kv_norm_g, mla_w_ukv=m_mla_w_ukv, ev_w_out=m_ev_w_out, ssd_w_in=m_ssd_w_in, ssd_conv_w=m_ssd_conv_w, ssd_conv_b=m_ssd_conv_b, ssd_A_log=m_ssd_A_log, ssd_dt_bias=m_ssd_dt_bias, ssd_D=m_ssd_D, ssd_norm_g=m_ssd_norm_g, ssd_w_out=m_ssd_w_out, final_g=m_final_g)
    M2 = dict(ada_w=v_ada_w, ada_b=v_ada_b, norm_g=v_norm_g, ffn_w1=v_ffn_w1, ffn_w3=v_ffn_w3, ffn_w2=v_ffn_w2, ev_w_in=v_ev_w_in, gdn_conv_w=v_gdn_conv_w, gdn_A_log=v_gdn_A_log, gdn_dt_bias=v_gdn_dt_bias, gdn_norm_g=v_gdn_norm_g, mla_q_norm_g=v_mla_q_norm_g, mla_w_uq=v_mla_w_uq, mla_kv_norm_g=v_mla_kv_norm_g, mla_w_ukv=v_mla_w_ukv, ev_w_out=v_ev_w_out, ssd_w_in=v_ssd_w_in, ssd_conv_w=v_ssd_conv_w, ssd_conv_b=v_ssd_conv_b, ssd_A_log=v_ssd_A_log, ssd_dt_bias=v_ssd_dt_bias, ssd_D=v_ssd_D, ssd_norm_g=v_ssd_norm_g, ssd_w_out=v_ssd_w_out, final_g=v_final_g)
    names = list(P)
    xi, yi, ci = lax.axis_index("x"), lax.axis_index("y"), lax.axis_index("c")
    chip = 2 * xi + yi
    bidx = 4 * xi + 2 * yi + ci
    xa = x[0]
    S_, D = xa.shape
    tgt = loss_target[0]
    depth = ffn_w1.shape[0]

    wg = allgather_chips_2level(_pack([P[n] for n, _, _ in BIG], BF16), name="gather_weights")
    per_chip = [_unpack(wg[s]) for s in range(4)]
    W = {n: jnp.concatenate([per_chip[s][k] for s in range(4)], axis=ax) for k, (n, _, ax) in enumerate(BIG)}
    sg = allgather_chips(_flat_pack([P[n] for n, _, _ in SMALL_SHARDED], 1024, 16), name="gather_small")
    per_chip_s = [_flat_unpack(sg[s], [sh for _, sh, _ in SMALL_SHARDED]) for s in range(4)]
    Wf = {n: jnp.concatenate([per_chip_s[s][k] for s in range(4)], axis=ax) for k, (n, _, ax) in enumerate(SMALL_SHARDED)}

    c_all = allgather8(jnp.pad(c, ((0, 7), (0, 0))), name="gather_c").reshape(8, 8, D)[:, 0]
    c_act, = rowmap(lambda t: (_silu(t),), [jnp.pad(c_all, ((0, 8), (0, 0)))], [], (F32,), name="c_act", tile=16)
    ncol = ada_w.shape[2]
    ada_b_loc = lax.dynamic_slice(ada_b, (0, chip * ncol), (depth, ncol))
    mod_loc = [mm((c_act, ada_w[l]), name=f"mod_{l}", epi=lambda acc, b: (acc + b,), epi_pars=(ada_b_loc[l][None],),
                  epi_out_dtypes=(F32,), tm=16, tn=256)[0][:8] for l in range(depth)]
    mod_g = allgather8(jnp.stack(mod_loc).reshape(-1, 1024), name="gather_mod").reshape(8, depth, 8, ncol)
    mod_b = lax.dynamic_index_in_dim(mod_g[0::2], bidx, axis=2, keepdims=False)
    mod = jnp.transpose(mod_b, (1, 0, 2)).reshape(depth, 3, 3, D)

    def ev_ext(w):
        z = lambda n: jnp.zeros((w.shape[0], n), w.dtype)
        return jnp.concatenate([w[:, :2048], _rep(w[:, 2048:2052], 128), _rep(w[:, 2052:2056], 128), w[:, 2056:2440], z(128),
                                w[:, 2440:2696], w[:, 2696:2760], z(192)], axis=1)

    def ev_ext_t(dw):
        return jnp.concatenate([dw[:, :2048], dw[:, 2048:2560].reshape(-1, 4, 128).sum(-1), dw[:, 2560:3072].reshape(-1, 4, 128).sum(-1),
                                dw[:, 3072:3456], dw[:, 3584:3840], dw[:, 3840:3904]], axis=1)

    def od_ext(w):
        return jnp.concatenate([w[:, 2048:5120], w[:, :2048], _rep(w[:, 5120:5152], 64)], axis=1)

    def od_ext_t(dw):
        return jnp.concatenate([dw[:, 3072:5120], dw[:, :3072], dw[:, 5120:].reshape(-1, 32, 64).sum(-1)], axis=1)

    def wq_ext(w):
        return jnp.pad(w, ((0, 0), (0, 0), (0, 64))).reshape(384, 1024)

    def wq_ext_t(dw):
        return dw.reshape(384, 4, 256)[:, :, :192]

    def wkv_ext(w):
        return jnp.concatenate([w[:, :, :128].reshape(256, 512), w[:, :, 128:].reshape(256, 512)], axis=1)

    def wkv_ext_t(dw):
        return jnp.concatenate([dw[:, :512].reshape(256, 4, 128), dw[:, 512:].reshape(256, 4, 128)], axis=2)

    half = 32
    inv_freq = 10000.0 ** (-jnp.arange(half, dtype=F32) / half)
    ang = positions[0].astype(F32)[:, None] * inv_freq
    zpad = jnp.zeros((S_, 64), F32)
    cos_t = jnp.concatenate([jnp.cos(ang), jnp.cos(ang), zpad], axis=1)
    sin_t = jnp.concatenate([jnp.sin(ang), jnp.sin(ang), zpad], axis=1)
    ii = jnp.arange(128)
    rot = (jnp.where((ii[:, None] < 32) & (ii[None, :] == ii[:, None] + 32), 1.0, 0.0)
           - jnp.where((ii[:, None] >= 32) & (ii[:, None] < 64) & (ii[None, :] == ii[:, None] - 32), 1.0, 0.0)).astype(F32)

    grads = {}
    dmod = [[[None] * 3 for _ in range(3)] for _ in range(depth)]
    dnorm_g = [[None] * 3 for _ in range(depth)]

    def acc(name, idx, val):
        grads.setdefault(name, {})[idx] = val

    def ffn_sub(xin, l, k, j):
        g, (shift, scale, gate) = Wf["norm_g"][l, k][None], [mod[l, k, t][None] for t in range(3)]
        w1, w3, w2 = W["ffn_w1"][l, j], W["ffn_w3"][l, j], W["ffn_w2"][l, j]
        tag = f"l{l}f{j}"
        h, = rowmap(lambda *a: (_adaln(*a),), [xin], [g, shift, scale], (BF16,), name=f"adaln_{tag}", tile=1024)
        a, u16, v16 = ffn_mid_fwd(h, w1, w3, name=f"ffn_mid_{tag}")
        xn, = mm((a, w2), name=f"ffn_out_{tag}", epi=lambda acc_, xr, gt: (xr + 0.5 * gt * acc_,), epi_rows=(xin,),
                 epi_pars=(gate,), epi_out_dtypes=(F32,))

        def bwd(dxn):
            du, dv = ffn_mid_bwd(u16, v16, dxn, ((0.5 * gate) * w2).astype(BF16), name=f"ffn_midb_{tag}")
            dh = mm([(du, w1), (dv, w3)], tb=True, name=f"ffn_dh_{tag}")
            acc("ffn_w1", (l, j), mm((h, du), ta=True, name=f"ffn_dw1_{tag}"))
            acc("ffn_w3", (l, j), mm((h, dv), ta=True, name=f"ffn_dw3_{tag}"))
            dw2, dgate = _gate_grads(mm((a, dxn), ta=True, name=f"ffn_dw2_{tag}"), w2, gate, 0.5, name=f"ffn_dgate_{tag}")
            acc("ffn_w2", (l, j), dw2)
            dx, dg, dsh, dsc = _vjp_rows(_adaln, 1, 3, (F32,), [xin], [dh], [g, shift, scale], name=f"adalnb_{tag}", extra=dxn, tile=1024)
            dnorm_g[l][k] = dg[0]
            dmod[l][k] = [dsh[0], dsc[0], dgate]
            return dx

        return xn, bwd

    def mixer_tail(xin, l, tag, dh, dxn, g, shift, scale, dgate):
        dx, dg, dsh, dsc = _vjp_rows(_adaln, 1, 3, (F32,), [xin], [dh], [g, shift, scale], name=f"adalnb_{tag}", extra=dxn, tile=1024)
        dnorm_g[l][1] = dg[0]
        dmod[l][1] = [dsh[0], dsc[0], dgate]
        return dx

    def even_sub(xin, l):
        e = l // 2
        tag = f"l{l}m"
        g, (shift, scale, gate) = Wf["norm_g"][l, 1][None], [mod[l, 1, t][None] for t in range(3)]
        wext, wq, wkv, wout = ev_ext(W["ev_w_in"][e]), wq_ext(W["mla_w_uq"][e]), wkv_ext(W["mla_w_ukv"][e]), W["ev_w_out"][e]
        conv_w, zb = Wf["gdn_conv_w"][e], jnp.zeros((1, 1536), F32)
        alog_e, dtb_e = _rep(gdn_A_log[e], 128)[None], _rep(gdn_dt_bias[e], 128)[None]
        gg, qg, kvg = gdn_norm_g[e][None], mla_q_norm_g[e][None], mla_kv_norm_g[e][None]
        h, = rowmap(lambda *a: (_adaln(*a),), [xin], [g, shift, scale], (BF16,), name=f"adaln_{tag}", tile=1024)
        proj = mm((h, wext), name=f"ev_in_{tag}")
        qkvc = conv_fwd(proj, 0, 3, conv_w, zb, name=f"gdn_conv_{tag}")
        o_g, hist = gdn_fwd(qkvc, proj, 4, 5, alog_e, dtb_e, name=f"gdn_{tag}")
        o_a, = rowmap(lambda o, z, g_: (_gdn_gate(o, z, g_),), [o_g, (proj, 512, 3)], [gg], (BF16,), name=f"gdn_gate_{tag}")
        cqn, = rowmap(lambda t, g_: (_rms(t, g_),), [(proj, 384, 8)], [qg], (BF16,), name=f"q_norm_{tag}")
        ckvn, = rowmap(lambda t, g_: (_rms(t, g_),), [(proj, 256, 14)], [kvg], (BF16,), name=f"kv_norm_{tag}")
        q0 = mm((cqn, wq), name=f"q_up_{tag}")
        kv = mm((ckvn, wkv), name=f"kv_up_{tag}", out_dtype=BF16)
        q, = rowmap(lambda t, cs, sn, r: (_rope_q(t, cs, sn, r),), [q0, cos_t, sin_t], [rot], (BF16,), name=f"rope_q_{tag}")
        kp, = rowmap(lambda t, cs, sn, r: (_rope(t, cs, sn, r),), [(proj, 128, 30), cos_t, sin_t], [rot], (BF16,), name=f"rope_k_{tag}")
        o_b, lse = att_fwd(q, kv, kp, name=f"att_{tag}")
        xn, = mm([(o_a, wout[:512]), (o_b, wout[512:])], name=f"ev_out_{tag}", epi=lambda acc_, xr, gt: (xr + gt * acc_,),
                 epi_rows=(xin,), epi_pars=(gate,), epi_out_dtypes=(F32,))

        def bwd(dxn):
            wout_g = (gate * wout).astype(BF16)
            do_a = mm((dxn, wout_g[:512]), tb=True, name=f"ev_doa_{tag}")
            do_b = mm((dxn, wout_g[512:]), tb=True, name=f"ev_dob_{tag}")
            dw_raw = jnp.concatenate([mm((o_a, dxn), ta=True, name=f"ev_dwoa_{tag}"), mm((o_b, dxn), ta=True, name=f"ev_dwob_{tag}")], axis=0)
            dwo, dgate = _gate_grads(dw_raw, wout, gate, 1.0, name=f"ev_dgate_{tag}")
            acc("ev_w_out", e, dwo)
            dsum, = rowmap(lambda d, o_: (jnp.concatenate([jnp.broadcast_to(jnp.sum(d[:, 128 * hh:128 * (hh + 1)] * o_[:, 128 * hh:128 * (hh + 1)],
                                                                                        axis=-1, keepdims=True), (d.shape[0], 128))
                                                  for hh in range(4)], axis=1),), [do_b, o_b], [], (F32,), name=f"att_dsum_{tag}")
            dq4, dk2, dv = att_bwd(q, kv, kp, lse, dsum, do_b, name=f"att_bwd_{tag}")

            def rope_qb(d0, d1, d2, d3, cs, sn, r):
                parts = []
                for d in (d0, d1, d2, d3):
                    parts += [d[:, :128], _rope_t(d[:, 128:], cs, sn, r)]
                return (jnp.concatenate(parts, axis=1) * ATT_SCALE,)

            dq0, = rowmap(rope_qb, [dq4[0], dq4[1], dq4[2], dq4[3], cos_t, sin_t], [rot], (BF16,), name=f"rope_qb_{tag}")

            def rope_kb(d, cs, sn, r):
                dkp = d[:, 128:256] + d[:, 384:512] + d[:, 640:768] + d[:, 896:1024]
                return jnp.concatenate([d[:, 256 * hh:256 * hh + 128] for hh in range(4)], axis=1), _rope_t(dkp, cs, sn, r)

            dkn, dkr = rowmap(rope_kb, [dk2, cos_t, sin_t], [rot], (BF16, BF16), name=f"rope_kb_{tag}")
            dcqn = mm((dq0, wq), tb=True, name=f"q_upb_{tag}")
            acc("mla_w_uq", e, wq_ext_t(mm((cqn, dq0), ta=True, name=f"q_dw_{tag}")))
            dckvn = mm([(dkn, wkv[:, :512]), (dv, wkv[:, 512:])], tb=True, name=f"kv_upb_{tag}")
            acc("mla_w_ukv", e, wkv_ext_t(jnp.concatenate([mm((ckvn, dkn), ta=True, name=f"kv_dwk_{tag}"), mm((ckvn, dv), ta=True, name=f"kv_dwv_{tag}")], axis=1)))
            dcq, dqg = _vjp_rows(_rms, 1, 1, (BF16,), [(proj, 384, 8)], [dcqn], [qg], name=f"q_normb_{tag}")
            dckv, dkvg = _vjp_rows(_rms, 1, 1, (BF16,), [(proj, 256, 14)], [dckvn], [kvg], name=f"kv_normb_{tag}")
            acc("mla_q_norm_g", e, dqg[0])
            acc("mla_kv_norm_g", e, dkvg[0])
            do_g, dz, dgg = _vjp_rows(_gdn_gate, 2, 1, (F32, BF16), [o_g, (proj, 512, 3)], [do_a], [gg], name=f"gdn_gateb_{tag}")
            acc("gdn_norm_g", e, dgg[0])
            dqkvc, dbe, dae, dal, ddt = gdn_bwd(qkvc, proj, 4, 5, alog_e, dtb_e, hist, do_g, name=f"gdnb_{tag}")
            acc("gdn_A_log", e, dal.reshape(4, 128).sum(-1))
            acc("gdn_dt_bias", e, ddt.reshape(4, 128).sum(-1))
            dpre, dcw, _ = conv_bwd_pre(proj, 0, 3, conv_w, zb, dqkvc, name=f"gdn_convb_{tag}")
            acc("gdn_conv_w", e, dcw)
            dqkv = conv_bwd_x(dpre, conv_w, name=f"gdn_convx_{tag}", out_dtype=BF16)
            zc = lambda n: jnp.zeros((S_, n), BF16)
            dproj = jnp.concatenate([dqkv, dz, dbe, dae, dcq, zc(128), dckv, dkr, zc(128)], axis=1)
            dh = mm((dproj, wext), tb=True, name=f"ev_inb_{tag}")
            acc("ev_w_in", e, ev_ext_t(mm((h, dproj), ta=True, name=f"ev_dwin_{tag}")))
            return mixer_tail(xin, l, tag, dh, dxn, g, shift, scale, dgate)

        return xn, bwd

    def odd_sub(xin, l):
        o = l // 2
        tag = f"l{l}m"
        g, (shift, scale, gate) = Wf["norm_g"][l, 1][None], [mod[l, 1, t][None] for t in range(3)]
        wext, wout = od_ext(W["ssd_w_in"][o]), W["ssd_w_out"][o]
        conv_w, conv_b, ng = Wf["ssd_conv_w"][o], Wf["ssd_conv_b"][o][None], Wf["ssd_norm_g"][o][None]
        ex = lambda v: _rep(v, 64)[None]
        na_e, dtb_e, dsk_e = ex(-jnp.exp(ssd_A_log[o])), ex(ssd_dt_bias[o]), ex(ssd_D[o])
        h, = rowmap(lambda *a: (_adaln(*a),), [xin], [g, shift, scale], (BF16,), name=f"adaln_{tag}", tile=1024)
        proj = mm((h, wext), name=f"ssd_in_{tag}")
        zv = [(proj, 512, 6 + t) for t in range(4)]
        xbc = conv_fwd(proj, 0, 6, conv_w, conv_b, name=f"ssd_conv_{tag}")
        ys, hist = ssd_fwd(xbc, proj, na_e, dtb_e, dsk_e, name=f"ssd_{tag}")
        yn, = rowmap(lambda *a: (_ssd_gate(*a),), [ys] + zv, [ng], (BF16,), name=f"ssd_gate_{tag}", tile=512)
        xn, = mm((yn, wout), name=f"ssd_out_{tag}", epi=lambda acc_, xr, gt: (xr + gt * acc_,), epi_rows=(xin,),
                 epi_pars=(gate,), epi_out_dtypes=(F32,))

        def bwd(dxn):
            dyn = mm((dxn, (gate * wout).astype(BF16)), tb=True, name=f"ssd_dyn_{tag}", out_dtype=BF16)
            dwo, dgate = _gate_grads(mm((yn, dxn), ta=True, name=f"ssd_dwout_{tag}"), wout, gate, 1.0, name=f"ssd_dgate_{tag}")
            acc("ssd_w_out", o, dwo)
            dys, dz0, dz1, dz2, dz3, dng = _vjp_rows(_ssd_gate, 5, 1, (F32, BF16, BF16, BF16, BF16), [ys] + zv, [dyn], [ng],
                                                     name=f"ssd_gateb_{tag}", tile=512)
            acc("ssd_norm_g", o, dng[0])
            dxs, ddtx, db_, dc_, dna, ddtb, ddsk = ssd_bwd(xbc, proj, na_e, dtb_e, dsk_e, hist, dys, name=f"ssdb_{tag}")
            acc("ssd_A_log", o, dna.reshape(32, 64).sum(-1) * (-jnp.exp(ssd_A_log[o])))
            acc("ssd_dt_bias", o, ddtb.reshape(32, 64).sum(-1))
            acc("ssd_D", o, ddsk.reshape(32, 64).sum(-1))
            dxp, dcws, dcbs = [], [], []
            for part, (cb0, ncb, dpart) in enumerate(((0, 4, dxs), (4, 1, db_), (5, 1, dc_))):
                cols = slice(512 * cb0, 512 * (cb0 + ncb))
                dpre, dcw, dcb = conv_bwd_pre(proj, cb0, ncb, conv_w[:, cols], conv_b[:, cols], dpart, name=f"ssd_convb{part}_{tag}")
                dxp.append(conv_bwd_x(dpre, conv_w[:, cols], name=f"ssd_convx{part}_{tag}", out_dtype=BF16))
                dcws.append(dcw)
                dcbs.append(dcb[0])
            acc("ssd_conv_w", o, jnp.concatenate(dcws, axis=1))
            acc("ssd_conv_b", o, jnp.concatenate(dcbs))
            dproj = jnp.concatenate(dxp + [dz0, dz1, dz2, dz3, ddtx], axis=1)
            dh = mm((dproj, wext), tb=True, name=f"ssd_inb_{tag}")
            acc("ssd_w_in", o, od_ext_t(mm((h, dproj), ta=True, name=f"ssd_dwin_{tag}")))
            return mixer_tail(xin, l, tag, dh, dxn, g, shift, scale, dgate)

        return xn, bwd

    tape = []
    xc = xa
    for l in range(depth):
        xc, b0 = ffn_sub(xc, l, 0, 0)
        xc, b1 = (even_sub if l % 2 == 0 else odd_sub)(xc, l)
        xc, b2 = ffn_sub(xc, l, 2, 1)
        tape += [b0, b1, b2]

    def head(xr, tg, g_):
        def f(xv, gv):
            err = _rms(xv, gv) - tg
            return 0.5 * jnp.sum(jnp.mean(err * err, axis=-1, keepdims=True), axis=0, keepdims=True)
        lo, vjp = jax.vjp(f, xr, g_)
        dxv, dgv = vjp(jnp.ones_like(lo))
        return dxv, jnp.broadcast_to(lo, (1, 128)), dgv

    dx, loss_p, dfg = rowmap(head, [xc, tgt], [final_g[None]], (F32,), name="loss_head", n_reduce=2)
    loss = lax.psum(loss_p[0, 0], ("x", "y", "c"))

    for b in reversed(tape):
        dx = b(dx)
    grad_x = dx[None]

    full = {n: jnp.stack([grads[n][k] for k in sorted(grads[n])]) for n in ("ev_w_in", "mla_w_uq", "mla_w_ukv", "ev_w_out", "ssd_w_in", "ssd_w_out")}
    for n in ("ffn_w1", "ffn_w3", "ffn_w2"):
        full[n] = jnp.stack([jnp.stack([grads[n][(l, j)] for j in range(2)]) for l in range(depth)])
    gall = _pack_by_owner([full[n] for n, _, _ in BIG])
    half = PACK_ROWS // 2
    from_sib = swap_other_half(gall, name="swap_half")
    pair = add_pairs(lax.dynamic_slice_in_dim(gall, ci * half, half, axis=1), from_sib, name="add_sibling", out_dtype=BF16)
    recv = exchange_chips(pair, name="exchange_grads")
    part = sum_slots(recv, name="sum_chips")
    sib = swap_sibling(part, name="swap_sibling")
    lo = jnp.where(ci == 0, part, sib)
    hi_ = jnp.where(ci == 0, sib, part)
    g_tot = _unpack(jnp.concatenate([lo, hi_], axis=0))

    dmod_flat = jnp.stack([jnp.stack([jnp.stack(dmod[l][k]) for k in range(3)]) for l in range(depth)]).reshape(depth, 9 * D)
    small_names = ["norm_g", "gdn_conv_w", "gdn_A_log", "gdn_dt_bias", "gdn_norm_g", "mla_q_norm_g", "mla_kv_norm_g",
                   "ssd_conv_w", "ssd_conv_b", "ssd_A_log", "ssd_dt_bias", "ssd_D", "ssd_norm_g", "final_g"]
    small_full = {n: jnp.stack([grads[n][k] for k in sorted(grads[n])]) for n in small_names if n in grads}
    small_full["norm_g"] = jnp.stack([jnp.stack(dnorm_g[l]) for l in range(depth)])
    small_full["final_g"] = dfg[0]
    small_list = [dmod_flat] + [small_full[n] for n in small_names]
    small_shapes = [a.shape for a in small_list]
    sp = _flat_pack(small_list, 128, 8)
    sgath = allgather8(sp, name="gather_small_grads").reshape(8, sp.shape[0], 128)
    ssum = sum_slots(sgath, name="sum_small")
    tot = dict(zip(["ada_b"] + small_names, _flat_unpack(ssum, small_shapes)))
    dmod_all = sgath.reshape(8, -1)[:, :depth * 9 * D].reshape(8, depth, 9 * D)
    dmod_loc = lax.dynamic_slice(dmod_all, (0, 0, chip * ncol), (8, depth, ncol))
    g_ada_w = jnp.stack([mm((c_act, jnp.pad(dmod_loc[:, l], ((0, 8), (0, 0)))), ta=True, name=f"ada_dw_{l}", tk=16, tn=256)
                         for l in range(depth)])

    def own(n, a):
        for m_, sh, ax in SMALL_SHARDED:
            if m_ == n:
                return lax.dynamic_slice_in_dim(a, chip * sh[ax], sh[ax], axis=ax)
        return a

    res = {}
    for k, (n, _, _) in enumerate(BIG):
        res[n] = adamw(P[n], [g_tot[k]], M1[n], M2[n], name=f"adamw_{n}")
    res["ada_w"] = adamw(ada_w, [g_ada_w], m_ada_w, v_ada_w, name="adamw_ada_w")
    sm = ["ada_b"] + small_names
    shapes = [P[n].shape for n in sm]
    pk = lambda d: _flat_pack([d[n] for n in sm], 128, 8)
    outs = adamw(pk(P), [pk({n: own(n, tot[n]).reshape(P[n].shape) for n in sm})], pk(M1), pk(M2), name="adamw_small")
    un = [_flat_unpack(o, shapes) for o in outs]
    for i, n in enumerate(sm):
        res[n] = tuple(un[t][i] for t in range(4))
    return (loss, grad_x, *[res[n][0] for n in names], *[res[n][1] for n in names], *[res[n][2] for n in names], *[res[n][3] for n in names])
```

```python
import functools
import math

import jax
import jax.numpy as jnp
from jax import lax
from jax.experimental import pallas as pl
from jax.experimental.pallas import tpu as pltpu

F32 = jnp.float32
BF16 = jnp.bfloat16
HI = lax.Precision.HIGHEST
HI3 = lax.Precision.HIGH
VMEM_LIMIT = 56 * 1024 * 1024
NORM_EPS = 1e-6
MM_VMEM_BUDGET = 40 * 1024 * 1024


def _cp(sem=None):
    if sem is None:
        return pltpu.CompilerParams(vmem_limit_bytes=VMEM_LIMIT)
    return pltpu.CompilerParams(dimension_semantics=sem, vmem_limit_bytes=VMEM_LIMIT)


def _pick(dim, prefs):
    for p in prefs:
        if dim % p == 0:
            return p
    return dim


def mm(pairs, *, ta=False, tb=False, out_dtype=F32, name, epi=None, epi_rows=(), epi_pars=(), epi_out_dtypes=None,
       tm=None, tn=None, tk=None):
    if not isinstance(pairs, (list, tuple)) or not isinstance(pairs[0], (list, tuple)):
        pairs = [pairs]
    npair = len(pairs)
    a0, b0 = pairs[0]
    M = a0.shape[1] if ta else a0.shape[0]
    K = a0.shape[0] if ta else a0.shape[1]
    N = b0.shape[0] if tb else b0.shape[1]
    for a, b in pairs:
        assert (a.shape == ((K, M) if ta else (M, K))), (a.shape, M, K)
        assert (b.shape == ((N, K) if tb else (K, N))), (b.shape, K, N)
    tm = tm or _pick(M, (1024, 1408, 512, 384, 256, 128))
    tk = tk or (K if K <= 1024 else _pick(K, (1024, 1408, 512, 256, 128)))
    if tn is None:
        n_epi_out = 1 if epi is None else len(epi_out_dtypes)
        for tn in (1024, 1408, 512, 384, 256, 128, N):
            if N % tn:
                continue
            need = sum(2 * tk * (tm * a.dtype.itemsize + tn * b.dtype.itemsize) for a, b in pairs)
            need += tm * tn * 4 * (1 + 2 * n_epi_out + 2 * len(epi_rows))
            if need <= MM_VMEM_BUDGET:
                break
    nk = K // tk
    assert M % tm == 0 and N % tn == 0 and K % tk == 0, (M, N, K, tm, tn, tk)
    n_rows, n_pars = len(epi_rows), len(epi_pars)
    if epi is None:
        out_dtypes = (out_dtype,)
    else:
        out_dtypes = tuple(epi_out_dtypes)
    n_out = len(out_dtypes)
    dn = (((0 if ta else 1,), (1 if tb else 0,)), ((), ()))

    def body(*refs):
        ab = refs[:2 * npair]
        rows = refs[2 * npair:2 * npair + n_rows]
        pars = refs[2 * npair + n_rows:2 * npair + n_rows + n_pars]
        outs = refs[2 * npair + n_rows + n_pars:2 * npair + n_rows + n_pars + n_out]
        acc_ref = refs[-1]
        k = pl.program_id(2)

        @pl.when(k == 0)
        def _():
            acc_ref[...] = jnp.zeros_like(acc_ref)

        acc = acc_ref[...]
        for p in range(npair):
            a = ab[2 * p][...].astype(BF16)
            b = ab[2 * p + 1][...].astype(BF16)
            acc = acc + lax.dot_general(a, b, dn, preferred_element_type=F32)
        acc_ref[...] = acc

        @pl.when(k == nk - 1)
        def _():
            r = acc_ref[...]
            if epi is None:
                outs[0][...] = r.astype(outs[0].dtype)
            else:
                res = epi(r, *[x[...] for x in rows], *[x[...] for x in pars])
                for o, v in zip(outs, res):
                    o[...] = v.astype(o.dtype)

    a_spec = pl.BlockSpec((tk, tm), lambda i, j, k: (k, i)) if ta else pl.BlockSpec((tm, tk), lambda i, j, k: (i, k))
    b_spec = pl.BlockSpec((tn, tk), lambda i, j, k: (j, k)) if tb else pl.BlockSpec((tk, tn), lambda i, j, k: (k, j))
    in_specs = []
    args = []
    for a, b in pairs:
        in_specs += [a_spec, b_spec]
        args += [a, b]
    for r in epi_rows:
        in_specs.append(pl.BlockSpec((tm, tn), lambda i, j, k: (i, j)))
        args.append(r)
    for p_ in epi_pars:
        in_specs.append(pl.BlockSpec((1, tn), lambda i, j, k: (0, j)))
        args.append(p_)
    out_specs = [pl.BlockSpec((tm, tn), lambda i, j, k: (i, j)) for _ in range(n_out)]
    out_shape = [jax.ShapeDtypeStruct((M, N), d) for d in out_dtypes]
    res = pl.pallas_call(
        body, grid=(M // tm, N // tn, nk), in_specs=in_specs, out_specs=out_specs, out_shape=out_shape,
        scratch_shapes=[pltpu.VMEM((tm, tn), F32)], compiler_params=_cp(("parallel", "parallel", "arbitrary")), name=name,
    )(*args)
    return res[0] if epi is None else tuple(res)


def rowmap(fn, rows, pars, out_dtypes, *, name, tile=1024, n_reduce=0):
    views = []
    for r in rows:
        if isinstance(r, tuple):
            views.append(r)
        else:
            views.append((r, r.shape[1], 0))
    S = views[0][0].shape[0]
    tile = min(tile, S)
    assert S % tile == 0
    nt = S // tile
    row_structs = [jax.ShapeDtypeStruct((tile, w), a.dtype) for a, w, _ in views]
    par_structs = [jax.ShapeDtypeStruct(p.shape, p.dtype) for p in pars]
    out_structs = jax.eval_shape(fn, *row_structs, *par_structs)
    n_out = len(out_structs)
    n_row_out = n_out - n_reduce
    nr, npar = len(views), len(pars)

    def body(*refs):
        ins = [x[...] for x in refs[:nr + npar]]
        outs = refs[nr + npar:]
        res = fn(*ins)
        for o, v in zip(outs[:n_row_out], res[:n_row_out]):
            o[...] = v.astype(o.dtype)
        if n_reduce:
            i = pl.program_id(0)

            @pl.when(i == 0)
            def _():
                for o, v in zip(outs[n_row_out:], res[n_row_out:]):
                    o[...] = v.astype(o.dtype)

            @pl.when(i > 0)
            def _():
                for o, v in zip(outs[n_row_out:], res[n_row_out:]):
                    o[...] += v.astype(o.dtype)

    in_specs = [pl.BlockSpec((tile, w), functools.partial(lambda i, c: (i, c), c=c)) for _, w, c in views]
    in_specs += [pl.BlockSpec(p.shape, lambda i: (0, 0)) for p in pars]
    out_specs = [pl.BlockSpec((tile, s.shape[1]), lambda i: (i, 0)) for s in out_structs[:n_row_out]]
    out_specs += [pl.BlockSpec(s.shape, lambda i: (0, 0)) for s in out_structs[n_row_out:]]
    out_shape = [jax.ShapeDtypeStruct((S, s.shape[1]), d) for s, d in zip(out_structs[:n_row_out], out_dtypes[:n_row_out])]
    out_shape += [jax.ShapeDtypeStruct(s.shape, F32) for s in out_structs[n_row_out:]]
    res = pl.pallas_call(
        body, grid=(nt,), in_specs=in_specs, out_specs=out_specs, out_shape=out_shape,
        compiler_params=_cp(("arbitrary",) if n_reduce else ("parallel",)), name=name,
    )(*[v[0] for v in views], *pars)
    return tuple(res)


CH = 64


def _softplus(x):
    return jnp.where(x > 20.0, x, jnp.log(1.0 + jnp.exp(jnp.minimum(x, 20.0))))


def _dot(a, b, dn=(((1,), (0,)), ((), ())), hi=False):
    if hi:
        return lax.dot_general(a.astype(F32), b.astype(F32), dn, precision=HI if hi is True else hi, preferred_element_type=F32)
    return lax.dot_general(a.astype(BF16), b.astype(BF16), dn, preferred_element_type=F32)


_NT = (((1,), (1,)), ((), ()))
_TN = (((0,), (0,)), ((), ()))


def _chunk_consts():
    r = lax.broadcasted_iota(jnp.int32, (CH, 2 * CH), 0)
    c0 = lax.broadcasted_iota(jnp.int32, (CH, 2 * CH), 1)
    c = jnp.where(c0 >= CH, c0 - CH, c0)
    r1 = lax.broadcasted_iota(jnp.int32, (CH, CH), 0)
    c1 = lax.broadcasted_iota(jnp.int32, (CH, CH), 1)
    return dict(
        lower2=r >= c, strict2=r > c, U2=(r <= c).astype(F32), eye2=(r == c).astype(F32),
        L=(r1 >= c1).astype(F32), ones=jnp.ones((CH, CH), F32), Z=jnp.zeros((CH, 2 * CH), F32))


@jax.custom_vjp
def _tri_inv2(a2s, eye2, Z):
    def prod(x2, y):
        return _dot(x2, jnp.concatenate([y, Z], axis=0), hi=HI3)

    bs = [-a2 for a2 in a2s]
    ts = [eye2 + b for b in bs]
    for _ in range(5):
        bs = [prod(b, b) for b in bs]
        ts = [t + prod(t, b) for t, b in zip(ts, bs)]
    return tuple(ts)


def _tri_inv2_fwd(a2s, eye2, Z):
    ts = _tri_inv2(a2s, eye2, Z)
    return ts, (ts, eye2, Z)


def _tri_inv2_bwd(res, dts):
    ts, eye2, Z = res
    xs = [_dot(t2, dt2, _TN, hi=HI3)[:CH] for t2, dt2 in zip(ts, dts)]
    das = tuple(-_dot(x2, jnp.concatenate([t2, Z], axis=0), _NT, hi=HI3) for x2, t2 in zip(xs, ts))
    return das, jnp.zeros_like(eye2), jnp.zeros_like(Z)


_tri_inv2.defvjp(_tri_inv2_fwd, _tri_inv2_bwd)


def _gdn_heads(qs, ks, vs, bxs, axs, Ss, alogs, dtbs, cst):
    lower2, strict2, U2, eye2, L, ones, Z = (cst[n] for n in ("lower2", "strict2", "U2", "eye2", "L", "ones", "Z"))
    H = range(len(qs))

    def prod(x2, y):
        return _dot(x2, jnp.concatenate([y, Z], axis=0), hi=HI3)

    qn = [qs[h] * lax.rsqrt(jnp.sum(qs[h] * qs[h], axis=-1, keepdims=True) + NORM_EPS) * (128.0 ** -0.5) for h in H]
    kn = [ks[h] * lax.rsqrt(jnp.sum(ks[h] * ks[h], axis=-1, keepdims=True) + NORM_EPS) for h in H]
    beta = [jax.nn.sigmoid(bxs[h]) for h in H]
    g = [-jnp.exp(alogs[h]) * _softplus(axs[h] + dtbs[h]) for h in H]
    gc = [_dot(L, g[h], hi=HI3) for h in H]
    n2 = [_dot(ones, g[h] * U2, hi=HI3) for h in H]
    decay2 = [jnp.where(lower2, jnp.exp(jnp.where(lower2, gc[h] - n2[h], 0.0)), 0.0) for h in H]
    kb = [kn[h] * beta[h] for h in H]
    kn2 = [jnp.concatenate([kn[h], kn[h]], axis=0) for h in H]
    a2 = tuple(jnp.where(strict2, _dot(kb[h], kn2[h], _NT) * decay2[h], 0.0) for h in H)
    t2 = _tri_inv2(a2, eye2, Z)
    glast = [jnp.sum(g[h], axis=0, keepdims=True) for h in H]
    u = [prod(t2[h], vs[h] * beta[h]) for h in H]
    w = [prod(t2[h], kb[h] * jnp.exp(gc[h])) for h in H]
    attn2 = [jnp.where(lower2, _dot(qn[h], kn2[h], _NT) * decay2[h], 0.0) for h in H]
    k_end = [kn[h] * jnp.exp(glast[h] - gc[h]) for h in H]
    q_start = [qn[h] * jnp.exp(gc[h]) for h in H]
    v_new = [u[h] - _dot(w[h], Ss[h]) for h in H]
    o = [_dot(q_start[h], Ss[h]) + _dot(attn2[h], jnp.concatenate([v_new[h], Z], axis=0)) for h in H]
    s_new = [Ss[h] * jnp.exp(glast[h]) + _dot(k_end[h], v_new[h], _TN) for h in H]
    return tuple(o), tuple(s_new)


def gdn_fwd(qkv, proj, bcol, acol, alog_e, dtb_e, *, name):
    S_ = qkv.shape[0]
    nc = S_ // CH

    def body(q_ref, k_ref, v_ref, b_ref, a_ref, al_ref, dt_ref, o_ref, hist_ref, s_ref):
        i = pl.program_id(0)

        @pl.when(i == 0)
        def _():
            s_ref[...] = jnp.zeros_like(s_ref)

        cst = _chunk_consts()
        hist_ref[0] = s_ref[...]
        heads = [slice(128 * h, 128 * (h + 1)) for h in range(4)]
        rd = lambda ref: tuple(ref[:, ls] for ls in heads)
        os_, s_news = _gdn_heads(rd(q_ref), rd(k_ref), rd(v_ref), rd(b_ref), rd(a_ref), tuple(s_ref[ls, :] for ls in heads),
                                 rd(al_ref), rd(dt_ref), cst)
        for ls, o, s_new in zip(heads, os_, s_news):
            o_ref[:, ls] = o
            s_ref[ls, :] = s_new

    blk = lambda cb: pl.BlockSpec((CH, 512), functools.partial(lambda i, cb: (i, cb), cb=cb))
    par = pl.BlockSpec((1, 512), lambda i: (0, 0))
    return pl.pallas_call(
        body, grid=(nc,), in_specs=[blk(0), blk(1), blk(2), blk(bcol), blk(acol), par, par],
        out_specs=[pl.BlockSpec((CH, 512), lambda i: (i, 0)), pl.BlockSpec((1, 512, 128), lambda i: (i, 0, 0))],
        out_shape=[jax.ShapeDtypeStruct((S_, 512), F32), jax.ShapeDtypeStruct((nc, 512, 128), F32)],
        scratch_shapes=[pltpu.VMEM((512, 128), F32)], compiler_params=_cp(("arbitrary",)), name=name,
    )(qkv, qkv, qkv, proj, proj, alog_e, dtb_e)


def gdn_bwd(qkv, proj, bcol, acol, alog_e, dtb_e, hist, do, *, name):
    S_ = qkv.shape[0]
    nc = S_ // CH

    def body(q_ref, k_ref, v_ref, b_ref, a_ref, al_ref, dt_ref, hist_ref, do_ref, dqkv_ref, db_ref, da_ref, dal_ref, ddt_ref, ds_ref):
        i = pl.program_id(0)

        @pl.when(i == 0)
        def _():
            ds_ref[...] = jnp.zeros_like(ds_ref)
            dal_ref[...] = jnp.zeros_like(dal_ref)
            ddt_ref[...] = jnp.zeros_like(ddt_ref)

        cst = _chunk_consts()
        heads = [slice(128 * h, 128 * (h + 1)) for h in range(4)]
        rd = lambda ref: tuple(ref[:, ls] for ls in heads)
        fn = functools.partial(_gdn_heads, cst=cst)
        _, vjp = jax.vjp(fn, rd(q_ref), rd(k_ref), rd(v_ref), rd(b_ref), rd(a_ref), tuple(hist_ref[0, ls, :] for ls in heads),
                         rd(al_ref), rd(dt_ref))
        grads = vjp((rd(do_ref), tuple(ds_ref[ls, :] for ls in heads)))
        for h in range(4):
            ls = heads[h]
            dq, dk, dv, db, da, ds_in, dal, ddt = (t[h] for t in grads)
            dqkv_ref[:, 128 * h:128 * (h + 1)] = dq
            dqkv_ref[:, 512 + 128 * h:512 + 128 * (h + 1)] = dk
            dqkv_ref[:, 1024 + 128 * h:1024 + 128 * (h + 1)] = dv
            db_ref[:, ls] = db.astype(db_ref.dtype)
            da_ref[:, ls] = da.astype(da_ref.dtype)
            ds_ref[ls, :] = ds_in
            dal_ref[:, ls] += dal
            ddt_ref[:, ls] += ddt

    rblk = lambda cb: pl.BlockSpec((CH, 512), functools.partial(lambda i, cb: (nc - 1 - i, cb), cb=cb))
    par = pl.BlockSpec((1, 512), lambda i: (0, 0))
    return pl.pallas_call(
        body, grid=(nc,),
        in_specs=[rblk(0), rblk(1), rblk(2), rblk(bcol), rblk(acol), par, par,
                  pl.BlockSpec((1, 512, 128), lambda i: (nc - 1 - i, 0, 0)), rblk(0)],
        out_specs=[pl.BlockSpec((CH, 1536), lambda i: (nc - 1 - i, 0)), rblk(0), rblk(0), par, par],
        out_shape=[jax.ShapeDtypeStruct((S_, 1536), F32), jax.ShapeDtypeStruct((S_, 512), BF16), jax.ShapeDtypeStruct((S_, 512), BF16),
                   jax.ShapeDtypeStruct((1, 512), F32), jax.ShapeDtypeStruct((1, 512), F32)],
        scratch_shapes=[pltpu.VMEM((512, 128), F32)], compiler_params=_cp(("arbitrary",)), name=name,
    )(qkv, qkv, qkv, proj, proj, alog_e, dtb_e, hist, do)


def _ssd_pairs(xs, dtxs, bms, cms, hss, nas, dtbs, dsks, cst):
    lower2, U2, L, ones = cst["lower2"], cst["U2"], cst["L"], cst["ones"]
    lane = lax.broadcasted_iota(jnp.int32, (1, 2 * CH), 1)
    mask_l = (lane < CH).astype(F32)
    mask_r = 1.0 - mask_l
    ones_w = jnp.ones((CH, 2 * CH), F32)
    P_ = range(len(xs))
    G_ = range(len(bms))
    per = len(xs) // len(bms)
    cb2 = [_dot(cms[g], jnp.concatenate([bms[g], bms[g]], axis=0), _NT) for g in G_]
    dt = [_softplus(dtxs[p] + dtbs[p]) for p in P_]
    da = [dt[p] * nas[p] for p in P_]
    m = [_dot(L, da[p], hi=HI3) for p in P_]
    n2 = [_dot(ones, da[p] * U2, hi=HI3) for p in P_]
    lm2 = [jnp.where(lower2, jnp.exp(jnp.where(lower2, m[p] - n2[p], 0.0)), 0.0) for p in P_]
    xdt = [xs[p] * dt[p] for p in P_]
    x2 = [jnp.concatenate([xdt[p] * mask_l, xdt[p] * mask_r], axis=0) for p in P_]
    y_diag = [_dot(cb2[p // per] * lm2[p], x2[p]) for p in P_]
    alast = [jnp.sum(da[p], axis=0, keepdims=True) for p in P_]
    y_off = [_dot(cms[p // per], hss[p], _NT) * jnp.exp(m[p]) for p in P_]
    cd = [jnp.exp(_dot(da[p], ones_w, _TN, hi=HI3)) for p in P_]
    hs_new = [hss[p] * cd[p] + _dot(xdt[p] * jnp.exp(alast[p] - m[p]), bms[p // per], _TN) for p in P_]
    ys = [y_diag[p] + y_off[p] + dsks[p] * xs[p] for p in P_]
    return tuple(ys), tuple(hs_new)


def _ssd_specs(nc, rev):
    ci = (lambda i: nc - 1 - i) if rev else (lambda i: i)
    col = lambda w, c: pl.BlockSpec((CH, w), functools.partial(lambda i, c: (ci(i), c), c=c))
    xg = [col(512, g) for g in range(4)]
    dtg = [col(512, 10 + g) for g in range(4)]
    par = pl.BlockSpec((1, 2048), lambda i: (0, 0))
    hist = pl.BlockSpec((1, 2048, 128), lambda i: (ci(i), 0, 0))
    return xg, dtg, col(512, 4), col(512, 5), par, hist, col


def _ssd_read(x_refs, dt_refs, b_ref, c_ref, na_ref, dtb_ref, dsk_ref):
    sl = [slice(128 * p, 128 * (p + 1)) for p in range(4)]
    xs = tuple(x_refs[g][:, s] for g in range(4) for s in sl)
    dts = tuple(dt_refs[g][:, s] for g in range(4) for s in sl)
    bms = tuple(b_ref[:, s] for s in sl)
    cms = tuple(c_ref[:, s] for s in sl)
    lanes = [slice(128 * p, 128 * (p + 1)) for p in range(16)]
    pars = [tuple(r[:, s] for s in lanes) for r in (na_ref, dtb_ref, dsk_ref)]
    return xs, dts, bms, cms, pars, lanes


def ssd_fwd(xbc, proj, na_e, dtb_e, dsk_e, *, name):
    S_ = xbc.shape[0]
    nc = S_ // CH
    xg, dtg, bs, cs, par, hist, _ = _ssd_specs(nc, False)

    def body(*refs):
        x_refs, dt_refs = refs[0:4], refs[4:8]
        b_ref, c_ref, na_ref, dtb_ref, dsk_ref, y_ref, hist_ref, s_ref = refs[8:]
        i = pl.program_id(0)

        @pl.when(i == 0)
        def _():
            s_ref[...] = jnp.zeros_like(s_ref)

        cst = _chunk_consts()
        hist_ref[0] = s_ref[...]
        xs, dts, bms, cms, pars, lanes = _ssd_read(x_refs, dt_refs, b_ref, c_ref, na_ref, dtb_ref, dsk_ref)
        ys, hs_new = _ssd_pairs(xs, dts, bms, cms, tuple(s_ref[s, :] for s in lanes), *pars, cst)
        for p, s in enumerate(lanes):
            y_ref[:, s] = ys[p]
            s_ref[s, :] = hs_new[p]

    return pl.pallas_call(
        body, grid=(nc,), in_specs=xg + dtg + [bs, cs, par, par, par],
        out_specs=[pl.BlockSpec((CH, 2048), lambda i: (i, 0)), hist],
        out_shape=[jax.ShapeDtypeStruct((S_, 2048), F32), jax.ShapeDtypeStruct((nc, 2048, 128), F32)],
        scratch_shapes=[pltpu.VMEM((2048, 128), F32)], compiler_params=_cp(("arbitrary",)), name=name,
    )(xbc, xbc, xbc, xbc, proj, proj, proj, proj, xbc, xbc, na_e, dtb_e, dsk_e)


def ssd_bwd(xbc, proj, na_e, dtb_e, dsk_e, hist, dy, *, name):
    S_ = xbc.shape[0]
    nc = S_ // CH
    xg, dtg, bs, cs, par, hist_spec, col = _ssd_specs(nc, True)
    wide = pl.BlockSpec((CH, 2048), lambda i: (nc - 1 - i, 0))

    def body(*refs):
        x_refs, dt_refs = refs[0:4], refs[4:8]
        (b_ref, c_ref, na_ref, dtb_ref, dsk_ref, hist_ref, dy_ref,
         dx_ref, ddt_ref, db_ref, dc_ref, dna_ref, ddtb_ref, ddsk_ref, ds_ref) = refs[8:]
        i = pl.program_id(0)

        @pl.when(i == 0)
        def _():
            ds_ref[...] = jnp.zeros_like(ds_ref)
            dna_ref[...] = jnp.zeros_like(dna_ref)
            ddtb_ref[...] = jnp.zeros_like(ddtb_ref)
            ddsk_ref[...] = jnp.zeros_like(ddsk_ref)

        cst = _chunk_consts()
        xs, dts, bms, cms, pars, lanes = _ssd_read(x_refs, dt_refs, b_ref, c_ref, na_ref, dtb_ref, dsk_ref)
        fn = functools.partial(_ssd_pairs, cst=cst)
        _, vjp = jax.vjp(fn, xs, dts, bms, cms, tuple(hist_ref[0, s, :] for s in lanes), *pars)
        dxs, ddts, dbs, dcs, dhs, dnas, ddtbs, ddsks = vjp((tuple(dy_ref[:, s] for s in lanes), tuple(ds_ref[s, :] for s in lanes)))
        for g in range(4):
            db_ref[:, 128 * g:128 * (g + 1)] = dbs[g]
            dc_ref[:, 128 * g:128 * (g + 1)] = dcs[g]
        for p, s in enumerate(lanes):
            dx_ref[:, s] = dxs[p]
            ddt_ref[:, s] = ddts[p].astype(ddt_ref.dtype)
            ds_ref[s, :] = dhs[p]
            dna_ref[:, s] += dnas[p]
            ddtb_ref[:, s] += ddtbs[p]
            ddsk_ref[:, s] += ddsks[p]

    half = pl.BlockSpec((CH, 512), lambda i: (nc - 1 - i, 0))
    return pl.pallas_call(
        body, grid=(nc,), in_specs=xg + dtg + [bs, cs, par, par, par, hist_spec, wide],
        out_specs=[wide, wide, half, half, par, par, par],
        out_shape=[jax.ShapeDtypeStruct((S_, 2048), F32), jax.ShapeDtypeStruct((S_, 2048), BF16),
                   jax.ShapeDtypeStruct((S_, 512), F32), jax.ShapeDtypeStruct((S_, 512), F32)] +
                  [jax.ShapeDtypeStruct((1, 2048), F32)] * 3,
        scratch_shapes=[pltpu.VMEM((2048, 128), F32)], compiler_params=_cp(("arbitrary",)), name=name,
    )(xbc, xbc, xbc, xbc, proj, proj, proj, proj, xbc, xbc, na_e, dtb_e, dsk_e, hist, dy)


ATT_T = 1024
ATT_SCALE = 192.0 ** -0.5
NEG = -1e30


def _chunk_mask(shape):
    return lax.broadcasted_iota(jnp.int32, shape, 1) // CH <= lax.broadcasted_iota(jnp.int32, shape, 0) // CH


def _tri_pairs(n, by_row):
    pairs = [(i, j) for i in range(n) for j in range(i + 1)] if by_row else [(i, j) for j in range(n) for i in range(j, n)]
    return jnp.asarray([p[0] for p in pairs], jnp.int32), jnp.asarray([p[1] for p in pairs], jnp.int32)


def att_fwd(q, kv, kp, *, name):
    S_ = q.shape[0]
    T = min(ATT_T, S_)
    n = S_ // T
    ii, jj = _tri_pairs(n, True)
    HP = 2

    def body(ii_ref, jj_ref, q_ref, kn_ref, kp_ref, v_ref, o_ref, lse_ref, m_ref, l_ref, acc_ref):
        t = pl.program_id(1)
        i, j = ii_ref[t], jj_ref[t]

        @pl.when(j == 0)
        def _():
            m_ref[...] = jnp.full_like(m_ref, NEG)
            l_ref[...] = jnp.zeros_like(l_ref)
            acc_ref[...] = jnp.zeros_like(acc_ref)

        H = range(HP)
        hs = [slice(128 * h, 128 * (h + 1)) for h in H]

        def step(diag):
            kp_ = kp_ref[...]
            k2 = [jnp.concatenate([kn_ref[:, hs[h]], kp_], axis=1) for h in H]
            s = [_dot(q_ref[:, 256 * h:256 * (h + 1)], k2[h], _NT) for h in H]
            if diag:
                mask = _chunk_mask(s[0].shape)
                s = [jnp.where(mask, s[h], NEG) for h in H]
            m_prev = [m_ref[:, hs[h]] for h in H]
            m_cur = [jnp.maximum(m_prev[h], jnp.max(s[h], axis=-1, keepdims=True)) for h in H]
            p = [jnp.exp(s[h] - m_cur[h][:, :1]) for h in H]
            alpha = [jnp.exp(m_prev[h] - m_cur[h]) for h in H]
            for h in H:
                l_ref[:, hs[h]] = alpha[h] * l_ref[:, hs[h]] + jnp.sum(p[h], axis=-1, keepdims=True)
            pv = [_dot(p[h], v_ref[:, hs[h]]) for h in H]
            for h in H:
                acc_ref[:, hs[h]] = acc_ref[:, hs[h]] * alpha[h] + pv[h]
                m_ref[:, hs[h]] = m_cur[h]

        @pl.when(j < i)
        def _():
            step(False)

        @pl.when(j == i)
        def _():
            step(True)
            o_ref[...] = acc_ref[...] / l_ref[...]
            lse_ref[...] = m_ref[...] + jnp.log(l_ref[...])

    W = 128 * HP
    grid_spec = pltpu.PrefetchScalarGridSpec(
        num_scalar_prefetch=2, grid=(4 // HP, ii.shape[0]),
        in_specs=[pl.BlockSpec((T, 2 * W), lambda h, t, ii_, jj_: (ii_[t], h)), pl.BlockSpec((T, W), lambda h, t, ii_, jj_: (jj_[t], h)),
                  pl.BlockSpec((T, 128), lambda h, t, ii_, jj_: (jj_[t], 0)),
                  pl.BlockSpec((T, W), lambda h, t, ii_, jj_: (jj_[t], 4 // HP + h))],
        out_specs=[pl.BlockSpec((T, W), lambda h, t, ii_, jj_: (ii_[t], h))] * 2,
        scratch_shapes=[pltpu.VMEM((T, W), F32)] * 3)
    return pl.pallas_call(
        body, grid_spec=grid_spec, out_shape=[jax.ShapeDtypeStruct((S_, 512), F32), jax.ShapeDtypeStruct((S_, 512), F32)],
        compiler_params=_cp(("parallel", "arbitrary")), name=name,
    )(ii, jj, q, kv, kp, kv)


def att_bwd(q, kv, kp, lse, dsum, do, *, name):
    S_ = q.shape[0]
    T = min(ATT_T, S_)
    n = S_ // T
    ii, jj = _tri_pairs(n, False)
    last = ii.shape[0] - 1

    def body(ii_ref, jj_ref, q_ref, kn_ref, kp_ref, v_ref, lse_ref, d_ref, do_ref, dq_hbm, dk_ref, dv_ref, dq_acc, sem):
        h, t = pl.program_id(0), pl.program_id(1)
        i, j = ii_ref[t], jj_ref[t]

        @pl.when(t == 0)
        def _():
            dq_acc[...] = jnp.zeros_like(dq_acc)

        def step(diag):
            k2 = jnp.concatenate([kn_ref[...], kp_ref[...]], axis=1)
            qb = q_ref[...]
            dob = do_ref[...].astype(BF16)
            s = _dot(qb, k2, _NT)
            p = jnp.exp(s - lse_ref[:, :1])
            if diag:
                p = jnp.where(_chunk_mask(s.shape), p, 0.0)
            if diag:
                dv_ref[...] = _dot(p, dob, _TN)
            else:
                dv_ref[...] += _dot(p, dob, _TN)
            ds = (p * (_dot(dob, v_ref[...], _NT) - d_ref[:, :1])).astype(BF16)
            if diag:
                dk_ref[...] = _dot(ds, qb, _TN)
            else:
                dk_ref[...] += _dot(ds, qb, _TN)
            rows = pl.ds(pl.multiple_of(i * T, T), T)
            dq_acc[rows, :] += _dot(ds, k2)

        @pl.when(i > j)
        def _():
            step(False)

        @pl.when(i == j)
        def _():
            step(True)

        @pl.when(t == last)
        def _():
            cp = pltpu.make_async_copy(dq_acc, dq_hbm.at[h], sem)
            cp.start()
            cp.wait()

    qmap = lambda h, t, ii_, jj_: (ii_[t], h)
    grid_spec = pltpu.PrefetchScalarGridSpec(
        num_scalar_prefetch=2, grid=(4, ii.shape[0]),
        in_specs=[pl.BlockSpec((T, 256), qmap), pl.BlockSpec((T, 128), lambda h, t, ii_, jj_: (jj_[t], h)),
                  pl.BlockSpec((T, 128), lambda h, t, ii_, jj_: (jj_[t], 0)), pl.BlockSpec((T, 128), lambda h, t, ii_, jj_: (jj_[t], 4 + h)),
                  pl.BlockSpec((T, 128), qmap), pl.BlockSpec((T, 128), qmap), pl.BlockSpec((T, 128), qmap)],
        out_specs=[pl.BlockSpec(memory_space=pl.ANY), pl.BlockSpec((T, 256), lambda h, t, ii_, jj_: (jj_[t], h)),
                   pl.BlockSpec((T, 128), lambda h, t, ii_, jj_: (jj_[t], h))],
        scratch_shapes=[pltpu.VMEM((S_, 256), F32), pltpu.SemaphoreType.DMA])
    return pl.pallas_call(
        body, grid_spec=grid_spec,
        out_shape=[jax.ShapeDtypeStruct((4, S_, 256), F32), jax.ShapeDtypeStruct((S_, 1024), F32), jax.ShapeDtypeStruct((S_, 512), F32)],
        compiler_params=_cp(("arbitrary", "arbitrary")), name=name,
    )(ii, jj, q, kv, kp, kv, lse, dsum, do)


CONV_T = 1024


def _shift_down(x, halo, s):
    sh = pltpu.roll(x, s, axis=0)
    hr = pltpu.roll(halo, s, axis=0)
    r8 = lax.broadcasted_iota(jnp.int32, hr.shape, 0)
    top = jnp.where(r8 < s, hr, sh[:8])
    return jnp.concatenate([top, sh[8:]], axis=0)


def _shift_up(x, halo, s):
    n = x.shape[0]
    sh = pltpu.roll(x, n - s, axis=0)
    hr = pltpu.roll(halo, 8 - s, axis=0)
    r8 = lax.broadcasted_iota(jnp.int32, hr.shape, 0)
    bot = jnp.where(r8 >= 8 - s, hr, sh[n - 8:])
    return jnp.concatenate([sh[:n - 8], bot], axis=0)


def _conv_pre(x, halo, w, b):
    y = x * w[3:4] + b
    for j in range(3):
        y = y + _shift_down(x, halo, 3 - j) * w[j:j + 1]
    return y


def conv_fwd(src, cb0, ncb, w, b, *, name):
    S_ = src.shape[0]
    T = min(CONV_T, S_)
    nt = S_ // T

    def body(x_ref, h_ref, w_ref, b_ref, o_ref):
        i = pl.program_id(1)
        halo = jnp.where(i > 0, h_ref[...], 0.0)
        y = _conv_pre(x_ref[...], halo, w_ref[...], b_ref[...])
        o_ref[...] = y * jax.nn.sigmoid(y)

    return pl.pallas_call(
        body, grid=(ncb, nt),
        in_specs=[pl.BlockSpec((T, 512), lambda c, i: (i, cb0 + c)),
                  pl.BlockSpec((8, 512), lambda c, i: (jnp.maximum(i * (T // 8) - 1, 0), cb0 + c)),
                  pl.BlockSpec((4, 512), lambda c, i: (0, c)), pl.BlockSpec((1, 512), lambda c, i: (0, c))],
        out_specs=pl.BlockSpec((T, 512), lambda c, i: (i, c)),
        out_shape=jax.ShapeDtypeStruct((S_, 512 * ncb), F32), compiler_params=_cp(("parallel", "parallel")), name=name,
    )(src, src, w, b)


def conv_bwd_pre(src, cb0, ncb, w, b, dy, *, name):
    S_ = src.shape[0]
    T = min(CONV_T, S_)
    nt = S_ // T

    def body(x_ref, h_ref, w_ref, b_ref, dy_ref, dp_ref, dw_ref, db_ref):
        i = pl.program_id(1)
        halo = jnp.where(i > 0, h_ref[...], 0.0)
        x = x_ref[...]
        y = _conv_pre(x, halo, w_ref[...], b_ref[...])
        sg = jax.nn.sigmoid(y)
        dpre = dy_ref[...] * (sg * (1.0 + y * (1.0 - sg)))
        dp_ref[...] = dpre
        rows = [jnp.sum(dpre * _shift_down(x, halo, 3 - j), axis=0, keepdims=True) for j in range(3)]
        rows.append(jnp.sum(dpre * x, axis=0, keepdims=True))
        dw = jnp.concatenate(rows, axis=0)
        db = jnp.sum(dpre, axis=0, keepdims=True)

        @pl.when(i == 0)
        def _():
            dw_ref[...] = dw
            db_ref[...] = db

        @pl.when(i > 0)
        def _():
            dw_ref[...] += dw
            db_ref[...] += db

    return pl.pallas_call(
        body, grid=(ncb, nt),
        in_specs=[pl.BlockSpec((T, 512), lambda c, i: (i, cb0 + c)),
                  pl.BlockSpec((8, 512), lambda c, i: (jnp.maximum(i * (T // 8) - 1, 0), cb0 + c)),
                  pl.BlockSpec((4, 512), lambda c, i: (0, c)), pl.BlockSpec((1, 512), lambda c, i: (0, c)),
                  pl.BlockSpec((T, 512), lambda c, i: (i, c))],
        out_specs=[pl.BlockSpec((T, 512), lambda c, i: (i, c)), pl.BlockSpec((4, 512), lambda c, i: (0, c)),
                   pl.BlockSpec((1, 512), lambda c, i: (0, c))],
        out_shape=[jax.ShapeDtypeStruct((S_, 512 * ncb), F32), jax.ShapeDtypeStruct((4, 512 * ncb), F32),
                   jax.ShapeDtypeStruct((1, 512 * ncb), F32)],
        compiler_params=_cp(("parallel", "arbitrary")), name=name,
    )(src, src, w, b, dy)


def conv_bwd_x(dpre, w, *, name, out_dtype=F32):
    S_, C = dpre.shape
    T = min(CONV_T, S_)
    nt = S_ // T
    ncb = C // 512

    def body(d_ref, h_ref, w_ref, o_ref):
        i = pl.program_id(1)
        halo = jnp.where(i < nt - 1, h_ref[...], 0.0)
        d = d_ref[...]
        w_ = w_ref[...]
        y = d * w_[3:4]
        for j in range(3):
            y = y + _shift_up(d, halo, 3 - j) * w_[j:j + 1]
        o_ref[...] = y.astype(o_ref.dtype)

    return pl.pallas_call(
        body, grid=(ncb, nt),
        in_specs=[pl.BlockSpec((T, 512), lambda c, i: (i, c)),
                  pl.BlockSpec((8, 512), lambda c, i: (jnp.minimum((i + 1) * (T // 8), S_ // 8 - 1), c)),
                  pl.BlockSpec((4, 512), lambda c, i: (0, c))],
        out_specs=pl.BlockSpec((T, 512), lambda c, i: (i, c)),
        out_shape=jax.ShapeDtypeStruct((S_, C), out_dtype), compiler_params=_cp(("parallel", "parallel")), name=name,
    )(dpre, dpre, w)


def ffn_mid_fwd(h, w1, w3, *, name):
    S_, D = h.shape
    F = w1.shape[1]
    tm, tn = _pick(S_, (2048, 1024, 512, 256)), 256

    def body(h_ref, w1_ref, w3_ref, a_ref, u_ref, v_ref):
        hb = h_ref[...]
        u = _dot(hb, w1_ref[...])
        v = _dot(hb, w3_ref[...])
        a_ref[...] = (u * jax.nn.sigmoid(u) * v).astype(a_ref.dtype)
        u_ref[...] = u.astype(u_ref.dtype)
        v_ref[...] = v.astype(v_ref.dtype)

    o = pl.BlockSpec((tm, tn), lambda i, j: (i, j))
    return pl.pallas_call(
        body, grid=(S_ // tm, F // tn),
        in_specs=[pl.BlockSpec((tm, D), lambda i, j: (i, 0)), pl.BlockSpec((D, tn), lambda i, j: (0, j)),
                  pl.BlockSpec((D, tn), lambda i, j: (0, j))],
        out_specs=[o, o, o], out_shape=[jax.ShapeDtypeStruct((S_, F), BF16)] * 3,
        compiler_params=_cp(("parallel", "parallel")), name=name,
    )(h, w1, w3)


def ffn_mid_bwd(u, v, dy, w2, *, name):
    S_, F = u.shape
    D = dy.shape[1]
    tm, tn = _pick(S_, (2048, 1024, 512, 256)), 256

    def body(u_ref, v_ref, dy_ref, w2_ref, du_ref, dv_ref):
        u_ = u_ref[...].astype(F32)
        v_ = v_ref[...].astype(F32)
        da = _dot(dy_ref[...], w2_ref[...], _NT)
        sg = jax.nn.sigmoid(u_)
        dv_ref[...] = (da * (u_ * sg)).astype(dv_ref.dtype)
        du_ref[...] = (da * v_ * (sg * (1.0 + u_ * (1.0 - sg)))).astype(du_ref.dtype)

    o = pl.BlockSpec((tm, tn), lambda i, j: (i, j))
    return pl.pallas_call(
        body, grid=(S_ // tm, F // tn),
        in_specs=[o, o, pl.BlockSpec((tm, D), lambda i, j: (i, 0)), pl.BlockSpec((tn, D), lambda i, j: (j, 0))],
        out_specs=[o, o], out_shape=[jax.ShapeDtypeStruct((S_, F), BF16)] * 2,
        compiler_params=_cp(("parallel", "parallel")), name=name,
    )(u, v, dy, w2)


MESH = pl.DeviceIdType.MESH
ANY = pl.BlockSpec(memory_space=pl.ANY)


def allgather8(x_shard, *, name):
    m_per, n = x_shard.shape

    def body(x_ref, out_ref, send_sems, recv_sems, local_sem):
        x, y, c = lax.axis_index("x"), lax.axis_index("y"), lax.axis_index("c")
        me, sibling = (x, y, c), (x, y, 1 - c)
        chips = [(1 - x, y), (x, 1 - y), (1 - x, 1 - y)]

        def rows(px, py, pc):
            return out_ref.at[pl.ds((4 * px + 2 * py + pc) * m_per, m_per), :]

        def copy(k, block, to, src=None):
            return pltpu.make_async_remote_copy(
                src_ref=rows(*block) if src is None else src, dst_ref=rows(*block),
                send_sem=send_sems.at[k], recv_sem=recv_sems.at[k], device_id=to, device_id_type=MESH)

        mine = pltpu.make_async_copy(x_ref, rows(*me), local_sem)
        mine.start()
        first = [copy(0, me, sibling, src=x_ref)]
        first += [copy(1 + j, me, (*chip, c), src=x_ref) for j, chip in enumerate(chips)]
        for cp in first:
            cp.start()
        passed = [copy(4 + j, (*chip, c), sibling) for j, chip in enumerate(chips)]
        for j, chip in enumerate(chips):
            copy(1 + j, (*chip, c), me).wait_recv()
            passed[j].start()
        copy(0, sibling, me).wait_recv()
        for j, chip in enumerate(chips):
            copy(4 + j, (*chip, 1 - c), me).wait_recv()
        for cp in first + passed:
            cp.wait_send()
        mine.wait()

    return pl.pallas_call(
        body, out_shape=jax.ShapeDtypeStruct((8 * m_per, n), x_shard.dtype),
        in_specs=[pl.BlockSpec(memory_space=pltpu.VMEM)], out_specs=pl.BlockSpec(memory_space=pltpu.VMEM),
        scratch_shapes=[pltpu.SemaphoreType.DMA((7,)), pltpu.SemaphoreType.DMA((7,)), pltpu.SemaphoreType.DMA],
        name=name,
    )(x_shard)


def _chip_peers():
    x, y, c = lax.axis_index("x"), lax.axis_index("y"), lax.axis_index("c")
    return x, y, c, [(1 - x, y), (x, 1 - y), (1 - x, 1 - y)]


def allgather_chips(x_shard, *, name):
    r, cdim = x_shard.shape

    def body(x_ref, out_ref, send_sems, recv_sems, local_sem):
        x, y, c, chips = _chip_peers()
        me = 2 * x + y
        mine = pltpu.make_async_copy(x_ref, out_ref.at[me], local_sem)
        mine.start()
        sends = []
        for k, (px, py) in enumerate(chips):
            cp = pltpu.make_async_remote_copy(src_ref=x_ref, dst_ref=out_ref.at[me], send_sem=send_sems.at[k],
                                              recv_sem=recv_sems.at[k], device_id=(px, py, c), device_id_type=MESH)
            cp.start()
            sends.append(cp)
        for k, (px, py) in enumerate(chips):
            pltpu.make_async_remote_copy(src_ref=x_ref, dst_ref=out_ref.at[2 * px + py], send_sem=send_sems.at[k],
                                         recv_sem=recv_sems.at[k], device_id=(px, py, c), device_id_type=MESH).wait_recv()
        for cp in sends:
            cp.wait_send()
        mine.wait()

    return pl.pallas_call(
        body, out_shape=jax.ShapeDtypeStruct((4, r, cdim), x_shard.dtype), in_specs=[ANY], out_specs=ANY,
        scratch_shapes=[pltpu.SemaphoreType.DMA((3,)), pltpu.SemaphoreType.DMA((3,)), pltpu.SemaphoreType.DMA],
        name=name,
    )(x_shard)


def allgather_chips_2level(x_shard, *, name):
    r, cdim = x_shard.shape
    half = r // 2

    def body(x_ref, out_ref, send_sems, recv_sems, local_sem):
        x, y, c, chips = _chip_peers()
        me = 2 * x + y
        mine_rows = pl.ds(c * half, half)
        other_rows = pl.ds((1 - c) * half, half)
        mine = pltpu.make_async_copy(x_ref, out_ref.at[me], local_sem)
        mine.start()

        def copy(k, slot, rows, to, src=None):
            dst = out_ref.at[slot, rows, :]
            return pltpu.make_async_remote_copy(src_ref=dst if src is None else src, dst_ref=dst, send_sem=send_sems.at[k],
                                                recv_sem=recv_sems.at[k], device_id=to, device_id_type=MESH)

        first = [copy(k, me, mine_rows, (px, py, c), src=x_ref.at[mine_rows, :]) for k, (px, py) in enumerate(chips)]
        for cp in first:
            cp.start()
        passed = [copy(3 + k, 2 * px + py, mine_rows, (x, y, 1 - c)) for k, (px, py) in enumerate(chips)]
        for k, (px, py) in enumerate(chips):
            copy(k, 2 * px + py, mine_rows, (px, py, c)).wait_recv()
            passed[k].start()
        for k, (px, py) in enumerate(chips):
            copy(3 + k, 2 * px + py, other_rows, (x, y, 1 - c)).wait_recv()
        for cp in first + passed:
            cp.wait_send()
        mine.wait()

    return pl.pallas_call(
        body, out_shape=jax.ShapeDtypeStruct((4, r, cdim), x_shard.dtype), in_specs=[ANY], out_specs=ANY,
        scratch_shapes=[pltpu.SemaphoreType.DMA((6,)), pltpu.SemaphoreType.DMA((6,)), pltpu.SemaphoreType.DMA],
        name=name,
    )(x_shard)


def exchange_chips(g, *, name):
    _, r, cdim = g.shape

    def body(g_ref, out_ref, send_sems, recv_sems, local_sem):
        x, y, c, chips = _chip_peers()
        me = 2 * x + y
        mine = pltpu.make_async_copy(g_ref.at[me], out_ref.at[me], local_sem)
        mine.start()
        sends = []
        for k, (px, py) in enumerate(chips):
            cp = pltpu.make_async_remote_copy(src_ref=g_ref.at[2 * px + py], dst_ref=out_ref.at[me], send_sem=send_sems.at[k],
                                              recv_sem=recv_sems.at[k], device_id=(px, py, c), device_id_type=MESH)
            cp.start()
            sends.append(cp)
        for k, (px, py) in enumerate(chips):
            pltpu.make_async_remote_copy(src_ref=g_ref.at[me], dst_ref=out_ref.at[2 * px + py], send_sem=send_sems.at[k],
                                         recv_sem=recv_sems.at[k], device_id=(px, py, c), device_id_type=MESH).wait_recv()
        for cp in sends:
            cp.wait_send()
        mine.wait()

    return pl.pallas_call(
        body, out_shape=jax.ShapeDtypeStruct(g.shape, g.dtype), in_specs=[ANY], out_specs=ANY,
        scratch_shapes=[pltpu.SemaphoreType.DMA((3,)), pltpu.SemaphoreType.DMA((3,)), pltpu.SemaphoreType.DMA],
        name=name,
    )(g)


def swap_sibling(p, *, name):
    def body(p_ref, out_ref, send_sem, recv_sem):
        x, y, c = lax.axis_index("x"), lax.axis_index("y"), lax.axis_index("c")
        cp = pltpu.make_async_remote_copy(src_ref=p_ref, dst_ref=out_ref, send_sem=send_sem, recv_sem=recv_sem,
                                          device_id=(x, y, 1 - c), device_id_type=MESH)
        cp.start()
        cp.wait()

    return pl.pallas_call(
        body, out_shape=jax.ShapeDtypeStruct(p.shape, p.dtype), in_specs=[ANY], out_specs=ANY,
        scratch_shapes=[pltpu.SemaphoreType.DMA, pltpu.SemaphoreType.DMA], name=name,
    )(p)


def swap_other_half(g, *, name):
    n, r, cdim = g.shape
    half = r // 2

    def body(g_ref, out_ref, send_sem, recv_sem):
        x, y, c = lax.axis_index("x"), lax.axis_index("y"), lax.axis_index("c")
        cp = pltpu.make_async_remote_copy(src_ref=g_ref.at[:, pl.ds((1 - c) * half, half), :], dst_ref=out_ref, send_sem=send_sem,
                                          recv_sem=recv_sem, device_id=(x, y, 1 - c), device_id_type=MESH)
        cp.start()
        cp.wait()

    return pl.pallas_call(
        body, out_shape=jax.ShapeDtypeStruct((n, half, cdim), g.dtype), in_specs=[ANY], out_specs=ANY,
        scratch_shapes=[pltpu.SemaphoreType.DMA, pltpu.SemaphoreType.DMA], name=name,
    )(g)


def add_pairs(a, b, *, name, out_dtype):
    n, rows, cdim = a.shape
    t = _pick(rows, (256, 128, 64, 32, 16))

    def body(a_ref, b_ref, o_ref):
        o_ref[...] = (a_ref[...].astype(F32) + b_ref[...].astype(F32)).astype(o_ref.dtype)

    spec = pl.BlockSpec((n, t, cdim), lambda i: (0, i, 0))
    return pl.pallas_call(
        body, grid=(rows // t,), in_specs=[spec, spec], out_specs=spec, out_shape=jax.ShapeDtypeStruct(a.shape, out_dtype),
        compiler_params=_cp(("parallel",)), name=name,
    )(a, b)


def sum_slots(r, *, name):
    n, rows, cdim = r.shape
    t = _pick(rows, (256, 128, 64, 32, 16, 8))

    def body(r_ref, o_ref):
        acc = r_ref[0].astype(F32)
        for s in range(1, n):
            acc = acc + r_ref[s].astype(F32)
        o_ref[...] = acc

    return pl.pallas_call(
        body, grid=(rows // t,), in_specs=[pl.BlockSpec((n, t, cdim), lambda i: (0, i, 0))],
        out_specs=pl.BlockSpec((t, cdim), lambda i: (i, 0)), out_shape=jax.ShapeDtypeStruct((rows, cdim), F32),
        compiler_params=_cp(("parallel",)), name=name,
    )(r)


def _rms(x, g):
    return x * lax.rsqrt(jnp.mean(x * x, axis=-1, keepdims=True) + NORM_EPS) * g


def _adaln(x, g, shift, scale):
    return _rms(x, g) * (1.0 + scale) + shift


def _silu(x):
    return x * jax.nn.sigmoid(x)


def _gdn_gate(o, z, g):
    return jnp.concatenate([_rms(o[:, 128 * h:128 * (h + 1)], g) * _silu(z[:, 128 * h:128 * (h + 1)]) for h in range(4)], axis=1)


def _ssd_gate(y, z0, z1, z2, z3, g):
    outs = []
    for k, z in enumerate((z0, z1, z2, z3)):
        t = y[:, 512 * k:512 * (k + 1)] * _silu(z)
        outs.append(t * lax.rsqrt(jnp.mean(t * t, axis=-1, keepdims=True) + NORM_EPS))
    return jnp.concatenate(outs, axis=1) * g


def _rope(x, cos, sin, rot):
    return x * cos + _dot(x, rot, hi=True) * sin


def _rope_q(q, cos, sin, rot):
    parts = []
    for h in range(4):
        parts += [q[:, 256 * h:256 * h + 128], _rope(q[:, 256 * h + 128:256 * (h + 1)], cos, sin, rot)]
    return jnp.concatenate(parts, axis=1) * ATT_SCALE


def _rope_t(d, cos, sin, rot):
    return d * cos + _dot(d * sin, rot, _NT, hi=True)


def _vjp_rows(fn, n_rows, n_pars, out_dtypes, rows, cts, pars, *, name, tile=1024, extra=None):
    nct = len(cts)

    def bwd(*a):
        r, c, e, p = a[:n_rows], a[n_rows:n_rows + nct], a[n_rows + nct:len(a) - n_pars], a[len(a) - n_pars:]
        out, vjp = jax.vjp(fn, *[t.astype(F32) for t in r], *p)
        ct = tuple(t.astype(F32) for t in c)
        grads = vjp(ct[0] if not isinstance(out, tuple) else ct)
        drows = list(grads[:n_rows])
        if e:
            drows[0] = drows[0] + e[0]
        return (*drows, *grads[n_rows:])

    return rowmap(bwd, list(rows) + list(cts) + ([extra] if extra is not None else []), list(pars), out_dtypes,
                  name=name, tile=tile, n_reduce=n_pars)


def _gate_grads(dw_raw, w, gate, scale, *, name):
    dw, dg = rowmap(lambda r, w_, gt: ((scale * gt) * r, jnp.sum((scale * w_.astype(F32)) * r, axis=0, keepdims=True)),
                    [dw_raw, w], [gate], (F32, F32), name=name, tile=256, n_reduce=1)
    return dw, dg[0]


ADAM_LR, ADAM_B1, ADAM_B2, ADAM_EPS, ADAM_WD, ADAM_STEP = 0.001, 0.9, 0.999, 1e-08, 0.01, 10


def _adam_math(w, g, m, v):
    m = ADAM_B1 * m + (1.0 - ADAM_B1) * g
    v = ADAM_B2 * v + (1.0 - ADAM_B2) * (g * g)
    m_hat = m / (1.0 - ADAM_B1 ** ADAM_STEP)
    v_hat = v / (1.0 - ADAM_B2 ** ADAM_STEP)
    delta = -ADAM_LR * (m_hat / (jnp.sqrt(v_hat) + ADAM_EPS) + ADAM_WD * w)
    return delta, m, v


def adamw(w, gs, m, v, *, name):
    shape = w.shape
    last = shape[-1]
    to2 = lambda a: a.reshape(-1, last)
    rows = w.size // last
    tile = _pick(rows, (256, 128, 64, 32, 16, 8))
    ng = len(gs)

    def fn(w_, *rest):
        g = rest[0]
        for t in rest[1:ng]:
            g = g + t
        m_, v_ = rest[ng], rest[ng + 1]
        return (g, *_adam_math(w_, g, m_, v_))

    outs = rowmap(fn, [to2(w)] + [to2(g) for g in gs] + [to2(m), to2(v)], [], (F32,) * 4, name=name, tile=tile)
    return tuple(o.reshape(shape) for o in outs)


PACK_W = 1024
BIG = (
    ("ffn_w1", (4, 2, 1024, 704), 3), ("ffn_w3", (4, 2, 1024, 704), 3), ("ffn_w2", (4, 2, 704, 1024), 2),
    ("ev_w_in", (2, 1024, 690), 2), ("mla_w_uq", (2, 96, 4, 192), 1), ("mla_w_ukv", (2, 64, 4, 256), 1),
    ("ev_w_out", (2, 256, 1024), 1), ("ssd_w_in", (2, 1024, 1288), 2), ("ssd_w_out", (2, 512, 1024), 1))


def _seg_rows(shape):
    n = math.prod(shape)
    return -(-n // (16 * PACK_W)) * 16


PACK_ROWS = -(-sum(_seg_rows(sh) for _, sh, _ in BIG) // 512) * 512


def _pack(shards, dtype):
    parts = []
    for (_, shape, _), a in zip(BIG, shards):
        flat = a.reshape(-1).astype(dtype)
        pad = _seg_rows(shape) * PACK_W - flat.shape[0]
        parts.append(jnp.pad(flat, (0, pad)) if pad else flat)
    tail = PACK_ROWS - sum(_seg_rows(sh) for _, sh, _ in BIG)
    if tail:
        parts.append(jnp.zeros((tail * PACK_W,), dtype))
    return jnp.concatenate(parts).reshape(-1, PACK_W)


def _pack_by_owner(fulls):
    cols = []
    for (_, shape, ax), f in zip(BIG, fulls):
        blk = jnp.stack([lax.slice_in_dim(f, s * shape[ax], (s + 1) * shape[ax], axis=ax).reshape(-1).astype(BF16) for s in range(4)])
        pad = _seg_rows(shape) * PACK_W - blk.shape[1]
        cols.append((jnp.pad(blk, ((0, 0), (0, pad))) if pad else blk).reshape(4, -1, PACK_W))
    tail = PACK_ROWS - sum(_seg_rows(sh) for _, sh, _ in BIG)
    if tail:
        cols.append(jnp.zeros((4, tail, PACK_W), BF16))
    return jnp.concatenate(cols, axis=1)


def _unpack(buf):
    out, r0 = [], 0
    for _, shape, _ in BIG:
        n = math.prod(shape)
        out.append(buf[r0:r0 + _seg_rows(shape)].reshape(-1)[:n].reshape(shape))
        r0 += _seg_rows(shape)
    return out


SMALL_SHARDED = (
    ("norm_g", (4, 3, 256), 2), ("gdn_conv_w", (2, 4, 384), 2), ("ssd_conv_w", (2, 4, 768), 2),
    ("ssd_conv_b", (2, 768), 1), ("ssd_norm_g", (2, 512), 1))


def _flat_pack(arrs, width, row_mult):
    flat = jnp.concatenate([a.reshape(-1).astype(F32) for a in arrs])
    n = flat.shape[0]
    tot = -(-n // (width * row_mult)) * width * row_mult
    return jnp.pad(flat, (0, tot - n)).reshape(-1, width)


def _flat_unpack(buf, shapes):
    flat = buf.reshape(-1)
    out, o = [], 0
    for s in shapes:
        n = math.prod(s)
        out.append(flat[o:o + n].reshape(s))
        o += n
    return out


def _rep(v, n):
    return jnp.repeat(v, n, axis=-1)


def kernel(x, c, positions, ada_w, ada_b, norm_g, ffn_w1, ffn_w3, ffn_w2, ev_w_in, gdn_conv_w, gdn_A_log, gdn_dt_bias, gdn_norm_g, mla_q_norm_g, mla_w_uq, mla_kv_norm_g, mla_w_ukv, ev_w_out, ssd_w_in, ssd_conv_w, ssd_conv_b, ssd_A_log, ssd_dt_bias, ssd_D, ssd_norm_g, ssd_w_out, final_g, loss_target, m_ada_w, m_ada_b, m_norm_g, m_ffn_w1, m_ffn_w3, m_ffn_w2, m_ev_w_in, m_gdn_conv_w, m_gdn_A_log, m_gdn_dt_bias, m_gdn_norm_g, m_mla_q_norm_g, m_mla_w_uq, m_mla_kv_norm_g, m_mla_w_ukv, m_ev_w_out, m_ssd_w_in, m_ssd_conv_w, m_ssd_conv_b, m_ssd_A_log, m_ssd_dt_bias, m_ssd_D, m_ssd_norm_g, m_ssd_w_out, m_final_g, v_ada_w, v_ada_b, v_norm_g, v_ffn_w1, v_ffn_w3, v_ffn_w2, v_ev_w_in, v_gdn_conv_w, v_gdn_A_log, v_gdn_dt_bias, v_gdn_norm_g, v_mla_q_norm_g, v_mla_w_uq, v_mla_kv_norm_g, v_mla_w_ukv, v_ev_w_out, v_ssd_w_in, v_ssd_conv_w, v_ssd_conv_b, v_ssd_A_log, v_ssd_dt_bias, v_ssd_D, v_ssd_norm_g, v_ssd_w_out, v_final_g):
    P = dict(ada_w=ada_w, ada_b=ada_b, norm_g=norm_g, ffn_w1=ffn_w1, ffn_w3=ffn_w3, ffn_w2=ffn_w2, ev_w_in=ev_w_in, gdn_conv_w=gdn_conv_w, gdn_A_log=gdn_A_log, gdn_dt_bias=gdn_dt_bias, gdn_norm_g=gdn_norm_g, mla_q_norm_g=mla_q_norm_g, mla_w_uq=mla_w_uq, mla_kv_norm_g=mla_kv_norm_g, mla_w_ukv=mla_w_ukv, ev_w_out=ev_w_out, ssd_w_in=ssd_w_in, ssd_conv_w=ssd_conv_w, ssd_conv_b=ssd_conv_b, ssd_A_log=ssd_A_log, ssd_dt_bias=ssd_dt_bias, ssd_D=ssd_D, ssd_norm_g=ssd_norm_g, ssd_w_out=ssd_w_out, final_g=final_g)
    M1 = dict(ada_w=m_ada_w, ada_b=m_ada_b, norm_g=m_norm_g, ffn_w1=m_ffn_w1, ffn_w3=m_ffn_w3, ffn_w2=m_ffn_w2, ev_w_in=m_ev_w_in, gdn_conv_w=m_gdn_conv_w, gdn_A_log=m_gdn_A_log, gdn_dt_bias=m_gdn_dt_bias, gdn_norm_g=m_gdn_norm_g, mla_q_norm_g=m_mla_q_norm_g, mla_w_uq=m_mla_w_uq, mla_kv_norm_g=m_mla_kv_norm_g, mla_w_ukv=m_mla_w_ukv, ev_w_out=m_ev_w_out, ssd_w_in=m_ssd_w_in, ssd_conv_w=m_ssd_conv_w, ssd_conv_b=m_ssd_conv_b, ssd_A_log=m_ssd_A_log, ssd_dt_bias=m_ssd_dt_bias, ssd_D=m_ssd_D, ssd_norm_g=m_ssd_norm_g, ssd_w_out=m_ssd_w_out, final_g=m_final_g)
    M2 = dict(ada_w=v_ada_w, ada_b=v_ada_b, norm_g=v_norm_g, ffn_w1=v_ffn_w1, ffn_w3=v_ffn_w3, ffn_w2=v_ffn_w2, ev_w_in=v_ev_w_in, gdn_conv_w=v_gdn_conv_w, gdn_A_log=v_gdn_A_log, gdn_dt_bias=v_gdn_dt_bias, gdn_norm_g=v_gdn_norm_g, mla_q_norm_g=v_mla_q_norm_g, mla_w_uq=v_mla_w_uq, mla_kv_norm_g=v_mla_kv_norm_g, mla_w_ukv=v_mla_w_ukv, ev_w_out=v_ev_w_out, ssd_w_in=v_ssd_w_in, ssd_conv_w=v_ssd_conv_w, ssd_conv_b=v_ssd_conv_b, ssd_A_log=v_ssd_A_log, ssd_dt_bias=v_ssd_dt_bias, ssd_D=v_ssd_D, ssd_norm_g=v_ssd_norm_g, ssd_w_out=v_ssd_w_out, final_g=v_final_g)
    names = list(P)
    xi, yi, ci = lax.axis_index("x"), lax.axis_index("y"), lax.axis_index("c")
    chip = 2 * xi + yi
    bidx = 4 * xi + 2 * yi + ci
    xa = x[0]
    S_, D = xa.shape
    tgt = loss_target[0]
    depth = ffn_w1.shape[0]

    wg = allgather_chips_2level(_pack([P[n] for n, _, _ in BIG], BF16), name="gather_weights")
    per_chip = [_unpack(wg[s]) for s in range(4)]
    W = {n: jnp.concatenate([per_chip[s][k] for s in range(4)], axis=ax) for k, (n, _, ax) in enumerate(BIG)}
    sg = allgather_chips(_flat_pack([P[n] for n, _, _ in SMALL_SHARDED], 1024, 16), name="gather_small")
    per_chip_s = [_flat_unpack(sg[s], [sh for _, sh, _ in SMALL_SHARDED]) for s in range(4)]
    Wf = {n: jnp.concatenate([per_chip_s[s][k] for s in range(4)], axis=ax) for k, (n, _, ax) in enumerate(SMALL_SHARDED)}

    c_all = allgather8(jnp.pad(c, ((0, 7), (0, 0))), name="gather_c").reshape(8, 8, D)[:, 0]
    c_act, = rowmap(lambda t: (_silu(t),), [jnp.pad(c_all, ((0, 8), (0, 0)))], [], (F32,), name="c_act", tile=16)
    ncol = ada_w.shape[2]
    ada_b_loc = lax.dynamic_slice(ada_b, (0, chip * ncol), (depth, ncol))
    mod_loc = [mm((c_act, ada_w[l]), name=f"mod_{l}", epi=lambda acc, b: (acc + b,), epi_pars=(ada_b_loc[l][None],),
                  epi_out_dtypes=(F32,), tm=16, tn=256)[0][:8] for l in range(depth)]
    mod_g = allgather8(jnp.stack(mod_loc).reshape(-1, 1024), name="gather_mod").reshape(8, depth, 8, ncol)
    mod_b = lax.dynamic_index_in_dim(mod_g[0::2], bidx, axis=2, keepdims=False)
    mod = jnp.transpose(mod_b, (1, 0, 2)).reshape(depth, 3, 3, D)

    def ev_ext(w):
        z = lambda n: jnp.zeros((w.shape[0], n), w.dtype)
        return jnp.concatenate([w[:, :2048], _rep(w[:, 2048:2052], 128), _rep(w[:, 2052:2056], 128), w[:, 2056:2440], z(128),
                                w[:, 2440:2696], w[:, 2696:2760], z(192)], axis=1)

    def ev_ext_t(dw):
        return jnp.concatenate([dw[:, :2048], dw[:, 2048:2560].reshape(-1, 4, 128).sum(-1), dw[:, 2560:3072].reshape(-1, 4, 128).sum(-1),
                                dw[:, 3072:3456], dw[:, 3584:3840], dw[:, 3840:3904]], axis=1)

    def od_ext(w):
        return jnp.concatenate([w[:, 2048:5120], w[:, :2048], _rep(w[:, 5120:5152], 64)], axis=1)

    def od_ext_t(dw):
        return jnp.concatenate([dw[:, 3072:5120], dw[:, :3072], dw[:, 5120:].reshape(-1, 32, 64).sum(-1)], axis=1)

    def wq_ext(w):
        return jnp.pad(w, ((0, 0), (0, 0), (0, 64))).reshape(384, 1024)

    def wq_ext_t(dw):
        return dw.reshape(384, 4, 256)[:, :, :192]

    def wkv_ext(w):
        return jnp.concatenate([w[:, :, :128].reshape(256, 512), w[:, :, 128:].reshape(256, 512)], axis=1)

    def wkv_ext_t(dw):
        return jnp.concatenate([dw[:, :512].reshape(256, 4, 128), dw[:, 512:].reshape(256, 4, 128)], axis=2)

    half = 32
    inv_freq = 10000.0 ** (-jnp.arange(half, dtype=F32) / half)
    ang = positions[0].astype(F32)[:, None] * inv_freq
    zpad = jnp.zeros((S_, 64), F32)
    cos_t = jnp.concatenate([jnp.cos(ang), jnp.cos(ang), zpad], axis=1)
    sin_t = jnp.concatenate([jnp.sin(ang), jnp.sin(ang), zpad], axis=1)
    ii = jnp.arange(128)
    rot = (jnp.where((ii[:, None] < 32) & (ii[None, :] == ii[:, None] + 32), 1.0, 0.0)
           - jnp.where((ii[:, None] >= 32) & (ii[:, None] < 64) & (ii[None, :] == ii[:, None] - 32), 1.0, 0.0)).astype(F32)

    grads = {}
    dmod = [[[None] * 3 for _ in range(3)] for _ in range(depth)]
    dnorm_g = [[None] * 3 for _ in range(depth)]

    def acc(name, idx, val):
        grads.setdefault(name, {})[idx] = val

    def ffn_sub(xin, l, k, j):
        g, (shift, scale, gate) = Wf["norm_g"][l, k][None], [mod[l, k, t][None] for t in range(3)]
        w1, w3, w2 = W["ffn_w1"][l, j], W["ffn_w3"][l, j], W["ffn_w2"][l, j]
        tag = f"l{l}f{j}"
        h, = rowmap(lambda *a: (_adaln(*a),), [xin], [g, shift, scale], (BF16,), name=f"adaln_{tag}", tile=1024)
        a, u16, v16 = ffn_mid_fwd(h, w1, w3, name=f"ffn_mid_{tag}")
        xn, = mm((a, w2), name=f"ffn_out_{tag}", epi=lambda acc_, xr, gt: (xr + 0.5 * gt * acc_,), epi_rows=(xin,),
                 epi_pars=(gate,), epi_out_dtypes=(F32,))

        def bwd(dxn):
            du, dv = ffn_mid_bwd(u16, v16, dxn, ((0.5 * gate) * w2).astype(BF16), name=f"ffn_midb_{tag}")
            dh = mm([(du, w1), (dv, w3)], tb=True, name=f"ffn_dh_{tag}")
            acc("ffn_w1", (l, j), mm((h, du), ta=True, name=f"ffn_dw1_{tag}"))
            acc("ffn_w3", (l, j), mm((h, dv), ta=True, name=f"ffn_dw3_{tag}"))
            dw2, dgate = _gate_grads(mm((a, dxn), ta=True, name=f"ffn_dw2_{tag}"), w2, gate, 0.5, name=f"ffn_dgate_{tag}")
            acc("ffn_w2", (l, j), dw2)
            dx, dg, dsh, dsc = _vjp_rows(_adaln, 1, 3, (F32,), [xin], [dh], [g, shift, scale], name=f"adalnb_{tag}", extra=dxn, tile=1024)
            dnorm_g[l][k] = dg[0]
            dmod[l][k] = [dsh[0], dsc[0], dgate]
            return dx

        return xn, bwd

    def mixer_tail(xin, l, tag, dh, dxn, g, shift, scale, dgate):
        dx, dg, dsh, dsc = _vjp_rows(_adaln, 1, 3, (F32,), [xin], [dh], [g, shift, scale], name=f"adalnb_{tag}", extra=dxn, tile=1024)
        dnorm_g[l][1] = dg[0]
        dmod[l][1] = [dsh[0], dsc[0], dgate]
        return dx

    def even_sub(xin, l):
        e = l // 2
        tag = f"l{l}m"
        g, (shift, scale, gate) = Wf["norm_g"][l, 1][None], [mod[l, 1, t][None] for t in range(3)]
        wext, wq, wkv, wout = ev_ext(W["ev_w_in"][e]), wq_ext(W["mla_w_uq"][e]), wkv_ext(W["mla_w_ukv"][e]), W["ev_w_out"][e]
        conv_w, zb = Wf["gdn_conv_w"][e], jnp.zeros((1, 1536), F32)
        alog_e, dtb_e = _rep(gdn_A_log[e], 128)[None], _rep(gdn_dt_bias[e], 128)[None]
        gg, qg, kvg = gdn_norm_g[e][None], mla_q_norm_g[e][None], mla_kv_norm_g[e][None]
        h, = rowmap(lambda *a: (_adaln(*a),), [xin], [g, shift, scale], (BF16,), name=f"adaln_{tag}", tile=1024)
        proj = mm((h, wext), name=f"ev_in_{tag}")
        qkvc = conv_fwd(proj, 0, 3, conv_w, zb, name=f"gdn_conv_{tag}")
        o_g, hist = gdn_fwd(qkvc, proj, 4, 5, alog_e, dtb_e, name=f"gdn_{tag}")
        o_a, = rowmap(lambda o, z, g_: (_gdn_gate(o, z, g_),), [o_g, (proj, 512, 3)], [gg], (BF16,), name=f"gdn_gate_{tag}")
        cqn, = rowmap(lambda t, g_: (_rms(t, g_),), [(proj, 384, 8)], [qg], (BF16,), name=f"q_norm_{tag}")
        ckvn, = rowmap(lambda t, g_: (_rms(t, g_),), [(proj, 256, 14)], [kvg], (BF16,), name=f"kv_norm_{tag}")
        q0 = mm((cqn, wq), name=f"q_up_{tag}")
        kv = mm((ckvn, wkv), name=f"kv_up_{tag}", out_dtype=BF16)
        q, = rowmap(lambda t, cs, sn, r: (_rope_q(t, cs, sn, r),), [q0, cos_t, sin_t], [rot], (BF16,), name=f"rope_q_{tag}")
        kp, = rowmap(lambda t, cs, sn, r: (_rope(t, cs, sn, r),), [(proj, 128, 30), cos_t, sin_t], [rot], (BF16,), name=f"rope_k_{tag}")
        o_b, lse = att_fwd(q, kv, kp, name=f"att_{tag}")
        xn, = mm([(o_a, wout[:512]), (o_b, wout[512:])], name=f"ev_out_{tag}", epi=lambda acc_, xr, gt: (xr + gt * acc_,),
                 epi_rows=(xin,), epi_pars=(gate,), epi_out_dtypes=(F32,))

        def bwd(dxn):
            wout_g = (gate * wout).astype(BF16)
            do_a = mm((dxn, wout_g[:512]), tb=True, name=f"ev_doa_{tag}")
            do_b = mm((dxn, wout_g[512:]), tb=True, name=f"ev_dob_{tag}", out_dtype=BF16)
            dw_raw = jnp.concatenate([mm((o_a, dxn), ta=True, name=f"ev_dwoa_{tag}"), mm((o_b, dxn), ta=True, name=f"ev_dwob_{tag}")], axis=0)
            dwo, dgate = _gate_grads(dw_raw, wout, gate, 1.0, name=f"ev_dgate_{tag}")
            acc("ev_w_out", e, dwo)
            dsum, = rowmap(lambda d, o_: (jnp.concatenate([jnp.broadcast_to(jnp.sum(d[:, 128 * hh:128 * (hh + 1)] * o_[:, 128 * hh:128 * (hh + 1)],
                                                                                        axis=-1, keepdims=True), (d.shape[0], 128))
                                                  for hh in range(4)], axis=1),), [do_b, o_b], [], (F32,), name=f"att_dsum_{tag}")
            dq4, dk2, dv = att_bwd(q, kv, kp, lse, dsum, do_b, name=f"att_bwd_{tag}")

            def rope_qb(d0, d1, d2, d3, cs, sn, r):
                parts = []
                for d in (d0, d1, d2, d3):
                    parts += [d[:, :128], _rope_t(d[:, 128:], cs, sn, r)]
                return (jnp.concatenate(parts, axis=1) * ATT_SCALE,)

            dq0, = rowmap(rope_qb, [dq4[0], dq4[1], dq4[2], dq4[3], cos_t, sin_t], [rot], (BF16,), name=f"rope_qb_{tag}")

            def rope_kb(d, cs, sn, r):
                dkp = d[:, 128:256] + d[:, 384:512] + d[:, 640:768] + d[:, 896:1024]
                return jnp.concatenate([d[:, 256 * hh:256 * hh + 128] for hh in range(4)], axis=1), _rope_t(dkp, cs, sn, r)

            dkn, dkr = rowmap(rope_kb, [dk2, cos_t, sin_t], [rot], (BF16, BF16), name=f"rope_kb_{tag}")
            dcqn = mm((dq0, wq), tb=True, name=f"q_upb_{tag}")
            acc("mla_w_uq", e, wq_ext_t(mm((cqn, dq0), ta=True, name=f"q_dw_{tag}")))
            dckvn = mm([(dkn, wkv[:, :512]), (dv, wkv[:, 512:])], tb=True, name=f"kv_upb_{tag}")
            acc("mla_w_ukv", e, wkv_ext_t(jnp.concatenate([mm((ckvn, dkn), ta=True, name=f"kv_dwk_{tag}"), mm((ckvn, dv), ta=True, name=f"kv_dwv_{tag}")], axis=1)))
            dcq, dqg = _vjp_rows(_rms, 1, 1, (BF16,), [(proj, 384, 8)], [dcqn], [qg], name=f"q_normb_{tag}")
            dckv, dkvg = _vjp_rows(_rms, 1, 1, (BF16,), [(proj, 256, 14)], [dckvn], [kvg], name=f"kv_normb_{tag}")
            acc("mla_q_norm_g", e, dqg[0])
            acc("mla_kv_norm_g", e, dkvg[0])
            do_g, dz, dgg = _vjp_rows(_gdn_gate, 2, 1, (F32, BF16), [o_g, (proj, 512, 3)], [do_a], [gg], name=f"gdn_gateb_{tag}")
            acc("gdn_norm_g", e, dgg[0])
            dqkvc, dbe, dae, dal, ddt = gdn_bwd(qkvc, proj, 4, 5, alog_e, dtb_e, hist, do_g, name=f"gdnb_{tag}")
            acc("gdn_A_log", e, dal.reshape(4, 128).sum(-1))
            acc("gdn_dt_bias", e, ddt.reshape(4, 128).sum(-1))
            dpre, dcw, _ = conv_bwd_pre(proj, 0, 3, conv_w, zb, dqkvc, name=f"gdn_convb_{tag}")
            acc("gdn_conv_w", e, dcw)
            dqkv = conv_bwd_x(dpre, conv_w, name=f"gdn_convx_{tag}", out_dtype=BF16)
            zc = lambda n: jnp.zeros((S_, n), BF16)
            dproj = jnp.concatenate([dqkv, dz, dbe, dae, dcq, zc(128), dckv, dkr, zc(128)], axis=1)
            dh = mm((dproj, wext), tb=True, name=f"ev_inb_{tag}")
            acc("ev_w_in", e, ev_ext_t(mm((h, dproj), ta=True, name=f"ev_dwin_{tag}")))
            return mixer_tail(xin, l, tag, dh, dxn, g, shift, scale, dgate)

        return xn, bwd

    def odd_sub(xin, l):
        o = l // 2
        tag = f"l{l}m"
        g, (shift, scale, gate) = Wf["norm_g"][l, 1][None], [mod[l, 1, t][None] for t in range(3)]
        wext, wout = od_ext(W["ssd_w_in"][o]), W["ssd_w_out"][o]
        conv_w, conv_b, ng = Wf["ssd_conv_w"][o], Wf["ssd_conv_b"][o][None], Wf["ssd_norm_g"][o][None]
        ex = lambda v: _rep(v, 64)[None]
        na_e, dtb_e, dsk_e = ex(-jnp.exp(ssd_A_log[o])), ex(ssd_dt_bias[o]), ex(ssd_D[o])
        h, = rowmap(lambda *a: (_adaln(*a),), [xin], [g, shift, scale], (BF16,), name=f"adaln_{tag}", tile=1024)
        proj = mm((h, wext), name=f"ssd_in_{tag}")
        zv = [(proj, 512, 6 + t) for t in range(4)]
        xbc = conv_fwd(proj, 0, 6, conv_w, conv_b, name=f"ssd_conv_{tag}")
        ys, hist = ssd_fwd(xbc, proj, na_e, dtb_e, dsk_e, name=f"ssd_{tag}")
        yn, = rowmap(lambda *a: (_ssd_gate(*a),), [ys] + zv, [ng], (BF16,), name=f"ssd_gate_{tag}", tile=512)
        xn, = mm((yn, wout), name=f"ssd_out_{tag}", epi=lambda acc_, xr, gt: (xr + gt * acc_,), epi_rows=(xin,),
                 epi_pars=(gate,), epi_out_dtypes=(F32,))

        def bwd(dxn):
            dyn = mm((dxn, (gate * wout).astype(BF16)), tb=True, name=f"ssd_dyn_{tag}", out_dtype=BF16)
            dwo, dgate = _gate_grads(mm((yn, dxn), ta=True, name=f"ssd_dwout_{tag}"), wout, gate, 1.0, name=f"ssd_dgate_{tag}")
            acc("ssd_w_out", o, dwo)
            dys, dz0, dz1, dz2, dz3, dng = _vjp_rows(_ssd_gate, 5, 1, (F32, BF16, BF16, BF16, BF16), [ys] + zv, [dyn], [ng],
                                                     name=f"ssd_gateb_{tag}", tile=512)
            acc("ssd_norm_g", o, dng[0])
            dxs, ddtx, db_, dc_, dna, ddtb, ddsk = ssd_bwd(xbc, proj, na_e, dtb_e, dsk_e, hist, dys, name=f"ssdb_{tag}")
            acc("ssd_A_log", o, dna.reshape(32, 64).sum(-1) * (-jnp.exp(ssd_A_log[o])))
            acc("ssd_dt_bias", o, ddtb.reshape(32, 64).sum(-1))
            acc("ssd_D", o, ddsk.reshape(32, 64).sum(-1))
            dxp, dcws, dcbs = [], [], []
            for part, (cb0, ncb, dpart) in enumerate(((0, 4, dxs), (4, 1, db_), (5, 1, dc_))):
                cols = slice(512 * cb0, 512 * (cb0 + ncb))
                dpre, dcw, dcb = conv_bwd_pre(proj, cb0, ncb, conv_w[:, cols], conv_b[:, cols], dpart, name=f"ssd_convb{part}_{tag}")
                dxp.append(conv_bwd_x(dpre, conv_w[:, cols], name=f"ssd_convx{part}_{tag}", out_dtype=BF16))
                dcws.append(dcw)
                dcbs.append(dcb[0])
            acc("ssd_conv_w", o, jnp.concatenate(dcws, axis=1))
            acc("ssd_conv_b", o, jnp.concatenate(dcbs))
            dproj = jnp.concatenate(dxp + [dz0, dz1, dz2, dz3, ddtx], axis=1)
            dh = mm((dproj, wext), tb=True, name=f"ssd_inb_{tag}")
            acc("ssd_w_in", o, od_ext_t(mm((h, dproj), ta=True, name=f"ssd_dwin_{tag}")))
            return mixer_tail(xin, l, tag, dh, dxn, g, shift, scale, dgate)

        return xn, bwd

    tape = []
    xc = xa
    for l in range(depth):
        xc, b0 = ffn_sub(xc, l, 0, 0)
        xc, b1 = (even_sub if l % 2 == 0 else odd_sub)(xc, l)
        xc, b2 = ffn_sub(xc, l, 2, 1)
        tape += [b0, b1, b2]

    def head(xr, tg, g_):
        def f(xv, gv):
            err = _rms(xv, gv) - tg
            return 0.5 * jnp.sum(jnp.mean(err * err, axis=-1, keepdims=True), axis=0, keepdims=True)
        lo, vjp = jax.vjp(f, xr, g_)
        dxv, dgv = vjp(jnp.ones_like(lo))
        return dxv, jnp.broadcast_to(lo, (1, 128)), dgv

    dx, loss_p, dfg = rowmap(head, [xc, tgt], [final_g[None]], (F32,), name="loss_head", n_reduce=2)
    loss = lax.psum(loss_p[0, 0], ("x", "y", "c"))

    for b in reversed(tape):
        dx = b(dx)
    grad_x = dx[None]

    full = {n: jnp.stack([grads[n][k] for k in sorted(grads[n])]) for n in ("ev_w_in", "mla_w_uq", "mla_w_ukv", "ev_w_out", "ssd_w_in", "ssd_w_out")}
    for n in ("ffn_w1", "ffn_w3", "ffn_w2"):
        full[n] = jnp.stack([jnp.stack([grads[n][(l, j)] for j in range(2)]) for l in range(depth)])
    gall = _pack_by_owner([full[n] for n, _, _ in BIG])
    half = PACK_ROWS // 2
    from_sib = swap_other_half(gall, name="swap_half")
    pair = add_pairs(lax.dynamic_slice_in_dim(gall, ci * half, half, axis=1), from_sib, name="add_sibling", out_dtype=BF16)
    recv = exchange_chips(pair, name="exchange_grads")
    part = sum_slots(recv, name="sum_chips")
    sib = swap_sibling(part, name="swap_sibling")
    lo = jnp.where(ci == 0, part, sib)
    hi_ = jnp.where(ci == 0, sib, part)
    g_tot = _unpack(jnp.concatenate([lo, hi_], axis=0))

    dmod_flat = jnp.stack([jnp.stack([jnp.stack(dmod[l][k]) for k in range(3)]) for l in range(depth)]).reshape(depth, 9 * D)
    small_names = ["norm_g", "gdn_conv_w", "gdn_A_log", "gdn_dt_bias", "gdn_norm_g", "mla_q_norm_g", "mla_kv_norm_g",
                   "ssd_conv_w", "ssd_conv_b", "ssd_A_log", "ssd_dt_bias", "ssd_D", "ssd_norm_g", "final_g"]
    small_full = {n: jnp.stack([grads[n][k] for k in sorted(grads[n])]) for n in small_names if n in grads}
    small_full["norm_g"] = jnp.stack([jnp.stack(dnorm_g[l]) for l in range(depth)])
    small_full["final_g"] = dfg[0]
    small_list = [dmod_flat] + [small_full[n] for n in small_names]
    small_shapes = [a.shape for a in small_list]
    sp = _flat_pack(small_list, 128, 8)
    sgath = allgather8(sp, name="gather_small_grads").reshape(8, sp.shape[0], 128)
    ssum = sum_slots(sgath, name="sum_small")
    tot = dict(zip(["ada_b"] + small_names, _flat_unpack(ssum, small_shapes)))
    dmod_all = sgath.reshape(8, -1)[:, :depth * 9 * D].reshape(8, depth, 9 * D)
    dmod_loc = lax.dynamic_slice(dmod_all, (0, 0, chip * ncol), (8, depth, ncol))
    g_ada_w = jnp.stack([mm((c_act, jnp.pad(dmod_loc[:, l], ((0, 8), (0, 0)))), ta=True, name=f"ada_dw_{l}", tk=16, tn=256)
                         for l in range(depth)])

    def own(n, a):
        for m_, sh, ax in SMALL_SHARDED:
            if m_ == n:
                return lax.dynamic_slice_in_dim(a, chip * sh[ax], sh[ax], axis=ax)
        return a

    res = {}
    for k, (n, _, _) in enumerate(BIG):
        res[n] = adamw(P[n], [g_tot[k]], M1[n], M2[n], name=f"adamw_{n}")
    res["ada_w"] = adamw(ada_w, [g_ada_w], m_ada_w, v_ada_w, name="adamw_ada_w")
    sm = ["ada_b"] + small_names
    shapes = [P[n].shape for n in sm]
    pk = lambda d: _flat_pack([d[n] for n in sm], 128, 8)
    outs = adamw(pk(P), [pk({n: own(n, tot[n]).reshape(P[n].shape) for n in sm})], pk(M1), pk(M2), name="adamw_small")
    un = [_flat_unpack(o, shapes) for o in outs]
    for i, n in enumerate(sm):
        res[n] = tuple(un[t][i] for t in range(4))
    return (loss, grad_x, *[res[n][0] for n in names], *[res[n][1] for n in names], *[res[n][2] for n in names], *[res[n][3] for n in names])
```
